```python
import math
import jax
import jax.numpy as jnp
from jax import lax
import numpy as np

D_MODEL = 1024
BATCH = 8
SEQ = 4096
DEPTH = 4

A_HEADS = 6
A_KDIM = 128
A_VDIM = 64
A_CHUNK = 64
B_HEADS = 6
B_HDIM = 64
B_BLOCK = 128
C_GROUPS = ((128, 1), (512, 4), (2048, 16))
C_HEADS_PER_GROUP = 4
C_HEADS = C_HEADS_PER_GROUP * len(C_GROUPS)
C_HDIM = 64
C_BLOCK = 128
D_FF = 2816
N_BRANCH = 3
N_NORMS = 6
EPS = 1e-6
NEG_BIG = -1e30
TINY = 1e-30

A_QK = A_HEADS * A_KDIM
A_V = A_HEADS * A_VDIM
B_W = B_HEADS * B_HDIM
C_W = C_HEADS * C_HDIM
C_OUT = C_HEADS_PER_GROUP * C_HDIM
IN_COLS = 2 * A_QK + 2 * A_V + 3 * B_W + 3 * C_W + N_BRANCH * D_MODEL

kernel_name = 'hybrid_hgrn2_stickbreak_dilated_macaron'


def rms_norm(x, g):
    xf = x.astype(jnp.float32)
    y = xf * lax.rsqrt(jnp.mean(xf * xf, axis=-1, keepdims=True) + EPS)
    return (y * g.astype(jnp.float32)).astype(x.dtype)


def swiglu_ffn(h, w_in, w_out):
    a, b = jnp.split(h @ w_in, 2, axis=-1)
    return (jax.nn.silu(a) * b) @ w_out


def alibi_slopes(n):
    def pow2_slopes(m):
        start = 2.0 ** (-8.0 / m)
        return [start ** (i + 1) for i in range(m)]
    if math.log2(n).is_integer():
        s = pow2_slopes(n)
    else:
        c = 2 ** int(math.floor(math.log2(n)))
        s = pow2_slopes(c) + pow2_slopes(2 * c)[0::2][: n - c]
    return sorted(s, reverse=True)


def hgrn2_mixer(q, f_raw, i, g, lb, norm_g):
    f32 = jnp.float32
    bsz, seq, nh, kd = q.shape
    vd = i.shape[-1]
    lb = lb.astype(f32)
    f_raw = f_raw.astype(f32)
    qf = jax.nn.silu(q.astype(f32))
    f = lb + (1.0 - lb) * jax.nn.sigmoid(f_raw)
    log_f = jnp.log(jnp.maximum(f, TINY))
    kf = (1.0 - lb) * jax.nn.sigmoid(-f_raw)
    vf = i.astype(f32)
    n = seq // A_CHUNK

    def to_chunks(t):
        return t.reshape(bsz, n, A_CHUNK, nh, t.shape[-1]).transpose(1, 0, 3, 2, 4)

    causal = jnp.tril(jnp.ones((A_CHUNK, A_CHUNK), dtype=bool))

    def step(state, xs):
        qc, kc, vc, lfc = xs
        b = jnp.cumsum(lfc, axis=2)
        o_inter = jnp.einsum('bhtk,bhkv->bhtv', qc * jnp.exp(b), state)
        diff = b[:, :, :, None, :] - b[:, :, None, :, :]
        decay = jnp.exp(jnp.where(causal[:, :, None], diff, NEG_BIG))
        scores = jnp.einsum('bhtk,bhsk,bhtsk->bhts', qc, kc, decay)
        o_intra = jnp.einsum('bhts,bhsv->bhtv', scores, vc)
        b_end = b[:, :, -1:, :]
        new_state = (jnp.exp(b_end[:, :, 0, :])[..., None] * state
                     + jnp.einsum('bhsk,bhsv->bhkv', kc * jnp.exp(b_end - b), vc))
        return new_state, o_inter + o_intra

    state0 = jnp.zeros((bsz, nh, kd, vd), f32)
    _, o = lax.scan(step, state0, (to_chunks(qf), to_chunks(kf), to_chunks(vf), to_chunks(log_f)))
    o = o.transpose(1, 0, 3, 2, 4).reshape(bsz, seq, nh, vd)
    o = rms_norm(o, norm_g) * jax.nn.silu(g.astype(f32))
    return o.reshape(bsz, seq, nh * vd)


def stick_breaking_mixer(q, k, v):
    f32 = jnp.float32
    bsz, seq, nh, d = q.shape
    qh, kh, vh = (t.astype(f32).transpose(0, 2, 1, 3) for t in (q, k, v))
    nb = seq // B_BLOCK
    q_blocks = qh.reshape(bsz, nh, nb, B_BLOCK, d).transpose(2, 0, 1, 3, 4)
    starts = jnp.arange(nb) * B_BLOCK
    key_pos = jnp.arange(seq)
    scale = d ** -0.5

    def block(args):
        qb, t0 = args
        z = jnp.einsum('bhtd,bhsd->bhts', qb, kh) * scale
        causal = key_pos[None, :] < (t0 + jnp.arange(B_BLOCK))[:, None]
        log_1m = jnp.where(causal, jax.nn.log_sigmoid(-z), 0.0)
        suffix = lax.cumsum(log_1m, axis=3, reverse=True) - log_1m
        a = jnp.where(causal, jnp.exp(jnp.where(causal, jax.nn.log_sigmoid(z) + suffix, NEG_BIG)), 0.0)
        return jnp.einsum('bhts,bhsd->bhtd', a, vh)

    o = lax.map(block, (q_blocks, starts))
    return o.transpose(1, 0, 3, 2, 4).reshape(bsz, seq, nh * d)


def dilated_group_attention(q, k, v, window, dilation, slopes):
    f32 = jnp.float32
    bsz, seq, nh, d = q.shape
    sub_len = seq // dilation
    nb = -(-sub_len // C_BLOCK)
    lp = nb * C_BLOCK
    pad = lp - sub_len

    def strided(t):
        return t.astype(f32).reshape(bsz, sub_len, dilation, nh, d).transpose(0, 2, 3, 1, 4)

    qs = jnp.pad(strided(q), ((0, 0), (0, 0), (0, 0), (0, pad), (0, 0)))
    ks = jnp.pad(strided(k), ((0, 0), (0, 0), (0, 0), (C_BLOCK, pad), (0, 0)))
    vs = jnp.pad(strided(v), ((0, 0), (0, 0), (0, 0), (C_BLOCK, pad), (0, 0)))
    blk = (bsz, dilation, nh, nb, C_BLOCK, d)
    qb = qs.reshape(blk)
    kb = jnp.concatenate([ks[:, :, :, :lp].reshape(blk), ks[:, :, :, C_BLOCK:].reshape(blk)], axis=4)
    vb = jnp.concatenate([vs[:, :, :, :lp].reshape(blk), vs[:, :, :, C_BLOCK:].reshape(blk)], axis=4)
    s = jnp.einsum('brhnid,brhnjd->brhnij', qb, kb) * d ** -0.5
    qi = jnp.arange(C_BLOCK)[:, None]
    kj = jnp.arange(2 * C_BLOCK)[None, :]
    delta = qi + C_BLOCK - kj
    key_idx = (jnp.arange(nb) * C_BLOCK)[:, None, None] + kj[None] - C_BLOCK
    valid = (delta >= 0) & (delta <= window // dilation) & (key_idx >= 0)
    bias = -slopes[:, None, None, None] * (delta * dilation).astype(f32)
    s = jnp.where(valid, s + bias, NEG_BIG)
    m = jnp.max(s, axis=-1, keepdims=True)
    p = jnp.exp(s - m)
    den = jnp.sum(p, axis=-1, keepdims=True)
    o = jnp.einsum('brhnij,brhnjd->brhnid', p, vb) / den
    lse = (m + jnp.log(den))[..., 0]
    o = o.reshape(bsz, dilation, nh, lp, d)[:, :, :, :sub_len].transpose(0, 3, 1, 2, 4).reshape(bsz, seq, nh, d)
    lse = lse.reshape(bsz, dilation, nh, lp)[:, :, :, :sub_len].transpose(0, 3, 1, 2).reshape(bsz, seq, nh)
    return o, lse


def dilated_mixer(q, k, v):
    bsz, seq = q.shape[:2]
    slopes = jnp.asarray(alibi_slopes(C_HEADS), jnp.float32)
    outs, lses = [], []
    for gi, (w, r) in enumerate(C_GROUPS):
        sl = slice(gi * C_HEADS_PER_GROUP, (gi + 1) * C_HEADS_PER_GROUP)
        o, lse = dilated_group_attention(q[:, :, sl], k[:, :, sl], v[:, :, sl], w, r, slopes[sl])
        outs.append(o)
        lses.append(lse)
    wts = jax.nn.softmax(jnp.stack(lses, axis=2), axis=2)
    o = jnp.sum(wts[..., None] * jnp.stack(outs, axis=2), axis=2)
    return o.reshape(bsz, seq, C_OUT)


def token_mixing(h, w_in, lb, a_norm_g, w_br_a, w_br_b, w_br_c, w_out):
    bsz, seq, _ = h.shape
    widths = [A_QK, A_QK, A_V, A_V, B_W, B_W, B_W, C_W, C_W, C_W]
    idx = [int(v) for v in np.cumsum(widths)]
    aq, af, ai, ag, bq, bk, bv, cq, ck, cv, gate_logits = jnp.split(h @ w_in, idx, axis=-1)

    def heads(t, n):
        return t.reshape(bsz, seq, n, -1)

    ya = hgrn2_mixer(heads(aq, A_HEADS), heads(af, A_HEADS), heads(ai, A_HEADS), heads(ag, A_HEADS),
                     lb.reshape(A_HEADS, A_KDIM), a_norm_g)
    yb = stick_breaking_mixer(heads(bq, B_HEADS), heads(bk, B_HEADS), heads(bv, B_HEADS))
    yc = dilated_mixer(heads(cq, C_HEADS), heads(ck, C_HEADS), heads(cv, C_HEADS))
    gates = jax.nn.sigmoid(gate_logits.reshape(bsz, seq, N_BRANCH, D_MODEL).astype(jnp.float32)).astype(h.dtype)
    merged = (gates[:, :, 0] * (ya.astype(h.dtype) @ w_br_a)
              + gates[:, :, 1] * (yb.astype(h.dtype) @ w_br_b)
              + gates[:, :, 2] * (yc.astype(h.dtype) @ w_br_c))
    return merged @ w_out


def modulated_sublayer(x, fn, g_pre, g_post, m, res_w):
    shift, scale, gate = m[:, 0][:, None], m[:, 1][:, None], m[:, 2][:, None]
    h = rms_norm(x, g_pre) * (1.0 + scale) + shift
    return x + res_w * gate * rms_norm(fn(h), g_post)


def _fwd_setup_inputs(seed: int = 0) -> dict:
    key = jax.random.key(seed)
    ks = jax.random.split(key, 18)
    nrm = jax.random.normal
    f32 = jnp.float32
    return {
        'x': nrm(ks[0], (BATCH, SEQ, D_MODEL), f32),
        'c': nrm(ks[1], (BATCH, D_MODEL), f32),
        'w_ada': nrm(ks[2], (DEPTH, D_MODEL, 9 * D_MODEL), f32) * (0.5 * D_MODEL ** -0.5),
        'b_ada': nrm(ks[3], (DEPTH, 9 * D_MODEL), f32) * 0.01,
        'norm_g': 1.0 + 0.05 * nrm(ks[4], (DEPTH, N_NORMS, D_MODEL), f32),
        'ffn1_w_in': nrm(ks[5], (DEPTH, D_MODEL, 2 * D_FF), f32) * D_MODEL ** -0.5,
        'ffn1_w_out': nrm(ks[6], (DEPTH, D_FF, D_MODEL), f32) * D_FF ** -0.5,
        'w_in': nrm(ks[7], (DEPTH, D_MODEL, IN_COLS), f32) * D_MODEL ** -0.5,
        'hgrn_lb_logits': nrm(ks[8], (DEPTH, A_QK), f32),
        'hgrn_norm_g': 1.0 + 0.05 * nrm(ks[9], (DEPTH, A_VDIM), f32),
        'w_branch_a': nrm(ks[10], (DEPTH, A_V, D_MODEL), f32) * A_V ** -0.5,
        'w_branch_b': nrm(ks[11], (DEPTH, B_W, D_MODEL), f32) * B_W ** -0.5,
        'w_branch_c': nrm(ks[12], (DEPTH, C_OUT, D_MODEL), f32) * C_OUT ** -0.5,
        'w_out': nrm(ks[13], (DEPTH, D_MODEL, D_MODEL), f32) * D_MODEL ** -0.5,
        'ffn2_w_in': nrm(ks[14], (DEPTH, D_MODEL, 2 * D_FF), f32) * D_MODEL ** -0.5,
        'ffn2_w_out': nrm(ks[15], (DEPTH, D_FF, D_MODEL), f32) * D_FF ** -0.5,
    }


def _fwd_reference(x, c, w_ada, b_ada, norm_g, ffn1_w_in, ffn1_w_out, w_in, hgrn_lb_logits, hgrn_norm_g,
              w_branch_a, w_branch_b, w_branch_c, w_out, ffn2_w_in, ffn2_w_out):
    bsz = x.shape[0]
    lb_p = jax.nn.softmax(hgrn_lb_logits.astype(jnp.float32), axis=0)
    lb_all = jnp.cumsum(lb_p, axis=0) - lb_p[0:1]
    c_act = jax.nn.silu(c)
    for l in range(DEPTH):
        mod = (c_act @ w_ada[l] + b_ada[l]).reshape(bsz, 3, 3, D_MODEL)
        x = modulated_sublayer(x, lambda h: swiglu_ffn(h, ffn1_w_in[l], ffn1_w_out[l]),
                               norm_g[l, 0], norm_g[l, 1], mod[:, 0], 0.5)
        x = modulated_sublayer(x, lambda h: token_mixing(h, w_in[l], lb_all[l], hgrn_norm_g[l], w_branch_a[l],
                                                         w_branch_b[l], w_branch_c[l], w_out[l]),
                               norm_g[l, 2], norm_g[l, 3], mod[:, 1], 1.0)
        x = modulated_sublayer(x, lambda h: swiglu_ffn(h, ffn2_w_in[l], ffn2_w_out[l]),
                               norm_g[l, 4], norm_g[l, 5], mod[:, 2], 0.5)
    return x


import jax as _jax
import jax.numpy as _jnp

TWIN_FORMAT = 'train_step'
FWD_PARAMS = ['x', 'c', 'w_ada', 'b_ada', 'norm_g', 'ffn1_w_in', 'ffn1_w_out', 'w_in', 'hgrn_lb_logits', 'hgrn_norm_g', 'w_branch_a', 'w_branch_b', 'w_branch_c', 'w_out', 'ffn2_w_in', 'ffn2_w_out']
TWIN_WEIGHTS = ['w_ada', 'b_ada', 'norm_g', 'ffn1_w_in', 'ffn1_w_out', 'w_in', 'hgrn_lb_logits', 'hgrn_norm_g', 'w_branch_a', 'w_branch_b', 'w_branch_c', 'w_out', 'ffn2_w_in', 'ffn2_w_out']
TWIN_DIFF_INPUT = 'x'
TWIN_INPUTS = ['x', 'c', 'w_ada', 'b_ada', 'norm_g', 'ffn1_w_in', 'ffn1_w_out', 'w_in', 'hgrn_lb_logits', 'hgrn_norm_g', 'w_branch_a', 'w_branch_b', 'w_branch_c', 'w_out', 'ffn2_w_in', 'ffn2_w_out', 'loss_target', 'm_w_ada', 'm_b_ada', 'm_norm_g', 'm_ffn1_w_in', 'm_ffn1_w_out', 'm_w_in', 'm_hgrn_lb_logits', 'm_hgrn_norm_g', 'm_w_branch_a', 'm_w_branch_b', 'm_w_branch_c', 'm_w_out', 'm_ffn2_w_in', 'm_ffn2_w_out', 'v_w_ada', 'v_b_ada', 'v_norm_g', 'v_ffn1_w_in', 'v_ffn1_w_out', 'v_w_in', 'v_hgrn_lb_logits', 'v_hgrn_norm_g', 'v_w_branch_a', 'v_w_branch_b', 'v_w_branch_c', 'v_w_out', 'v_ffn2_w_in', 'v_ffn2_w_out']
TWIN_OUTPUTS = ['loss', 'grad_x', 'grad_w_ada', 'grad_b_ada', 'grad_norm_g', 'grad_ffn1_w_in', 'grad_ffn1_w_out', 'grad_w_in', 'grad_hgrn_lb_logits', 'grad_hgrn_norm_g', 'grad_w_branch_a', 'grad_w_branch_b', 'grad_w_branch_c', 'grad_w_out', 'grad_ffn2_w_in', 'grad_ffn2_w_out', 'delta_w_ada', 'delta_b_ada', 'delta_norm_g', 'delta_ffn1_w_in', 'delta_ffn1_w_out', 'delta_w_in', 'delta_hgrn_lb_logits', 'delta_hgrn_norm_g', 'delta_w_branch_a', 'delta_w_branch_b', 'delta_w_branch_c', 'delta_w_out', 'delta_ffn2_w_in', 'delta_ffn2_w_out', 'new_m_w_ada', 'new_m_b_ada', 'new_m_norm_g', 'new_m_ffn1_w_in', 'new_m_ffn1_w_out', 'new_m_w_in', 'new_m_hgrn_lb_logits', 'new_m_hgrn_norm_g', 'new_m_w_branch_a', 'new_m_w_branch_b', 'new_m_w_branch_c', 'new_m_w_out', 'new_m_ffn2_w_in', 'new_m_ffn2_w_out', 'new_v_w_ada', 'new_v_b_ada', 'new_v_norm_g', 'new_v_ffn1_w_in', 'new_v_ffn1_w_out', 'new_v_w_in', 'new_v_hgrn_lb_logits', 'new_v_hgrn_norm_g', 'new_v_w_branch_a', 'new_v_w_branch_b', 'new_v_w_branch_c', 'new_v_w_out', 'new_v_ffn2_w_in', 'new_v_ffn2_w_out']
TWIN_LEAF_KINDS = {'loss': 'loss', 'grad_x': 'grad_x', 'grad_w_ada': 'grad_w', 'grad_b_ada': 'grad_w', 'grad_norm_g': 'grad_w', 'grad_ffn1_w_in': 'grad_w', 'grad_ffn1_w_out': 'grad_w', 'grad_w_in': 'grad_w', 'grad_hgrn_lb_logits': 'grad_w', 'grad_hgrn_norm_g': 'grad_w', 'grad_w_branch_a': 'grad_w', 'grad_w_branch_b': 'grad_w', 'grad_w_branch_c': 'grad_w', 'grad_w_out': 'grad_w', 'grad_ffn2_w_in': 'grad_w', 'grad_ffn2_w_out': 'grad_w', 'delta_w_ada': 'delta_w', 'delta_b_ada': 'delta_w', 'delta_norm_g': 'delta_w', 'delta_ffn1_w_in': 'delta_w', 'delta_ffn1_w_out': 'delta_w', 'delta_w_in': 'delta_w', 'delta_hgrn_lb_logits': 'delta_w', 'delta_hgrn_norm_g': 'delta_w', 'delta_w_branch_a': 'delta_w', 'delta_w_branch_b': 'delta_w', 'delta_w_branch_c': 'delta_w', 'delta_w_out': 'delta_w', 'delta_ffn2_w_in': 'delta_w', 'delta_ffn2_w_out': 'delta_w', 'new_m_w_ada': 'new_m', 'new_m_b_ada': 'new_m', 'new_m_norm_g': 'new_m', 'new_m_ffn1_w_in': 'new_m', 'new_m_ffn1_w_out': 'new_m', 'new_m_w_in': 'new_m', 'new_m_hgrn_lb_logits': 'new_m', 'new_m_hgrn_norm_g': 'new_m', 'new_m_w_branch_a': 'new_m', 'new_m_w_branch_b': 'new_m', 'new_m_w_branch_c': 'new_m', 'new_m_w_out': 'new_m', 'new_m_ffn2_w_in': 'new_m', 'new_m_ffn2_w_out': 'new_m', 'new_v_w_ada': 'new_v', 'new_v_b_ada': 'new_v', 'new_v_norm_g': 'new_v', 'new_v_ffn1_w_in': 'new_v', 'new_v_ffn1_w_out': 'new_v', 'new_v_w_in': 'new_v', 'new_v_hgrn_lb_logits': 'new_v', 'new_v_hgrn_norm_g': 'new_v', 'new_v_w_branch_a': 'new_v', 'new_v_w_branch_b': 'new_v', 'new_v_w_branch_c': 'new_v', 'new_v_w_out': 'new_v', 'new_v_ffn2_w_in': 'new_v', 'new_v_ffn2_w_out': 'new_v'}


def _forward(args):
    return _fwd_reference(*[args[k] for k in FWD_PARAMS])


def _output_shape():
    out = _jax.eval_shape(lambda: _forward(_fwd_setup_inputs(0)))
    return out.shape, out.dtype

N_MICROBATCH = 1
ADAM_LR = 0.001
ADAM_B1 = 0.9
ADAM_B2 = 0.999
ADAM_EPS = 1e-08
ADAM_WD = 0.01
ADAM_STEP = 10
PER_EXAMPLE_BATCH_AXIS = {'x': 0, 'c': 0, 'loss_target': 0}
SHARED_INPUTS = []
_WEIGHT_DTYPES = {'w_ada': _jnp.float32, 'b_ada': _jnp.float32, 'norm_g': _jnp.float32, 'ffn1_w_in': _jnp.float32, 'ffn1_w_out': _jnp.float32, 'w_in': _jnp.float32, 'hgrn_lb_logits': _jnp.float32, 'hgrn_norm_g': _jnp.float32, 'w_branch_a': _jnp.float32, 'w_branch_b': _jnp.float32, 'w_branch_c': _jnp.float32, 'w_out': _jnp.float32, 'ffn2_w_in': _jnp.float32, 'ffn2_w_out': _jnp.float32}
MOMENT_SCALE = {'w_ada': 1.119738e+00, 'b_ada': 2.035108e+00, 'norm_g': 1.582097e+00, 'ffn1_w_in': 4.490506e-02, 'ffn1_w_out': 8.188062e-02, 'w_in': 1.322491e-01, 'hgrn_lb_logits': 5.535552e-03, 'hgrn_norm_g': 6.728893e-01, 'w_branch_a': 1.645473e-01, 'w_branch_b': 3.205007e-01, 'w_branch_c': 1.903293e-01, 'w_out': 4.027531e-01, 'ffn2_w_in': 4.401298e-02, 'ffn2_w_out': 8.141140e-02}


def _to_microbatches(a, axis):
    t = _jnp.moveaxis(a, axis, 0)
    t = t.reshape((N_MICROBATCH, t.shape[0] // N_MICROBATCH) + t.shape[1:])
    return _jnp.moveaxis(t, 1, axis + 1)


def setup_inputs(seed: int = 0) -> dict:
    inp = _fwd_setup_inputs(seed)
    key = _jax.random.fold_in(_jax.random.key(seed), 7919)
    shape, _ = _output_shape()
    out = dict(inp)
    out["loss_target"] = _jax.random.normal(_jax.random.fold_in(key, 0), shape, _jnp.float32)
    for i, name in enumerate(TWIN_WEIGHTS):
        w = inp[name].astype(_jnp.float32)
        if MOMENT_SCALE is None:
            s = _jnp.sqrt(_jnp.mean(_jnp.square(w)) + 1e-30)
        else:
            s = MOMENT_SCALE[name]
        km, kv = _jax.random.split(_jax.random.fold_in(key, i + 1))
        out[name] = w
        out["m_" + name] = s * _jax.random.normal(km, w.shape, _jnp.float32)
        out["v_" + name] = (s * s) * _jax.random.uniform(kv, w.shape, _jnp.float32, 0.5, 1.5)
    if N_MICROBATCH > 1:
        for name, axis in PER_EXAMPLE_BATCH_AXIS.items():
            out[name] = _to_microbatches(out[name], axis)
    return {'x': out['x'], 'c': out['c'], 'w_ada': out['w_ada'], 'b_ada': out['b_ada'], 'norm_g': out['norm_g'], 'ffn1_w_in': out['ffn1_w_in'], 'ffn1_w_out': out['ffn1_w_out'], 'w_in': out['w_in'], 'hgrn_lb_logits': out['hgrn_lb_logits'], 'hgrn_norm_g': out['hgrn_norm_g'], 'w_branch_a': out['w_branch_a'], 'w_branch_b': out['w_branch_b'], 'w_branch_c': out['w_branch_c'], 'w_out': out['w_out'], 'ffn2_w_in': out['ffn2_w_in'], 'ffn2_w_out': out['ffn2_w_out'], 'loss_target': out['loss_target'], 'm_w_ada': out['m_w_ada'], 'm_b_ada': out['m_b_ada'], 'm_norm_g': out['m_norm_g'], 'm_ffn1_w_in': out['m_ffn1_w_in'], 'm_ffn1_w_out': out['m_ffn1_w_out'], 'm_w_in': out['m_w_in'], 'm_hgrn_lb_logits': out['m_hgrn_lb_logits'], 'm_hgrn_norm_g': out['m_hgrn_norm_g'], 'm_w_branch_a': out['m_w_branch_a'], 'm_w_branch_b': out['m_w_branch_b'], 'm_w_branch_c': out['m_w_branch_c'], 'm_w_out': out['m_w_out'], 'm_ffn2_w_in': out['m_ffn2_w_in'], 'm_ffn2_w_out': out['m_ffn2_w_out'], 'v_w_ada': out['v_w_ada'], 'v_b_ada': out['v_b_ada'], 'v_norm_g': out['v_norm_g'], 'v_ffn1_w_in': out['v_ffn1_w_in'], 'v_ffn1_w_out': out['v_ffn1_w_out'], 'v_w_in': out['v_w_in'], 'v_hgrn_lb_logits': out['v_hgrn_lb_logits'], 'v_hgrn_norm_g': out['v_hgrn_norm_g'], 'v_w_branch_a': out['v_w_branch_a'], 'v_w_branch_b': out['v_w_branch_b'], 'v_w_branch_c': out['v_w_branch_c'], 'v_w_out': out['v_w_out'], 'v_ffn2_w_in': out['v_ffn2_w_in'], 'v_ffn2_w_out': out['v_ffn2_w_out']}


def _loss(weights, diff, rest, loss_target):
    with _jax.named_scope("forward"):
        args = {**rest, TWIN_DIFF_INPUT: diff, **{k: w.astype(_WEIGHT_DTYPES[k]) for k, w in weights.items()}}
        y = _forward(args)
    with _jax.named_scope("loss_head"):
        err = _jnp.square(y.astype(_jnp.float32) - loss_target)
        return 0.5 * _jnp.sum(_jnp.mean(err, axis=-1)) if err.ndim else 0.5 * err


def _adamw(w, g, m, v):
    m = ADAM_B1 * m + (1.0 - ADAM_B1) * g
    v = ADAM_B2 * v + (1.0 - ADAM_B2) * _jnp.square(g)
    m_hat = m / (1.0 - ADAM_B1 ** ADAM_STEP)
    v_hat = v / (1.0 - ADAM_B2 ** ADAM_STEP)
    delta = -ADAM_LR * (m_hat / (_jnp.sqrt(v_hat) + ADAM_EPS) + ADAM_WD * w)
    return delta, m, v


def reference(x, c, w_ada, b_ada, norm_g, ffn1_w_in, ffn1_w_out, w_in, hgrn_lb_logits, hgrn_norm_g, w_branch_a, w_branch_b, w_branch_c, w_out, ffn2_w_in, ffn2_w_out, loss_target, m_w_ada, m_b_ada, m_norm_g, m_ffn1_w_in, m_ffn1_w_out, m_w_in, m_hgrn_lb_logits, m_hgrn_norm_g, m_w_branch_a, m_w_branch_b, m_w_branch_c, m_w_out, m_ffn2_w_in, m_ffn2_w_out, v_w_ada, v_b_ada, v_norm_g, v_ffn1_w_in, v_ffn1_w_out, v_w_in, v_hgrn_lb_logits, v_hgrn_norm_g, v_w_branch_a, v_w_branch_b, v_w_branch_c, v_w_out, v_ffn2_w_in, v_ffn2_w_out):
    given = dict(x=x, c=c, w_ada=w_ada, b_ada=b_ada, norm_g=norm_g, ffn1_w_in=ffn1_w_in, ffn1_w_out=ffn1_w_out, w_in=w_in, hgrn_lb_logits=hgrn_lb_logits, hgrn_norm_g=hgrn_norm_g, w_branch_a=w_branch_a, w_branch_b=w_branch_b, w_branch_c=w_branch_c, w_out=w_out, ffn2_w_in=ffn2_w_in, ffn2_w_out=ffn2_w_out, loss_target=loss_target, m_w_ada=m_w_ada, m_b_ada=m_b_ada, m_norm_g=m_norm_g, m_ffn1_w_in=m_ffn1_w_in, m_ffn1_w_out=m_ffn1_w_out, m_w_in=m_w_in, m_hgrn_lb_logits=m_hgrn_lb_logits, m_hgrn_norm_g=m_hgrn_norm_g, m_w_branch_a=m_w_branch_a, m_w_branch_b=m_w_branch_b, m_w_branch_c=m_w_branch_c, m_w_out=m_w_out, m_ffn2_w_in=m_ffn2_w_in, m_ffn2_w_out=m_ffn2_w_out, v_w_ada=v_w_ada, v_b_ada=v_b_ada, v_norm_g=v_norm_g, v_ffn1_w_in=v_ffn1_w_in, v_ffn1_w_out=v_ffn1_w_out, v_w_in=v_w_in, v_hgrn_lb_logits=v_hgrn_lb_logits, v_hgrn_norm_g=v_hgrn_norm_g, v_w_branch_a=v_w_branch_a, v_w_branch_b=v_w_branch_b, v_w_branch_c=v_w_branch_c, v_w_out=v_w_out, v_ffn2_w_in=v_ffn2_w_in, v_ffn2_w_out=v_ffn2_w_out)
    weights = {n: given[n] for n in TWIN_WEIGHTS}
    shared = {n: given[n] for n in SHARED_INPUTS}
    per_example = {n: given[n] for n in ['x', 'c']}
    grad_fn = _jax.value_and_grad(_loss, argnums=(0, 1))

    def one_microbatch(ex, loss_target):
        ex = dict(ex)
        diff = ex.pop(TWIN_DIFF_INPUT)
        return grad_fn(weights, diff, {**shared, **ex}, loss_target)

    if N_MICROBATCH == 1:
        loss, (grad_w, grad_x) = one_microbatch(per_example, given["loss_target"])
    else:
        def body(carry, xs):
            loss_sum, grad_sum = carry
            l_k, (gw_k, gx_k) = one_microbatch(xs[0], xs[1])
            with _jax.named_scope("update"):
                return (loss_sum + l_k, _jax.tree.map(_jnp.add, grad_sum, gw_k)), gx_k

        init = (_jnp.zeros((), _jnp.float32), _jax.tree.map(_jnp.zeros_like, weights))
        (loss, grad_w), grad_x = _jax.lax.scan(body, init, (per_example, given["loss_target"]))
    with _jax.named_scope("update"):
        delta_w, new_m, new_v = {}, {}, {}
        for n in TWIN_WEIGHTS:
            delta_w[n], new_m[n], new_v[n] = _adamw(weights[n], grad_w[n], given["m_" + n], given["v_" + n])
    return (loss, grad_x, *[grad_w[n] for n in TWIN_WEIGHTS], *[delta_w[n] for n in TWIN_WEIGHTS],
            *[new_m[n] for n in TWIN_WEIGHTS], *[new_v[n] for n in TWIN_WEIGHTS])
```

```python
import functools
import math

import jax
import jax.numpy as jnp
from jax import lax
from jax.experimental import pallas as pl
from jax.experimental.pallas import tpu as pltpu

F32 = jnp.float32
BF16 = jnp.bfloat16

A_HEADS, A_KDIM, A_VDIM, A_CHUNK = 6, 128, 64, 64
B_HEADS, HDIM = 6, 64
C_GROUPS = ((128, 1), (512, 4), (2048, 16))
C_HPG = 4
C_HEADS = C_HPG * len(C_GROUPS)
N_BRANCH = 3
EPS = 1e-6
NEG_BIG = -1e30
TINY = 1e-30
A_QK = A_HEADS * A_KDIM
A_V = A_HEADS * A_VDIM
B_W = B_HEADS * HDIM
C_W = C_HEADS * HDIM
C_OUT = C_HPG * HDIM
COL_AQ, COL_AF, COL_AI, COL_AG = 0, A_QK, 2 * A_QK, 2 * A_QK + A_V
COL_BQ = 2 * A_QK + 2 * A_V
COL_BK, COL_BV = COL_BQ + B_W, COL_BQ + 2 * B_W
COL_CQ = COL_BQ + 3 * B_W
COL_CK, COL_CV = COL_CQ + C_W, COL_CQ + 2 * C_W
COL_GATE = COL_CQ + 3 * C_W

ADAM_LR, ADAM_B1, ADAM_B2, ADAM_EPS, ADAM_WD, ADAM_STEP = 0.001, 0.9, 0.999, 1e-08, 0.01, 10

N_DEV = 8
LANES = 128
VMEM_LIMIT = 48 * 1024 * 1024
SUB = 16
EXP_CLAMP = 80.0
MESH = pl.DeviceIdType.MESH

BIG_WEIGHTS = ("ffn1_w_in", "ffn1_w_out", "w_in", "w_branch_a", "w_branch_b", "w_branch_c", "w_out",
               "ffn2_w_in", "ffn2_w_out")
ROW_SHARDED = ("ffn1_w_out", "w_out", "ffn2_w_out")


def _cparams(sem):
    return pltpu.CompilerParams(dimension_semantics=sem, vmem_limit_bytes=VMEM_LIMIT)


def _tile(n, cap):
    best, t = None, LANES
    while t <= min(n, cap):
        if n % t == 0:
            best = t
        t += LANES
    return best or n


def _rows(t, cap=256):
    r = cap
    while t % r:
        r //= 2
    return r


def _dot(a, b):
    return jnp.dot(a, b, preferred_element_type=F32)


def _dot_nt(a, b):
    return lax.dot_general(a, b, (((1,), (1,)), ((), ())), preferred_element_type=F32)


def _dot_tn(a, b):
    return lax.dot_general(a, b, (((0,), (0,)), ((), ())), preferred_element_type=F32)


def _split3(x):
    h = x.astype(BF16)
    r = x - h.astype(F32)
    m = r.astype(BF16)
    lo = (r - m.astype(F32)).astype(BF16)
    return h, m, lo


def _ones_left(mat01, x):
    h, m, lo = _split3(x)
    return _dot(mat01, h) + _dot(mat01, m) + _dot(mat01, lo)


def _ones_right(x, mat01):
    h, m, lo = _split3(x)
    return _dot(h, mat01) + _dot(m, mat01) + _dot(lo, mat01)


def _silu(x):
    return x * jax.nn.sigmoid(x)


def _dsilu(x):
    s = jax.nn.sigmoid(x)
    return s * (1.0 + x * (1.0 - s))


def _matmul(a, b, *, ta=False, tb=False, out_dtype=F32, name):
    if ta:
        kdim, m = a.shape
    else:
        m, kdim = a.shape
    n = b.shape[0] if tb else b.shape[1]
    tm, tn, tk = _tile(m, 512), _tile(n, 512), _tile(kdim, 1024)
    nk = kdim // tk
    dims = (((0 if ta else 1,), (1 if tb else 0,)), ((), ()))

    def body(a_ref, b_ref, o_ref, *scratch):
        p = lax.dot_general(a_ref[...].astype(BF16), b_ref[...].astype(BF16), dims, preferred_element_type=F32)
        if nk == 1:
            o_ref[...] = p.astype(o_ref.dtype)
            return
        acc = scratch[0]
        k = pl.program_id(2)

        @pl.when(k == 0)
        def _():
            acc[...] = p

        @pl.when(k > 0)
        def _():
            acc[...] += p

        @pl.when(k == nk - 1)
        def _():
            o_ref[...] = acc[...].astype(o_ref.dtype)

    a_spec = pl.BlockSpec((tk, tm), lambda i, j, k: (k, i)) if ta else pl.BlockSpec((tm, tk), lambda i, j, k: (i, k))
    b_spec = pl.BlockSpec((tn, tk), lambda i, j, k: (j, k)) if tb else pl.BlockSpec((tk, tn), lambda i, j, k: (k, j))
    return pl.pallas_call(
        body, name=name, grid=(m // tm, n // tn, nk), in_specs=[a_spec, b_spec],
        out_specs=pl.BlockSpec((tm, tn), lambda i, j, k: (i, j)),
        out_shape=jax.ShapeDtypeStruct((m, n), out_dtype),
        scratch_shapes=[pltpu.VMEM((tm, tn), F32)] if nk > 1 else [],
        compiler_params=_cparams(("parallel", "parallel", "arbitrary")),
    )(a, b)


def _rms_fwd(z, mcol, acol, res, out_dtype, name):
    t, d = z.shape
    tr = _rows(t)
    has_res = res is not None

    def body(*refs):
        if has_res:
            z_ref, m_ref, a_ref, r_ref, o_ref = refs
        else:
            z_ref, m_ref, a_ref, o_ref = refs
        zf = z_ref[...]
        r = lax.rsqrt(jnp.mean(zf * zf, axis=-1, keepdims=True) + EPS)
        y = zf * r * m_ref[...] + a_ref[...]
        if has_res:
            y = r_ref[...] + y
        o_ref[...] = y.astype(o_ref.dtype)

    row = pl.BlockSpec((tr, d), lambda i: (i, 0))
    col = pl.BlockSpec((1, d), lambda i: (0, 0))
    ins = [z, mcol, acol] + ([res] if has_res else [])
    return pl.pallas_call(
        body, name=name, grid=(t // tr,), in_specs=[row, col, col] + ([row] if has_res else []),
        out_specs=row, out_shape=jax.ShapeDtypeStruct((t, d), out_dtype),
        compiler_params=_cparams(("parallel",)),
    )(*ins)


def _rms_bwd(d_out, z, mcol, dres, out_dtype, name):
    t, d = z.shape
    tr = _rows(t)
    has_res = dres is not None

    def body(*refs):
        if has_res:
            d_ref, z_ref, m_ref, r_ref, o_ref, s1_ref, s2_ref = refs
        else:
            d_ref, z_ref, m_ref, o_ref, s1_ref, s2_ref = refs
        i = pl.program_id(0)
        zf = z_ref[...]
        r = lax.rsqrt(jnp.mean(zf * zf, axis=-1, keepdims=True) + EPS)
        zh = zf * r
        df = d_ref[...].astype(F32)
        dzh = df * m_ref[...]
        dz = r * (dzh - zh * jnp.mean(dzh * zh, axis=-1, keepdims=True))
        if has_res:
            dz = dz + r_ref[...]
        o_ref[...] = dz.astype(o_ref.dtype)
        s1 = jnp.sum(df * zh, axis=0, keepdims=True)
        s2 = jnp.sum(df, axis=0, keepdims=True)

        @pl.when(i == 0)
        def _():
            s1_ref[...] = s1
            s2_ref[...] = s2

        @pl.when(i > 0)
        def _():
            s1_ref[...] += s1
            s2_ref[...] += s2

    row = pl.BlockSpec((tr, d), lambda i: (i, 0))
    col = pl.BlockSpec((1, d), lambda i: (0, 0))
    ins = [d_out, z, mcol] + ([dres] if has_res else [])
    return pl.pallas_call(
        body, name=name, grid=(t // tr,), in_specs=[row, row, col] + ([row] if has_res else []),
        out_specs=[row, col, col],
        out_shape=[jax.ShapeDtypeStruct((t, d), out_dtype), jax.ShapeDtypeStruct((1, d), F32),
                   jax.ShapeDtypeStruct((1, d), F32)],
        compiler_params=_cparams(("arbitrary",)),
    )(*ins)


def _swiglu_fwd(u, name):
    t, f2 = u.shape
    f = f2 // 2
    tr = _rows(t)

    def body(u_ref, s_ref):
        a = u_ref[:, :f].astype(F32)
        b = u_ref[:, f:].astype(F32)
        s_ref[...] = (_silu(a) * b).astype(s_ref.dtype)

    return pl.pallas_call(
        body, name=name, grid=(t // tr,), in_specs=[pl.BlockSpec((tr, f2), lambda i: (i, 0))],
        out_specs=pl.BlockSpec((tr, f), lambda i: (i, 0)), out_shape=jax.ShapeDtypeStruct((t, f), BF16),
        compiler_params=_cparams(("parallel",)),
    )(u)


def _swiglu_bwd(u, ds, name):
    t, f2 = u.shape
    f = f2 // 2
    tr = _rows(t)

    def body(u_ref, ds_ref, du_ref):
        a = u_ref[:, :f].astype(F32)
        b = u_ref[:, f:].astype(F32)
        g = ds_ref[...].astype(F32)
        du_ref[:, :f] = (g * b * _dsilu(a)).astype(du_ref.dtype)
        du_ref[:, f:] = (g * _silu(a)).astype(du_ref.dtype)

    return pl.pallas_call(
        body, name=name, grid=(t // tr,),
        in_specs=[pl.BlockSpec((tr, f2), lambda i: (i, 0)), pl.BlockSpec((tr, f), lambda i: (i, 0))],
        out_specs=pl.BlockSpec((tr, f2), lambda i: (i, 0)), out_shape=jax.ShapeDtypeStruct((t, f2), BF16),
        compiler_params=_cparams(("parallel",)),
    )(u, ds)


def _loss_head(y, target, name):
    t, d = y.shape
    tr = _rows(t)

    def body(y_ref, t_ref, dy_ref, sq_ref):
        i = pl.program_id(0)
        e = y_ref[...] - t_ref[...]
        dy_ref[...] = e * (1.0 / d)
        s = jnp.sum(e * e, axis=0, keepdims=True)

        @pl.when(i == 0)
        def _():
            sq_ref[...] = s

        @pl.when(i > 0)
        def _():
            sq_ref[...] += s

    row = pl.BlockSpec((tr, d), lambda i: (i, 0))
    col = pl.BlockSpec((1, d), lambda i: (0, 0))
    return pl.pallas_call(
        body, name=name, grid=(t // tr,), in_specs=[row, row], out_specs=[row, col],
        out_shape=[jax.ShapeDtypeStruct((t, d), F32), jax.ShapeDtypeStruct((1, d), F32)],
        compiler_params=_cparams(("arbitrary",)),
    )(y, target)


def _hgrn_consts():
    c = A_CHUNK
    shift = SUB.bit_length() - 1
    r = lax.broadcasted_iota(jnp.int32, (c, c), 0)
    s = lax.broadcasted_iota(jnp.int32, (c, c), 1)
    sub_r = lax.shift_right_logical(r, shift)
    incl = s <= r
    masks = [jnp.logical_and(sub_r == i, incl) for i in range(c // SUB)]
    rev_incl = jnp.where(s >= r, 1.0, 0.0).astype(BF16)
    r2 = lax.broadcasted_iota(jnp.int32, (2 * c + 8, c), 0)
    s2 = lax.broadcasted_iota(jnp.int32, (2 * c + 8, c), 1)
    sub_start = lax.shift_left(lax.shift_right_logical(r2 - c, shift), shift)
    running = jnp.where(s2 <= r2, 1.0, 0.0)
    before = jnp.where(s2 < sub_start, 1.0, 0.0)
    stack = jnp.where(r2 < c, running, jnp.where(r2 < 2 * c, before, 1.0)).astype(BF16)
    return stack, masks, incl, rev_incl


def _hgrn_chunk(q_raw, f_raw, lbv, stack):
    c = A_CHUNK
    sg = jax.nn.sigmoid(f_raw)
    sgn = jax.nn.sigmoid(-f_raw)
    f = lbv + (1.0 - lbv) * sg
    logf = jnp.log(jnp.maximum(f, TINY))
    k = (1.0 - lbv) * sgn
    q = _silu(q_raw)
    bb = _ones_left(stack, logf)
    b = bb[:c]
    bsrow = bb[c:2 * c]
    b_end = bb[2 * c:2 * c + 1]
    e_sub = jnp.exp(b - bsrow)
    e_b = jnp.exp(b)
    e_end = jnp.exp(b_end - b)
    qs = q * e_sub
    q_in = q * e_b
    kend = k * e_end
    kfac = [jnp.exp(jnp.minimum(bsrow[i * SUB:i * SUB + 1] - b, EXP_CLAMP)) for i in range(c // SUB)]
    return dict(sg=sg, sgn=sgn, f=f, k=k, q=q, b=b, b_end=b_end, e_sub=e_sub, e_b=e_b, e_end=e_end,
                qs=qs, q_in=q_in, kend=kend, kfac=kfac)


def _hgrn_scores(ch, masks):
    qs_b = ch["qs"].astype(BF16)
    a = None
    for i, mk in enumerate(masks):
        ki = (ch["k"] * ch["kfac"][i]).astype(BF16)
        part = jnp.where(mk, _dot_nt(qs_b, ki), 0.0)
        a = part if a is None else a + part
    return a


def _hgrn_fwd(p, lb, hn2, name):
    t = p.shape[0]
    tb = _rows(t)
    nt = t // tb
    nc = tb // A_CHUNK
    c = A_CHUNK

    def body(q_ref, f_ref, i_ref, g_ref, lb_ref, hn_ref, y_ref, o_ref, st_ref, s_scr):
        j = pl.program_id(1)

        @pl.when(j == 0)
        def _():
            s_scr[...] = jnp.zeros_like(s_scr)

        stack, masks, _, _ = _hgrn_consts()
        for hh in range(2):
            lsl = slice(A_KDIM * hh, A_KDIM * (hh + 1))
            hsl = slice(A_VDIM * hh, A_VDIM * (hh + 1))
            lbv = lb_ref[:, lsl]

            def chunk(ci, carry, lsl=lsl, hsl=hsl, lbv=lbv, hh=hh):
                r0 = pl.multiple_of(ci * c, c)
                ch = _hgrn_chunk(q_ref[pl.ds(r0, c), lsl], f_ref[pl.ds(r0, c), lsl], lbv, stack)
                v = i_ref[pl.ds(r0, c), hsl].astype(BF16)
                st = s_scr[hh]
                st_ref[hh, ci] = st
                a = _hgrn_scores(ch, masks)
                o = _dot_nt(ch["q_in"].astype(BF16), st.astype(BF16)) + _dot(a.astype(BF16), v)
                s_scr[hh] = st * jnp.exp(ch["b_end"]) + _dot_tn(v, ch["kend"].astype(BF16))
                o_ref[pl.ds(r0, c), hsl] = o
                return carry

            lax.fori_loop(0, nc, chunk, 0)
        for hh in range(2):
            hsl = slice(A_VDIM * hh, A_VDIM * (hh + 1))
            o = o_ref[:, hsl]
            r = lax.rsqrt(jnp.mean(o * o, axis=-1, keepdims=True) + EPS)
            y_ref[:, hsl] = (o * r * hn_ref[:, hsl] * _silu(g_ref[:, hsl])).astype(y_ref.dtype)

    w2 = 2 * A_KDIM
    return pl.pallas_call(
        body, name=name, grid=(A_HEADS // 2, nt),
        in_specs=[pl.BlockSpec((tb, w2), lambda h, j: (j, COL_AQ // w2 + h)),
                  pl.BlockSpec((tb, w2), lambda h, j: (j, COL_AF // w2 + h)),
                  pl.BlockSpec((tb, LANES), lambda h, j: (j, COL_AI // LANES + h)),
                  pl.BlockSpec((tb, LANES), lambda h, j: (j, COL_AG // LANES + h)),
                  pl.BlockSpec((1, w2), lambda h, j: (0, h)),
                  pl.BlockSpec((1, LANES), lambda h, j: (0, 0))],
        out_specs=[pl.BlockSpec((tb, LANES), lambda h, j: (j, h)),
                   pl.BlockSpec((tb, LANES), lambda h, j: (j, h)),
                   pl.BlockSpec((2, nc, A_VDIM, A_KDIM), lambda h, j: (h, j, 0, 0))],
        out_shape=[jax.ShapeDtypeStruct((t, A_V), BF16), jax.ShapeDtypeStruct((t, A_V), F32),
                   jax.ShapeDtypeStruct((A_HEADS, t // c, A_VDIM, A_KDIM), F32)],
        scratch_shapes=[pltpu.VMEM((2, A_VDIM, A_KDIM), F32)],
        compiler_params=_cparams(("parallel", "arbitrary")),
    )(p, p, p, p, lb, hn2)


def _hgrn_bwd(p, lb, hn2, o_raw, states, dya, name):
    t = p.shape[0]
    tb = _rows(t)
    nt = t // tb
    nc = tb // A_CHUNK
    c = A_CHUNK

    def body(q_ref, f_ref, i_ref, g_ref, lb_ref, hn_ref, o_ref, st_ref, dy_ref,
             dq_ref, df_ref, di_ref, dg_ref, dlb_ref, dhn_ref, ds_scr, do_scr):
        j = pl.program_id(1)

        @pl.when(j == 0)
        def _():
            ds_scr[...] = jnp.zeros_like(ds_scr)
            dlb_ref[...] = jnp.zeros_like(dlb_ref)
            dhn_ref[...] = jnp.zeros_like(dhn_ref)

        stack, masks, incl, rev_incl = _hgrn_consts()
        for hh in range(2):
            hsl = slice(A_VDIM * hh, A_VDIM * (hh + 1))
            o = o_ref[:, hsl]
            g = g_ref[:, hsl]
            dy = dy_ref[:, hsl].astype(F32)
            hn = hn_ref[:, hsl]
            r = lax.rsqrt(jnp.mean(o * o, axis=-1, keepdims=True) + EPS)
            oh = o * r
            sgate = _silu(g)
            dg_ref[:, hsl] = dy * oh * hn * _dsilu(g)
            dhn_ref[0, :, hsl] += jnp.sum(dy * oh * sgate, axis=0, keepdims=True)
            doh = dy * hn * sgate
            do_scr[:, hsl] = r * (doh - oh * jnp.mean(doh * oh, axis=-1, keepdims=True))

        for hh in range(2):
            lsl = slice(A_KDIM * hh, A_KDIM * (hh + 1))
            hsl = slice(A_VDIM * hh, A_VDIM * (hh + 1))
            lbv = lb_ref[:, lsl]

            def chunk(it, carry, lsl=lsl, hsl=hsl, lbv=lbv, hh=hh):
                ci = nc - 1 - it
                r0 = pl.multiple_of(ci * c, c)
                q_raw = q_ref[pl.ds(r0, c), lsl]
                f_raw = f_ref[pl.ds(r0, c), lsl]
                ch = _hgrn_chunk(q_raw, f_raw, lbv, stack)
                v = i_ref[pl.ds(r0, c), hsl].astype(BF16)
                do = do_scr[pl.ds(r0, c), hsl]
                do_b = do.astype(BF16)
                st = st_ref[hh, ci]
                st_b = st.astype(BF16)
                dst = ds_scr[hh]
                dst_b = dst.astype(BF16)
                qs_b = ch["qs"].astype(BF16)
                kend_b = ch["kend"].astype(BF16)
                a = _hgrn_scores(ch, masks)
                da = jnp.where(incl, _dot_nt(do_b, v), 0.0)
                dv = _dot_tn(a.astype(BF16), do_b) + _dot_nt(kend_b, dst_b)
                dq_i = None
                dk_i = None
                kdk_i = None
                for i, mk in enumerate(masks):
                    dam = jnp.where(mk, da, 0.0).astype(BF16)
                    ki = (ch["k"] * ch["kfac"][i]).astype(BF16)
                    pq = _dot(dam, ki)
                    pk = _dot_tn(dam, qs_b)
                    dq_i = pq if dq_i is None else dq_i + pq
                    dk_i = ch["kfac"][i] * pk if dk_i is None else dk_i + ch["kfac"][i] * pk
                    kdk_i = ki.astype(F32) * pk if kdk_i is None else kdk_i + ki.astype(F32) * pk
                dq_x = _dot(do_b, st_b)
                dk_x = _dot(v, dst_b)
                dq = ch["e_sub"] * dq_i + ch["e_b"] * dq_x
                dk = dk_i + ch["e_end"] * dk_x
                ds_scr[hh] = dst * jnp.exp(ch["b_end"]) + _dot_tn(do_b, ch["q_in"].astype(BF16))
                kx = ch["kend"] * dk_x
                db = (qs_b.astype(F32) * dq_i + ch["q_in"] * dq_x) - (kdk_i + kx)
                later = (jnp.exp(ch["b_end"]) * jnp.sum(dst * st, axis=0, keepdims=True)
                         + jnp.sum(kx, axis=0, keepdims=True))
                dlogf = later + _ones_left(rev_incl, db)
                dfv = jnp.where(ch["f"] > TINY, dlogf / ch["f"], 0.0)
                dq_ref[pl.ds(r0, c), lsl] = dq * _dsilu(q_raw)
                df_ref[pl.ds(r0, c), lsl] = (1.0 - lbv) * ch["sg"] * ch["sgn"] * (dfv - dk)
                dlb_ref[:, lsl] += jnp.sum(dfv * (1.0 - ch["sg"]) - dk * ch["sgn"], axis=0, keepdims=True)
                di_ref[pl.ds(r0, c), hsl] = dv
                return carry

            lax.fori_loop(0, nc, chunk, 0)

    w2 = 2 * A_KDIM
    rev = lambda j: nt - 1 - j
    return pl.pallas_call(
        body, name=name, grid=(A_HEADS // 2, nt),
        in_specs=[pl.BlockSpec((tb, w2), lambda h, j: (rev(j), COL_AQ // w2 + h)),
                  pl.BlockSpec((tb, w2), lambda h, j: (rev(j), COL_AF // w2 + h)),
                  pl.BlockSpec((tb, LANES), lambda h, j: (rev(j), COL_AI // LANES + h)),
                  pl.BlockSpec((tb, LANES), lambda h, j: (rev(j), COL_AG // LANES + h)),
                  pl.BlockSpec((1, w2), lambda h, j: (0, h)),
                  pl.BlockSpec((1, LANES), lambda h, j: (0, 0)),
                  pl.BlockSpec((tb, LANES), lambda h, j: (rev(j), h)),
                  pl.BlockSpec((2, nc, A_VDIM, A_KDIM), lambda h, j: (h, rev(j), 0, 0)),
                  pl.BlockSpec((tb, LANES), lambda h, j: (rev(j), h))],
        out_specs=[pl.BlockSpec((tb, w2), lambda h, j: (rev(j), h)),
                   pl.BlockSpec((tb, w2), lambda h, j: (rev(j), h)),
                   pl.BlockSpec((tb, LANES), lambda h, j: (rev(j), h)),
                   pl.BlockSpec((tb, LANES), lambda h, j: (rev(j), h)),
                   pl.BlockSpec((1, w2), lambda h, j: (0, h)),
                   pl.BlockSpec((1, 1, LANES), lambda h, j: (h, 0, 0))],
        out_shape=[jax.ShapeDtypeStruct((t, A_QK), F32), jax.ShapeDtypeStruct((t, A_QK), F32),
                   jax.ShapeDtypeStruct((t, A_V), F32), jax.ShapeDtypeStruct((t, A_V), F32),
                   jax.ShapeDtypeStruct((1, A_QK), F32), jax.ShapeDtypeStruct((A_HEADS // 2, 1, LANES), F32)],
        scratch_shapes=[pltpu.VMEM((2, A_VDIM, A_KDIM), F32), pltpu.VMEM((tb, LANES), F32)],
        compiler_params=_cparams(("parallel", "arbitrary")),
    )(p, p, p, p, lb, hn2, o_raw, states, dya)


BLK = 128
SCALE = HDIM ** -0.5


def _softplus(z):
    return jnp.maximum(z, 0.0) + jnp.log(1.0 + jnp.exp(-jnp.abs(z)))


def _sb_fwd(p, name):
    t = p.shape[0]
    nq = t // BLK

    def body(q_ref, k_ref, v_ref, o_ref, tot_ref, kb, vb):
        qi = pl.program_id(1)

        @pl.when(qi == 0)
        def _():
            kb[...] = k_ref[...].astype(BF16)
            vb[...] = v_ref[...].astype(BF16)

        row = lax.broadcasted_iota(jnp.int32, (BLK, BLK), 0)
        col = lax.broadcasted_iota(jnp.int32, (BLK, BLK), 1)
        later = (row > col).astype(BF16)
        qs = [(q_ref[:, HDIM * h:HDIM * (h + 1)] * SCALE).astype(BF16) for h in range(2)]

        def step(it, carry):
            jj = qi - it
            k0 = pl.multiple_of(jj * BLK, BLK)
            kk = kb[pl.ds(k0, BLK), :]
            vv = vb[pl.ds(k0, BLK), :]
            mask = (col + jj * BLK) < (row + qi * BLK)
            out = []
            for h in range(2):
                run, acc = carry[2 * h], carry[2 * h + 1]
                hs = slice(HDIM * h, HDIM * (h + 1))
                z = _dot_nt(qs[h], kk[:, hs])
                sp = _softplus(z)
                lm = jnp.where(mask, -sp, 0.0)
                sfx = _ones_right(lm, later)
                a = jnp.where(mask, jnp.exp(z - sp + sfx + run), 0.0)
                acc = acc + _dot(a.astype(BF16), vv[:, hs])
                run = run + jnp.sum(lm, axis=1, keepdims=True)
                out += [run, acc]
            return tuple(out)

        z1 = jnp.zeros((BLK, 1), F32)
        z64 = jnp.zeros((BLK, HDIM), F32)
        res = lax.fori_loop(0, qi + 1, step, (z1, z64, z1, z64))
        o_ref[:, :HDIM] = res[1]
        o_ref[:, HDIM:] = res[3]
        tot_ref[:, :HDIM] = jnp.broadcast_to(res[0], (BLK, HDIM))
        tot_ref[:, HDIM:] = jnp.broadcast_to(res[2], (BLK, HDIM))

    out_blk = pl.BlockSpec((BLK, LANES), lambda h, i: (i, h))
    return pl.pallas_call(
        body, name=name, grid=(B_HEADS // 2, nq),
        in_specs=[pl.BlockSpec((BLK, LANES), lambda h, i: (i, COL_BQ // LANES + h)),
                  pl.BlockSpec((t, LANES), lambda h, i: (0, COL_BK // LANES + h)),
                  pl.BlockSpec((t, LANES), lambda h, i: (0, COL_BV // LANES + h))],
        out_specs=[out_blk, out_blk],
        out_shape=[jax.ShapeDtypeStruct((t, B_W), F32)] * 2,
        scratch_shapes=[pltpu.VMEM((t, LANES), BF16), pltpu.VMEM((t, LANES), BF16)],
        compiler_params=_cparams(("parallel", "arbitrary")),
    )(p, p, p)


def _sb_bwd(p, tot, do, name):
    t = p.shape[0]
    nq = t // BLK

    def body(q_ref, k_ref, v_ref, tot_ref, do_ref, dq_ref, dk_ref, dv_ref, kb, vb):
        qi = pl.program_id(1)

        @pl.when(qi == 0)
        def _():
            kb[...] = k_ref[...].astype(BF16)
            vb[...] = v_ref[...].astype(BF16)
            dk_ref[...] = jnp.zeros_like(dk_ref)
            dv_ref[...] = jnp.zeros_like(dv_ref)

        row = lax.broadcasted_iota(jnp.int32, (BLK, BLK), 0)
        col = lax.broadcasted_iota(jnp.int32, (BLK, BLK), 1)
        upto = (row <= col).astype(BF16)
        earlier = (row < col).astype(BF16)
        hsl = [slice(HDIM * h, HDIM * (h + 1)) for h in range(2)]
        qs = [(q_ref[:, hsl[h]] * SCALE).astype(BF16) for h in range(2)]
        qb = [q_ref[:, hsl[h]].astype(BF16) for h in range(2)]
        dob = [do_ref[:, hsl[h]].astype(BF16) for h in range(2)]
        total = [tot_ref[:, HDIM * h:HDIM * h + 1] for h in range(2)]

        def step(jj, carry):
            k0 = pl.multiple_of(jj * BLK, BLK)
            kk = kb[pl.ds(k0, BLK), :]
            vv = vb[pl.ds(k0, BLK), :]
            mask = (col + jj * BLK) < (row + qi * BLK)
            out = []
            for h in range(2):
                run, grun, dq = carry[3 * h], carry[3 * h + 1], carry[3 * h + 2]
                z = _dot_nt(qs[h], kk[:, hsl[h]])
                sp = _softplus(z)
                lm = jnp.where(mask, -sp, 0.0)
                sfx = total[h] - (run + _ones_right(lm, upto))
                a = jnp.where(mask, jnp.exp(z - sp + sfx), 0.0)
                da = _dot_nt(dob[h], vv[:, hsl[h]])
                g = a * da
                before = grun + _ones_right(g, earlier)
                dz = jnp.where(mask, g * jnp.exp(-sp) - jnp.exp(z - sp) * before, 0.0)
                dzb = (dz * SCALE).astype(BF16)
                dq = dq + _dot(dzb, kk[:, hsl[h]])
                dk_ref[pl.ds(k0, BLK), hsl[h]] += _dot_tn(dzb, qb[h])
                dv_ref[pl.ds(k0, BLK), hsl[h]] += _dot_tn(a.astype(BF16), dob[h])
                run = run + jnp.sum(lm, axis=1, keepdims=True)
                grun = grun + jnp.sum(g, axis=1, keepdims=True)
                out += [run, grun, dq]
            return tuple(out)

        z1 = jnp.zeros((BLK, 1), F32)
        z64 = jnp.zeros((BLK, HDIM), F32)
        res = lax.fori_loop(0, qi + 1, step, (z1, z1, z64, z1, z1, z64))
        dq_ref[:, :HDIM] = res[2]
        dq_ref[:, HDIM:] = res[5]

    blk = lambda h, i: (i, h)
    whole = lambda h, i: (0, h)
    return pl.pallas_call(
        body, name=name, grid=(B_HEADS // 2, nq),
        in_specs=[pl.BlockSpec((BLK, LANES), lambda h, i: (i, COL_BQ // LANES + h)),
                  pl.BlockSpec((t, LANES), lambda h, i: (0, COL_BK // LANES + h)),
                  pl.BlockSpec((t, LANES), lambda h, i: (0, COL_BV // LANES + h)),
                  pl.BlockSpec((BLK, LANES), blk), pl.BlockSpec((BLK, LANES), blk)],
        out_specs=[pl.BlockSpec((BLK, LANES), blk), pl.BlockSpec((t, LANES), whole),
                   pl.BlockSpec((t, LANES), whole)],
        out_shape=[jax.ShapeDtypeStruct((t, B_W), F32)] * 3,
        scratch_shapes=[pltpu.VMEM((t, LANES), BF16), pltpu.VMEM((t, LANES), BF16)],
        compiler_params=_cparams(("parallel", "arbitrary")),
    )(p, p, p, tot, do)


def _alibi_slopes(n):
    def pow2(m):
        start = 2.0 ** (-8.0 / m)
        return [start ** (i + 1) for i in range(m)]
    if math.log2(n).is_integer():
        s = pow2(n)
    else:
        c = 2 ** int(math.floor(math.log2(n)))
        s = pow2(c) + pow2(2 * c)[0::2][: n - c]
    return sorted(s, reverse=True)


def _dil_scores(qh, kh, sl, prev, exists=None):
    row = lax.broadcasted_iota(jnp.int32, (BLK, BLK), 0)
    col = lax.broadcasted_iota(jnp.int32, (BLK, BLK), 1)
    dist = row - col + (BLK if prev else 0)
    if prev:
        valid = (col - row) >= jnp.where(exists, 0, 2 * BLK)
    else:
        valid = col <= row
    s = _dot_nt(qh, kh) - sl * dist.astype(F32)
    return s, valid


def _dil_fwd(q, k, v, slope_cols, name):
    ln, cw = q.shape
    nb = ln // BLK

    def body(q_ref, kc_ref, kp_ref, vc_ref, vp_ref, sl_ref, o_ref, lse_ref):
        i = pl.program_id(1)
        for h in range(2):
            hs = slice(HDIM * h, HDIM * (h + 1))
            sl = sl_ref[:, HDIM * h:HDIM * h + 1]
            qh = (q_ref[:, hs] * SCALE).astype(BF16)
            sc, vc_ok = _dil_scores(qh, kc_ref[:, hs].astype(BF16), sl, False)
            sp, vp_ok = _dil_scores(qh, kp_ref[:, hs].astype(BF16), sl, True, i > 0)
            sc = jnp.where(vc_ok, sc, NEG_BIG)
            sp = jnp.where(vp_ok, sp, NEG_BIG)
            m = jnp.maximum(jnp.max(sc, axis=1, keepdims=True), jnp.max(sp, axis=1, keepdims=True))
            pc = jnp.exp(sc - m)
            pp = jnp.exp(sp - m)
            den = jnp.sum(pc, axis=1, keepdims=True) + jnp.sum(pp, axis=1, keepdims=True)
            o = _dot(pc.astype(BF16), vc_ref[:, hs].astype(BF16)) + _dot(pp.astype(BF16), vp_ref[:, hs].astype(BF16))
            o_ref[:, hs] = o / den
            lse_ref[:, hs] = jnp.broadcast_to(m + jnp.log(den), (BLK, HDIM))

    cur = pl.BlockSpec((BLK, LANES), lambda c, i: (i, c))
    prv = pl.BlockSpec((BLK, LANES), lambda c, i: (jnp.maximum(i - 1, 0), c))
    return pl.pallas_call(
        body, name=name, grid=(cw // LANES, nb),
        in_specs=[cur, cur, prv, cur, prv, pl.BlockSpec((1, LANES), lambda c, i: (0, c))],
        out_specs=[cur, cur], out_shape=[jax.ShapeDtypeStruct((ln, cw), F32)] * 2,
        compiler_params=_cparams(("parallel", "parallel")),
    )(q, k, k, v, v, slope_cols)


def _dil_bwd(q, k, v, do, o, lse, slope_cols, name):
    ln, cw = q.shape
    nb = ln // BLK

    def body(q_ref, qn_ref, kc_ref, kp_ref, vc_ref, vp_ref, do_ref, don_ref, o_ref, on_ref, l_ref, ln_ref, sl_ref,
             dq_ref, dk_ref, dv_ref):
        i = pl.program_id(1)
        has_prev = i > 0
        has_next = i < nb - 1
        for h in range(2):
            hs = slice(HDIM * h, HDIM * (h + 1))
            sl = sl_ref[:, HDIM * h:HDIM * h + 1]
            qb = q_ref[:, hs].astype(BF16)
            qnb = qn_ref[:, hs].astype(BF16)
            qh = (q_ref[:, hs] * SCALE).astype(BF16)
            qnh = (qn_ref[:, hs] * SCALE).astype(BF16)
            kc = kc_ref[:, hs].astype(BF16)
            kp = kp_ref[:, hs].astype(BF16)
            vc = vc_ref[:, hs].astype(BF16)
            vp = vp_ref[:, hs].astype(BF16)
            do_f = do_ref[:, hs]
            don_f = don_ref[:, hs]
            dob = do_f.astype(BF16)
            donb = don_f.astype(BF16)
            delta = jnp.sum(do_f * o_ref[:, hs], axis=1, keepdims=True)
            deltan = jnp.sum(don_f * on_ref[:, hs], axis=1, keepdims=True)
            lse = l_ref[:, HDIM * h:HDIM * h + 1]
            lsen = ln_ref[:, HDIM * h:HDIM * h + 1]
            s, ok = _dil_scores(qh, kc, sl, False)
            p_cc = jnp.where(ok, jnp.exp(jnp.where(ok, s, NEG_BIG) - lse), 0.0)
            ds_cc = p_cc * (_dot_nt(dob, vc) - delta)
            s, ok = _dil_scores(qh, kp, sl, True, has_prev)
            p_cp = jnp.where(ok, jnp.exp(jnp.where(ok, s, NEG_BIG) - lse), 0.0)
            ds_cp = p_cp * (_dot_nt(dob, vp) - delta)
            s, ok = _dil_scores(qnh, kc, sl, True, has_next)
            p_nc = jnp.where(ok, jnp.exp(jnp.where(ok, s, NEG_BIG) - lsen), 0.0)
            ds_nc = p_nc * (_dot_nt(donb, vc) - deltan)
            ds_cc_b = (ds_cc * SCALE).astype(BF16)
            ds_cp_b = (ds_cp * SCALE).astype(BF16)
            ds_nc_b = (ds_nc * SCALE).astype(BF16)
            dq_ref[:, hs] = _dot(ds_cc_b, kc) + _dot(ds_cp_b, kp)
            dk_ref[:, hs] = _dot_tn(ds_cc_b, qb) + _dot_tn(ds_nc_b, qnb)
            dv_ref[:, hs] = _dot_tn(p_cc.astype(BF16), dob) + _dot_tn(p_nc.astype(BF16), donb)

    cur = pl.BlockSpec((BLK, LANES), lambda c, i: (i, c))
    prv = pl.BlockSpec((BLK, LANES), lambda c, i: (jnp.maximum(i - 1, 0), c))
    nxt = pl.BlockSpec((BLK, LANES), lambda c, i: (jnp.minimum(i + 1, nb - 1), c))
    return pl.pallas_call(
        body, name=name, grid=(cw // LANES, nb),
        in_specs=[cur, nxt, cur, prv, cur, prv, cur, nxt, cur, nxt, cur, nxt,
                  pl.BlockSpec((1, LANES), lambda c, i: (0, c))],
        out_specs=[cur, cur, cur], out_shape=[jax.ShapeDtypeStruct((ln, cw), F32)] * 3,
        compiler_params=_cparams(("parallel", "parallel")),
    )(q, q, k, k, v, v, do, do, o, o, lse, lse, slope_cols)


def _dil_merge(os_, ls_, name):
    t, w = os_[0].shape
    tr = _rows(t)

    def body(o0, o1, o2, l0, l1, l2, y_ref, lse_ref):
        a, b, c = l0[...], l1[...], l2[...]
        m = jnp.maximum(jnp.maximum(a, b), c)
        ea, eb, ec = jnp.exp(a - m), jnp.exp(b - m), jnp.exp(c - m)
        den = ea + eb + ec
        y_ref[...] = (ea * o0[...] + eb * o1[...] + ec * o2[...]) / den
        lse_ref[...] = m + jnp.log(den)

    row = pl.BlockSpec((tr, w), lambda i: (i, 0))
    return pl.pallas_call(
        body, name=name, grid=(t // tr,), in_specs=[row] * 6, out_specs=[row, row],
        out_shape=[jax.ShapeDtypeStruct((t, w), F32)] * 2, compiler_params=_cparams(("parallel",)),
    )(*os_, *ls_)


def _gate_fwd(ys, gl, ws, name):
    t = gl.shape[0]
    d = gl.shape[1] // N_BRANCH
    tr = _rows(t)

    def body(ya, yb, yc, gl_ref, wa, wb, wc, m_ref):
        acc = None
        for i, (y, w) in enumerate(((ya, wa), (yb, wb), (yc, wc))):
            z = _dot(y[...].astype(BF16), w[...])
            term = jax.nn.sigmoid(gl_ref[:, i * d:(i + 1) * d]) * z
            acc = term if acc is None else acc + term
        m_ref[...] = acc.astype(m_ref.dtype)

    rows = [pl.BlockSpec((tr, y.shape[1]), lambda i: (i, 0)) for y in ys]
    wsp = [pl.BlockSpec(w.shape, lambda i: (0, 0)) for w in ws]
    return pl.pallas_call(
        body, name=name, grid=(t // tr,),
        in_specs=rows + [pl.BlockSpec((tr, N_BRANCH * d), lambda i: (i, 0))] + wsp,
        out_specs=pl.BlockSpec((tr, d), lambda i: (i, 0)), out_shape=jax.ShapeDtypeStruct((t, d), BF16),
        compiler_params=_cparams(("parallel",)),
    )(*ys, gl, *ws)


def _gate_bwd(dm, ys, gl, ws, name):
    t = gl.shape[0]
    d = gl.shape[1] // N_BRANCH
    tr = _rows(t)

    def body(dm_ref, ya, yb, yc, gl_ref, wa, wb, wc, dya, dyb, dyc, dgl_ref, dwa, dwb, dwc):
        step = pl.program_id(0)
        dmv = dm_ref[...].astype(F32)
        for i, (y, w, dy, dw) in enumerate(((ya, wa, dya, dwa), (yb, wb, dyb, dwb), (yc, wc, dyc, dwc))):
            yb16 = y[...].astype(BF16)
            z = _dot(yb16, w[...])
            sg = jax.nn.sigmoid(gl_ref[:, i * d:(i + 1) * d])
            dgl_ref[:, i * d:(i + 1) * d] = dmv * z * sg * (1.0 - sg)
            e = (dmv * sg).astype(BF16)
            dy[...] = _dot_nt(e, w[...])
            contrib = _dot_tn(yb16, e)

            @pl.when(step == 0)
            def _(dw=dw, contrib=contrib):
                dw[...] = contrib

            @pl.when(step > 0)
            def _(dw=dw, contrib=contrib):
                dw[...] += contrib

    rows = [pl.BlockSpec((tr, y.shape[1]), lambda i: (i, 0)) for y in ys]
    wsp = [pl.BlockSpec(w.shape, lambda i: (0, 0)) for w in ws]
    gsp = pl.BlockSpec((tr, N_BRANCH * d), lambda i: (i, 0))
    return pl.pallas_call(
        body, name=name, grid=(t // tr,),
        in_specs=[pl.BlockSpec((tr, d), lambda i: (i, 0))] + rows + [gsp] + wsp,
        out_specs=rows + [gsp] + wsp,
        out_shape=[jax.ShapeDtypeStruct(y.shape, F32) for y in ys] + [jax.ShapeDtypeStruct(gl.shape, F32)]
        + [jax.ShapeDtypeStruct(w.shape, F32) for w in ws],
        compiler_params=_cparams(("arbitrary",)),
    )(dm, *ys, gl, *ws)


def _adamw(w, m, v, gparts, name):
    r, c = w.shape
    n = gparts.shape[0]
    br = LANES if r % LANES == 0 else r
    c1 = 1.0 - ADAM_B1 ** ADAM_STEP
    c2 = 1.0 - ADAM_B2 ** ADAM_STEP

    def body(w_ref, m_ref, v_ref, g_ref, go_ref, d_ref, mo_ref, vo_ref):
        g = g_ref[0].astype(F32)
        for i in range(1, n):
            g = g + g_ref[i].astype(F32)
        mn = ADAM_B1 * m_ref[...] + (1.0 - ADAM_B1) * g
        vn = ADAM_B2 * v_ref[...] + (1.0 - ADAM_B2) * (g * g)
        go_ref[...] = g
        mo_ref[...] = mn
        vo_ref[...] = vn
        d_ref[...] = -ADAM_LR * ((mn / c1) / (jnp.sqrt(vn / c2) + ADAM_EPS) + ADAM_WD * w_ref[...])

    blk = pl.BlockSpec((br, c), lambda i: (i, 0))
    return pl.pallas_call(
        body, name=name, grid=(r // br,),
        in_specs=[blk, blk, blk, pl.BlockSpec((n, br, c), lambda i: (0, i, 0))],
        out_specs=[blk] * 4, out_shape=[jax.ShapeDtypeStruct((r, c), F32)] * 4,
        compiler_params=_cparams(("parallel",)),
    )(w, m, v, gparts)


def _my_coords():
    return lax.axis_index("x"), lax.axis_index("y"), lax.axis_index("c")


def _all_gather(x_shard, in_vmem, with_sum, name):
    m_per, n = x_shard.shape

    def body(x_ref, out_ref, *rest):
        if with_sum:
            sum_ref, send_sems, recv_sems, local_sem = rest
        else:
            send_sems, recv_sems, local_sem = rest
        x, y, c = _my_coords()
        me, sibling = (x, y, c), (x, y, 1 - c)
        chips = [(1 - x, y), (x, 1 - y), (1 - x, 1 - y)]

        def rows(px, py, pc):
            return out_ref.at[pl.ds((4 * px + 2 * py + pc) * m_per, m_per), :]

        def copy(k, block, to, src=None):
            return pltpu.make_async_remote_copy(
                src_ref=rows(*block) if src is None else src, dst_ref=rows(*block),
                send_sem=send_sems.at[k], recv_sem=recv_sems.at[k], device_id=to, device_id_type=MESH)

        mine = pltpu.make_async_copy(x_ref, rows(*me), local_sem)
        mine.start()
        first = [copy(0, me, sibling, src=x_ref)]
        first += [copy(1 + j, me, (*chip, c), src=x_ref) for j, chip in enumerate(chips)]
        for cp in first:
            cp.start()
        passed = [copy(4 + j, (*chip, c), sibling) for j, chip in enumerate(chips)]
        for j, chip in enumerate(chips):
            copy(1 + j, (*chip, c), me).wait_recv()
            passed[j].start()
        copy(0, sibling, me).wait_recv()
        for j, chip in enumerate(chips):
            copy(4 + j, (*chip, 1 - c), me).wait_recv()
        for cp in first + passed:
            cp.wait_send()
        mine.wait()
        if with_sum:
            acc = out_ref[pl.ds(0, m_per), :]
            for d in range(1, N_DEV):
                acc = acc + out_ref[pl.ds(d * m_per, m_per), :]
            sum_ref[...] = acc

    space = pltpu.VMEM if in_vmem else pl.ANY
    out_shape = [jax.ShapeDtypeStruct((N_DEV * m_per, n), x_shard.dtype)]
    out_specs = [pl.BlockSpec(memory_space=space)]
    if with_sum:
        out_shape.append(jax.ShapeDtypeStruct((m_per, n), x_shard.dtype))
        out_specs.append(pl.BlockSpec(memory_space=pltpu.VMEM))
    res = pl.pallas_call(
        body, name=name, out_shape=out_shape, in_specs=[pl.BlockSpec(memory_space=space)], out_specs=out_specs,
        scratch_shapes=[pltpu.SemaphoreType.DMA((7,)), pltpu.SemaphoreType.DMA((7,)), pltpu.SemaphoreType.DMA],
        compiler_params=pltpu.CompilerParams(vmem_limit_bytes=VMEM_LIMIT),
    )(x_shard)
    return res if with_sum else res[0]


def _all_to_all(send, name):
    _, r, c = send.shape

    def body(send_ref, recv_ref, send_sems, recv_sems, local_sem):
        x, y, cc = _my_coords()
        me = 4 * x + 2 * y + cc
        mine = pltpu.make_async_copy(send_ref.at[me], recv_ref.at[me], local_sem)
        mine.start()
        copies = []
        for k in range(1, N_DEV):
            px = 1 - x if k & 4 else x
            py = 1 - y if k & 2 else y
            pc = 1 - cc if k & 1 else cc
            peer = 4 * px + 2 * py + pc
            cp = pltpu.make_async_remote_copy(
                src_ref=send_ref.at[peer], dst_ref=recv_ref.at[me],
                send_sem=send_sems.at[k - 1], recv_sem=recv_sems.at[k - 1],
                device_id=(px, py, pc), device_id_type=MESH)
            cp.start()
            copies.append((cp, peer, (px, py, pc)))
        for k, (cp, peer, pid) in enumerate(copies):
            cp.wait_send()
            pltpu.make_async_remote_copy(
                src_ref=send_ref.at[me], dst_ref=recv_ref.at[peer],
                send_sem=send_sems.at[k], recv_sem=recv_sems.at[k], device_id=pid, device_id_type=MESH).wait_recv()
        mine.wait()

    return pl.pallas_call(
        body, name=name, out_shape=jax.ShapeDtypeStruct(send.shape, send.dtype),
        in_specs=[pl.BlockSpec(memory_space=pl.ANY)], out_specs=pl.BlockSpec(memory_space=pl.ANY),
        scratch_shapes=[pltpu.SemaphoreType.DMA((7,)), pltpu.SemaphoreType.DMA((7,)), pltpu.SemaphoreType.DMA],
    )(send)


def _row(v):
    return v.reshape(1, -1)


def _ffn_fwd(x, w_in, w_out, g_pre, g_post, m, res_w, tag):
    shift, scale, gate = m[0], m[1], m[2]
    mpre = _row(g_pre * (1.0 + scale))
    mpost = _row(res_w * gate * g_post)
    h = _rms_fwd(x, mpre, _row(shift), None, BF16, tag + "_pre")
    u = _matmul(h, w_in, out_dtype=BF16, name=tag + "_in")
    s = _swiglu_fwd(u, tag + "_act")
    y = _matmul(s, w_out, name=tag + "_out")
    x_new = _rms_fwd(y, mpost, jnp.zeros_like(mpost), x, F32, tag + "_post")
    return x_new, (x, h, u, s, y, mpre, mpost)


def _sub_bwd_post(dx_new, y, mpost, g_post, gate, res_w, tag):
    dy, c1, _ = _rms_bwd(dx_new, y, mpost, None, BF16, tag + "_post_bwd")
    c1 = c1[0]
    return dy, c1 * res_w * g_post, c1 * res_w * gate


def _sub_bwd_pre(dh, x, mpre, dx_new, g_pre, scale, tag):
    dx, c2, c3 = _rms_bwd(dh, x, mpre, dx_new, F32, tag + "_pre_bwd")
    c2, c3 = c2[0], c3[0]
    return dx, c3, c2 * g_pre, c2 * (1.0 + scale)


def _ffn_bwd(dx_new, saved, w_in, w_out, g_pre, g_post, m, res_w, tag):
    x, h, u, s, y, mpre, mpost = saved
    scale, gate = m[1], m[2]
    dy, dgate, dg_post = _sub_bwd_post(dx_new, y, mpost, g_post, gate, res_w, tag)
    ds = _matmul(dy, w_out, tb=True, out_dtype=BF16, name=tag + "_out_dx")
    dw_out = _matmul(s, dy, ta=True, out_dtype=BF16, name=tag + "_out_dw")
    du = _swiglu_bwd(u, ds, tag + "_act_bwd")
    dh = _matmul(du, w_in, tb=True, name=tag + "_in_dx")
    dw_in = _matmul(h, du, ta=True, out_dtype=BF16, name=tag + "_in_dw")
    dx, dshift, dscale, dg_pre = _sub_bwd_pre(dh, x, mpre, dx_new, g_pre, scale, tag)
    return dx, dw_in, dw_out, jnp.stack([dshift, dscale, dgate]), dg_pre, dg_post


def _slope_cols(gi):
    _, r = C_GROUPS[gi]
    sl = jnp.asarray(_alibi_slopes(C_HEADS)[gi * C_HPG:(gi + 1) * C_HPG], F32) * float(r)
    return jnp.tile(jnp.repeat(sl, HDIM), r).reshape(1, r * C_OUT)


def _group_view(a, gi):
    _, r = C_GROUPS[gi]
    t = a.shape[0]
    return a.reshape(t // r, r * a.shape[1])


def _mix_fwd(x, w, g_pre, g_post, m, lb, hn, tag):
    t, d = x.shape
    shift, scale, gate = m[0], m[1], m[2]
    mpre = _row(g_pre * (1.0 + scale))
    mpost = _row(gate * g_post)
    h = _rms_fwd(x, mpre, _row(shift), None, BF16, tag + "_pre")
    p = _matmul(h, w["w_in"], name=tag + "_in")
    hn2 = _row(jnp.tile(hn, 2))
    ya, oa, states = _hgrn_fwd(p, _row(lb), hn2, tag + "_hgrn")
    yb, sb_tot = _sb_fwd(p, tag + "_sb")
    og, lg, qkv = [], [], []
    for gi in range(len(C_GROUPS)):
        cs = slice(gi * C_OUT, (gi + 1) * C_OUT)
        q = _group_view(p[:, COL_CQ:COL_CK][:, cs], gi)
        k = _group_view(p[:, COL_CK:COL_CV][:, cs], gi)
        v = _group_view(p[:, COL_CV:COL_GATE][:, cs], gi)
        o, lse = _dil_fwd(q, k, v, _slope_cols(gi), tag + "_dil%d" % gi)
        og.append(o.reshape(t, C_OUT))
        lg.append(lse.reshape(t, C_OUT))
        qkv.append((q, k, v))
    yc, lse_c = _dil_merge(og, lg, tag + "_dil_merge")
    gl = p[:, COL_GATE:]
    ws = (w["w_branch_a"], w["w_branch_b"], w["w_branch_c"])
    merged = _gate_fwd((ya, yb, yc), gl, ws, tag + "_gate")
    y = _matmul(merged, w["w_out"], name=tag + "_out")
    x_new = _rms_fwd(y, mpost, jnp.zeros_like(mpost), x, F32, tag + "_post")
    return x_new, (x, h, p, hn2, ya, oa, states, yb, sb_tot, qkv, yc, lse_c, gl, merged, y, mpre, mpost)


def _mix_bwd(dx_new, saved, w, g_pre, g_post, m, lb, tag):
    x, h, p, hn2, ya, oa, states, yb, sb_tot, qkv, yc, lse_c, gl, merged, y, mpre, mpost = saved
    t = x.shape[0]
    scale, gate = m[1], m[2]
    dy, dgate, dg_post = _sub_bwd_post(dx_new, y, mpost, g_post, gate, 1.0, tag)
    dmerged = _matmul(dy, w["w_out"], tb=True, out_dtype=BF16, name=tag + "_out_dx")
    dw_out = _matmul(merged, dy, ta=True, out_dtype=BF16, name=tag + "_out_dw")
    ws = (w["w_branch_a"], w["w_branch_b"], w["w_branch_c"])
    dya, dyb, dyc, dgl, dwa, dwb, dwc = _gate_bwd(dmerged, (ya, yb, yc), gl, ws, tag + "_gate_bwd")
    dqa, dfa, dia, dga, dlb, dhn = _hgrn_bwd(p, _row(lb), hn2, oa, states, dya, tag + "_hgrn_bwd")
    dbq, dbk, dbv = _sb_bwd(p, sb_tot, dyb, tag + "_sb_bwd")
    dcq, dck, dcv = [], [], []
    for gi in range(len(C_GROUPS)):
        q, k, v = qkv[gi]
        dq, dk, dv = _dil_bwd(q, k, v, _group_view(dyc, gi), _group_view(yc, gi), _group_view(lse_c, gi),
                              _slope_cols(gi), tag + "_dil%d_bwd" % gi)
        dcq.append(dq.reshape(t, C_OUT))
        dck.append(dk.reshape(t, C_OUT))
        dcv.append(dv.reshape(t, C_OUT))
    dp = jnp.concatenate([dqa, dfa, dia, dga, dbq, dbk, dbv] + dcq + dck + dcv + [dgl], axis=1)
    dh = _matmul(dp, w["w_in"], tb=True, name=tag + "_in_dx")
    dw_in = _matmul(h, dp, ta=True, out_dtype=BF16, name=tag + "_in_dw")
    dx, dshift, dscale, dg_pre = _sub_bwd_pre(dh, x, mpre, dx_new, g_pre, scale, tag)
    dhn_v = jnp.sum(dhn, axis=(0, 1))
    dhn_v = dhn_v[:A_VDIM] + dhn_v[A_VDIM:]
    dws = dict(w_in=dw_in, w_out=dw_out, w_branch_a=dwa.astype(BF16), w_branch_b=dwb.astype(BF16),
               w_branch_c=dwc.astype(BF16))
    return dx, dws, jnp.stack([dshift, dscale, dgate]), dg_pre, dg_post, dlb[0], dhn_v


def _local_step(x, target, mod, norm_g, lb_all, hnorm, wts):
    depth = mod.shape[0]
    d = x.shape[1]
    saved = []
    for l in range(depth):
        wl = {k: v[l] for k, v in wts.items()}
        x, s0 = _ffn_fwd(x, wl["ffn1_w_in"], wl["ffn1_w_out"], norm_g[l, 0], norm_g[l, 1], mod[l, 0], 0.5, "ffn1")
        x, s1 = _mix_fwd(x, wl, norm_g[l, 2], norm_g[l, 3], mod[l, 1], lb_all[l], hnorm[l], "mix")
        x, s2 = _ffn_fwd(x, wl["ffn2_w_in"], wl["ffn2_w_out"], norm_g[l, 4], norm_g[l, 5], mod[l, 2], 0.5, "ffn2")
        saved.append((s0, s1, s2))
    dx, sq = _loss_head(x, target, "loss_head")
    loss = 0.5 * jnp.sum(sq) / d
    dmod, dng, dlb, dhn = [], [], [], []
    dws = {k: [] for k in wts}
    for l in reversed(range(depth)):
        wl = {k: v[l] for k, v in wts.items()}
        s0, s1, s2 = saved[l]
        dx, dwi2, dwo2, dm2, dgp2, dgq2 = _ffn_bwd(dx, s2, wl["ffn2_w_in"], wl["ffn2_w_out"], norm_g[l, 4],
                                                   norm_g[l, 5], mod[l, 2], 0.5, "ffn2")
        dx, dwm, dm1, dgp1, dgq1, dlb_l, dhn_l = _mix_bwd(dx, s1, wl, norm_g[l, 2], norm_g[l, 3], mod[l, 1],
                                                          lb_all[l], "mix")
        dx, dwi1, dwo1, dm0, dgp0, dgq0 = _ffn_bwd(dx, s0, wl["ffn1_w_in"], wl["ffn1_w_out"], norm_g[l, 0],
                                                   norm_g[l, 1], mod[l, 0], 0.5, "ffn1")
        dmod.append(jnp.stack([dm0, dm1, dm2]))
        dng.append(jnp.stack([dgp0, dgq0, dgp1, dgq1, dgp2, dgq2]))
        dlb.append(dlb_l)
        dhn.append(dhn_l)
        dws["ffn1_w_in"].append(dwi1)
        dws["ffn1_w_out"].append(dwo1)
        dws["ffn2_w_in"].append(dwi2)
        dws["ffn2_w_out"].append(dwo2)
        for k, g in dwm.items():
            dws[k].append(g)
    rev = lambda lst: jnp.stack(lst[::-1])
    return (loss, dx, rev(dmod), rev(dng), rev(dlb), rev(dhn), {k: rev(v) for k, v in dws.items()})


def _lb_all(logits):
    lb_p = jax.nn.softmax(logits.astype(F32), axis=0)
    return jnp.cumsum(lb_p, axis=0) - lb_p[0:1]


def _pad_rows(a, rows):
    return jnp.pad(a, ((0, rows - a.shape[0]), (0, 0)))


def _gather_weights(shards):
    flat = jnp.concatenate([shards[k].astype(BF16).reshape(-1, 1024) for k in BIG_WEIGHTS], axis=0)
    got = _all_gather(flat, False, False, "weights_all_gather").reshape(N_DEV, flat.shape[0], 1024)
    out, off = {}, 0
    for k in BIG_WEIGHTS:
        depth, r, c = shards[k].shape
        nrow = depth * r * c // 1024
        blk = got[:, off:off + nrow].reshape(N_DEV, depth, r, c)
        off += nrow
        if k in ROW_SHARDED:
            out[k] = blk.transpose(1, 0, 2, 3).reshape(depth, N_DEV * r, c)
        else:
            out[k] = blk.transpose(1, 2, 0, 3).reshape(depth, r, N_DEV * c)
    return out


def _scatter_grads(dws, shard_shapes):
    parts = []
    for k in BIG_WEIGHTS:
        depth, r, c = shard_shapes[k]
        g = dws[k]
        if k in ROW_SHARDED:
            g = g.reshape(depth, N_DEV, r, c).transpose(1, 0, 2, 3)
        else:
            g = g.reshape(depth, r, N_DEV, c).transpose(2, 0, 1, 3)
        parts.append(g.reshape(N_DEV, -1, 1024))
    send = jnp.concatenate(parts, axis=1)
    recv = _all_to_all(send, "grads_all_to_all")
    out, off = {}, 0
    for k in BIG_WEIGHTS:
        depth, r, c = shard_shapes[k]
        nrow = depth * r * c // 1024
        out[k] = recv[:, off:off + nrow].reshape(N_DEV, depth * r, c)
        off += nrow
    return out


def kernel(x, c, w_ada, b_ada, norm_g, ffn1_w_in, ffn1_w_out, w_in, hgrn_lb_logits, hgrn_norm_g, w_branch_a, w_branch_b, w_branch_c, w_out, ffn2_w_in, ffn2_w_out, loss_target, m_w_ada, m_b_ada, m_norm_g, m_ffn1_w_in, m_ffn1_w_out, m_w_in, m_hgrn_lb_logits, m_hgrn_norm_g, m_w_branch_a, m_w_branch_b, m_w_branch_c, m_w_out, m_ffn2_w_in, m_ffn2_w_out, v_w_ada, v_b_ada, v_norm_g, v_ffn1_w_in, v_ffn1_w_out, v_w_in, v_hgrn_lb_logits, v_hgrn_norm_g, v_w_branch_a, v_w_branch_b, v_w_branch_c, v_w_out, v_ffn2_w_in, v_ffn2_w_out):
    weights = dict(w_ada=w_ada, b_ada=b_ada, norm_g=norm_g, ffn1_w_in=ffn1_w_in, ffn1_w_out=ffn1_w_out, w_in=w_in,
                   hgrn_lb_logits=hgrn_lb_logits, hgrn_norm_g=hgrn_norm_g, w_branch_a=w_branch_a,
                   w_branch_b=w_branch_b, w_branch_c=w_branch_c, w_out=w_out, ffn2_w_in=ffn2_w_in,
                   ffn2_w_out=ffn2_w_out)
    mom1 = dict(w_ada=m_w_ada, b_ada=m_b_ada, norm_g=m_norm_g, ffn1_w_in=m_ffn1_w_in, ffn1_w_out=m_ffn1_w_out,
                w_in=m_w_in, hgrn_lb_logits=m_hgrn_lb_logits, hgrn_norm_g=m_hgrn_norm_g, w_branch_a=m_w_branch_a,
                w_branch_b=m_w_branch_b, w_branch_c=m_w_branch_c, w_out=m_w_out, ffn2_w_in=m_ffn2_w_in,
                ffn2_w_out=m_ffn2_w_out)
    mom2 = dict(w_ada=v_w_ada, b_ada=v_b_ada, norm_g=v_norm_g, ffn1_w_in=v_ffn1_w_in, ffn1_w_out=v_ffn1_w_out,
                w_in=v_w_in, hgrn_lb_logits=v_hgrn_lb_logits, hgrn_norm_g=v_hgrn_norm_g, w_branch_a=v_w_branch_a,
                w_branch_b=v_w_branch_b, w_branch_c=v_w_branch_c, w_out=v_w_out, ffn2_w_in=v_ffn2_w_in,
                ffn2_w_out=v_ffn2_w_out)
    order = list(weights)
    depth, d, ada_cols = w_ada.shape
    nd = d // LANES
    xi, yi, ci = _my_coords()
    me = 4 * xi + 2 * yi + ci

    small = jnp.concatenate([c.reshape(nd, LANES), norm_g.reshape(depth * 6, LANES)], axis=0)
    g1 = _all_gather(small, True, False, "small_all_gather").reshape(N_DEV, small.shape[0], LANES)
    c_act = _silu(g1[:, :nd].reshape(N_DEV, d))
    norm_full = g1[:, nd:].reshape(N_DEV, depth, 6, LANES).transpose(1, 2, 0, 3).reshape(depth, 6, d)

    c_pad = _pad_rows(c_act, 16)
    mod_sh = jnp.stack([_matmul(c_pad, w_ada[l], name="ada_mod")[:N_DEV]
                        + lax.dynamic_slice_in_dim(b_ada[l], me * ada_cols, ada_cols)[None]
                        for l in range(depth)])
    g2 = _all_gather(mod_sh.reshape(-1, LANES), True, False, "mod_all_gather")
    g2 = g2.reshape(N_DEV, depth, N_DEV, ada_cols)
    mod = lax.dynamic_index_in_dim(g2, me, axis=2, keepdims=False)
    mod = mod.transpose(1, 0, 2).reshape(depth, 3, 3, d)

    big = {k: weights[k] for k in BIG_WEIGHTS}
    wts = _gather_weights(big)
    lb_all, lb_vjp = jax.vjp(_lb_all, hgrn_lb_logits)

    loss, dx, dmod, dng, dlb, dhn, dws = _local_step(x[0], loss_target[0], mod, norm_full, lb_all, hgrn_norm_g, wts)
    loss = lax.psum(loss, ("x", "y", "c"))

    dhn_pad = jnp.pad(dhn.reshape(-1), (0, 8 * LANES - dhn.size))
    pieces = [dmod.reshape(-1), dng.reshape(-1), dlb.reshape(-1), dhn_pad]
    sizes = [p_.size for p_ in pieces]
    smallg = jnp.concatenate(pieces).reshape(-1, LANES)
    g3, gsum = _all_gather(smallg, True, True, "small_grads_all_gather")
    g3 = g3.reshape(N_DEV, -1)
    gsum = gsum.reshape(-1)
    dmod_all = g3[:, :sizes[0]].reshape(N_DEV, depth, 9 * d)
    o1 = sizes[0]
    grads = {}
    grads["b_ada"] = gsum[:o1].reshape(depth, 9 * d)
    dng_sum = gsum[o1:o1 + sizes[1]].reshape(depth, 6, nd, LANES)
    grads["norm_g"] = lax.dynamic_index_in_dim(dng_sum, me, axis=2, keepdims=False)
    o2 = o1 + sizes[1]
    dlb_sum = gsum[o2:o2 + sizes[2]].reshape(depth, A_QK)
    grads["hgrn_lb_logits"] = lb_vjp(dlb_sum)[0]
    o3 = o2 + sizes[2]
    grads["hgrn_norm_g"] = gsum[o3:o3 + dhn.size].reshape(depth, A_VDIM)
    dmod_mine = lax.dynamic_slice_in_dim(dmod_all, me * ada_cols, ada_cols, axis=2)
    grads["w_ada"] = jnp.stack([_matmul(c_pad, _pad_rows(dmod_mine[:, l], 16), ta=True, name="ada_dw")
                                for l in range(depth)])

    shard_shapes = {k: weights[k].shape for k in BIG_WEIGHTS}
    gparts = _scatter_grads(dws, shard_shapes)

    outs = {}
    for k in order:
        w = weights[k]
        w2 = w.reshape(-1, w.shape[-1])
        gp = gparts[k] if k in gparts else grads[k].reshape((1,) + w2.shape)
        res = _adamw(w2, mom1[k].reshape(w2.shape), mom2[k].reshape(w2.shape), gp, "adamw")
        outs[k] = [r.reshape(w.shape) for r in res]
    return (loss, dx[None], *[outs[k][0] for k in order], *[outs[k][1] for k in order],
            *[outs[k][2] for k in order], *[outs[k][3] for k in order])
```

```python
import functools
import math

import jax
import jax.numpy as jnp
from jax import lax
from jax.experimental import pallas as pl
from jax.experimental.pallas import tpu as pltpu

F32 = jnp.float32
BF16 = jnp.bfloat16

A_HEADS, A_KDIM, A_VDIM, A_CHUNK = 6, 128, 64, 64
B_HEADS, HDIM = 6, 64
C_GROUPS = ((128, 1), (512, 4), (2048, 16))
C_HPG = 4
C_HEADS = C_HPG * len(C_GROUPS)
N_BRANCH = 3
EPS = 1e-6
NEG_BIG = -1e30
TINY = 1e-30
A_QK = A_HEADS * A_KDIM
A_V = A_HEADS * A_VDIM
B_W = B_HEADS * HDIM
C_W = C_HEADS * HDIM
C_OUT = C_HPG * HDIM
COL_AQ, COL_AF, COL_AI, COL_AG = 0, A_QK, 2 * A_QK, 2 * A_QK + A_V
COL_BQ = 2 * A_QK + 2 * A_V
COL_BK, COL_BV = COL_BQ + B_W, COL_BQ + 2 * B_W
COL_CQ = COL_BQ + 3 * B_W
COL_CK, COL_CV = COL_CQ + C_W, COL_CQ + 2 * C_W
COL_GATE = COL_CQ + 3 * C_W

ADAM_LR, ADAM_B1, ADAM_B2, ADAM_EPS, ADAM_WD, ADAM_STEP = 0.001, 0.9, 0.999, 1e-08, 0.01, 10

N_DEV = 8
LANES = 128
VMEM_LIMIT = 48 * 1024 * 1024
MATMUL_VMEM_BUDGET = 28 * 1024 * 1024
SUB = 16
EXP_CLAMP = 80.0
MESH = pl.DeviceIdType.MESH

BIG_WEIGHTS = ("ffn1_w_in", "ffn1_w_out", "w_in", "w_branch_a", "w_branch_b", "w_branch_c", "w_out",
               "ffn2_w_in", "ffn2_w_out")
ROW_SHARDED = ("ffn1_w_out", "w_out", "ffn2_w_out")


def _cparams(sem):
    return pltpu.CompilerParams(dimension_semantics=sem, vmem_limit_bytes=VMEM_LIMIT)


def _tile(n, cap):
    best, t = None, LANES
    while t <= min(n, cap):
        if n % t == 0:
            best = t
        t += LANES
    return best or n


def _rows(t, cap=256):
    r = cap
    while t % r:
        r //= 2
    return r


def _divisors(n):
    return [t for t in range(LANES, n + 1, LANES) if n % t == 0] or [n]


def _matmul_tiles(m, n, k, a_size, b_size, o_size):
    best, best_key = None, None
    for tm in _divisors(m):
        for tn in _divisors(n):
            for tk in _divisors(k):
                if tm > 1024 or tn > 3072 or tk > 4096:
                    continue
                cast = (tm * tk * 2 if a_size > 2 else 0) + (tk * tn * 2 if b_size > 2 else 0)
                need = 2 * (tm * tk * a_size + tk * tn * b_size + tm * tn * o_size) + 2 * tm * tn * 4 + cast
                if need > MATMUL_VMEM_BUDGET:
                    continue
                key = (tm * tn * tk, tk)
                if best_key is None or key > best_key:
                    best, best_key = (tm, tn, tk), key
    return best


def _dot(a, b):
    return jnp.dot(a, b, preferred_element_type=F32)


def _dot_nt(a, b):
    return lax.dot_general(a, b, (((1,), (1,)), ((), ())), preferred_element_type=F32)


def _dot_tn(a, b):
    return lax.dot_general(a, b, (((0,), (0,)), ((), ())), preferred_element_type=F32)


def _split3(x):
    h = x.astype(BF16)
    r = x - h.astype(F32)
    m = r.astype(BF16)
    lo = (r - m.astype(F32)).astype(BF16)
    return h, m, lo


def _ones_left(mat01, x):
    h, m, lo = _split3(x)
    return _dot(mat01, h) + _dot(mat01, m) + _dot(mat01, lo)


def _silu(x):
    return x * jax.nn.sigmoid(x)


def _dsilu(x):
    s = jax.nn.sigmoid(x)
    return s * (1.0 + x * (1.0 - s))


def _matmul(a, b, *, ta=False, tb=False, out_dtype=F32, name):
    if ta:
        kdim, m = a.shape
    else:
        m, kdim = a.shape
    n = b.shape[0] if tb else b.shape[1]
    tm, tn, tk = _matmul_tiles(m, n, kdim, a.dtype.itemsize, b.dtype.itemsize, jnp.dtype(out_dtype).itemsize)
    nk = kdim // tk
    ni, nj = m // tm, n // tn
    a_bytes, b_bytes = m * kdim * a.dtype.itemsize, kdim * n * b.dtype.itemsize
    j_outer = nk == 1 and (b_bytes + a_bytes * nj) < (a_bytes + b_bytes * ni)
    dims = (((0 if ta else 1,), (1 if tb else 0,)), ((), ()))

    def body(a_ref, b_ref, o_ref, *scratch):
        p = lax.dot_general(a_ref[...].astype(BF16), b_ref[...].astype(BF16), dims, preferred_element_type=F32)
        if nk == 1:
            o_ref[...] = p.astype(o_ref.dtype)
            return
        acc = scratch[0]
        k = pl.program_id(2)

        @pl.when(k == 0)
        def _():
            acc[...] = p

        @pl.when(k > 0)
        def _():
            acc[...] += p

        @pl.when(k == nk - 1)
        def _():
            o_ref[...] = acc[...].astype(o_ref.dtype)

    def spec(shape, pick):
        if j_outer:
            return pl.BlockSpec(shape, lambda j, i, k: pick(i, j, k))
        return pl.BlockSpec(shape, lambda i, j, k: pick(i, j, k))

    a_spec = spec((tk, tm), lambda i, j, k: (k, i)) if ta else spec((tm, tk), lambda i, j, k: (i, k))
    b_spec = spec((tn, tk), lambda i, j, k: (j, k)) if tb else spec((tk, tn), lambda i, j, k: (k, j))
    return pl.pallas_call(
        body, name=name, grid=(nj, ni, nk) if j_outer else (ni, nj, nk), in_specs=[a_spec, b_spec],
        out_specs=spec((tm, tn), lambda i, j, k: (i, j)),
        out_shape=jax.ShapeDtypeStruct((m, n), out_dtype),
        scratch_shapes=[pltpu.VMEM((tm, tn), F32)] if nk > 1 else [],
        compiler_params=_cparams(("parallel", "parallel", "arbitrary")),
    )(a, b)


def _rms_fwd(z, mcol, acol, res, out_dtype, name):
    t, d = z.shape
    tr = _rows(t)
    has_res = res is not None

    def body(*refs):
        if has_res:
            z_ref, m_ref, a_ref, r_ref, o_ref = refs
        else:
            z_ref, m_ref, a_ref, o_ref = refs
        zf = z_ref[...]
        r = lax.rsqrt(jnp.mean(zf * zf, axis=-1, keepdims=True) + EPS)
        y = zf * r * m_ref[...] + a_ref[...]
        if has_res:
            y = r_ref[...] + y
        o_ref[...] = y.astype(o_ref.dtype)

    row = pl.BlockSpec((tr, d), lambda i: (i, 0))
    col = pl.BlockSpec((1, d), lambda i: (0, 0))
    ins = [z, mcol, acol] + ([res] if has_res else [])
    return pl.pallas_call(
        body, name=name, grid=(t // tr,), in_specs=[row, col, col] + ([row] if has_res else []),
        out_specs=row, out_shape=jax.ShapeDtypeStruct((t, d), out_dtype),
        compiler_params=_cparams(("parallel",)),
    )(*ins)


def _rms_bwd(d_out, z, mcol, dres, out_dtype, name):
    t, d = z.shape
    tr = _rows(t)
    has_res = dres is not None

    def body(*refs):
        if has_res:
            d_ref, z_ref, m_ref, r_ref, o_ref, s1_ref, s2_ref = refs
        else:
            d_ref, z_ref, m_ref, o_ref, s1_ref, s2_ref = refs
        i = pl.program_id(0)
        zf = z_ref[...]
        r = lax.rsqrt(jnp.mean(zf * zf, axis=-1, keepdims=True) + EPS)
        zh = zf * r
        df = d_ref[...].astype(F32)
        dzh = df * m_ref[...]
        dz = r * (dzh - zh * jnp.mean(dzh * zh, axis=-1, keepdims=True))
        if has_res:
            dz = dz + r_ref[...]
        o_ref[...] = dz.astype(o_ref.dtype)
        s1 = jnp.sum(df * zh, axis=0, keepdims=True)
        s2 = jnp.sum(df, axis=0, keepdims=True)

        @pl.when(i == 0)
        def _():
            s1_ref[...] = s1
            s2_ref[...] = s2

        @pl.when(i > 0)
        def _():
            s1_ref[...] += s1
            s2_ref[...] += s2

    row = pl.BlockSpec((tr, d), lambda i: (i, 0))
    col = pl.BlockSpec((1, d), lambda i: (0, 0))
    ins = [d_out, z, mcol] + ([dres] if has_res else [])
    return pl.pallas_call(
        body, name=name, grid=(t // tr,), in_specs=[row, row, col] + ([row] if has_res else []),
        out_specs=[row, col, col],
        out_shape=[jax.ShapeDtypeStruct((t, d), out_dtype), jax.ShapeDtypeStruct((1, d), F32),
                   jax.ShapeDtypeStruct((1, d), F32)],
        compiler_params=_cparams(("arbitrary",)),
    )(*ins)


def _swiglu_fwd(u, name):
    t, f2 = u.shape
    f = f2 // 2
    tr = _rows(t)

    def body(u_ref, s_ref):
        a = u_ref[:, :f].astype(F32)
        b = u_ref[:, f:].astype(F32)
        s_ref[...] = (_silu(a) * b).astype(s_ref.dtype)

    return pl.pallas_call(
        body, name=name, grid=(t // tr,), in_specs=[pl.BlockSpec((tr, f2), lambda i: (i, 0))],
        out_specs=pl.BlockSpec((tr, f), lambda i: (i, 0)), out_shape=jax.ShapeDtypeStruct((t, f), BF16),
        compiler_params=_cparams(("parallel",)),
    )(u)


def _swiglu_bwd(u, ds, name):
    t, f2 = u.shape
    f = f2 // 2
    tr = _rows(t)

    def body(u_ref, ds_ref, du_ref):
        a = u_ref[:, :f].astype(F32)
        b = u_ref[:, f:].astype(F32)
        g = ds_ref[...].astype(F32)
        du_ref[:, :f] = (g * b * _dsilu(a)).astype(du_ref.dtype)
        du_ref[:, f:] = (g * _silu(a)).astype(du_ref.dtype)

    return pl.pallas_call(
        body, name=name, grid=(t // tr,),
        in_specs=[pl.BlockSpec((tr, f2), lambda i: (i, 0)), pl.BlockSpec((tr, f), lambda i: (i, 0))],
        out_specs=pl.BlockSpec((tr, f2), lambda i: (i, 0)), out_shape=jax.ShapeDtypeStruct((t, f2), BF16),
        compiler_params=_cparams(("parallel",)),
    )(u, ds)


def _loss_head(y, target, name):
    t, d = y.shape
    tr = _rows(t)

    def body(y_ref, t_ref, dy_ref, sq_ref):
        i = pl.program_id(0)
        e = y_ref[...] - t_ref[...]
        dy_ref[...] = e * (1.0 / d)
        s = jnp.sum(e * e, axis=0, keepdims=True)

        @pl.when(i == 0)
        def _():
            sq_ref[...] = s

        @pl.when(i > 0)
        def _():
            sq_ref[...] += s

    row = pl.BlockSpec((tr, d), lambda i: (i, 0))
    col = pl.BlockSpec((1, d), lambda i: (0, 0))
    return pl.pallas_call(
        body, name=name, grid=(t // tr,), in_specs=[row, row], out_specs=[row, col],
        out_shape=[jax.ShapeDtypeStruct((t, d), F32), jax.ShapeDtypeStruct((1, d), F32)],
        compiler_params=_cparams(("arbitrary",)),
    )(y, target)


def _hgrn_consts():
    c = A_CHUNK
    shift = SUB.bit_length() - 1
    r = lax.broadcasted_iota(jnp.int32, (c, c), 0)
    s = lax.broadcasted_iota(jnp.int32, (c, c), 1)
    sub_r = lax.shift_right_logical(r, shift)
    incl = s <= r
    masks = [jnp.logical_and(sub_r == i, incl) for i in range(c // SUB)]
    rev_incl = jnp.where(s >= r, 1.0, 0.0).astype(BF16)
    r2 = lax.broadcasted_iota(jnp.int32, (2 * c + 8, c), 0)
    s2 = lax.broadcasted_iota(jnp.int32, (2 * c + 8, c), 1)
    sub_start = lax.shift_left(lax.shift_right_logical(r2 - c, shift), shift)
    running = jnp.where(s2 <= r2, 1.0, 0.0)
    before = jnp.where(s2 < sub_start, 1.0, 0.0)
    stack = jnp.where(r2 < c, running, jnp.where(r2 < 2 * c, before, 1.0)).astype(BF16)
    return stack, masks, incl, rev_incl


def _hgrn_chunk(q_raw, f_raw, lbv, stack):
    c = A_CHUNK
    sg = jax.nn.sigmoid(f_raw)
    sgn = jax.nn.sigmoid(-f_raw)
    f = lbv + (1.0 - lbv) * sg
    logf = jnp.log(jnp.maximum(f, TINY))
    k = (1.0 - lbv) * sgn
    q = _silu(q_raw)
    bb = _ones_left(stack, logf)
    b = bb[:c]
    bsrow = bb[c:2 * c]
    b_end = bb[2 * c:2 * c + 1]
    e_sub = jnp.exp(b - bsrow)
    e_b = jnp.exp(b)
    e_end = jnp.exp(b_end - b)
    qs = q * e_sub
    q_in = q * e_b
    kend = k * e_end
    kfac = [jnp.exp(jnp.minimum(bsrow[i * SUB:i * SUB + 1] - b, EXP_CLAMP)) for i in range(c // SUB)]
    return dict(sg=sg, sgn=sgn, f=f, k=k, q=q, b=b, b_end=b_end, e_sub=e_sub, e_b=e_b, e_end=e_end,
                qs=qs, q_in=q_in, kend=kend, kfac=kfac)


def _hgrn_scores(ch, masks):
    qs_b = ch["qs"].astype(BF16)
    a = None
    for i, mk in enumerate(masks):
        ki = (ch["k"] * ch["kfac"][i]).astype(BF16)
        part = jnp.where(mk, _dot_nt(qs_b, ki), 0.0)
        a = part if a is None else a + part
    return a


def _hgrn_fwd(p, lb, hn2, name):
    t = p.shape[0]
    tb = _rows(t)
    nt = t // tb
    nc = tb // A_CHUNK
    c = A_CHUNK

    def body(q_ref, f_ref, i_ref, g_ref, lb_ref, hn_ref, y_ref, o_ref, st_ref, s_scr):
        j = pl.program_id(1)

        @pl.when(j == 0)
        def _():
            s_scr[...] = jnp.zeros_like(s_scr)

        stack, masks, _, _ = _hgrn_consts()
        for hh in range(2):
            lsl = slice(A_KDIM * hh, A_KDIM * (hh + 1))
            hsl = slice(A_VDIM * hh, A_VDIM * (hh + 1))
            lbv = lb_ref[:, lsl]

            def chunk(ci, carry, lsl=lsl, hsl=hsl, lbv=lbv, hh=hh):
                r0 = pl.multiple_of(ci * c, c)
                ch = _hgrn_chunk(q_ref[pl.ds(r0, c), lsl], f_ref[pl.ds(r0, c), lsl], lbv, stack)
                v = i_ref[pl.ds(r0, c), hsl].astype(BF16)
                st = s_scr[hh]
                st_ref[hh, ci] = st
                a = _hgrn_scores(ch, masks)
                o = _dot_nt(ch["q_in"].astype(BF16), st.astype(BF16)) + _dot(a.astype(BF16), v)
                s_scr[hh] = st * jnp.exp(ch["b_end"]) + _dot_tn(v, ch["kend"].astype(BF16))
                o_ref[pl.ds(r0, c), hsl] = o
                return carry

            lax.fori_loop(0, nc, chunk, 0)
        for hh in range(2):
            hsl = slice(A_VDIM * hh, A_VDIM * (hh + 1))
            o = o_ref[:, hsl]
            r = lax.rsqrt(jnp.mean(o * o, axis=-1, keepdims=True) + EPS)
            y_ref[:, hsl] = (o * r * hn_ref[:, hsl] * _silu(g_ref[:, hsl])).astype(y_ref.dtype)

    w2 = 2 * A_KDIM
    return pl.pallas_call(
        body, name=name, grid=(A_HEADS // 2, nt),
        in_specs=[pl.BlockSpec((tb, w2), lambda h, j: (j, COL_AQ // w2 + h)),
                  pl.BlockSpec((tb, w2), lambda h, j: (j, COL_AF // w2 + h)),
                  pl.BlockSpec((tb, LANES), lambda h, j: (j, COL_AI // LANES + h)),
                  pl.BlockSpec((tb, LANES), lambda h, j: (j, COL_AG // LANES + h)),
                  pl.BlockSpec((1, w2), lambda h, j: (0, h)),
                  pl.BlockSpec((1, LANES), lambda h, j: (0, 0))],
        out_specs=[pl.BlockSpec((tb, LANES), lambda h, j: (j, h)),
                   pl.BlockSpec((tb, LANES), lambda h, j: (j, h)),
                   pl.BlockSpec((2, nc, A_VDIM, A_KDIM), lambda h, j: (h, j, 0, 0))],
        out_shape=[jax.ShapeDtypeStruct((t, A_V), BF16), jax.ShapeDtypeStruct((t, A_V), F32),
                   jax.ShapeDtypeStruct((A_HEADS, t // c, A_VDIM, A_KDIM), F32)],
        scratch_shapes=[pltpu.VMEM((2, A_VDIM, A_KDIM), F32)],
        compiler_params=_cparams(("parallel", "arbitrary")),
    )(p, p, p, p, lb, hn2)


def _hgrn_bwd(p, lb, hn2, o_raw, states, dya, name):
    t = p.shape[0]
    tb = _rows(t)
    nt = t // tb
    nc = tb // A_CHUNK
    c = A_CHUNK

    def body(q_ref, f_ref, i_ref, g_ref, lb_ref, hn_ref, o_ref, st_ref, dy_ref,
             dq_ref, df_ref, di_ref, dg_ref, dlb_ref, dhn_ref, ds_scr, do_scr):
        j = pl.program_id(1)

        @pl.when(j == 0)
        def _():
            ds_scr[...] = jnp.zeros_like(ds_scr)
            dlb_ref[...] = jnp.zeros_like(dlb_ref)
            dhn_ref[...] = jnp.zeros_like(dhn_ref)

        stack, masks, incl, rev_incl = _hgrn_consts()
        for hh in range(2):
            hsl = slice(A_VDIM * hh, A_VDIM * (hh + 1))
            o = o_ref[:, hsl]
            g = g_ref[:, hsl]
            dy = dy_ref[:, hsl].astype(F32)
            hn = hn_ref[:, hsl]
            r = lax.rsqrt(jnp.mean(o * o, axis=-1, keepdims=True) + EPS)
            oh = o * r
            sgate = _silu(g)
            dg_ref[:, hsl] = dy * oh * hn * _dsilu(g)
            dhn_ref[0, :, hsl] += jnp.sum(dy * oh * sgate, axis=0, keepdims=True)
            doh = dy * hn * sgate
            do_scr[:, hsl] = r * (doh - oh * jnp.mean(doh * oh, axis=-1, keepdims=True))

        for hh in range(2):
            lsl = slice(A_KDIM * hh, A_KDIM * (hh + 1))
            hsl = slice(A_VDIM * hh, A_VDIM * (hh + 1))
            lbv = lb_ref[:, lsl]

            def chunk(it, carry, lsl=lsl, hsl=hsl, lbv=lbv, hh=hh):
                ci = nc - 1 - it
                r0 = pl.multiple_of(ci * c, c)
                q_raw = q_ref[pl.ds(r0, c), lsl]
                f_raw = f_ref[pl.ds(r0, c), lsl]
                ch = _hgrn_chunk(q_raw, f_raw, lbv, stack)
                v = i_ref[pl.ds(r0, c), hsl].astype(BF16)
                do = do_scr[pl.ds(r0, c), hsl]
                do_b = do.astype(BF16)
                st = st_ref[hh, ci]
                st_b = st.astype(BF16)
                dst = ds_scr[hh]
                dst_b = dst.astype(BF16)
                qs_b = ch["qs"].astype(BF16)
                kend_b = ch["kend"].astype(BF16)
                a = _hgrn_scores(ch, masks)
                da = jnp.where(incl, _dot_nt(do_b, v), 0.0)
                dv = _dot_tn(a.astype(BF16), do_b) + _dot_nt(kend_b, dst_b)
                dq_i = None
                dk_i = None
                kdk_i = None
                for i, mk in enumerate(masks):
                    dam = jnp.where(mk, da, 0.0).astype(BF16)
                    ki = (ch["k"] * ch["kfac"][i]).astype(BF16)
                    pq = _dot(dam, ki)
                    pk = _dot_tn(dam, qs_b)
                    dq_i = pq if dq_i is None else dq_i + pq
                    dk_i = ch["kfac"][i] * pk if dk_i is None else dk_i + ch["kfac"][i] * pk
                    kdk_i = ki.astype(F32) * pk if kdk_i is None else kdk_i + ki.astype(F32) * pk
                dq_x = _dot(do_b, st_b)
                dk_x = _dot(v, dst_b)
                dq = ch["e_sub"] * dq_i + ch["e_b"] * dq_x
                dk = dk_i + ch["e_end"] * dk_x
                ds_scr[hh] = dst * jnp.exp(ch["b_end"]) + _dot_tn(do_b, ch["q_in"].astype(BF16))
                kx = ch["kend"] * dk_x
                db = (qs_b.astype(F32) * dq_i + ch["q_in"] * dq_x) - (kdk_i + kx)
                later = (jnp.exp(ch["b_end"]) * jnp.sum(dst * st, axis=0, keepdims=True)
                         + jnp.sum(kx, axis=0, keepdims=True))
                dlogf = later + _ones_left(rev_incl, db)
                dfv = jnp.where(ch["f"] > TINY, dlogf / ch["f"], 0.0)
                dq_ref[pl.ds(r0, c), lsl] = dq * _dsilu(q_raw)
                df_ref[pl.ds(r0, c), lsl] = (1.0 - lbv) * ch["sg"] * ch["sgn"] * (dfv - dk)
                dlb_ref[:, lsl] += jnp.sum(dfv * (1.0 - ch["sg"]) - dk * ch["sgn"], axis=0, keepdims=True)
                di_ref[pl.ds(r0, c), hsl] = dv
                return carry

            lax.fori_loop(0, nc, chunk, 0)

    w2 = 2 * A_KDIM
    rev = lambda j: nt - 1 - j
    return pl.pallas_call(
        body, name=name, grid=(A_HEADS // 2, nt),
        in_specs=[pl.BlockSpec((tb, w2), lambda h, j: (rev(j), COL_AQ // w2 + h)),
                  pl.BlockSpec((tb, w2), lambda h, j: (rev(j), COL_AF // w2 + h)),
                  pl.BlockSpec((tb, LANES), lambda h, j: (rev(j), COL_AI // LANES + h)),
                  pl.BlockSpec((tb, LANES), lambda h, j: (rev(j), COL_AG // LANES + h)),
                  pl.BlockSpec((1, w2), lambda h, j: (0, h)),
                  pl.BlockSpec((1, LANES), lambda h, j: (0, 0)),
                  pl.BlockSpec((tb, LANES), lambda h, j: (rev(j), h)),
                  pl.BlockSpec((2, nc, A_VDIM, A_KDIM), lambda h, j: (h, rev(j), 0, 0)),
                  pl.BlockSpec((tb, LANES), lambda h, j: (rev(j), h))],
        out_specs=[pl.BlockSpec((tb, w2), lambda h, j: (rev(j), h)),
                   pl.BlockSpec((tb, w2), lambda h, j: (rev(j), h)),
                   pl.BlockSpec((tb, LANES), lambda h, j: (rev(j), h)),
                   pl.BlockSpec((tb, LANES), lambda h, j: (rev(j), h)),
                   pl.BlockSpec((1, w2), lambda h, j: (0, h)),
                   pl.BlockSpec((1, 1, LANES), lambda h, j: (h, 0, 0))],
        out_shape=[jax.ShapeDtypeStruct((t, A_QK), F32), jax.ShapeDtypeStruct((t, A_QK), F32),
                   jax.ShapeDtypeStruct((t, A_V), F32), jax.ShapeDtypeStruct((t, A_V), F32),
                   jax.ShapeDtypeStruct((1, A_QK), F32), jax.ShapeDtypeStruct((A_HEADS // 2, 1, LANES), F32)],
        scratch_shapes=[pltpu.VMEM((2, A_VDIM, A_KDIM), F32), pltpu.VMEM((tb, LANES), F32)],
        compiler_params=_cparams(("parallel", "arbitrary")),
    )(p, p, p, p, lb, hn2, o_raw, states, dya)


BLK = 128
SCALE = HDIM ** -0.5
SB_CHUNK = 4


def _softplus(z):
    return jnp.maximum(z, 0.0) + jnp.log(1.0 + jnp.exp(-jnp.abs(z)))


def _split2(x):
    hi = x.astype(BF16)
    return hi, (x - hi.astype(F32)).astype(BF16)


def _sb_sum_matrix(keep):
    sp = lax.broadcasted_iota(jnp.int32, (2 * BLK, 2 * BLK), 0) & (BLK - 1)
    s = lax.broadcasted_iota(jnp.int32, (2 * BLK, 2 * BLK), 1)
    return jnp.where(jnp.logical_or(s >= BLK, keep(sp, s)), 1.0, 0.0).astype(BF16)


def _sb_fwd(p, kv, name):
    t = p.shape[0]
    nq = t // BLK
    cw = SB_CHUNK * BLK

    def body(q_ref, kb, vb, o_ref, tot_ref, zbuf, stage, sbuf, abuf):
        qi = pl.program_id(1)

        @pl.when(qi == 0)
        def _():
            abuf[...] = jnp.zeros_like(abuf)

        row = lax.broadcasted_iota(jnp.int32, (BLK, BLK), 0)
        col = lax.broadcasted_iota(jnp.int32, (BLK, BLK), 1)
        sums = _sb_sum_matrix(lambda sp, s: sp >= s)
        hsl = [slice(HDIM * h, HDIM * (h + 1)) for h in range(2)]
        nchunk = qi // SB_CHUNK + 1
        for h in range(2):
            zbuf[h] = _dot_nt((q_ref[:, hsl[h]] * SCALE).astype(BF16), kb[:, hsl[h]])

        def causal(j):
            return (col + j * BLK) < (row + qi * BLK)

        def l_pass(c, carry):
            for b in range(SB_CHUNK):
                j = c * SB_CHUNK + b
                off = pl.multiple_of(j * BLK, BLK)
                mask = causal(j)
                for h in range(2):
                    lm = jnp.where(mask, -_softplus(zbuf[h, :, pl.ds(off, BLK)]), 0.0)
                    hi, lo = _split2(lm)
                    stage[h, pl.ds(off, BLK), :BLK] = hi
                    stage[h, pl.ds(off, BLK), BLK:] = lo
            return carry

        lax.fori_loop(0, nchunk, l_pass, 0)

        def sum_pass(c, carry):
            rows = pl.ds(pl.multiple_of(c * cw, cw), cw)
            for h in range(2):
                sbuf[h, rows, :] = _dot(stage[h, rows, :], sums)
            return carry

        lax.fori_loop(0, nchunk, sum_pass, 0)

        def a_pass(it, carry):
            c = nchunk - 1 - it
            runs = list(carry)
            for b in reversed(range(SB_CHUNK)):
                j = c * SB_CHUNK + b
                off = pl.multiple_of(j * BLK, BLK)
                mask = causal(j)
                for h in range(2):
                    s = sbuf[h, pl.ds(off, BLK), :BLK]
                    a = jnp.where(mask, jnp.exp(zbuf[h, :, pl.ds(off, BLK)] + s + runs[h]), 0.0)
                    abuf[h, :, pl.ds(off, BLK)] = a.astype(BF16)
                    runs[h] = runs[h] + sbuf[h, pl.ds(off, BLK), BLK:]
            return tuple(runs)

        zero = jnp.zeros((BLK, BLK), F32)
        runs = lax.fori_loop(0, nchunk, a_pass, (zero, zero))
        for h in range(2):
            tot_ref[:, hsl[h]] = runs[h][:, :HDIM]
            o_ref[:, hsl[h]] = _dot(abuf[h], vb[:, hsl[h]])

    out_blk = pl.BlockSpec((BLK, LANES), lambda h, i: (i, h))
    return pl.pallas_call(
        body, name=name, grid=(B_HEADS // 2, nq),
        in_specs=[pl.BlockSpec((BLK, LANES), lambda h, i: (i, COL_BQ // LANES + h)),
                  pl.BlockSpec((t, LANES), lambda h, i: (0, h)),
                  pl.BlockSpec((t, LANES), lambda h, i: (0, B_W // LANES + h))],
        out_specs=[out_blk, out_blk],
        out_shape=[jax.ShapeDtypeStruct((t, B_W), F32)] * 2,
        scratch_shapes=[pltpu.VMEM((2, BLK, t), F32), pltpu.VMEM((2, t, 2 * BLK), BF16),
                        pltpu.VMEM((2, t, 2 * BLK), F32), pltpu.VMEM((2, BLK, t), BF16)],
        compiler_params=_cparams(("parallel", "arbitrary")),
    )(p, kv, kv)


def _sb_bwd(p, kv, tot, do, name):
    t = p.shape[0]
    nq = t // BLK
    cw = SB_CHUNK * BLK

    def body(q_ref, kb, vb, tot_ref, do_ref, dq_ref, dk_ref, dv_ref, zbuf, dabuf, stage, gstage, sbuf, abuf, dzbuf):
        qi = pl.program_id(1)

        @pl.when(qi == 0)
        def _():
            dk_ref[...] = jnp.zeros_like(dk_ref)
            dv_ref[...] = jnp.zeros_like(dv_ref)
            dzbuf[...] = jnp.zeros_like(dzbuf)

        row = lax.broadcasted_iota(jnp.int32, (BLK, BLK), 0)
        col = lax.broadcasted_iota(jnp.int32, (BLK, BLK), 1)
        sums = _sb_sum_matrix(lambda sp, s: sp <= s)
        hsl = [slice(HDIM * h, HDIM * (h + 1)) for h in range(2)]
        qb = [q_ref[:, hsl[h]].astype(BF16) for h in range(2)]
        dob = [do_ref[:, hsl[h]].astype(BF16) for h in range(2)]
        total = [jnp.concatenate([tot_ref[:, hsl[h]], tot_ref[:, hsl[h]]], axis=1) for h in range(2)]
        nchunk = qi // SB_CHUNK + 1
        for h in range(2):
            zbuf[h] = _dot_nt((q_ref[:, hsl[h]] * SCALE).astype(BF16), kb[:, hsl[h]])
            dabuf[h] = _dot_nt(dob[h], vb[:, hsl[h]])

        def causal(j):
            return (col + j * BLK) < (row + qi * BLK)

        def blocks(c):
            for b in range(SB_CHUNK):
                j = c * SB_CHUNK + b
                yield j, pl.ds(pl.multiple_of(j * BLK, BLK), BLK)

        def l_pass(c, carry):
            for j, blk_ in blocks(c):
                mask = causal(j)
                for h in range(2):
                    lm = jnp.where(mask, -_softplus(zbuf[h, :, blk_]), 0.0)
                    hi, lo = _split2(lm)
                    stage[h, blk_, :BLK] = hi
                    stage[h, blk_, BLK:] = lo
            return carry

        lax.fori_loop(0, nchunk, l_pass, 0)

        def sum_pass(src):
            def run_(c, carry):
                rows = pl.ds(pl.multiple_of(c * cw, cw), cw)
                for h in range(2):
                    sbuf[h, rows, :] = _dot(src[h, rows, :], sums)
                return carry
            lax.fori_loop(0, nchunk, run_, 0)

        sum_pass(stage)

        def g_pass(c, carry):
            runs = list(carry)
            for j, blk_ in blocks(c):
                mask = causal(j)
                for h in range(2):
                    lm = stage[h, blk_, :BLK].astype(F32) + stage[h, blk_, BLK:].astype(F32)
                    log_a = zbuf[h, :, blk_] + lm + (total[h] - runs[h] - sbuf[h, blk_, :BLK])
                    a = jnp.where(mask, jnp.exp(log_a), 0.0)
                    abuf[h, :, blk_] = a.astype(BF16)
                    hi, lo = _split2(a * dabuf[h, :, blk_])
                    gstage[h, blk_, :BLK] = hi
                    gstage[h, blk_, BLK:] = lo
                    runs[h] = runs[h] + sbuf[h, blk_, BLK:]
            return tuple(runs)

        zero = jnp.zeros((BLK, BLK), F32)
        lax.fori_loop(0, nchunk, g_pass, (zero, zero))
        sum_pass(gstage)

        def dz_pass(c, carry):
            runs = list(carry)
            for j, blk_ in blocks(c):
                mask = causal(j)
                for h in range(2):
                    lm = stage[h, blk_, :BLK].astype(F32) + stage[h, blk_, BLK:].astype(F32)
                    g = gstage[h, blk_, :BLK].astype(F32) + gstage[h, blk_, BLK:].astype(F32)
                    before = runs[h] + sbuf[h, blk_, :BLK] - g
                    dz = jnp.where(mask, g * jnp.exp(lm) - jnp.exp(zbuf[h, :, blk_] + lm) * before, 0.0)
                    dzbuf[h, :, blk_] = (dz * SCALE).astype(BF16)
                    runs[h] = runs[h] + sbuf[h, blk_, BLK:]
            return tuple(runs)

        lax.fori_loop(0, nchunk, dz_pass, (zero, zero))
        for h in range(2):
            dq_ref[:, hsl[h]] = _dot(dzbuf[h], kb[:, hsl[h]])

        def kv_pass(c, carry):
            cols = pl.ds(pl.multiple_of(c * cw, cw), cw)
            for h in range(2):
                dk_ref[cols, hsl[h]] += _dot_tn(dzbuf[h, :, cols], qb[h])
                dv_ref[cols, hsl[h]] += _dot_tn(abuf[h, :, cols], dob[h])
            return carry

        lax.fori_loop(0, nchunk, kv_pass, 0)

    blk = lambda h, i: (i, h)
    whole = lambda h, i: (0, h)
    return pl.pallas_call(
        body, name=name, grid=(B_HEADS // 2, nq),
        in_specs=[pl.BlockSpec((BLK, LANES), lambda h, i: (i, COL_BQ // LANES + h)),
                  pl.BlockSpec((t, LANES), lambda h, i: (0, h)),
                  pl.BlockSpec((t, LANES), lambda h, i: (0, B_W // LANES + h)),
                  pl.BlockSpec((BLK, LANES), blk), pl.BlockSpec((BLK, LANES), blk)],
        out_specs=[pl.BlockSpec((BLK, LANES), blk), pl.BlockSpec((t, LANES), whole),
                   pl.BlockSpec((t, LANES), whole)],
        out_shape=[jax.ShapeDtypeStruct((t, B_W), F32)] * 3,
        scratch_shapes=[pltpu.VMEM((2, BLK, t), F32), pltpu.VMEM((2, BLK, t), F32),
                        pltpu.VMEM((2, t, 2 * BLK), BF16), pltpu.VMEM((2, t, 2 * BLK), BF16),
                        pltpu.VMEM((2, t, 2 * BLK), F32), pltpu.VMEM((2, BLK, t), BF16),
                        pltpu.VMEM((2, BLK, t), BF16)],
        compiler_params=_cparams(("parallel", "arbitrary")),
    )(p, kv, kv, tot, do)


def _alibi_slopes(n):
    def pow2(m):
        start = 2.0 ** (-8.0 / m)
        return [start ** (i + 1) for i in range(m)]
    if math.log2(n).is_integer():
        s = pow2(n)
    else:
        c = 2 ** int(math.floor(math.log2(n)))
        s = pow2(c) + pow2(2 * c)[0::2][: n - c]
    return sorted(s, reverse=True)


def _dil_scores(qh, kh, sl, prev, exists=None):
    row = lax.broadcasted_iota(jnp.int32, (BLK, BLK), 0)
    col = lax.broadcasted_iota(jnp.int32, (BLK, BLK), 1)
    dist = row - col + (BLK if prev else 0)
    if prev:
        valid = (col - row) >= jnp.where(exists, 0, 2 * BLK)
    else:
        valid = col <= row
    s = _dot_nt(qh, kh) - sl * dist.astype(F32)
    return s, valid


def _dil_fwd(q, k, v, slope_cols, name):
    ln, cw = q.shape
    nb = ln // BLK

    def body(q_ref, kc_ref, kp_ref, vc_ref, vp_ref, sl_ref, o_ref, lse_ref):
        i = pl.program_id(1)
        for h in range(2):
            hs = slice(HDIM * h, HDIM * (h + 1))
            sl = sl_ref[:, HDIM * h:HDIM * h + 1]
            qh = (q_ref[:, hs] * SCALE).astype(BF16)
            sc, vc_ok = _dil_scores(qh, kc_ref[:, hs].astype(BF16), sl, False)
            sp, vp_ok = _dil_scores(qh, kp_ref[:, hs].astype(BF16), sl, True, i > 0)
            sc = jnp.where(vc_ok, sc, NEG_BIG)
            sp = jnp.where(vp_ok, sp, NEG_BIG)
            m = jnp.maximum(jnp.max(sc, axis=1, keepdims=True), jnp.max(sp, axis=1, keepdims=True))
            pc = jnp.exp(sc - m)
            pp = jnp.exp(sp - m)
            den = jnp.sum(pc, axis=1, keepdims=True) + jnp.sum(pp, axis=1, keepdims=True)
            o = _dot(pc.astype(BF16), vc_ref[:, hs].astype(BF16)) + _dot(pp.astype(BF16), vp_ref[:, hs].astype(BF16))
            o_ref[:, hs] = o / den
            lse_ref[:, hs] = jnp.broadcast_to(m + jnp.log(den), (BLK, HDIM))

    cur = pl.BlockSpec((BLK, LANES), lambda c, i: (i, c))
    prv = pl.BlockSpec((BLK, LANES), lambda c, i: (jnp.maximum(i - 1, 0), c))
    return pl.pallas_call(
        body, name=name, grid=(cw // LANES, nb),
        in_specs=[cur, cur, prv, cur, prv, pl.BlockSpec((1, LANES), lambda c, i: (0, c))],
        out_specs=[cur, cur], out_shape=[jax.ShapeDtypeStruct((ln, cw), F32)] * 2,
        compiler_params=_cparams(("parallel", "parallel")),
    )(q, k, k, v, v, slope_cols)


def _dil_bwd(q, k, v, do, o, lse, slope_cols, name):
    ln, cw = q.shape
    nb = ln // BLK

    def body(q_ref, qn_ref, kc_ref, kp_ref, vc_ref, vp_ref, do_ref, don_ref, o_ref, on_ref, l_ref, ln_ref, sl_ref,
             dq_ref, dk_ref, dv_ref):
        i = pl.program_id(1)
        has_prev = i > 0
        has_next = i < nb - 1
        for h in range(2):
            hs = slice(HDIM * h, HDIM * (h + 1))
            sl = sl_ref[:, HDIM * h:HDIM * h + 1]
            qb = q_ref[:, hs].astype(BF16)
            qnb = qn_ref[:, hs].astype(BF16)
            qh = (q_ref[:, hs] * SCALE).astype(BF16)
            qnh = (qn_ref[:, hs] * SCALE).astype(BF16)
            kc = kc_ref[:, hs].astype(BF16)
            kp = kp_ref[:, hs].astype(BF16)
            vc = vc_ref[:, hs].astype(BF16)
            vp = vp_ref[:, hs].astype(BF16)
            do_f = do_ref[:, hs]
            don_f = don_ref[:, hs]
            dob = do_f.astype(BF16)
            donb = don_f.astype(BF16)
            delta = jnp.sum(do_f * o_ref[:, hs], axis=1, keepdims=True)
            deltan = jnp.sum(don_f * on_ref[:, hs], axis=1, keepdims=True)
            lse = l_ref[:, HDIM * h:HDIM * h + 1]
            lsen = ln_ref[:, HDIM * h:HDIM * h + 1]
            s, ok = _dil_scores(qh, kc, sl, False)
            p_cc = jnp.where(ok, jnp.exp(jnp.where(ok, s, NEG_BIG) - lse), 0.0)
            ds_cc = p_cc * (_dot_nt(dob, vc) - delta)
            s, ok = _dil_scores(qh, kp, sl, True, has_prev)
            p_cp = jnp.where(ok, jnp.exp(jnp.where(ok, s, NEG_BIG) - lse), 0.0)
            ds_cp = p_cp * (_dot_nt(dob, vp) - delta)
            s, ok = _dil_scores(qnh, kc, sl, True, has_next)
            p_nc = jnp.where(ok, jnp.exp(jnp.where(ok, s, NEG_BIG) - lsen), 0.0)
            ds_nc = p_nc * (_dot_nt(donb, vc) - deltan)
            ds_cc_b = (ds_cc * SCALE).astype(BF16)
            ds_cp_b = (ds_cp * SCALE).astype(BF16)
            ds_nc_b = (ds_nc * SCALE).astype(BF16)
            dq_ref[:, hs] = _dot(ds_cc_b, kc) + _dot(ds_cp_b, kp)
            dk_ref[:, hs] = _dot_tn(ds_cc_b, qb) + _dot_tn(ds_nc_b, qnb)
            dv_ref[:, hs] = _dot_tn(p_cc.astype(BF16), dob) + _dot_tn(p_nc.astype(BF16), donb)

    cur = pl.BlockSpec((BLK, LANES), lambda c, i: (i, c))
    prv = pl.BlockSpec((BLK, LANES), lambda c, i: (jnp.maximum(i - 1, 0), c))
    nxt = pl.BlockSpec((BLK, LANES), lambda c, i: (jnp.minimum(i + 1, nb - 1), c))
    return pl.pallas_call(
        body, name=name, grid=(cw // LANES, nb),
        in_specs=[cur, nxt, cur, prv, cur, prv, cur, nxt, cur, nxt, cur, nxt,
                  pl.BlockSpec((1, LANES), lambda c, i: (0, c))],
        out_specs=[cur, cur, cur], out_shape=[jax.ShapeDtypeStruct((ln, cw), F32)] * 3,
        compiler_params=_cparams(("parallel", "parallel")),
    )(q, q, k, k, v, v, do, do, o, o, lse, lse, slope_cols)


def _dil_merge(os_, ls_, name):
    t, w = os_[0].shape
    tr = _rows(t)

    def body(o0, o1, o2, l0, l1, l2, y_ref, lse_ref):
        a, b, c = l0[...], l1[...], l2[...]
        m = jnp.maximum(jnp.maximum(a, b), c)
        ea, eb, ec = jnp.exp(a - m), jnp.exp(b - m), jnp.exp(c - m)
        den = ea + eb + ec
        y_ref[...] = (ea * o0[...] + eb * o1[...] + ec * o2[...]) / den
        lse_ref[...] = m + jnp.log(den)

    row = pl.BlockSpec((tr, w), lambda i: (i, 0))
    return pl.pallas_call(
        body, name=name, grid=(t // tr,), in_specs=[row] * 6, out_specs=[row, row],
        out_shape=[jax.ShapeDtypeStruct((t, w), F32)] * 2, compiler_params=_cparams(("parallel",)),
    )(*os_, *ls_)


def _gate_fwd(ys, gl, ws, name):
    t = gl.shape[0]
    d = gl.shape[1] // N_BRANCH
    tr = _rows(t)

    def body(ya, yb, yc, gl_ref, wa, wb, wc, m_ref):
        acc = None
        for i, (y, w) in enumerate(((ya, wa), (yb, wb), (yc, wc))):
            z = _dot(y[...].astype(BF16), w[...])
            term = jax.nn.sigmoid(gl_ref[:, i * d:(i + 1) * d]) * z
            acc = term if acc is None else acc + term
        m_ref[...] = acc.astype(m_ref.dtype)

    rows = [pl.BlockSpec((tr, y.shape[1]), lambda i: (i, 0)) for y in ys]
    wsp = [pl.BlockSpec(w.shape, lambda i: (0, 0)) for w in ws]
    return pl.pallas_call(
        body, name=name, grid=(t // tr,),
        in_specs=rows + [pl.BlockSpec((tr, N_BRANCH * d), lambda i: (i, 0))] + wsp,
        out_specs=pl.BlockSpec((tr, d), lambda i: (i, 0)), out_shape=jax.ShapeDtypeStruct((t, d), BF16),
        compiler_params=_cparams(("parallel",)),
    )(*ys, gl, *ws)


def _gate_bwd(dm, ys, gl, ws, name):
    t = gl.shape[0]
    d = gl.shape[1] // N_BRANCH
    tr = _rows(t)

    def body(dm_ref, ya, yb, yc, gl_ref, wa, wb, wc, dya, dyb, dyc, dgl_ref, dwa, dwb, dwc):
        step = pl.program_id(0)
        dmv = dm_ref[...].astype(F32)
        for i, (y, w, dy, dw) in enumerate(((ya, wa, dya, dwa), (yb, wb, dyb, dwb), (yc, wc, dyc, dwc))):
            yb16 = y[...].astype(BF16)
            z = _dot(yb16, w[...])
            sg = jax.nn.sigmoid(gl_ref[:, i * d:(i + 1) * d])
            dgl_ref[:, i * d:(i + 1) * d] = dmv * z * sg * (1.0 - sg)
            e = (dmv * sg).astype(BF16)
            dy[...] = _dot_nt(e, w[...])
            contrib = _dot_tn(yb16, e)

            @pl.when(step == 0)
            def _(dw=dw, contrib=contrib):
                dw[...] = contrib

            @pl.when(step > 0)
            def _(dw=dw, contrib=contrib):
                dw[...] += contrib

    rows = [pl.BlockSpec((tr, y.shape[1]), lambda i: (i, 0)) for y in ys]
    wsp = [pl.BlockSpec(w.shape, lambda i: (0, 0)) for w in ws]
    gsp = pl.BlockSpec((tr, N_BRANCH * d), lambda i: (i, 0))
    return pl.pallas_call(
        body, name=name, grid=(t // tr,),
        in_specs=[pl.BlockSpec((tr, d), lambda i: (i, 0))] + rows + [gsp] + wsp,
        out_specs=rows + [gsp] + wsp,
        out_shape=[jax.ShapeDtypeStruct(y.shape, F32) for y in ys] + [jax.ShapeDtypeStruct(gl.shape, F32)]
        + [jax.ShapeDtypeStruct(w.shape, F32) for w in ws],
        compiler_params=_cparams(("arbitrary",)),
    )(dm, *ys, gl, *ws)


def _adamw(w, m, v, gparts, name):
    r, c = w.shape
    n = gparts.shape[0]
    br = LANES if r % LANES == 0 else r
    c1 = 1.0 - ADAM_B1 ** ADAM_STEP
    c2 = 1.0 - ADAM_B2 ** ADAM_STEP

    def body(w_ref, m_ref, v_ref, g_ref, go_ref, d_ref, mo_ref, vo_ref):
        g = g_ref[0].astype(F32)
        for i in range(1, n):
            g = g + g_ref[i].astype(F32)
        mn = ADAM_B1 * m_ref[...] + (1.0 - ADAM_B1) * g
        vn = ADAM_B2 * v_ref[...] + (1.0 - ADAM_B2) * (g * g)
        go_ref[...] = g
        mo_ref[...] = mn
        vo_ref[...] = vn
        d_ref[...] = -ADAM_LR * ((mn / c1) / (jnp.sqrt(vn / c2) + ADAM_EPS) + ADAM_WD * w_ref[...])

    blk = pl.BlockSpec((br, c), lambda i: (i, 0))
    return pl.pallas_call(
        body, name=name, grid=(r // br,),
        in_specs=[blk, blk, blk, pl.BlockSpec((n, br, c), lambda i: (0, i, 0))],
        out_specs=[blk] * 4, out_shape=[jax.ShapeDtypeStruct((r, c), F32)] * 4,
        compiler_params=_cparams(("parallel",)),
    )(w, m, v, gparts)


def _my_coords():
    return lax.axis_index("x"), lax.axis_index("y"), lax.axis_index("c")


def _all_gather(x_shard, in_vmem, with_sum, name):
    m_per, n = x_shard.shape

    def body(x_ref, out_ref, *rest):
        if with_sum:
            sum_ref, send_sems, recv_sems, local_sem = rest
        else:
            send_sems, recv_sems, local_sem = rest
        x, y, c = _my_coords()
        me, sibling = (x, y, c), (x, y, 1 - c)
        chips = [(1 - x, y), (x, 1 - y), (1 - x, 1 - y)]

        def rows(px, py, pc):
            return out_ref.at[pl.ds((4 * px + 2 * py + pc) * m_per, m_per), :]

        def copy(k, block, to, src=None):
            return pltpu.make_async_remote_copy(
                src_ref=rows(*block) if src is None else src, dst_ref=rows(*block),
                send_sem=send_sems.at[k], recv_sem=recv_sems.at[k], device_id=to, device_id_type=MESH)

        mine = pltpu.make_async_copy(x_ref, rows(*me), local_sem)
        mine.start()
        first = [copy(0, me, sibling, src=x_ref)]
        first += [copy(1 + j, me, (*chip, c), src=x_ref) for j, chip in enumerate(chips)]
        for cp in first:
            cp.start()
        passed = [copy(4 + j, (*chip, c), sibling) for j, chip in enumerate(chips)]
        for j, chip in enumerate(chips):
            copy(1 + j, (*chip, c), me).wait_recv()
            passed[j].start()
        copy(0, sibling, me).wait_recv()
        for j, chip in enumerate(chips):
            copy(4 + j, (*chip, 1 - c), me).wait_recv()
        for cp in first + passed:
            cp.wait_send()
        mine.wait()
        if with_sum:
            acc = out_ref[pl.ds(0, m_per), :]
            for d in range(1, N_DEV):
                acc = acc + out_ref[pl.ds(d * m_per, m_per), :]
            sum_ref[...] = acc

    space = pltpu.VMEM if in_vmem else pl.ANY
    out_shape = [jax.ShapeDtypeStruct((N_DEV * m_per, n), x_shard.dtype)]
    out_specs = [pl.BlockSpec(memory_space=space)]
    if with_sum:
        out_shape.append(jax.ShapeDtypeStruct((m_per, n), x_shard.dtype))
        out_specs.append(pl.BlockSpec(memory_space=pltpu.VMEM))
    res = pl.pallas_call(
        body, name=name, out_shape=out_shape, in_specs=[pl.BlockSpec(memory_space=space)], out_specs=out_specs,
        scratch_shapes=[pltpu.SemaphoreType.DMA((7,)), pltpu.SemaphoreType.DMA((7,)), pltpu.SemaphoreType.DMA],
        compiler_params=pltpu.CompilerParams(vmem_limit_bytes=VMEM_LIMIT),
    )(x_shard)
    return res if with_sum else res[0]


def _all_to_all(send, name):
    _, r, c = send.shape

    def body(send_ref, recv_ref, send_sems, recv_sems, local_sem):
        x, y, cc = _my_coords()
        me = 4 * x + 2 * y + cc
        mine = pltpu.make_async_copy(send_ref.at[me], recv_ref.at[me], local_sem)
        mine.start()
        copies = []
        for k in range(1, N_DEV):
            px = 1 - x if k & 4 else x
            py = 1 - y if k & 2 else y
            pc = 1 - cc if k & 1 else cc
            peer = 4 * px + 2 * py + pc
            cp = pltpu.make_async_remote_copy(
                src_ref=send_ref.at[peer], dst_ref=recv_ref.at[me],
                send_sem=send_sems.at[k - 1], recv_sem=recv_sems.at[k - 1],
                device_id=(px, py, pc), device_id_type=MESH)
            cp.start()
            copies.append((cp, peer, (px, py, pc)))
        for k, (cp, peer, pid) in enumerate(copies):
            cp.wait_send()
            pltpu.make_async_remote_copy(
                src_ref=send_ref.at[me], dst_ref=recv_ref.at[peer],
                send_sem=send_sems.at[k], recv_sem=recv_sems.at[k], device_id=pid, device_id_type=MESH).wait_recv()
        mine.wait()

    return pl.pallas_call(
        body, name=name, out_shape=jax.ShapeDtypeStruct(send.shape, send.dtype),
        in_specs=[pl.BlockSpec(memory_space=pl.ANY)], out_specs=pl.BlockSpec(memory_space=pl.ANY),
        scratch_shapes=[pltpu.SemaphoreType.DMA((7,)), pltpu.SemaphoreType.DMA((7,)), pltpu.SemaphoreType.DMA],
    )(send)


def _row(v):
    return v.reshape(1, -1)


def _ffn_fwd(x, w_in, w_out, g_pre, g_post, m, res_w, tag):
    shift, scale, gate = m[0], m[1], m[2]
    mpre = _row(g_pre * (1.0 + scale))
    mpost = _row(res_w * gate * g_post)
    h = _rms_fwd(x, mpre, _row(shift), None, BF16, tag + "_pre")
    u = _matmul(h, w_in, out_dtype=BF16, name=tag + "_in")
    s = _swiglu_fwd(u, tag + "_act")
    y = _matmul(s, w_out, name=tag + "_out")
    x_new = _rms_fwd(y, mpost, jnp.zeros_like(mpost), x, F32, tag + "_post")
    return x_new, (x, h, u, s, y, mpre, mpost)


def _sub_bwd_post(dx_new, y, mpost, g_post, gate, res_w, tag):
    dy, c1, _ = _rms_bwd(dx_new, y, mpost, None, BF16, tag + "_post_bwd")
    c1 = c1[0]
    return dy, c1 * res_w * g_post, c1 * res_w * gate


def _sub_bwd_pre(dh, x, mpre, dx_new, g_pre, scale, tag):
    dx, c2, c3 = _rms_bwd(dh, x, mpre, dx_new, F32, tag + "_pre_bwd")
    c2, c3 = c2[0], c3[0]
    return dx, c3, c2 * g_pre, c2 * (1.0 + scale)


def _ffn_bwd(dx_new, saved, w_in, w_out, g_pre, g_post, m, res_w, tag):
    x, h, u, s, y, mpre, mpost = saved
    scale, gate = m[1], m[2]
    dy, dgate, dg_post = _sub_bwd_post(dx_new, y, mpost, g_post, gate, res_w, tag)
    ds = _matmul(dy, w_out, tb=True, out_dtype=BF16, name=tag + "_out_dx")
    dw_out = _matmul(s, dy, ta=True, out_dtype=BF16, name=tag + "_out_dw")
    du = _swiglu_bwd(u, ds, tag + "_act_bwd")
    dh = _matmul(du, w_in, tb=True, name=tag + "_in_dx")
    dw_in = _matmul(h, du, ta=True, out_dtype=BF16, name=tag + "_in_dw")
    dx, dshift, dscale, dg_pre = _sub_bwd_pre(dh, x, mpre, dx_new, g_pre, scale, tag)
    return dx, dw_in, dw_out, jnp.stack([dshift, dscale, dgate]), dg_pre, dg_post


def _slope_cols(gi):
    _, r = C_GROUPS[gi]
    sl = jnp.asarray(_alibi_slopes(C_HEADS)[gi * C_HPG:(gi + 1) * C_HPG], F32) * float(r)
    return jnp.tile(jnp.repeat(sl, HDIM), r).reshape(1, r * C_OUT)


def _group_view(a, gi):
    _, r = C_GROUPS[gi]
    t = a.shape[0]
    return a.reshape(t // r, r * a.shape[1])


def _mix_fwd(x, w, g_pre, g_post, m, lb, hn, tag):
    t, d = x.shape
    shift, scale, gate = m[0], m[1], m[2]
    mpre = _row(g_pre * (1.0 + scale))
    mpost = _row(gate * g_post)
    h = _rms_fwd(x, mpre, _row(shift), None, BF16, tag + "_pre")
    p = _matmul(h, w["w_in"], name=tag + "_in")
    hn2 = _row(jnp.tile(hn, 2))
    ya, oa, states = _hgrn_fwd(p, _row(lb), hn2, tag + "_hgrn")
    kv = p[:, COL_BK:COL_CQ].astype(BF16)
    yb, sb_tot = _sb_fwd(p, kv, tag + "_sb")
    og, lg, qkv = [], [], []
    for gi in range(len(C_GROUPS)):
        cs = slice(gi * C_OUT, (gi + 1) * C_OUT)
        q = _group_view(p[:, COL_CQ:COL_CK][:, cs], gi)
        k = _group_view(p[:, COL_CK:COL_CV][:, cs], gi)
        v = _group_view(p[:, COL_CV:COL_GATE][:, cs], gi)
        o, lse = _dil_fwd(q, k, v, _slope_cols(gi), tag + "_dil%d" % gi)
        og.append(o.reshape(t, C_OUT))
        lg.append(lse.reshape(t, C_OUT))
        qkv.append((q, k, v))
    yc, lse_c = _dil_merge(og, lg, tag + "_dil_merge")
    gl = p[:, COL_GATE:]
    ws = (w["w_branch_a"], w["w_branch_b"], w["w_branch_c"])
    merged = _gate_fwd((ya, yb, yc), gl, ws, tag + "_gate")
    y = _matmul(merged, w["w_out"], name=tag + "_out")
    x_new = _rms_fwd(y, mpost, jnp.zeros_like(mpost), x, F32, tag + "_post")
    return x_new, (x, h, p, hn2, ya, oa, states, yb, kv, sb_tot, qkv, yc, lse_c, gl, merged, y, mpre, mpost)


def _mix_bwd(dx_new, saved, w, g_pre, g_post, m, lb, tag):
    x, h, p, hn2, ya, oa, states, yb, kv, sb_tot, qkv, yc, lse_c, gl, merged, y, mpre, mpost = saved
    t = x.shape[0]
    scale, gate = m[1], m[2]
    dy, dgate, dg_post = _sub_bwd_post(dx_new, y, mpost, g_post, gate, 1.0, tag)
    dmerged = _matmul(dy, w["w_out"], tb=True, out_dtype=BF16, name=tag + "_out_dx")
    dw_out = _matmul(merged, dy, ta=True, out_dtype=BF16, name=tag + "_out_dw")
    ws = (w["w_branch_a"], w["w_branch_b"], w["w_branch_c"])
    dya, dyb, dyc, dgl, dwa, dwb, dwc = _gate_bwd(dmerged, (ya, yb, yc), gl, ws, tag + "_gate_bwd")
    dqa, dfa, dia, dga, dlb, dhn = _hgrn_bwd(p, _row(lb), hn2, oa, states, dya, tag + "_hgrn_bwd")
    dbq, dbk, dbv = _sb_bwd(p, kv, sb_tot, dyb, tag + "_sb_bwd")
    dcq, dck, dcv = [], [], []
    for gi in range(len(C_GROUPS)):
        q, k, v = qkv[gi]
        dq, dk, dv = _dil_bwd(q, k, v, _group_view(dyc, gi), _group_view(yc, gi), _group_view(lse_c, gi),
                              _slope_cols(gi), tag + "_dil%d_bwd" % gi)
        dcq.append(dq.reshape(t, C_OUT))
        dck.append(dk.reshape(t, C_OUT))
        dcv.append(dv.reshape(t, C_OUT))
    dp = jnp.concatenate([dqa, dfa, dia, dga, dbq, dbk, dbv] + dcq + dck + dcv + [dgl], axis=1).astype(BF16)
    dh = _matmul(dp, w["w_in"], tb=True, name=tag + "_in_dx")
    dw_in = _matmul(h, dp, ta=True, out_dtype=BF16, name=tag + "_in_dw")
    dx, dshift, dscale, dg_pre = _sub_bwd_pre(dh, x, mpre, dx_new, g_pre, scale, tag)
    dhn_v = jnp.sum(dhn, axis=(0, 1))
    dhn_v = dhn_v[:A_VDIM] + dhn_v[A_VDIM:]
    dws = dict(w_in=dw_in, w_out=dw_out, w_branch_a=dwa.astype(BF16), w_branch_b=dwb.astype(BF16),
               w_branch_c=dwc.astype(BF16))
    return dx, dws, jnp.stack([dshift, dscale, dgate]), dg_pre, dg_post, dlb[0], dhn_v


def _local_step(x, target, mod, norm_g, lb_all, hnorm, wts):
    depth = mod.shape[0]
    d = x.shape[1]
    saved = []
    for l in range(depth):
        wl = {k: v[l] for k, v in wts.items()}
        x, s0 = _ffn_fwd(x, wl["ffn1_w_in"], wl["ffn1_w_out"], norm_g[l, 0], norm_g[l, 1], mod[l, 0], 0.5, "ffn1")
        x, s1 = _mix_fwd(x, wl, norm_g[l, 2], norm_g[l, 3], mod[l, 1], lb_all[l], hnorm[l], "mix")
        x, s2 = _ffn_fwd(x, wl["ffn2_w_in"], wl["ffn2_w_out"], norm_g[l, 4], norm_g[l, 5], mod[l, 2], 0.5, "ffn2")
        saved.append((s0, s1, s2))
    dx, sq = _loss_head(x, target, "loss_head")
    loss = 0.5 * jnp.sum(sq) / d
    dmod, dng, dlb, dhn = [], [], [], []
    dws = {k: [] for k in wts}
    for l in reversed(range(depth)):
        wl = {k: v[l] for k, v in wts.items()}
        s0, s1, s2 = saved[l]
        dx, dwi2, dwo2, dm2, dgp2, dgq2 = _ffn_bwd(dx, s2, wl["ffn2_w_in"], wl["ffn2_w_out"], norm_g[l, 4],
                                                   norm_g[l, 5], mod[l, 2], 0.5, "ffn2")
        dx, dwm, dm1, dgp1, dgq1, dlb_l, dhn_l = _mix_bwd(dx, s1, wl, norm_g[l, 2], norm_g[l, 3], mod[l, 1],
                                                          lb_all[l], "mix")
        dx, dwi1, dwo1, dm0, dgp0, dgq0 = _ffn_bwd(dx, s0, wl["ffn1_w_in"], wl["ffn1_w_out"], norm_g[l, 0],
                                                   norm_g[l, 1], mod[l, 0], 0.5, "ffn1")
        dmod.append(jnp.stack([dm0, dm1, dm2]))
        dng.append(jnp.stack([dgp0, dgq0, dgp1, dgq1, dgp2, dgq2]))
        dlb.append(dlb_l)
        dhn.append(dhn_l)
        dws["ffn1_w_in"].append(dwi1)
        dws["ffn1_w_out"].append(dwo1)
        dws["ffn2_w_in"].append(dwi2)
        dws["ffn2_w_out"].append(dwo2)
        for k, g in dwm.items():
            dws[k].append(g)
    rev = lambda lst: jnp.stack(lst[::-1])
    return (loss, dx, rev(dmod), rev(dng), rev(dlb), rev(dhn), {k: rev(v) for k, v in dws.items()})


def _lb_all(logits):
    lb_p = jax.nn.softmax(logits.astype(F32), axis=0)
    return jnp.cumsum(lb_p, axis=0) - lb_p[0:1]


def _pad_rows(a, rows):
    return jnp.pad(a, ((0, rows - a.shape[0]), (0, 0)))


def _gather_weights(shards):
    flat = jnp.concatenate([shards[k].astype(BF16).reshape(-1, 1024) for k in BIG_WEIGHTS], axis=0)
    got = _all_gather(flat, False, False, "weights_all_gather").reshape(N_DEV, flat.shape[0], 1024)
    out, off = {}, 0
    for k in BIG_WEIGHTS:
        depth, r, c = shards[k].shape
        nrow = depth * r * c // 1024
        blk = got[:, off:off + nrow].reshape(N_DEV, depth, r, c)
        off += nrow
        if k in ROW_SHARDED:
            out[k] = blk.transpose(1, 0, 2, 3).reshape(depth, N_DEV * r, c)
        else:
            out[k] = blk.transpose(1, 2, 0, 3).reshape(depth, r, N_DEV * c)
    return out


def _scatter_grads(dws, shard_shapes):
    parts = []
    for k in BIG_WEIGHTS:
        depth, r, c = shard_shapes[k]
        g = dws[k]
        if k in ROW_SHARDED:
            g = g.reshape(depth, N_DEV, r, c).transpose(1, 0, 2, 3)
        else:
            g = g.reshape(depth, r, N_DEV, c).transpose(2, 0, 1, 3)
        parts.append(g.reshape(N_DEV, -1, 1024))
    send = jnp.concatenate(parts, axis=1)
    recv = _all_to_all(send, "grads_all_to_all")
    out, off = {}, 0
    for k in BIG_WEIGHTS:
        depth, r, c = shard_shapes[k]
        nrow = depth * r * c // 1024
        out[k] = recv[:, off:off + nrow].reshape(N_DEV, depth * r, c)
        off += nrow
    return out


def kernel(x, c, w_ada, b_ada, norm_g, ffn1_w_in, ffn1_w_out, w_in, hgrn_lb_logits, hgrn_norm_g, w_branch_a, w_branch_b, w_branch_c, w_out, ffn2_w_in, ffn2_w_out, loss_target, m_w_ada, m_b_ada, m_norm_g, m_ffn1_w_in, m_ffn1_w_out, m_w_in, m_hgrn_lb_logits, m_hgrn_norm_g, m_w_branch_a, m_w_branch_b, m_w_branch_c, m_w_out, m_ffn2_w_in, m_ffn2_w_out, v_w_ada, v_b_ada, v_norm_g, v_ffn1_w_in, v_ffn1_w_out, v_w_in, v_hgrn_lb_logits, v_hgrn_norm_g, v_w_branch_a, v_w_branch_b, v_w_branch_c, v_w_out, v_ffn2_w_in, v_ffn2_w_out):
    weights = dict(w_ada=w_ada, b_ada=b_ada, norm_g=norm_g, ffn1_w_in=ffn1_w_in, ffn1_w_out=ffn1_w_out, w_in=w_in,
                   hgrn_lb_logits=hgrn_lb_logits, hgrn_norm_g=hgrn_norm_g, w_branch_a=w_branch_a,
                   w_branch_b=w_branch_b, w_branch_c=w_branch_c, w_out=w_out, ffn2_w_in=ffn2_w_in,
                   ffn2_w_out=ffn2_w_out)
    mom1 = dict(w_ada=m_w_ada, b_ada=m_b_ada, norm_g=m_norm_g, ffn1_w_in=m_ffn1_w_in, ffn1_w_out=m_ffn1_w_out,
                w_in=m_w_in, hgrn_lb_logits=m_hgrn_lb_logits, hgrn_norm_g=m_hgrn_norm_g, w_branch_a=m_w_branch_a,
                w_branch_b=m_w_branch_b, w_branch_c=m_w_branch_c, w_out=m_w_out, ffn2_w_in=m_ffn2_w_in,
                ffn2_w_out=m_ffn2_w_out)
    mom2 = dict(w_ada=v_w_ada, b_ada=v_b_ada, norm_g=v_norm_g, ffn1_w_in=v_ffn1_w_in, ffn1_w_out=v_ffn1_w_out,
                w_in=v_w_in, hgrn_lb_logits=v_hgrn_lb_logits, hgrn_norm_g=v_hgrn_norm_g, w_branch_a=v_w_branch_a,
                w_branch_b=v_w_branch_b, w_branch_c=v_w_branch_c, w_out=v_w_out, ffn2_w_in=v_ffn2_w_in,
                ffn2_w_out=v_ffn2_w_out)
    order = list(weights)
    depth, d, ada_cols = w_ada.shape
    nd = d // LANES
    xi, yi, ci = _my_coords()
    me = 4 * xi + 2 * yi + ci

    small = jnp.concatenate([c.reshape(nd, LANES), norm_g.reshape(depth * 6, LANES)], axis=0)
    g1 = _all_gather(small, True, False, "small_all_gather").reshape(N_DEV, small.shape[0], LANES)
    c_act = _silu(g1[:, :nd].reshape(N_DEV, d))
    norm_full = g1[:, nd:].reshape(N_DEV, depth, 6, LANES).transpose(1, 2, 0, 3).reshape(depth, 6, d)

    c_pad = _pad_rows(c_act, 16)
    mod_sh = jnp.stack([_matmul(c_pad, w_ada[l], name="ada_mod")[:N_DEV]
                        + lax.dynamic_slice_in_dim(b_ada[l], me * ada_cols, ada_cols)[None]
                        for l in range(depth)])
    g2 = _all_gather(mod_sh.reshape(-1, LANES), True, False, "mod_all_gather")
    g2 = g2.reshape(N_DEV, depth, N_DEV, ada_cols)
    mod = lax.dynamic_index_in_dim(g2, me, axis=2, keepdims=False)
    mod = mod.transpose(1, 0, 2).reshape(depth, 3, 3, d)

    big = {k: weights[k] for k in BIG_WEIGHTS}
    wts = _gather_weights(big)
    lb_all, lb_vjp = jax.vjp(_lb_all, hgrn_lb_logits)

    loss, dx, dmod, dng, dlb, dhn, dws = _local_step(x[0], loss_target[0], mod, norm_full, lb_all, hgrn_norm_g, wts)
    loss = lax.psum(loss, ("x", "y", "c"))

    dhn_pad = jnp.pad(dhn.reshape(-1), (0, 8 * LANES - dhn.size))
    pieces = [dmod.reshape(-1), dng.reshape(-1), dlb.reshape(-1), dhn_pad]
    sizes = [p_.size for p_ in pieces]
    smallg = jnp.concatenate(pieces).reshape(-1, LANES)
    g3, gsum = _all_gather(smallg, True, True, "small_grads_all_gather")
    g3 = g3.reshape(N_DEV, -1)
    gsum = gsum.reshape(-1)
    dmod_all = g3[:, :sizes[0]].reshape(N_DEV, depth, 9 * d)
    o1 = sizes[0]
    grads = {}
    grads["b_ada"] = gsum[:o1].reshape(depth, 9 * d)
    dng_sum = gsum[o1:o1 + sizes[1]].reshape(depth, 6, nd, LANES)
    grads["norm_g"] = lax.dynamic_index_in_dim(dng_sum, me, axis=2, keepdims=False)
    o2 = o1 + sizes[1]
    dlb_sum = gsum[o2:o2 + sizes[2]].reshape(depth, A_QK)
    grads["hgrn_lb_logits"] = lb_vjp(dlb_sum)[0]
    o3 = o2 + sizes[2]
    grads["hgrn_norm_g"] = gsum[o3:o3 + dhn.size].reshape(depth, A_VDIM)
    dmod_mine = lax.dynamic_slice_in_dim(dmod_all, me * ada_cols, ada_cols, axis=2)
    grads["w_ada"] = jnp.stack([_matmul(c_pad, _pad_rows(dmod_mine[:, l], 16), ta=True, name="ada_dw")
                                for l in range(depth)])

    shard_shapes = {k: weights[k].shape for k in BIG_WEIGHTS}
    gparts = _scatter_grads(dws, shard_shapes)

    outs = {}
    for k in order:
        w = weights[k]
        w2 = w.reshape(-1, w.shape[-1])
        gp = gparts[k] if k in gparts else grads[k].reshape((1,) + w2.shape)
        res = _adamw(w2, mom1[k].reshape(w2.shape), mom2[k].reshape(w2.shape), gp, "adamw")
        outs[k] = [r.reshape(w.shape) for r in res]
    return (loss, dx[None], *[outs[k][0] for k in order], *[outs[k][1] for k in order],
            *[outs[k][2] for k in order], *[outs[k][3] for k in order])
```

```python
import functools
import math

import jax
import jax.numpy as jnp
from jax import lax
from jax.experimental import pallas as pl
from jax.experimental.pallas import tpu as pltpu

F32 = jnp.float32
BF16 = jnp.bfloat16

A_HEADS, A_KDIM, A_VDIM, A_CHUNK = 6, 128, 64, 64
B_HEADS, HDIM = 6, 64
C_GROUPS = ((128, 1), (512, 4), (2048, 16))
C_HPG = 4
C_HEADS = C_HPG * len(C_GROUPS)
N_BRANCH = 3
EPS = 1e-6
NEG_BIG = -1e30
TINY = 1e-30
A_QK = A_HEADS * A_KDIM
A_V = A_HEADS * A_VDIM
B_W = B_HEADS * HDIM
C_W = C_HEADS * HDIM
C_OUT = C_HPG * HDIM
COL_AQ, COL_AF, COL_AI, COL_AG = 0, A_QK, 2 * A_QK, 2 * A_QK + A_V
COL_BQ = 2 * A_QK + 2 * A_V
COL_BK, COL_BV = COL_BQ + B_W, COL_BQ + 2 * B_W
COL_CQ = COL_BQ + 3 * B_W
COL_CK, COL_CV = COL_CQ + C_W, COL_CQ + 2 * C_W
COL_GATE = COL_CQ + 3 * C_W

ADAM_LR, ADAM_B1, ADAM_B2, ADAM_EPS, ADAM_WD, ADAM_STEP = 0.001, 0.9, 0.999, 1e-08, 0.01, 10

N_DEV = 8
LANES = 128
VMEM_LIMIT = 48 * 1024 * 1024
MATMUL_VMEM_BUDGET = 28 * 1024 * 1024
SUB = 16
EXP_CLAMP = 80.0
MESH = pl.DeviceIdType.MESH

BIG_WEIGHTS = ("ffn1_w_in", "ffn1_w_out", "w_in", "w_branch_a", "w_branch_b", "w_branch_c", "w_out",
               "ffn2_w_in", "ffn2_w_out")
ROW_SHARDED = ("ffn1_w_out", "w_out", "ffn2_w_out")


def _cparams(sem):
    return pltpu.CompilerParams(dimension_semantics=sem, vmem_limit_bytes=VMEM_LIMIT)


def _tile(n, cap):
    best, t = None, LANES
    while t <= min(n, cap):
        if n % t == 0:
            best = t
        t += LANES
    return best or n


def _rows(t, cap=256):
    r = cap
    while t % r:
        r //= 2
    return r


def _divisors(n):
    return [t for t in range(LANES, n + 1, LANES) if n % t == 0] or [n]


def _matmul_tiles(m, n, k, a_size, b_size, o_size):
    best, best_key = None, None
    for tm in _divisors(m):
        for tn in _divisors(n):
            for tk in _divisors(k):
                if tm > 1024 or tn > 3072 or tk > 4096:
                    continue
                cast = (tm * tk * 2 if a_size > 2 else 0) + (tk * tn * 2 if b_size > 2 else 0)
                need = 2 * (tm * tk * a_size + tk * tn * b_size + tm * tn * o_size) + 2 * tm * tn * 4 + cast
                if need > MATMUL_VMEM_BUDGET:
                    continue
                key = (tm * tn * tk, tk)
                if best_key is None or key > best_key:
                    best, best_key = (tm, tn, tk), key
    return best


def _dot(a, b):
    return jnp.dot(a, b, preferred_element_type=F32)


def _dot_nt(a, b):
    return lax.dot_general(a, b, (((1,), (1,)), ((), ())), preferred_element_type=F32)


def _dot_tn(a, b):
    return lax.dot_general(a, b, (((0,), (0,)), ((), ())), preferred_element_type=F32)


def _split3(x):
    h = x.astype(BF16)
    r = x - h.astype(F32)
    m = r.astype(BF16)
    lo = (r - m.astype(F32)).astype(BF16)
    return h, m, lo


def _ones_left(mat01, x):
    h, m, lo = _split3(x)
    return _dot(mat01, h) + _dot(mat01, m) + _dot(mat01, lo)


def _silu(x):
    return x * jax.nn.sigmoid(x)


def _dsilu(x):
    s = jax.nn.sigmoid(x)
    return s * (1.0 + x * (1.0 - s))


def _matmul(a, b, *, ta=False, tb=False, out_dtype=F32, name):
    if ta:
        kdim, m = a.shape
    else:
        m, kdim = a.shape
    n = b.shape[0] if tb else b.shape[1]
    tm, tn, tk = _matmul_tiles(m, n, kdim, a.dtype.itemsize, b.dtype.itemsize, jnp.dtype(out_dtype).itemsize)
    nk = kdim // tk
    ni, nj = m // tm, n // tn
    a_bytes, b_bytes = m * kdim * a.dtype.itemsize, kdim * n * b.dtype.itemsize
    j_outer = nk == 1 and (b_bytes + a_bytes * nj) < (a_bytes + b_bytes * ni)
    dims = (((0 if ta else 1,), (1 if tb else 0,)), ((), ()))

    def body(a_ref, b_ref, o_ref, *scratch):
        p = lax.dot_general(a_ref[...].astype(BF16), b_ref[...].astype(BF16), dims, preferred_element_type=F32)
        if nk == 1:
            o_ref[...] = p.astype(o_ref.dtype)
            return
        acc = scratch[0]
        k = pl.program_id(2)

        @pl.when(k == 0)
        def _():
            acc[...] = p

        @pl.when(k > 0)
        def _():
            acc[...] += p

        @pl.when(k == nk - 1)
        def _():
            o_ref[...] = acc[...].astype(o_ref.dtype)

    def spec(shape, pick):
        if j_outer:
            return pl.BlockSpec(shape, lambda j, i, k: pick(i, j, k))
        return pl.BlockSpec(shape, lambda i, j, k: pick(i, j, k))

    a_spec = spec((tk, tm), lambda i, j, k: (k, i)) if ta else spec((tm, tk), lambda i, j, k: (i, k))
    b_spec = spec((tn, tk), lambda i, j, k: (j, k)) if tb else spec((tk, tn), lambda i, j, k: (k, j))
    return pl.pallas_call(
        body, name=name, grid=(nj, ni, nk) if j_outer else (ni, nj, nk), in_specs=[a_spec, b_spec],
        out_specs=spec((tm, tn), lambda i, j, k: (i, j)),
        out_shape=jax.ShapeDtypeStruct((m, n), out_dtype),
        scratch_shapes=[pltpu.VMEM((tm, tn), F32)] if nk > 1 else [],
        compiler_params=_cparams(("parallel", "parallel", "arbitrary")),
    )(a, b)


def _rms_fwd(z, mcol, acol, res, out_dtype, name):
    t, d = z.shape
    tr = _rows(t)
    has_res = res is not None

    def body(*refs):
        if has_res:
            z_ref, m_ref, a_ref, r_ref, o_ref = refs
        else:
            z_ref, m_ref, a_ref, o_ref = refs
        zf = z_ref[...]
        r = lax.rsqrt(jnp.mean(zf * zf, axis=-1, keepdims=True) + EPS)
        y = zf * r * m_ref[...] + a_ref[...]
        if has_res:
            y = r_ref[...] + y
        o_ref[...] = y.astype(o_ref.dtype)

    row = pl.BlockSpec((tr, d), lambda i: (i, 0))
    col = pl.BlockSpec((1, d), lambda i: (0, 0))
    ins = [z, mcol, acol] + ([res] if has_res else [])
    return pl.pallas_call(
        body, name=name, grid=(t // tr,), in_specs=[row, col, col] + ([row] if has_res else []),
        out_specs=row, out_shape=jax.ShapeDtypeStruct((t, d), out_dtype),
        compiler_params=_cparams(("parallel",)),
    )(*ins)


def _rms_bwd(d_out, z, mcol, dres, out_dtype, name):
    t, d = z.shape
    tr = _rows(t)
    has_res = dres is not None

    def body(*refs):
        if has_res:
            d_ref, z_ref, m_ref, r_ref, o_ref, s1_ref, s2_ref = refs
        else:
            d_ref, z_ref, m_ref, o_ref, s1_ref, s2_ref = refs
        i = pl.program_id(0)
        zf = z_ref[...]
        r = lax.rsqrt(jnp.mean(zf * zf, axis=-1, keepdims=True) + EPS)
        zh = zf * r
        df = d_ref[...].astype(F32)
        dzh = df * m_ref[...]
        dz = r * (dzh - zh * jnp.mean(dzh * zh, axis=-1, keepdims=True))
        if has_res:
            dz = dz + r_ref[...]
        o_ref[...] = dz.astype(o_ref.dtype)
        s1 = jnp.sum(df * zh, axis=0, keepdims=True)
        s2 = jnp.sum(df, axis=0, keepdims=True)

        @pl.when(i == 0)
        def _():
            s1_ref[...] = s1
            s2_ref[...] = s2

        @pl.when(i > 0)
        def _():
            s1_ref[...] += s1
            s2_ref[...] += s2

    row = pl.BlockSpec((tr, d), lambda i: (i, 0))
    col = pl.BlockSpec((1, d), lambda i: (0, 0))
    ins = [d_out, z, mcol] + ([dres] if has_res else [])
    return pl.pallas_call(
        body, name=name, grid=(t // tr,), in_specs=[row, row, col] + ([row] if has_res else []),
        out_specs=[row, col, col],
        out_shape=[jax.ShapeDtypeStruct((t, d), out_dtype), jax.ShapeDtypeStruct((1, d), F32),
                   jax.ShapeDtypeStruct((1, d), F32)],
        compiler_params=_cparams(("arbitrary",)),
    )(*ins)


def _swiglu_fwd(u, name):
    t, f2 = u.shape
    f = f2 // 2
    tr = _rows(t)

    def body(u_ref, s_ref):
        a = u_ref[:, :f].astype(F32)
        b = u_ref[:, f:].astype(F32)
        s_ref[...] = (_silu(a) * b).astype(s_ref.dtype)

    return pl.pallas_call(
        body, name=name, grid=(t // tr,), in_specs=[pl.BlockSpec((tr, f2), lambda i: (i, 0))],
        out_specs=pl.BlockSpec((tr, f), lambda i: (i, 0)), out_shape=jax.ShapeDtypeStruct((t, f), BF16),
        compiler_params=_cparams(("parallel",)),
    )(u)


def _swiglu_bwd(u, ds, name):
    t, f2 = u.shape
    f = f2 // 2
    tr = _rows(t)

    def body(u_ref, ds_ref, du_ref):
        a = u_ref[:, :f].astype(F32)
        b = u_ref[:, f:].astype(F32)
        g = ds_ref[...].astype(F32)
        du_ref[:, :f] = (g * b * _dsilu(a)).astype(du_ref.dtype)
        du_ref[:, f:] = (g * _silu(a)).astype(du_ref.dtype)

    return pl.pallas_call(
        body, name=name, grid=(t // tr,),
        in_specs=[pl.BlockSpec((tr, f2), lambda i: (i, 0)), pl.BlockSpec((tr, f), lambda i: (i, 0))],
        out_specs=pl.BlockSpec((tr, f2), lambda i: (i, 0)), out_shape=jax.ShapeDtypeStruct((t, f2), BF16),
        compiler_params=_cparams(("parallel",)),
    )(u, ds)


def _loss_head(y, target, name):
    t, d = y.shape
    tr = _rows(t)

    def body(y_ref, t_ref, dy_ref, sq_ref):
        i = pl.program_id(0)
        e = y_ref[...] - t_ref[...]
        dy_ref[...] = e * (1.0 / d)
        s = jnp.sum(e * e, axis=0, keepdims=True)

        @pl.when(i == 0)
        def _():
            sq_ref[...] = s

        @pl.when(i > 0)
        def _():
            sq_ref[...] += s

    row = pl.BlockSpec((tr, d), lambda i: (i, 0))
    col = pl.BlockSpec((1, d), lambda i: (0, 0))
    return pl.pallas_call(
        body, name=name, grid=(t // tr,), in_specs=[row, row], out_specs=[row, col],
        out_shape=[jax.ShapeDtypeStruct((t, d), F32), jax.ShapeDtypeStruct((1, d), F32)],
        compiler_params=_cparams(("arbitrary",)),
    )(y, target)


def _hgrn_consts():
    c = A_CHUNK
    shift = SUB.bit_length() - 1
    r = lax.broadcasted_iota(jnp.int32, (c, c), 0)
    s = lax.broadcasted_iota(jnp.int32, (c, c), 1)
    sub_r = lax.shift_right_logical(r, shift)
    incl = s <= r
    masks = [jnp.logical_and(sub_r == i, incl) for i in range(c // SUB)]
    rev_incl = jnp.where(s >= r, 1.0, 0.0).astype(BF16)
    r2 = lax.broadcasted_iota(jnp.int32, (2 * c + 8, c), 0)
    s2 = lax.broadcasted_iota(jnp.int32, (2 * c + 8, c), 1)
    sub_start = lax.shift_left(lax.shift_right_logical(r2 - c, shift), shift)
    running = jnp.where(s2 <= r2, 1.0, 0.0)
    before = jnp.where(s2 < sub_start, 1.0, 0.0)
    stack = jnp.where(r2 < c, running, jnp.where(r2 < 2 * c, before, 1.0)).astype(BF16)
    return stack, masks, incl, rev_incl


def _hgrn_chunk(q_raw, f_raw, lbv, stack):
    c = A_CHUNK
    sg = jax.nn.sigmoid(f_raw)
    sgn = jax.nn.sigmoid(-f_raw)
    f = lbv + (1.0 - lbv) * sg
    logf = jnp.log(jnp.maximum(f, TINY))
    k = (1.0 - lbv) * sgn
    q = _silu(q_raw)
    bb = _ones_left(stack, logf)
    b = bb[:c]
    bsrow = bb[c:2 * c]
    b_end = bb[2 * c:2 * c + 1]
    e_sub = jnp.exp(b - bsrow)
    e_b = jnp.exp(b)
    e_end = jnp.exp(b_end - b)
    qs = q * e_sub
    q_in = q * e_b
    kend = k * e_end
    kfac = [jnp.exp(jnp.minimum(bsrow[i * SUB:i * SUB + 1] - b, EXP_CLAMP)) for i in range(c // SUB)]
    return dict(sg=sg, sgn=sgn, f=f, k=k, q=q, b=b, b_end=b_end, e_sub=e_sub, e_b=e_b, e_end=e_end,
                qs=qs, q_in=q_in, kend=kend, kfac=kfac)


def _hgrn_scores(ch, masks):
    qs_b = ch["qs"].astype(BF16)
    a = None
    for i, mk in enumerate(masks):
        ki = (ch["k"] * ch["kfac"][i]).astype(BF16)
        part = jnp.where(mk, _dot_nt(qs_b, ki), 0.0)
        a = part if a is None else a + part
    return a


def _hgrn_fwd(p, lb, hn2, name):
    t = p.shape[0]
    tb = _rows(t)
    nt = t // tb
    nc = tb // A_CHUNK
    c = A_CHUNK

    def body(q_ref, f_ref, i_ref, g_ref, lb_ref, hn_ref, y_ref, o_ref, st_ref, s_scr):
        j = pl.program_id(1)

        @pl.when(j == 0)
        def _():
            s_scr[...] = jnp.zeros_like(s_scr)

        stack, masks, _, _ = _hgrn_consts()
        states = [s_scr[0], s_scr[1]]
        for ci in range(nc):
            rows = pl.ds(ci * c, c)
            for hh in range(2):
                lsl = slice(A_KDIM * hh, A_KDIM * (hh + 1))
                hsl = slice(A_VDIM * hh, A_VDIM * (hh + 1))
                ch = _hgrn_chunk(q_ref[rows, lsl], f_ref[rows, lsl], lb_ref[:, lsl], stack)
                v = i_ref[rows, hsl].astype(BF16)
                st = states[hh]
                st_ref[hh, ci] = st
                a = _hgrn_scores(ch, masks)
                o_ref[rows, hsl] = _dot_nt(ch["q_in"].astype(BF16), st.astype(BF16)) + _dot(a.astype(BF16), v)
                states[hh] = st * jnp.exp(ch["b_end"]) + _dot_tn(v, ch["kend"].astype(BF16))
        s_scr[0] = states[0]
        s_scr[1] = states[1]
        for hh in range(2):
            hsl = slice(A_VDIM * hh, A_VDIM * (hh + 1))
            o = o_ref[:, hsl]
            r = lax.rsqrt(jnp.mean(o * o, axis=-1, keepdims=True) + EPS)
            y_ref[:, hsl] = (o * r * hn_ref[:, hsl] * _silu(g_ref[:, hsl])).astype(y_ref.dtype)

    w2 = 2 * A_KDIM
    return pl.pallas_call(
        body, name=name, grid=(A_HEADS // 2, nt),
        in_specs=[pl.BlockSpec((tb, w2), lambda h, j: (j, COL_AQ // w2 + h)),
                  pl.BlockSpec((tb, w2), lambda h, j: (j, COL_AF // w2 + h)),
                  pl.BlockSpec((tb, LANES), lambda h, j: (j, COL_AI // LANES + h)),
                  pl.BlockSpec((tb, LANES), lambda h, j: (j, COL_AG // LANES + h)),
                  pl.BlockSpec((1, w2), lambda h, j: (0, h)),
                  pl.BlockSpec((1, LANES), lambda h, j: (0, 0))],
        out_specs=[pl.BlockSpec((tb, LANES), lambda h, j: (j, h)),
                   pl.BlockSpec((tb, LANES), lambda h, j: (j, h)),
                   pl.BlockSpec((2, nc, A_VDIM, A_KDIM), lambda h, j: (h, j, 0, 0))],
        out_shape=[jax.ShapeDtypeStruct((t, A_V), BF16), jax.ShapeDtypeStruct((t, A_V), F32),
                   jax.ShapeDtypeStruct((A_HEADS, t // c, A_VDIM, A_KDIM), F32)],
        scratch_shapes=[pltpu.VMEM((2, A_VDIM, A_KDIM), F32)],
        compiler_params=_cparams(("parallel", "arbitrary")),
    )(p, p, p, p, lb, hn2)


def _hgrn_bwd(p, lb, hn2, o_raw, states, dya, name):
    t = p.shape[0]
    tb = _rows(t)
    nt = t // tb
    nc = tb // A_CHUNK
    c = A_CHUNK

    def body(q_ref, f_ref, i_ref, g_ref, lb_ref, hn_ref, o_ref, st_ref, dy_ref,
             dq_ref, df_ref, di_ref, dg_ref, dlb_ref, dhn_ref, ds_scr, do_scr):
        j = pl.program_id(1)

        @pl.when(j == 0)
        def _():
            ds_scr[...] = jnp.zeros_like(ds_scr)
            dlb_ref[...] = jnp.zeros_like(dlb_ref)
            dhn_ref[...] = jnp.zeros_like(dhn_ref)

        stack, masks, incl, rev_incl = _hgrn_consts()
        for hh in range(2):
            hsl = slice(A_VDIM * hh, A_VDIM * (hh + 1))
            o = o_ref[:, hsl]
            g = g_ref[:, hsl]
            dy = dy_ref[:, hsl].astype(F32)
            hn = hn_ref[:, hsl]
            r = lax.rsqrt(jnp.mean(o * o, axis=-1, keepdims=True) + EPS)
            oh = o * r
            sgate = _silu(g)
            dg_ref[:, hsl] = dy * oh * hn * _dsilu(g)
            dhn_ref[0, :, hsl] += jnp.sum(dy * oh * sgate, axis=0, keepdims=True)
            doh = dy * hn * sgate
            do_scr[:, hsl] = r * (doh - oh * jnp.mean(doh * oh, axis=-1, keepdims=True))

        dstates = [ds_scr[0], ds_scr[1]]
        dlb_acc = [jnp.zeros((1, A_KDIM), F32), jnp.zeros((1, A_KDIM), F32)]
        for ci in reversed(range(nc)):
            rows = pl.ds(ci * c, c)
            for hh in range(2):
                lsl = slice(A_KDIM * hh, A_KDIM * (hh + 1))
                hsl = slice(A_VDIM * hh, A_VDIM * (hh + 1))
                lbv = lb_ref[:, lsl]
                q_raw = q_ref[rows, lsl]
                f_raw = f_ref[rows, lsl]
                ch = _hgrn_chunk(q_raw, f_raw, lbv, stack)
                v = i_ref[rows, hsl].astype(BF16)
                do = do_scr[rows, hsl]
                do_b = do.astype(BF16)
                st = st_ref[hh, ci]
                st_b = st.astype(BF16)
                dst = dstates[hh]
                dst_b = dst.astype(BF16)
                qs_b = ch["qs"].astype(BF16)
                kend_b = ch["kend"].astype(BF16)
                a = _hgrn_scores(ch, masks)
                da = jnp.where(incl, _dot_nt(do_b, v), 0.0)
                dv = _dot_tn(a.astype(BF16), do_b) + _dot_nt(kend_b, dst_b)
                dq_i = None
                dk_i = None
                kdk_i = None
                for i, mk in enumerate(masks):
                    dam = jnp.where(mk, da, 0.0).astype(BF16)
                    ki = (ch["k"] * ch["kfac"][i]).astype(BF16)
                    pq = _dot(dam, ki)
                    pk = _dot_tn(dam, qs_b)
                    dq_i = pq if dq_i is None else dq_i + pq
                    dk_i = ch["kfac"][i] * pk if dk_i is None else dk_i + ch["kfac"][i] * pk
                    kdk_i = ki.astype(F32) * pk if kdk_i is None else kdk_i + ki.astype(F32) * pk
                dq_x = _dot(do_b, st_b)
                dk_x = _dot(v, dst_b)
                dq = ch["e_sub"] * dq_i + ch["e_b"] * dq_x
                dk = dk_i + ch["e_end"] * dk_x
                dstates[hh] = dst * jnp.exp(ch["b_end"]) + _dot_tn(do_b, ch["q_in"].astype(BF16))
                kx = ch["kend"] * dk_x
                db = (qs_b.astype(F32) * dq_i + ch["q_in"] * dq_x) - (kdk_i + kx)
                later = (jnp.exp(ch["b_end"]) * jnp.sum(dst * st, axis=0, keepdims=True)
                         + jnp.sum(kx, axis=0, keepdims=True))
                dlogf = later + _ones_left(rev_incl, db)
                dfv = jnp.where(ch["f"] > TINY, dlogf / ch["f"], 0.0)
                dq_ref[rows, lsl] = dq * _dsilu(q_raw)
                df_ref[rows, lsl] = (1.0 - lbv) * ch["sg"] * ch["sgn"] * (dfv - dk)
                dlb_acc[hh] = dlb_acc[hh] + jnp.sum(dfv * (1.0 - ch["sg"]) - dk * ch["sgn"], axis=0, keepdims=True)
                di_ref[rows, hsl] = dv
        for hh in range(2):
            ds_scr[hh] = dstates[hh]
            dlb_ref[:, A_KDIM * hh:A_KDIM * (hh + 1)] += dlb_acc[hh]

    w2 = 2 * A_KDIM
    rev = lambda j: nt - 1 - j
    return pl.pallas_call(
        body, name=name, grid=(A_HEADS // 2, nt),
        in_specs=[pl.BlockSpec((tb, w2), lambda h, j: (rev(j), COL_AQ // w2 + h)),
                  pl.BlockSpec((tb, w2), lambda h, j: (rev(j), COL_AF // w2 + h)),
                  pl.BlockSpec((tb, LANES), lambda h, j: (rev(j), COL_AI // LANES + h)),
                  pl.BlockSpec((tb, LANES), lambda h, j: (rev(j), COL_AG // LANES + h)),
                  pl.BlockSpec((1, w2), lambda h, j: (0, h)),
                  pl.BlockSpec((1, LANES), lambda h, j: (0, 0)),
                  pl.BlockSpec((tb, LANES), lambda h, j: (rev(j), h)),
                  pl.BlockSpec((2, nc, A_VDIM, A_KDIM), lambda h, j: (h, rev(j), 0, 0)),
                  pl.BlockSpec((tb, LANES), lambda h, j: (rev(j), h))],
        out_specs=[pl.BlockSpec((tb, w2), lambda h, j: (rev(j), h)),
                   pl.BlockSpec((tb, w2), lambda h, j: (rev(j), h)),
                   pl.BlockSpec((tb, LANES), lambda h, j: (rev(j), h)),
                   pl.BlockSpec((tb, LANES), lambda h, j: (rev(j), h)),
                   pl.BlockSpec((1, w2), lambda h, j: (0, h)),
                   pl.BlockSpec((1, 1, LANES), lambda h, j: (h, 0, 0))],
        out_shape=[jax.ShapeDtypeStruct((t, A_QK), F32), jax.ShapeDtypeStruct((t, A_QK), F32),
                   jax.ShapeDtypeStruct((t, A_V), F32), jax.ShapeDtypeStruct((t, A_V), F32),
                   jax.ShapeDtypeStruct((1, A_QK), F32), jax.ShapeDtypeStruct((A_HEADS // 2, 1, LANES), F32)],
        scratch_shapes=[pltpu.VMEM((2, A_VDIM, A_KDIM), F32), pltpu.VMEM((tb, LANES), F32)],
        compiler_params=_cparams(("parallel", "arbitrary")),
    )(p, p, p, p, lb, hn2, o_raw, states, dya)


BLK = 128
SCALE = HDIM ** -0.5
SB_CHUNK = 4


def _softplus(z):
    return jnp.maximum(z, 0.0) + jnp.log(1.0 + jnp.exp(-jnp.abs(z)))


def _split2(x):
    hi = x.astype(BF16)
    return hi, (x - hi.astype(F32)).astype(BF16)


def _sb_sum_matrix(keep):
    sp = lax.broadcasted_iota(jnp.int32, (2 * BLK, 2 * BLK), 0) & (BLK - 1)
    s = lax.broadcasted_iota(jnp.int32, (2 * BLK, 2 * BLK), 1)
    return jnp.where(jnp.logical_or(s >= BLK, keep(sp, s)), 1.0, 0.0).astype(BF16)


def _sb_fwd(p, kv, name):
    t = p.shape[0]
    nq = t // BLK
    cw = SB_CHUNK * BLK

    def body(q_ref, kb, vb, o_ref, tot_ref, zbuf, stage, sbuf, abuf):
        qi = pl.program_id(1)

        @pl.when(qi == 0)
        def _():
            abuf[...] = jnp.zeros_like(abuf)

        row = lax.broadcasted_iota(jnp.int32, (BLK, BLK), 0)
        col = lax.broadcasted_iota(jnp.int32, (BLK, BLK), 1)
        sums = _sb_sum_matrix(lambda sp, s: sp >= s)
        hsl = [slice(HDIM * h, HDIM * (h + 1)) for h in range(2)]
        nchunk = qi // SB_CHUNK + 1
        for h in range(2):
            zbuf[h] = _dot_nt((q_ref[:, hsl[h]] * SCALE).astype(BF16), kb[:, hsl[h]])

        def causal(j):
            return (col + j * BLK) < (row + qi * BLK)

        def l_pass(c, carry):
            for b in range(SB_CHUNK):
                j = c * SB_CHUNK + b
                off = pl.multiple_of(j * BLK, BLK)
                mask = causal(j)
                for h in range(2):
                    lm = jnp.where(mask, -_softplus(zbuf[h, :, pl.ds(off, BLK)]), 0.0)
                    hi, lo = _split2(lm)
                    stage[h, pl.ds(off, BLK), :BLK] = hi
                    stage[h, pl.ds(off, BLK), BLK:] = lo
            return carry

        lax.fori_loop(0, nchunk, l_pass, 0)

        def sum_pass(c, carry):
            rows = pl.ds(pl.multiple_of(c * cw, cw), cw)
            for h in range(2):
                sbuf[h, rows, :] = _dot(stage[h, rows, :], sums)
            return carry

        lax.fori_loop(0, nchunk, sum_pass, 0)

        def a_pass(it, carry):
            c = nchunk - 1 - it
            runs = list(carry)
            for b in reversed(range(SB_CHUNK)):
                j = c * SB_CHUNK + b
                off = pl.multiple_of(j * BLK, BLK)
                mask = causal(j)
                for h in range(2):
                    s = sbuf[h, pl.ds(off, BLK), :BLK]
                    a = jnp.where(mask, jnp.exp(zbuf[h, :, pl.ds(off, BLK)] + s + runs[h]), 0.0)
                    abuf[h, :, pl.ds(off, BLK)] = a.astype(BF16)
                    runs[h] = runs[h] + sbuf[h, pl.ds(off, BLK), BLK:]
            return tuple(runs)

        zero = jnp.zeros((BLK, BLK), F32)
        runs = lax.fori_loop(0, nchunk, a_pass, (zero, zero))
        for h in range(2):
            tot_ref[:, hsl[h]] = runs[h][:, :HDIM]
            o_ref[:, hsl[h]] = _dot(abuf[h], vb[:, hsl[h]])

    out_blk = pl.BlockSpec((BLK, LANES), lambda h, i: (i, h))
    return pl.pallas_call(
        body, name=name, grid=(B_HEADS // 2, nq),
        in_specs=[pl.BlockSpec((BLK, LANES), lambda h, i: (i, COL_BQ // LANES + h)),
                  pl.BlockSpec((t, LANES), lambda h, i: (0, h)),
                  pl.BlockSpec((t, LANES), lambda h, i: (0, B_W // LANES + h))],
        out_specs=[out_blk, out_blk],
        out_shape=[jax.ShapeDtypeStruct((t, B_W), F32)] * 2,
        scratch_shapes=[pltpu.VMEM((2, BLK, t), F32), pltpu.VMEM((2, t, 2 * BLK), BF16),
                        pltpu.VMEM((2, t, 2 * BLK), F32), pltpu.VMEM((2, BLK, t), BF16)],
        compiler_params=_cparams(("parallel", "arbitrary")),
    )(p, kv, kv)


def _sb_bwd(p, kv, tot, do, name):
    t = p.shape[0]
    nq = t // BLK
    cw = SB_CHUNK * BLK

    def body(q_ref, kb, vb, tot_ref, do_ref, dq_ref, dk_ref, dv_ref, zbuf, dabuf, stage, gstage, sbuf, abuf, dzbuf,
             dkt, dvt):
        qi = pl.program_id(1)

        @pl.when(qi == 0)
        def _():
            dkt[...] = jnp.zeros_like(dkt)
            dvt[...] = jnp.zeros_like(dvt)
            dzbuf[...] = jnp.zeros_like(dzbuf)
            abuf[...] = jnp.zeros_like(abuf)

        row = lax.broadcasted_iota(jnp.int32, (BLK, BLK), 0)
        col = lax.broadcasted_iota(jnp.int32, (BLK, BLK), 1)
        sums = _sb_sum_matrix(lambda sp, s: sp <= s)
        hsl = [slice(HDIM * h, HDIM * (h + 1)) for h in range(2)]
        dob = [do_ref[:, hsl[h]].astype(BF16) for h in range(2)]
        total =[jnp.concatenate([tot_ref[:, hsl[h]], tot_ref[:, hsl[h]]], axis=1) for h in range(2)]
        nchunk = qi // SB_CHUNK + 1
        for h in range(2):
            zbuf[h] = _dot_nt((q_ref[:, hsl[h]] * SCALE).astype(BF16), kb[:, hsl[h]])
            dabuf[h] = _dot_nt(dob[h], vb[:, hsl[h]])

        def causal(j):
            return (col + j * BLK) < (row + qi * BLK)

        def blocks(c):
            for b in range(SB_CHUNK):
                j = c * SB_CHUNK + b
                yield j, pl.ds(pl.multiple_of(j * BLK, BLK), BLK)

        def l_pass(c, carry):
            for j, blk_ in blocks(c):
                mask = causal(j)
                for h in range(2):
                    lm = jnp.where(mask, -_softplus(zbuf[h, :, blk_]), 0.0)
                    hi, lo = _split2(lm)
                    stage[h, blk_, :BLK] = hi
                    stage[h, blk_, BLK:] = lo
            return carry

        lax.fori_loop(0, nchunk, l_pass, 0)

        def sum_pass(src):
            def run_(c, carry):
                rows = pl.ds(pl.multiple_of(c * cw, cw), cw)
                for h in range(2):
                    sbuf[h, rows, :] = _dot(src[h, rows, :], sums)
                return carry
            lax.fori_loop(0, nchunk, run_, 0)

        sum_pass(stage)

        def g_pass(c, carry):
            runs = list(carry)
            for j, blk_ in blocks(c):
                mask = causal(j)
                for h in range(2):
                    lm = stage[h, blk_, :BLK].astype(F32) + stage[h, blk_, BLK:].astype(F32)
                    log_a = zbuf[h, :, blk_] + lm + (total[h] - runs[h] - sbuf[h, blk_, :BLK])
                    a = jnp.where(mask, jnp.exp(log_a), 0.0)
                    abuf[h, :, blk_] = a.astype(BF16)
                    hi, lo = _split2(a * dabuf[h, :, blk_])
                    gstage[h, blk_, :BLK] = hi
                    gstage[h, blk_, BLK:] = lo
                    runs[h] = runs[h] + sbuf[h, blk_, BLK:]
            return tuple(runs)

        zero = jnp.zeros((BLK, BLK), F32)
        lax.fori_loop(0, nchunk, g_pass, (zero, zero))
        sum_pass(gstage)

        def dz_pass(c, carry):
            runs = list(carry)
            for j, blk_ in blocks(c):
                mask = causal(j)
                for h in range(2):
                    lm = stage[h, blk_, :BLK].astype(F32) + stage[h, blk_, BLK:].astype(F32)
                    g = gstage[h, blk_, :BLK].astype(F32) + gstage[h, blk_, BLK:].astype(F32)
                    before = runs[h] + sbuf[h, blk_, :BLK] - g
                    dz = jnp.where(mask, g * jnp.exp(lm) - jnp.exp(zbuf[h, :, blk_] + lm) * before, 0.0)
                    dzbuf[h, :, blk_] = (dz * SCALE).astype(BF16)
                    runs[h] = runs[h] + sbuf[h, blk_, BLK:]
            return tuple(runs)

        lax.fori_loop(0, nchunk, dz_pass, (zero, zero))
        for h in range(2):
            dq_ref[:, hsl[h]] = _dot(dzbuf[h], kb[:, hsl[h]])
        q_t = q_ref[...].T.astype(BF16)
        do_t = do_ref[...].T.astype(BF16)
        for h in range(2):
            dkt[hsl[h], :] += _dot(q_t[hsl[h], :], dzbuf[h])
            dvt[hsl[h], :] += _dot(do_t[hsl[h], :], abuf[h])

        @pl.when(qi == nq - 1)
        def _():
            dk_ref[...] = dkt[...].T
            dv_ref[...] = dvt[...].T

    blk = lambda h, i: (i, h)
    whole = lambda h, i: (0, h)
    return pl.pallas_call(
        body, name=name, grid=(B_HEADS // 2, nq),
        in_specs=[pl.BlockSpec((BLK, LANES), lambda h, i: (i, COL_BQ // LANES + h)),
                  pl.BlockSpec((t, LANES), lambda h, i: (0, h)),
                  pl.BlockSpec((t, LANES), lambda h, i: (0, B_W // LANES + h)),
                  pl.BlockSpec((BLK, LANES), blk), pl.BlockSpec((BLK, LANES), blk)],
        out_specs=[pl.BlockSpec((BLK, LANES), blk), pl.BlockSpec((t, LANES), whole),
                   pl.BlockSpec((t, LANES), whole)],
        out_shape=[jax.ShapeDtypeStruct((t, B_W), F32)] * 3,
        scratch_shapes=[pltpu.VMEM((2, BLK, t), F32), pltpu.VMEM((2, BLK, t), F32),
                        pltpu.VMEM((2, t, 2 * BLK), BF16), pltpu.VMEM((2, t, 2 * BLK), BF16),
                        pltpu.VMEM((2, t, 2 * BLK), F32), pltpu.VMEM((2, BLK, t), BF16),
                        pltpu.VMEM((2, BLK, t), BF16), pltpu.VMEM((LANES, t), F32), pltpu.VMEM((LANES, t), F32)],
        compiler_params=_cparams(("parallel", "arbitrary")),
    )(p, kv, kv, tot, do)


def _alibi_slopes(n):
    def pow2(m):
        start = 2.0 ** (-8.0 / m)
        return [start ** (i + 1) for i in range(m)]
    if math.log2(n).is_integer():
        s = pow2(n)
    else:
        c = 2 ** int(math.floor(math.log2(n)))
        s = pow2(c) + pow2(2 * c)[0::2][: n - c]
    return sorted(s, reverse=True)


def _dil_scores(qh, kh, sl, prev, exists=None):
    row = lax.broadcasted_iota(jnp.int32, (BLK, BLK), 0)
    col = lax.broadcasted_iota(jnp.int32, (BLK, BLK), 1)
    dist = row - col + (BLK if prev else 0)
    if prev:
        valid = (col - row) >= jnp.where(exists, 0, 2 * BLK)
    else:
        valid = col <= row
    s = _dot_nt(qh, kh) - sl * dist.astype(F32)
    return s, valid


def _dil_fwd(q, k, v, slope_cols, name):
    ln, cw = q.shape
    nb = ln // BLK

    def body(q_ref, kc_ref, kp_ref, vc_ref, vp_ref, sl_ref, o_ref, lse_ref):
        i = pl.program_id(1)
        for h in range(2):
            hs = slice(HDIM * h, HDIM * (h + 1))
            sl = sl_ref[:, HDIM * h:HDIM * h + 1]
            qh = (q_ref[:, hs] * SCALE).astype(BF16)
            sc, vc_ok = _dil_scores(qh, kc_ref[:, hs].astype(BF16), sl, False)
            sp, vp_ok = _dil_scores(qh, kp_ref[:, hs].astype(BF16), sl, True, i > 0)
            sc = jnp.where(vc_ok, sc, NEG_BIG)
            sp = jnp.where(vp_ok, sp, NEG_BIG)
            m = jnp.maximum(jnp.max(sc, axis=1, keepdims=True), jnp.max(sp, axis=1, keepdims=True))
            pc = jnp.exp(sc - m)
            pp = jnp.exp(sp - m)
            den = jnp.sum(pc, axis=1, keepdims=True) + jnp.sum(pp, axis=1, keepdims=True)
            o = _dot(pc.astype(BF16), vc_ref[:, hs].astype(BF16)) + _dot(pp.astype(BF16), vp_ref[:, hs].astype(BF16))
            o_ref[:, hs] = o / den
            lse_ref[:, hs] = jnp.broadcast_to(m + jnp.log(den), (BLK, HDIM))

    cur = pl.BlockSpec((BLK, LANES), lambda c, i: (i, c))
    prv = pl.BlockSpec((BLK, LANES), lambda c, i: (jnp.maximum(i - 1, 0), c))
    return pl.pallas_call(
        body, name=name, grid=(cw // LANES, nb),
        in_specs=[cur, cur, prv, cur, prv, pl.BlockSpec((1, LANES), lambda c, i: (0, c))],
        out_specs=[cur, cur], out_shape=[jax.ShapeDtypeStruct((ln, cw), F32)] * 2,
        compiler_params=_cparams(("parallel", "parallel")),
    )(q, k, k, v, v, slope_cols)


def _dil_bwd(q, k, v, do, o, lse, slope_cols, name):
    ln, cw = q.shape
    nb = ln // BLK

    def body(q_ref, qn_ref, kc_ref, kp_ref, vc_ref, vp_ref, do_ref, don_ref, o_ref, on_ref, l_ref, ln_ref, sl_ref,
             dq_ref, dk_ref, dv_ref):
        i = pl.program_id(1)
        has_prev = i > 0
        has_next = i < nb - 1
        for h in range(2):
            hs = slice(HDIM * h, HDIM * (h + 1))
            sl = sl_ref[:, HDIM * h:HDIM * h + 1]
            qb = q_ref[:, hs].astype(BF16)
            qnb = qn_ref[:, hs].astype(BF16)
            qh = (q_ref[:, hs] * SCALE).astype(BF16)
            qnh = (qn_ref[:, hs] * SCALE).astype(BF16)
            kc = kc_ref[:, hs].astype(BF16)
            kp = kp_ref[:, hs].astype(BF16)
            vc = vc_ref[:, hs].astype(BF16)
            vp = vp_ref[:, hs].astype(BF16)
            do_f = do_ref[:, hs]
            don_f = don_ref[:, hs]
            dob = do_f.astype(BF16)
            donb = don_f.astype(BF16)
            delta = jnp.sum(do_f * o_ref[:, hs], axis=1, keepdims=True)
            deltan = jnp.sum(don_f * on_ref[:, hs], axis=1, keepdims=True)
            lse = l_ref[:, HDIM * h:HDIM * h + 1]
            lsen = ln_ref[:, HDIM * h:HDIM * h + 1]
            s, ok = _dil_scores(qh, kc, sl, False)
            p_cc = jnp.where(ok, jnp.exp(jnp.where(ok, s, NEG_BIG) - lse), 0.0)
            ds_cc = p_cc * (_dot_nt(dob, vc) - delta)
            s, ok = _dil_scores(qh, kp, sl, True, has_prev)
            p_cp = jnp.where(ok, jnp.exp(jnp.where(ok, s, NEG_BIG) - lse), 0.0)
            ds_cp = p_cp * (_dot_nt(dob, vp) - delta)
            s, ok = _dil_scores(qnh, kc, sl, True, has_next)
            p_nc = jnp.where(ok, jnp.exp(jnp.where(ok, s, NEG_BIG) - lsen), 0.0)
            ds_nc = p_nc * (_dot_nt(donb, vc) - deltan)
            ds_cc_b = (ds_cc * SCALE).astype(BF16)
            ds_cp_b = (ds_cp * SCALE).astype(BF16)
            ds_nc_b = (ds_nc * SCALE).astype(BF16)
            dq_ref[:, hs] = _dot(ds_cc_b, kc) + _dot(ds_cp_b, kp)
            dk_ref[:, hs] = _dot_tn(ds_cc_b, qb) + _dot_tn(ds_nc_b, qnb)
            dv_ref[:, hs] = _dot_tn(p_cc.astype(BF16), dob) + _dot_tn(p_nc.astype(BF16), donb)

    cur = pl.BlockSpec((BLK, LANES), lambda c, i: (i, c))
    prv = pl.BlockSpec((BLK, LANES), lambda c, i: (jnp.maximum(i - 1, 0), c))
    nxt = pl.BlockSpec((BLK, LANES), lambda c, i: (jnp.minimum(i + 1, nb - 1), c))
    return pl.pallas_call(
        body, name=name, grid=(cw // LANES, nb),
        in_specs=[cur, nxt, cur, prv, cur, prv, cur, nxt, cur, nxt, cur, nxt,
                  pl.BlockSpec((1, LANES), lambda c, i: (0, c))],
        out_specs=[cur, cur, cur], out_shape=[jax.ShapeDtypeStruct((ln, cw), F32)] * 3,
        compiler_params=_cparams(("parallel", "parallel")),
    )(q, q, k, k, v, v, do, do, o, o, lse, lse, slope_cols)


def _dil_merge(os_, ls_, name):
    t, w = os_[0].shape
    tr = _rows(t)

    def body(o0, o1, o2, l0, l1, l2, y_ref, lse_ref):
        a, b, c = l0[...], l1[...], l2[...]
        m = jnp.maximum(jnp.maximum(a, b), c)
        ea, eb, ec = jnp.exp(a - m), jnp.exp(b - m), jnp.exp(c - m)
        den = ea + eb + ec
        y_ref[...] = (ea * o0[...] + eb * o1[...] + ec * o2[...]) / den
        lse_ref[...] = m + jnp.log(den)

    row = pl.BlockSpec((tr, w), lambda i: (i, 0))
    return pl.pallas_call(
        body, name=name, grid=(t // tr,), in_specs=[row] * 6, out_specs=[row, row],
        out_shape=[jax.ShapeDtypeStruct((t, w), F32)] * 2, compiler_params=_cparams(("parallel",)),
    )(*os_, *ls_)


def _gate_fwd(ys, gl, ws, name):
    t = gl.shape[0]
    d = gl.shape[1] // N_BRANCH
    tr = _rows(t)

    def body(ya, yb, yc, gl_ref, wa, wb, wc, m_ref):
        acc = None
        for i, (y, w) in enumerate(((ya, wa), (yb, wb), (yc, wc))):
            z = _dot(y[...].astype(BF16), w[...])
            term = jax.nn.sigmoid(gl_ref[:, i * d:(i + 1) * d]) * z
            acc = term if acc is None else acc + term
        m_ref[...] = acc.astype(m_ref.dtype)

    rows = [pl.BlockSpec((tr, y.shape[1]), lambda i: (i, 0)) for y in ys]
    wsp = [pl.BlockSpec(w.shape, lambda i: (0, 0)) for w in ws]
    return pl.pallas_call(
        body, name=name, grid=(t // tr,),
        in_specs=rows + [pl.BlockSpec((tr, N_BRANCH * d), lambda i: (i, 0))] + wsp,
        out_specs=pl.BlockSpec((tr, d), lambda i: (i, 0)), out_shape=jax.ShapeDtypeStruct((t, d), BF16),
        compiler_params=_cparams(("parallel",)),
    )(*ys, gl, *ws)


def _gate_bwd(dm, ys, gl, ws, name):
    t = gl.shape[0]
    d = gl.shape[1] // N_BRANCH
    tr = _rows(t)

    def body(dm_ref, ya, yb, yc, gl_ref, wa, wb, wc, dya, dyb, dyc, dgl_ref, dwa, dwb, dwc):
        step = pl.program_id(0)
        dmv = dm_ref[...].astype(F32)
        for i, (y, w, dy, dw) in enumerate(((ya, wa, dya, dwa), (yb, wb, dyb, dwb), (yc, wc, dyc, dwc))):
            yb16 = y[...].astype(BF16)
            z = _dot(yb16, w[...])
            sg = jax.nn.sigmoid(gl_ref[:, i * d:(i + 1) * d])
            dgl_ref[:, i * d:(i + 1) * d] = dmv * z * sg * (1.0 - sg)
            e = (dmv * sg).astype(BF16)
            dy[...] = _dot_nt(e, w[...])
            contrib = _dot_tn(yb16, e)

            @pl.when(step == 0)
            def _(dw=dw, contrib=contrib):
                dw[...] = contrib

            @pl.when(step > 0)
            def _(dw=dw, contrib=contrib):
                dw[...] += contrib

    rows = [pl.BlockSpec((tr, y.shape[1]), lambda i: (i, 0)) for y in ys]
    wsp = [pl.BlockSpec(w.shape, lambda i: (0, 0)) for w in ws]
    gsp = pl.BlockSpec((tr, N_BRANCH * d), lambda i: (i, 0))
    return pl.pallas_call(
        body, name=name, grid=(t // tr,),
        in_specs=[pl.BlockSpec((tr, d), lambda i: (i, 0))] + rows + [gsp] + wsp,
        out_specs=rows + [gsp] + wsp,
        out_shape=[jax.ShapeDtypeStruct(y.shape, F32) for y in ys] + [jax.ShapeDtypeStruct(gl.shape, F32)]
        + [jax.ShapeDtypeStruct(w.shape, F32) for w in ws],
        compiler_params=_cparams(("arbitrary",)),
    )(dm, *ys, gl, *ws)


def _adamw(w, m, v, gparts, name):
    r, c = w.shape
    n = gparts.shape[0]
    br = LANES if r % LANES == 0 else r
    c1 = 1.0 - ADAM_B1 ** ADAM_STEP
    c2 = 1.0 - ADAM_B2 ** ADAM_STEP

    def body(w_ref, m_ref, v_ref, g_ref, go_ref, d_ref, mo_ref, vo_ref):
        g = g_ref[0].astype(F32)
        for i in range(1, n):
            g = g + g_ref[i].astype(F32)
        mn = ADAM_B1 * m_ref[...] + (1.0 - ADAM_B1) * g
        vn = ADAM_B2 * v_ref[...] + (1.0 - ADAM_B2) * (g * g)
        go_ref[...] = g
        mo_ref[...] = mn
        vo_ref[...] = vn
        d_ref[...] = -ADAM_LR * ((mn / c1) / (jnp.sqrt(vn / c2) + ADAM_EPS) + ADAM_WD * w_ref[...])

    blk = pl.BlockSpec((br, c), lambda i: (i, 0))
    return pl.pallas_call(
        body, name=name, grid=(r // br,),
        in_specs=[blk, blk, blk, pl.BlockSpec((n, br, c), lambda i: (0, i, 0))],
        out_specs=[blk] * 4, out_shape=[jax.ShapeDtypeStruct((r, c), F32)] * 4,
        compiler_params=_cparams(("parallel",)),
    )(w, m, v, gparts)


def _my_coords():
    return lax.axis_index("x"), lax.axis_index("y"), lax.axis_index("c")


def _all_gather(x_shard, in_vmem, with_sum, name):
    m_per, n = x_shard.shape

    def body(x_ref, out_ref, *rest):
        if with_sum:
            sum_ref, send_sems, recv_sems, local_sem = rest
        else:
            send_sems, recv_sems, local_sem = rest
        x, y, c = _my_coords()
        me, sibling = (x, y, c), (x, y, 1 - c)
        chips = [(1 - x, y), (x, 1 - y), (1 - x, 1 - y)]

        def rows(px, py, pc):
            return out_ref.at[pl.ds((4 * px + 2 * py + pc) * m_per, m_per), :]

        def copy(k, block, to, src=None):
            return pltpu.make_async_remote_copy(
                src_ref=rows(*block) if src is None else src, dst_ref=rows(*block),
                send_sem=send_sems.at[k], recv_sem=recv_sems.at[k], device_id=to, device_id_type=MESH)

        mine = pltpu.make_async_copy(x_ref, rows(*me), local_sem)
        mine.start()
        first = [copy(0, me, sibling, src=x_ref)]
        first += [copy(1 + j, me, (*chip, c), src=x_ref) for j, chip in enumerate(chips)]
        for cp in first:
            cp.start()
        passed = [copy(4 + j, (*chip, c), sibling) for j, chip in enumerate(chips)]
        for j, chip in enumerate(chips):
            copy(1 + j, (*chip, c), me).wait_recv()
            passed[j].start()
        copy(0, sibling, me).wait_recv()
        for j, chip in enumerate(chips):
            copy(4 + j, (*chip, 1 - c), me).wait_recv()
        for cp in first + passed:
            cp.wait_send()
        mine.wait()
        if with_sum:
            acc = out_ref[pl.ds(0, m_per), :]
            for d in range(1, N_DEV):
                acc = acc + out_ref[pl.ds(d * m_per, m_per), :]
            sum_ref[...] = acc

    space = pltpu.VMEM if in_vmem else pl.ANY
    out_shape = [jax.ShapeDtypeStruct((N_DEV * m_per, n), x_shard.dtype)]
    out_specs = [pl.BlockSpec(memory_space=space)]
    if with_sum:
        out_shape.append(jax.ShapeDtypeStruct((m_per, n), x_shard.dtype))
        out_specs.append(pl.BlockSpec(memory_space=pltpu.VMEM))
    res = pl.pallas_call(
        body, name=name, out_shape=out_shape, in_specs=[pl.BlockSpec(memory_space=space)], out_specs=out_specs,
        scratch_shapes=[pltpu.SemaphoreType.DMA((7,)), pltpu.SemaphoreType.DMA((7,)), pltpu.SemaphoreType.DMA],
        compiler_params=pltpu.CompilerParams(vmem_limit_bytes=VMEM_LIMIT),
    )(x_shard)
    return res if with_sum else res[0]


def _all_to_all(send, name):
    _, r, c = send.shape

    def body(send_ref, recv_ref, send_sems, recv_sems, local_sem):
        x, y, cc = _my_coords()
        me = 4 * x + 2 * y + cc
        mine = pltpu.make_async_copy(send_ref.at[me], recv_ref.at[me], local_sem)
        mine.start()
        copies = []
        for k in range(1, N_DEV):
            px = 1 - x if k & 4 else x
            py = 1 - y if k & 2 else y
            pc = 1 - cc if k & 1 else cc
            peer = 4 * px + 2 * py + pc
            cp = pltpu.make_async_remote_copy(
                src_ref=send_ref.at[peer], dst_ref=recv_ref.at[me],
                send_sem=send_sems.at[k - 1], recv_sem=recv_sems.at[k - 1],
                device_id=(px, py, pc), device_id_type=MESH)
            cp.start()
            copies.append((cp, peer, (px, py, pc)))
        for k, (cp, peer, pid) in enumerate(copies):
            cp.wait_send()
            pltpu.make_async_remote_copy(
                src_ref=send_ref.at[me], dst_ref=recv_ref.at[peer],
                send_sem=send_sems.at[k], recv_sem=recv_sems.at[k], device_id=pid, device_id_type=MESH).wait_recv()
        mine.wait()

    return pl.pallas_call(
        body, name=name, out_shape=jax.ShapeDtypeStruct(send.shape, send.dtype),
        in_specs=[pl.BlockSpec(memory_space=pl.ANY)], out_specs=pl.BlockSpec(memory_space=pl.ANY),
        scratch_shapes=[pltpu.SemaphoreType.DMA((7,)), pltpu.SemaphoreType.DMA((7,)), pltpu.SemaphoreType.DMA],
    )(send)


def _row(v):
    return v.reshape(1, -1)


def _ffn_fwd(x, w_in, w_out, g_pre, g_post, m, res_w, tag):
    shift, scale, gate = m[0], m[1], m[2]
    mpre = _row(g_pre * (1.0 + scale))
    mpost = _row(res_w * gate * g_post)
    h = _rms_fwd(x, mpre, _row(shift), None, BF16, tag + "_pre")
    u = _matmul(h, w_in, out_dtype=BF16, name=tag + "_in")
    s = _swiglu_fwd(u, tag + "_act")
    y = _matmul(s, w_out, name=tag + "_out")
    x_new = _rms_fwd(y, mpost, jnp.zeros_like(mpost), x, F32, tag + "_post")
    return x_new, (x, h, u, s, y, mpre, mpost)


def _sub_bwd_post(dx_new, y, mpost, g_post, gate, res_w, tag):
    dy, c1, _ = _rms_bwd(dx_new, y, mpost, None, BF16, tag + "_post_bwd")
    c1 = c1[0]
    return dy, c1 * res_w * g_post, c1 * res_w * gate


def _sub_bwd_pre(dh, x, mpre, dx_new, g_pre, scale, tag):
    dx, c2, c3 = _rms_bwd(dh, x, mpre, dx_new, F32, tag + "_pre_bwd")
    c2, c3 = c2[0], c3[0]
    return dx, c3, c2 * g_pre, c2 * (1.0 + scale)


def _ffn_bwd(dx_new, saved, w_in, w_out, g_pre, g_post, m, res_w, tag):
    x, h, u, s, y, mpre, mpost = saved
    scale, gate = m[1], m[2]
    dy, dgate, dg_post = _sub_bwd_post(dx_new, y, mpost, g_post, gate, res_w, tag)
    ds = _matmul(dy, w_out, tb=True, out_dtype=BF16, name=tag + "_out_dx")
    dw_out = _matmul(s, dy, ta=True, out_dtype=BF16, name=tag + "_out_dw")
    du = _swiglu_bwd(u, ds, tag + "_act_bwd")
    dh = _matmul(du, w_in, tb=True, name=tag + "_in_dx")
    dw_in = _matmul(h, du, ta=True, out_dtype=BF16, name=tag + "_in_dw")
    dx, dshift, dscale, dg_pre = _sub_bwd_pre(dh, x, mpre, dx_new, g_pre, scale, tag)
    return dx, dw_in, dw_out, jnp.stack([dshift, dscale, dgate]), dg_pre, dg_post


def _slope_cols(gi):
    _, r = C_GROUPS[gi]
    sl = jnp.asarray(_alibi_slopes(C_HEADS)[gi * C_HPG:(gi + 1) * C_HPG], F32) * float(r)
    return jnp.tile(jnp.repeat(sl, HDIM), r).reshape(1, r * C_OUT)


def _group_view(a, gi):
    _, r = C_GROUPS[gi]
    t = a.shape[0]
    return a.reshape(t // r, r * a.shape[1])


def _mix_fwd(x, w, g_pre, g_post, m, lb, hn, tag):
    t, d = x.shape
    shift, scale, gate = m[0], m[1], m[2]
    mpre = _row(g_pre * (1.0 + scale))
    mpost = _row(gate * g_post)
    h = _rms_fwd(x, mpre, _row(shift), None, BF16, tag + "_pre")
    p = _matmul(h, w["w_in"], name=tag + "_in")
    hn2 = _row(jnp.tile(hn, 2))
    ya, oa, states = _hgrn_fwd(p, _row(lb), hn2, tag + "_hgrn")
    kv = p[:, COL_BK:COL_CQ].astype(BF16)
    yb, sb_tot = _sb_fwd(p, kv, tag + "_sb")
    og, lg, qkv = [], [], []
    for gi in range(len(C_GROUPS)):
        cs = slice(gi * C_OUT, (gi + 1) * C_OUT)
        q = _group_view(p[:, COL_CQ:COL_CK][:, cs], gi)
        k = _group_view(p[:, COL_CK:COL_CV][:, cs], gi)
        v = _group_view(p[:, COL_CV:COL_GATE][:, cs], gi)
        o, lse = _dil_fwd(q, k, v, _slope_cols(gi), tag + "_dil%d" % gi)
        og.append(o.reshape(t, C_OUT))
        lg.append(lse.reshape(t, C_OUT))
        qkv.append((q, k, v))
    yc, lse_c = _dil_merge(og, lg, tag + "_dil_merge")
    gl = p[:, COL_GATE:]
    ws = (w["w_branch_a"], w["w_branch_b"], w["w_branch_c"])
    merged = _gate_fwd((ya, yb, yc), gl, ws, tag + "_gate")
    y = _matmul(merged, w["w_out"], name=tag + "_out")
    x_new = _rms_fwd(y, mpost, jnp.zeros_like(mpost), x, F32, tag + "_post")
    return x_new, (x, h, p, hn2, ya, oa, states, yb, kv, sb_tot, qkv, yc, lse_c, gl, merged, y, mpre, mpost)


def _mix_bwd(dx_new, saved, w, g_pre, g_post, m, lb, tag):
    x, h, p, hn2, ya, oa, states, yb, kv, sb_tot, qkv, yc, lse_c, gl, merged, y, mpre, mpost = saved
    t = x.shape[0]
    scale, gate = m[1], m[2]
    dy, dgate, dg_post = _sub_bwd_post(dx_new, y, mpost, g_post, gate, 1.0, tag)
    dmerged = _matmul(dy, w["w_out"], tb=True, out_dtype=BF16, name=tag + "_out_dx")
    dw_out = _matmul(merged, dy, ta=True, out_dtype=BF16, name=tag + "_out_dw")
    ws = (w["w_branch_a"], w["w_branch_b"], w["w_branch_c"])
    dya, dyb, dyc, dgl, dwa, dwb, dwc = _gate_bwd(dmerged, (ya, yb, yc), gl, ws, tag + "_gate_bwd")
    dqa, dfa, dia, dga, dlb, dhn = _hgrn_bwd(p, _row(lb), hn2, oa, states, dya, tag + "_hgrn_bwd")
    dbq, dbk, dbv = _sb_bwd(p, kv, sb_tot, dyb, tag + "_sb_bwd")
    dcq, dck, dcv = [], [], []
    for gi in range(len(C_GROUPS)):
        q, k, v = qkv[gi]
        dq, dk, dv = _dil_bwd(q, k, v, _group_view(dyc, gi), _group_view(yc, gi), _group_view(lse_c, gi),
                              _slope_cols(gi), tag + "_dil%d_bwd" % gi)
        dcq.append(dq.reshape(t, C_OUT))
        dck.append(dk.reshape(t, C_OUT))
        dcv.append(dv.reshape(t, C_OUT))
    dp = jnp.concatenate([dqa, dfa, dia, dga, dbq, dbk, dbv] + dcq + dck + dcv + [dgl], axis=1).astype(BF16)
    dh = _matmul(dp, w["w_in"], tb=True, name=tag + "_in_dx")
    dw_in = _matmul(h, dp, ta=True, out_dtype=BF16, name=tag + "_in_dw")
    dx, dshift, dscale, dg_pre = _sub_bwd_pre(dh, x, mpre, dx_new, g_pre, scale, tag)
    dhn_v = jnp.sum(dhn, axis=(0, 1))
    dhn_v = dhn_v[:A_VDIM] + dhn_v[A_VDIM:]
    dws = dict(w_in=dw_in, w_out=dw_out, w_branch_a=dwa.astype(BF16), w_branch_b=dwb.astype(BF16),
               w_branch_c=dwc.astype(BF16))
    return dx, dws, jnp.stack([dshift, dscale, dgate]), dg_pre, dg_post, dlb[0], dhn_v


def _local_step(x, target, mod, norm_g, lb_all, hnorm, wts):
    depth = mod.shape[0]
    d = x.shape[1]
    saved = []
    for l in range(depth):
        wl = {k: v[l] for k, v in wts.items()}
        x, s0 = _ffn_fwd(x, wl["ffn1_w_in"], wl["ffn1_w_out"], norm_g[l, 0], norm_g[l, 1], mod[l, 0], 0.5, "ffn1")
        x, s1 = _mix_fwd(x, wl, norm_g[l, 2], norm_g[l, 3], mod[l, 1], lb_all[l], hnorm[l], "mix")
        x, s2 = _ffn_fwd(x, wl["ffn2_w_in"], wl["ffn2_w_out"], norm_g[l, 4], norm_g[l, 5], mod[l, 2], 0.5, "ffn2")
        saved.append((s0, s1, s2))
    dx, sq = _loss_head(x, target, "loss_head")
    loss = 0.5 * jnp.sum(sq) / d
    dmod, dng, dlb, dhn = [], [], [], []
    dws = {k: [] for k in wts}
    for l in reversed(range(depth)):
        wl = {k: v[l] for k, v in wts.items()}
        s0, s1, s2 = saved[l]
        dx, dwi2, dwo2, dm2, dgp2, dgq2 = _ffn_bwd(dx, s2, wl["ffn2_w_in"], wl["ffn2_w_out"], norm_g[l, 4],
                                                   norm_g[l, 5], mod[l, 2], 0.5, "ffn2")
        dx, dwm, dm1, dgp1, dgq1, dlb_l, dhn_l = _mix_bwd(dx, s1, wl, norm_g[l, 2], norm_g[l, 3], mod[l, 1],
                                                          lb_all[l], "mix")
        dx, dwi1, dwo1, dm0, dgp0, dgq0 = _ffn_bwd(dx, s0, wl["ffn1_w_in"], wl["ffn1_w_out"], norm_g[l, 0],
                                                   norm_g[l, 1], mod[l, 0], 0.5, "ffn1")
        dmod.append(jnp.stack([dm0, dm1, dm2]))
        dng.append(jnp.stack([dgp0, dgq0, dgp1, dgq1, dgp2, dgq2]))
        dlb.append(dlb_l)
        dhn.append(dhn_l)
        dws["ffn1_w_in"].append(dwi1)
        dws["ffn1_w_out"].append(dwo1)
        dws["ffn2_w_in"].append(dwi2)
        dws["ffn2_w_out"].append(dwo2)
        for k, g in dwm.items():
            dws[k].append(g)
    rev = lambda lst: jnp.stack(lst[::-1])
    return (loss, dx, rev(dmod), rev(dng), rev(dlb), rev(dhn), {k: v[::-1] for k, v in dws.items()})


def _lb_all(logits):
    lb_p = jax.nn.softmax(logits.astype(F32), axis=0)
    return jnp.cumsum(lb_p, axis=0) - lb_p[0:1]


def _pad_rows(a, rows):
    return jnp.pad(a, ((0, rows - a.shape[0]), (0, 0)))


def _gather_weights(shards):
    flat = jnp.concatenate([shards[k].astype(BF16).reshape(-1, 1024) for k in BIG_WEIGHTS], axis=0)
    got = _all_gather(flat, False, False, "weights_all_gather").reshape(N_DEV, flat.shape[0], 1024)
    out, off = {}, 0
    for k in BIG_WEIGHTS:
        depth, r, c = shards[k].shape
        nrow = depth * r * c // 1024
        blk = got[:, off:off + nrow].reshape(N_DEV, depth, r, c)
        off += nrow
        if k in ROW_SHARDED:
            out[k] = blk.transpose(1, 0, 2, 3).reshape(depth, N_DEV * r, c)
        else:
            out[k] = blk.transpose(1, 2, 0, 3).reshape(depth, r, N_DEV * c)
    return out


def _scatter_grads(dws, shard_shapes):
    parts = []
    for k in BIG_WEIGHTS:
        depth, r, c = shard_shapes[k]
        for g in dws[k]:
            if k in ROW_SHARDED:
                g = g.reshape(N_DEV, r, c)
            else:
                g = g.reshape(r, N_DEV, c).transpose(1, 0, 2)
            parts.append(g.reshape(N_DEV, -1, 1024))
    send = jnp.concatenate(parts, axis=1)
    recv = _all_to_all(send, "grads_all_to_all")
    out, off = {}, 0
    for k in BIG_WEIGHTS:
        depth, r, c = shard_shapes[k]
        nrow = depth * r * c // 1024
        out[k] = recv[:, off:off + nrow].reshape(N_DEV, depth * r, c)
        off += nrow
    return out


def kernel(x, c, w_ada, b_ada, norm_g, ffn1_w_in, ffn1_w_out, w_in, hgrn_lb_logits, hgrn_norm_g, w_branch_a, w_branch_b, w_branch_c, w_out, ffn2_w_in, ffn2_w_out, loss_target, m_w_ada, m_b_ada, m_norm_g, m_ffn1_w_in, m_ffn1_w_out, m_w_in, m_hgrn_lb_logits, m_hgrn_norm_g, m_w_branch_a, m_w_branch_b, m_w_branch_c, m_w_out, m_ffn2_w_in, m_ffn2_w_out, v_w_ada, v_b_ada, v_norm_g, v_ffn1_w_in, v_ffn1_w_out, v_w_in, v_hgrn_lb_logits, v_hgrn_norm_g, v_w_branch_a, v_w_branch_b, v_w_branch_c, v_w_out, v_ffn2_w_in, v_ffn2_w_out):
    weights = dict(w_ada=w_ada, b_ada=b_ada, norm_g=norm_g, ffn1_w_in=ffn1_w_in, ffn1_w_out=ffn1_w_out, w_in=w_in,
                   hgrn_lb_logits=hgrn_lb_logits, hgrn_norm_g=hgrn_norm_g, w_branch_a=w_branch_a,
                   w_branch_b=w_branch_b, w_branch_c=w_branch_c, w_out=w_out, ffn2_w_in=ffn2_w_in,
                   ffn2_w_out=ffn2_w_out)
    mom1 = dict(w_ada=m_w_ada, b_ada=m_b_ada, norm_g=m_norm_g, ffn1_w_in=m_ffn1_w_in, ffn1_w_out=m_ffn1_w_out,
                w_in=m_w_in, hgrn_lb_logits=m_hgrn_lb_logits, hgrn_norm_g=m_hgrn_norm_g, w_branch_a=m_w_branch_a,
                w_branch_b=m_w_branch_b, w_branch_c=m_w_branch_c, w_out=m_w_out, ffn2_w_in=m_ffn2_w_in,
                ffn2_w_out=m_ffn2_w_out)
    mom2 = dict(w_ada=v_w_ada, b_ada=v_b_ada, norm_g=v_norm_g, ffn1_w_in=v_ffn1_w_in, ffn1_w_out=v_ffn1_w_out,
                w_in=v_w_in, hgrn_lb_logits=v_hgrn_lb_logits, hgrn_norm_g=v_hgrn_norm_g, w_branch_a=v_w_branch_a,
                w_branch_b=v_w_branch_b, w_branch_c=v_w_branch_c, w_out=v_w_out, ffn2_w_in=v_ffn2_w_in,
                ffn2_w_out=v_ffn2_w_out)
    order = list(weights)
    depth, d, ada_cols = w_ada.shape
    nd = d // LANES
    xi, yi, ci = _my_coords()
    me = 4 * xi + 2 * yi + ci

    small = jnp.concatenate([c.reshape(nd, LANES), norm_g.reshape(depth * 6, LANES)], axis=0)
    g1 = _all_gather(small, True, False, "small_all_gather").reshape(N_DEV, small.shape[0], LANES)
    c_act = _silu(g1[:, :nd].reshape(N_DEV, d))
    norm_full = g1[:, nd:].reshape(N_DEV, depth, 6, LANES).transpose(1, 2, 0, 3).reshape(depth, 6, d)

    c_pad = _pad_rows(c_act, 16)
    mod_sh = jnp.stack([_matmul(c_pad, w_ada[l], name="ada_mod")[:N_DEV]
                        + lax.dynamic_slice_in_dim(b_ada[l], me * ada_cols, ada_cols)[None]
                        for l in range(depth)])
    g2 = _all_gather(mod_sh.reshape(-1, LANES), True, False, "mod_all_gather")
    g2 = g2.reshape(N_DEV, depth, N_DEV, ada_cols)
    mod = lax.dynamic_index_in_dim(g2, me, axis=2, keepdims=False)
    mod = mod.transpose(1, 0, 2).reshape(depth, 3, 3, d)

    big = {k: weights[k] for k in BIG_WEIGHTS}
    wts = _gather_weights(big)
    lb_all, lb_vjp = jax.vjp(_lb_all, hgrn_lb_logits)

    loss, dx, dmod, dng, dlb, dhn, dws = _local_step(x[0], loss_target[0], mod, norm_full, lb_all, hgrn_norm_g, wts)
    loss = lax.psum(loss, ("x", "y", "c"))

    dhn_pad = jnp.pad(dhn.reshape(-1), (0, 8 * LANES - dhn.size))
    pieces = [dmod.reshape(-1), dng.reshape(-1), dlb.reshape(-1), dhn_pad]
    sizes = [p_.size for p_ in pieces]
    smallg = jnp.concatenate(pieces).reshape(-1, LANES)
    g3, gsum = _all_gather(smallg, True, True, "small_grads_all_gather")
    g3 = g3.reshape(N_DEV, -1)
    gsum = gsum.reshape(-1)
    dmod_all = g3[:, :sizes[0]].reshape(N_DEV, depth, 9 * d)
    o1 = sizes[0]
    grads = {}
    grads["b_ada"] = gsum[:o1].reshape(depth, 9 * d)
    dng_sum = gsum[o1:o1 + sizes[1]].reshape(depth, 6, nd, LANES)
    grads["norm_g"] = lax.dynamic_index_in_dim(dng_sum, me, axis=2, keepdims=False)
    o2 = o1 + sizes[1]
    dlb_sum = gsum[o2:o2 + sizes[2]].reshape(depth, A_QK)
    grads["hgrn_lb_logits"] = lb_vjp(dlb_sum)[0]
    o3 = o2 + sizes[2]
    grads["hgrn_norm_g"] = gsum[o3:o3 + dhn.size].reshape(depth, A_VDIM)
    dmod_mine = lax.dynamic_slice_in_dim(dmod_all, me * ada_cols, ada_cols, axis=2)
    grads["w_ada"] = jnp.stack([_matmul(c_pad, _pad_rows(dmod_mine[:, l], 16), ta=True, name="ada_dw")
                                for l in range(depth)])

    shard_shapes = {k: weights[k].shape for k in BIG_WEIGHTS}
    gparts = _scatter_grads(dws, shard_shapes)

    outs = {}
    for k in order:
        w = weights[k]
        w2 = w.reshape(-1, w.shape[-1])
        gp = gparts[k] if k in gparts else grads[k].reshape((1,) + w2.shape)
        res = _adamw(w2, mom1[k].reshape(w2.shape), mom2[k].reshape(w2.shape), gp, "adamw")
        outs[k] = [r.reshape(w.shape) for r in res]
    return (loss, dx[None], *[outs[k][0] for k in order], *[outs[k][1] for k in order],
            *[outs[k][2] for k in order], *[outs[k][3] for k in order])
```

```python
import functools
import math

import jax
import jax.numpy as jnp
from jax import lax
from jax.experimental import pallas as pl
from jax.experimental.pallas import tpu as pltpu

F32 = jnp.float32
BF16 = jnp.bfloat16

A_HEADS, A_KDIM, A_VDIM, A_CHUNK = 6, 128, 64, 64
B_HEADS, HDIM = 6, 64
C_GROUPS = ((128, 1), (512, 4), (2048, 16))
C_HPG = 4
C_HEADS = C_HPG * len(C_GROUPS)
N_BRANCH = 3
EPS = 1e-6
NEG_BIG = -1e30
TINY = 1e-30
A_QK = A_HEADS * A_KDIM
A_V = A_HEADS * A_VDIM
B_W = B_HEADS * HDIM
C_W = C_HEADS * HDIM
C_OUT = C_HPG * HDIM
COL_AQ, COL_AF, COL_AI, COL_AG = 0, A_QK, 2 * A_QK, 2 * A_QK + A_V
COL_BQ = 2 * A_QK + 2 * A_V
COL_BK, COL_BV = COL_BQ + B_W, COL_BQ + 2 * B_W
COL_CQ = COL_BQ + 3 * B_W
COL_CK, COL_CV = COL_CQ + C_W, COL_CQ + 2 * C_W
COL_GATE = COL_CQ + 3 * C_W

ADAM_LR, ADAM_B1, ADAM_B2, ADAM_EPS, ADAM_WD, ADAM_STEP = 0.001, 0.9, 0.999, 1e-08, 0.01, 10

N_DEV = 8
LANES = 128
VMEM_LIMIT = 48 * 1024 * 1024
MATMUL_VMEM_BUDGET = 28 * 1024 * 1024
FLAT_COLS = 1024
SUB = 16
EXP_CLAMP = 80.0
MESH = pl.DeviceIdType.MESH

BIG_WEIGHTS = ("ffn1_w_in", "ffn1_w_out", "w_in", "w_branch_a", "w_branch_b", "w_branch_c", "w_out",
               "ffn2_w_in", "ffn2_w_out")
ROW_SHARDED = ("ffn1_w_out", "w_out", "ffn2_w_out")


def _cparams(sem):
    return pltpu.CompilerParams(dimension_semantics=sem, vmem_limit_bytes=VMEM_LIMIT)


def _tile(n, cap):
    best, t = None, LANES
    while t <= min(n, cap):
        if n % t == 0:
            best = t
        t += LANES
    return best or n


def _rows(t, cap=256):
    r = cap
    while t % r:
        r //= 2
    return r


def _divisors(n):
    return [t for t in range(LANES, n + 1, LANES) if n % t == 0] or [n]


def _matmul_tiles(m, n, k, a_size, b_size, o_size):
    best, best_key = None, None
    for tm in _divisors(m):
        for tn in _divisors(n):
            for tk in _divisors(k):
                if tm > 1024 or tn > 3072 or tk > 4096:
                    continue
                cast = (tm * tk * 2 if a_size > 2 else 0) + (tk * tn * 2 if b_size > 2 else 0)
                need = 2 * (tm * tk * a_size + tk * tn * b_size + tm * tn * o_size) + 2 * tm * tn * 4 + cast
                if need > MATMUL_VMEM_BUDGET:
                    continue
                key = (tm * tn * tk, tk)
                if best_key is None or key > best_key:
                    best, best_key = (tm, tn, tk), key
    return best


def _dot(a, b):
    return jnp.dot(a, b, preferred_element_type=F32)


def _dot_nt(a, b):
    return lax.dot_general(a, b, (((1,), (1,)), ((), ())), preferred_element_type=F32)


def _dot_tn(a, b):
    return lax.dot_general(a, b, (((0,), (0,)), ((), ())), preferred_element_type=F32)


def _split3(x):
    h = x.astype(BF16)
    r = x - h.astype(F32)
    m = r.astype(BF16)
    lo = (r - m.astype(F32)).astype(BF16)
    return h, m, lo


def _ones_left(mat01, x):
    h, m, lo = _split3(x)
    return _dot(mat01, h) + _dot(mat01, m) + _dot(mat01, lo)


def _silu(x):
    return x * jax.nn.sigmoid(x)


def _dsilu(x):
    s = jax.nn.sigmoid(x)
    return s * (1.0 + x * (1.0 - s))


def _matmul(a, b, *, ta=False, tb=False, out_dtype=F32, name):
    if ta:
        kdim, m = a.shape
    else:
        m, kdim = a.shape
    n = b.shape[0] if tb else b.shape[1]
    tm, tn, tk = _matmul_tiles(m, n, kdim, a.dtype.itemsize, b.dtype.itemsize, jnp.dtype(out_dtype).itemsize)
    nk = kdim // tk
    ni, nj = m // tm, n // tn
    a_bytes, b_bytes = m * kdim * a.dtype.itemsize, kdim * n * b.dtype.itemsize
    j_outer = nk == 1 and (b_bytes + a_bytes * nj) < (a_bytes + b_bytes * ni)
    dims = (((0 if ta else 1,), (1 if tb else 0,)), ((), ()))

    def body(a_ref, b_ref, o_ref, *scratch):
        p = lax.dot_general(a_ref[...].astype(BF16), b_ref[...].astype(BF16), dims, preferred_element_type=F32)
        if nk == 1:
            o_ref[...] = p.astype(o_ref.dtype)
            return
        acc = scratch[0]
        k = pl.program_id(2)

        @pl.when(k == 0)
        def _():
            acc[...] = p

        @pl.when(k > 0)
        def _():
            acc[...] += p

        @pl.when(k == nk - 1)
        def _():
            o_ref[...] = acc[...].astype(o_ref.dtype)

    def spec(shape, pick):
        if j_outer:
            return pl.BlockSpec(shape, lambda j, i, k: pick(i, j, k))
        return pl.BlockSpec(shape, lambda i, j, k: pick(i, j, k))

    a_spec = spec((tk, tm), lambda i, j, k: (k, i)) if ta else spec((tm, tk), lambda i, j, k: (i, k))
    b_spec = spec((tn, tk), lambda i, j, k: (j, k)) if tb else spec((tk, tn), lambda i, j, k: (k, j))
    return pl.pallas_call(
        body, name=name, grid=(nj, ni, nk) if j_outer else (ni, nj, nk), in_specs=[a_spec, b_spec],
        out_specs=spec((tm, tn), lambda i, j, k: (i, j)),
        out_shape=jax.ShapeDtypeStruct((m, n), out_dtype),
        scratch_shapes=[pltpu.VMEM((tm, tn), F32)] if nk > 1 else [],
        compiler_params=_cparams(("parallel", "parallel", "arbitrary")),
    )(a, b)


def _rms_fwd(z, mcol, acol, res, out_dtype, name):
    t, d = z.shape
    tr = _rows(t)
    has_res = res is not None

    def body(*refs):
        if has_res:
            z_ref, m_ref, a_ref, r_ref, o_ref = refs
        else:
            z_ref, m_ref, a_ref, o_ref = refs
        zf = z_ref[...]
        r = lax.rsqrt(jnp.mean(zf * zf, axis=-1, keepdims=True) + EPS)
        y = zf * r * m_ref[...] + a_ref[...]
        if has_res:
            y = r_ref[...] + y
        o_ref[...] = y.astype(o_ref.dtype)

    row = pl.BlockSpec((tr, d), lambda i: (i, 0))
    col = pl.BlockSpec((1, d), lambda i: (0, 0))
    ins = [z, mcol, acol] + ([res] if has_res else [])
    return pl.pallas_call(
        body, name=name, grid=(t // tr,), in_specs=[row, col, col] + ([row] if has_res else []),
        out_specs=row, out_shape=jax.ShapeDtypeStruct((t, d), out_dtype),
        compiler_params=_cparams(("parallel",)),
    )(*ins)


def _rms_bwd(d_out, z, mcol, dres, out_dtype, name):
    t, d = z.shape
    tr = _rows(t)
    has_res = dres is not None

    def body(*refs):
        if has_res:
            d_ref, z_ref, m_ref, r_ref, o_ref, s1_ref, s2_ref = refs
        else:
            d_ref, z_ref, m_ref, o_ref, s1_ref, s2_ref = refs
        i = pl.program_id(0)
        zf = z_ref[...]
        r = lax.rsqrt(jnp.mean(zf * zf, axis=-1, keepdims=True) + EPS)
        zh = zf * r
        df = d_ref[...].astype(F32)
        dzh = df * m_ref[...]
        dz = r * (dzh - zh * jnp.mean(dzh * zh, axis=-1, keepdims=True))
        if has_res:
            dz = dz + r_ref[...]
        o_ref[...] = dz.astype(o_ref.dtype)
        s1 = jnp.sum(df * zh, axis=0, keepdims=True)
        s2 = jnp.sum(df, axis=0, keepdims=True)

        @pl.when(i == 0)
        def _():
            s1_ref[...] = s1
            s2_ref[...] = s2

        @pl.when(i > 0)
        def _():
            s1_ref[...] += s1
            s2_ref[...] += s2

    row = pl.BlockSpec((tr, d), lambda i: (i, 0))
    col = pl.BlockSpec((1, d), lambda i: (0, 0))
    ins = [d_out, z, mcol] + ([dres] if has_res else [])
    return pl.pallas_call(
        body, name=name, grid=(t // tr,), in_specs=[row, row, col] + ([row] if has_res else []),
        out_specs=[row, col, col],
        out_shape=[jax.ShapeDtypeStruct((t, d), out_dtype), jax.ShapeDtypeStruct((1, d), F32),
                   jax.ShapeDtypeStruct((1, d), F32)],
        compiler_params=_cparams(("arbitrary",)),
    )(*ins)


def _swiglu_fwd(u, name):
    t, f2 = u.shape
    f = f2 // 2
    tr = _rows(t)

    def body(u_ref, s_ref):
        a = u_ref[:, :f].astype(F32)
        b = u_ref[:, f:].astype(F32)
        s_ref[...] = (_silu(a) * b).astype(s_ref.dtype)

    return pl.pallas_call(
        body, name=name, grid=(t // tr,), in_specs=[pl.BlockSpec((tr, f2), lambda i: (i, 0))],
        out_specs=pl.BlockSpec((tr, f), lambda i: (i, 0)), out_shape=jax.ShapeDtypeStruct((t, f), BF16),
        compiler_params=_cparams(("parallel",)),
    )(u)


def _swiglu_bwd(u, ds, name):
    t, f2 = u.shape
    f = f2 // 2
    tr = _rows(t)

    def body(u_ref, ds_ref, du_ref):
        a = u_ref[:, :f].astype(F32)
        b = u_ref[:, f:].astype(F32)
        g = ds_ref[...].astype(F32)
        du_ref[:, :f] = (g * b * _dsilu(a)).astype(du_ref.dtype)
        du_ref[:, f:] = (g * _silu(a)).astype(du_ref.dtype)

    return pl.pallas_call(
        body, name=name, grid=(t // tr,),
        in_specs=[pl.BlockSpec((tr, f2), lambda i: (i, 0)), pl.BlockSpec((tr, f), lambda i: (i, 0))],
        out_specs=pl.BlockSpec((tr, f2), lambda i: (i, 0)), out_shape=jax.ShapeDtypeStruct((t, f2), BF16),
        compiler_params=_cparams(("parallel",)),
    )(u, ds)


def _loss_head(y, target, name):
    t, d = y.shape
    tr = _rows(t)

    def body(y_ref, t_ref, dy_ref, sq_ref):
        i = pl.program_id(0)
        e = y_ref[...] - t_ref[...]
        dy_ref[...] = e * (1.0 / d)
        s = jnp.sum(e * e, axis=0, keepdims=True)

        @pl.when(i == 0)
        def _():
            sq_ref[...] = s

        @pl.when(i > 0)
        def _():
            sq_ref[...] += s

    row = pl.BlockSpec((tr, d), lambda i: (i, 0))
    col = pl.BlockSpec((1, d), lambda i: (0, 0))
    return pl.pallas_call(
        body, name=name, grid=(t // tr,), in_specs=[row, row], out_specs=[row, col],
        out_shape=[jax.ShapeDtypeStruct((t, d), F32), jax.ShapeDtypeStruct((1, d), F32)],
        compiler_params=_cparams(("arbitrary",)),
    )(y, target)


def _hgrn_consts():
    c = A_CHUNK
    shift = SUB.bit_length() - 1
    r = lax.broadcasted_iota(jnp.int32, (c, c), 0)
    s = lax.broadcasted_iota(jnp.int32, (c, c), 1)
    sub_r = lax.shift_right_logical(r, shift)
    incl = s <= r
    masks = [jnp.logical_and(sub_r == i, incl) for i in range(c // SUB)]
    rev_incl = jnp.where(s >= r, 1.0, 0.0).astype(BF16)
    r2 = lax.broadcasted_iota(jnp.int32, (2 * c + 8, c), 0)
    s2 = lax.broadcasted_iota(jnp.int32, (2 * c + 8, c), 1)
    sub_start = lax.shift_left(lax.shift_right_logical(r2 - c, shift), shift)
    running = jnp.where(s2 <= r2, 1.0, 0.0)
    before = jnp.where(s2 < sub_start, 1.0, 0.0)
    stack = jnp.where(r2 < c, running, jnp.where(r2 < 2 * c, before, 1.0)).astype(BF16)
    return stack, masks, incl, rev_incl


def _hgrn_chunk(q_raw, f_raw, lbv, stack):
    c = A_CHUNK
    sg = jax.nn.sigmoid(f_raw)
    sgn = jax.nn.sigmoid(-f_raw)
    f = lbv + (1.0 - lbv) * sg
    logf = jnp.log(jnp.maximum(f, TINY))
    k = (1.0 - lbv) * sgn
    q = _silu(q_raw)
    bb = _ones_left(stack, logf)
    b = bb[:c]
    bsrow = bb[c:2 * c]
    b_end = bb[2 * c:2 * c + 1]
    e_sub = jnp.exp(b - bsrow)
    e_b = jnp.exp(b)
    e_end = jnp.exp(b_end - b)
    qs = q * e_sub
    q_in = q * e_b
    kend = k * e_end
    kfac = [jnp.exp(jnp.minimum(bsrow[i * SUB:i * SUB + 1] - b, EXP_CLAMP)) for i in range(c // SUB)]
    return dict(sg=sg, sgn=sgn, f=f, k=k, q=q, b=b, b_end=b_end, e_sub=e_sub, e_b=e_b, e_end=e_end,
                qs=qs, q_in=q_in, kend=kend, kfac=kfac)


def _hgrn_scores(ch, masks):
    qs_b = ch["qs"].astype(BF16)
    a = None
    for i, mk in enumerate(masks):
        ki = (ch["k"] * ch["kfac"][i]).astype(BF16)
        part = jnp.where(mk, _dot_nt(qs_b, ki), 0.0)
        a = part if a is None else a + part
    return a


def _hgrn_fwd(p, lb, hn2, name):
    t = p.shape[0]
    tb = _rows(t)
    nt = t // tb
    nc = tb // A_CHUNK
    c = A_CHUNK

    def body(q_ref, f_ref, i_ref, g_ref, lb_ref, hn_ref, y_ref, o_ref, st_ref, s_scr):
        j = pl.program_id(1)

        @pl.when(j == 0)
        def _():
            s_scr[...] = jnp.zeros_like(s_scr)

        stack, masks, _, _ = _hgrn_consts()
        states = [s_scr[0], s_scr[1]]
        for ci in range(nc):
            rows = pl.ds(ci * c, c)
            for hh in range(2):
                lsl = slice(A_KDIM * hh, A_KDIM * (hh + 1))
                hsl = slice(A_VDIM * hh, A_VDIM * (hh + 1))
                ch = _hgrn_chunk(q_ref[rows, lsl], f_ref[rows, lsl], lb_ref[:, lsl], stack)
                v = i_ref[rows, hsl].astype(BF16)
                st = states[hh]
                st_ref[hh, ci] = st
                a = _hgrn_scores(ch, masks)
                o_ref[rows, hsl] = _dot_nt(ch["q_in"].astype(BF16), st.astype(BF16)) + _dot(a.astype(BF16), v)
                states[hh] = st * jnp.exp(ch["b_end"]) + _dot_tn(v, ch["kend"].astype(BF16))
        s_scr[0] = states[0]
        s_scr[1] = states[1]
        for hh in range(2):
            hsl = slice(A_VDIM * hh, A_VDIM * (hh + 1))
            o = o_ref[:, hsl]
            r = lax.rsqrt(jnp.mean(o * o, axis=-1, keepdims=True) + EPS)
            y_ref[:, hsl] = (o * r * hn_ref[:, hsl] * _silu(g_ref[:, hsl])).astype(y_ref.dtype)

    w2 = 2 * A_KDIM
    return pl.pallas_call(
        body, name=name, grid=(A_HEADS // 2, nt),
        in_specs=[pl.BlockSpec((tb, w2), lambda h, j: (j, COL_AQ // w2 + h)),
                  pl.BlockSpec((tb, w2), lambda h, j: (j, COL_AF // w2 + h)),
                  pl.BlockSpec((tb, LANES), lambda h, j: (j, COL_AI // LANES + h)),
                  pl.BlockSpec((tb, LANES), lambda h, j: (j, COL_AG // LANES + h)),
                  pl.BlockSpec((1, w2), lambda h, j: (0, h)),
                  pl.BlockSpec((1, LANES), lambda h, j: (0, 0))],
        out_specs=[pl.BlockSpec((tb, LANES), lambda h, j: (j, h)),
                   pl.BlockSpec((tb, LANES), lambda h, j: (j, h)),
                   pl.BlockSpec((2, nc, A_VDIM, A_KDIM), lambda h, j: (h, j, 0, 0))],
        out_shape=[jax.ShapeDtypeStruct((t, A_V), BF16), jax.ShapeDtypeStruct((t, A_V), F32),
                   jax.ShapeDtypeStruct((A_HEADS, t // c, A_VDIM, A_KDIM), F32)],
        scratch_shapes=[pltpu.VMEM((2, A_VDIM, A_KDIM), F32)],
        compiler_params=_cparams(("parallel", "arbitrary")),
    )(p, p, p, p, lb, hn2)


def _hgrn_bwd(p, lb, hn2, o_raw, states, dya, name):
    t = p.shape[0]
    tb = _rows(t)
    nt = t // tb
    nc = tb // A_CHUNK
    c = A_CHUNK

    def body(q_ref, f_ref, i_ref, g_ref, lb_ref, hn_ref, o_ref, st_ref, dy_ref,
             dq_ref, df_ref, di_ref, dg_ref, dlb_ref, dhn_ref, ds_scr, do_scr):
        j = pl.program_id(1)

        @pl.when(j == 0)
        def _():
            ds_scr[...] = jnp.zeros_like(ds_scr)
            dlb_ref[...] = jnp.zeros_like(dlb_ref)
            dhn_ref[...] = jnp.zeros_like(dhn_ref)

        stack, masks, incl, rev_incl = _hgrn_consts()
        for hh in range(2):
            hsl = slice(A_VDIM * hh, A_VDIM * (hh + 1))
            o = o_ref[:, hsl]
            g = g_ref[:, hsl]
            dy = dy_ref[:, hsl].astype(F32)
            hn = hn_ref[:, hsl]
            r = lax.rsqrt(jnp.mean(o * o, axis=-1, keepdims=True) + EPS)
            oh = o * r
            sgate = _silu(g)
            dg_ref[:, hsl] = dy * oh * hn * _dsilu(g)
            dhn_ref[0, :, hsl] += jnp.sum(dy * oh * sgate, axis=0, keepdims=True)
            doh = dy * hn * sgate
            do_scr[:, hsl] = r * (doh - oh * jnp.mean(doh * oh, axis=-1, keepdims=True))

        dstates = [ds_scr[0], ds_scr[1]]
        dlb_acc = [jnp.zeros((1, A_KDIM), F32), jnp.zeros((1, A_KDIM), F32)]
        for ci in reversed(range(nc)):
            rows = pl.ds(ci * c, c)
            for hh in range(2):
                lsl = slice(A_KDIM * hh, A_KDIM * (hh + 1))
                hsl = slice(A_VDIM * hh, A_VDIM * (hh + 1))
                lbv = lb_ref[:, lsl]
                q_raw = q_ref[rows, lsl]
                f_raw = f_ref[rows, lsl]
                ch = _hgrn_chunk(q_raw, f_raw, lbv, stack)
                v = i_ref[rows, hsl].astype(BF16)
                do = do_scr[rows, hsl]
                do_b = do.astype(BF16)
                st = st_ref[hh, ci]
                st_b = st.astype(BF16)
                dst = dstates[hh]
                dst_b = dst.astype(BF16)
                qs_b = ch["qs"].astype(BF16)
                kend_b = ch["kend"].astype(BF16)
                a = _hgrn_scores(ch, masks)
                da = jnp.where(incl, _dot_nt(do_b, v), 0.0)
                dv = _dot_tn(a.astype(BF16), do_b) + _dot_nt(kend_b, dst_b)
                dq_i = None
                dk_i = None
                kdk_i = None
                for i, mk in enumerate(masks):
                    dam = jnp.where(mk, da, 0.0).astype(BF16)
                    ki = (ch["k"] * ch["kfac"][i]).astype(BF16)
                    pq = _dot(dam, ki)
                    pk = _dot_tn(dam, qs_b)
                    dq_i = pq if dq_i is None else dq_i + pq
                    dk_i = ch["kfac"][i] * pk if dk_i is None else dk_i + ch["kfac"][i] * pk
                    kdk_i = ki.astype(F32) * pk if kdk_i is None else kdk_i + ki.astype(F32) * pk
                dq_x = _dot(do_b, st_b)
                dk_x = _dot(v, dst_b)
                dq = ch["e_sub"] * dq_i + ch["e_b"] * dq_x
                dk = dk_i + ch["e_end"] * dk_x
                dstates[hh] = dst * jnp.exp(ch["b_end"]) + _dot_tn(do_b, ch["q_in"].astype(BF16))
                kx = ch["kend"] * dk_x
                db = (qs_b.astype(F32) * dq_i + ch["q_in"] * dq_x) - (kdk_i + kx)
                later = (jnp.exp(ch["b_end"]) * jnp.sum(dst * st, axis=0, keepdims=True)
                         + jnp.sum(kx, axis=0, keepdims=True))
                dlogf = later + _ones_left(rev_incl, db)
                dfv = jnp.where(ch["f"] > TINY, dlogf / ch["f"], 0.0)
                dq_ref[rows, lsl] = dq * _dsilu(q_raw)
                df_ref[rows, lsl] = (1.0 - lbv) * ch["sg"] * ch["sgn"] * (dfv - dk)
                dlb_acc[hh] = dlb_acc[hh] + jnp.sum(dfv * (1.0 - ch["sg"]) - dk * ch["sgn"], axis=0, keepdims=True)
                di_ref[rows, hsl] = dv
        for hh in range(2):
            ds_scr[hh] = dstates[hh]
            dlb_ref[:, A_KDIM * hh:A_KDIM * (hh + 1)] += dlb_acc[hh]

    w2 = 2 * A_KDIM
    rev = lambda j: nt - 1 - j
    return pl.pallas_call(
        body, name=name, grid=(A_HEADS // 2, nt),
        in_specs=[pl.BlockSpec((tb, w2), lambda h, j: (rev(j), COL_AQ // w2 + h)),
                  pl.BlockSpec((tb, w2), lambda h, j: (rev(j), COL_AF // w2 + h)),
                  pl.BlockSpec((tb, LANES), lambda h, j: (rev(j), COL_AI // LANES + h)),
                  pl.BlockSpec((tb, LANES), lambda h, j: (rev(j), COL_AG // LANES + h)),
                  pl.BlockSpec((1, w2), lambda h, j: (0, h)),
                  pl.BlockSpec((1, LANES), lambda h, j: (0, 0)),
                  pl.BlockSpec((tb, LANES), lambda h, j: (rev(j), h)),
                  pl.BlockSpec((2, nc, A_VDIM, A_KDIM), lambda h, j: (h, rev(j), 0, 0)),
                  pl.BlockSpec((tb, LANES), lambda h, j: (rev(j), h))],
        out_specs=[pl.BlockSpec((tb, w2), lambda h, j: (rev(j), h)),
                   pl.BlockSpec((tb, w2), lambda h, j: (rev(j), h)),
                   pl.BlockSpec((tb, LANES), lambda h, j: (rev(j), h)),
                   pl.BlockSpec((tb, LANES), lambda h, j: (rev(j), h)),
                   pl.BlockSpec((1, w2), lambda h, j: (0, h)),
                   pl.BlockSpec((1, 1, LANES), lambda h, j: (h, 0, 0))],
        out_shape=[jax.ShapeDtypeStruct((t, A_QK), F32), jax.ShapeDtypeStruct((t, A_QK), F32),
                   jax.ShapeDtypeStruct((t, A_V), F32), jax.ShapeDtypeStruct((t, A_V), F32),
                   jax.ShapeDtypeStruct((1, A_QK), F32), jax.ShapeDtypeStruct((A_HEADS // 2, 1, LANES), F32)],
        scratch_shapes=[pltpu.VMEM((2, A_VDIM, A_KDIM), F32), pltpu.VMEM((tb, LANES), F32)],
        compiler_params=_cparams(("parallel", "arbitrary")),
    )(p, p, p, p, lb, hn2, o_raw, states, dya)


BLK = 128
SCALE = HDIM ** -0.5
SB_CHUNK = 4


def _softplus(z):
    return jnp.maximum(z, 0.0) + jnp.log(1.0 + jnp.exp(-jnp.abs(z)))


def _split2(x):
    hi = x.astype(BF16)
    return hi, (x - hi.astype(F32)).astype(BF16)


def _sb_sum_matrix(keep):
    sp = lax.broadcasted_iota(jnp.int32, (2 * BLK, 2 * BLK), 0) & (BLK - 1)
    s = lax.broadcasted_iota(jnp.int32, (2 * BLK, 2 * BLK), 1)
    return jnp.where(jnp.logical_or(s >= BLK, keep(sp, s)), 1.0, 0.0).astype(BF16)


def _sb_fwd(p, kv, name, gather=None):
    t = p.shape[0]
    nq = t // BLK
    nh = B_HEADS // 2
    cw = SB_CHUNK * BLK
    fused = gather is not None

    def body(*refs):
        if fused:
            q_ref, kb, vb, x_ref, o_ref, tot_ref, got_ref, zbuf, stage, sbuf, abuf, ssem, rsem, lsem = refs
        else:
            q_ref, kb, vb, o_ref, tot_ref, zbuf, stage, sbuf, abuf = refs
        hp = pl.program_id(0)
        qi = pl.program_id(1)
        if fused:
            g = _Gather(x_ref, got_ref, ssem, rsem, lsem)
            pl.when(jnp.logical_and(hp == 0, qi == 0))(g.start)
            pl.when(jnp.logical_and(hp == nh - 1, qi == 0))(g.forward)

        @pl.when(qi == 0)
        def _():
            abuf[...] = jnp.zeros_like(abuf)

        row = lax.broadcasted_iota(jnp.int32, (BLK, BLK), 0)
        col = lax.broadcasted_iota(jnp.int32, (BLK, BLK), 1)
        sums = _sb_sum_matrix(lambda sp, s: sp >= s)
        hsl = [slice(HDIM * h, HDIM * (h + 1)) for h in range(2)]
        nchunk = qi // SB_CHUNK + 1
        for h in range(2):
            zbuf[h] = _dot_nt((q_ref[:, hsl[h]] * SCALE).astype(BF16), kb[:, hsl[h]])

        def causal(j):
            return (col + j * BLK) < (row + qi * BLK)

        def l_pass(c, carry):
            for b in range(SB_CHUNK):
                j = c * SB_CHUNK + b
                off = pl.multiple_of(j * BLK, BLK)
                mask = causal(j)
                for h in range(2):
                    lm = jnp.where(mask, -_softplus(zbuf[h, :, pl.ds(off, BLK)]), 0.0)
                    hi, lo = _split2(lm)
                    stage[h, pl.ds(off, BLK), :BLK] = hi
                    stage[h, pl.ds(off, BLK), BLK:] = lo
            return carry

        lax.fori_loop(0, nchunk, l_pass, 0)

        def sum_pass(c, carry):
            rows = pl.ds(pl.multiple_of(c * cw, cw), cw)
            for h in range(2):
                sbuf[h, rows, :] = _dot(stage[h, rows, :], sums)
            return carry

        lax.fori_loop(0, nchunk, sum_pass, 0)

        def a_pass(it, carry):
            c = nchunk - 1 - it
            runs = list(carry)
            for b in reversed(range(SB_CHUNK)):
                j = c * SB_CHUNK + b
                off = pl.multiple_of(j * BLK, BLK)
                mask = causal(j)
                for h in range(2):
                    s = sbuf[h, pl.ds(off, BLK), :BLK]
                    a = jnp.where(mask, jnp.exp(zbuf[h, :, pl.ds(off, BLK)] + s + runs[h]), 0.0)
                    abuf[h, :, pl.ds(off, BLK)] = a.astype(BF16)
                    runs[h] = runs[h] + sbuf[h, pl.ds(off, BLK), BLK:]
            return tuple(runs)

        zero = jnp.zeros((BLK, BLK), F32)
        runs = lax.fori_loop(0, nchunk, a_pass, (zero, zero))
        for h in range(2):
            tot_ref[:, hsl[h]] = runs[h][:, :HDIM]
            o_ref[:, hsl[h]] = _dot(abuf[h], vb[:, hsl[h]])
        if fused:
            pl.when(jnp.logical_and(hp == nh - 1, qi == nq - 1))(g.finish)

    out_blk = pl.BlockSpec((BLK, LANES), lambda h, i: (i, h))
    hbm = pl.BlockSpec(memory_space=pl.ANY)
    in_specs = [pl.BlockSpec((BLK, LANES), lambda h, i: (i, COL_BQ // LANES + h)),
                pl.BlockSpec((t, LANES), lambda h, i: (0, h)),
                pl.BlockSpec((t, LANES), lambda h, i: (0, B_W // LANES + h))]
    out_shape = [jax.ShapeDtypeStruct((t, B_W), F32)] * 2
    scratch = [pltpu.VMEM((2, BLK, t), F32), pltpu.VMEM((2, t, 2 * BLK), BF16),
               pltpu.VMEM((2, t, 2 * BLK), F32), pltpu.VMEM((2, BLK, t), BF16)]
    if fused:
        out_shape = out_shape + [jax.ShapeDtypeStruct((N_DEV * gather.shape[0], gather.shape[1]), gather.dtype)]
    return pl.pallas_call(
        body, name=name, grid=(nh, nq),
        in_specs=in_specs + ([hbm] if fused else []),
        out_specs=[out_blk, out_blk] + ([hbm] if fused else []),
        out_shape=out_shape,
        scratch_shapes=scratch + (COMM_SEMS if fused else []),
        compiler_params=_cparams(("arbitrary", "arbitrary")),
    )(p, kv, kv, *([gather] if fused else []))


def _sb_bwd(p, kv, tot, do, name, exchange=None):
    t = p.shape[0]
    nq = t // BLK
    nh = B_HEADS // 2
    cw = SB_CHUNK * BLK
    fused = exchange is not None

    def body(*refs):
        if fused:
            (q_ref, kb, vb, tot_ref, do_ref, send_ref, dq_ref, dk_ref, dv_ref, recv_ref,
             zbuf, dabuf, stage, gstage, sbuf, abuf, dzbuf, dkt, dvt, ssem, rsem, lsem) = refs
        else:
            (q_ref, kb, vb, tot_ref, do_ref, dq_ref, dk_ref, dv_ref,
             zbuf, dabuf, stage, gstage, sbuf, abuf, dzbuf, dkt, dvt) = refs
        hp = pl.program_id(0)
        qi = pl.program_id(1)
        if fused:
            ex = _Exchange(send_ref, recv_ref, ssem, rsem, lsem)
            pl.when(jnp.logical_and(hp == 0, qi == 0))(ex.start)

        @pl.when(qi == 0)
        def _():
            dkt[...] = jnp.zeros_like(dkt)
            dvt[...] = jnp.zeros_like(dvt)
            dzbuf[...] = jnp.zeros_like(dzbuf)
            abuf[...] = jnp.zeros_like(abuf)

        row = lax.broadcasted_iota(jnp.int32, (BLK, BLK), 0)
        col = lax.broadcasted_iota(jnp.int32, (BLK, BLK), 1)
        sums = _sb_sum_matrix(lambda sp, s: sp <= s)
        hsl = [slice(HDIM * h, HDIM * (h + 1)) for h in range(2)]
        dob = [do_ref[:, hsl[h]].astype(BF16) for h in range(2)]
        total =[jnp.concatenate([tot_ref[:, hsl[h]], tot_ref[:, hsl[h]]], axis=1) for h in range(2)]
        nchunk = qi // SB_CHUNK + 1
        for h in range(2):
            zbuf[h] = _dot_nt((q_ref[:, hsl[h]] * SCALE).astype(BF16), kb[:, hsl[h]])
            dabuf[h] = _dot_nt(dob[h], vb[:, hsl[h]])

        def causal(j):
            return (col + j * BLK) < (row + qi * BLK)

        def blocks(c):
            for b in range(SB_CHUNK):
                j = c * SB_CHUNK + b
                yield j, pl.ds(pl.multiple_of(j * BLK, BLK), BLK)

        def l_pass(c, carry):
            for j, blk_ in blocks(c):
                mask = causal(j)
                for h in range(2):
                    lm = jnp.where(mask, -_softplus(zbuf[h, :, blk_]), 0.0)
                    hi, lo = _split2(lm)
                    stage[h, blk_, :BLK] = hi
                    stage[h, blk_, BLK:] = lo
            return carry

        lax.fori_loop(0, nchunk, l_pass, 0)

        def sum_pass(src):
            def run_(c, carry):
                rows = pl.ds(pl.multiple_of(c * cw, cw), cw)
                for h in range(2):
                    sbuf[h, rows, :] = _dot(src[h, rows, :], sums)
                return carry
            lax.fori_loop(0, nchunk, run_, 0)

        sum_pass(stage)

        def g_pass(c, carry):
            runs = list(carry)
            for j, blk_ in blocks(c):
                mask = causal(j)
                for h in range(2):
                    lm = stage[h, blk_, :BLK].astype(F32) + stage[h, blk_, BLK:].astype(F32)
                    log_a = zbuf[h, :, blk_] + lm + (total[h] - runs[h] - sbuf[h, blk_, :BLK])
                    a = jnp.where(mask, jnp.exp(log_a), 0.0)
                    abuf[h, :, blk_] = a.astype(BF16)
                    hi, lo = _split2(a * dabuf[h, :, blk_])
                    gstage[h, blk_, :BLK] = hi
                    gstage[h, blk_, BLK:] = lo
                    runs[h] = runs[h] + sbuf[h, blk_, BLK:]
            return tuple(runs)

        zero = jnp.zeros((BLK, BLK), F32)
        lax.fori_loop(0, nchunk, g_pass, (zero, zero))
        sum_pass(gstage)

        def dz_pass(c, carry):
            runs = list(carry)
            for j, blk_ in blocks(c):
                mask = causal(j)
                for h in range(2):
                    lm = stage[h, blk_, :BLK].astype(F32) + stage[h, blk_, BLK:].astype(F32)
                    g = gstage[h, blk_, :BLK].astype(F32) + gstage[h, blk_, BLK:].astype(F32)
                    before = runs[h] + sbuf[h, blk_, :BLK] - g
                    dz = jnp.where(mask, g * jnp.exp(lm) - jnp.exp(zbuf[h, :, blk_] + lm) * before, 0.0)
                    dzbuf[h, :, blk_] = (dz * SCALE).astype(BF16)
                    runs[h] = runs[h] + sbuf[h, blk_, BLK:]
            return tuple(runs)

        lax.fori_loop(0, nchunk, dz_pass, (zero, zero))
        for h in range(2):
            dq_ref[:, hsl[h]] = _dot(dzbuf[h], kb[:, hsl[h]])
        q_t = q_ref[...].T.astype(BF16)
        do_t = do_ref[...].T.astype(BF16)
        for h in range(2):
            dkt[hsl[h], :] += _dot(q_t[hsl[h], :], dzbuf[h])
            dvt[hsl[h], :] += _dot(do_t[hsl[h], :], abuf[h])

        @pl.when(qi == nq - 1)
        def _():
            dk_ref[...] = dkt[...].T
            dv_ref[...] = dvt[...].T

        if fused:
            pl.when(jnp.logical_and(hp == nh - 1, qi == nq - 1))(ex.finish)

    blk = lambda h, i: (i, h)
    whole = lambda h, i: (0, h)
    hbm = pl.BlockSpec(memory_space=pl.ANY)
    in_specs = [pl.BlockSpec((BLK, LANES), lambda h, i: (i, COL_BQ // LANES + h)),
                pl.BlockSpec((t, LANES), lambda h, i: (0, h)),
                pl.BlockSpec((t, LANES), lambda h, i: (0, B_W // LANES + h)),
                pl.BlockSpec((BLK, LANES), blk), pl.BlockSpec((BLK, LANES), blk)]
    out_specs = [pl.BlockSpec((BLK, LANES), blk), pl.BlockSpec((t, LANES), whole), pl.BlockSpec((t, LANES), whole)]
    out_shape = [jax.ShapeDtypeStruct((t, B_W), F32)] * 3
    scratch = [pltpu.VMEM((2, BLK, t), F32), pltpu.VMEM((2, BLK, t), F32),
               pltpu.VMEM((2, t, 2 * BLK), BF16), pltpu.VMEM((2, t, 2 * BLK), BF16),
               pltpu.VMEM((2, t, 2 * BLK), F32), pltpu.VMEM((2, BLK, t), BF16),
               pltpu.VMEM((2, BLK, t), BF16), pltpu.VMEM((LANES, t), F32), pltpu.VMEM((LANES, t), F32)]
    if fused:
        out_shape = out_shape + [jax.ShapeDtypeStruct(exchange.shape, exchange.dtype)]
    return pl.pallas_call(
        body, name=name, grid=(nh, nq),
        in_specs=in_specs + ([hbm] if fused else []),
        out_specs=out_specs + ([hbm] if fused else []),
        out_shape=out_shape,
        scratch_shapes=scratch + (COMM_SEMS if fused else []),
        compiler_params=_cparams(("arbitrary", "arbitrary")),
    )(p, kv, kv, tot, do, *([exchange] if fused else []))


def _alibi_slopes(n):
    def pow2(m):
        start = 2.0 ** (-8.0 / m)
        return [start ** (i + 1) for i in range(m)]
    if math.log2(n).is_integer():
        s = pow2(n)
    else:
        c = 2 ** int(math.floor(math.log2(n)))
        s = pow2(c) + pow2(2 * c)[0::2][: n - c]
    return sorted(s, reverse=True)


def _dil_scores(qh, kh, sl, prev, exists=None):
    row = lax.broadcasted_iota(jnp.int32, (BLK, BLK), 0)
    col = lax.broadcasted_iota(jnp.int32, (BLK, BLK), 1)
    dist = row - col + (BLK if prev else 0)
    if prev:
        valid = (col - row) >= jnp.where(exists, 0, 2 * BLK)
    else:
        valid = col <= row
    s = _dot_nt(qh, kh) - sl * dist.astype(F32)
    return s, valid


def _dil_fwd(q, k, v, slope_cols, name):
    ln, cw = q.shape
    nb = ln // BLK

    def body(q_ref, kc_ref, kp_ref, vc_ref, vp_ref, sl_ref, o_ref, lse_ref):
        i = pl.program_id(1)
        for h in range(2):
            hs = slice(HDIM * h, HDIM * (h + 1))
            sl = sl_ref[:, HDIM * h:HDIM * h + 1]
            qh = (q_ref[:, hs] * SCALE).astype(BF16)
            sc, vc_ok = _dil_scores(qh, kc_ref[:, hs].astype(BF16), sl, False)
            sp, vp_ok = _dil_scores(qh, kp_ref[:, hs].astype(BF16), sl, True, i > 0)
            sc = jnp.where(vc_ok, sc, NEG_BIG)
            sp = jnp.where(vp_ok, sp, NEG_BIG)
            m = jnp.maximum(jnp.max(sc, axis=1, keepdims=True), jnp.max(sp, axis=1, keepdims=True))
            pc = jnp.exp(sc - m)
            pp = jnp.exp(sp - m)
            den = jnp.sum(pc, axis=1, keepdims=True) + jnp.sum(pp, axis=1, keepdims=True)
            o = _dot(pc.astype(BF16), vc_ref[:, hs].astype(BF16)) + _dot(pp.astype(BF16), vp_ref[:, hs].astype(BF16))
            o_ref[:, hs] = o / den
            lse_ref[:, hs] = jnp.broadcast_to(m + jnp.log(den), (BLK, HDIM))

    cur = pl.BlockSpec((BLK, LANES), lambda c, i: (i, c))
    prv = pl.BlockSpec((BLK, LANES), lambda c, i: (jnp.maximum(i - 1, 0), c))
    return pl.pallas_call(
        body, name=name, grid=(cw // LANES, nb),
        in_specs=[cur, cur, prv, cur, prv, pl.BlockSpec((1, LANES), lambda c, i: (0, c))],
        out_specs=[cur, cur], out_shape=[jax.ShapeDtypeStruct((ln, cw), F32)] * 2,
        compiler_params=_cparams(("parallel", "parallel")),
    )(q, k, k, v, v, slope_cols)


def _dil_bwd(q, k, v, do, o, lse, slope_cols, name):
    ln, cw = q.shape
    nb = ln // BLK

    def body(q_ref, qn_ref, kc_ref, kp_ref, vc_ref, vp_ref, do_ref, don_ref, o_ref, on_ref, l_ref, ln_ref, sl_ref,
             dq_ref, dk_ref, dv_ref):
        i = pl.program_id(1)
        has_prev = i > 0
        has_next = i < nb - 1
        for h in range(2):
            hs = slice(HDIM * h, HDIM * (h + 1))
            sl = sl_ref[:, HDIM * h:HDIM * h + 1]
            qb = q_ref[:, hs].astype(BF16)
            qnb = qn_ref[:, hs].astype(BF16)
            qh = (q_ref[:, hs] * SCALE).astype(BF16)
            qnh = (qn_ref[:, hs] * SCALE).astype(BF16)
            kc = kc_ref[:, hs].astype(BF16)
            kp = kp_ref[:, hs].astype(BF16)
            vc = vc_ref[:, hs].astype(BF16)
            vp = vp_ref[:, hs].astype(BF16)
            do_f = do_ref[:, hs]
            don_f = don_ref[:, hs]
            dob = do_f.astype(BF16)
            donb = don_f.astype(BF16)
            delta = jnp.sum(do_f * o_ref[:, hs], axis=1, keepdims=True)
            deltan = jnp.sum(don_f * on_ref[:, hs], axis=1, keepdims=True)
            lse = l_ref[:, HDIM * h:HDIM * h + 1]
            lsen = ln_ref[:, HDIM * h:HDIM * h + 1]
            s, ok = _dil_scores(qh, kc, sl, False)
            p_cc = jnp.where(ok, jnp.exp(jnp.where(ok, s, NEG_BIG) - lse), 0.0)
            ds_cc = p_cc * (_dot_nt(dob, vc) - delta)
            s, ok = _dil_scores(qh, kp, sl, True, has_prev)
            p_cp = jnp.where(ok, jnp.exp(jnp.where(ok, s, NEG_BIG) - lse), 0.0)
            ds_cp = p_cp * (_dot_nt(dob, vp) - delta)
            s, ok = _dil_scores(qnh, kc, sl, True, has_next)
            p_nc = jnp.where(ok, jnp.exp(jnp.where(ok, s, NEG_BIG) - lsen), 0.0)
            ds_nc = p_nc * (_dot_nt(donb, vc) - deltan)
            ds_cc_b = (ds_cc * SCALE).astype(BF16)
            ds_cp_b = (ds_cp * SCALE).astype(BF16)
            ds_nc_b = (ds_nc * SCALE).astype(BF16)
            dq_ref[:, hs] = _dot(ds_cc_b, kc) + _dot(ds_cp_b, kp)
            dk_ref[:, hs] = _dot_tn(ds_cc_b, qb) + _dot_tn(ds_nc_b, qnb)
            dv_ref[:, hs] = _dot_tn(p_cc.astype(BF16), dob) + _dot_tn(p_nc.astype(BF16), donb)

    cur = pl.BlockSpec((BLK, LANES), lambda c, i: (i, c))
    prv = pl.BlockSpec((BLK, LANES), lambda c, i: (jnp.maximum(i - 1, 0), c))
    nxt = pl.BlockSpec((BLK, LANES), lambda c, i: (jnp.minimum(i + 1, nb - 1), c))
    return pl.pallas_call(
        body, name=name, grid=(cw // LANES, nb),
        in_specs=[cur, nxt, cur, prv, cur, prv, cur, nxt, cur, nxt, cur, nxt,
                  pl.BlockSpec((1, LANES), lambda c, i: (0, c))],
        out_specs=[cur, cur, cur], out_shape=[jax.ShapeDtypeStruct((ln, cw), F32)] * 3,
        compiler_params=_cparams(("parallel", "parallel")),
    )(q, q, k, k, v, v, do, do, o, o, lse, lse, slope_cols)


def _dil_merge(os_, ls_, name):
    t, w = os_[0].shape
    tr = _rows(t)

    def body(o0, o1, o2, l0, l1, l2, y_ref, lse_ref):
        a, b, c = l0[...], l1[...], l2[...]
        m = jnp.maximum(jnp.maximum(a, b), c)
        ea, eb, ec = jnp.exp(a - m), jnp.exp(b - m), jnp.exp(c - m)
        den = ea + eb + ec
        y_ref[...] = (ea * o0[...] + eb * o1[...] + ec * o2[...]) / den
        lse_ref[...] = m + jnp.log(den)

    row = pl.BlockSpec((tr, w), lambda i: (i, 0))
    return pl.pallas_call(
        body, name=name, grid=(t // tr,), in_specs=[row] * 6, out_specs=[row, row],
        out_shape=[jax.ShapeDtypeStruct((t, w), F32)] * 2, compiler_params=_cparams(("parallel",)),
    )(*os_, *ls_)


def _gate_fwd(ys, gl, ws, name):
    t = gl.shape[0]
    d = gl.shape[1] // N_BRANCH
    tr = _rows(t)

    def body(ya, yb, yc, gl_ref, wa, wb, wc, m_ref):
        acc = None
        for i, (y, w) in enumerate(((ya, wa), (yb, wb), (yc, wc))):
            z = _dot(y[...].astype(BF16), w[...])
            term = jax.nn.sigmoid(gl_ref[:, i * d:(i + 1) * d]) * z
            acc = term if acc is None else acc + term
        m_ref[...] = acc.astype(m_ref.dtype)

    rows = [pl.BlockSpec((tr, y.shape[1]), lambda i: (i, 0)) for y in ys]
    wsp = [pl.BlockSpec(w.shape, lambda i: (0, 0)) for w in ws]
    return pl.pallas_call(
        body, name=name, grid=(t // tr,),
        in_specs=rows + [pl.BlockSpec((tr, N_BRANCH * d), lambda i: (i, 0))] + wsp,
        out_specs=pl.BlockSpec((tr, d), lambda i: (i, 0)), out_shape=jax.ShapeDtypeStruct((t, d), BF16),
        compiler_params=_cparams(("parallel",)),
    )(*ys, gl, *ws)


def _gate_bwd(dm, ys, gl, ws, name):
    t = gl.shape[0]
    d = gl.shape[1] // N_BRANCH
    tr = _rows(t)

    def body(dm_ref, ya, yb, yc, gl_ref, wa, wb, wc, dya, dyb, dyc, dgl_ref, dwa, dwb, dwc):
        step = pl.program_id(0)
        dmv = dm_ref[...].astype(F32)
        for i, (y, w, dy, dw) in enumerate(((ya, wa, dya, dwa), (yb, wb, dyb, dwb), (yc, wc, dyc, dwc))):
            yb16 = y[...].astype(BF16)
            z = _dot(yb16, w[...])
            sg = jax.nn.sigmoid(gl_ref[:, i * d:(i + 1) * d])
            dgl_ref[:, i * d:(i + 1) * d] = dmv * z * sg * (1.0 - sg)
            e = (dmv * sg).astype(BF16)
            dy[...] = _dot_nt(e, w[...])
            contrib = _dot_tn(yb16, e)

            @pl.when(step == 0)
            def _(dw=dw, contrib=contrib):
                dw[...] = contrib

            @pl.when(step > 0)
            def _(dw=dw, contrib=contrib):
                dw[...] += contrib

    rows = [pl.BlockSpec((tr, y.shape[1]), lambda i: (i, 0)) for y in ys]
    wsp = [pl.BlockSpec(w.shape, lambda i: (0, 0)) for w in ws]
    gsp = pl.BlockSpec((tr, N_BRANCH * d), lambda i: (i, 0))
    return pl.pallas_call(
        body, name=name, grid=(t // tr,),
        in_specs=[pl.BlockSpec((tr, d), lambda i: (i, 0))] + rows + [gsp] + wsp,
        out_specs=rows + [gsp] + wsp,
        out_shape=[jax.ShapeDtypeStruct(y.shape, F32) for y in ys] + [jax.ShapeDtypeStruct(gl.shape, F32)]
        + [jax.ShapeDtypeStruct(w.shape, F32) for w in ws],
        compiler_params=_cparams(("arbitrary",)),
    )(dm, *ys, gl, *ws)


def _adamw(w, m, v, gparts, name):
    r, c = w.shape
    n = gparts.shape[0]
    br = LANES if r % LANES == 0 else r
    c1 = 1.0 - ADAM_B1 ** ADAM_STEP
    c2 = 1.0 - ADAM_B2 ** ADAM_STEP

    def body(w_ref, m_ref, v_ref, g_ref, go_ref, d_ref, mo_ref, vo_ref):
        g = g_ref[0].astype(F32)
        for i in range(1, n):
            g = g + g_ref[i].astype(F32)
        mn = ADAM_B1 * m_ref[...] + (1.0 - ADAM_B1) * g
        vn = ADAM_B2 * v_ref[...] + (1.0 - ADAM_B2) * (g * g)
        go_ref[...] = g
        mo_ref[...] = mn
        vo_ref[...] = vn
        d_ref[...] = -ADAM_LR * ((mn / c1) / (jnp.sqrt(vn / c2) + ADAM_EPS) + ADAM_WD * w_ref[...])

    blk = pl.BlockSpec((br, c), lambda i: (i, 0))
    return pl.pallas_call(
        body, name=name, grid=(r // br,),
        in_specs=[blk, blk, blk, pl.BlockSpec((n, br, c), lambda i: (0, i, 0))],
        out_specs=[blk] * 4, out_shape=[jax.ShapeDtypeStruct((r, c), F32)] * 4,
        compiler_params=_cparams(("parallel",)),
    )(w, m, v, gparts)


def _my_coords():
    return lax.axis_index("x"), lax.axis_index("y"), lax.axis_index("c")


COMM_SEMS = [pltpu.SemaphoreType.DMA((7,)), pltpu.SemaphoreType.DMA((7,)), pltpu.SemaphoreType.DMA]


class _Gather:
    def __init__(self, x_ref, out_ref, send_sems, recv_sems, local_sem):
        self.x_ref, self.out_ref = x_ref, out_ref
        self.send_sems, self.recv_sems, self.local_sem = send_sems, recv_sems, local_sem
        self.m_per = x_ref.shape[0]
        x, y, c = _my_coords()
        self.c = c
        self.me, self.sibling = (x, y, c), (x, y, 1 - c)
        self.chips = [(1 - x, y), (x, 1 - y), (1 - x, 1 - y)]

    def rows(self, px, py, pc):
        return self.out_ref.at[pl.ds((4 * px + 2 * py + pc) * self.m_per, self.m_per), :]

    def copy(self, k, block, to, src=None):
        return pltpu.make_async_remote_copy(
            src_ref=self.rows(*block) if src is None else src, dst_ref=self.rows(*block),
            send_sem=self.send_sems.at[k], recv_sem=self.recv_sems.at[k], device_id=to, device_id_type=MESH)

    def mine(self):
        return pltpu.make_async_copy(self.x_ref, self.rows(*self.me), self.local_sem)

    def first(self):
        out = [self.copy(0, self.me, self.sibling, src=self.x_ref)]
        return out + [self.copy(1 + j, self.me, (*chip, self.c), src=self.x_ref) for j, chip in enumerate(self.chips)]

    def passed(self):
        return [self.copy(4 + j, (*chip, self.c), self.sibling) for j, chip in enumerate(self.chips)]

    def start(self):
        self.mine().start()
        for cp in self.first():
            cp.start()

    def forward(self):
        passed = self.passed()
        for j, chip in enumerate(self.chips):
            self.copy(1 + j, (*chip, self.c), self.me).wait_recv()
            passed[j].start()

    def finish(self):
        self.copy(0, self.sibling, self.me).wait_recv()
        for j, chip in enumerate(self.chips):
            self.copy(4 + j, (*chip, 1 - self.c), self.me).wait_recv()
        for cp in self.first() + self.passed():
            cp.wait_send()
        self.mine().wait()


class _Exchange:
    def __init__(self, send_ref, recv_ref, send_sems, recv_sems, local_sem):
        self.send_ref, self.recv_ref = send_ref, recv_ref
        self.send_sems, self.recv_sems, self.local_sem = send_sems, recv_sems, local_sem
        x, y, c = _my_coords()
        self.me = 4 * x + 2 * y + c
        self.peers = []
        for k in range(1, N_DEV):
            px = 1 - x if k & 4 else x
            py = 1 - y if k & 2 else y
            pc = 1 - c if k & 1 else c
            self.peers.append((4 * px + 2 * py + pc, (px, py, pc)))

    def mine(self):
        return pltpu.make_async_copy(self.send_ref.at[self.me], self.recv_ref.at[self.me], self.local_sem)

    def copy(self, k, src_slot, dst_slot):
        return pltpu.make_async_remote_copy(
            src_ref=self.send_ref.at[src_slot], dst_ref=self.recv_ref.at[dst_slot],
            send_sem=self.send_sems.at[k], recv_sem=self.recv_sems.at[k],
            device_id=self.peers[k][1], device_id_type=MESH)

    def start(self):
        self.mine().start()
        for k, (peer, _) in enumerate(self.peers):
            self.copy(k, peer, self.me).start()

    def finish(self):
        for k, (peer, _) in enumerate(self.peers):
            self.copy(k, peer, self.me).wait_send()
            self.copy(k, self.me, peer).wait_recv()
        self.mine().wait()


def _all_gather(x_shard, in_vmem, with_sum, name):
    m_per, n = x_shard.shape

    def body(x_ref, out_ref, *rest):
        if with_sum:
            sum_ref, send_sems, recv_sems, local_sem = rest
        else:
            send_sems, recv_sems, local_sem = rest
        g = _Gather(x_ref, out_ref, send_sems, recv_sems, local_sem)
        g.start()
        g.forward()
        g.finish()
        if with_sum:
            acc = out_ref[pl.ds(0, m_per), :]
            for d in range(1, N_DEV):
                acc = acc + out_ref[pl.ds(d * m_per, m_per), :]
            sum_ref[...] = acc

    space = pltpu.VMEM if in_vmem else pl.ANY
    out_shape = [jax.ShapeDtypeStruct((N_DEV * m_per, n), x_shard.dtype)]
    out_specs = [pl.BlockSpec(memory_space=space)]
    if with_sum:
        out_shape.append(jax.ShapeDtypeStruct((m_per, n), x_shard.dtype))
        out_specs.append(pl.BlockSpec(memory_space=pltpu.VMEM))
    res = pl.pallas_call(
        body, name=name, out_shape=out_shape, in_specs=[pl.BlockSpec(memory_space=space)], out_specs=out_specs,
        scratch_shapes=COMM_SEMS, compiler_params=pltpu.CompilerParams(vmem_limit_bytes=VMEM_LIMIT),
    )(x_shard)
    return res if with_sum else res[0]


def _all_to_all(send, name):
    def body(send_ref, recv_ref, send_sems, recv_sems, local_sem):
        ex = _Exchange(send_ref, recv_ref, send_sems, recv_sems, local_sem)
        ex.start()
        ex.finish()

    return pl.pallas_call(
        body, name=name, out_shape=jax.ShapeDtypeStruct(send.shape, send.dtype),
        in_specs=[pl.BlockSpec(memory_space=pl.ANY)], out_specs=pl.BlockSpec(memory_space=pl.ANY),
        scratch_shapes=COMM_SEMS,
    )(send)


def _row(v):
    return v.reshape(1, -1)


def _ffn_fwd(x, w_in, w_out, g_pre, g_post, m, res_w, tag):
    shift, scale, gate = m[0], m[1], m[2]
    mpre = _row(g_pre * (1.0 + scale))
    mpost = _row(res_w * gate * g_post)
    h = _rms_fwd(x, mpre, _row(shift), None, BF16, tag + "_pre")
    u = _matmul(h, w_in, out_dtype=BF16, name=tag + "_in")
    s = _swiglu_fwd(u, tag + "_act")
    y = _matmul(s, w_out, name=tag + "_out")
    x_new = _rms_fwd(y, mpost, jnp.zeros_like(mpost), x, F32, tag + "_post")
    return x_new, (x, h, u, s, y, mpre, mpost)


def _sub_bwd_post(dx_new, y, mpost, g_post, gate, res_w, tag):
    dy, c1, _ = _rms_bwd(dx_new, y, mpost, None, BF16, tag + "_post_bwd")
    c1 = c1[0]
    return dy, c1 * res_w * g_post, c1 * res_w * gate


def _sub_bwd_pre(dh, x, mpre, dx_new, g_pre, scale, tag):
    dx, c2, c3 = _rms_bwd(dh, x, mpre, dx_new, F32, tag + "_pre_bwd")
    c2, c3 = c2[0], c3[0]
    return dx, c3, c2 * g_pre, c2 * (1.0 + scale)


def _ffn_bwd(dx_new, saved, w_in, w_out, g_pre, g_post, m, res_w, tag):
    x, h, u, s, y, mpre, mpost = saved
    scale, gate = m[1], m[2]
    dy, dgate, dg_post = _sub_bwd_post(dx_new, y, mpost, g_post, gate, res_w, tag)
    ds = _matmul(dy, w_out, tb=True, out_dtype=BF16, name=tag + "_out_dx")
    dw_out = _matmul(s, dy, ta=True, out_dtype=BF16, name=tag + "_out_dw")
    du = _swiglu_bwd(u, ds, tag + "_act_bwd")
    dh = _matmul(du, w_in, tb=True, name=tag + "_in_dx")
    dw_in = _matmul(h, du, ta=True, out_dtype=BF16, name=tag + "_in_dw")
    dx, dshift, dscale, dg_pre = _sub_bwd_pre(dh, x, mpre, dx_new, g_pre, scale, tag)
    return dx, dw_in, dw_out, jnp.stack([dshift, dscale, dgate]), dg_pre, dg_post


def _slope_cols(gi):
    _, r = C_GROUPS[gi]
    sl = jnp.asarray(_alibi_slopes(C_HEADS)[gi * C_HPG:(gi + 1) * C_HPG], F32) * float(r)
    return jnp.tile(jnp.repeat(sl, HDIM), r).reshape(1, r * C_OUT)


def _group_view(a, gi):
    _, r = C_GROUPS[gi]
    t = a.shape[0]
    return a.reshape(t // r, r * a.shape[1])


def _mix_fwd(x, w, g_pre, g_post, m, lb, hn, tag, gather=None):
    t, d = x.shape
    shift, scale, gate = m[0], m[1], m[2]
    mpre = _row(g_pre * (1.0 + scale))
    mpost = _row(gate * g_post)
    h = _rms_fwd(x, mpre, _row(shift), None, BF16, tag + "_pre")
    p = _matmul(h, w["w_in"], name=tag + "_in")
    hn2 = _row(jnp.tile(hn, 2))
    ya, oa, states = _hgrn_fwd(p, _row(lb), hn2, tag + "_hgrn")
    kv = p[:, COL_BK:COL_CQ].astype(BF16)
    if gather is None:
        (yb, sb_tot), gathered = _sb_fwd(p, kv, tag + "_sb"), None
    else:
        yb, sb_tot, gathered = _sb_fwd(p, kv, tag + "_sb_gather", gather)
    og, lg, qkv = [], [], []
    for gi in range(len(C_GROUPS)):
        cs = slice(gi * C_OUT, (gi + 1) * C_OUT)
        q = _group_view(p[:, COL_CQ:COL_CK][:, cs], gi)
        k = _group_view(p[:, COL_CK:COL_CV][:, cs], gi)
        v = _group_view(p[:, COL_CV:COL_GATE][:, cs], gi)
        o, lse = _dil_fwd(q, k, v, _slope_cols(gi), tag + "_dil%d" % gi)
        og.append(o.reshape(t, C_OUT))
        lg.append(lse.reshape(t, C_OUT))
        qkv.append((q, k, v))
    yc, lse_c = _dil_merge(og, lg, tag + "_dil_merge")
    gl = p[:, COL_GATE:]
    ws = (w["w_branch_a"], w["w_branch_b"], w["w_branch_c"])
    merged = _gate_fwd((ya, yb, yc), gl, ws, tag + "_gate")
    y = _matmul(merged, w["w_out"], name=tag + "_out")
    x_new = _rms_fwd(y, mpost, jnp.zeros_like(mpost), x, F32, tag + "_post")
    return x_new, (x, h, p, hn2, ya, oa, states, yb, kv, sb_tot, qkv, yc, lse_c, gl, merged, y, mpre, mpost), gathered


def _mix_bwd(dx_new, saved, w, g_pre, g_post, m, lb, tag, exchange=None):
    x, h, p, hn2, ya, oa, states, yb, kv, sb_tot, qkv, yc, lse_c, gl, merged, y, mpre, mpost = saved
    t = x.shape[0]
    scale, gate = m[1], m[2]
    dy, dgate, dg_post = _sub_bwd_post(dx_new, y, mpost, g_post, gate, 1.0, tag)
    dmerged = _matmul(dy, w["w_out"], tb=True, out_dtype=BF16, name=tag + "_out_dx")
    dw_out = _matmul(merged, dy, ta=True, out_dtype=BF16, name=tag + "_out_dw")
    ws = (w["w_branch_a"], w["w_branch_b"], w["w_branch_c"])
    dya, dyb, dyc, dgl, dwa, dwb, dwc = _gate_bwd(dmerged, (ya, yb, yc), gl, ws, tag + "_gate_bwd")
    dqa, dfa, dia, dga, dlb, dhn = _hgrn_bwd(p, _row(lb), hn2, oa, states, dya, tag + "_hgrn_bwd")
    if exchange is None:
        (dbq, dbk, dbv), received = _sb_bwd(p, kv, sb_tot, dyb, tag + "_sb_bwd"), None
    else:
        dbq, dbk, dbv, received = _sb_bwd(p, kv, sb_tot, dyb, tag + "_sb_bwd_exchange", exchange)
    dcq, dck, dcv = [], [], []
    for gi in range(len(C_GROUPS)):
        q, k, v = qkv[gi]
        dq, dk, dv = _dil_bwd(q, k, v, _group_view(dyc, gi), _group_view(yc, gi), _group_view(lse_c, gi),
                              _slope_cols(gi), tag + "_dil%d_bwd" % gi)
        dcq.append(dq.reshape(t, C_OUT))
        dck.append(dk.reshape(t, C_OUT))
        dcv.append(dv.reshape(t, C_OUT))
    dp = jnp.concatenate([dqa, dfa, dia, dga, dbq, dbk, dbv] + dcq + dck + dcv + [dgl], axis=1).astype(BF16)
    dh = _matmul(dp, w["w_in"], tb=True, name=tag + "_in_dx")
    dw_in = _matmul(h, dp, ta=True, out_dtype=BF16, name=tag + "_in_dw")
    dx, dshift, dscale, dg_pre = _sub_bwd_pre(dh, x, mpre, dx_new, g_pre, scale, tag)
    dhn_v = jnp.sum(dhn, axis=(0, 1))
    dhn_v = dhn_v[:A_VDIM] + dhn_v[A_VDIM:]
    dws = dict(w_in=dw_in, w_out=dw_out, w_branch_a=dwa.astype(BF16), w_branch_b=dwb.astype(BF16),
               w_branch_c=dwc.astype(BF16))
    return dx, dws, jnp.stack([dshift, dscale, dgate]), dg_pre, dg_post, dlb[0], dhn_v, received


class _LocalWeights:
    def __init__(self, wts):
        self.wts = wts

    def first(self):
        return None

    def shard(self, l):
        return None

    def layer(self, l, gathered):
        return {k: v[l] for k, v in self.wts.items()}

    def pack(self, l, dws):
        return dws

    def last(self, packed):
        return packed


class _ShardedWeights:
    def __init__(self, shards):
        self.shards = shards

    def shard(self, l):
        return jnp.concatenate([self.shards[k][l].astype(BF16).reshape(-1, FLAT_COLS) for k in BIG_WEIGHTS], axis=0)

    def first(self):
        return _all_gather(self.shard(0), False, False, "weights_all_gather")

    def layer(self, l, gathered):
        got = gathered.reshape(N_DEV, -1, FLAT_COLS)
        out, off = {}, 0
        for k in BIG_WEIGHTS:
            _, r, c = self.shards[k].shape
            nrow = r * c // FLAT_COLS
            blk = got[:, off:off + nrow].reshape(N_DEV, r, c)
            off += nrow
            out[k] = blk.reshape(N_DEV * r, c) if k in ROW_SHARDED else blk.transpose(1, 0, 2).reshape(r, N_DEV * c)
        return out

    def pack(self, l, dws):
        parts = []
        for k in BIG_WEIGHTS:
            _, r, c = self.shards[k].shape
            g = dws[k]
            g = g.reshape(N_DEV, r, c) if k in ROW_SHARDED else g.reshape(r, N_DEV, c).transpose(1, 0, 2)
            parts.append(g.reshape(N_DEV, -1, FLAT_COLS))
        return jnp.concatenate(parts, axis=1)

    def last(self, packed):
        return _all_to_all(packed, "grads_all_to_all")

    def partial_sums(self, received):
        out = {}
        off = 0
        for k in BIG_WEIGHTS:
            _, r, c = self.shards[k].shape
            nrow = r * c // FLAT_COLS
            out[k] = jnp.concatenate([rec[:, off:off + nrow].reshape(N_DEV, r, c) for rec in received], axis=1)
            off += nrow
        return out


def _local_step(x, target, mod, norm_g, lb_all, hnorm, supply):
    depth = mod.shape[0]
    d = x.shape[1]
    saved, wls = [], []
    gathered = supply.first()
    for l in range(depth):
        wl = supply.layer(l, gathered)
        wls.append(wl)
        x, s0 = _ffn_fwd(x, wl["ffn1_w_in"], wl["ffn1_w_out"], norm_g[l, 0], norm_g[l, 1], mod[l, 0], 0.5, "ffn1")
        nxt = supply.shard(l + 1) if l + 1 < depth else None
        x, s1, gathered = _mix_fwd(x, wl, norm_g[l, 2], norm_g[l, 3], mod[l, 1], lb_all[l], hnorm[l], "mix", nxt)
        x, s2 = _ffn_fwd(x, wl["ffn2_w_in"], wl["ffn2_w_out"], norm_g[l, 4], norm_g[l, 5], mod[l, 2], 0.5, "ffn2")
        saved.append((s0, s1, s2))
    dx, sq = _loss_head(x, target, "loss_head")
    loss = 0.5 * jnp.sum(sq) / d
    dmod, dng, dlb, dhn = [], [], [], []
    returned = [None] * depth
    pending = None
    for l in reversed(range(depth)):
        wl = wls[l]
        s0, s1, s2 = saved[l]
        dx, dwi2, dwo2, dm2, dgp2, dgq2 = _ffn_bwd(dx, s2, wl["ffn2_w_in"], wl["ffn2_w_out"], norm_g[l, 4],
                                                   norm_g[l, 5], mod[l, 2], 0.5, "ffn2")
        fuse = pending is not None and isinstance(supply, _ShardedWeights)
        dx, dwm, dm1, dgp1, dgq1, dlb_l, dhn_l, received = _mix_bwd(
            dx, s1, wl, norm_g[l, 2], norm_g[l, 3], mod[l, 1], lb_all[l], "mix", pending if fuse else None)
        if pending is not None:
            returned[l + 1] = received if fuse else pending
        dx, dwi1, dwo1, dm0, dgp0, dgq0 = _ffn_bwd(dx, s0, wl["ffn1_w_in"], wl["ffn1_w_out"], norm_g[l, 0],
                                                   norm_g[l, 1], mod[l, 0], 0.5, "ffn1")
        dmod.append(jnp.stack([dm0, dm1, dm2]))
        dng.append(jnp.stack([dgp0, dgq0, dgp1, dgq1, dgp2, dgq2]))
        dlb.append(dlb_l)
        dhn.append(dhn_l)
        pending = supply.pack(l, dict(dwm, ffn1_w_in=dwi1, ffn1_w_out=dwo1, ffn2_w_in=dwi2, ffn2_w_out=dwo2))
    returned[0] = supply.last(pending)
    rev = lambda lst: jnp.stack(lst[::-1])
    return loss, dx, rev(dmod), rev(dng), rev(dlb), rev(dhn), returned


def _lb_all(logits):
    lb_p = jax.nn.softmax(logits.astype(F32), axis=0)
    return jnp.cumsum(lb_p, axis=0) - lb_p[0:1]


def _pad_rows(a, rows):
    return jnp.pad(a, ((0, rows - a.shape[0]), (0, 0)))


def kernel(x, c, w_ada, b_ada, norm_g, ffn1_w_in, ffn1_w_out, w_in, hgrn_lb_logits, hgrn_norm_g, w_branch_a, w_branch_b, w_branch_c, w_out, ffn2_w_in, ffn2_w_out, loss_target, m_w_ada, m_b_ada, m_norm_g, m_ffn1_w_in, m_ffn1_w_out, m_w_in, m_hgrn_lb_logits, m_hgrn_norm_g, m_w_branch_a, m_w_branch_b, m_w_branch_c, m_w_out, m_ffn2_w_in, m_ffn2_w_out, v_w_ada, v_b_ada, v_norm_g, v_ffn1_w_in, v_ffn1_w_out, v_w_in, v_hgrn_lb_logits, v_hgrn_norm_g, v_w_branch_a, v_w_branch_b, v_w_branch_c, v_w_out, v_ffn2_w_in, v_ffn2_w_out):
    weights = dict(w_ada=w_ada, b_ada=b_ada, norm_g=norm_g, ffn1_w_in=ffn1_w_in, ffn1_w_out=ffn1_w_out, w_in=w_in,
                   hgrn_lb_logits=hgrn_lb_logits, hgrn_norm_g=hgrn_norm_g, w_branch_a=w_branch_a,
                   w_branch_b=w_branch_b, w_branch_c=w_branch_c, w_out=w_out, ffn2_w_in=ffn2_w_in,
                   ffn2_w_out=ffn2_w_out)
    mom1 = dict(w_ada=m_w_ada, b_ada=m_b_ada, norm_g=m_norm_g, ffn1_w_in=m_ffn1_w_in, ffn1_w_out=m_ffn1_w_out,
                w_in=m_w_in, hgrn_lb_logits=m_hgrn_lb_logits, hgrn_norm_g=m_hgrn_norm_g, w_branch_a=m_w_branch_a,
                w_branch_b=m_w_branch_b, w_branch_c=m_w_branch_c, w_out=m_w_out, ffn2_w_in=m_ffn2_w_in,
                ffn2_w_out=m_ffn2_w_out)
    mom2 = dict(w_ada=v_w_ada, b_ada=v_b_ada, norm_g=v_norm_g, ffn1_w_in=v_ffn1_w_in, ffn1_w_out=v_ffn1_w_out,
                w_in=v_w_in, hgrn_lb_logits=v_hgrn_lb_logits, hgrn_norm_g=v_hgrn_norm_g, w_branch_a=v_w_branch_a,
                w_branch_b=v_w_branch_b, w_branch_c=v_w_branch_c, w_out=v_w_out, ffn2_w_in=v_ffn2_w_in,
                ffn2_w_out=v_ffn2_w_out)
    order = list(weights)
    depth, d, ada_cols = w_ada.shape
    nd = d // LANES
    xi, yi, ci = _my_coords()
    me = 4 * xi + 2 * yi + ci

    small = jnp.concatenate([c.reshape(nd, LANES), norm_g.reshape(depth * 6, LANES)], axis=0)
    g1 = _all_gather(small, True, False, "small_all_gather").reshape(N_DEV, small.shape[0], LANES)
    c_act = _silu(g1[:, :nd].reshape(N_DEV, d))
    norm_full = g1[:, nd:].reshape(N_DEV, depth, 6, LANES).transpose(1, 2, 0, 3).reshape(depth, 6, d)

    c_pad = _pad_rows(c_act, 16)
    mod_sh = jnp.stack([_matmul(c_pad, w_ada[l], name="ada_mod")[:N_DEV]
                        + lax.dynamic_slice_in_dim(b_ada[l], me * ada_cols, ada_cols)[None]
                        for l in range(depth)])
    g2 = _all_gather(mod_sh.reshape(-1, LANES), True, False, "mod_all_gather")
    g2 = g2.reshape(N_DEV, depth, N_DEV, ada_cols)
    mod = lax.dynamic_index_in_dim(g2, me, axis=2, keepdims=False)
    mod = mod.transpose(1, 0, 2).reshape(depth, 3, 3, d)

    supply = _ShardedWeights({k: weights[k] for k in BIG_WEIGHTS})
    lb_all, lb_vjp = jax.vjp(_lb_all, hgrn_lb_logits)

    loss, dx, dmod, dng, dlb, dhn, received = _local_step(x[0], loss_target[0], mod, norm_full, lb_all,
                                                          hgrn_norm_g, supply)
    loss = lax.psum(loss, ("x", "y", "c"))

    dhn_pad = jnp.pad(dhn.reshape(-1), (0, 8 * LANES - dhn.size))
    pieces = [dmod.reshape(-1), dng.reshape(-1), dlb.reshape(-1), dhn_pad]
    sizes = [p_.size for p_ in pieces]
    smallg = jnp.concatenate(pieces).reshape(-1, LANES)
    g3, gsum = _all_gather(smallg, True, True, "small_grads_all_gather")
    g3 = g3.reshape(N_DEV, -1)
    gsum = gsum.reshape(-1)
    dmod_all = g3[:, :sizes[0]].reshape(N_DEV, depth, 9 * d)
    o1 = sizes[0]
    grads = {}
    grads["b_ada"] = gsum[:o1].reshape(depth, 9 * d)
    dng_sum = gsum[o1:o1 + sizes[1]].reshape(depth, 6, nd, LANES)
    grads["norm_g"] = lax.dynamic_index_in_dim(dng_sum, me, axis=2, keepdims=False)
    o2 = o1 + sizes[1]
    dlb_sum = gsum[o2:o2 + sizes[2]].reshape(depth, A_QK)
    grads["hgrn_lb_logits"] = lb_vjp(dlb_sum)[0]
    o3 = o2 + sizes[2]
    grads["hgrn_norm_g"] = gsum[o3:o3 + dhn.size].reshape(depth, A_VDIM)
    dmod_mine = lax.dynamic_slice_in_dim(dmod_all, me * ada_cols, ada_cols, axis=2)
    grads["w_ada"] = jnp.stack([_matmul(c_pad, _pad_rows(dmod_mine[:, l], 16), ta=True, name="ada_dw")
                                for l in range(depth)])

    gparts = supply.partial_sums(received)

    outs = {}
    for k in order:
        w = weights[k]
        w2 = w.reshape(-1, w.shape[-1])
        gp = gparts[k] if k in gparts else grads[k].reshape((1,) + w2.shape)
        res = _adamw(w2, mom1[k].reshape(w2.shape), mom2[k].reshape(w2.shape), gp, "adamw")
        outs[k] = [r.reshape(w.shape) for r in res]
    return (loss, dx[None], *[outs[k][0] for k in order], *[outs[k][1] for k in order],
            *[outs[k][2] for k in order], *[outs[k][3] for k in order])
```

```python
import functools
import math

import jax
import jax.numpy as jnp
from jax import lax
from jax.experimental import pallas as pl
from jax.experimental.pallas import tpu as pltpu

F32 = jnp.float32
BF16 = jnp.bfloat16

A_HEADS, A_KDIM, A_VDIM, A_CHUNK = 6, 128, 64, 64
B_HEADS, HDIM = 6, 64
C_GROUPS = ((128, 1), (512, 4), (2048, 16))
C_HPG = 4
C_HEADS = C_HPG * len(C_GROUPS)
N_BRANCH = 3
EPS = 1e-6
NEG_BIG = -1e30
TINY = 1e-30
A_QK = A_HEADS * A_KDIM
A_V = A_HEADS * A_VDIM
B_W = B_HEADS * HDIM
C_W = C_HEADS * HDIM
C_OUT = C_HPG * HDIM
COL_AQ, COL_AF, COL_AI, COL_AG = 0, A_QK, 2 * A_QK, 2 * A_QK + A_V
COL_BQ = 2 * A_QK + 2 * A_V
COL_BK, COL_BV = COL_BQ + B_W, COL_BQ + 2 * B_W
COL_CQ = COL_BQ + 3 * B_W
COL_CK, COL_CV = COL_CQ + C_W, COL_CQ + 2 * C_W
COL_GATE = COL_CQ + 3 * C_W

ADAM_LR, ADAM_B1, ADAM_B2, ADAM_EPS, ADAM_WD, ADAM_STEP = 0.001, 0.9, 0.999, 1e-08, 0.01, 10

N_DEV = 8
LANES = 128
VMEM_LIMIT = 48 * 1024 * 1024
MATMUL_VMEM_BUDGET = 28 * 1024 * 1024
SUB = 16
EXP_CLAMP = 80.0
MESH = pl.DeviceIdType.MESH

BIG_WEIGHTS = ("ffn1_w_in", "ffn1_w_out", "w_in", "w_branch_a", "w_branch_b", "w_branch_c", "w_out",
               "ffn2_w_in", "ffn2_w_out")
ROW_SHARDED = ("ffn1_w_out", "w_out", "ffn2_w_out")


def _cparams(sem):
    return pltpu.CompilerParams(dimension_semantics=sem, vmem_limit_bytes=VMEM_LIMIT)


def _tile(n, cap):
    best, t = None, LANES
    while t <= min(n, cap):
        if n % t == 0:
            best = t
        t += LANES
    return best or n


def _rows(t, cap=256):
    r = cap
    while t % r:
        r //= 2
    return r


def _divisors(n):
    return [t for t in range(LANES, n + 1, LANES) if n % t == 0] or [n]


def _matmul_tiles(m, n, k, a_size, b_size, o_size):
    best, best_key = None, None
    for tm in _divisors(m):
        for tn in _divisors(n):
            for tk in _divisors(k):
                if tm > 1024 or tn > 3072 or tk > 4096:
                    continue
                cast = (tm * tk * 2 if a_size > 2 else 0) + (tk * tn * 2 if b_size > 2 else 0)
                need = 2 * (tm * tk * a_size + tk * tn * b_size + tm * tn * o_size) + 2 * tm * tn * 4 + cast
                if need > MATMUL_VMEM_BUDGET:
                    continue
                key = (tm * tn * tk, tk)
                if best_key is None or key > best_key:
                    best, best_key = (tm, tn, tk), key
    return best


def _dot(a, b):
    return jnp.dot(a, b, preferred_element_type=F32)


def _dot_nt(a, b):
    return lax.dot_general(a, b, (((1,), (1,)), ((), ())), preferred_element_type=F32)


def _dot_tn(a, b):
    return lax.dot_general(a, b, (((0,), (0,)), ((), ())), preferred_element_type=F32)


def _split3(x):
    h = x.astype(BF16)
    r = x - h.astype(F32)
    m = r.astype(BF16)
    lo = (r - m.astype(F32)).astype(BF16)
    return h, m, lo


def _ones_left(mat01, x):
    h, m, lo = _split3(x)
    return _dot(mat01, h) + _dot(mat01, m) + _dot(mat01, lo)


def _silu(x):
    return x * jax.nn.sigmoid(x)


def _dsilu(x):
    s = jax.nn.sigmoid(x)
    return s * (1.0 + x * (1.0 - s))


def _matmul(a, b, *, ta=False, tb=False, out_dtype=F32, name):
    if ta:
        kdim, m = a.shape
    else:
        m, kdim = a.shape
    n = b.shape[0] if tb else b.shape[1]
    tm, tn, tk = _matmul_tiles(m, n, kdim, a.dtype.itemsize, b.dtype.itemsize, jnp.dtype(out_dtype).itemsize)
    nk = kdim // tk
    ni, nj = m // tm, n // tn
    a_bytes, b_bytes = m * kdim * a.dtype.itemsize, kdim * n * b.dtype.itemsize
    j_outer = nk == 1 and (b_bytes + a_bytes * nj) < (a_bytes + b_bytes * ni)
    dims = (((0 if ta else 1,), (1 if tb else 0,)), ((), ()))

    def body(a_ref, b_ref, o_ref, *scratch):
        p = lax.dot_general(a_ref[...].astype(BF16), b_ref[...].astype(BF16), dims, preferred_element_type=F32)
        if nk == 1:
            o_ref[...] = p.astype(o_ref.dtype)
            return
        acc = scratch[0]
        k = pl.program_id(2)

        @pl.when(k == 0)
        def _():
            acc[...] = p

        @pl.when(k > 0)
        def _():
            acc[...] += p

        @pl.when(k == nk - 1)
        def _():
            o_ref[...] = acc[...].astype(o_ref.dtype)

    def spec(shape, pick):
        if j_outer:
            return pl.BlockSpec(shape, lambda j, i, k: pick(i, j, k))
        return pl.BlockSpec(shape, lambda i, j, k: pick(i, j, k))

    a_spec = spec((tk, tm), lambda i, j, k: (k, i)) if ta else spec((tm, tk), lambda i, j, k: (i, k))
    b_spec = spec((tn, tk), lambda i, j, k: (j, k)) if tb else spec((tk, tn), lambda i, j, k: (k, j))
    return pl.pallas_call(
        body, name=name, grid=(nj, ni, nk) if j_outer else (ni, nj, nk), in_specs=[a_spec, b_spec],
        out_specs=spec((tm, tn), lambda i, j, k: (i, j)),
        out_shape=jax.ShapeDtypeStruct((m, n), out_dtype),
        scratch_shapes=[pltpu.VMEM((tm, tn), F32)] if nk > 1 else [],
        compiler_params=_cparams(("parallel", "parallel", "arbitrary")),
    )(a, b)


def _rms_fwd(z, mcol, acol, res, out_dtype, name):
    t, d = z.shape
    tr = _rows(t)
    has_res = res is not None

    def body(*refs):
        if has_res:
            z_ref, m_ref, a_ref, r_ref, o_ref = refs
        else:
            z_ref, m_ref, a_ref, o_ref = refs
        zf = z_ref[...]
        r = lax.rsqrt(jnp.mean(zf * zf, axis=-1, keepdims=True) + EPS)
        y = zf * r * m_ref[...] + a_ref[...]
        if has_res:
            y = r_ref[...] + y
        o_ref[...] = y.astype(o_ref.dtype)

    row = pl.BlockSpec((tr, d), lambda i: (i, 0))
    col = pl.BlockSpec((1, d), lambda i: (0, 0))
    ins = [z, mcol, acol] + ([res] if has_res else [])
    return pl.pallas_call(
        body, name=name, grid=(t // tr,), in_specs=[row, col, col] + ([row] if has_res else []),
        out_specs=row, out_shape=jax.ShapeDtypeStruct((t, d), out_dtype),
        compiler_params=_cparams(("parallel",)),
    )(*ins)


def _rms_bwd(d_out, z, mcol, dres, out_dtype, name):
    t, d = z.shape
    tr = _rows(t)
    has_res = dres is not None

    def body(*refs):
        if has_res:
            d_ref, z_ref, m_ref, r_ref, o_ref, s1_ref, s2_ref = refs
        else:
            d_ref, z_ref, m_ref, o_ref, s1_ref, s2_ref = refs
        i = pl.program_id(0)
        zf = z_ref[...]
        r = lax.rsqrt(jnp.mean(zf * zf, axis=-1, keepdims=True) + EPS)
        zh = zf * r
        df = d_ref[...].astype(F32)
        dzh = df * m_ref[...]
        dz = r * (dzh - zh * jnp.mean(dzh * zh, axis=-1, keepdims=True))
        if has_res:
            dz = dz + r_ref[...]
        o_ref[...] = dz.astype(o_ref.dtype)
        s1 = jnp.sum(df * zh, axis=0, keepdims=True)
        s2 = jnp.sum(df, axis=0, keepdims=True)

        @pl.when(i == 0)
        def _():
            s1_ref[...] = s1
            s2_ref[...] = s2

        @pl.when(i > 0)
        def _():
            s1_ref[...] += s1
            s2_ref[...] += s2

    row = pl.BlockSpec((tr, d), lambda i: (i, 0))
    col = pl.BlockSpec((1, d), lambda i: (0, 0))
    ins = [d_out, z, mcol] + ([dres] if has_res else [])
    return pl.pallas_call(
        body, name=name, grid=(t // tr,), in_specs=[row, row, col] + ([row] if has_res else []),
        out_specs=[row, col, col],
        out_shape=[jax.ShapeDtypeStruct((t, d), out_dtype), jax.ShapeDtypeStruct((1, d), F32),
                   jax.ShapeDtypeStruct((1, d), F32)],
        compiler_params=_cparams(("arbitrary",)),
    )(*ins)


def _swiglu_fwd(u, name):
    t, f2 = u.shape
    f = f2 // 2
    tr = _rows(t)

    def body(u_ref, s_ref):
        a = u_ref[:, :f].astype(F32)
        b = u_ref[:, f:].astype(F32)
        s_ref[...] = (_silu(a) * b).astype(s_ref.dtype)

    return pl.pallas_call(
        body, name=name, grid=(t // tr,), in_specs=[pl.BlockSpec((tr, f2), lambda i: (i, 0))],
        out_specs=pl.BlockSpec((tr, f), lambda i: (i, 0)), out_shape=jax.ShapeDtypeStruct((t, f), BF16),
        compiler_params=_cparams(("parallel",)),
    )(u)


def _swiglu_bwd(u, ds, name):
    t, f2 = u.shape
    f = f2 // 2
    tr = _rows(t)

    def body(u_ref, ds_ref, du_ref):
        a = u_ref[:, :f].astype(F32)
        b = u_ref[:, f:].astype(F32)
        g = ds_ref[...].astype(F32)
        du_ref[:, :f] = (g * b * _dsilu(a)).astype(du_ref.dtype)
        du_ref[:, f:] = (g * _silu(a)).astype(du_ref.dtype)

    return pl.pallas_call(
        body, name=name, grid=(t // tr,),
        in_specs=[pl.BlockSpec((tr, f2), lambda i: (i, 0)), pl.BlockSpec((tr, f), lambda i: (i, 0))],
        out_specs=pl.BlockSpec((tr, f2), lambda i: (i, 0)), out_shape=jax.ShapeDtypeStruct((t, f2), BF16),
        compiler_params=_cparams(("parallel",)),
    )(u, ds)


def _loss_head(y, target, name):
    t, d = y.shape
    tr = _rows(t)

    def body(y_ref, t_ref, dy_ref, sq_ref):
        i = pl.program_id(0)
        e = y_ref[...] - t_ref[...]
        dy_ref[...] = e * (1.0 / d)
        s = jnp.sum(e * e, axis=0, keepdims=True)

        @pl.when(i == 0)
        def _():
            sq_ref[...] = s

        @pl.when(i > 0)
        def _():
            sq_ref[...] += s

    row = pl.BlockSpec((tr, d), lambda i: (i, 0))
    col = pl.BlockSpec((1, d), lambda i: (0, 0))
    return pl.pallas_call(
        body, name=name, grid=(t // tr,), in_specs=[row, row], out_specs=[row, col],
        out_shape=[jax.ShapeDtypeStruct((t, d), F32), jax.ShapeDtypeStruct((1, d), F32)],
        compiler_params=_cparams(("arbitrary",)),
    )(y, target)


def _hgrn_consts():
    c = A_CHUNK
    shift = SUB.bit_length() - 1
    r = lax.broadcasted_iota(jnp.int32, (c, c), 0)
    s = lax.broadcasted_iota(jnp.int32, (c, c), 1)
    sub_r = lax.shift_right_logical(r, shift)
    incl = s <= r
    masks = [jnp.logical_and(sub_r == i, incl) for i in range(c // SUB)]
    rev_incl = jnp.where(s >= r, 1.0, 0.0).astype(BF16)
    r2 = lax.broadcasted_iota(jnp.int32, (2 * c + 8, c), 0)
    s2 = lax.broadcasted_iota(jnp.int32, (2 * c + 8, c), 1)
    sub_start = lax.shift_left(lax.shift_right_logical(r2 - c, shift), shift)
    running = jnp.where(s2 <= r2, 1.0, 0.0)
    before = jnp.where(s2 < sub_start, 1.0, 0.0)
    stack = jnp.where(r2 < c, running, jnp.where(r2 < 2 * c, before, 1.0)).astype(BF16)
    return stack, masks, incl, rev_incl


def _hgrn_chunk(q_raw, f_raw, lbv, stack):
    c = A_CHUNK
    sg = jax.nn.sigmoid(f_raw)
    sgn = jax.nn.sigmoid(-f_raw)
    f = lbv + (1.0 - lbv) * sg
    logf = jnp.log(jnp.maximum(f, TINY))
    k = (1.0 - lbv) * sgn
    q = _silu(q_raw)
    bb = _ones_left(stack, logf)
    b = bb[:c]
    bsrow = bb[c:2 * c]
    b_end = bb[2 * c:2 * c + 1]
    e_sub = jnp.exp(b - bsrow)
    e_b = jnp.exp(b)
    e_end = jnp.exp(b_end - b)
    qs = q * e_sub
    q_in = q * e_b
    kend = k * e_end
    kfac = [jnp.exp(jnp.minimum(bsrow[i * SUB:i * SUB + 1] - b, EXP_CLAMP)) for i in range(c // SUB)]
    return dict(sg=sg, sgn=sgn, f=f, k=k, q=q, b=b, b_end=b_end, e_sub=e_sub, e_b=e_b, e_end=e_end,
                qs=qs, q_in=q_in, kend=kend, kfac=kfac)


def _hgrn_scores(ch, masks):
    qs_b = ch["qs"].astype(BF16)
    a = None
    for i, mk in enumerate(masks):
        ki = (ch["k"] * ch["kfac"][i]).astype(BF16)
        part = jnp.where(mk, _dot_nt(qs_b, ki), 0.0)
        a = part if a is None else a + part
    return a


def _hgrn_fwd(p, lb, hn2, name):
    t = p.shape[0]
    tb = _rows(t)
    nt = t // tb
    nc = tb // A_CHUNK
    c = A_CHUNK

    def body(q_ref, f_ref, i_ref, g_ref, lb_ref, hn_ref, y_ref, o_ref, st_ref, s_scr):
        j = pl.program_id(1)

        @pl.when(j == 0)
        def _():
            s_scr[...] = jnp.zeros_like(s_scr)

        stack, masks, _, _ = _hgrn_consts()
        states = [s_scr[0], s_scr[1]]
        for ci in range(nc):
            rows = pl.ds(ci * c, c)
            for hh in range(2):
                lsl = slice(A_KDIM * hh, A_KDIM * (hh + 1))
                hsl = slice(A_VDIM * hh, A_VDIM * (hh + 1))
                ch = _hgrn_chunk(q_ref[rows, lsl], f_ref[rows, lsl], lb_ref[:, lsl], stack)
                v = i_ref[rows, hsl].astype(BF16)
                st = states[hh]
                st_ref[hh, ci] = st
                a = _hgrn_scores(ch, masks)
                o_ref[rows, hsl] = _dot_nt(ch["q_in"].astype(BF16), st.astype(BF16)) + _dot(a.astype(BF16), v)
                states[hh] = st * jnp.exp(ch["b_end"]) + _dot_tn(v, ch["kend"].astype(BF16))
        s_scr[0] = states[0]
        s_scr[1] = states[1]
        for hh in range(2):
            hsl = slice(A_VDIM * hh, A_VDIM * (hh + 1))
            o = o_ref[:, hsl]
            r = lax.rsqrt(jnp.mean(o * o, axis=-1, keepdims=True) + EPS)
            y_ref[:, hsl] = (o * r * hn_ref[:, hsl] * _silu(g_ref[:, hsl])).astype(y_ref.dtype)

    w2 = 2 * A_KDIM
    return pl.pallas_call(
        body, name=name, grid=(A_HEADS // 2, nt),
        in_specs=[pl.BlockSpec((tb, w2), lambda h, j: (j, COL_AQ // w2 + h)),
                  pl.BlockSpec((tb, w2), lambda h, j: (j, COL_AF // w2 + h)),
                  pl.BlockSpec((tb, LANES), lambda h, j: (j, COL_AI // LANES + h)),
                  pl.BlockSpec((tb, LANES), lambda h, j: (j, COL_AG // LANES + h)),
                  pl.BlockSpec((1, w2), lambda h, j: (0, h)),
                  pl.BlockSpec((1, LANES), lambda h, j: (0, 0))],
        out_specs=[pl.BlockSpec((tb, LANES), lambda h, j: (j, h)),
                   pl.BlockSpec((tb, LANES), lambda h, j: (j, h)),
                   pl.BlockSpec((2, nc, A_VDIM, A_KDIM), lambda h, j: (h, j, 0, 0))],
        out_shape=[jax.ShapeDtypeStruct((t, A_V), BF16), jax.ShapeDtypeStruct((t, A_V), F32),
                   jax.ShapeDtypeStruct((A_HEADS, t // c, A_VDIM, A_KDIM), F32)],
        scratch_shapes=[pltpu.VMEM((2, A_VDIM, A_KDIM), F32)],
        compiler_params=_cparams(("parallel", "arbitrary")),
    )(p, p, p, p, lb, hn2)


def _hgrn_bwd(p, lb, hn2, o_raw, states, dya, name):
    t = p.shape[0]
    tb = _rows(t)
    nt = t // tb
    nc = tb // A_CHUNK
    c = A_CHUNK

    def body(q_ref, f_ref, i_ref, g_ref, lb_ref, hn_ref, o_ref, st_ref, dy_ref,
             dq_ref, df_ref, di_ref, dg_ref, dlb_ref, dhn_ref, ds_scr, do_scr):
        j = pl.program_id(1)

        @pl.when(j == 0)
        def _():
            ds_scr[...] = jnp.zeros_like(ds_scr)
            dlb_ref[...] = jnp.zeros_like(dlb_ref)
            dhn_ref[...] = jnp.zeros_like(dhn_ref)

        stack, masks, incl, rev_incl = _hgrn_consts()
        for hh in range(2):
            hsl = slice(A_VDIM * hh, A_VDIM * (hh + 1))
            o = o_ref[:, hsl]
            g = g_ref[:, hsl]
            dy = dy_ref[:, hsl].astype(F32)
            hn = hn_ref[:, hsl]
            r = lax.rsqrt(jnp.mean(o * o, axis=-1, keepdims=True) + EPS)
            oh = o * r
            sgate = _silu(g)
            dg_ref[:, hsl] = dy * oh * hn * _dsilu(g)
            dhn_ref[0, :, hsl] += jnp.sum(dy * oh * sgate, axis=0, keepdims=True)
            doh = dy * hn * sgate
            do_scr[:, hsl] = r * (doh - oh * jnp.mean(doh * oh, axis=-1, keepdims=True))

        dstates = [ds_scr[0], ds_scr[1]]
        dlb_acc = [jnp.zeros((1, A_KDIM), F32), jnp.zeros((1, A_KDIM), F32)]
        for ci in reversed(range(nc)):
            rows = pl.ds(ci * c, c)
            for hh in range(2):
                lsl = slice(A_KDIM * hh, A_KDIM * (hh + 1))
                hsl = slice(A_VDIM * hh, A_VDIM * (hh + 1))
                lbv = lb_ref[:, lsl]
                q_raw = q_ref[rows, lsl]
                f_raw = f_ref[rows, lsl]
                ch = _hgrn_chunk(q_raw, f_raw, lbv, stack)
                v = i_ref[rows, hsl].astype(BF16)
                do = do_scr[rows, hsl]
                do_b = do.astype(BF16)
                st = st_ref[hh, ci]
                st_b = st.astype(BF16)
                dst = dstates[hh]
                dst_b = dst.astype(BF16)
                qs_b = ch["qs"].astype(BF16)
                kend_b = ch["kend"].astype(BF16)
                a = _hgrn_scores(ch, masks)
                da = jnp.where(incl, _dot_nt(do_b, v), 0.0)
                dv = _dot_tn(a.astype(BF16), do_b) + _dot_nt(kend_b, dst_b)
                dq_i = None
                dk_i = None
                kdk_i = None
                for i, mk in enumerate(masks):
                    dam = jnp.where(mk, da, 0.0).astype(BF16)
                    ki = (ch["k"] * ch["kfac"][i]).astype(BF16)
                    pq = _dot(dam, ki)
                    pk = _dot_tn(dam, qs_b)
                    dq_i = pq if dq_i is None else dq_i + pq
                    dk_i = ch["kfac"][i] * pk if dk_i is None else dk_i + ch["kfac"][i] * pk
                    kdk_i = ki.astype(F32) * pk if kdk_i is None else kdk_i + ki.astype(F32) * pk
                dq_x = _dot(do_b, st_b)
                dk_x = _dot(v, dst_b)
                dq = ch["e_sub"] * dq_i + ch["e_b"] * dq_x
                dk = dk_i + ch["e_end"] * dk_x
                dstates[hh] = dst * jnp.exp(ch["b_end"]) + _dot_tn(do_b, ch["q_in"].astype(BF16))
                kx = ch["kend"] * dk_x
                db = (qs_b.astype(F32) * dq_i + ch["q_in"] * dq_x) - (kdk_i + kx)
                later = (jnp.exp(ch["b_end"]) * jnp.sum(dst * st, axis=0, keepdims=True)
                         + jnp.sum(kx, axis=0, keepdims=True))
                dlogf = later + _ones_left(rev_incl, db)
                dfv = jnp.where(ch["f"] > TINY, dlogf / ch["f"], 0.0)
                dq_ref[rows, lsl] = dq * _dsilu(q_raw)
                df_ref[rows, lsl] = (1.0 - lbv) * ch["sg"] * ch["sgn"] * (dfv - dk)
                dlb_acc[hh] = dlb_acc[hh] + jnp.sum(dfv * (1.0 - ch["sg"]) - dk * ch["sgn"], axis=0, keepdims=True)
                di_ref[rows, hsl] = dv
        for hh in range(2):
            ds_scr[hh] = dstates[hh]
            dlb_ref[:, A_KDIM * hh:A_KDIM * (hh + 1)] += dlb_acc[hh]

    w2 = 2 * A_KDIM
    rev = lambda j: nt - 1 - j
    return pl.pallas_call(
        body, name=name, grid=(A_HEADS // 2, nt),
        in_specs=[pl.BlockSpec((tb, w2), lambda h, j: (rev(j), COL_AQ // w2 + h)),
                  pl.BlockSpec((tb, w2), lambda h, j: (rev(j), COL_AF // w2 + h)),
                  pl.BlockSpec((tb, LANES), lambda h, j: (rev(j), COL_AI // LANES + h)),
                  pl.BlockSpec((tb, LANES), lambda h, j: (rev(j), COL_AG // LANES + h)),
                  pl.BlockSpec((1, w2), lambda h, j: (0, h)),
                  pl.BlockSpec((1, LANES), lambda h, j: (0, 0)),
                  pl.BlockSpec((tb, LANES), lambda h, j: (rev(j), h)),
                  pl.BlockSpec((2, nc, A_VDIM, A_KDIM), lambda h, j: (h, rev(j), 0, 0)),
                  pl.BlockSpec((tb, LANES), lambda h, j: (rev(j), h))],
        out_specs=[pl.BlockSpec((tb, w2), lambda h, j: (rev(j), h)),
                   pl.BlockSpec((tb, w2), lambda h, j: (rev(j), h)),
                   pl.BlockSpec((tb, LANES), lambda h, j: (rev(j), h)),
                   pl.BlockSpec((tb, LANES), lambda h, j: (rev(j), h)),
                   pl.BlockSpec((1, w2), lambda h, j: (0, h)),
                   pl.BlockSpec((1, 1, LANES), lambda h, j: (h, 0, 0))],
        out_shape=[jax.ShapeDtypeStruct((t, A_QK), F32), jax.ShapeDtypeStruct((t, A_QK), F32),
                   jax.ShapeDtypeStruct((t, A_V), F32), jax.ShapeDtypeStruct((t, A_V), F32),
                   jax.ShapeDtypeStruct((1, A_QK), F32), jax.ShapeDtypeStruct((A_HEADS // 2, 1, LANES), F32)],
        scratch_shapes=[pltpu.VMEM((2, A_VDIM, A_KDIM), F32), pltpu.VMEM((tb, LANES), F32)],
        compiler_params=_cparams(("parallel", "arbitrary")),
    )(p, p, p, p, lb, hn2, o_raw, states, dya)


BLK = 128
SCALE = HDIM ** -0.5
SB_CHUNK = 4


def _softplus(z):
    return jnp.maximum(z, 0.0) + jnp.log(1.0 + jnp.exp(-jnp.abs(z)))


def _split2(x):
    hi = x.astype(BF16)
    return hi, (x - hi.astype(F32)).astype(BF16)


def _sb_sum_matrix(keep):
    sp = lax.broadcasted_iota(jnp.int32, (2 * BLK, 2 * BLK), 0) & (BLK - 1)
    s = lax.broadcasted_iota(jnp.int32, (2 * BLK, 2 * BLK), 1)
    return jnp.where(jnp.logical_or(s >= BLK, keep(sp, s)), 1.0, 0.0).astype(BF16)


def _sb_fwd(p, kv, name, gather=None):
    t = p.shape[0]
    nq = t // BLK
    nh = B_HEADS // 2
    cw = SB_CHUNK * BLK
    fused = gather is not None
    n = len(gather) if fused else 0

    def body(*refs):
        q_ref, kb, vb = refs[:3]
        o_ref, tot_ref = refs[3 + n:5 + n]
        zbuf, stage, sbuf, abuf = refs[5 + 2 * n:9 + 2 * n]
        hp = pl.program_id(0)
        qi = pl.program_id(1)
        if fused:
            g = _Many(_Gather, refs[3:3 + n], refs[5 + n:5 + 2 * n], *refs[9 + 2 * n:])
            pl.when(jnp.logical_and(hp == 0, qi == 0))(g.start)
            pl.when(jnp.logical_and(hp == nh - 1, qi == 0))(g.forward)

        @pl.when(qi == 0)
        def _():
            abuf[...] = jnp.zeros_like(abuf)

        row = lax.broadcasted_iota(jnp.int32, (BLK, BLK), 0)
        col = lax.broadcasted_iota(jnp.int32, (BLK, BLK), 1)
        sums = _sb_sum_matrix(lambda sp, s: sp >= s)
        hsl = [slice(HDIM * h, HDIM * (h + 1)) for h in range(2)]
        nchunk = qi // SB_CHUNK + 1
        for h in range(2):
            zbuf[h] = _dot_nt((q_ref[:, hsl[h]] * SCALE).astype(BF16), kb[:, hsl[h]])

        def causal(j):
            return (col + j * BLK) < (row + qi * BLK)

        def l_pass(c, carry):
            for b in range(SB_CHUNK):
                j = c * SB_CHUNK + b
                off = pl.multiple_of(j * BLK, BLK)
                mask = causal(j)
                for h in range(2):
                    lm = jnp.where(mask, -_softplus(zbuf[h, :, pl.ds(off, BLK)]), 0.0)
                    hi, lo = _split2(lm)
                    stage[h, pl.ds(off, BLK), :BLK] = hi
                    stage[h, pl.ds(off, BLK), BLK:] = lo
            return carry

        lax.fori_loop(0, nchunk, l_pass, 0)

        def sum_pass(c, carry):
            rows = pl.ds(pl.multiple_of(c * cw, cw), cw)
            for h in range(2):
                sbuf[h, rows, :] = _dot(stage[h, rows, :], sums)
            return carry

        lax.fori_loop(0, nchunk, sum_pass, 0)

        def a_pass(it, carry):
            c = nchunk - 1 - it
            runs = list(carry)
            for b in reversed(range(SB_CHUNK)):
                j = c * SB_CHUNK + b
                off = pl.multiple_of(j * BLK, BLK)
                mask = causal(j)
                for h in range(2):
                    s = sbuf[h, pl.ds(off, BLK), :BLK]
                    a = jnp.where(mask, jnp.exp(zbuf[h, :, pl.ds(off, BLK)] + s + runs[h]), 0.0)
                    abuf[h, :, pl.ds(off, BLK)] = a.astype(BF16)
                    runs[h] = runs[h] + sbuf[h, pl.ds(off, BLK), BLK:]
            return tuple(runs)

        zero = jnp.zeros((BLK, BLK), F32)
        runs = lax.fori_loop(0, nchunk, a_pass, (zero, zero))
        for h in range(2):
            tot_ref[:, hsl[h]] = runs[h][:, :HDIM]
            o_ref[:, hsl[h]] = _dot(abuf[h], vb[:, hsl[h]])
        if fused:
            pl.when(jnp.logical_and(hp == nh - 1, qi == nq - 1))(g.finish)

    out_blk = pl.BlockSpec((BLK, LANES), lambda h, i: (i, h))
    hbm = pl.BlockSpec(memory_space=pl.ANY)
    in_specs = [pl.BlockSpec((BLK, LANES), lambda h, i: (i, COL_BQ // LANES + h)),
                pl.BlockSpec((t, LANES), lambda h, i: (0, h)),
                pl.BlockSpec((t, LANES), lambda h, i: (0, B_W // LANES + h))]
    out_shape = [jax.ShapeDtypeStruct((t, B_W), F32)] * 2
    scratch = [pltpu.VMEM((2, BLK, t), F32), pltpu.VMEM((2, t, 2 * BLK), BF16),
               pltpu.VMEM((2, t, 2 * BLK), F32), pltpu.VMEM((2, BLK, t), BF16)]
    if fused:
        out_shape = out_shape + _gathered_shapes(gather)
    return pl.pallas_call(
        body, name=name, grid=(nh, nq),
        in_specs=in_specs + [hbm] * n,
        out_specs=[out_blk, out_blk] + [hbm] * n,
        out_shape=out_shape,
        scratch_shapes=scratch + (_comm_sems(n) if fused else []),
        compiler_params=_cparams(("arbitrary", "arbitrary")),
    )(p, kv, kv, *(gather if fused else []))


def _sb_bwd(p, kv, tot, do, name, exchange=None):
    t = p.shape[0]
    nq = t // BLK
    nh = B_HEADS // 2
    cw = SB_CHUNK * BLK
    fused = exchange is not None
    n = len(exchange) if fused else 0

    def body(*refs):
        q_ref, kb, vb, tot_ref, do_ref = refs[:5]
        dq_ref, dk_ref, dv_ref = refs[5 + n:8 + n]
        zbuf, dabuf, stage, gstage, sbuf, abuf, dzbuf, dkt, dvt = refs[8 + 2 * n:17 + 2 * n]
        hp = pl.program_id(0)
        qi = pl.program_id(1)
        if fused:
            ex = _Many(_Exchange, refs[5:5 + n], refs[8 + n:8 + 2 * n], *refs[17 + 2 * n:])
            pl.when(jnp.logical_and(hp == 0, qi == 0))(ex.start)

        @pl.when(qi == 0)
        def _():
            dkt[...] = jnp.zeros_like(dkt)
            dvt[...] = jnp.zeros_like(dvt)
            dzbuf[...] = jnp.zeros_like(dzbuf)
            abuf[...] = jnp.zeros_like(abuf)

        row = lax.broadcasted_iota(jnp.int32, (BLK, BLK), 0)
        col = lax.broadcasted_iota(jnp.int32, (BLK, BLK), 1)
        sums = _sb_sum_matrix(lambda sp, s: sp <= s)
        hsl = [slice(HDIM * h, HDIM * (h + 1)) for h in range(2)]
        dob = [do_ref[:, hsl[h]].astype(BF16) for h in range(2)]
        total =[jnp.concatenate([tot_ref[:, hsl[h]], tot_ref[:, hsl[h]]], axis=1) for h in range(2)]
        nchunk = qi // SB_CHUNK + 1
        for h in range(2):
            zbuf[h] = _dot_nt((q_ref[:, hsl[h]] * SCALE).astype(BF16), kb[:, hsl[h]])
            dabuf[h] = _dot_nt(dob[h], vb[:, hsl[h]])

        def causal(j):
            return (col + j * BLK) < (row + qi * BLK)

        def blocks(c):
            for b in range(SB_CHUNK):
                j = c * SB_CHUNK + b
                yield j, pl.ds(pl.multiple_of(j * BLK, BLK), BLK)

        def l_pass(c, carry):
            for j, blk_ in blocks(c):
                mask = causal(j)
                for h in range(2):
                    lm = jnp.where(mask, -_softplus(zbuf[h, :, blk_]), 0.0)
                    hi, lo = _split2(lm)
                    stage[h, blk_, :BLK] = hi
                    stage[h, blk_, BLK:] = lo
            return carry

        lax.fori_loop(0, nchunk, l_pass, 0)

        def sum_pass(src):
            def run_(c, carry):
                rows = pl.ds(pl.multiple_of(c * cw, cw), cw)
                for h in range(2):
                    sbuf[h, rows, :] = _dot(src[h, rows, :], sums)
                return carry
            lax.fori_loop(0, nchunk, run_, 0)

        sum_pass(stage)

        def g_pass(c, carry):
            runs = list(carry)
            for j, blk_ in blocks(c):
                mask = causal(j)
                for h in range(2):
                    lm = stage[h, blk_, :BLK].astype(F32) + stage[h, blk_, BLK:].astype(F32)
                    log_a = zbuf[h, :, blk_] + lm + (total[h] - runs[h] - sbuf[h, blk_, :BLK])
                    a = jnp.where(mask, jnp.exp(log_a), 0.0)
                    abuf[h, :, blk_] = a.astype(BF16)
                    hi, lo = _split2(a * dabuf[h, :, blk_])
                    gstage[h, blk_, :BLK] = hi
                    gstage[h, blk_, BLK:] = lo
                    runs[h] = runs[h] + sbuf[h, blk_, BLK:]
            return tuple(runs)

        zero = jnp.zeros((BLK, BLK), F32)
        lax.fori_loop(0, nchunk, g_pass, (zero, zero))
        sum_pass(gstage)

        def dz_pass(c, carry):
            runs = list(carry)
            for j, blk_ in blocks(c):
                mask = causal(j)
                for h in range(2):
                    lm = stage[h, blk_, :BLK].astype(F32) + stage[h, blk_, BLK:].astype(F32)
                    g = gstage[h, blk_, :BLK].astype(F32) + gstage[h, blk_, BLK:].astype(F32)
                    before = runs[h] + sbuf[h, blk_, :BLK] - g
                    dz = jnp.where(mask, g * jnp.exp(lm) - jnp.exp(zbuf[h, :, blk_] + lm) * before, 0.0)
                    dzbuf[h, :, blk_] = (dz * SCALE).astype(BF16)
                    runs[h] = runs[h] + sbuf[h, blk_, BLK:]
            return tuple(runs)

        lax.fori_loop(0, nchunk, dz_pass, (zero, zero))
        for h in range(2):
            dq_ref[:, hsl[h]] = _dot(dzbuf[h], kb[:, hsl[h]])
        q_t = q_ref[...].T.astype(BF16)
        do_t = do_ref[...].T.astype(BF16)
        for h in range(2):
            dkt[hsl[h], :] += _dot(q_t[hsl[h], :], dzbuf[h])
            dvt[hsl[h], :] += _dot(do_t[hsl[h], :], abuf[h])

        @pl.when(qi == nq - 1)
        def _():
            dk_ref[...] = dkt[...].T
            dv_ref[...] = dvt[...].T

        if fused:
            pl.when(jnp.logical_and(hp == nh - 1, qi == nq - 1))(ex.finish)

    blk = lambda h, i: (i, h)
    whole = lambda h, i: (0, h)
    hbm = pl.BlockSpec(memory_space=pl.ANY)
    in_specs = [pl.BlockSpec((BLK, LANES), lambda h, i: (i, COL_BQ // LANES + h)),
                pl.BlockSpec((t, LANES), lambda h, i: (0, h)),
                pl.BlockSpec((t, LANES), lambda h, i: (0, B_W // LANES + h)),
                pl.BlockSpec((BLK, LANES), blk), pl.BlockSpec((BLK, LANES), blk)]
    out_specs = [pl.BlockSpec((BLK, LANES), blk), pl.BlockSpec((t, LANES), whole), pl.BlockSpec((t, LANES), whole)]
    out_shape = [jax.ShapeDtypeStruct((t, B_W), F32)] * 3
    scratch = [pltpu.VMEM((2, BLK, t), F32), pltpu.VMEM((2, BLK, t), F32),
               pltpu.VMEM((2, t, 2 * BLK), BF16), pltpu.VMEM((2, t, 2 * BLK), BF16),
               pltpu.VMEM((2, t, 2 * BLK), F32), pltpu.VMEM((2, BLK, t), BF16),
               pltpu.VMEM((2, BLK, t), BF16), pltpu.VMEM((LANES, t), F32), pltpu.VMEM((LANES, t), F32)]
    if fused:
        out_shape = out_shape + [jax.ShapeDtypeStruct(e.shape, e.dtype) for e in exchange]
    return pl.pallas_call(
        body, name=name, grid=(nh, nq),
        in_specs=in_specs + [hbm] * n,
        out_specs=out_specs + [hbm] * n,
        out_shape=out_shape,
        scratch_shapes=scratch + (_comm_sems(n) if fused else []),
        compiler_params=_cparams(("arbitrary", "arbitrary")),
    )(p, kv, kv, tot, do, *(exchange if fused else []))


def _alibi_slopes(n):
    def pow2(m):
        start = 2.0 ** (-8.0 / m)
        return [start ** (i + 1) for i in range(m)]
    if math.log2(n).is_integer():
        s = pow2(n)
    else:
        c = 2 ** int(math.floor(math.log2(n)))
        s = pow2(c) + pow2(2 * c)[0::2][: n - c]
    return sorted(s, reverse=True)


def _dil_scores(qh, kh, sl, prev, exists=None):
    row = lax.broadcasted_iota(jnp.int32, (BLK, BLK), 0)
    col = lax.broadcasted_iota(jnp.int32, (BLK, BLK), 1)
    dist = row - col + (BLK if prev else 0)
    if prev:
        valid = (col - row) >= jnp.where(exists, 0, 2 * BLK)
    else:
        valid = col <= row
    s = _dot_nt(qh, kh) - sl * dist.astype(F32)
    return s, valid


def _dil_fwd(q, k, v, slope_cols, name):
    ln, cw = q.shape
    nb = ln // BLK

    def body(q_ref, kc_ref, kp_ref, vc_ref, vp_ref, sl_ref, o_ref, lse_ref):
        i = pl.program_id(1)
        for h in range(2):
            hs = slice(HDIM * h, HDIM * (h + 1))
            sl = sl_ref[:, HDIM * h:HDIM * h + 1]
            qh = (q_ref[:, hs] * SCALE).astype(BF16)
            sc, vc_ok = _dil_scores(qh, kc_ref[:, hs].astype(BF16), sl, False)
            sp, vp_ok = _dil_scores(qh, kp_ref[:, hs].astype(BF16), sl, True, i > 0)
            sc = jnp.where(vc_ok, sc, NEG_BIG)
            sp = jnp.where(vp_ok, sp, NEG_BIG)
            m = jnp.maximum(jnp.max(sc, axis=1, keepdims=True), jnp.max(sp, axis=1, keepdims=True))
            pc = jnp.exp(sc - m)
            pp = jnp.exp(sp - m)
            den = jnp.sum(pc, axis=1, keepdims=True) + jnp.sum(pp, axis=1, keepdims=True)
            o = _dot(pc.astype(BF16), vc_ref[:, hs].astype(BF16)) + _dot(pp.astype(BF16), vp_ref[:, hs].astype(BF16))
            o_ref[:, hs] = o / den
            lse_ref[:, hs] = jnp.broadcast_to(m + jnp.log(den), (BLK, HDIM))

    cur = pl.BlockSpec((BLK, LANES), lambda c, i: (i, c))
    prv = pl.BlockSpec((BLK, LANES), lambda c, i: (jnp.maximum(i - 1, 0), c))
    return pl.pallas_call(
        body, name=name, grid=(cw // LANES, nb),
        in_specs=[cur, cur, prv, cur, prv, pl.BlockSpec((1, LANES), lambda c, i: (0, c))],
        out_specs=[cur, cur], out_shape=[jax.ShapeDtypeStruct((ln, cw), F32)] * 2,
        compiler_params=_cparams(("parallel", "parallel")),
    )(q, k, k, v, v, slope_cols)


def _dil_bwd(q, k, v, do, o, lse, slope_cols, name):
    ln, cw = q.shape
    nb = ln // BLK

    def body(q_ref, qn_ref, kc_ref, kp_ref, vc_ref, vp_ref, do_ref, don_ref, o_ref, on_ref, l_ref, ln_ref, sl_ref,
             dq_ref, dk_ref, dv_ref):
        i = pl.program_id(1)
        has_prev = i > 0
        has_next = i < nb - 1
        for h in range(2):
            hs = slice(HDIM * h, HDIM * (h + 1))
            sl = sl_ref[:, HDIM * h:HDIM * h + 1]
            qb = q_ref[:, hs].astype(BF16)
            qnb = qn_ref[:, hs].astype(BF16)
            qh = (q_ref[:, hs] * SCALE).astype(BF16)
            qnh = (qn_ref[:, hs] * SCALE).astype(BF16)
            kc = kc_ref[:, hs].astype(BF16)
            kp = kp_ref[:, hs].astype(BF16)
            vc = vc_ref[:, hs].astype(BF16)
            vp = vp_ref[:, hs].astype(BF16)
            do_f = do_ref[:, hs]
            don_f = don_ref[:, hs]
            dob = do_f.astype(BF16)
            donb = don_f.astype(BF16)
            delta = jnp.sum(do_f * o_ref[:, hs], axis=1, keepdims=True)
            deltan = jnp.sum(don_f * on_ref[:, hs], axis=1, keepdims=True)
            lse = l_ref[:, HDIM * h:HDIM * h + 1]
            lsen = ln_ref[:, HDIM * h:HDIM * h + 1]
            s, ok = _dil_scores(qh, kc, sl, False)
            p_cc = jnp.where(ok, jnp.exp(jnp.where(ok, s, NEG_BIG) - lse), 0.0)
            ds_cc = p_cc * (_dot_nt(dob, vc) - delta)
            s, ok = _dil_scores(qh, kp, sl, True, has_prev)
            p_cp = jnp.where(ok, jnp.exp(jnp.where(ok, s, NEG_BIG) - lse), 0.0)
            ds_cp = p_cp * (_dot_nt(dob, vp) - delta)
            s, ok = _dil_scores(qnh, kc, sl, True, has_next)
            p_nc = jnp.where(ok, jnp.exp(jnp.where(ok, s, NEG_BIG) - lsen), 0.0)
            ds_nc = p_nc * (_dot_nt(donb, vc) - deltan)
            ds_cc_b = (ds_cc * SCALE).astype(BF16)
            ds_cp_b = (ds_cp * SCALE).astype(BF16)
            ds_nc_b = (ds_nc * SCALE).astype(BF16)
            dq_ref[:, hs] = _dot(ds_cc_b, kc) + _dot(ds_cp_b, kp)
            dk_ref[:, hs] = _dot_tn(ds_cc_b, qb) + _dot_tn(ds_nc_b, qnb)
            dv_ref[:, hs] = _dot_tn(p_cc.astype(BF16), dob) + _dot_tn(p_nc.astype(BF16), donb)

    cur = pl.BlockSpec((BLK, LANES), lambda c, i: (i, c))
    prv = pl.BlockSpec((BLK, LANES), lambda c, i: (jnp.maximum(i - 1, 0), c))
    nxt = pl.BlockSpec((BLK, LANES), lambda c, i: (jnp.minimum(i + 1, nb - 1), c))
    return pl.pallas_call(
        body, name=name, grid=(cw // LANES, nb),
        in_specs=[cur, nxt, cur, prv, cur, prv, cur, nxt, cur, nxt, cur, nxt,
                  pl.BlockSpec((1, LANES), lambda c, i: (0, c))],
        out_specs=[cur, cur, cur], out_shape=[jax.ShapeDtypeStruct((ln, cw), F32)] * 3,
        compiler_params=_cparams(("parallel", "parallel")),
    )(q, q, k, k, v, v, do, do, o, o, lse, lse, slope_cols)


def _dil_merge(os_, ls_, name):
    t, w = os_[0].shape
    tr = _rows(t)

    def body(o0, o1, o2, l0, l1, l2, y_ref, lse_ref):
        a, b, c = l0[...], l1[...], l2[...]
        m = jnp.maximum(jnp.maximum(a, b), c)
        ea, eb, ec = jnp.exp(a - m), jnp.exp(b - m), jnp.exp(c - m)
        den = ea + eb + ec
        y_ref[...] = (ea * o0[...] + eb * o1[...] + ec * o2[...]) / den
        lse_ref[...] = m + jnp.log(den)

    row = pl.BlockSpec((tr, w), lambda i: (i, 0))
    return pl.pallas_call(
        body, name=name, grid=(t // tr,), in_specs=[row] * 6, out_specs=[row, row],
        out_shape=[jax.ShapeDtypeStruct((t, w), F32)] * 2, compiler_params=_cparams(("parallel",)),
    )(*os_, *ls_)


def _gate_fwd(ys, gl, ws, name):
    t = gl.shape[0]
    d = gl.shape[1] // N_BRANCH
    tr = _rows(t)

    def body(ya, yb, yc, gl_ref, wa, wb, wc, m_ref):
        acc = None
        for i, (y, w) in enumerate(((ya, wa), (yb, wb), (yc, wc))):
            z = _dot(y[...].astype(BF16), w[...])
            term = jax.nn.sigmoid(gl_ref[:, i * d:(i + 1) * d]) * z
            acc = term if acc is None else acc + term
        m_ref[...] = acc.astype(m_ref.dtype)

    rows = [pl.BlockSpec((tr, y.shape[1]), lambda i: (i, 0)) for y in ys]
    wsp = [pl.BlockSpec(w.shape, lambda i: (0, 0)) for w in ws]
    return pl.pallas_call(
        body, name=name, grid=(t // tr,),
        in_specs=rows + [pl.BlockSpec((tr, N_BRANCH * d), lambda i: (i, 0))] + wsp,
        out_specs=pl.BlockSpec((tr, d), lambda i: (i, 0)), out_shape=jax.ShapeDtypeStruct((t, d), BF16),
        compiler_params=_cparams(("parallel",)),
    )(*ys, gl, *ws)


def _gate_bwd(dm, ys, gl, ws, name):
    t = gl.shape[0]
    d = gl.shape[1] // N_BRANCH
    tr = _rows(t)

    def body(dm_ref, ya, yb, yc, gl_ref, wa, wb, wc, dya, dyb, dyc, dgl_ref, dwa, dwb, dwc):
        step = pl.program_id(0)
        dmv = dm_ref[...].astype(F32)
        for i, (y, w, dy, dw) in enumerate(((ya, wa, dya, dwa), (yb, wb, dyb, dwb), (yc, wc, dyc, dwc))):
            yb16 = y[...].astype(BF16)
            z = _dot(yb16, w[...])
            sg = jax.nn.sigmoid(gl_ref[:, i * d:(i + 1) * d])
            dgl_ref[:, i * d:(i + 1) * d] = dmv * z * sg * (1.0 - sg)
            e = (dmv * sg).astype(BF16)
            dy[...] = _dot_nt(e, w[...])
            contrib = _dot_tn(yb16, e)

            @pl.when(step == 0)
            def _(dw=dw, contrib=contrib):
                dw[...] = contrib

            @pl.when(step > 0)
            def _(dw=dw, contrib=contrib):
                dw[...] += contrib

    rows = [pl.BlockSpec((tr, y.shape[1]), lambda i: (i, 0)) for y in ys]
    wsp = [pl.BlockSpec(w.shape, lambda i: (0, 0)) for w in ws]
    gsp = pl.BlockSpec((tr, N_BRANCH * d), lambda i: (i, 0))
    return pl.pallas_call(
        body, name=name, grid=(t // tr,),
        in_specs=[pl.BlockSpec((tr, d), lambda i: (i, 0))] + rows + [gsp] + wsp,
        out_specs=rows + [gsp] + wsp,
        out_shape=[jax.ShapeDtypeStruct(y.shape, F32) for y in ys] + [jax.ShapeDtypeStruct(gl.shape, F32)]
        + [jax.ShapeDtypeStruct(w.shape, F32) for w in ws],
        compiler_params=_cparams(("arbitrary",)),
    )(dm, *ys, gl, *ws)


def _adamw(w, m, v, gparts, name):
    r, c = w.shape
    n = gparts.shape[0]
    br = LANES if r % LANES == 0 else r
    c1 = 1.0 - ADAM_B1 ** ADAM_STEP
    c2 = 1.0 - ADAM_B2 ** ADAM_STEP

    def body(w_ref, m_ref, v_ref, g_ref, go_ref, d_ref, mo_ref, vo_ref):
        g = g_ref[0].astype(F32)
        for i in range(1, n):
            g = g + g_ref[i].astype(F32)
        mn = ADAM_B1 * m_ref[...] + (1.0 - ADAM_B1) * g
        vn = ADAM_B2 * v_ref[...] + (1.0 - ADAM_B2) * (g * g)
        go_ref[...] = g
        mo_ref[...] = mn
        vo_ref[...] = vn
        d_ref[...] = -ADAM_LR * ((mn / c1) / (jnp.sqrt(vn / c2) + ADAM_EPS) + ADAM_WD * w_ref[...])

    blk = pl.BlockSpec((br, c), lambda i: (i, 0))
    return pl.pallas_call(
        body, name=name, grid=(r // br,),
        in_specs=[blk, blk, blk, pl.BlockSpec((n, br, c), lambda i: (0, i, 0))],
        out_specs=[blk] * 4, out_shape=[jax.ShapeDtypeStruct((r, c), F32)] * 4,
        compiler_params=_cparams(("parallel",)),
    )(w, m, v, gparts)


def _my_coords():
    return lax.axis_index("x"), lax.axis_index("y"), lax.axis_index("c")


COMM_SEMS = [pltpu.SemaphoreType.DMA((7,)), pltpu.SemaphoreType.DMA((7,)), pltpu.SemaphoreType.DMA]


class _Gather:
    def __init__(self, x_ref, out_ref, send_sems, recv_sems, local_sem):
        self.x_ref, self.out_ref = x_ref, out_ref
        self.send_sems, self.recv_sems, self.local_sem = send_sems, recv_sems, local_sem
        self.m_per = x_ref.shape[0]
        x, y, c = _my_coords()
        self.c = c
        self.me, self.sibling = (x, y, c), (x, y, 1 - c)
        self.chips = [(1 - x, y), (x, 1 - y), (1 - x, 1 - y)]

    def rows(self, px, py, pc):
        return self.out_ref.at[pl.ds((4 * px + 2 * py + pc) * self.m_per, self.m_per), :]

    def copy(self, k, block, to, src=None):
        return pltpu.make_async_remote_copy(
            src_ref=self.rows(*block) if src is None else src, dst_ref=self.rows(*block),
            send_sem=self.send_sems.at[k], recv_sem=self.recv_sems.at[k], device_id=to, device_id_type=MESH)

    def mine(self):
        return pltpu.make_async_copy(self.x_ref, self.rows(*self.me), self.local_sem)

    def first(self):
        out = [self.copy(0, self.me, self.sibling, src=self.x_ref)]
        return out + [self.copy(1 + j, self.me, (*chip, self.c), src=self.x_ref) for j, chip in enumerate(self.chips)]

    def passed(self):
        return [self.copy(4 + j, (*chip, self.c), self.sibling) for j, chip in enumerate(self.chips)]

    def start(self):
        self.mine().start()
        for cp in self.first():
            cp.start()

    def forward(self):
        passed = self.passed()
        for j, chip in enumerate(self.chips):
            self.copy(1 + j, (*chip, self.c), self.me).wait_recv()
            passed[j].start()

    def finish(self):
        self.copy(0, self.sibling, self.me).wait_recv()
        for j, chip in enumerate(self.chips):
            self.copy(4 + j, (*chip, 1 - self.c), self.me).wait_recv()
        for cp in self.first() + self.passed():
            cp.wait_send()
        self.mine().wait()


class _Exchange:
    def __init__(self, send_ref, recv_ref, send_sems, recv_sems, local_sem):
        self.send_ref, self.recv_ref = send_ref, recv_ref
        self.send_sems, self.recv_sems, self.local_sem = send_sems, recv_sems, local_sem
        x, y, c = _my_coords()
        self.me = 4 * x + 2 * y + c
        self.peers = []
        for k in range(1, N_DEV):
            px = 1 - x if k & 4 else x
            py = 1 - y if k & 2 else y
            pc = 1 - c if k & 1 else c
            self.peers.append((4 * px + 2 * py + pc, (px, py, pc)))

    def mine(self):
        return pltpu.make_async_copy(self.send_ref.at[self.me], self.recv_ref.at[self.me], self.local_sem)

    def copy(self, k, src_slot, dst_slot):
        return pltpu.make_async_remote_copy(
            src_ref=self.send_ref.at[src_slot], dst_ref=self.recv_ref.at[dst_slot],
            send_sem=self.send_sems.at[k], recv_sem=self.recv_sems.at[k],
            device_id=self.peers[k][1], device_id_type=MESH)

    def start(self):
        self.mine().start()
        for k, (peer, _) in enumerate(self.peers):
            self.copy(k, peer, self.me).start()

    def finish(self):
        for k, (peer, _) in enumerate(self.peers):
            self.copy(k, peer, self.me).wait_send()
            self.copy(k, self.me, peer).wait_recv()
        self.mine().wait()


def _all_gather(x_shard, in_vmem, with_sum, name):
    m_per, n = x_shard.shape

    def body(x_ref, out_ref, *rest):
        if with_sum:
            sum_ref, send_sems, recv_sems, local_sem = rest
        else:
            send_sems, recv_sems, local_sem = rest
        g = _Gather(x_ref, out_ref, send_sems, recv_sems, local_sem)
        g.start()
        g.forward()
        g.finish()
        if with_sum:
            acc = out_ref[pl.ds(0, m_per), :]
            for d in range(1, N_DEV):
                acc = acc + out_ref[pl.ds(d * m_per, m_per), :]
            sum_ref[...] = acc

    space = pltpu.VMEM if in_vmem else pl.ANY
    out_shape = [jax.ShapeDtypeStruct((N_DEV * m_per, n), x_shard.dtype)]
    out_specs = [pl.BlockSpec(memory_space=space)]
    if with_sum:
        out_shape.append(jax.ShapeDtypeStruct((m_per, n), x_shard.dtype))
        out_specs.append(pl.BlockSpec(memory_space=pltpu.VMEM))
    res = pl.pallas_call(
        body, name=name, out_shape=out_shape, in_specs=[pl.BlockSpec(memory_space=space)], out_specs=out_specs,
        scratch_shapes=COMM_SEMS, compiler_params=pltpu.CompilerParams(vmem_limit_bytes=VMEM_LIMIT),
    )(x_shard)
    return res if with_sum else res[0]


def _comm_sems(n):
    return [pltpu.SemaphoreType.DMA((n, 7)), pltpu.SemaphoreType.DMA((n, 7)), pltpu.SemaphoreType.DMA((n,))]


class _Many:
    def __init__(self, kind, ins, outs, send_sems, recv_sems, local_sems):
        self.parts = [kind(i, o, send_sems.at[b], recv_sems.at[b], local_sems.at[b])
                      for b, (i, o) in enumerate(zip(ins, outs))]

    def start(self):
        for part in self.parts:
            part.start()

    def forward(self):
        for part in self.parts:
            part.forward()

    def finish(self):
        for part in self.parts:
            part.finish()


def _gathered_shapes(shards):
    return [jax.ShapeDtypeStruct((N_DEV * s.shape[0],) + s.shape[1:], s.dtype) for s in shards]


def _all_gather_many(shards, name):
    n = len(shards)

    def body(*refs):
        g = _Many(_Gather, refs[:n], refs[n:2 * n], *refs[2 * n:])
        g.start()
        g.forward()
        g.finish()

    hbm = pl.BlockSpec(memory_space=pl.ANY)
    return pl.pallas_call(body, name=name, out_shape=_gathered_shapes(shards), in_specs=[hbm] * n,
                          out_specs=[hbm] * n, scratch_shapes=_comm_sems(n))(*shards)


def _all_to_all_many(sends, name):
    n = len(sends)

    def body(*refs):
        ex = _Many(_Exchange, refs[:n], refs[n:2 * n], *refs[2 * n:])
        ex.start()
        ex.finish()

    hbm = pl.BlockSpec(memory_space=pl.ANY)
    return pl.pallas_call(body, name=name, out_shape=[jax.ShapeDtypeStruct(s.shape, s.dtype) for s in sends],
                          in_specs=[hbm] * n, out_specs=[hbm] * n, scratch_shapes=_comm_sems(n))(*sends)


def _row(v):
    return v.reshape(1, -1)


def _ffn_fwd(x, w_in, w_out, g_pre, g_post, m, res_w, tag):
    shift, scale, gate = m[0], m[1], m[2]
    mpre = _row(g_pre * (1.0 + scale))
    mpost = _row(res_w * gate * g_post)
    h = _rms_fwd(x, mpre, _row(shift), None, BF16, tag + "_pre")
    u = _matmul(h, w_in, out_dtype=BF16, name=tag + "_in")
    s = _swiglu_fwd(u, tag + "_act")
    y = _matmul(s, w_out, name=tag + "_out")
    x_new = _rms_fwd(y, mpost, jnp.zeros_like(mpost), x, F32, tag + "_post")
    return x_new, (x, h, u, s, y, mpre, mpost)


def _sub_bwd_post(dx_new, y, mpost, g_post, gate, res_w, tag):
    dy, c1, _ = _rms_bwd(dx_new, y, mpost, None, BF16, tag + "_post_bwd")
    c1 = c1[0]
    return dy, c1 * res_w * g_post, c1 * res_w * gate


def _sub_bwd_pre(dh, x, mpre, dx_new, g_pre, scale, tag):
    dx, c2, c3 = _rms_bwd(dh, x, mpre, dx_new, F32, tag + "_pre_bwd")
    c2, c3 = c2[0], c3[0]
    return dx, c3, c2 * g_pre, c2 * (1.0 + scale)


def _ffn_bwd(dx_new, saved, w_in, w_out, g_pre, g_post, m, res_w, tag):
    x, h, u, s, y, mpre, mpost = saved
    scale, gate = m[1], m[2]
    dy, dgate, dg_post = _sub_bwd_post(dx_new, y, mpost, g_post, gate, res_w, tag)
    ds = _matmul(dy, w_out, tb=True, out_dtype=BF16, name=tag + "_out_dx")
    dw_out = _matmul(s, dy, ta=True, out_dtype=BF16, name=tag + "_out_dw")
    du = _swiglu_bwd(u, ds, tag + "_act_bwd")
    dh = _matmul(du, w_in, tb=True, name=tag + "_in_dx")
    dw_in = _matmul(h, du, ta=True, out_dtype=BF16, name=tag + "_in_dw")
    dx, dshift, dscale, dg_pre = _sub_bwd_pre(dh, x, mpre, dx_new, g_pre, scale, tag)
    return dx, dw_in, dw_out, jnp.stack([dshift, dscale, dgate]), dg_pre, dg_post


def _slope_cols(gi):
    _, r = C_GROUPS[gi]
    sl = jnp.asarray(_alibi_slopes(C_HEADS)[gi * C_HPG:(gi + 1) * C_HPG], F32) * float(r)
    return jnp.tile(jnp.repeat(sl, HDIM), r).reshape(1, r * C_OUT)


def _group_view(a, gi):
    _, r = C_GROUPS[gi]
    t = a.shape[0]
    return a.reshape(t // r, r * a.shape[1])


def _mix_fwd(x, w, g_pre, g_post, m, lb, hn, tag, gather=None):
    t, d = x.shape
    shift, scale, gate = m[0], m[1], m[2]
    mpre = _row(g_pre * (1.0 + scale))
    mpost = _row(gate * g_post)
    h = _rms_fwd(x, mpre, _row(shift), None, BF16, tag + "_pre")
    p = _matmul(h, w["w_in"], name=tag + "_in")
    hn2 = _row(jnp.tile(hn, 2))
    ya, oa, states = _hgrn_fwd(p, _row(lb), hn2, tag + "_hgrn")
    kv = p[:, COL_BK:COL_CQ].astype(BF16)
    if gather is None:
        (yb, sb_tot), gathered = _sb_fwd(p, kv, tag + "_sb"), None
    else:
        res = _sb_fwd(p, kv, tag + "_sb_gather", gather)
        yb, sb_tot, gathered = res[0], res[1], list(res[2:])
    og, lg, qkv = [], [], []
    for gi in range(len(C_GROUPS)):
        cs = slice(gi * C_OUT, (gi + 1) * C_OUT)
        q = _group_view(p[:, COL_CQ:COL_CK][:, cs], gi)
        k = _group_view(p[:, COL_CK:COL_CV][:, cs], gi)
        v = _group_view(p[:, COL_CV:COL_GATE][:, cs], gi)
        o, lse = _dil_fwd(q, k, v, _slope_cols(gi), tag + "_dil%d" % gi)
        og.append(o.reshape(t, C_OUT))
        lg.append(lse.reshape(t, C_OUT))
        qkv.append((q, k, v))
    yc, lse_c = _dil_merge(og, lg, tag + "_dil_merge")
    gl = p[:, COL_GATE:]
    ws = (w["w_branch_a"], w["w_branch_b"], w["w_branch_c"])
    merged = _gate_fwd((ya, yb, yc), gl, ws, tag + "_gate")
    y = _matmul(merged, w["w_out"], name=tag + "_out")
    x_new = _rms_fwd(y, mpost, jnp.zeros_like(mpost), x, F32, tag + "_post")
    return x_new, (x, h, p, hn2, ya, oa, states, yb, kv, sb_tot, qkv, yc, lse_c, gl, merged, y, mpre, mpost), gathered


def _mix_bwd(dx_new, saved, w, g_pre, g_post, m, lb, tag, exchange=None):
    x, h, p, hn2, ya, oa, states, yb, kv, sb_tot, qkv, yc, lse_c, gl, merged, y, mpre, mpost = saved
    t = x.shape[0]
    scale, gate = m[1], m[2]
    dy, dgate, dg_post = _sub_bwd_post(dx_new, y, mpost, g_post, gate, 1.0, tag)
    dmerged = _matmul(dy, w["w_out"], tb=True, out_dtype=BF16, name=tag + "_out_dx")
    dw_out = _matmul(merged, dy, ta=True, out_dtype=BF16, name=tag + "_out_dw")
    ws = (w["w_branch_a"], w["w_branch_b"], w["w_branch_c"])
    dya, dyb, dyc, dgl, dwa, dwb, dwc = _gate_bwd(dmerged, (ya, yb, yc), gl, ws, tag + "_gate_bwd")
    dqa, dfa, dia, dga, dlb, dhn = _hgrn_bwd(p, _row(lb), hn2, oa, states, dya, tag + "_hgrn_bwd")
    if exchange is None:
        (dbq, dbk, dbv), received = _sb_bwd(p, kv, sb_tot, dyb, tag + "_sb_bwd"), None
    else:
        res = _sb_bwd(p, kv, sb_tot, dyb, tag + "_sb_bwd_exchange", exchange)
        dbq, dbk, dbv, received = res[0], res[1], res[2], list(res[3:])
    dcq, dck, dcv = [], [], []
    for gi in range(len(C_GROUPS)):
        q, k, v = qkv[gi]
        dq, dk, dv = _dil_bwd(q, k, v, _group_view(dyc, gi), _group_view(yc, gi), _group_view(lse_c, gi),
                              _slope_cols(gi), tag + "_dil%d_bwd" % gi)
        dcq.append(dq.reshape(t, C_OUT))
        dck.append(dk.reshape(t, C_OUT))
        dcv.append(dv.reshape(t, C_OUT))
    dp = jnp.concatenate([dqa, dfa, dia, dga, dbq, dbk, dbv] + dcq + dck + dcv + [dgl], axis=1).astype(BF16)
    dh = _matmul(dp, w["w_in"], tb=True, name=tag + "_in_dx")
    dw_in = _matmul(h, dp, ta=True, out_dtype=BF16, name=tag + "_in_dw")
    dx, dshift, dscale, dg_pre = _sub_bwd_pre(dh, x, mpre, dx_new, g_pre, scale, tag)
    dhn_v = jnp.sum(dhn, axis=(0, 1))
    dhn_v = dhn_v[:A_VDIM] + dhn_v[A_VDIM:]
    dws = dict(w_in=dw_in, w_out=dw_out, w_branch_a=dwa.astype(BF16), w_branch_b=dwb.astype(BF16),
               w_branch_c=dwc.astype(BF16))
    return dx, dws, jnp.stack([dshift, dscale, dgate]), dg_pre, dg_post, dlb[0], dhn_v, received


class _LocalWeights:
    def __init__(self, wts):
        self.wts = wts

    def first(self):
        return None

    def shard(self, l):
        return None

    def layer(self, l, gathered):
        return {k: v[l] for k, v in self.wts.items()}

    def pack(self, l, dws):
        return dws

    def last(self, packed):
        return packed


class _ShardedWeights:
    def __init__(self, shards):
        self.shards = shards

    def shard(self, l):
        return [self.shards[k][l].astype(BF16) for k in BIG_WEIGHTS]

    def first(self):
        return _all_gather_many(self.shard(0), "weights_all_gather")

    def layer(self, l, gathered):
        out = {}
        for k, got in zip(BIG_WEIGHTS, gathered):
            _, r, c = self.shards[k].shape
            out[k] = got if k in ROW_SHARDED else got.reshape(N_DEV, r, c).transpose(1, 0, 2).reshape(r, N_DEV * c)
        return out

    def pack(self, l, dws):
        out = []
        for k in BIG_WEIGHTS:
            _, r, c = self.shards[k].shape
            g = dws[k]
            out.append(g.reshape(N_DEV, r, c) if k in ROW_SHARDED else g.reshape(r, N_DEV, c).transpose(1, 0, 2))
        return out

    def last(self, packed):
        return _all_to_all_many(packed, "grads_all_to_all")

    def partial_sums(self, received):
        return {k: jnp.concatenate([rec[b] for rec in received], axis=1) for b, k in enumerate(BIG_WEIGHTS)}


def _local_step(x, target, mod, norm_g, lb_all, hnorm, supply):
    depth = mod.shape[0]
    d = x.shape[1]
    saved, wls = [], []
    gathered = supply.first()
    for l in range(depth):
        wl = supply.layer(l, gathered)
        wls.append(wl)
        x, s0 = _ffn_fwd(x, wl["ffn1_w_in"], wl["ffn1_w_out"], norm_g[l, 0], norm_g[l, 1], mod[l, 0], 0.5, "ffn1")
        nxt = supply.shard(l + 1) if l + 1 < depth else None
        x, s1, gathered = _mix_fwd(x, wl, norm_g[l, 2], norm_g[l, 3], mod[l, 1], lb_all[l], hnorm[l], "mix", nxt)
        x, s2 = _ffn_fwd(x, wl["ffn2_w_in"], wl["ffn2_w_out"], norm_g[l, 4], norm_g[l, 5], mod[l, 2], 0.5, "ffn2")
        saved.append((s0, s1, s2))
    dx, sq = _loss_head(x, target, "loss_head")
    loss = 0.5 * jnp.sum(sq) / d
    dmod, dng, dlb, dhn = [], [], [], []
    returned = [None] * depth
    pending = None
    for l in reversed(range(depth)):
        wl = wls[l]
        s0, s1, s2 = saved[l]
        dx, dwi2, dwo2, dm2, dgp2, dgq2 = _ffn_bwd(dx, s2, wl["ffn2_w_in"], wl["ffn2_w_out"], norm_g[l, 4],
                                                   norm_g[l, 5], mod[l, 2], 0.5, "ffn2")
        fuse = pending is not None and isinstance(supply, _ShardedWeights)
        dx, dwm, dm1, dgp1, dgq1, dlb_l, dhn_l, received = _mix_bwd(
            dx, s1, wl, norm_g[l, 2], norm_g[l, 3], mod[l, 1], lb_all[l], "mix", pending if fuse else None)
        if pending is not None:
            returned[l + 1] = received if fuse else pending
        dx, dwi1, dwo1, dm0, dgp0, dgq0 = _ffn_bwd(dx, s0, wl["ffn1_w_in"], wl["ffn1_w_out"], norm_g[l, 0],
                                                   norm_g[l, 1], mod[l, 0], 0.5, "ffn1")
        dmod.append(jnp.stack([dm0, dm1, dm2]))
        dng.append(jnp.stack([dgp0, dgq0, dgp1, dgq1, dgp2, dgq2]))
        dlb.append(dlb_l)
        dhn.append(dhn_l)
        pending = supply.pack(l, dict(dwm, ffn1_w_in=dwi1, ffn1_w_out=dwo1, ffn2_w_in=dwi2, ffn2_w_out=dwo2))
    returned[0] = supply.last(pending)
    rev = lambda lst: jnp.stack(lst[::-1])
    return loss, dx, rev(dmod), rev(dng), rev(dlb), rev(dhn), returned


def _lb_all(logits):
    lb_p = jax.nn.softmax(logits.astype(F32), axis=0)
    return jnp.cumsum(lb_p, axis=0) - lb_p[0:1]


def _pad_rows(a, rows):
    return jnp.pad(a, ((0, rows - a.shape[0]), (0, 0)))


def kernel(x, c, w_ada, b_ada, norm_g, ffn1_w_in, ffn1_w_out, w_in, hgrn_lb_logits, hgrn_norm_g, w_branch_a, w_branch_b, w_branch_c, w_out, ffn2_w_in, ffn2_w_out, loss_target, m_w_ada, m_b_ada, m_norm_g, m_ffn1_w_in, m_ffn1_w_out, m_w_in, m_hgrn_lb_logits, m_hgrn_norm_g, m_w_branch_a, m_w_branch_b, m_w_branch_c, m_w_out, m_ffn2_w_in, m_ffn2_w_out, v_w_ada, v_b_ada, v_norm_g, v_ffn1_w_in, v_ffn1_w_out, v_w_in, v_hgrn_lb_logits, v_hgrn_norm_g, v_w_branch_a, v_w_branch_b, v_w_branch_c, v_w_out, v_ffn2_w_in, v_ffn2_w_out):
    weights = dict(w_ada=w_ada, b_ada=b_ada, norm_g=norm_g, ffn1_w_in=ffn1_w_in, ffn1_w_out=ffn1_w_out, w_in=w_in,
                   hgrn_lb_logits=hgrn_lb_logits, hgrn_norm_g=hgrn_norm_g, w_branch_a=w_branch_a,
                   w_branch_b=w_branch_b, w_branch_c=w_branch_c, w_out=w_out, ffn2_w_in=ffn2_w_in,
                   ffn2_w_out=ffn2_w_out)
    mom1 = dict(w_ada=m_w_ada, b_ada=m_b_ada, norm_g=m_norm_g, ffn1_w_in=m_ffn1_w_in, ffn1_w_out=m_ffn1_w_out,
                w_in=m_w_in, hgrn_lb_logits=m_hgrn_lb_logits, hgrn_norm_g=m_hgrn_norm_g, w_branch_a=m_w_branch_a,
                w_branch_b=m_w_branch_b, w_branch_c=m_w_branch_c, w_out=m_w_out, ffn2_w_in=m_ffn2_w_in,
                ffn2_w_out=m_ffn2_w_out)
    mom2 = dict(w_ada=v_w_ada, b_ada=v_b_ada, norm_g=v_norm_g, ffn1_w_in=v_ffn1_w_in, ffn1_w_out=v_ffn1_w_out,
                w_in=v_w_in, hgrn_lb_logits=v_hgrn_lb_logits, hgrn_norm_g=v_hgrn_norm_g, w_branch_a=v_w_branch_a,
                w_branch_b=v_w_branch_b, w_branch_c=v_w_branch_c, w_out=v_w_out, ffn2_w_in=v_ffn2_w_in,
                ffn2_w_out=v_ffn2_w_out)
    order = list(weights)
    depth, d, ada_cols = w_ada.shape
    nd = d // LANES
    xi, yi, ci = _my_coords()
    me = 4 * xi + 2 * yi + ci

    small = jnp.concatenate([c.reshape(nd, LANES), norm_g.reshape(depth * 6, LANES)], axis=0)
    g1 = _all_gather(small, True, False, "small_all_gather").reshape(N_DEV, small.shape[0], LANES)
    c_act = _silu(g1[:, :nd].reshape(N_DEV, d))
    norm_full = g1[:, nd:].reshape(N_DEV, depth, 6, LANES).transpose(1, 2, 0, 3).reshape(depth, 6, d)

    c_pad = _pad_rows(c_act, 16)
    mod_sh = jnp.stack([_matmul(c_pad, w_ada[l], name="ada_mod")[:N_DEV]
                        + lax.dynamic_slice_in_dim(b_ada[l], me * ada_cols, ada_cols)[None]
                        for l in range(depth)])
    g2 = _all_gather(mod_sh.reshape(-1, LANES), True, False, "mod_all_gather")
    g2 = g2.reshape(N_DEV, depth, N_DEV, ada_cols)
    mod = lax.dynamic_index_in_dim(g2, me, axis=2, keepdims=False)
    mod = mod.transpose(1, 0, 2).reshape(depth, 3, 3, d)

    supply = _ShardedWeights({k: weights[k] for k in BIG_WEIGHTS})
    lb_all, lb_vjp = jax.vjp(_lb_all, hgrn_lb_logits)

    loss, dx, dmod, dng, dlb, dhn, received = _local_step(x[0], loss_target[0], mod, norm_full, lb_all,
                                                          hgrn_norm_g, supply)
    loss = lax.psum(loss, ("x", "y", "c"))

    dhn_pad = jnp.pad(dhn.reshape(-1), (0, 8 * LANES - dhn.size))
    pieces = [dmod.reshape(-1), dng.reshape(-1), dlb.reshape(-1), dhn_pad]
    sizes = [p_.size for p_ in pieces]
    smallg = jnp.concatenate(pieces).reshape(-1, LANES)
    g3, gsum = _all_gather(smallg, True, True, "small_grads_all_gather")
    g3 = g3.reshape(N_DEV, -1)
    gsum = gsum.reshape(-1)
    dmod_all = g3[:, :sizes[0]].reshape(N_DEV, depth, 9 * d)
    o1 = sizes[0]
    grads = {}
    grads["b_ada"] = gsum[:o1].reshape(depth, 9 * d)
    dng_sum = gsum[o1:o1 + sizes[1]].reshape(depth, 6, nd, LANES)
    grads["norm_g"] = lax.dynamic_index_in_dim(dng_sum, me, axis=2, keepdims=False)
    o2 = o1 + sizes[1]
    dlb_sum = gsum[o2:o2 + sizes[2]].reshape(depth, A_QK)
    grads["hgrn_lb_logits"] = lb_vjp(dlb_sum)[0]
    o3 = o2 + sizes[2]
    grads["hgrn_norm_g"] = gsum[o3:o3 + dhn.size].reshape(depth, A_VDIM)
    dmod_mine = lax.dynamic_slice_in_dim(dmod_all, me * ada_cols, ada_cols, axis=2)
    grads["w_ada"] = jnp.stack([_matmul(c_pad, _pad_rows(dmod_mine[:, l], 16), ta=True, name="ada_dw")
                                for l in range(depth)])

    gparts = supply.partial_sums(received)

    outs = {}
    for k in order:
        w = weights[k]
        w2 = w.reshape(-1, w.shape[-1])
        gp = gparts[k] if k in gparts else grads[k].reshape((1,) + w2.shape)
        res = _adamw(w2, mom1[k].reshape(w2.shape), mom2[k].reshape(w2.shape), gp, "adamw")
        outs[k] = [r.reshape(w.shape) for r in res]
    return (loss, dx[None], *[outs[k][0] for k in order], *[outs[k][1] for k in order],
            *[outs[k][2] for k in order], *[outs[k][3] for k in order])
```

```python
import functools
import math

import jax
import jax.numpy as jnp
from jax import lax
from jax.experimental import pallas as pl
from jax.experimental.pallas import tpu as pltpu

F32 = jnp.float32
BF16 = jnp.bfloat16

A_HEADS, A_KDIM, A_VDIM, A_CHUNK = 6, 128, 64, 64
B_HEADS, HDIM = 6, 64
C_GROUPS = ((128, 1), (512, 4), (2048, 16))
C_HPG = 4
C_HEADS = C_HPG * len(C_GROUPS)
N_BRANCH = 3
EPS = 1e-6
NEG_BIG = -1e30
TINY = 1e-30
A_QK = A_HEADS * A_KDIM
A_V = A_HEADS * A_VDIM
B_W = B_HEADS * HDIM
C_W = C_HEADS * HDIM
C_OUT = C_HPG * HDIM
COL_AQ, COL_AF, COL_AI, COL_AG = 0, A_QK, 2 * A_QK, 2 * A_QK + A_V
COL_BQ = 2 * A_QK + 2 * A_V
COL_BK, COL_BV = COL_BQ + B_W, COL_BQ + 2 * B_W
COL_CQ = COL_BQ + 3 * B_W
COL_CK, COL_CV = COL_CQ + C_W, COL_CQ + 2 * C_W
COL_GATE = COL_CQ + 3 * C_W

ADAM_LR, ADAM_B1, ADAM_B2, ADAM_EPS, ADAM_WD, ADAM_STEP = 0.001, 0.9, 0.999, 1e-08, 0.01, 10

N_DEV = 8
LANES = 128
VMEM_LIMIT = 48 * 1024 * 1024
MATMUL_VMEM_BUDGET = 28 * 1024 * 1024
SUB = 16
EXP_CLAMP = 80.0
MESH = pl.DeviceIdType.MESH

BIG_WEIGHTS = ("ffn1_w_in", "ffn1_w_out", "w_in", "w_branch_a", "w_branch_b", "w_branch_c", "w_out",
               "ffn2_w_in", "ffn2_w_out")
ROW_SHARDED = ("ffn1_w_out", "w_out", "ffn2_w_out")


def _cparams(sem):
    return pltpu.CompilerParams(dimension_semantics=sem, vmem_limit_bytes=VMEM_LIMIT)


def _tile(n, cap):
    best, t = None, LANES
    while t <= min(n, cap):
        if n % t == 0:
            best = t
        t += LANES
    return best or n


def _rows(t, cap=256):
    r = cap
    while t % r:
        r //= 2
    return r


def _divisors(n):
    return [t for t in range(LANES, n + 1, LANES) if n % t == 0] or [n]


def _matmul_tiles(m, n, k, a_size, b_size, o_size):
    best, best_key = None, None
    for tm in _divisors(m):
        for tn in _divisors(n):
            for tk in _divisors(k):
                if tm > 1024 or tn > 3072 or tk > 4096:
                    continue
                cast = (tm * tk * 2 if a_size > 2 else 0) + (tk * tn * 2 if b_size > 2 else 0)
                need = 2 * (tm * tk * a_size + tk * tn * b_size + tm * tn * o_size) + 2 * tm * tn * 4 + cast
                if need > MATMUL_VMEM_BUDGET:
                    continue
                key = (tm * tn * tk, tk)
                if best_key is None or key > best_key:
                    best, best_key = (tm, tn, tk), key
    return best


def _dot(a, b):
    return jnp.dot(a, b, preferred_element_type=F32)


def _dot_nt(a, b):
    return lax.dot_general(a, b, (((1,), (1,)), ((), ())), preferred_element_type=F32)


def _dot_tn(a, b):
    return lax.dot_general(a, b, (((0,), (0,)), ((), ())), preferred_element_type=F32)


def _split3(x):
    h = x.astype(BF16)
    r = x - h.astype(F32)
    m = r.astype(BF16)
    lo = (r - m.astype(F32)).astype(BF16)
    return h, m, lo


def _ones_left(mat01, x):
    h, m, lo = _split3(x)
    return _dot(mat01, h) + _dot(mat01, m) + _dot(mat01, lo)


def _silu(x):
    return x * jax.nn.sigmoid(x)


def _dsilu(x):
    s = jax.nn.sigmoid(x)
    return s * (1.0 + x * (1.0 - s))


def _matmul(a, b, *, ta=False, tb=False, out_dtype=F32, name):
    if ta:
        kdim, m = a.shape
    else:
        m, kdim = a.shape
    n = b.shape[0] if tb else b.shape[1]
    tm, tn, tk = _matmul_tiles(m, n, kdim, a.dtype.itemsize, b.dtype.itemsize, jnp.dtype(out_dtype).itemsize)
    nk = kdim // tk
    ni, nj = m // tm, n // tn
    a_bytes, b_bytes = m * kdim * a.dtype.itemsize, kdim * n * b.dtype.itemsize
    j_outer = nk == 1 and (b_bytes + a_bytes * nj) < (a_bytes + b_bytes * ni)
    dims = (((0 if ta else 1,), (1 if tb else 0,)), ((), ()))

    def body(a_ref, b_ref, o_ref, *scratch):
        p = lax.dot_general(a_ref[...].astype(BF16), b_ref[...].astype(BF16), dims, preferred_element_type=F32)
        if nk == 1:
            o_ref[...] = p.astype(o_ref.dtype)
            return
        acc = scratch[0]
        k = pl.program_id(2)

        @pl.when(k == 0)
        def _():
            acc[...] = p

        @pl.when(k > 0)
        def _():
            acc[...] += p

        @pl.when(k == nk - 1)
        def _():
            o_ref[...] = acc[...].astype(o_ref.dtype)

    def spec(shape, pick):
        if j_outer:
            return pl.BlockSpec(shape, lambda j, i, k: pick(i, j, k))
        return pl.BlockSpec(shape, lambda i, j, k: pick(i, j, k))

    a_spec = spec((tk, tm), lambda i, j, k: (k, i)) if ta else spec((tm, tk), lambda i, j, k: (i, k))
    b_spec = spec((tn, tk), lambda i, j, k: (j, k)) if tb else spec((tk, tn), lambda i, j, k: (k, j))
    return pl.pallas_call(
        body, name=name, grid=(nj, ni, nk) if j_outer else (ni, nj, nk), in_specs=[a_spec, b_spec],
        out_specs=spec((tm, tn), lambda i, j, k: (i, j)),
        out_shape=jax.ShapeDtypeStruct((m, n), out_dtype),
        scratch_shapes=[pltpu.VMEM((tm, tn), F32)] if nk > 1 else [],
        compiler_params=_cparams(("parallel", "parallel", "arbitrary")),
    )(a, b)


def _rms_fwd(z, mcol, acol, res, out_dtype, name):
    t, d = z.shape
    tr = _rows(t)
    has_res = res is not None

    def body(*refs):
        if has_res:
            z_ref, m_ref, a_ref, r_ref, o_ref = refs
        else:
            z_ref, m_ref, a_ref, o_ref = refs
        zf = z_ref[...]
        r = lax.rsqrt(jnp.mean(zf * zf, axis=-1, keepdims=True) + EPS)
        y = zf * r * m_ref[...] + a_ref[...]
        if has_res:
            y = r_ref[...] + y
        o_ref[...] = y.astype(o_ref.dtype)

    row = pl.BlockSpec((tr, d), lambda i: (i, 0))
    col = pl.BlockSpec((1, d), lambda i: (0, 0))
    ins = [z, mcol, acol] + ([res] if has_res else [])
    return pl.pallas_call(
        body, name=name, grid=(t // tr,), in_specs=[row, col, col] + ([row] if has_res else []),
        out_specs=row, out_shape=jax.ShapeDtypeStruct((t, d), out_dtype),
        compiler_params=_cparams(("parallel",)),
    )(*ins)


def _rms_bwd(d_out, z, mcol, dres, out_dtype, name):
    t, d = z.shape
    tr = _rows(t)
    has_res = dres is not None

    def body(*refs):
        if has_res:
            d_ref, z_ref, m_ref, r_ref, o_ref, s1_ref, s2_ref = refs
        else:
            d_ref, z_ref, m_ref, o_ref, s1_ref, s2_ref = refs
        i = pl.program_id(0)
        zf = z_ref[...]
        r = lax.rsqrt(jnp.mean(zf * zf, axis=-1, keepdims=True) + EPS)
        zh = zf * r
        df = d_ref[...].astype(F32)
        dzh = df * m_ref[...]
        dz = r * (dzh - zh * jnp.mean(dzh * zh, axis=-1, keepdims=True))
        if has_res:
            dz = dz + r_ref[...]
        o_ref[...] = dz.astype(o_ref.dtype)
        s1 = jnp.sum(df * zh, axis=0, keepdims=True)
        s2 = jnp.sum(df, axis=0, keepdims=True)

        @pl.when(i == 0)
        def _():
            s1_ref[...] = s1
            s2_ref[...] = s2

        @pl.when(i > 0)
        def _():
            s1_ref[...] += s1
            s2_ref[...] += s2

    row = pl.BlockSpec((tr, d), lambda i: (i, 0))
    col = pl.BlockSpec((1, d), lambda i: (0, 0))
    ins = [d_out, z, mcol] + ([dres] if has_res else [])
    return pl.pallas_call(
        body, name=name, grid=(t // tr,), in_specs=[row, row, col] + ([row] if has_res else []),
        out_specs=[row, col, col],
        out_shape=[jax.ShapeDtypeStruct((t, d), out_dtype), jax.ShapeDtypeStruct((1, d), F32),
                   jax.ShapeDtypeStruct((1, d), F32)],
        compiler_params=_cparams(("arbitrary",)),
    )(*ins)


def _swiglu_fwd(u, name):
    t, f2 = u.shape
    f = f2 // 2
    tr = _rows(t)

    def body(u_ref, s_ref):
        a = u_ref[:, :f].astype(F32)
        b = u_ref[:, f:].astype(F32)
        s_ref[...] = (_silu(a) * b).astype(s_ref.dtype)

    return pl.pallas_call(
        body, name=name, grid=(t // tr,), in_specs=[pl.BlockSpec((tr, f2), lambda i: (i, 0))],
        out_specs=pl.BlockSpec((tr, f), lambda i: (i, 0)), out_shape=jax.ShapeDtypeStruct((t, f), BF16),
        compiler_params=_cparams(("parallel",)),
    )(u)


def _swiglu_bwd(u, ds, name):
    t, f2 = u.shape
    f = f2 // 2
    tr = _rows(t)

    def body(u_ref, ds_ref, du_ref):
        a = u_ref[:, :f].astype(F32)
        b = u_ref[:, f:].astype(F32)
        g = ds_ref[...].astype(F32)
        du_ref[:, :f] = (g * b * _dsilu(a)).astype(du_ref.dtype)
        du_ref[:, f:] = (g * _silu(a)).astype(du_ref.dtype)

    return pl.pallas_call(
        body, name=name, grid=(t // tr,),
        in_specs=[pl.BlockSpec((tr, f2), lambda i: (i, 0)), pl.BlockSpec((tr, f), lambda i: (i, 0))],
        out_specs=pl.BlockSpec((tr, f2), lambda i: (i, 0)), out_shape=jax.ShapeDtypeStruct((t, f2), BF16),
        compiler_params=_cparams(("parallel",)),
    )(u, ds)


def _loss_head(y, target, name):
    t, d = y.shape
    tr = _rows(t)

    def body(y_ref, t_ref, dy_ref, sq_ref):
        i = pl.program_id(0)
        e = y_ref[...] - t_ref[...]
        dy_ref[...] = e * (1.0 / d)
        s = jnp.sum(e * e, axis=0, keepdims=True)

        @pl.when(i == 0)
        def _():
            sq_ref[...] = s

        @pl.when(i > 0)
        def _():
            sq_ref[...] += s

    row = pl.BlockSpec((tr, d), lambda i: (i, 0))
    col = pl.BlockSpec((1, d), lambda i: (0, 0))
    return pl.pallas_call(
        body, name=name, grid=(t // tr,), in_specs=[row, row], out_specs=[row, col],
        out_shape=[jax.ShapeDtypeStruct((t, d), F32), jax.ShapeDtypeStruct((1, d), F32)],
        compiler_params=_cparams(("arbitrary",)),
    )(y, target)


def _hgrn_consts():
    c = A_CHUNK
    shift = SUB.bit_length() - 1
    r = lax.broadcasted_iota(jnp.int32, (c, c), 0)
    s = lax.broadcasted_iota(jnp.int32, (c, c), 1)
    sub_r = lax.shift_right_logical(r, shift)
    incl = s <= r
    masks = [jnp.logical_and(sub_r == i, incl) for i in range(c // SUB)]
    rev_incl = jnp.where(s >= r, 1.0, 0.0).astype(BF16)
    r2 = lax.broadcasted_iota(jnp.int32, (2 * c + 8, c), 0)
    s2 = lax.broadcasted_iota(jnp.int32, (2 * c + 8, c), 1)
    sub_start = lax.shift_left(lax.shift_right_logical(r2 - c, shift), shift)
    running = jnp.where(s2 <= r2, 1.0, 0.0)
    before = jnp.where(s2 < sub_start, 1.0, 0.0)
    stack = jnp.where(r2 < c, running, jnp.where(r2 < 2 * c, before, 1.0)).astype(BF16)
    return stack, masks, incl, rev_incl


def _hgrn_chunk(q_raw, f_raw, lbv, stack):
    c = A_CHUNK
    sg = jax.nn.sigmoid(f_raw)
    sgn = jax.nn.sigmoid(-f_raw)
    f = lbv + (1.0 - lbv) * sg
    logf = jnp.log(jnp.maximum(f, TINY))
    k = (1.0 - lbv) * sgn
    q = _silu(q_raw)
    bb = _ones_left(stack, logf)
    b = bb[:c]
    bsrow = bb[c:2 * c]
    b_end = bb[2 * c:2 * c + 1]
    e_sub = jnp.exp(b - bsrow)
    e_b = jnp.exp(b)
    e_end = jnp.exp(b_end - b)
    qs = q * e_sub
    q_in = q * e_b
    kend = k * e_end
    kfac = [jnp.exp(jnp.minimum(bsrow[i * SUB:i * SUB + 1] - b, EXP_CLAMP)) for i in range(c // SUB)]
    return dict(sg=sg, sgn=sgn, f=f, k=k, q=q, b=b, b_end=b_end, e_sub=e_sub, e_b=e_b, e_end=e_end,
                qs=qs, q_in=q_in, kend=kend, kfac=kfac)


def _hgrn_scores(ch, masks):
    qs_b = ch["qs"].astype(BF16)
    a = None
    for i, mk in enumerate(masks):
        ki = (ch["k"] * ch["kfac"][i]).astype(BF16)
        part = jnp.where(mk, _dot_nt(qs_b, ki), 0.0)
        a = part if a is None else a + part
    return a


def _hgrn_fwd(p, lb, hn2, name):
    t = p.shape[0]
    tb = _rows(t)
    nt = t // tb
    nc = tb // A_CHUNK
    c = A_CHUNK

    def body(q_ref, f_ref, i_ref, g_ref, lb_ref, hn_ref, y_ref, o_ref, st_ref, s_scr):
        j = pl.program_id(1)

        @pl.when(j == 0)
        def _():
            s_scr[...] = jnp.zeros_like(s_scr)

        stack, masks, _, _ = _hgrn_consts()
        states = [s_scr[0], s_scr[1]]
        for ci in range(nc):
            rows = pl.ds(ci * c, c)
            for hh in range(2):
                lsl = slice(A_KDIM * hh, A_KDIM * (hh + 1))
                hsl = slice(A_VDIM * hh, A_VDIM * (hh + 1))
                ch = _hgrn_chunk(q_ref[rows, lsl], f_ref[rows, lsl], lb_ref[:, lsl], stack)
                v = i_ref[rows, hsl].astype(BF16)
                st = states[hh]
                st_ref[hh, ci] = st
                a = _hgrn_scores(ch, masks)
                o_ref[rows, hsl] = _dot_nt(ch["q_in"].astype(BF16), st.astype(BF16)) + _dot(a.astype(BF16), v)
                states[hh] = st * jnp.exp(ch["b_end"]) + _dot_tn(v, ch["kend"].astype(BF16))
        s_scr[0] = states[0]
        s_scr[1] = states[1]
        for hh in range(2):
            hsl = slice(A_VDIM * hh, A_VDIM * (hh + 1))
            o = o_ref[:, hsl]
            r = lax.rsqrt(jnp.mean(o * o, axis=-1, keepdims=True) + EPS)
            y_ref[:, hsl] = (o * r * hn_ref[:, hsl] * _silu(g_ref[:, hsl])).astype(y_ref.dtype)

    w2 = 2 * A_KDIM
    return pl.pallas_call(
        body, name=name, grid=(A_HEADS // 2, nt),
        in_specs=[pl.BlockSpec((tb, w2), lambda h, j: (j, COL_AQ // w2 + h)),
                  pl.BlockSpec((tb, w2), lambda h, j: (j, COL_AF // w2 + h)),
                  pl.BlockSpec((tb, LANES), lambda h, j: (j, COL_AI // LANES + h)),
                  pl.BlockSpec((tb, LANES), lambda h, j: (j, COL_AG // LANES + h)),
                  pl.BlockSpec((1, w2), lambda h, j: (0, h)),
                  pl.BlockSpec((1, LANES), lambda h, j: (0, 0))],
        out_specs=[pl.BlockSpec((tb, LANES), lambda h, j: (j, h)),
                   pl.BlockSpec((tb, LANES), lambda h, j: (j, h)),
                   pl.BlockSpec((2, nc, A_VDIM, A_KDIM), lambda h, j: (h, j, 0, 0))],
        out_shape=[jax.ShapeDtypeStruct((t, A_V), BF16), jax.ShapeDtypeStruct((t, A_V), F32),
                   jax.ShapeDtypeStruct((A_HEADS, t // c, A_VDIM, A_KDIM), F32)],
        scratch_shapes=[pltpu.VMEM((2, A_VDIM, A_KDIM), F32)],
        compiler_params=_cparams(("parallel", "arbitrary")),
    )(p, p, p, p, lb, hn2)


def _hgrn_bwd(p, lb, hn2, o_raw, states, dya, name):
    t = p.shape[0]
    tb = _rows(t)
    nt = t // tb
    nc = tb // A_CHUNK
    c = A_CHUNK

    def body(q_ref, f_ref, i_ref, g_ref, lb_ref, hn_ref, o_ref, st_ref, dy_ref,
             dq_ref, df_ref, di_ref, dg_ref, dlb_ref, dhn_ref, ds_scr, do_scr):
        j = pl.program_id(1)

        @pl.when(j == 0)
        def _():
            ds_scr[...] = jnp.zeros_like(ds_scr)
            dlb_ref[...] = jnp.zeros_like(dlb_ref)
            dhn_ref[...] = jnp.zeros_like(dhn_ref)

        stack, masks, incl, rev_incl = _hgrn_consts()
        for hh in range(2):
            hsl = slice(A_VDIM * hh, A_VDIM * (hh + 1))
            o = o_ref[:, hsl]
            g = g_ref[:, hsl]
            dy = dy_ref[:, hsl].astype(F32)
            hn = hn_ref[:, hsl]
            r = lax.rsqrt(jnp.mean(o * o, axis=-1, keepdims=True) + EPS)
            oh = o * r
            sgate = _silu(g)
            dg_ref[:, hsl] = dy * oh * hn * _dsilu(g)
            dhn_ref[0, :, hsl] += jnp.sum(dy * oh * sgate, axis=0, keepdims=True)
            doh = dy * hn * sgate
            do_scr[:, hsl] = r * (doh - oh * jnp.mean(doh * oh, axis=-1, keepdims=True))

        dstates = [ds_scr[0], ds_scr[1]]
        dlb_acc = [jnp.zeros((1, A_KDIM), F32), jnp.zeros((1, A_KDIM), F32)]
        for ci in reversed(range(nc)):
            rows = pl.ds(ci * c, c)
            for hh in range(2):
                lsl = slice(A_KDIM * hh, A_KDIM * (hh + 1))
                hsl = slice(A_VDIM * hh, A_VDIM * (hh + 1))
                lbv = lb_ref[:, lsl]
                q_raw = q_ref[rows, lsl]
                f_raw = f_ref[rows, lsl]
                ch = _hgrn_chunk(q_raw, f_raw, lbv, stack)
                v = i_ref[rows, hsl].astype(BF16)
                do = do_scr[rows, hsl]
                do_b = do.astype(BF16)
                st = st_ref[hh, ci]
                st_b = st.astype(BF16)
                dst = dstates[hh]
                dst_b = dst.astype(BF16)
                qs_b = ch["qs"].astype(BF16)
                kend_b = ch["kend"].astype(BF16)
                a = _hgrn_scores(ch, masks)
                da = jnp.where(incl, _dot_nt(do_b, v), 0.0)
                dv = _dot_tn(a.astype(BF16), do_b) + _dot_nt(kend_b, dst_b)
                dq_i = None
                dk_i = None
                kdk_i = None
                for i, mk in enumerate(masks):
                    dam = jnp.where(mk, da, 0.0).astype(BF16)
                    ki = (ch["k"] * ch["kfac"][i]).astype(BF16)
                    pq = _dot(dam, ki)
                    pk = _dot_tn(dam, qs_b)
                    dq_i = pq if dq_i is None else dq_i + pq
                    dk_i = ch["kfac"][i] * pk if dk_i is None else dk_i + ch["kfac"][i] * pk
                    kdk_i = ki.astype(F32) * pk if kdk_i is None else kdk_i + ki.astype(F32) * pk
                dq_x = _dot(do_b, st_b)
                dk_x = _dot(v, dst_b)
                dq = ch["e_sub"] * dq_i + ch["e_b"] * dq_x
                dk = dk_i + ch["e_end"] * dk_x
                dstates[hh] = dst * jnp.exp(ch["b_end"]) + _dot_tn(do_b, ch["q_in"].astype(BF16))
                kx = ch["kend"] * dk_x
                db = (qs_b.astype(F32) * dq_i + ch["q_in"] * dq_x) - (kdk_i + kx)
                later = (jnp.exp(ch["b_end"]) * jnp.sum(dst * st, axis=0, keepdims=True)
                         + jnp.sum(kx, axis=0, keepdims=True))
                dlogf = later + _ones_left(rev_incl, db)
                dfv = jnp.where(ch["f"] > TINY, dlogf / ch["f"], 0.0)
                dq_ref[rows, lsl] = dq * _dsilu(q_raw)
                df_ref[rows, lsl] = (1.0 - lbv) * ch["sg"] * ch["sgn"] * (dfv - dk)
                dlb_acc[hh] = dlb_acc[hh] + jnp.sum(dfv * (1.0 - ch["sg"]) - dk * ch["sgn"], axis=0, keepdims=True)
                di_ref[rows, hsl] = dv
        for hh in range(2):
            ds_scr[hh] = dstates[hh]
            dlb_ref[:, A_KDIM * hh:A_KDIM * (hh + 1)] += dlb_acc[hh]

    w2 = 2 * A_KDIM
    rev = lambda j: nt - 1 - j
    return pl.pallas_call(
        body, name=name, grid=(A_HEADS // 2, nt),
        in_specs=[pl.BlockSpec((tb, w2), lambda h, j: (rev(j), COL_AQ // w2 + h)),
                  pl.BlockSpec((tb, w2), lambda h, j: (rev(j), COL_AF // w2 + h)),
                  pl.BlockSpec((tb, LANES), lambda h, j: (rev(j), COL_AI // LANES + h)),
                  pl.BlockSpec((tb, LANES), lambda h, j: (rev(j), COL_AG // LANES + h)),
                  pl.BlockSpec((1, w2), lambda h, j: (0, h)),
                  pl.BlockSpec((1, LANES), lambda h, j: (0, 0)),
                  pl.BlockSpec((tb, LANES), lambda h, j: (rev(j), h)),
                  pl.BlockSpec((2, nc, A_VDIM, A_KDIM), lambda h, j: (h, rev(j), 0, 0)),
                  pl.BlockSpec((tb, LANES), lambda h, j: (rev(j), h))],
        out_specs=[pl.BlockSpec((tb, w2), lambda h, j: (rev(j), h)),
                   pl.BlockSpec((tb, w2), lambda h, j: (rev(j), h)),
                   pl.BlockSpec((tb, LANES), lambda h, j: (rev(j), h)),
                   pl.BlockSpec((tb, LANES), lambda h, j: (rev(j), h)),
                   pl.BlockSpec((1, w2), lambda h, j: (0, h)),
                   pl.BlockSpec((1, 1, LANES), lambda h, j: (h, 0, 0))],
        out_shape=[jax.ShapeDtypeStruct((t, A_QK), F32), jax.ShapeDtypeStruct((t, A_QK), F32),
                   jax.ShapeDtypeStruct((t, A_V), F32), jax.ShapeDtypeStruct((t, A_V), F32),
                   jax.ShapeDtypeStruct((1, A_QK), F32), jax.ShapeDtypeStruct((A_HEADS // 2, 1, LANES), F32)],
        scratch_shapes=[pltpu.VMEM((2, A_VDIM, A_KDIM), F32), pltpu.VMEM((tb, LANES), F32)],
        compiler_params=_cparams(("parallel", "arbitrary")),
    )(p, p, p, p, lb, hn2, o_raw, states, dya)


BLK = 128
SCALE = HDIM ** -0.5
SB_CHUNK = 4


def _softplus(z):
    return jnp.maximum(z, 0.0) + jnp.log(1.0 + jnp.exp(-jnp.abs(z)))


def _split2(x):
    hi = x.astype(BF16)
    return hi, (x - hi.astype(F32)).astype(BF16)


def _sb_sum_matrix(keep):
    sp = lax.broadcasted_iota(jnp.int32, (2 * BLK, 2 * BLK), 0) & (BLK - 1)
    s = lax.broadcasted_iota(jnp.int32, (2 * BLK, 2 * BLK), 1)
    return jnp.where(jnp.logical_or(s >= BLK, keep(sp, s)), 1.0, 0.0).astype(BF16)


def _sb_fwd(p, kv, name, gather=None):
    t = p.shape[0]
    nq = t // BLK
    nh = B_HEADS // 2
    cw = SB_CHUNK * BLK
    fused = gather is not None
    n = len(gather) if fused else 0

    def body(*refs):
        q_ref, kb, vb = refs[:3]
        o_ref, tot_ref = refs[3 + n:5 + n]
        zbuf, stage, sbuf, abuf = refs[5 + 2 * n:9 + 2 * n]
        hp = pl.program_id(0)
        qi = pl.program_id(1)
        if fused:
            g = _Many(_Gather, refs[3:3 + n], refs[5 + n:5 + 2 * n], *refs[9 + 2 * n:])
            pl.when(jnp.logical_and(hp == 0, qi == 0))(g.start)
            pl.when(jnp.logical_and(hp == nh - 1, qi == 0))(g.forward)

        @pl.when(qi == 0)
        def _():
            abuf[...] = jnp.zeros_like(abuf)

        row = lax.broadcasted_iota(jnp.int32, (BLK, BLK), 0)
        col = lax.broadcasted_iota(jnp.int32, (BLK, BLK), 1)
        sums = _sb_sum_matrix(lambda sp, s: sp >= s)
        hsl = [slice(HDIM * h, HDIM * (h + 1)) for h in range(2)]
        nchunk = qi // SB_CHUNK + 1
        for h in range(2):
            zbuf[h] = _dot_nt((q_ref[:, hsl[h]] * SCALE).astype(BF16), kb[:, hsl[h]])

        col_minus_row = col - row

        def causal(j):
            return col_minus_row < (qi - j) * BLK

        def l_pass(c, carry):
            for b in range(SB_CHUNK):
                j = c * SB_CHUNK + b
                off = pl.multiple_of(j * BLK, BLK)
                mask = causal(j)
                for h in range(2):
                    lm = jnp.where(mask, -_softplus(zbuf[h, :, pl.ds(off, BLK)]), 0.0)
                    hi, lo = _split2(lm)
                    stage[h, pl.ds(off, BLK), :BLK] = hi
                    stage[h, pl.ds(off, BLK), BLK:] = lo
            return carry

        lax.fori_loop(0, nchunk, l_pass, 0)

        def sum_pass(c, carry):
            rows = pl.ds(pl.multiple_of(c * cw, cw), cw)
            for h in range(2):
                sbuf[h, rows, :] = _dot(stage[h, rows, :], sums)
            return carry

        lax.fori_loop(0, nchunk, sum_pass, 0)

        def a_pass(it, carry):
            c = nchunk - 1 - it
            runs = list(carry)
            for b in reversed(range(SB_CHUNK)):
                j = c * SB_CHUNK + b
                off = pl.multiple_of(j * BLK, BLK)
                mask = causal(j)
                for h in range(2):
                    s = sbuf[h, pl.ds(off, BLK), :BLK]
                    a = jnp.where(mask, jnp.exp(zbuf[h, :, pl.ds(off, BLK)] + s + runs[h]), 0.0)
                    abuf[h, :, pl.ds(off, BLK)] = a.astype(BF16)
                    runs[h] = runs[h] + sbuf[h, pl.ds(off, BLK), BLK:]
            return tuple(runs)

        zero = jnp.zeros((BLK, BLK), F32)
        runs = lax.fori_loop(0, nchunk, a_pass, (zero, zero))
        for h in range(2):
            tot_ref[:, hsl[h]] = runs[h][:, :HDIM]
            o_ref[:, hsl[h]] = _dot(abuf[h], vb[:, hsl[h]])
        if fused:
            pl.when(jnp.logical_and(hp == nh - 1, qi == nq - 1))(g.finish)

    out_blk = pl.BlockSpec((BLK, LANES), lambda h, i: (i, h))
    hbm = pl.BlockSpec(memory_space=pl.ANY)
    in_specs = [pl.BlockSpec((BLK, LANES), lambda h, i: (i, COL_BQ // LANES + h)),
                pl.BlockSpec((t, LANES), lambda h, i: (0, h)),
                pl.BlockSpec((t, LANES), lambda h, i: (0, B_W // LANES + h))]
    out_shape = [jax.ShapeDtypeStruct((t, B_W), F32)] * 2
    scratch = [pltpu.VMEM((2, BLK, t), F32), pltpu.VMEM((2, t, 2 * BLK), BF16),
               pltpu.VMEM((2, t, 2 * BLK), F32), pltpu.VMEM((2, BLK, t), BF16)]
    if fused:
        out_shape = out_shape + _gathered_shapes(gather)
    return pl.pallas_call(
        body, name=name, grid=(nh, nq),
        in_specs=in_specs + [hbm] * n,
        out_specs=[out_blk, out_blk] + [hbm] * n,
        out_shape=out_shape,
        scratch_shapes=scratch + (_comm_sems(n) if fused else []),
        compiler_params=_cparams(("arbitrary", "arbitrary")),
    )(p, kv, kv, *(gather if fused else []))


def _sb_bwd(p, kv, tot, do, name, exchange=None):
    t = p.shape[0]
    nq = t // BLK
    nh = B_HEADS // 2
    cw = SB_CHUNK * BLK
    fused = exchange is not None
    n = len(exchange) if fused else 0

    def body(*refs):
        q_ref, kb, vb, tot_ref, do_ref = refs[:5]
        dq_ref, dk_ref, dv_ref = refs[5 + n:8 + n]
        zbuf, dabuf, stage, gstage, sbuf, abuf, dzbuf, dkt, dvt = refs[8 + 2 * n:17 + 2 * n]
        hp = pl.program_id(0)
        qi = pl.program_id(1)
        if fused:
            ex = _Many(_Exchange, refs[5:5 + n], refs[8 + n:8 + 2 * n], *refs[17 + 2 * n:])
            pl.when(jnp.logical_and(hp == 0, qi == 0))(ex.start)

        @pl.when(qi == 0)
        def _():
            dkt[...] = jnp.zeros_like(dkt)
            dvt[...] = jnp.zeros_like(dvt)
            dzbuf[...] = jnp.zeros_like(dzbuf)
            abuf[...] = jnp.zeros_like(abuf)

        row = lax.broadcasted_iota(jnp.int32, (BLK, BLK), 0)
        col = lax.broadcasted_iota(jnp.int32, (BLK, BLK), 1)
        sums = _sb_sum_matrix(lambda sp, s: sp <= s)
        hsl = [slice(HDIM * h, HDIM * (h + 1)) for h in range(2)]
        dob = [do_ref[:, hsl[h]].astype(BF16) for h in range(2)]
        total =[jnp.concatenate([tot_ref[:, hsl[h]], tot_ref[:, hsl[h]]], axis=1) for h in range(2)]
        nchunk = qi // SB_CHUNK + 1
        for h in range(2):
            zbuf[h] = _dot_nt((q_ref[:, hsl[h]] * SCALE).astype(BF16), kb[:, hsl[h]])
            dabuf[h] = _dot_nt(dob[h], vb[:, hsl[h]])

        col_minus_row = col - row

        def causal(j):
            return col_minus_row < (qi - j) * BLK

        def blocks(c):
            for b in range(SB_CHUNK):
                j = c * SB_CHUNK + b
                yield j, pl.ds(pl.multiple_of(j * BLK, BLK), BLK)

        def l_pass(c, carry):
            for j, blk_ in blocks(c):
                mask = causal(j)
                for h in range(2):
                    lm = jnp.where(mask, -_softplus(zbuf[h, :, blk_]), 0.0)
                    hi, lo = _split2(lm)
                    stage[h, blk_, :BLK] = hi
                    stage[h, blk_, BLK:] = lo
            return carry

        lax.fori_loop(0, nchunk, l_pass, 0)

        def sum_pass(src):
            def run_(c, carry):
                rows = pl.ds(pl.multiple_of(c * cw, cw), cw)
                for h in range(2):
                    sbuf[h, rows, :] = _dot(src[h, rows, :], sums)
                return carry
            lax.fori_loop(0, nchunk, run_, 0)

        sum_pass(stage)

        def g_pass(c, carry):
            runs = list(carry)
            for j, blk_ in blocks(c):
                mask = causal(j)
                for h in range(2):
                    lm = stage[h, blk_, :BLK].astype(F32) + stage[h, blk_, BLK:].astype(F32)
                    log_a = zbuf[h, :, blk_] + lm + (total[h] - runs[h] - sbuf[h, blk_, :BLK])
                    a = jnp.where(mask, jnp.exp(log_a), 0.0)
                    abuf[h, :, blk_] = a.astype(BF16)
                    hi, lo = _split2(a * dabuf[h, :, blk_])
                    gstage[h, blk_, :BLK] = hi
                    gstage[h, blk_, BLK:] = lo
                    runs[h] = runs[h] + sbuf[h, blk_, BLK:]
            return tuple(runs)

        zero = jnp.zeros((BLK, BLK), F32)
        lax.fori_loop(0, nchunk, g_pass, (zero, zero))
        sum_pass(gstage)

        def dz_pass(c, carry):
            runs = list(carry)
            for j, blk_ in blocks(c):
                mask = causal(j)
                for h in range(2):
                    lm = stage[h, blk_, :BLK].astype(F32) + stage[h, blk_, BLK:].astype(F32)
                    g = gstage[h, blk_, :BLK].astype(F32) + gstage[h, blk_, BLK:].astype(F32)
                    before = runs[h] + sbuf[h, blk_, :BLK] - g
                    dz = jnp.where(mask, g * jnp.exp(lm) - jnp.exp(zbuf[h, :, blk_] + lm) * before, 0.0)
                    dzbuf[h, :, blk_] = (dz * SCALE).astype(BF16)
                    runs[h] = runs[h] + sbuf[h, blk_, BLK:]
            return tuple(runs)

        lax.fori_loop(0, nchunk, dz_pass, (zero, zero))
        for h in range(2):
            dq_ref[:, hsl[h]] = _dot(dzbuf[h], kb[:, hsl[h]])
        q_t = q_ref[...].T.astype(BF16)
        do_t = do_ref[...].T.astype(BF16)
        for h in range(2):
            dkt[hsl[h], :] += _dot(q_t[hsl[h], :], dzbuf[h])
            dvt[hsl[h], :] += _dot(do_t[hsl[h], :], abuf[h])

        @pl.when(qi == nq - 1)
        def _():
            dk_ref[...] = dkt[...].T
            dv_ref[...] = dvt[...].T

        if fused:
            pl.when(jnp.logical_and(hp == nh - 1, qi == nq - 1))(ex.finish)

    blk = lambda h, i: (i, h)
    whole = lambda h, i: (0, h)
    hbm = pl.BlockSpec(memory_space=pl.ANY)
    in_specs = [pl.BlockSpec((BLK, LANES), lambda h, i: (i, COL_BQ // LANES + h)),
                pl.BlockSpec((t, LANES), lambda h, i: (0, h)),
                pl.BlockSpec((t, LANES), lambda h, i: (0, B_W // LANES + h)),
                pl.BlockSpec((BLK, LANES), blk), pl.BlockSpec((BLK, LANES), blk)]
    out_specs = [pl.BlockSpec((BLK, LANES), blk), pl.BlockSpec((t, LANES), whole), pl.BlockSpec((t, LANES), whole)]
    out_shape = [jax.ShapeDtypeStruct((t, B_W), F32)] * 3
    scratch = [pltpu.VMEM((2, BLK, t), F32), pltpu.VMEM((2, BLK, t), F32),
               pltpu.VMEM((2, t, 2 * BLK), BF16), pltpu.VMEM((2, t, 2 * BLK), BF16),
               pltpu.VMEM((2, t, 2 * BLK), F32), pltpu.VMEM((2, BLK, t), BF16),
               pltpu.VMEM((2, BLK, t), BF16), pltpu.VMEM((LANES, t), F32), pltpu.VMEM((LANES, t), F32)]
    if fused:
        out_shape = out_shape + [jax.ShapeDtypeStruct(e.shape, e.dtype) for e in exchange]
    return pl.pallas_call(
        body, name=name, grid=(nh, nq),
        in_specs=in_specs + [hbm] * n,
        out_specs=out_specs + [hbm] * n,
        out_shape=out_shape,
        scratch_shapes=scratch + (_comm_sems(n) if fused else []),
        compiler_params=_cparams(("arbitrary", "arbitrary")),
    )(p, kv, kv, tot, do, *(exchange if fused else []))


def _alibi_slopes(n):
    def pow2(m):
        start = 2.0 ** (-8.0 / m)
        return [start ** (i + 1) for i in range(m)]
    if math.log2(n).is_integer():
        s = pow2(n)
    else:
        c = 2 ** int(math.floor(math.log2(n)))
        s = pow2(c) + pow2(2 * c)[0::2][: n - c]
    return sorted(s, reverse=True)


def _dil_scores(qh, kh, sl, prev, exists=None):
    row = lax.broadcasted_iota(jnp.int32, (BLK, BLK), 0)
    col = lax.broadcasted_iota(jnp.int32, (BLK, BLK), 1)
    dist = row - col + (BLK if prev else 0)
    if prev:
        valid = (col - row) >= jnp.where(exists, 0, 2 * BLK)
    else:
        valid = col <= row
    s = _dot_nt(qh, kh) - sl * dist.astype(F32)
    return s, valid


def _dil_rows(rho, r):
    return pl.ds(rho, BLK, stride=r) if r > 1 else pl.ds(0, BLK)


def _dil_fwd(p, gi, name):
    t = p.shape[0]
    _, r = C_GROUPS[gi]
    sbr = BLK * r
    nsb = t // sbr
    slope_cols = _slope_cols(gi)

    def body(q_ref, kc_ref, kp_ref, vc_ref, vp_ref, sl_ref, o_ref, lse_ref):
        i = pl.program_id(1)

        def residue(rho, carry):
            rows = _dil_rows(rho, r)
            q2, kc2, kp2, vc2, vp2 = q_ref[rows, :], kc_ref[rows, :], kp_ref[rows, :], vc_ref[rows, :], vp_ref[rows, :]
            outs, lses = [], []
            for h in range(2):
                hs = slice(HDIM * h, HDIM * (h + 1))
                sl = sl_ref[:, HDIM * h:HDIM * h + 1]
                qh = (q2[:, hs] * SCALE).astype(BF16)
                sc, vc_ok = _dil_scores(qh, kc2[:, hs].astype(BF16), sl, False)
                sp, vp_ok = _dil_scores(qh, kp2[:, hs].astype(BF16), sl, True, i > 0)
                sc = jnp.where(vc_ok, sc, NEG_BIG)
                sp = jnp.where(vp_ok, sp, NEG_BIG)
                m = jnp.maximum(jnp.max(sc, axis=1, keepdims=True), jnp.max(sp, axis=1, keepdims=True))
                pc = jnp.exp(sc - m)
                pp = jnp.exp(sp - m)
                den = jnp.sum(pc, axis=1, keepdims=True) + jnp.sum(pp, axis=1, keepdims=True)
                o = _dot(pc.astype(BF16), vc2[:, hs].astype(BF16)) + _dot(pp.astype(BF16), vp2[:, hs].astype(BF16))
                outs.append(o / den)
                lses.append(jnp.broadcast_to(m + jnp.log(den), (BLK, HDIM)))
            o_ref[rows, :] = jnp.concatenate(outs, axis=1)
            lse_ref[rows, :] = jnp.concatenate(lses, axis=1)
            return carry

        lax.fori_loop(0, r, residue, 0)

    def at(col0, pick):
        return pl.BlockSpec((sbr, LANES), lambda c, i: (pick(i), col0 // LANES + c))

    cur = lambda i: i
    prv = lambda i: jnp.maximum(i - 1, 0)
    cq, ck, cv = COL_CQ + gi * C_OUT, COL_CK + gi * C_OUT, COL_CV + gi * C_OUT
    out = pl.BlockSpec((sbr, LANES), lambda c, i: (i, c))
    return pl.pallas_call(
        body, name=name, grid=(C_OUT // LANES, nsb),
        in_specs=[at(cq, cur), at(ck, cur), at(ck, prv), at(cv, cur), at(cv, prv),
                  pl.BlockSpec((1, LANES), lambda c, i: (0, c))],
        out_specs=[out, out], out_shape=[jax.ShapeDtypeStruct((t, C_OUT), F32)] * 2,
        compiler_params=_cparams(("parallel", "parallel")),
    )(p, p, p, p, p, slope_cols)


def _dil_bwd(p, do, o, lse, gi, name):
    t = p.shape[0]
    _, r = C_GROUPS[gi]
    sbr = BLK * r
    nsb = t // sbr
    slope_cols = _slope_cols(gi)

    def body(q_ref, qn_ref, kc_ref, kp_ref, vc_ref, vp_ref, do_ref, don_ref, o_ref, on_ref, l_ref, ln_ref, sl_ref,
             dq_ref, dk_ref, dv_ref):
        i = pl.program_id(1)
        has_prev = i > 0
        has_next = i < nsb - 1

        def residue(rho, carry):
            rows = _dil_rows(rho, r)
            q2, qn2, kc2, kp2, vc2, vp2 = (ref[rows, :] for ref in (q_ref, qn_ref, kc_ref, kp_ref, vc_ref, vp_ref))
            do2, don2, o2, on2, l2, ln2 = (ref[rows, :] for ref in (do_ref, don_ref, o_ref, on_ref, l_ref, ln_ref))
            dqs, dks, dvs = [], [], []
            for h in range(2):
                hs = slice(HDIM * h, HDIM * (h + 1))
                sl = sl_ref[:, HDIM * h:HDIM * h + 1]
                qb = q2[:, hs].astype(BF16)
                qnb = qn2[:, hs].astype(BF16)
                qh = (q2[:, hs] * SCALE).astype(BF16)
                qnh = (qn2[:, hs] * SCALE).astype(BF16)
                kc = kc2[:, hs].astype(BF16)
                kp = kp2[:, hs].astype(BF16)
                vc = vc2[:, hs].astype(BF16)
                vp = vp2[:, hs].astype(BF16)
                do_f = do2[:, hs]
                don_f = don2[:, hs]
                dob = do_f.astype(BF16)
                donb = don_f.astype(BF16)
                delta = jnp.sum(do_f * o2[:, hs], axis=1, keepdims=True)
                deltan = jnp.sum(don_f * on2[:, hs], axis=1, keepdims=True)
                lse_c = l2[:, HDIM * h:HDIM * h + 1]
                lse_n = ln2[:, HDIM * h:HDIM * h + 1]
                s, ok = _dil_scores(qh, kc, sl, False)
                p_cc = jnp.where(ok, jnp.exp(jnp.where(ok, s, NEG_BIG) - lse_c), 0.0)
                ds_cc = p_cc * (_dot_nt(dob, vc) - delta)
                s, ok = _dil_scores(qh, kp, sl, True, has_prev)
                p_cp = jnp.where(ok, jnp.exp(jnp.where(ok, s, NEG_BIG) - lse_c), 0.0)
                ds_cp = p_cp * (_dot_nt(dob, vp) - delta)
                s, ok = _dil_scores(qnh, kc, sl, True, has_next)
                p_nc = jnp.where(ok, jnp.exp(jnp.where(ok, s, NEG_BIG) - lse_n), 0.0)
                ds_nc = p_nc * (_dot_nt(donb, vc) - deltan)
                ds_cc_b = (ds_cc * SCALE).astype(BF16)
                ds_cp_b = (ds_cp * SCALE).astype(BF16)
                ds_nc_b = (ds_nc * SCALE).astype(BF16)
                dqs.append(_dot(ds_cc_b, kc) + _dot(ds_cp_b, kp))
                dks.append(_dot_tn(ds_cc_b, qb) + _dot_tn(ds_nc_b, qnb))
                dvs.append(_dot_tn(p_cc.astype(BF16), dob) + _dot_tn(p_nc.astype(BF16), donb))
            dq_ref[rows, :] = jnp.concatenate(dqs, axis=1)
            dk_ref[rows, :] = jnp.concatenate(dks, axis=1)
            dv_ref[rows, :] = jnp.concatenate(dvs, axis=1)
            return carry

        lax.fori_loop(0, r, residue, 0)

    def at(col0, pick):
        return pl.BlockSpec((sbr, LANES), lambda c, i: (pick(i), col0 // LANES + c))

    cur = lambda i: i
    prv = lambda i: jnp.maximum(i - 1, 0)
    nxt = lambda i: jnp.minimum(i + 1, nsb - 1)
    cq, ck, cv = COL_CQ + gi * C_OUT, COL_CK + gi * C_OUT, COL_CV + gi * C_OUT
    return pl.pallas_call(
        body, name=name, grid=(C_OUT // LANES, nsb),
        in_specs=[at(cq, cur), at(cq, nxt), at(ck, cur), at(ck, prv), at(cv, cur), at(cv, prv),
                  at(0, cur), at(0, nxt), at(0, cur), at(0, nxt), at(0, cur), at(0, nxt),
                  pl.BlockSpec((1, LANES), lambda c, i: (0, c))],
        out_specs=[at(0, cur)] * 3, out_shape=[jax.ShapeDtypeStruct((t, C_OUT), F32)] * 3,
        compiler_params=_cparams(("parallel", "parallel")),
    )(p, p, p, p, p, p, do, do, o, o, lse, lse, slope_cols)


def _dil_merge(os_, ls_, name):
    t, w = os_[0].shape
    tr = _rows(t)

    def body(o0, o1, o2, l0, l1, l2, y_ref, lse_ref):
        a, b, c = l0[...], l1[...], l2[...]
        m = jnp.maximum(jnp.maximum(a, b), c)
        ea, eb, ec = jnp.exp(a - m), jnp.exp(b - m), jnp.exp(c - m)
        den = ea + eb + ec
        y_ref[...] = (ea * o0[...] + eb * o1[...] + ec * o2[...]) / den
        lse_ref[...] = m + jnp.log(den)

    row = pl.BlockSpec((tr, w), lambda i: (i, 0))
    return pl.pallas_call(
        body, name=name, grid=(t // tr,), in_specs=[row] * 6, out_specs=[row, row],
        out_shape=[jax.ShapeDtypeStruct((t, w), F32)] * 2, compiler_params=_cparams(("parallel",)),
    )(*os_, *ls_)


def _gate_fwd(ys, gl, ws, name):
    t = gl.shape[0]
    d = gl.shape[1] // N_BRANCH
    tr = _rows(t)

    def body(ya, yb, yc, gl_ref, wa, wb, wc, m_ref):
        acc = None
        for i, (y, w) in enumerate(((ya, wa), (yb, wb), (yc, wc))):
            z = _dot(y[...].astype(BF16), w[...])
            term = jax.nn.sigmoid(gl_ref[:, i * d:(i + 1) * d]) * z
            acc = term if acc is None else acc + term
        m_ref[...] = acc.astype(m_ref.dtype)

    rows = [pl.BlockSpec((tr, y.shape[1]), lambda i: (i, 0)) for y in ys]
    wsp = [pl.BlockSpec(w.shape, lambda i: (0, 0)) for w in ws]
    return pl.pallas_call(
        body, name=name, grid=(t // tr,),
        in_specs=rows + [pl.BlockSpec((tr, N_BRANCH * d), lambda i: (i, 0))] + wsp,
        out_specs=pl.BlockSpec((tr, d), lambda i: (i, 0)), out_shape=jax.ShapeDtypeStruct((t, d), BF16),
        compiler_params=_cparams(("parallel",)),
    )(*ys, gl, *ws)


def _gate_bwd(dm, ys, gl, ws, name):
    t = gl.shape[0]
    d = gl.shape[1] // N_BRANCH
    tr = _rows(t)

    def body(dm_ref, ya, yb, yc, gl_ref, wa, wb, wc, dya, dyb, dyc, dgl_ref, dwa, dwb, dwc):
        step = pl.program_id(0)
        dmv = dm_ref[...].astype(F32)
        for i, (y, w, dy, dw) in enumerate(((ya, wa, dya, dwa), (yb, wb, dyb, dwb), (yc, wc, dyc, dwc))):
            yb16 = y[...].astype(BF16)
            z = _dot(yb16, w[...])
            sg = jax.nn.sigmoid(gl_ref[:, i * d:(i + 1) * d])
            dgl_ref[:, i * d:(i + 1) * d] = dmv * z * sg * (1.0 - sg)
            e = (dmv * sg).astype(BF16)
            dy[...] = _dot_nt(e, w[...])
            contrib = _dot_tn(yb16, e)

            @pl.when(step == 0)
            def _(dw=dw, contrib=contrib):
                dw[...] = contrib

            @pl.when(step > 0)
            def _(dw=dw, contrib=contrib):
                dw[...] += contrib

    rows = [pl.BlockSpec((tr, y.shape[1]), lambda i: (i, 0)) for y in ys]
    wsp = [pl.BlockSpec(w.shape, lambda i: (0, 0)) for w in ws]
    gsp = pl.BlockSpec((tr, N_BRANCH * d), lambda i: (i, 0))
    return pl.pallas_call(
        body, name=name, grid=(t // tr,),
        in_specs=[pl.BlockSpec((tr, d), lambda i: (i, 0))] + rows + [gsp] + wsp,
        out_specs=rows + [gsp] + wsp,
        out_shape=[jax.ShapeDtypeStruct(y.shape, F32) for y in ys] + [jax.ShapeDtypeStruct(gl.shape, F32)]
        + [jax.ShapeDtypeStruct(w.shape, F32) for w in ws],
        compiler_params=_cparams(("arbitrary",)),
    )(dm, *ys, gl, *ws)


def _adamw(w, m, v, gparts, name):
    r, c = w.shape
    n = gparts.shape[0]
    br = LANES if r % LANES == 0 else r
    c1 = 1.0 - ADAM_B1 ** ADAM_STEP
    c2 = 1.0 - ADAM_B2 ** ADAM_STEP

    def body(w_ref, m_ref, v_ref, g_ref, go_ref, d_ref, mo_ref, vo_ref):
        g = g_ref[0].astype(F32)
        for i in range(1, n):
            g = g + g_ref[i].astype(F32)
        mn = ADAM_B1 * m_ref[...] + (1.0 - ADAM_B1) * g
        vn = ADAM_B2 * v_ref[...] + (1.0 - ADAM_B2) * (g * g)
        go_ref[...] = g
        mo_ref[...] = mn
        vo_ref[...] = vn
        d_ref[...] = -ADAM_LR * ((mn / c1) / (jnp.sqrt(vn / c2) + ADAM_EPS) + ADAM_WD * w_ref[...])

    blk = pl.BlockSpec((br, c), lambda i: (i, 0))
    return pl.pallas_call(
        body, name=name, grid=(r // br,),
        in_specs=[blk, blk, blk, pl.BlockSpec((n, br, c), lambda i: (0, i, 0))],
        out_specs=[blk] * 4, out_shape=[jax.ShapeDtypeStruct((r, c), F32)] * 4,
        compiler_params=_cparams(("parallel",)),
    )(w, m, v, gparts)


def _my_coords():
    return lax.axis_index("x"), lax.axis_index("y"), lax.axis_index("c")


COMM_SEMS = [pltpu.SemaphoreType.DMA((7,)), pltpu.SemaphoreType.DMA((7,)), pltpu.SemaphoreType.DMA]


class _Gather:
    def __init__(self, x_ref, out_ref, send_sems, recv_sems, local_sem):
        self.x_ref, self.out_ref = x_ref, out_ref
        self.send_sems, self.recv_sems, self.local_sem = send_sems, recv_sems, local_sem
        self.m_per = x_ref.shape[0]
        x, y, c = _my_coords()
        self.c = c
        self.me, self.sibling = (x, y, c), (x, y, 1 - c)
        self.chips = [(1 - x, y), (x, 1 - y), (1 - x, 1 - y)]

    def rows(self, px, py, pc):
        return self.out_ref.at[pl.ds((4 * px + 2 * py + pc) * self.m_per, self.m_per), :]

    def copy(self, k, block, to, src=None):
        return pltpu.make_async_remote_copy(
            src_ref=self.rows(*block) if src is None else src, dst_ref=self.rows(*block),
            send_sem=self.send_sems.at[k], recv_sem=self.recv_sems.at[k], device_id=to, device_id_type=MESH)

    def mine(self):
        return pltpu.make_async_copy(self.x_ref, self.rows(*self.me), self.local_sem)

    def first(self):
        out = [self.copy(0, self.me, self.sibling, src=self.x_ref)]
        return out + [self.copy(1 + j, self.me, (*chip, self.c), src=self.x_ref) for j, chip in enumerate(self.chips)]

    def passed(self):
        return [self.copy(4 + j, (*chip, self.c), self.sibling) for j, chip in enumerate(self.chips)]

    def start(self):
        self.mine().start()
        for cp in self.first():
            cp.start()

    def forward(self):
        passed = self.passed()
        for j, chip in enumerate(self.chips):
            self.copy(1 + j, (*chip, self.c), self.me).wait_recv()
            passed[j].start()

    def finish(self):
        self.copy(0, self.sibling, self.me).wait_recv()
        for j, chip in enumerate(self.chips):
            self.copy(4 + j, (*chip, 1 - self.c), self.me).wait_recv()
        for cp in self.first() + self.passed():
            cp.wait_send()
        self.mine().wait()


class _Exchange:
    def __init__(self, send_ref, recv_ref, send_sems, recv_sems, local_sem):
        self.send_ref, self.recv_ref = send_ref, recv_ref
        self.send_sems, self.recv_sems, self.local_sem = send_sems, recv_sems, local_sem
        x, y, c = _my_coords()
        self.me = 4 * x + 2 * y + c
        self.peers = []
        for k in range(1, N_DEV):
            px = 1 - x if k & 4 else x
            py = 1 - y if k & 2 else y
            pc = 1 - c if k & 1 else c
            self.peers.append((4 * px + 2 * py + pc, (px, py, pc)))

    def mine(self):
        return pltpu.make_async_copy(self.send_ref.at[self.me], self.recv_ref.at[self.me], self.local_sem)

    def copy(self, k, src_slot, dst_slot):
        return pltpu.make_async_remote_copy(
            src_ref=self.send_ref.at[src_slot], dst_ref=self.recv_ref.at[dst_slot],
            send_sem=self.send_sems.at[k], recv_sem=self.recv_sems.at[k],
            device_id=self.peers[k][1], device_id_type=MESH)

    def start(self):
        self.mine().start()
        for k, (peer, _) in enumerate(self.peers):
            self.copy(k, peer, self.me).start()

    def finish(self):
        for k, (peer, _) in enumerate(self.peers):
            self.copy(k, peer, self.me).wait_send()
            self.copy(k, self.me, peer).wait_recv()
        self.mine().wait()


def _all_gather(x_shard, in_vmem, with_sum, name):
    m_per, n = x_shard.shape

    def body(x_ref, out_ref, *rest):
        if with_sum:
            sum_ref, send_sems, recv_sems, local_sem = rest
        else:
            send_sems, recv_sems, local_sem = rest
        g = _Gather(x_ref, out_ref, send_sems, recv_sems, local_sem)
        g.start()
        g.forward()
        g.finish()
        if with_sum:
            acc = out_ref[pl.ds(0, m_per), :]
            for d in range(1, N_DEV):
                acc = acc + out_ref[pl.ds(d * m_per, m_per), :]
            sum_ref[...] = acc

    space = pltpu.VMEM if in_vmem else pl.ANY
    out_shape = [jax.ShapeDtypeStruct((N_DEV * m_per, n), x_shard.dtype)]
    out_specs = [pl.BlockSpec(memory_space=space)]
    if with_sum:
        out_shape.append(jax.ShapeDtypeStruct((m_per, n), x_shard.dtype))
        out_specs.append(pl.BlockSpec(memory_space=pltpu.VMEM))
    res = pl.pallas_call(
        body, name=name, out_shape=out_shape, in_specs=[pl.BlockSpec(memory_space=space)], out_specs=out_specs,
        scratch_shapes=COMM_SEMS, compiler_params=pltpu.CompilerParams(vmem_limit_bytes=VMEM_LIMIT),
    )(x_shard)
    return res if with_sum else res[0]


def _comm_sems(n):
    return [pltpu.SemaphoreType.DMA((n, 7)), pltpu.SemaphoreType.DMA((n, 7)), pltpu.SemaphoreType.DMA((n,))]


class _Many:
    def __init__(self, kind, ins, outs, send_sems, recv_sems, local_sems):
        self.parts = [kind(i, o, send_sems.at[b], recv_sems.at[b], local_sems.at[b])
                      for b, (i, o) in enumerate(zip(ins, outs))]

    def start(self):
        for part in self.parts:
            part.start()

    def forward(self):
        for part in self.parts:
            part.forward()

    def finish(self):
        for part in self.parts:
            part.finish()


def _gathered_shapes(shards):
    return [jax.ShapeDtypeStruct((N_DEV * s.shape[0],) + s.shape[1:], s.dtype) for s in shards]


def _all_gather_many(shards, name):
    n = len(shards)

    def body(*refs):
        g = _Many(_Gather, refs[:n], refs[n:2 * n], *refs[2 * n:])
        g.start()
        g.forward()
        g.finish()

    hbm = pl.BlockSpec(memory_space=pl.ANY)
    return pl.pallas_call(body, name=name, out_shape=_gathered_shapes(shards), in_specs=[hbm] * n,
                          out_specs=[hbm] * n, scratch_shapes=_comm_sems(n))(*shards)


def _all_to_all_many(sends, name):
    n = len(sends)

    def body(*refs):
        ex = _Many(_Exchange, refs[:n], refs[n:2 * n], *refs[2 * n:])
        ex.start()
        ex.finish()

    hbm = pl.BlockSpec(memory_space=pl.ANY)
    return pl.pallas_call(body, name=name, out_shape=[jax.ShapeDtypeStruct(s.shape, s.dtype) for s in sends],
                          in_specs=[hbm] * n, out_specs=[hbm] * n, scratch_shapes=_comm_sems(n))(*sends)


def _row(v):
    return v.reshape(1, -1)


def _ffn_fwd(x, w_in, w_out, g_pre, g_post, m, res_w, tag):
    shift, scale, gate = m[0], m[1], m[2]
    mpre = _row(g_pre * (1.0 + scale))
    mpost = _row(res_w * gate * g_post)
    h = _rms_fwd(x, mpre, _row(shift), None, BF16, tag + "_pre")
    u = _matmul(h, w_in, out_dtype=BF16, name=tag + "_in")
    s = _swiglu_fwd(u, tag + "_act")
    y = _matmul(s, w_out, name=tag + "_out")
    x_new = _rms_fwd(y, mpost, jnp.zeros_like(mpost), x, F32, tag + "_post")
    return x_new, (x, h, u, s, y, mpre, mpost)


def _sub_bwd_post(dx_new, y, mpost, g_post, gate, res_w, tag):
    dy, c1, _ = _rms_bwd(dx_new, y, mpost, None, BF16, tag + "_post_bwd")
    c1 = c1[0]
    return dy, c1 * res_w * g_post, c1 * res_w * gate


def _sub_bwd_pre(dh, x, mpre, dx_new, g_pre, scale, tag):
    dx, c2, c3 = _rms_bwd(dh, x, mpre, dx_new, F32, tag + "_pre_bwd")
    c2, c3 = c2[0], c3[0]
    return dx, c3, c2 * g_pre, c2 * (1.0 + scale)


def _ffn_bwd(dx_new, saved, w_in, w_out, g_pre, g_post, m, res_w, tag):
    x, h, u, s, y, mpre, mpost = saved
    scale, gate = m[1], m[2]
    dy, dgate, dg_post = _sub_bwd_post(dx_new, y, mpost, g_post, gate, res_w, tag)
    ds = _matmul(dy, w_out, tb=True, out_dtype=BF16, name=tag + "_out_dx")
    dw_out = _matmul(s, dy, ta=True, out_dtype=BF16, name=tag + "_out_dw")
    du = _swiglu_bwd(u, ds, tag + "_act_bwd")
    dh = _matmul(du, w_in, tb=True, name=tag + "_in_dx")
    dw_in = _matmul(h, du, ta=True, out_dtype=BF16, name=tag + "_in_dw")
    dx, dshift, dscale, dg_pre = _sub_bwd_pre(dh, x, mpre, dx_new, g_pre, scale, tag)
    return dx, dw_in, dw_out, jnp.stack([dshift, dscale, dgate]), dg_pre, dg_post


def _slope_cols(gi):
    _, r = C_GROUPS[gi]
    sl = jnp.asarray(_alibi_slopes(C_HEADS)[gi * C_HPG:(gi + 1) * C_HPG], F32) * float(r)
    return jnp.repeat(sl, HDIM).reshape(1, C_OUT)


def _mix_fwd(x, w, g_pre, g_post, m, lb, hn, tag, gather=None):
    t, d = x.shape
    shift, scale, gate = m[0], m[1], m[2]
    mpre = _row(g_pre * (1.0 + scale))
    mpost = _row(gate * g_post)
    h = _rms_fwd(x, mpre, _row(shift), None, BF16, tag + "_pre")
    p = _matmul(h, w["w_in"], name=tag + "_in")
    hn2 = _row(jnp.tile(hn, 2))
    ya, oa, states = _hgrn_fwd(p, _row(lb), hn2, tag + "_hgrn")
    kv = p[:, COL_BK:COL_CQ].astype(BF16)
    if gather is None:
        (yb, sb_tot), gathered = _sb_fwd(p, kv, tag + "_sb"), None
    else:
        res = _sb_fwd(p, kv, tag + "_sb_gather", gather)
        yb, sb_tot, gathered = res[0], res[1], list(res[2:])
    og, lg = zip(*[_dil_fwd(p, gi, tag + "_dil%d" % gi) for gi in range(len(C_GROUPS))])
    yc, lse_c = _dil_merge(og, lg, tag + "_dil_merge")
    gl = p[:, COL_GATE:]
    ws = (w["w_branch_a"], w["w_branch_b"], w["w_branch_c"])
    merged = _gate_fwd((ya, yb, yc), gl, ws, tag + "_gate")
    y = _matmul(merged, w["w_out"], name=tag + "_out")
    x_new = _rms_fwd(y, mpost, jnp.zeros_like(mpost), x, F32, tag + "_post")
    return x_new, (x, h, p, hn2, ya, oa, states, yb, kv, sb_tot, yc, lse_c, gl, merged, y, mpre, mpost), gathered


def _mix_bwd(dx_new, saved, w, g_pre, g_post, m, lb, tag, exchange=None):
    x, h, p, hn2, ya, oa, states, yb, kv, sb_tot, yc, lse_c, gl, merged, y, mpre, mpost = saved
    t = x.shape[0]
    scale, gate = m[1], m[2]
    dy, dgate, dg_post = _sub_bwd_post(dx_new, y, mpost, g_post, gate, 1.0, tag)
    dmerged = _matmul(dy, w["w_out"], tb=True, out_dtype=BF16, name=tag + "_out_dx")
    dw_out = _matmul(merged, dy, ta=True, out_dtype=BF16, name=tag + "_out_dw")
    ws = (w["w_branch_a"], w["w_branch_b"], w["w_branch_c"])
    dya, dyb, dyc, dgl, dwa, dwb, dwc = _gate_bwd(dmerged, (ya, yb, yc), gl, ws, tag + "_gate_bwd")
    dqa, dfa, dia, dga, dlb, dhn = _hgrn_bwd(p, _row(lb), hn2, oa, states, dya, tag + "_hgrn_bwd")
    if exchange is None:
        (dbq, dbk, dbv), received = _sb_bwd(p, kv, sb_tot, dyb, tag + "_sb_bwd"), None
    else:
        res = _sb_bwd(p, kv, sb_tot, dyb, tag + "_sb_bwd_exchange", exchange)
        dbq, dbk, dbv, received = res[0], res[1], res[2], list(res[3:])
    dcq, dck, dcv = zip(*[_dil_bwd(p, dyc, yc, lse_c, gi, tag + "_dil%d_bwd" % gi) for gi in range(len(C_GROUPS))])
    dp = jnp.concatenate([dqa, dfa, dia, dga, dbq, dbk, dbv, *dcq, *dck, *dcv, dgl], axis=1).astype(BF16)
    dh = _matmul(dp, w["w_in"], tb=True, name=tag + "_in_dx")
    dw_in = _matmul(h, dp, ta=True, out_dtype=BF16, name=tag + "_in_dw")
    dx, dshift, dscale, dg_pre = _sub_bwd_pre(dh, x, mpre, dx_new, g_pre, scale, tag)
    dhn_v = jnp.sum(dhn, axis=(0, 1))
    dhn_v = dhn_v[:A_VDIM] + dhn_v[A_VDIM:]
    dws = dict(w_in=dw_in, w_out=dw_out, w_branch_a=dwa.astype(BF16), w_branch_b=dwb.astype(BF16),
               w_branch_c=dwc.astype(BF16))
    return dx, dws, jnp.stack([dshift, dscale, dgate]), dg_pre, dg_post, dlb[0], dhn_v, received


class _LocalWeights:
    def __init__(self, wts):
        self.wts = wts

    def first(self):
        return None

    def shard(self, l):
        return None

    def layer(self, l, gathered):
        return {k: v[l] for k, v in self.wts.items()}

    def pack(self, l, dws):
        return dws

    def last(self, packed):
        return packed


class _ShardedWeights:
    def __init__(self, shards):
        self.shards = shards

    def shard(self, l):
        return [self.shards[k][l].astype(BF16) for k in BIG_WEIGHTS]

    def first(self):
        return _all_gather_many(self.shard(0), "weights_all_gather")

    def layer(self, l, gathered):
        out = {}
        for k, got in zip(BIG_WEIGHTS, gathered):
            _, r, c = self.shards[k].shape
            out[k] = got if k in ROW_SHARDED else got.reshape(N_DEV, r, c).transpose(1, 0, 2).reshape(r, N_DEV * c)
        return out

    def pack(self, l, dws):
        out = []
        for k in BIG_WEIGHTS:
            _, r, c = self.shards[k].shape
            g = dws[k]
            out.append(g.reshape(N_DEV, r, c) if k in ROW_SHARDED else g.reshape(r, N_DEV, c).transpose(1, 0, 2))
        return out

    def last(self, packed):
        return _all_to_all_many(packed, "grads_all_to_all")

    def partial_sums(self, received):
        return {k: jnp.concatenate([rec[b] for rec in received], axis=1) for b, k in enumerate(BIG_WEIGHTS)}


def _local_step(x, target, mod, norm_g, lb_all, hnorm, supply):
    depth = mod.shape[0]
    d = x.shape[1]
    saved, wls = [], []
    gathered = supply.first()
    for l in range(depth):
        wl = supply.layer(l, gathered)
        wls.append(wl)
        x, s0 = _ffn_fwd(x, wl["ffn1_w_in"], wl["ffn1_w_out"], norm_g[l, 0], norm_g[l, 1], mod[l, 0], 0.5, "ffn1")
        nxt = supply.shard(l + 1) if l + 1 < depth else None
        x, s1, gathered = _mix_fwd(x, wl, norm_g[l, 2], norm_g[l, 3], mod[l, 1], lb_all[l], hnorm[l], "mix", nxt)
        x, s2 = _ffn_fwd(x, wl["ffn2_w_in"], wl["ffn2_w_out"], norm_g[l, 4], norm_g[l, 5], mod[l, 2], 0.5, "ffn2")
        saved.append((s0, s1, s2))
    dx, sq = _loss_head(x, target, "loss_head")
    loss = 0.5 * jnp.sum(sq) / d
    dmod, dng, dlb, dhn = [], [], [], []
    returned = [None] * depth
    pending = None
    for l in reversed(range(depth)):
        wl = wls[l]
        s0, s1, s2 = saved[l]
        dx, dwi2, dwo2, dm2, dgp2, dgq2 = _ffn_bwd(dx, s2, wl["ffn2_w_in"], wl["ffn2_w_out"], norm_g[l, 4],
                                                   norm_g[l, 5], mod[l, 2], 0.5, "ffn2")
        fuse = pending is not None and isinstance(supply, _ShardedWeights)
        dx, dwm, dm1, dgp1, dgq1, dlb_l, dhn_l, received = _mix_bwd(
            dx, s1, wl, norm_g[l, 2], norm_g[l, 3], mod[l, 1], lb_all[l], "mix", pending if fuse else None)
        if pending is not None:
            returned[l + 1] = received if fuse else pending
        dx, dwi1, dwo1, dm0, dgp0, dgq0 = _ffn_bwd(dx, s0, wl["ffn1_w_in"], wl["ffn1_w_out"], norm_g[l, 0],
                                                   norm_g[l, 1], mod[l, 0], 0.5, "ffn1")
        dmod.append(jnp.stack([dm0, dm1, dm2]))
        dng.append(jnp.stack([dgp0, dgq0, dgp1, dgq1, dgp2, dgq2]))
        dlb.append(dlb_l)
        dhn.append(dhn_l)
        pending = supply.pack(l, dict(dwm, ffn1_w_in=dwi1, ffn1_w_out=dwo1, ffn2_w_in=dwi2, ffn2_w_out=dwo2))
    returned[0] = supply.last(pending)
    rev = lambda lst: jnp.stack(lst[::-1])
    return loss, dx, rev(dmod), rev(dng), rev(dlb), rev(dhn), returned


def _lb_all(logits):
    lb_p = jax.nn.softmax(logits.astype(F32), axis=0)
    return jnp.cumsum(lb_p, axis=0) - lb_p[0:1]


def _pad_rows(a, rows):
    return jnp.pad(a, ((0, rows - a.shape[0]), (0, 0)))


def kernel(x, c, w_ada, b_ada, norm_g, ffn1_w_in, ffn1_w_out, w_in, hgrn_lb_logits, hgrn_norm_g, w_branch_a, w_branch_b, w_branch_c, w_out, ffn2_w_in, ffn2_w_out, loss_target, m_w_ada, m_b_ada, m_norm_g, m_ffn1_w_in, m_ffn1_w_out, m_w_in, m_hgrn_lb_logits, m_hgrn_norm_g, m_w_branch_a, m_w_branch_b, m_w_branch_c, m_w_out, m_ffn2_w_in, m_ffn2_w_out, v_w_ada, v_b_ada, v_norm_g, v_ffn1_w_in, v_ffn1_w_out, v_w_in, v_hgrn_lb_logits, v_hgrn_norm_g, v_w_branch_a, v_w_branch_b, v_w_branch_c, v_w_out, v_ffn2_w_in, v_ffn2_w_out):
    weights = dict(w_ada=w_ada, b_ada=b_ada, norm_g=norm_g, ffn1_w_in=ffn1_w_in, ffn1_w_out=ffn1_w_out, w_in=w_in,
                   hgrn_lb_logits=hgrn_lb_logits, hgrn_norm_g=hgrn_norm_g, w_branch_a=w_branch_a,
                   w_branch_b=w_branch_b, w_branch_c=w_branch_c, w_out=w_out, ffn2_w_in=ffn2_w_in,
                   ffn2_w_out=ffn2_w_out)
    mom1 = dict(w_ada=m_w_ada, b_ada=m_b_ada, norm_g=m_norm_g, ffn1_w_in=m_ffn1_w_in, ffn1_w_out=m_ffn1_w_out,
                w_in=m_w_in, hgrn_lb_logits=m_hgrn_lb_logits, hgrn_norm_g=m_hgrn_norm_g, w_branch_a=m_w_branch_a,
                w_branch_b=m_w_branch_b, w_branch_c=m_w_branch_c, w_out=m_w_out, ffn2_w_in=m_ffn2_w_in,
                ffn2_w_out=m_ffn2_w_out)
    mom2 = dict(w_ada=v_w_ada, b_ada=v_b_ada, norm_g=v_norm_g, ffn1_w_in=v_ffn1_w_in, ffn1_w_out=v_ffn1_w_out,
                w_in=v_w_in, hgrn_lb_logits=v_hgrn_lb_logits, hgrn_norm_g=v_hgrn_norm_g, w_branch_a=v_w_branch_a,
                w_branch_b=v_w_branch_b, w_branch_c=v_w_branch_c, w_out=v_w_out, ffn2_w_in=v_ffn2_w_in,
                ffn2_w_out=v_ffn2_w_out)
    order = list(weights)
    depth, d, ada_cols = w_ada.shape
    nd = d // LANES
    xi, yi, ci = _my_coords()
    me = 4 * xi + 2 * yi + ci

    small = jnp.concatenate([c.reshape(nd, LANES), norm_g.reshape(depth * 6, LANES)], axis=0)
    g1 = _all_gather(small, True, False, "small_all_gather").reshape(N_DEV, small.shape[0], LANES)
    c_act = _silu(g1[:, :nd].reshape(N_DEV, d))
    norm_full = g1[:, nd:].reshape(N_DEV, depth, 6, LANES).transpose(1, 2, 0, 3).reshape(depth, 6, d)

    c_pad = _pad_rows(c_act, 16)
    mod_sh = jnp.stack([_matmul(c_pad, w_ada[l], name="ada_mod")[:N_DEV]
                        + lax.dynamic_slice_in_dim(b_ada[l], me * ada_cols, ada_cols)[None]
                        for l in range(depth)])
    g2 = _all_gather(mod_sh.reshape(-1, LANES), True, False, "mod_all_gather")
    g2 = g2.reshape(N_DEV, depth, N_DEV, ada_cols)
    mod = lax.dynamic_index_in_dim(g2, me, axis=2, keepdims=False)
    mod = mod.transpose(1, 0, 2).reshape(depth, 3, 3, d)

    supply = _ShardedWeights({k: weights[k] for k in BIG_WEIGHTS})
    lb_all, lb_vjp = jax.vjp(_lb_all, hgrn_lb_logits)

    loss, dx, dmod, dng, dlb, dhn, received = _local_step(x[0], loss_target[0], mod, norm_full, lb_all,
                                                          hgrn_norm_g, supply)
    loss = lax.psum(loss, ("x", "y", "c"))

    dhn_pad = jnp.pad(dhn.reshape(-1), (0, 8 * LANES - dhn.size))
    pieces = [dmod.reshape(-1), dng.reshape(-1), dlb.reshape(-1), dhn_pad]
    sizes = [p_.size for p_ in pieces]
    smallg = jnp.concatenate(pieces).reshape(-1, LANES)
    g3, gsum = _all_gather(smallg, True, True, "small_grads_all_gather")
    g3 = g3.reshape(N_DEV, -1)
    gsum = gsum.reshape(-1)
    dmod_all = g3[:, :sizes[0]].reshape(N_DEV, depth, 9 * d)
    o1 = sizes[0]
    grads = {}
    grads["b_ada"] = gsum[:o1].reshape(depth, 9 * d)
    dng_sum = gsum[o1:o1 + sizes[1]].reshape(depth, 6, nd, LANES)
    grads["norm_g"] = lax.dynamic_index_in_dim(dng_sum, me, axis=2, keepdims=False)
    o2 = o1 + sizes[1]
    dlb_sum = gsum[o2:o2 + sizes[2]].reshape(depth, A_QK)
    grads["hgrn_lb_logits"] = lb_vjp(dlb_sum)[0]
    o3 = o2 + sizes[2]
    grads["hgrn_norm_g"] = gsum[o3:o3 + dhn.size].reshape(depth, A_VDIM)
    dmod_mine = lax.dynamic_slice_in_dim(dmod_all, me * ada_cols, ada_cols, axis=2)
    grads["w_ada"] = jnp.stack([_matmul(c_pad, _pad_rows(dmod_mine[:, l], 16), ta=True, name="ada_dw")
                                for l in range(depth)])

    gparts = supply.partial_sums(received)

    outs = {}
    for k in order:
        w = weights[k]
        w2 = w.reshape(-1, w.shape[-1])
        gp = gparts[k] if k in gparts else grads[k].reshape((1,) + w2.shape)
        res = _adamw(w2, mom1[k].reshape(w2.shape), mom2[k].reshape(w2.shape), gp, "adamw")
        outs[k] = [r.reshape(w.shape) for r in res]
    return (loss, dx[None], *[outs[k][0] for k in order], *[outs[k][1] for k in order],
            *[outs[k][2] for k in order], *[outs[k][3] for k in order])
```

```python
import functools
import math

import jax
import jax.numpy as jnp
from jax import lax
from jax.experimental import pallas as pl
from jax.experimental.pallas import tpu as pltpu

F32 = jnp.float32
BF16 = jnp.bfloat16

A_HEADS, A_KDIM, A_VDIM, A_CHUNK = 6, 128, 64, 64
B_HEADS, HDIM = 6, 64
C_GROUPS = ((128, 1), (512, 4), (2048, 16))
C_HPG = 4
C_HEADS = C_HPG * len(C_GROUPS)
N_BRANCH = 3
EPS = 1e-6
NEG_BIG = -1e30
TINY = 1e-30
A_QK = A_HEADS * A_KDIM
A_V = A_HEADS * A_VDIM
B_W = B_HEADS * HDIM
C_W = C_HEADS * HDIM
C_OUT = C_HPG * HDIM
COL_AQ, COL_AF, COL_AI, COL_AG = 0, A_QK, 2 * A_QK, 2 * A_QK + A_V
COL_BQ = 2 * A_QK + 2 * A_V
COL_BK, COL_BV = COL_BQ + B_W, COL_BQ + 2 * B_W
COL_CQ = COL_BQ + 3 * B_W
COL_CK, COL_CV = COL_CQ + C_W, COL_CQ + 2 * C_W
COL_GATE = COL_CQ + 3 * C_W

ADAM_LR, ADAM_B1, ADAM_B2, ADAM_EPS, ADAM_WD, ADAM_STEP = 0.001, 0.9, 0.999, 1e-08, 0.01, 10

N_DEV = 8
LANES = 128
VMEM_LIMIT = 48 * 1024 * 1024
MATMUL_VMEM_BUDGET = 28 * 1024 * 1024
SUB = 16
EXP_CLAMP = 80.0
MESH = pl.DeviceIdType.MESH

BIG_WEIGHTS = ("ffn1_w_in", "ffn1_w_out", "w_in", "w_branch_a", "w_branch_b", "w_branch_c", "w_out",
               "ffn2_w_in", "ffn2_w_out")
ROW_SHARDED = ("ffn1_w_out", "w_out", "ffn2_w_out")


def _cparams(sem):
    return pltpu.CompilerParams(dimension_semantics=sem, vmem_limit_bytes=VMEM_LIMIT)


def _tile(n, cap):
    best, t = None, LANES
    while t <= min(n, cap):
        if n % t == 0:
            best = t
        t += LANES
    return best or n


def _rows(t, cap=256):
    r = cap
    while t % r:
        r //= 2
    return r


def _divisors(n):
    return [t for t in range(LANES, n + 1, LANES) if n % t == 0] or [n]


def _matmul_tiles(m, n, k, a_size, b_size, o_size):
    best, best_key = None, None
    for tm in _divisors(m):
        for tn in _divisors(n):
            for tk in _divisors(k):
                if tm > 1024 or tn > 3072 or tk > 4096:
                    continue
                cast = (tm * tk * 2 if a_size > 2 else 0) + (tk * tn * 2 if b_size > 2 else 0)
                need = 2 * (tm * tk * a_size + tk * tn * b_size + tm * tn * o_size) + 2 * tm * tn * 4 + cast
                if need > MATMUL_VMEM_BUDGET:
                    continue
                key = (tm * tn * tk, tk)
                if best_key is None or key > best_key:
                    best, best_key = (tm, tn, tk), key
    return best


def _dot(a, b):
    return jnp.dot(a, b, preferred_element_type=F32)


def _dot_nt(a, b):
    return lax.dot_general(a, b, (((1,), (1,)), ((), ())), preferred_element_type=F32)


def _dot_tn(a, b):
    return lax.dot_general(a, b, (((0,), (0,)), ((), ())), preferred_element_type=F32)


def _split3(x):
    h = x.astype(BF16)
    r = x - h.astype(F32)
    m = r.astype(BF16)
    lo = (r - m.astype(F32)).astype(BF16)
    return h, m, lo


def _ones_left(mat01, x):
    h, m, lo = _split3(x)
    return _dot(mat01, h) + _dot(mat01, m) + _dot(mat01, lo)


def _silu(x):
    return x * jax.nn.sigmoid(x)


def _dsilu(x):
    s = jax.nn.sigmoid(x)
    return s * (1.0 + x * (1.0 - s))


def _matmul(a, b, *, ta=False, tb=False, out_dtype=F32, name):
    if ta:
        kdim, m = a.shape
    else:
        m, kdim = a.shape
    n = b.shape[0] if tb else b.shape[1]
    tm, tn, tk = _matmul_tiles(m, n, kdim, a.dtype.itemsize, b.dtype.itemsize, jnp.dtype(out_dtype).itemsize)
    nk = kdim // tk
    ni, nj = m // tm, n // tn
    a_bytes, b_bytes = m * kdim * a.dtype.itemsize, kdim * n * b.dtype.itemsize
    j_outer = nk == 1 and (b_bytes + a_bytes * nj) < (a_bytes + b_bytes * ni)
    dims = (((0 if ta else 1,), (1 if tb else 0,)), ((), ()))

    def body(a_ref, b_ref, o_ref, *scratch):
        p = lax.dot_general(a_ref[...].astype(BF16), b_ref[...].astype(BF16), dims, preferred_element_type=F32)
        if nk == 1:
            o_ref[...] = p.astype(o_ref.dtype)
            return
        acc = scratch[0]
        k = pl.program_id(2)

        @pl.when(k == 0)
        def _():
            acc[...] = p

        @pl.when(k > 0)
        def _():
            acc[...] += p

        @pl.when(k == nk - 1)
        def _():
            o_ref[...] = acc[...].astype(o_ref.dtype)

    def spec(shape, pick):
        if j_outer:
            return pl.BlockSpec(shape, lambda j, i, k: pick(i, j, k))
        return pl.BlockSpec(shape, lambda i, j, k: pick(i, j, k))

    a_spec = spec((tk, tm), lambda i, j, k: (k, i)) if ta else spec((tm, tk), lambda i, j, k: (i, k))
    b_spec = spec((tn, tk), lambda i, j, k: (j, k)) if tb else spec((tk, tn), lambda i, j, k: (k, j))
    return pl.pallas_call(
        body, name=name, grid=(nj, ni, nk) if j_outer else (ni, nj, nk), in_specs=[a_spec, b_spec],
        out_specs=spec((tm, tn), lambda i, j, k: (i, j)),
        out_shape=jax.ShapeDtypeStruct((m, n), out_dtype),
        scratch_shapes=[pltpu.VMEM((tm, tn), F32)] if nk > 1 else [],
        compiler_params=_cparams(("parallel", "parallel", "arbitrary")),
    )(a, b)


def _rms_fwd(z, mcol, acol, res, out_dtype, name):
    t, d = z.shape
    tr = _rows(t)
    has_res = res is not None

    def body(*refs):
        if has_res:
            z_ref, m_ref, a_ref, r_ref, o_ref = refs
        else:
            z_ref, m_ref, a_ref, o_ref = refs
        zf = z_ref[...]
        r = lax.rsqrt(jnp.mean(zf * zf, axis=-1, keepdims=True) + EPS)
        y = zf * r * m_ref[...] + a_ref[...]
        if has_res:
            y = r_ref[...] + y
        o_ref[...] = y.astype(o_ref.dtype)

    row = pl.BlockSpec((tr, d), lambda i: (i, 0))
    col = pl.BlockSpec((1, d), lambda i: (0, 0))
    ins = [z, mcol, acol] + ([res] if has_res else [])
    return pl.pallas_call(
        body, name=name, grid=(t // tr,), in_specs=[row, col, col] + ([row] if has_res else []),
        out_specs=row, out_shape=jax.ShapeDtypeStruct((t, d), out_dtype),
        compiler_params=_cparams(("parallel",)),
    )(*ins)


def _rms_bwd(d_out, z, mcol, dres, out_dtype, name):
    t, d = z.shape
    tr = _rows(t)
    has_res = dres is not None

    def body(*refs):
        if has_res:
            d_ref, z_ref, m_ref, r_ref, o_ref, s1_ref, s2_ref = refs
        else:
            d_ref, z_ref, m_ref, o_ref, s1_ref, s2_ref = refs
        i = pl.program_id(0)
        zf = z_ref[...]
        r = lax.rsqrt(jnp.mean(zf * zf, axis=-1, keepdims=True) + EPS)
        zh = zf * r
        df = d_ref[...].astype(F32)
        dzh = df * m_ref[...]
        dz = r * (dzh - zh * jnp.mean(dzh * zh, axis=-1, keepdims=True))
        if has_res:
            dz = dz + r_ref[...]
        o_ref[...] = dz.astype(o_ref.dtype)
        s1 = jnp.sum(df * zh, axis=0, keepdims=True)
        s2 = jnp.sum(df, axis=0, keepdims=True)

        @pl.when(i == 0)
        def _():
            s1_ref[...] = s1
            s2_ref[...] = s2

        @pl.when(i > 0)
        def _():
            s1_ref[...] += s1
            s2_ref[...] += s2

    row = pl.BlockSpec((tr, d), lambda i: (i, 0))
    col = pl.BlockSpec((1, d), lambda i: (0, 0))
    ins = [d_out, z, mcol] + ([dres] if has_res else [])
    return pl.pallas_call(
        body, name=name, grid=(t // tr,), in_specs=[row, row, col] + ([row] if has_res else []),
        out_specs=[row, col, col],
        out_shape=[jax.ShapeDtypeStruct((t, d), out_dtype), jax.ShapeDtypeStruct((1, d), F32),
                   jax.ShapeDtypeStruct((1, d), F32)],
        compiler_params=_cparams(("arbitrary",)),
    )(*ins)


def _swiglu_fwd(u, name):
    t, f2 = u.shape
    f = f2 // 2
    tr = _rows(t)

    def body(u_ref, s_ref):
        a = u_ref[:, :f].astype(F32)
        b = u_ref[:, f:].astype(F32)
        s_ref[...] = (_silu(a) * b).astype(s_ref.dtype)

    return pl.pallas_call(
        body, name=name, grid=(t // tr,), in_specs=[pl.BlockSpec((tr, f2), lambda i: (i, 0))],
        out_specs=pl.BlockSpec((tr, f), lambda i: (i, 0)), out_shape=jax.ShapeDtypeStruct((t, f), BF16),
        compiler_params=_cparams(("parallel",)),
    )(u)


def _swiglu_bwd(u, ds, name):
    t, f2 = u.shape
    f = f2 // 2
    tr = _rows(t)

    def body(u_ref, ds_ref, du_ref):
        a = u_ref[:, :f].astype(F32)
        b = u_ref[:, f:].astype(F32)
        g = ds_ref[...].astype(F32)
        du_ref[:, :f] = (g * b * _dsilu(a)).astype(du_ref.dtype)
        du_ref[:, f:] = (g * _silu(a)).astype(du_ref.dtype)

    return pl.pallas_call(
        body, name=name, grid=(t // tr,),
        in_specs=[pl.BlockSpec((tr, f2), lambda i: (i, 0)), pl.BlockSpec((tr, f), lambda i: (i, 0))],
        out_specs=pl.BlockSpec((tr, f2), lambda i: (i, 0)), out_shape=jax.ShapeDtypeStruct((t, f2), BF16),
        compiler_params=_cparams(("parallel",)),
    )(u, ds)


def _loss_head(y, target, name):
    t, d = y.shape
    tr = _rows(t)

    def body(y_ref, t_ref, dy_ref, sq_ref):
        i = pl.program_id(0)
        e = y_ref[...] - t_ref[...]
        dy_ref[...] = e * (1.0 / d)
        s = jnp.sum(e * e, axis=0, keepdims=True)

        @pl.when(i == 0)
        def _():
            sq_ref[...] = s

        @pl.when(i > 0)
        def _():
            sq_ref[...] += s

    row = pl.BlockSpec((tr, d), lambda i: (i, 0))
    col = pl.BlockSpec((1, d), lambda i: (0, 0))
    return pl.pallas_call(
        body, name=name, grid=(t // tr,), in_specs=[row, row], out_specs=[row, col],
        out_shape=[jax.ShapeDtypeStruct((t, d), F32), jax.ShapeDtypeStruct((1, d), F32)],
        compiler_params=_cparams(("arbitrary",)),
    )(y, target)


def _hgrn_consts():
    c = A_CHUNK
    shift = SUB.bit_length() - 1
    r = lax.broadcasted_iota(jnp.int32, (c, c), 0)
    s = lax.broadcasted_iota(jnp.int32, (c, c), 1)
    sub_r = lax.shift_right_logical(r, shift)
    incl = s <= r
    masks = [jnp.logical_and(sub_r == i, incl) for i in range(c // SUB)]
    rev_incl = jnp.where(s >= r, 1.0, 0.0).astype(BF16)
    r2 = lax.broadcasted_iota(jnp.int32, (2 * c + 8, c), 0)
    s2 = lax.broadcasted_iota(jnp.int32, (2 * c + 8, c), 1)
    sub_start = lax.shift_left(lax.shift_right_logical(r2 - c, shift), shift)
    running = jnp.where(s2 <= r2, 1.0, 0.0)
    before = jnp.where(s2 < sub_start, 1.0, 0.0)
    stack = jnp.where(r2 < c, running, jnp.where(r2 < 2 * c, before, 1.0)).astype(BF16)
    return stack, masks, incl, rev_incl


def _hgrn_gates(q_raw, f_raw, lbv, stack):
    sg = jax.nn.sigmoid(f_raw)
    sgn = jax.nn.sigmoid(-f_raw)
    f = lbv + (1.0 - lbv) * sg
    logf = jnp.log(jnp.maximum(f, TINY))
    return dict(sg=sg, sgn=sgn, f=f, k=(1.0 - lbv) * sgn, q=_silu(q_raw), bb=_ones_left(stack, logf))


def _hgrn_chunk(q_raw, f_raw, lbv, stack):
    return _hgrn_decays(_hgrn_gates(q_raw, f_raw, lbv, stack))


def _hgrn_decays(gates):
    c = A_CHUNK
    sg, sgn, f, k, q, bb = (gates[n] for n in ("sg", "sgn", "f", "k", "q", "bb"))
    b = bb[:c]
    bsrow = bb[c:2 * c]
    b_end = bb[2 * c:2 * c + 1]
    e_sub = jnp.exp(b - bsrow)
    e_b = jnp.exp(b)
    e_end = jnp.exp(b_end - b)
    qs = q * e_sub
    q_in = q * e_b
    kend = k * e_end
    kfac = [jnp.exp(jnp.minimum(bsrow[i * SUB:i * SUB + 1] - b, EXP_CLAMP)) for i in range(c // SUB)]
    return dict(sg=sg, sgn=sgn, f=f, k=k, q=q, b=b, b_end=b_end, e_sub=e_sub, e_b=e_b, e_end=e_end,
                qs=qs, q_in=q_in, kend=kend, kfac=kfac)


def _hgrn_scores(ch, masks):
    qs_b = ch["qs"].astype(BF16)
    a = None
    for i, mk in enumerate(masks):
        ki = (ch["k"] * ch["kfac"][i]).astype(BF16)
        part = jnp.where(mk, _dot_nt(qs_b, ki), 0.0)
        a = part if a is None else a + part
    return a


def _hgrn_fwd(p, lb, hn2, name):
    t = p.shape[0]
    tb = _rows(t)
    nt = t // tb
    nc = tb // A_CHUNK
    c = A_CHUNK

    def body(q_ref, f_ref, i_ref, g_ref, lb_ref, hn_ref, y_ref, o_ref, st_ref, s_scr):
        j = pl.program_id(1)

        @pl.when(j == 0)
        def _():
            s_scr[...] = jnp.zeros_like(s_scr)

        stack, masks, _, _ = _hgrn_consts()
        units = [(ci, hh) for ci in range(nc) for hh in range(2)]
        lsl = [slice(A_KDIM * hh, A_KDIM * (hh + 1)) for hh in range(2)]
        hsl = [slice(A_VDIM * hh, A_VDIM * (hh + 1)) for hh in range(2)]
        rows = [pl.ds(ci * c, c) for ci in range(nc)]
        gates = {u: _hgrn_gates(q_ref[rows[u[0]], lsl[u[1]]], f_ref[rows[u[0]], lsl[u[1]]], lb_ref[:, lsl[u[1]]], stack)
                 for u in units}
        ch = {u: _hgrn_decays(gates[u]) for u in units}
        v = {u: i_ref[rows[u[0]], hsl[u[1]]].astype(BF16) for u in units}
        a = {u: _hgrn_scores(ch[u], masks).astype(BF16) for u in units}
        grow = {u: _dot_tn(v[u], ch[u]["kend"].astype(BF16)) for u in units}
        states = [s_scr[0], s_scr[1]]
        entering = {}
        for ci, hh in units:
            entering[ci, hh] = states[hh]
            st_ref[hh, ci] = states[hh]
            states[hh] = states[hh] * jnp.exp(ch[ci, hh]["b_end"]) + grow[ci, hh]
        s_scr[0] = states[0]
        s_scr[1] = states[1]
        for u in units:
            o_ref[rows[u[0]], hsl[u[1]]] = (_dot_nt(ch[u]["q_in"].astype(BF16), entering[u].astype(BF16))
                                            + _dot(a[u], v[u]))
        for hh in range(2):
            hsl = slice(A_VDIM * hh, A_VDIM * (hh + 1))
            o = o_ref[:, hsl]
            r = lax.rsqrt(jnp.mean(o * o, axis=-1, keepdims=True) + EPS)
            y_ref[:, hsl] = (o * r * hn_ref[:, hsl] * _silu(g_ref[:, hsl])).astype(y_ref.dtype)

    w2 = 2 * A_KDIM
    return pl.pallas_call(
        body, name=name, grid=(A_HEADS // 2, nt),
        in_specs=[pl.BlockSpec((tb, w2), lambda h, j: (j, COL_AQ // w2 + h)),
                  pl.BlockSpec((tb, w2), lambda h, j: (j, COL_AF // w2 + h)),
                  pl.BlockSpec((tb, LANES), lambda h, j: (j, COL_AI // LANES + h)),
                  pl.BlockSpec((tb, LANES), lambda h, j: (j, COL_AG // LANES + h)),
                  pl.BlockSpec((1, w2), lambda h, j: (0, h)),
                  pl.BlockSpec((1, LANES), lambda h, j: (0, 0))],
        out_specs=[pl.BlockSpec((tb, LANES), lambda h, j: (j, h)),
                   pl.BlockSpec((tb, LANES), lambda h, j: (j, h)),
                   pl.BlockSpec((2, nc, A_VDIM, A_KDIM), lambda h, j: (h, j, 0, 0))],
        out_shape=[jax.ShapeDtypeStruct((t, A_V), BF16), jax.ShapeDtypeStruct((t, A_V), F32),
                   jax.ShapeDtypeStruct((A_HEADS, t // c, A_VDIM, A_KDIM), F32)],
        scratch_shapes=[pltpu.VMEM((2, A_VDIM, A_KDIM), F32)],
        compiler_params=_cparams(("parallel", "arbitrary")),
    )(p, p, p, p, lb, hn2)


def _hgrn_bwd(p, lb, hn2, o_raw, states, dya, name):
    t = p.shape[0]
    tb = _rows(t)
    nt = t // tb
    nc = tb // A_CHUNK
    c = A_CHUNK

    def body(q_ref, f_ref, i_ref, g_ref, lb_ref, hn_ref, o_ref, st_ref, dy_ref,
             dq_ref, df_ref, di_ref, dg_ref, dlb_ref, dhn_ref, ds_scr, do_scr):
        j = pl.program_id(1)

        @pl.when(j == 0)
        def _():
            ds_scr[...] = jnp.zeros_like(ds_scr)
            dlb_ref[...] = jnp.zeros_like(dlb_ref)
            dhn_ref[...] = jnp.zeros_like(dhn_ref)

        stack, masks, incl, rev_incl = _hgrn_consts()
        for hh in range(2):
            hsl = slice(A_VDIM * hh, A_VDIM * (hh + 1))
            o = o_ref[:, hsl]
            g = g_ref[:, hsl]
            dy = dy_ref[:, hsl].astype(F32)
            hn = hn_ref[:, hsl]
            r = lax.rsqrt(jnp.mean(o * o, axis=-1, keepdims=True) + EPS)
            oh = o * r
            sgate = _silu(g)
            dg_ref[:, hsl] = dy * oh * hn * _dsilu(g)
            dhn_ref[0, :, hsl] += jnp.sum(dy * oh * sgate, axis=0, keepdims=True)
            doh = dy * hn * sgate
            do_scr[:, hsl] = r * (doh - oh * jnp.mean(doh * oh, axis=-1, keepdims=True))

        units = [(ci, hh) for ci in reversed(range(nc)) for hh in range(2)]
        lsl = [slice(A_KDIM * hh, A_KDIM * (hh + 1)) for hh in range(2)]
        hsl = [slice(A_VDIM * hh, A_VDIM * (hh + 1)) for hh in range(2)]
        rows = [pl.ds(ci * c, c) for ci in range(nc)]
        q_raw = {u: q_ref[rows[u[0]], lsl[u[1]]] for u in units}
        gates = {u: _hgrn_gates(q_raw[u], f_ref[rows[u[0]], lsl[u[1]]], lb_ref[:, lsl[u[1]]], stack) for u in units}
        ch = {u: _hgrn_decays(gates[u]) for u in units}
        v = {u: i_ref[rows[u[0]], hsl[u[1]]].astype(BF16) for u in units}
        do_b = {u: do_scr[rows[u[0]], hsl[u[1]]].astype(BF16) for u in units}
        st = {u: st_ref[u[1], u[0]] for u in units}
        qs_b = {u: ch[u]["qs"].astype(BF16) for u in units}
        a_b = {u: _hgrn_scores(ch[u], masks).astype(BF16) for u in units}
        da = {u: jnp.where(incl, _dot_nt(do_b[u], v[u]), 0.0) for u in units}
        dq_x = {u: _dot(do_b[u], st[u].astype(BF16)) for u in units}
        grow = {u: _dot_tn(do_b[u], ch[u]["q_in"].astype(BF16)) for u in units}
        dstates = [ds_scr[0], ds_scr[1]]
        leaving = {}
        for ci, hh in units:
            leaving[ci, hh] = dstates[hh]
            dstates[hh] = dstates[hh] * jnp.exp(ch[ci, hh]["b_end"]) + grow[ci, hh]
        for hh in range(2):
            ds_scr[hh] = dstates[hh]
        dst_b = {u: leaving[u].astype(BF16) for u in units}
        dv = {u: _dot_tn(a_b[u], do_b[u]) + _dot_nt(ch[u]["kend"].astype(BF16), dst_b[u]) for u in units}
        dk_x = {u: _dot(v[u], dst_b[u]) for u in units}
        dlb_acc = [jnp.zeros((1, A_KDIM), F32), jnp.zeros((1, A_KDIM), F32)]
        for u in units:
            ci, hh = u
            cu = ch[u]
            lbv = lb_ref[:, lsl[hh]]
            dq_i = None
            dk_i = None
            kdk_i = None
            for i, mk in enumerate(masks):
                dam = jnp.where(mk, da[u], 0.0).astype(BF16)
                ki = (cu["k"] * cu["kfac"][i]).astype(BF16)
                pq = _dot(dam, ki)
                pk = _dot_tn(dam, qs_b[u])
                dq_i = pq if dq_i is None else dq_i + pq
                dk_i = cu["kfac"][i] * pk if dk_i is None else dk_i + cu["kfac"][i] * pk
                kdk_i = ki.astype(F32) * pk if kdk_i is None else kdk_i + ki.astype(F32) * pk
            dq = cu["e_sub"] * dq_i + cu["e_b"] * dq_x[u]
            dk = dk_i + cu["e_end"] * dk_x[u]
            kx = cu["kend"] * dk_x[u]
            db = (qs_b[u].astype(F32) * dq_i + cu["q_in"] * dq_x[u]) - (kdk_i + kx)
            later = (jnp.exp(cu["b_end"]) * jnp.sum(leaving[u] * st[u], axis=0, keepdims=True)
                     + jnp.sum(kx, axis=0, keepdims=True))
            dlogf = later + _ones_left(rev_incl, db)
            dfv = jnp.where(cu["f"] > TINY, dlogf / cu["f"], 0.0)
            dq_ref[rows[ci], lsl[hh]] = dq * _dsilu(q_raw[u])
            df_ref[rows[ci], lsl[hh]] = (1.0 - lbv) * cu["sg"] * cu["sgn"] * (dfv - dk)
            dlb_acc[hh] = dlb_acc[hh] + jnp.sum(dfv * (1.0 - cu["sg"]) - dk * cu["sgn"], axis=0, keepdims=True)
            di_ref[rows[ci], hsl[hh]] = dv[u]
        for hh in range(2):
            dlb_ref[:, A_KDIM * hh:A_KDIM * (hh + 1)] += dlb_acc[hh]

    w2 = 2 * A_KDIM
    rev = lambda j: nt - 1 - j
    return pl.pallas_call(
        body, name=name, grid=(A_HEADS // 2, nt),
        in_specs=[pl.BlockSpec((tb, w2), lambda h, j: (rev(j), COL_AQ // w2 + h)),
                  pl.BlockSpec((tb, w2), lambda h, j: (rev(j), COL_AF // w2 + h)),
                  pl.BlockSpec((tb, LANES), lambda h, j: (rev(j), COL_AI // LANES + h)),
                  pl.BlockSpec((tb, LANES), lambda h, j: (rev(j), COL_AG // LANES + h)),
                  pl.BlockSpec((1, w2), lambda h, j: (0, h)),
                  pl.BlockSpec((1, LANES), lambda h, j: (0, 0)),
                  pl.BlockSpec((tb, LANES), lambda h, j: (rev(j), h)),
                  pl.BlockSpec((2, nc, A_VDIM, A_KDIM), lambda h, j: (h, rev(j), 0, 0)),
                  pl.BlockSpec((tb, LANES), lambda h, j: (rev(j), h))],
        out_specs=[pl.BlockSpec((tb, w2), lambda h, j: (rev(j), h)),
                   pl.BlockSpec((tb, w2), lambda h, j: (rev(j), h)),
                   pl.BlockSpec((tb, LANES), lambda h, j: (rev(j), h)),
                   pl.BlockSpec((tb, LANES), lambda h, j: (rev(j), h)),
                   pl.BlockSpec((1, w2), lambda h, j: (0, h)),
                   pl.BlockSpec((1, 1, LANES), lambda h, j: (h, 0, 0))],
        out_shape=[jax.ShapeDtypeStruct((t, A_QK), F32), jax.ShapeDtypeStruct((t, A_QK), F32),
                   jax.ShapeDtypeStruct((t, A_V), F32), jax.ShapeDtypeStruct((t, A_V), F32),
                   jax.ShapeDtypeStruct((1, A_QK), F32), jax.ShapeDtypeStruct((A_HEADS // 2, 1, LANES), F32)],
        scratch_shapes=[pltpu.VMEM((2, A_VDIM, A_KDIM), F32), pltpu.VMEM((tb, LANES), F32)],
        compiler_params=_cparams(("parallel", "arbitrary")),
    )(p, p, p, p, lb, hn2, o_raw, states, dya)


BLK = 128
SCALE = HDIM ** -0.5
SB_CHUNK = 4


def _softplus(z):
    return jnp.maximum(z, 0.0) + jnp.log(1.0 + jnp.exp(-jnp.abs(z)))


def _split2(x):
    hi = x.astype(BF16)
    return hi, (x - hi.astype(F32)).astype(BF16)


def _sb_sum_matrix(keep, with_total=False):
    width = 2 * BLK if with_total else BLK
    sp = lax.broadcasted_iota(jnp.int32, (2 * BLK, width), 0) & (BLK - 1)
    s = lax.broadcasted_iota(jnp.int32, (2 * BLK, width), 1)
    return jnp.where(jnp.logical_or(s >= BLK, keep(sp, s)), 1.0, 0.0).astype(BF16)


def _lanes(col):
    return jnp.broadcast_to(col, (BLK, BLK))


def _sb_fwd(p, kv, name, gather=None):
    t = p.shape[0]
    nq = t // BLK
    nh = B_HEADS // 2
    cw = SB_CHUNK * BLK
    fused = gather is not None
    n = len(gather) if fused else 0

    def body(*refs):
        q_ref, kb, vb = refs[:3]
        o_ref, tot_ref = refs[3 + n:5 + n]
        zbuf, stage, sbuf, abuf = refs[5 + 2 * n:9 + 2 * n]
        hp = pl.program_id(0)
        qi = pl.program_id(1)
        if fused:
            g = _Many(_Gather, refs[3:3 + n], refs[5 + n:5 + 2 * n], *refs[9 + 2 * n:])
            pl.when(jnp.logical_and(hp == 0, qi == 0))(g.start)
            pl.when(jnp.logical_and(hp == nh - 1, qi == 0))(g.forward)

        @pl.when(qi == 0)
        def _():
            abuf[...] = jnp.zeros_like(abuf)

        row = lax.broadcasted_iota(jnp.int32, (BLK, BLK), 0)
        col = lax.broadcasted_iota(jnp.int32, (BLK, BLK), 1)
        sums = _sb_sum_matrix(lambda sp, s: sp >= s, True)
        hsl = [slice(HDIM * h, HDIM * (h + 1)) for h in range(2)]
        nchunk = qi // SB_CHUNK + 1
        for h in range(2):
            zbuf[h] = _dot_nt((q_ref[:, hsl[h]] * SCALE).astype(BF16), kb[:, hsl[h]])

        col_minus_row = col - row

        def causal(j):
            return col_minus_row < (qi - j) * BLK

        def l_pass(c, carry):
            for b in range(SB_CHUNK):
                j = c * SB_CHUNK + b
                off = pl.multiple_of(j * BLK, BLK)
                mask = causal(j)
                for h in range(2):
                    lm = jnp.where(mask, -_softplus(zbuf[h, :, pl.ds(off, BLK)]), 0.0)
                    hi, lo = _split2(lm)
                    stage[h, pl.ds(off, BLK), :BLK] = hi
                    stage[h, pl.ds(off, BLK), BLK:] = lo
            return carry

        lax.fori_loop(0, nchunk, l_pass, 0)

        def sum_pass(c, carry):
            rows = pl.ds(pl.multiple_of(c * cw, cw), cw)
            for h in range(2):
                sbuf[h, rows, :] = _dot(stage[h, rows, :], sums)
            return carry

        lax.fori_loop(0, nchunk, sum_pass, 0)

        def a_pass(it, carry):
            c = nchunk - 1 - it
            runs = list(carry)
            for b in reversed(range(SB_CHUNK)):
                j = c * SB_CHUNK + b
                off = pl.multiple_of(j * BLK, BLK)
                mask = causal(j)
                for h in range(2):
                    s = sbuf[h, pl.ds(off, BLK), :BLK]
                    a = jnp.where(mask, jnp.exp(zbuf[h, :, pl.ds(off, BLK)] + s + runs[h]), 0.0)
                    abuf[h, :, pl.ds(off, BLK)] = a.astype(BF16)
                    runs[h] = runs[h] + sbuf[h, pl.ds(off, BLK), BLK:]
            return tuple(runs)

        zero = jnp.zeros((BLK, BLK), F32)
        runs = lax.fori_loop(0, nchunk, a_pass, (zero, zero))
        for h in range(2):
            tot_ref[:, hsl[h]] = runs[h][:, :HDIM]
            o_ref[:, hsl[h]] = _dot(abuf[h], vb[:, hsl[h]])
        if fused:
            pl.when(jnp.logical_and(hp == nh - 1, qi == nq - 1))(g.finish)

    out_blk = pl.BlockSpec((BLK, LANES), lambda h, i: (i, h))
    hbm = pl.BlockSpec(memory_space=pl.ANY)
    in_specs = [pl.BlockSpec((BLK, LANES), lambda h, i: (i, COL_BQ // LANES + h)),
                pl.BlockSpec((t, LANES), lambda h, i: (0, h)),
                pl.BlockSpec((t, LANES), lambda h, i: (0, B_W // LANES + h))]
    out_shape = [jax.ShapeDtypeStruct((t, B_W), F32)] * 2
    scratch = [pltpu.VMEM((2, BLK, t), F32), pltpu.VMEM((2, t, 2 * BLK), BF16),
               pltpu.VMEM((2, t, 2 * BLK), F32), pltpu.VMEM((2, BLK, t), BF16)]
    if fused:
        out_shape = out_shape + _gathered_shapes(gather)
    return pl.pallas_call(
        body, name=name, grid=(nh, nq),
        in_specs=in_specs + [hbm] * n,
        out_specs=[out_blk, out_blk] + [hbm] * n,
        out_shape=out_shape,
        scratch_shapes=scratch + (_comm_sems(n) if fused else []),
        compiler_params=_cparams(("arbitrary", "arbitrary")),
    )(p, kv, kv, *(gather if fused else []))


def _sb_bwd(p, kv, tot, do, name, exchange=None):
    t = p.shape[0]
    nq = t // BLK
    nh = B_HEADS // 2
    cw = SB_CHUNK * BLK
    fused = exchange is not None
    n = len(exchange) if fused else 0

    def body(*refs):
        q_ref, kb, vb, tot_ref, do_ref = refs[:5]
        dq_ref, dk_ref, dv_ref = refs[5 + n:8 + n]
        zbuf, dabuf, lbuf, stage, sbuf, abuf, dzbuf, dkt, dvt = refs[8 + 2 * n:17 + 2 * n]
        hp = pl.program_id(0)
        qi = pl.program_id(1)
        if fused:
            ex = _Many(_Exchange, refs[5:5 + n], refs[8 + n:8 + 2 * n], *refs[17 + 2 * n:])
            pl.when(jnp.logical_and(hp == 0, qi == 0))(ex.start)

        @pl.when(qi == 0)
        def _():
            dkt[...] = jnp.zeros_like(dkt)
            dvt[...] = jnp.zeros_like(dvt)
            dzbuf[...] = jnp.zeros_like(dzbuf)
            abuf[...] = jnp.zeros_like(abuf)

        row = lax.broadcasted_iota(jnp.int32, (BLK, BLK), 0)
        col = lax.broadcasted_iota(jnp.int32, (BLK, BLK), 1)
        sums = _sb_sum_matrix(lambda sp, s: sp <= s)
        hsl = [slice(HDIM * h, HDIM * (h + 1)) for h in range(2)]
        dob = [do_ref[:, hsl[h]].astype(BF16) for h in range(2)]
        total =[jnp.concatenate([tot_ref[:, hsl[h]], tot_ref[:, hsl[h]]], axis=1) for h in range(2)]
        nchunk = qi // SB_CHUNK + 1
        for h in range(2):
            zbuf[h] = _dot_nt((q_ref[:, hsl[h]] * SCALE).astype(BF16), kb[:, hsl[h]])
            dabuf[h] = _dot_nt(dob[h], vb[:, hsl[h]])

        col_minus_row = col - row

        def causal(j):
            return col_minus_row < (qi - j) * BLK

        def blocks(c):
            for b in range(SB_CHUNK):
                j = c * SB_CHUNK + b
                yield j, pl.ds(pl.multiple_of(j * BLK, BLK), BLK)

        def l_pass(c, carry):
            for j, blk_ in blocks(c):
                mask = causal(j)
                for h in range(2):
                    lm = jnp.where(mask, -_softplus(zbuf[h, :, blk_]), 0.0)
                    lbuf[h, :, blk_] = lm
                    hi, lo = _split2(lm)
                    stage[h, blk_, :BLK] = hi
                    stage[h, blk_, BLK:] = lo
            return carry

        lax.fori_loop(0, nchunk, l_pass, 0)

        def sum_pass():
            def run_(c, carry):
                rows = pl.ds(pl.multiple_of(c * cw, cw), cw)
                for h in range(2):
                    sbuf[h, rows, :] = _dot(stage[h, rows, :], sums)
                return carry
            lax.fori_loop(0, nchunk, run_, 0)

        sum_pass()

        def g_pass(c, carry):
            runs = list(carry)
            for j, blk_ in blocks(c):
                mask = causal(j)
                for h in range(2):
                    upto = sbuf[h, blk_, :]
                    log_a = zbuf[h, :, blk_] + lbuf[h, :, blk_] + (total[h] - runs[h] - upto)
                    a = jnp.where(mask, jnp.exp(log_a), 0.0)
                    abuf[h, :, blk_] = a.astype(BF16)
                    g = a * dabuf[h, :, blk_]
                    dabuf[h, :, blk_] = g
                    hi, lo = _split2(g)
                    stage[h, blk_, :BLK] = hi
                    stage[h, blk_, BLK:] = lo
                    runs[h] = runs[h] + _lanes(upto[:, BLK - 1:BLK])
            return tuple(runs)

        zero = jnp.zeros((BLK, BLK), F32)
        lax.fori_loop(0, nchunk, g_pass, (zero, zero))
        sum_pass()

        def dz_pass(c, carry):
            runs = list(carry)
            for j, blk_ in blocks(c):
                mask = causal(j)
                for h in range(2):
                    lm = lbuf[h, :, blk_]
                    g = dabuf[h, :, blk_]
                    upto = sbuf[h, blk_, :]
                    before = runs[h] + upto - g
                    dz = jnp.where(mask, g * jnp.exp(lm) - jnp.exp(zbuf[h, :, blk_] + lm) * before, 0.0)
                    dzbuf[h, :, blk_] = (dz * SCALE).astype(BF16)
                    runs[h] = runs[h] + _lanes(upto[:, BLK - 1:BLK])
            return tuple(runs)

        lax.fori_loop(0, nchunk, dz_pass, (zero, zero))
        for h in range(2):
            dq_ref[:, hsl[h]] = _dot(dzbuf[h], kb[:, hsl[h]])
        q_t = q_ref[...].T.astype(BF16)
        do_t = do_ref[...].T.astype(BF16)
        for h in range(2):
            dkt[hsl[h], :] += _dot(q_t[hsl[h], :], dzbuf[h])
            dvt[hsl[h], :] += _dot(do_t[hsl[h], :], abuf[h])

        @pl.when(qi == nq - 1)
        def _():
            dk_ref[...] = dkt[...].T
            dv_ref[...] = dvt[...].T

        if fused:
            pl.when(jnp.logical_and(hp == nh - 1, qi == nq - 1))(ex.finish)

    blk = lambda h, i: (i, h)
    whole = lambda h, i: (0, h)
    hbm = pl.BlockSpec(memory_space=pl.ANY)
    in_specs = [pl.BlockSpec((BLK, LANES), lambda h, i: (i, COL_BQ // LANES + h)),
                pl.BlockSpec((t, LANES), lambda h, i: (0, h)),
                pl.BlockSpec((t, LANES), lambda h, i: (0, B_W // LANES + h)),
                pl.BlockSpec((BLK, LANES), blk), pl.BlockSpec((BLK, LANES), blk)]
    out_specs = [pl.BlockSpec((BLK, LANES), blk), pl.BlockSpec((t, LANES), whole), pl.BlockSpec((t, LANES), whole)]
    out_shape = [jax.ShapeDtypeStruct((t, B_W), F32)] * 3
    scratch = [pltpu.VMEM((2, BLK, t), F32), pltpu.VMEM((2, BLK, t), F32), pltpu.VMEM((2, BLK, t), F32),
               pltpu.VMEM((2, t, 2 * BLK), BF16), pltpu.VMEM((2, t, BLK), F32), pltpu.VMEM((2, BLK, t), BF16),
               pltpu.VMEM((2, BLK, t), BF16), pltpu.VMEM((LANES, t), F32), pltpu.VMEM((LANES, t), F32)]
    if fused:
        out_shape = out_shape + [jax.ShapeDtypeStruct(e.shape, e.dtype) for e in exchange]
    return pl.pallas_call(
        body, name=name, grid=(nh, nq),
        in_specs=in_specs + [hbm] * n,
        out_specs=out_specs + [hbm] * n,
        out_shape=out_shape,
        scratch_shapes=scratch + (_comm_sems(n) if fused else []),
        compiler_params=_cparams(("arbitrary", "arbitrary")),
    )(p, kv, kv, tot, do, *(exchange if fused else []))


def _alibi_slopes(n):
    def pow2(m):
        start = 2.0 ** (-8.0 / m)
        return [start ** (i + 1) for i in range(m)]
    if math.log2(n).is_integer():
        s = pow2(n)
    else:
        c = 2 ** int(math.floor(math.log2(n)))
        s = pow2(c) + pow2(2 * c)[0::2][: n - c]
    return sorted(s, reverse=True)


def _dil_scores(qh, kh, sl, prev, exists=None):
    row = lax.broadcasted_iota(jnp.int32, (BLK, BLK), 0)
    col = lax.broadcasted_iota(jnp.int32, (BLK, BLK), 1)
    dist = row - col + (BLK if prev else 0)
    if prev:
        valid = (col - row) >= jnp.where(exists, 0, 2 * BLK)
    else:
        valid = col <= row
    s = _dot_nt(qh, kh) - sl * dist.astype(F32)
    return s, valid


DIL_RESIDUES = 2


def _dil_rows(rho, r):
    return pl.ds(rho, BLK, stride=r) if r > 1 else pl.ds(0, BLK)


def _dil_fwd(p, gi, name):
    t = p.shape[0]
    _, r = C_GROUPS[gi]
    sbr = BLK * r
    nsb = t // sbr
    slope_cols = _slope_cols(gi)

    def body(q_ref, kc_ref, kp_ref, vc_ref, vp_ref, sl_ref, o_ref, lse_ref):
        i = pl.program_id(1)

        hsl = [slice(HDIM * h, HDIM * (h + 1)) for h in range(2)]
        sl = [sl_ref[:, HDIM * h:HDIM * h + 1] for h in range(2)]
        per_trip = min(r, DIL_RESIDUES)

        def residues(it, carry):
            rows = [_dil_rows(it * per_trip + dr, r) for dr in range(per_trip)]
            units = [(dr, h) for dr in range(per_trip) for h in range(2)]
            blocks = {dr: [ref[rows[dr], :] for ref in (q_ref, kc_ref, kp_ref, vc_ref, vp_ref)] for dr in range(per_trip)}
            qh = {u: (blocks[u[0]][0][:, hsl[u[1]]] * SCALE).astype(BF16) for u in units}
            sc = {u: _dil_scores(qh[u], blocks[u[0]][1][:, hsl[u[1]]].astype(BF16), sl[u[1]], False) for u in units}
            sp = {u: _dil_scores(qh[u], blocks[u[0]][2][:, hsl[u[1]]].astype(BF16), sl[u[1]], True, i > 0) for u in units}
            pc, pp, den, lse = {}, {}, {}, {}
            for u in units:
                s_c = jnp.where(sc[u][1], sc[u][0], NEG_BIG)
                s_p = jnp.where(sp[u][1], sp[u][0], NEG_BIG)
                m = jnp.maximum(jnp.max(s_c, axis=1, keepdims=True), jnp.max(s_p, axis=1, keepdims=True))
                pc[u] = jnp.exp(s_c - m)
                pp[u] = jnp.exp(s_p - m)
                den[u] = jnp.sum(pc[u], axis=1, keepdims=True) + jnp.sum(pp[u], axis=1, keepdims=True)
                lse[u] = jnp.broadcast_to(m + jnp.log(den[u]), (BLK, HDIM))
            o = {u: (_dot(pc[u].astype(BF16), blocks[u[0]][3][:, hsl[u[1]]].astype(BF16))
                     + _dot(pp[u].astype(BF16), blocks[u[0]][4][:, hsl[u[1]]].astype(BF16))) / den[u] for u in units}
            for dr in range(per_trip):
                o_ref[rows[dr], :] = jnp.concatenate([o[dr, 0], o[dr, 1]], axis=1)
                lse_ref[rows[dr], :] = jnp.concatenate([lse[dr, 0], lse[dr, 1]], axis=1)
            return carry

        lax.fori_loop(0, r // per_trip, residues, 0)

    def at(col0, pick):
        return pl.BlockSpec((sbr, LANES), lambda c, i: (pick(i), col0 // LANES + c))

    cur = lambda i: i
    prv = lambda i: jnp.maximum(i - 1, 0)
    cq, ck, cv = COL_CQ + gi * C_OUT, COL_CK + gi * C_OUT, COL_CV + gi * C_OUT
    out = pl.BlockSpec((sbr, LANES), lambda c, i: (i, c))
    return pl.pallas_call(
        body, name=name, grid=(C_OUT // LANES, nsb),
        in_specs=[at(cq, cur), at(ck, cur), at(ck, prv), at(cv, cur), at(cv, prv),
                  pl.BlockSpec((1, LANES), lambda c, i: (0, c))],
        out_specs=[out, out], out_shape=[jax.ShapeDtypeStruct((t, C_OUT), F32)] * 2,
        compiler_params=_cparams(("parallel", "parallel")),
    )(p, p, p, p, p, slope_cols)


def _dil_bwd(p, do, o, lse, gi, name):
    t = p.shape[0]
    _, r = C_GROUPS[gi]
    sbr = BLK * r
    nsb = t // sbr
    slope_cols = _slope_cols(gi)

    def body(q_ref, qn_ref, kc_ref, kp_ref, vc_ref, vp_ref, do_ref, don_ref, o_ref, on_ref, l_ref, ln_ref, sl_ref,
             dq_ref, dk_ref, dv_ref):
        i = pl.program_id(1)
        has_prev = i > 0
        has_next = i < nsb - 1

        hsl = [slice(HDIM * h, HDIM * (h + 1)) for h in range(2)]
        sl = [sl_ref[:, HDIM * h:HDIM * h + 1] for h in range(2)]
        per_trip = min(r, DIL_RESIDUES)
        in_refs = (q_ref, qn_ref, kc_ref, kp_ref, vc_ref, vp_ref, do_ref, don_ref, o_ref, on_ref, l_ref, ln_ref)

        def residues(it, carry):
            rows = [_dil_rows(it * per_trip + dr, r) for dr in range(per_trip)]
            units = [(dr, h) for dr in range(per_trip) for h in range(2)]
            blocks = {dr: [ref[rows[dr], :] for ref in in_refs] for dr in range(per_trip)}
            part = lambda u, k: blocks[u[0]][k][:, hsl[u[1]]]
            qb = {u: part(u, 0).astype(BF16) for u in units}
            qnb = {u: part(u, 1).astype(BF16) for u in units}
            qh = {u: (part(u, 0) * SCALE).astype(BF16) for u in units}
            qnh = {u: (part(u, 1) * SCALE).astype(BF16) for u in units}
            kc = {u: part(u, 2).astype(BF16) for u in units}
            kp = {u: part(u, 3).astype(BF16) for u in units}
            vc = {u: part(u, 4).astype(BF16) for u in units}
            vp = {u: part(u, 5).astype(BF16) for u in units}
            dob = {u: part(u, 6).astype(BF16) for u in units}
            donb = {u: part(u, 7).astype(BF16) for u in units}
            delta = {u: jnp.sum(part(u, 6) * part(u, 8), axis=1, keepdims=True) for u in units}
            deltan = {u: jnp.sum(part(u, 7) * part(u, 9), axis=1, keepdims=True) for u in units}
            lse_c = {u: part(u, 10)[:, :1] for u in units}
            lse_n = {u: part(u, 11)[:, :1] for u in units}
            s_cc = {u: _dil_scores(qh[u], kc[u], sl[u[1]], False) for u in units}
            s_cp = {u: _dil_scores(qh[u], kp[u], sl[u[1]], True, has_prev) for u in units}
            s_nc = {u: _dil_scores(qnh[u], kc[u], sl[u[1]], True, has_next) for u in units}
            da_cc = {u: _dot_nt(dob[u], vc[u]) for u in units}
            da_cp = {u: _dot_nt(dob[u], vp[u]) for u in units}
            da_nc = {u: _dot_nt(donb[u], vc[u]) for u in units}

            def prob(s_ok, lse_col):
                s, ok = s_ok
                return jnp.where(ok, jnp.exp(jnp.where(ok, s, NEG_BIG) - lse_col), 0.0)

            p_cc = {u: prob(s_cc[u], lse_c[u]) for u in units}
            p_cp = {u: prob(s_cp[u], lse_c[u]) for u in units}
            p_nc = {u: prob(s_nc[u], lse_n[u]) for u in units}
            ds_cc = {u: (p_cc[u] * (da_cc[u] - delta[u]) * SCALE).astype(BF16) for u in units}
            ds_cp = {u: (p_cp[u] * (da_cp[u] - delta[u]) * SCALE).astype(BF16) for u in units}
            ds_nc = {u: (p_nc[u] * (da_nc[u] - deltan[u]) * SCALE).astype(BF16) for u in units}
            dq = {u: _dot(ds_cc[u], kc[u]) + _dot(ds_cp[u], kp[u]) for u in units}
            dk = {u: _dot_tn(ds_cc[u], qb[u]) + _dot_tn(ds_nc[u], qnb[u]) for u in units}
            dv = {u: _dot_tn(p_cc[u].astype(BF16), dob[u]) + _dot_tn(p_nc[u].astype(BF16), donb[u]) for u in units}
            for dr in range(per_trip):
                dq_ref[rows[dr], :] = jnp.concatenate([dq[dr, 0], dq[dr, 1]], axis=1)
                dk_ref[rows[dr], :] = jnp.concatenate([dk[dr, 0], dk[dr, 1]], axis=1)
                dv_ref[rows[dr], :] = jnp.concatenate([dv[dr, 0], dv[dr, 1]], axis=1)
            return carry

        lax.fori_loop(0, r // per_trip, residues, 0)

    def at(col0, pick):
        return pl.BlockSpec((sbr, LANES), lambda c, i: (pick(i), col0 // LANES + c))

    cur = lambda i: i
    prv = lambda i: jnp.maximum(i - 1, 0)
    nxt = lambda i: jnp.minimum(i + 1, nsb - 1)
    cq, ck, cv = COL_CQ + gi * C_OUT, COL_CK + gi * C_OUT, COL_CV + gi * C_OUT
    return pl.pallas_call(
        body, name=name, grid=(C_OUT // LANES, nsb),
        in_specs=[at(cq, cur), at(cq, nxt), at(ck, cur), at(ck, prv), at(cv, cur), at(cv, prv),
                  at(0, cur), at(0, nxt), at(0, cur), at(0, nxt), at(0, cur), at(0, nxt),
                  pl.BlockSpec((1, LANES), lambda c, i: (0, c))],
        out_specs=[at(0, cur)] * 3, out_shape=[jax.ShapeDtypeStruct((t, C_OUT), F32)] * 3,
        compiler_params=_cparams(("parallel", "parallel")),
    )(p, p, p, p, p, p, do, do, o, o, lse, lse, slope_cols)


def _dil_merge(os_, ls_, name):
    t, w = os_[0].shape
    tr = _rows(t)

    def body(o0, o1, o2, l0, l1, l2, y_ref, lse_ref):
        a, b, c = l0[...], l1[...], l2[...]
        m = jnp.maximum(jnp.maximum(a, b), c)
        ea, eb, ec = jnp.exp(a - m), jnp.exp(b - m), jnp.exp(c - m)
        den = ea + eb + ec
        y_ref[...] = (ea * o0[...] + eb * o1[...] + ec * o2[...]) / den
        lse_ref[...] = m + jnp.log(den)

    row = pl.BlockSpec((tr, w), lambda i: (i, 0))
    return pl.pallas_call(
        body, name=name, grid=(t // tr,), in_specs=[row] * 6, out_specs=[row, row],
        out_shape=[jax.ShapeDtypeStruct((t, w), F32)] * 2, compiler_params=_cparams(("parallel",)),
    )(*os_, *ls_)


def _gate_fwd(ys, gl, ws, name):
    t = gl.shape[0]
    d = gl.shape[1] // N_BRANCH
    tr = _rows(t)

    def body(ya, yb, yc, gl_ref, wa, wb, wc, m_ref):
        acc = None
        for i, (y, w) in enumerate(((ya, wa), (yb, wb), (yc, wc))):
            z = _dot(y[...].astype(BF16), w[...])
            term = jax.nn.sigmoid(gl_ref[:, i * d:(i + 1) * d]) * z
            acc = term if acc is None else acc + term
        m_ref[...] = acc.astype(m_ref.dtype)

    rows = [pl.BlockSpec((tr, y.shape[1]), lambda i: (i, 0)) for y in ys]
    wsp = [pl.BlockSpec(w.shape, lambda i: (0, 0)) for w in ws]
    return pl.pallas_call(
        body, name=name, grid=(t // tr,),
        in_specs=rows + [pl.BlockSpec((tr, N_BRANCH * d), lambda i: (i, 0))] + wsp,
        out_specs=pl.BlockSpec((tr, d), lambda i: (i, 0)), out_shape=jax.ShapeDtypeStruct((t, d), BF16),
        compiler_params=_cparams(("parallel",)),
    )(*ys, gl, *ws)


def _gate_bwd(dm, ys, gl, ws, name):
    t = gl.shape[0]
    d = gl.shape[1] // N_BRANCH
    tr = _rows(t)

    def body(dm_ref, ya, yb, yc, gl_ref, wa, wb, wc, dya, dyb, dyc, dgl_ref, dwa, dwb, dwc):
        step = pl.program_id(0)
        dmv = dm_ref[...].astype(F32)
        for i, (y, w, dy, dw) in enumerate(((ya, wa, dya, dwa), (yb, wb, dyb, dwb), (yc, wc, dyc, dwc))):
            yb16 = y[...].astype(BF16)
            z = _dot(yb16, w[...])
            sg = jax.nn.sigmoid(gl_ref[:, i * d:(i + 1) * d])
            dgl_ref[:, i * d:(i + 1) * d] = dmv * z * sg * (1.0 - sg)
            e = (dmv * sg).astype(BF16)
            dy[...] = _dot_nt(e, w[...])
            contrib = _dot_tn(yb16, e)

            @pl.when(step == 0)
            def _(dw=dw, contrib=contrib):
                dw[...] = contrib

            @pl.when(step > 0)
            def _(dw=dw, contrib=contrib):
                dw[...] += contrib

    rows = [pl.BlockSpec((tr, y.shape[1]), lambda i: (i, 0)) for y in ys]
    wsp = [pl.BlockSpec(w.shape, lambda i: (0, 0)) for w in ws]
    gsp = pl.BlockSpec((tr, N_BRANCH * d), lambda i: (i, 0))
    return pl.pallas_call(
        body, name=name, grid=(t // tr,),
        in_specs=[pl.BlockSpec((tr, d), lambda i: (i, 0))] + rows + [gsp] + wsp,
        out_specs=rows + [gsp] + wsp,
        out_shape=[jax.ShapeDtypeStruct(y.shape, F32) for y in ys] + [jax.ShapeDtypeStruct(gl.shape, F32)]
        + [jax.ShapeDtypeStruct(w.shape, F32) for w in ws],
        compiler_params=_cparams(("arbitrary",)),
    )(dm, *ys, gl, *ws)


def _adamw(w, m, v, gparts, name):
    r, c = w.shape
    n = gparts.shape[0]
    br = LANES if r % LANES == 0 else r
    c1 = 1.0 - ADAM_B1 ** ADAM_STEP
    c2 = 1.0 - ADAM_B2 ** ADAM_STEP

    def body(w_ref, m_ref, v_ref, g_ref, go_ref, d_ref, mo_ref, vo_ref):
        g = g_ref[0].astype(F32)
        for i in range(1, n):
            g = g + g_ref[i].astype(F32)
        mn = ADAM_B1 * m_ref[...] + (1.0 - ADAM_B1) * g
        vn = ADAM_B2 * v_ref[...] + (1.0 - ADAM_B2) * (g * g)
        go_ref[...] = g
        mo_ref[...] = mn
        vo_ref[...] = vn
        d_ref[...] = -ADAM_LR * ((mn / c1) / (jnp.sqrt(vn / c2) + ADAM_EPS) + ADAM_WD * w_ref[...])

    blk = pl.BlockSpec((br, c), lambda i: (i, 0))
    return pl.pallas_call(
        body, name=name, grid=(r // br,),
        in_specs=[blk, blk, blk, pl.BlockSpec((n, br, c), lambda i: (0, i, 0))],
        out_specs=[blk] * 4, out_shape=[jax.ShapeDtypeStruct((r, c), F32)] * 4,
        compiler_params=_cparams(("parallel",)),
    )(w, m, v, gparts)


def _my_coords():
    return lax.axis_index("x"), lax.axis_index("y"), lax.axis_index("c")


COMM_SEMS = [pltpu.SemaphoreType.DMA((7,)), pltpu.SemaphoreType.DMA((7,)), pltpu.SemaphoreType.DMA]


class _Gather:
    def __init__(self, x_ref, out_ref, send_sems, recv_sems, local_sem):
        self.x_ref, self.out_ref = x_ref, out_ref
        self.send_sems, self.recv_sems, self.local_sem = send_sems, recv_sems, local_sem
        self.m_per = x_ref.shape[0]
        x, y, c = _my_coords()
        self.c = c
        self.me, self.sibling = (x, y, c), (x, y, 1 - c)
        self.chips = [(1 - x, y), (x, 1 - y), (1 - x, 1 - y)]

    def rows(self, px, py, pc):
        return self.out_ref.at[pl.ds((4 * px + 2 * py + pc) * self.m_per, self.m_per), :]

    def copy(self, k, block, to, src=None):
        return pltpu.make_async_remote_copy(
            src_ref=self.rows(*block) if src is None else src, dst_ref=self.rows(*block),
            send_sem=self.send_sems.at[k], recv_sem=self.recv_sems.at[k], device_id=to, device_id_type=MESH)

    def mine(self):
        return pltpu.make_async_copy(self.x_ref, self.rows(*self.me), self.local_sem)

    def first(self):
        out = [self.copy(0, self.me, self.sibling, src=self.x_ref)]
        return out + [self.copy(1 + j, self.me, (*chip, self.c), src=self.x_ref) for j, chip in enumerate(self.chips)]

    def passed(self):
        return [self.copy(4 + j, (*chip, self.c), self.sibling) for j, chip in enumerate(self.chips)]

    def start(self):
        self.mine().start()
        for cp in self.first():
            cp.start()

    def forward(self):
        passed = self.passed()
        for j, chip in enumerate(self.chips):
            self.copy(1 + j, (*chip, self.c), self.me).wait_recv()
            passed[j].start()

    def finish(self):
        self.copy(0, self.sibling, self.me).wait_recv()
        for j, chip in enumerate(self.chips):
            self.copy(4 + j, (*chip, 1 - self.c), self.me).wait_recv()
        for cp in self.first() + self.passed():
            cp.wait_send()
        self.mine().wait()


class _Exchange:
    def __init__(self, send_ref, recv_ref, send_sems, recv_sems, local_sem):
        self.send_ref, self.recv_ref = send_ref, recv_ref
        self.send_sems, self.recv_sems, self.local_sem = send_sems, recv_sems, local_sem
        x, y, c = _my_coords()
        self.me = 4 * x + 2 * y + c
        self.peers = []
        for k in range(1, N_DEV):
            px = 1 - x if k & 4 else x
            py = 1 - y if k & 2 else y
            pc = 1 - c if k & 1 else c
            self.peers.append((4 * px + 2 * py + pc, (px, py, pc)))

    def mine(self):
        return pltpu.make_async_copy(self.send_ref.at[self.me], self.recv_ref.at[self.me], self.local_sem)

    def copy(self, k, src_slot, dst_slot):
        return pltpu.make_async_remote_copy(
            src_ref=self.send_ref.at[src_slot], dst_ref=self.recv_ref.at[dst_slot],
            send_sem=self.send_sems.at[k], recv_sem=self.recv_sems.at[k],
            device_id=self.peers[k][1], device_id_type=MESH)

    def start(self):
        self.mine().start()
        for k, (peer, _) in enumerate(self.peers):
            self.copy(k, peer, self.me).start()

    def finish(self):
        for k, (peer, _) in enumerate(self.peers):
            self.copy(k, peer, self.me).wait_send()
            self.copy(k, self.me, peer).wait_recv()
        self.mine().wait()


def _all_gather(x_shard, in_vmem, with_sum, name):
    m_per, n = x_shard.shape

    def body(x_ref, out_ref, *rest):
        if with_sum:
            sum_ref, send_sems, recv_sems, local_sem = rest
        else:
            send_sems, recv_sems, local_sem = rest
        g = _Gather(x_ref, out_ref, send_sems, recv_sems, local_sem)
        g.start()
        g.forward()
        g.finish()
        if with_sum:
            acc = out_ref[pl.ds(0, m_per), :]
            for d in range(1, N_DEV):
                acc = acc + out_ref[pl.ds(d * m_per, m_per), :]
            sum_ref[...] = acc

    space = pltpu.VMEM if in_vmem else pl.ANY
    out_shape = [jax.ShapeDtypeStruct((N_DEV * m_per, n), x_shard.dtype)]
    out_specs = [pl.BlockSpec(memory_space=space)]
    if with_sum:
        out_shape.append(jax.ShapeDtypeStruct((m_per, n), x_shard.dtype))
        out_specs.append(pl.BlockSpec(memory_space=pltpu.VMEM))
    res = pl.pallas_call(
        body, name=name, out_shape=out_shape, in_specs=[pl.BlockSpec(memory_space=space)], out_specs=out_specs,
        scratch_shapes=COMM_SEMS, compiler_params=pltpu.CompilerParams(vmem_limit_bytes=VMEM_LIMIT),
    )(x_shard)
    return res if with_sum else res[0]


def _comm_sems(n):
    return [pltpu.SemaphoreType.DMA((n, 7)), pltpu.SemaphoreType.DMA((n, 7)), pltpu.SemaphoreType.DMA((n,))]


class _Many:
    def __init__(self, kind, ins, outs, send_sems, recv_sems, local_sems):
        self.parts = [kind(i, o, send_sems.at[b], recv_sems.at[b], local_sems.at[b])
                      for b, (i, o) in enumerate(zip(ins, outs))]

    def start(self):
        for part in self.parts:
            part.start()

    def forward(self):
        for part in self.parts:
            part.forward()

    def finish(self):
        for part in self.parts:
            part.finish()


def _gathered_shapes(shards):
    return [jax.ShapeDtypeStruct((N_DEV * s.shape[0],) + s.shape[1:], s.dtype) for s in shards]


def _all_gather_many(shards, name):
    n = len(shards)

    def body(*refs):
        g = _Many(_Gather, refs[:n], refs[n:2 * n], *refs[2 * n:])
        g.start()
        g.forward()
        g.finish()

    hbm = pl.BlockSpec(memory_space=pl.ANY)
    return pl.pallas_call(body, name=name, out_shape=_gathered_shapes(shards), in_specs=[hbm] * n,
                          out_specs=[hbm] * n, scratch_shapes=_comm_sems(n))(*shards)


def _all_to_all_many(sends, name):
    n = len(sends)

    def body(*refs):
        ex = _Many(_Exchange, refs[:n], refs[n:2 * n], *refs[2 * n:])
        ex.start()
        ex.finish()

    hbm = pl.BlockSpec(memory_space=pl.ANY)
    return pl.pallas_call(body, name=name, out_shape=[jax.ShapeDtypeStruct(s.shape, s.dtype) for s in sends],
                          in_specs=[hbm] * n, out_specs=[hbm] * n, scratch_shapes=_comm_sems(n))(*sends)


def _row(v):
    return v.reshape(1, -1)


def _ffn_fwd(x, w_in, w_out, g_pre, g_post, m, res_w, tag):
    shift, scale, gate = m[0], m[1], m[2]
    mpre = _row(g_pre * (1.0 + scale))
    mpost = _row(res_w * gate * g_post)
    h = _rms_fwd(x, mpre, _row(shift), None, BF16, tag + "_pre")
    u = _matmul(h, w_in, out_dtype=BF16, name=tag + "_in")
    s = _swiglu_fwd(u, tag + "_act")
    y = _matmul(s, w_out, name=tag + "_out")
    x_new = _rms_fwd(y, mpost, jnp.zeros_like(mpost), x, F32, tag + "_post")
    return x_new, (x, h, u, s, y, mpre, mpost)


def _sub_bwd_post(dx_new, y, mpost, g_post, gate, res_w, tag):
    dy, c1, _ = _rms_bwd(dx_new, y, mpost, None, BF16, tag + "_post_bwd")
    c1 = c1[0]
    return dy, c1 * res_w * g_post, c1 * res_w * gate


def _sub_bwd_pre(dh, x, mpre, dx_new, g_pre, scale, tag):
    dx, c2, c3 = _rms_bwd(dh, x, mpre, dx_new, F32, tag + "_pre_bwd")
    c2, c3 = c2[0], c3[0]
    return dx, c3, c2 * g_pre, c2 * (1.0 + scale)


def _ffn_bwd(dx_new, saved, w_in, w_out, g_pre, g_post, m, res_w, tag):
    x, h, u, s, y, mpre, mpost = saved
    scale, gate = m[1], m[2]
    dy, dgate, dg_post = _sub_bwd_post(dx_new, y, mpost, g_post, gate, res_w, tag)
    ds = _matmul(dy, w_out, tb=True, out_dtype=BF16, name=tag + "_out_dx")
    dw_out = _matmul(s, dy, ta=True, out_dtype=BF16, name=tag + "_out_dw")
    du = _swiglu_bwd(u, ds, tag + "_act_bwd")
    dh = _matmul(du, w_in, tb=True, name=tag + "_in_dx")
    dw_in = _matmul(h, du, ta=True, out_dtype=BF16, name=tag + "_in_dw")
    dx, dshift, dscale, dg_pre = _sub_bwd_pre(dh, x, mpre, dx_new, g_pre, scale, tag)
    return dx, dw_in, dw_out, jnp.stack([dshift, dscale, dgate]), dg_pre, dg_post


def _slope_cols(gi):
    _, r = C_GROUPS[gi]
    sl = jnp.asarray(_alibi_slopes(C_HEADS)[gi * C_HPG:(gi + 1) * C_HPG], F32) * float(r)
    return jnp.repeat(sl, HDIM).reshape(1, C_OUT)


def _mix_fwd(x, w, g_pre, g_post, m, lb, hn, tag, gather=None):
    t, d = x.shape
    shift, scale, gate = m[0], m[1], m[2]
    mpre = _row(g_pre * (1.0 + scale))
    mpost = _row(gate * g_post)
    h = _rms_fwd(x, mpre, _row(shift), None, BF16, tag + "_pre")
    p = _matmul(h, w["w_in"], name=tag + "_in")
    hn2 = _row(jnp.tile(hn, 2))
    ya, oa, states = _hgrn_fwd(p, _row(lb), hn2, tag + "_hgrn")
    kv = p[:, COL_BK:COL_CQ].astype(BF16)
    if gather is None:
        (yb, sb_tot), gathered = _sb_fwd(p, kv, tag + "_sb"), None
    else:
        res = _sb_fwd(p, kv, tag + "_sb_gather", gather)
        yb, sb_tot, gathered = res[0], res[1], list(res[2:])
    og, lg = zip(*[_dil_fwd(p, gi, tag + "_dil%d" % gi) for gi in range(len(C_GROUPS))])
    yc, lse_c = _dil_merge(og, lg, tag + "_dil_merge")
    gl = p[:, COL_GATE:]
    ws = (w["w_branch_a"], w["w_branch_b"], w["w_branch_c"])
    merged = _gate_fwd((ya, yb, yc), gl, ws, tag + "_gate")
    y = _matmul(merged, w["w_out"], name=tag + "_out")
    x_new = _rms_fwd(y, mpost, jnp.zeros_like(mpost), x, F32, tag + "_post")
    return x_new, (x, h, p, hn2, ya, oa, states, yb, kv, sb_tot, yc, lse_c, gl, merged, y, mpre, mpost), gathered


def _mix_bwd(dx_new, saved, w, g_pre, g_post, m, lb, tag, exchange=None):
    x, h, p, hn2, ya, oa, states, yb, kv, sb_tot, yc, lse_c, gl, merged, y, mpre, mpost = saved
    t = x.shape[0]
    scale, gate = m[1], m[2]
    dy, dgate, dg_post = _sub_bwd_post(dx_new, y, mpost, g_post, gate, 1.0, tag)
    dmerged = _matmul(dy, w["w_out"], tb=True, out_dtype=BF16, name=tag + "_out_dx")
    dw_out = _matmul(merged, dy, ta=True, out_dtype=BF16, name=tag + "_out_dw")
    ws = (w["w_branch_a"], w["w_branch_b"], w["w_branch_c"])
    dya, dyb, dyc, dgl, dwa, dwb, dwc = _gate_bwd(dmerged, (ya, yb, yc), gl, ws, tag + "_gate_bwd")
    dqa, dfa, dia, dga, dlb, dhn = _hgrn_bwd(p, _row(lb), hn2, oa, states, dya, tag + "_hgrn_bwd")
    if exchange is None:
        (dbq, dbk, dbv), received = _sb_bwd(p, kv, sb_tot, dyb, tag + "_sb_bwd"), None
    else:
        res = _sb_bwd(p, kv, sb_tot, dyb, tag + "_sb_bwd_exchange", exchange)
        dbq, dbk, dbv, received = res[0], res[1], res[2], list(res[3:])
    dcq, dck, dcv = zip(*[_dil_bwd(p, dyc, yc, lse_c, gi, tag + "_dil%d_bwd" % gi) for gi in range(len(C_GROUPS))])
    dp = jnp.concatenate([dqa, dfa, dia, dga, dbq, dbk, dbv, *dcq, *dck, *dcv, dgl], axis=1).astype(BF16)
    dh = _matmul(dp, w["w_in"], tb=True, name=tag + "_in_dx")
    dw_in = _matmul(h, dp, ta=True, out_dtype=BF16, name=tag + "_in_dw")
    dx, dshift, dscale, dg_pre = _sub_bwd_pre(dh, x, mpre, dx_new, g_pre, scale, tag)
    dhn_v = jnp.sum(dhn, axis=(0, 1))
    dhn_v = dhn_v[:A_VDIM] + dhn_v[A_VDIM:]
    dws = dict(w_in=dw_in, w_out=dw_out, w_branch_a=dwa.astype(BF16), w_branch_b=dwb.astype(BF16),
               w_branch_c=dwc.astype(BF16))
    return dx, dws, jnp.stack([dshift, dscale, dgate]), dg_pre, dg_post, dlb[0], dhn_v, received


class _LocalWeights:
    def __init__(self, wts):
        self.wts = wts

    def first(self):
        return None

    def shard(self, l):
        return None

    def layer(self, l, gathered):
        return {k: v[l] for k, v in self.wts.items()}

    def pack(self, l, dws):
        return dws

    def last(self, packed):
        return packed


class _ShardedWeights:
    def __init__(self, shards):
        self.shards = shards

    def shard(self, l):
        return [self.shards[k][l].astype(BF16) for k in BIG_WEIGHTS]

    def first(self):
        return _all_gather_many(self.shard(0), "weights_all_gather")

    def layer(self, l, gathered):
        out = {}
        for k, got in zip(BIG_WEIGHTS, gathered):
            _, r, c = self.shards[k].shape
            out[k] = got if k in ROW_SHARDED else got.reshape(N_DEV, r, c).transpose(1, 0, 2).reshape(r, N_DEV * c)
        return out

    def pack(self, l, dws):
        out = []
        for k in BIG_WEIGHTS:
            _, r, c = self.shards[k].shape
            g = dws[k]
            out.append(g.reshape(N_DEV, r, c) if k in ROW_SHARDED else g.reshape(r, N_DEV, c).transpose(1, 0, 2))
        return out

    def last(self, packed):
        return _all_to_all_many(packed, "grads_all_to_all")

    def partial_sums(self, received):
        return {k: jnp.concatenate([rec[b] for rec in received], axis=1) for b, k in enumerate(BIG_WEIGHTS)}


def _local_step(x, target, mod, norm_g, lb_all, hnorm, supply):
    depth = mod.shape[0]
    d = x.shape[1]
    saved, wls = [], []
    gathered = supply.first()
    for l in range(depth):
        wl = supply.layer(l, gathered)
        wls.append(wl)
        x, s0 = _ffn_fwd(x, wl["ffn1_w_in"], wl["ffn1_w_out"], norm_g[l, 0], norm_g[l, 1], mod[l, 0], 0.5, "ffn1")
        nxt = supply.shard(l + 1) if l + 1 < depth else None
        x, s1, gathered = _mix_fwd(x, wl, norm_g[l, 2], norm_g[l, 3], mod[l, 1], lb_all[l], hnorm[l], "mix", nxt)
        x, s2 = _ffn_fwd(x, wl["ffn2_w_in"], wl["ffn2_w_out"], norm_g[l, 4], norm_g[l, 5], mod[l, 2], 0.5, "ffn2")
        saved.append((s0, s1, s2))
    dx, sq = _loss_head(x, target, "loss_head")
    loss = 0.5 * jnp.sum(sq) / d
    dmod, dng, dlb, dhn = [], [], [], []
    returned = [None] * depth
    pending = None
    for l in reversed(range(depth)):
        wl = wls[l]
        s0, s1, s2 = saved[l]
        dx, dwi2, dwo2, dm2, dgp2, dgq2 = _ffn_bwd(dx, s2, wl["ffn2_w_in"], wl["ffn2_w_out"], norm_g[l, 4],
                                                   norm_g[l, 5], mod[l, 2], 0.5, "ffn2")
        fuse = pending is not None and isinstance(supply, _ShardedWeights)
        dx, dwm, dm1, dgp1, dgq1, dlb_l, dhn_l, received = _mix_bwd(
            dx, s1, wl, norm_g[l, 2], norm_g[l, 3], mod[l, 1], lb_all[l], "mix", pending if fuse else None)
        if pending is not None:
            returned[l + 1] = received if fuse else pending
        dx, dwi1, dwo1, dm0, dgp0, dgq0 = _ffn_bwd(dx, s0, wl["ffn1_w_in"], wl["ffn1_w_out"], norm_g[l, 0],
                                                   norm_g[l, 1], mod[l, 0], 0.5, "ffn1")
        dmod.append(jnp.stack([dm0, dm1, dm2]))
        dng.append(jnp.stack([dgp0, dgq0, dgp1, dgq1, dgp2, dgq2]))
        dlb.append(dlb_l)
        dhn.append(dhn_l)
        pending = supply.pack(l, dict(dwm, ffn1_w_in=dwi1, ffn1_w_out=dwo1, ffn2_w_in=dwi2, ffn2_w_out=dwo2))
    returned[0] = supply.last(pending)
    rev = lambda lst: jnp.stack(lst[::-1])
    return loss, dx, rev(dmod), rev(dng), rev(dlb), rev(dhn), returned


def _lb_all(logits):
    lb_p = jax.nn.softmax(logits.astype(F32), axis=0)
    return jnp.cumsum(lb_p, axis=0) - lb_p[0:1]


def _pad_rows(a, rows):
    return jnp.pad(a, ((0, rows - a.shape[0]), (0, 0)))


def kernel(x, c, w_ada, b_ada, norm_g, ffn1_w_in, ffn1_w_out, w_in, hgrn_lb_logits, hgrn_norm_g, w_branch_a, w_branch_b, w_branch_c, w_out, ffn2_w_in, ffn2_w_out, loss_target, m_w_ada, m_b_ada, m_norm_g, m_ffn1_w_in, m_ffn1_w_out, m_w_in, m_hgrn_lb_logits, m_hgrn_norm_g, m_w_branch_a, m_w_branch_b, m_w_branch_c, m_w_out, m_ffn2_w_in, m_ffn2_w_out, v_w_ada, v_b_ada, v_norm_g, v_ffn1_w_in, v_ffn1_w_out, v_w_in, v_hgrn_lb_logits, v_hgrn_norm_g, v_w_branch_a, v_w_branch_b, v_w_branch_c, v_w_out, v_ffn2_w_in, v_ffn2_w_out):
    weights = dict(w_ada=w_ada, b_ada=b_ada, norm_g=norm_g, ffn1_w_in=ffn1_w_in, ffn1_w_out=ffn1_w_out, w_in=w_in,
                   hgrn_lb_logits=hgrn_lb_logits, hgrn_norm_g=hgrn_norm_g, w_branch_a=w_branch_a,
                   w_branch_b=w_branch_b, w_branch_c=w_branch_c, w_out=w_out, ffn2_w_in=ffn2_w_in,
                   ffn2_w_out=ffn2_w_out)
    mom1 = dict(w_ada=m_w_ada, b_ada=m_b_ada, norm_g=m_norm_g, ffn1_w_in=m_ffn1_w_in, ffn1_w_out=m_ffn1_w_out,
                w_in=m_w_in, hgrn_lb_logits=m_hgrn_lb_logits, hgrn_norm_g=m_hgrn_norm_g, w_branch_a=m_w_branch_a,
                w_branch_b=m_w_branch_b, w_branch_c=m_w_branch_c, w_out=m_w_out, ffn2_w_in=m_ffn2_w_in,
                ffn2_w_out=m_ffn2_w_out)
    mom2 = dict(w_ada=v_w_ada, b_ada=v_b_ada, norm_g=v_norm_g, ffn1_w_in=v_ffn1_w_in, ffn1_w_out=v_ffn1_w_out,
                w_in=v_w_in, hgrn_lb_logits=v_hgrn_lb_logits, hgrn_norm_g=v_hgrn_norm_g, w_branch_a=v_w_branch_a,
                w_branch_b=v_w_branch_b, w_branch_c=v_w_branch_c, w_out=v_w_out, ffn2_w_in=v_ffn2_w_in,
                ffn2_w_out=v_ffn2_w_out)
    order = list(weights)
    depth, d, ada_cols = w_ada.shape
    nd = d // LANES
    xi, yi, ci = _my_coords()
    me = 4 * xi + 2 * yi + ci

    small = jnp.concatenate([c.reshape(nd, LANES), norm_g.reshape(depth * 6, LANES)], axis=0)
    g1 = _all_gather(small, True, False, "small_all_gather").reshape(N_DEV, small.shape[0], LANES)
    c_act = _silu(g1[:, :nd].reshape(N_DEV, d))
    norm_full = g1[:, nd:].reshape(N_DEV, depth, 6, LANES).transpose(1, 2, 0, 3).reshape(depth, 6, d)

    c_pad = _pad_rows(c_act, 16)
    mod_sh = jnp.stack([_matmul(c_pad, w_ada[l], name="ada_mod")[:N_DEV]
                        + lax.dynamic_slice_in_dim(b_ada[l], me * ada_cols, ada_cols)[None]
                        for l in range(depth)])
    g2 = _all_gather(mod_sh.reshape(-1, LANES), True, False, "mod_all_gather")
    g2 = g2.reshape(N_DEV, depth, N_DEV, ada_cols)
    mod = lax.dynamic_index_in_dim(g2, me, axis=2, keepdims=False)
    mod = mod.transpose(1, 0, 2).reshape(depth, 3, 3, d)

    supply = _ShardedWeights({k: weights[k] for k in BIG_WEIGHTS})
    lb_all, lb_vjp = jax.vjp(_lb_all, hgrn_lb_logits)

    loss, dx, dmod, dng, dlb, dhn, received = _local_step(x[0], loss_target[0], mod, norm_full, lb_all,
                                                          hgrn_norm_g, supply)
    loss = lax.psum(loss, ("x", "y", "c"))

    dhn_pad = jnp.pad(dhn.reshape(-1), (0, 8 * LANES - dhn.size))
    pieces = [dmod.reshape(-1), dng.reshape(-1), dlb.reshape(-1), dhn_pad]
    sizes = [p_.size for p_ in pieces]
    smallg = jnp.concatenate(pieces).reshape(-1, LANES)
    g3, gsum = _all_gather(smallg, True, True, "small_grads_all_gather")
    g3 = g3.reshape(N_DEV, -1)
    gsum = gsum.reshape(-1)
    dmod_all = g3[:, :sizes[0]].reshape(N_DEV, depth, 9 * d)
    o1 = sizes[0]
    grads = {}
    grads["b_ada"] = gsum[:o1].reshape(depth, 9 * d)
    dng_sum = gsum[o1:o1 + sizes[1]].reshape(depth, 6, nd, LANES)
    grads["norm_g"] = lax.dynamic_index_in_dim(dng_sum, me, axis=2, keepdims=False)
    o2 = o1 + sizes[1]
    dlb_sum = gsum[o2:o2 + sizes[2]].reshape(depth, A_QK)
    grads["hgrn_lb_logits"] = lb_vjp(dlb_sum)[0]
    o3 = o2 + sizes[2]
    grads["hgrn_norm_g"] = gsum[o3:o3 + dhn.size].reshape(depth, A_VDIM)
    dmod_mine = lax.dynamic_slice_in_dim(dmod_all, me * ada_cols, ada_cols, axis=2)
    grads["w_ada"] = jnp.stack([_matmul(c_pad, _pad_rows(dmod_mine[:, l], 16), ta=True, name="ada_dw")
                                for l in range(depth)])

    gparts = supply.partial_sums(received)

    outs = {}
    for k in order:
        w = weights[k]
        w2 = w.reshape(-1, w.shape[-1])
        gp = gparts[k] if k in gparts else grads[k].reshape((1,) + w2.shape)
        res = _adamw(w2, mom1[k].reshape(w2.shape), mom2[k].reshape(w2.shape), gp, "adamw")
        outs[k] = [r.reshape(w.shape) for r in res]
    return (loss, dx[None], *[outs[k][0] for k in order], *[outs[k][1] for k in order],
            *[outs[k][2] for k in order], *[outs[k][3] for k in order])
```

```python
import functools
import math

import jax
import jax.numpy as jnp
from jax import lax
from jax.experimental import pallas as pl
from jax.experimental.pallas import tpu as pltpu

F32 = jnp.float32
BF16 = jnp.bfloat16

A_HEADS, A_KDIM, A_VDIM, A_CHUNK = 6, 128, 64, 64
B_HEADS, HDIM = 6, 64
C_GROUPS = ((128, 1), (512, 4), (2048, 16))
C_HPG = 4
C_HEADS = C_HPG * len(C_GROUPS)
N_BRANCH = 3
EPS = 1e-6
NEG_BIG = -1e30
TINY = 1e-30
A_QK = A_HEADS * A_KDIM
A_V = A_HEADS * A_VDIM
B_W = B_HEADS * HDIM
C_W = C_HEADS * HDIM
C_OUT = C_HPG * HDIM
COL_AQ, COL_AF, COL_AI, COL_AG = 0, A_QK, 2 * A_QK, 2 * A_QK + A_V
COL_BQ = 2 * A_QK + 2 * A_V
COL_BK, COL_BV = COL_BQ + B_W, COL_BQ + 2 * B_W
COL_CQ = COL_BQ + 3 * B_W
COL_CK, COL_CV = COL_CQ + C_W, COL_CQ + 2 * C_W
COL_GATE = COL_CQ + 3 * C_W

ADAM_LR, ADAM_B1, ADAM_B2, ADAM_EPS, ADAM_WD, ADAM_STEP = 0.001, 0.9, 0.999, 1e-08, 0.01, 10

N_DEV = 8
LANES = 128
VMEM_LIMIT = 48 * 1024 * 1024
MATMUL_VMEM_BUDGET = 28 * 1024 * 1024
SUB = 16
EXP_CLAMP = 80.0
MESH = pl.DeviceIdType.MESH

BIG_WEIGHTS = ("ffn1_w_in", "ffn1_w_out", "w_in", "w_branch_a", "w_branch_b", "w_branch_c", "w_out",
               "ffn2_w_in", "ffn2_w_out")
ROW_SHARDED = ("ffn1_w_out", "w_out", "ffn2_w_out")


def _cparams(sem):
    return pltpu.CompilerParams(dimension_semantics=sem, vmem_limit_bytes=VMEM_LIMIT)


def _tile(n, cap):
    best, t = None, LANES
    while t <= min(n, cap):
        if n % t == 0:
            best = t
        t += LANES
    return best or n


def _rows(t, cap=256):
    r = cap
    while t % r:
        r //= 2
    return r


def _divisors(n):
    return [t for t in range(LANES, n + 1, LANES) if n % t == 0] or [n]


def _matmul_tiles(m, n, k, a_size, b_size, o_size):
    best, best_key = None, None
    for tm in _divisors(m):
        for tn in _divisors(n):
            for tk in _divisors(k):
                if tm > 1024 or tn > 3072 or tk > 4096:
                    continue
                cast = (tm * tk * 2 if a_size > 2 else 0) + (tk * tn * 2 if b_size > 2 else 0)
                need = 2 * (tm * tk * a_size + tk * tn * b_size + tm * tn * o_size) + 2 * tm * tn * 4 + cast
                if need > MATMUL_VMEM_BUDGET:
                    continue
                key = (tm * tn * tk, tk)
                if best_key is None or key > best_key:
                    best, best_key = (tm, tn, tk), key
    return best


def _dot(a, b):
    return jnp.dot(a, b, preferred_element_type=F32)


def _dot_nt(a, b):
    return lax.dot_general(a, b, (((1,), (1,)), ((), ())), preferred_element_type=F32)


def _dot_tn(a, b):
    return lax.dot_general(a, b, (((0,), (0,)), ((), ())), preferred_element_type=F32)


def _split3(x):
    h = x.astype(BF16)
    r = x - h.astype(F32)
    m = r.astype(BF16)
    lo = (r - m.astype(F32)).astype(BF16)
    return h, m, lo


def _ones_left(mat01, x):
    h, m, lo = _split3(x)
    return _dot(mat01, h) + _dot(mat01, m) + _dot(mat01, lo)


def _silu(x):
    return x * jax.nn.sigmoid(x)


def _dsilu(x):
    s = jax.nn.sigmoid(x)
    return s * (1.0 + x * (1.0 - s))


def _matmul(a, b, *, ta=False, tb=False, out_dtype=F32, name):
    if ta:
        kdim, m = a.shape
    else:
        m, kdim = a.shape
    n = b.shape[0] if tb else b.shape[1]
    tm, tn, tk = _matmul_tiles(m, n, kdim, a.dtype.itemsize, b.dtype.itemsize, jnp.dtype(out_dtype).itemsize)
    nk = kdim // tk
    ni, nj = m // tm, n // tn
    a_bytes, b_bytes = m * kdim * a.dtype.itemsize, kdim * n * b.dtype.itemsize
    j_outer = nk == 1 and (b_bytes + a_bytes * nj) < (a_bytes + b_bytes * ni)
    dims = (((0 if ta else 1,), (1 if tb else 0,)), ((), ()))

    def body(a_ref, b_ref, o_ref, *scratch):
        p = lax.dot_general(a_ref[...].astype(BF16), b_ref[...].astype(BF16), dims, preferred_element_type=F32)
        if nk == 1:
            o_ref[...] = p.astype(o_ref.dtype)
            return
        acc = scratch[0]
        k = pl.program_id(2)

        @pl.when(k == 0)
        def _():
            acc[...] = p

        @pl.when(k > 0)
        def _():
            acc[...] += p

        @pl.when(k == nk - 1)
        def _():
            o_ref[...] = acc[...].astype(o_ref.dtype)

    def spec(shape, pick):
        if j_outer:
            return pl.BlockSpec(shape, lambda j, i, k: pick(i, j, k))
        return pl.BlockSpec(shape, lambda i, j, k: pick(i, j, k))

    a_spec = spec((tk, tm), lambda i, j, k: (k, i)) if ta else spec((tm, tk), lambda i, j, k: (i, k))
    b_spec = spec((tn, tk), lambda i, j, k: (j, k)) if tb else spec((tk, tn), lambda i, j, k: (k, j))
    return pl.pallas_call(
        body, name=name, grid=(nj, ni, nk) if j_outer else (ni, nj, nk), in_specs=[a_spec, b_spec],
        out_specs=spec((tm, tn), lambda i, j, k: (i, j)),
        out_shape=jax.ShapeDtypeStruct((m, n), out_dtype),
        scratch_shapes=[pltpu.VMEM((tm, tn), F32)] if nk > 1 else [],
        compiler_params=_cparams(("parallel", "parallel", "arbitrary")),
    )(a, b)


def _rms_fwd(z, mcol, acol, res, out_dtype, name):
    t, d = z.shape
    tr = _rows(t)
    has_res = res is not None

    def body(*refs):
        if has_res:
            z_ref, m_ref, a_ref, r_ref, o_ref = refs
        else:
            z_ref, m_ref, a_ref, o_ref = refs
        zf = z_ref[...]
        r = lax.rsqrt(jnp.mean(zf * zf, axis=-1, keepdims=True) + EPS)
        y = zf * r * m_ref[...] + a_ref[...]
        if has_res:
            y = r_ref[...] + y
        o_ref[...] = y.astype(o_ref.dtype)

    row = pl.BlockSpec((tr, d), lambda i: (i, 0))
    col = pl.BlockSpec((1, d), lambda i: (0, 0))
    ins = [z, mcol, acol] + ([res] if has_res else [])
    return pl.pallas_call(
        body, name=name, grid=(t // tr,), in_specs=[row, col, col] + ([row] if has_res else []),
        out_specs=row, out_shape=jax.ShapeDtypeStruct((t, d), out_dtype),
        compiler_params=_cparams(("parallel",)),
    )(*ins)


def _rms_bwd(d_out, z, mcol, dres, out_dtype, name):
    t, d = z.shape
    tr = _rows(t)
    has_res = dres is not None

    def body(*refs):
        if has_res:
            d_ref, z_ref, m_ref, r_ref, o_ref, s1_ref, s2_ref = refs
        else:
            d_ref, z_ref, m_ref, o_ref, s1_ref, s2_ref = refs
        i = pl.program_id(0)
        zf = z_ref[...]
        r = lax.rsqrt(jnp.mean(zf * zf, axis=-1, keepdims=True) + EPS)
        zh = zf * r
        df = d_ref[...].astype(F32)
        dzh = df * m_ref[...]
        dz = r * (dzh - zh * jnp.mean(dzh * zh, axis=-1, keepdims=True))
        if has_res:
            dz = dz + r_ref[...]
        o_ref[...] = dz.astype(o_ref.dtype)
        s1 = jnp.sum(df * zh, axis=0, keepdims=True)
        s2 = jnp.sum(df, axis=0, keepdims=True)

        @pl.when(i == 0)
        def _():
            s1_ref[...] = s1
            s2_ref[...] = s2

        @pl.when(i > 0)
        def _():
            s1_ref[...] += s1
            s2_ref[...] += s2

    row = pl.BlockSpec((tr, d), lambda i: (i, 0))
    col = pl.BlockSpec((1, d), lambda i: (0, 0))
    ins = [d_out, z, mcol] + ([dres] if has_res else [])
    return pl.pallas_call(
        body, name=name, grid=(t // tr,), in_specs=[row, row, col] + ([row] if has_res else []),
        out_specs=[row, col, col],
        out_shape=[jax.ShapeDtypeStruct((t, d), out_dtype), jax.ShapeDtypeStruct((1, d), F32),
                   jax.ShapeDtypeStruct((1, d), F32)],
        compiler_params=_cparams(("arbitrary",)),
    )(*ins)


def _swiglu_fwd(u, name):
    t, f2 = u.shape
    f = f2 // 2
    tr = _rows(t)

    def body(u_ref, s_ref):
        a = u_ref[:, :f].astype(F32)
        b = u_ref[:, f:].astype(F32)
        s_ref[...] = (_silu(a) * b).astype(s_ref.dtype)

    return pl.pallas_call(
        body, name=name, grid=(t // tr,), in_specs=[pl.BlockSpec((tr, f2), lambda i: (i, 0))],
        out_specs=pl.BlockSpec((tr, f), lambda i: (i, 0)), out_shape=jax.ShapeDtypeStruct((t, f), BF16),
        compiler_params=_cparams(("parallel",)),
    )(u)


def _swiglu_bwd(u, ds, name):
    t, f2 = u.shape
    f = f2 // 2
    tr = _rows(t)

    def body(u_ref, ds_ref, du_ref):
        a = u_ref[:, :f].astype(F32)
        b = u_ref[:, f:].astype(F32)
        g = ds_ref[...].astype(F32)
        du_ref[:, :f] = (g * b * _dsilu(a)).astype(du_ref.dtype)
        du_ref[:, f:] = (g * _silu(a)).astype(du_ref.dtype)

    return pl.pallas_call(
        body, name=name, grid=(t // tr,),
        in_specs=[pl.BlockSpec((tr, f2), lambda i: (i, 0)), pl.BlockSpec((tr, f), lambda i: (i, 0))],
        out_specs=pl.BlockSpec((tr, f2), lambda i: (i, 0)), out_shape=jax.ShapeDtypeStruct((t, f2), BF16),
        compiler_params=_cparams(("parallel",)),
    )(u, ds)


def _loss_head(y, target, name):
    t, d = y.shape
    tr = _rows(t)

    def body(y_ref, t_ref, dy_ref, sq_ref):
        i = pl.program_id(0)
        e = y_ref[...] - t_ref[...]
        dy_ref[...] = e * (1.0 / d)
        s = jnp.sum(e * e, axis=0, keepdims=True)

        @pl.when(i == 0)
        def _():
            sq_ref[...] = s

        @pl.when(i > 0)
        def _():
            sq_ref[...] += s

    row = pl.BlockSpec((tr, d), lambda i: (i, 0))
    col = pl.BlockSpec((1, d), lambda i: (0, 0))
    return pl.pallas_call(
        body, name=name, grid=(t // tr,), in_specs=[row, row], out_specs=[row, col],
        out_shape=[jax.ShapeDtypeStruct((t, d), F32), jax.ShapeDtypeStruct((1, d), F32)],
        compiler_params=_cparams(("arbitrary",)),
    )(y, target)


def _hgrn_consts():
    c = A_CHUNK
    shift = SUB.bit_length() - 1
    r = lax.broadcasted_iota(jnp.int32, (c, c), 0)
    s = lax.broadcasted_iota(jnp.int32, (c, c), 1)
    sub_r = lax.shift_right_logical(r, shift)
    incl = s <= r
    masks = [jnp.logical_and(sub_r == i, incl) for i in range(c // SUB)]
    rev_incl = jnp.where(s >= r, 1.0, 0.0).astype(BF16)
    r2 = lax.broadcasted_iota(jnp.int32, (2 * c + 8, c), 0)
    s2 = lax.broadcasted_iota(jnp.int32, (2 * c + 8, c), 1)
    sub_start = lax.shift_left(lax.shift_right_logical(r2 - c, shift), shift)
    running = jnp.where(s2 <= r2, 1.0, 0.0)
    before = jnp.where(s2 < sub_start, 1.0, 0.0)
    stack = jnp.where(r2 < c, running, jnp.where(r2 < 2 * c, before, 1.0)).astype(BF16)
    return stack, masks, incl, rev_incl


def _hgrn_gates(q_raw, f_raw, lbv, stack):
    sg = jax.nn.sigmoid(f_raw)
    sgn = jax.nn.sigmoid(-f_raw)
    f = lbv + (1.0 - lbv) * sg
    logf = jnp.log(jnp.maximum(f, TINY))
    return dict(sg=sg, sgn=sgn, f=f, k=(1.0 - lbv) * sgn, q=_silu(q_raw), bb=_ones_left(stack, logf))


def _hgrn_chunk(q_raw, f_raw, lbv, stack):
    return _hgrn_decays(_hgrn_gates(q_raw, f_raw, lbv, stack))


def _hgrn_decays(gates):
    c = A_CHUNK
    sg, sgn, f, k, q, bb = (gates[n] for n in ("sg", "sgn", "f", "k", "q", "bb"))
    b = bb[:c]
    bsrow = bb[c:2 * c]
    b_end = bb[2 * c:2 * c + 1]
    e_sub = jnp.exp(b - bsrow)
    e_b = jnp.exp(b)
    e_end = jnp.exp(b_end - b)
    qs = q * e_sub
    q_in = q * e_b
    kend = k * e_end
    kfac = [jnp.exp(jnp.minimum(bsrow[i * SUB:i * SUB + 1] - b, EXP_CLAMP)) for i in range(c // SUB)]
    return dict(sg=sg, sgn=sgn, f=f, k=k, q=q, b=b, b_end=b_end, e_sub=e_sub, e_b=e_b, e_end=e_end,
                qs=qs, q_in=q_in, kend=kend, kfac=kfac)


def _hgrn_scores(ch, masks):
    qs_b = ch["qs"].astype(BF16)
    a = None
    for i, mk in enumerate(masks):
        ki = (ch["k"] * ch["kfac"][i]).astype(BF16)
        part = jnp.where(mk, _dot_nt(qs_b, ki), 0.0)
        a = part if a is None else a + part
    return a


def _hgrn_fwd(p, lb, hn2, name):
    t = p.shape[0]
    tb = _rows(t)
    nt = t // tb
    nc = tb // A_CHUNK
    c = A_CHUNK

    def body(q_ref, f_ref, i_ref, g_ref, lb_ref, hn_ref, y_ref, o_ref, st_ref, s_scr):
        j = pl.program_id(1)

        @pl.when(j == 0)
        def _():
            s_scr[...] = jnp.zeros_like(s_scr)

        stack, masks, _, _ = _hgrn_consts()
        units = [(ci, hh) for ci in range(nc) for hh in range(2)]
        lsl = [slice(A_KDIM * hh, A_KDIM * (hh + 1)) for hh in range(2)]
        hsl = [slice(A_VDIM * hh, A_VDIM * (hh + 1)) for hh in range(2)]
        rows = [pl.ds(ci * c, c) for ci in range(nc)]
        gates = {u: _hgrn_gates(q_ref[rows[u[0]], lsl[u[1]]], f_ref[rows[u[0]], lsl[u[1]]], lb_ref[:, lsl[u[1]]], stack)
                 for u in units}
        ch = {u: _hgrn_decays(gates[u]) for u in units}
        v = {u: i_ref[rows[u[0]], hsl[u[1]]].astype(BF16) for u in units}
        a = {u: _hgrn_scores(ch[u], masks).astype(BF16) for u in units}
        grow = {u: _dot_tn(v[u], ch[u]["kend"].astype(BF16)) for u in units}
        states = [s_scr[0], s_scr[1]]
        entering = {}
        for ci, hh in units:
            entering[ci, hh] = states[hh]
            st_ref[hh, ci] = states[hh]
            states[hh] = states[hh] * jnp.exp(ch[ci, hh]["b_end"]) + grow[ci, hh]
        s_scr[0] = states[0]
        s_scr[1] = states[1]
        for u in units:
            o_ref[rows[u[0]], hsl[u[1]]] = (_dot_nt(ch[u]["q_in"].astype(BF16), entering[u].astype(BF16))
                                            + _dot(a[u], v[u]))
        for hh in range(2):
            hsl = slice(A_VDIM * hh, A_VDIM * (hh + 1))
            o = o_ref[:, hsl]
            r = lax.rsqrt(jnp.mean(o * o, axis=-1, keepdims=True) + EPS)
            y_ref[:, hsl] = (o * r * hn_ref[:, hsl] * _silu(g_ref[:, hsl])).astype(y_ref.dtype)

    w2 = 2 * A_KDIM
    return pl.pallas_call(
        body, name=name, grid=(A_HEADS // 2, nt),
        in_specs=[pl.BlockSpec((tb, w2), lambda h, j: (j, COL_AQ // w2 + h)),
                  pl.BlockSpec((tb, w2), lambda h, j: (j, COL_AF // w2 + h)),
                  pl.BlockSpec((tb, LANES), lambda h, j: (j, COL_AI // LANES + h)),
                  pl.BlockSpec((tb, LANES), lambda h, j: (j, COL_AG // LANES + h)),
                  pl.BlockSpec((1, w2), lambda h, j: (0, h)),
                  pl.BlockSpec((1, LANES), lambda h, j: (0, 0))],
        out_specs=[pl.BlockSpec((tb, LANES), lambda h, j: (j, h)),
                   pl.BlockSpec((tb, LANES), lambda h, j: (j, h)),
                   pl.BlockSpec((2, nc, A_VDIM, A_KDIM), lambda h, j: (h, j, 0, 0))],
        out_shape=[jax.ShapeDtypeStruct((t, A_V), BF16), jax.ShapeDtypeStruct((t, A_V), F32),
                   jax.ShapeDtypeStruct((A_HEADS, t // c, A_VDIM, A_KDIM), F32)],
        scratch_shapes=[pltpu.VMEM((2, A_VDIM, A_KDIM), F32)],
        compiler_params=_cparams(("parallel", "arbitrary")),
    )(p, p, p, p, lb, hn2)


def _hgrn_bwd(p, lb, hn2, o_raw, states, dya, name):
    t = p.shape[0]
    tb = _rows(t)
    nt = t // tb
    nc = tb // A_CHUNK
    c = A_CHUNK

    def body(q_ref, f_ref, i_ref, g_ref, lb_ref, hn_ref, o_ref, st_ref, dy_ref,
             dq_ref, df_ref, di_ref, dg_ref, dlb_ref, dhn_ref, ds_scr, do_scr):
        j = pl.program_id(1)

        @pl.when(j == 0)
        def _():
            ds_scr[...] = jnp.zeros_like(ds_scr)
            dlb_ref[...] = jnp.zeros_like(dlb_ref)
            dhn_ref[...] = jnp.zeros_like(dhn_ref)

        stack, masks, incl, rev_incl = _hgrn_consts()
        for hh in range(2):
            hsl = slice(A_VDIM * hh, A_VDIM * (hh + 1))
            o = o_ref[:, hsl]
            g = g_ref[:, hsl]
            dy = dy_ref[:, hsl].astype(F32)
            hn = hn_ref[:, hsl]
            r = lax.rsqrt(jnp.mean(o * o, axis=-1, keepdims=True) + EPS)
            oh = o * r
            sgate = _silu(g)
            dg_ref[:, hsl] = dy * oh * hn * _dsilu(g)
            dhn_ref[0, :, hsl] += jnp.sum(dy * oh * sgate, axis=0, keepdims=True)
            doh = dy * hn * sgate
            do_scr[:, hsl] = r * (doh - oh * jnp.mean(doh * oh, axis=-1, keepdims=True))

        units = [(ci, hh) for ci in reversed(range(nc)) for hh in range(2)]
        lsl = [slice(A_KDIM * hh, A_KDIM * (hh + 1)) for hh in range(2)]
        hsl = [slice(A_VDIM * hh, A_VDIM * (hh + 1)) for hh in range(2)]
        rows = [pl.ds(ci * c, c) for ci in range(nc)]
        q_raw = {u: q_ref[rows[u[0]], lsl[u[1]]] for u in units}
        gates = {u: _hgrn_gates(q_raw[u], f_ref[rows[u[0]], lsl[u[1]]], lb_ref[:, lsl[u[1]]], stack) for u in units}
        ch = {u: _hgrn_decays(gates[u]) for u in units}
        v = {u: i_ref[rows[u[0]], hsl[u[1]]].astype(BF16) for u in units}
        do_b = {u: do_scr[rows[u[0]], hsl[u[1]]].astype(BF16) for u in units}
        st = {u: st_ref[u[1], u[0]] for u in units}
        qs_b = {u: ch[u]["qs"].astype(BF16) for u in units}
        a_b = {u: _hgrn_scores(ch[u], masks).astype(BF16) for u in units}
        da = {u: jnp.where(incl, _dot_nt(do_b[u], v[u]), 0.0) for u in units}
        dq_x = {u: _dot(do_b[u], st[u].astype(BF16)) for u in units}
        grow = {u: _dot_tn(do_b[u], ch[u]["q_in"].astype(BF16)) for u in units}
        dstates = [ds_scr[0], ds_scr[1]]
        leaving = {}
        for ci, hh in units:
            leaving[ci, hh] = dstates[hh]
            dstates[hh] = dstates[hh] * jnp.exp(ch[ci, hh]["b_end"]) + grow[ci, hh]
        for hh in range(2):
            ds_scr[hh] = dstates[hh]
        dst_b = {u: leaving[u].astype(BF16) for u in units}
        dv = {u: _dot_tn(a_b[u], do_b[u]) + _dot_nt(ch[u]["kend"].astype(BF16), dst_b[u]) for u in units}
        dk_x = {u: _dot(v[u], dst_b[u]) for u in units}
        dlb_acc = [jnp.zeros((1, A_KDIM), F32), jnp.zeros((1, A_KDIM), F32)]
        for u in units:
            ci, hh = u
            cu = ch[u]
            lbv = lb_ref[:, lsl[hh]]
            dq_i = None
            dk_i = None
            kdk_i = None
            for i, mk in enumerate(masks):
                dam = jnp.where(mk, da[u], 0.0).astype(BF16)
                ki = (cu["k"] * cu["kfac"][i]).astype(BF16)
                pq = _dot(dam, ki)
                pk = _dot_tn(dam, qs_b[u])
                dq_i = pq if dq_i is None else dq_i + pq
                dk_i = cu["kfac"][i] * pk if dk_i is None else dk_i + cu["kfac"][i] * pk
                kdk_i = ki.astype(F32) * pk if kdk_i is None else kdk_i + ki.astype(F32) * pk
            dq = cu["e_sub"] * dq_i + cu["e_b"] * dq_x[u]
            dk = dk_i + cu["e_end"] * dk_x[u]
            kx = cu["kend"] * dk_x[u]
            db = (qs_b[u].astype(F32) * dq_i + cu["q_in"] * dq_x[u]) - (kdk_i + kx)
            later = (jnp.exp(cu["b_end"]) * jnp.sum(leaving[u] * st[u], axis=0, keepdims=True)
                     + jnp.sum(kx, axis=0, keepdims=True))
            dlogf = later + _ones_left(rev_incl, db)
            dfv = jnp.where(cu["f"] > TINY, dlogf / cu["f"], 0.0)
            dq_ref[rows[ci], lsl[hh]] = dq * _dsilu(q_raw[u])
            df_ref[rows[ci], lsl[hh]] = (1.0 - lbv) * cu["sg"] * cu["sgn"] * (dfv - dk)
            dlb_acc[hh] = dlb_acc[hh] + jnp.sum(dfv * (1.0 - cu["sg"]) - dk * cu["sgn"], axis=0, keepdims=True)
            di_ref[rows[ci], hsl[hh]] = dv[u]
        for hh in range(2):
            dlb_ref[:, A_KDIM * hh:A_KDIM * (hh + 1)] += dlb_acc[hh]

    w2 = 2 * A_KDIM
    rev = lambda j: nt - 1 - j
    return pl.pallas_call(
        body, name=name, grid=(A_HEADS // 2, nt),
        in_specs=[pl.BlockSpec((tb, w2), lambda h, j: (rev(j), COL_AQ // w2 + h)),
                  pl.BlockSpec((tb, w2), lambda h, j: (rev(j), COL_AF // w2 + h)),
                  pl.BlockSpec((tb, LANES), lambda h, j: (rev(j), COL_AI // LANES + h)),
                  pl.BlockSpec((tb, LANES), lambda h, j: (rev(j), COL_AG // LANES + h)),
                  pl.BlockSpec((1, w2), lambda h, j: (0, h)),
                  pl.BlockSpec((1, LANES), lambda h, j: (0, 0)),
                  pl.BlockSpec((tb, LANES), lambda h, j: (rev(j), h)),
                  pl.BlockSpec((2, nc, A_VDIM, A_KDIM), lambda h, j: (h, rev(j), 0, 0)),
                  pl.BlockSpec((tb, LANES), lambda h, j: (rev(j), h))],
        out_specs=[pl.BlockSpec((tb, w2), lambda h, j: (rev(j), h)),
                   pl.BlockSpec((tb, w2), lambda h, j: (rev(j), h)),
                   pl.BlockSpec((tb, LANES), lambda h, j: (rev(j), h)),
                   pl.BlockSpec((tb, LANES), lambda h, j: (rev(j), h)),
                   pl.BlockSpec((1, w2), lambda h, j: (0, h)),
                   pl.BlockSpec((1, 1, LANES), lambda h, j: (h, 0, 0))],
        out_shape=[jax.ShapeDtypeStruct((t, A_QK), F32), jax.ShapeDtypeStruct((t, A_QK), F32),
                   jax.ShapeDtypeStruct((t, A_V), F32), jax.ShapeDtypeStruct((t, A_V), F32),
                   jax.ShapeDtypeStruct((1, A_QK), F32), jax.ShapeDtypeStruct((A_HEADS // 2, 1, LANES), F32)],
        scratch_shapes=[pltpu.VMEM((2, A_VDIM, A_KDIM), F32), pltpu.VMEM((tb, LANES), F32)],
        compiler_params=_cparams(("parallel", "arbitrary")),
    )(p, p, p, p, lb, hn2, o_raw, states, dya)


BLK = 128
SCALE = HDIM ** -0.5
SB_CHUNK = 4


def _softplus(z):
    return jnp.maximum(z, 0.0) + jnp.log(1.0 + jnp.exp(-jnp.abs(z)))


def _sb_sum_matrix(keep, with_total=False):
    width = 2 * BLK if with_total else BLK
    sp = lax.broadcasted_iota(jnp.int32, (BLK, width), 0)
    s = lax.broadcasted_iota(jnp.int32, (BLK, width), 1)
    return jnp.where(jnp.logical_or(s >= BLK, keep(sp, s)), 1.0, 0.0).astype(BF16)


def _lanes(col):
    return jnp.broadcast_to(col, (BLK, BLK))


def _sb_fwd(p, kv, name, gather=None):
    t = p.shape[0]
    nq = t // BLK
    nh = B_HEADS // 2
    cw = SB_CHUNK * BLK
    fused = gather is not None
    n = len(gather) if fused else 0

    def body(*refs):
        q_ref, kb, vb = refs[:3]
        o_ref, tot_ref = refs[3 + n:5 + n]
        zbuf, stage, sbuf, abuf = refs[5 + 2 * n:9 + 2 * n]
        hp = pl.program_id(0)
        qi = pl.program_id(1)
        if fused:
            g = _Many(_Gather, refs[3:3 + n], refs[5 + n:5 + 2 * n], *refs[9 + 2 * n:])
            pl.when(jnp.logical_and(hp == 0, qi == 0))(g.start)
            pl.when(jnp.logical_and(hp == nh - 1, qi == 0))(g.forward)

        @pl.when(qi == 0)
        def _():
            abuf[...] = jnp.zeros_like(abuf)

        row = lax.broadcasted_iota(jnp.int32, (BLK, BLK), 0)
        col = lax.broadcasted_iota(jnp.int32, (BLK, BLK), 1)
        sums = _sb_sum_matrix(lambda sp, s: sp >= s, True)
        hsl = [slice(HDIM * h, HDIM * (h + 1)) for h in range(2)]
        nchunk = qi // SB_CHUNK + 1
        for h in range(2):
            zbuf[h] = _dot_nt((q_ref[:, hsl[h]] * SCALE).astype(BF16), kb[:, hsl[h]])

        col_minus_row = col - row

        def causal(j):
            return col_minus_row < (qi - j) * BLK

        def l_pass(c, carry):
            for b in range(SB_CHUNK):
                j = c * SB_CHUNK + b
                off = pl.multiple_of(j * BLK, BLK)
                mask = causal(j)
                for h in range(2):
                    lm = jnp.where(mask, -_softplus(zbuf[h, :, pl.ds(off, BLK)]), 0.0)
                    stage[h, pl.ds(off, BLK), :] = lm.astype(BF16)
            return carry

        lax.fori_loop(0, nchunk, l_pass, 0)

        def sum_pass(c, carry):
            rows = pl.ds(pl.multiple_of(c * cw, cw), cw)
            for h in range(2):
                sbuf[h, rows, :] = _dot(stage[h, rows, :], sums)
            return carry

        lax.fori_loop(0, nchunk, sum_pass, 0)

        def a_pass(it, carry):
            c = nchunk - 1 - it
            runs = list(carry)
            for b in reversed(range(SB_CHUNK)):
                j = c * SB_CHUNK + b
                off = pl.multiple_of(j * BLK, BLK)
                mask = causal(j)
                for h in range(2):
                    s = sbuf[h, pl.ds(off, BLK), :BLK]
                    a = jnp.where(mask, jnp.exp(zbuf[h, :, pl.ds(off, BLK)] + s + runs[h]), 0.0)
                    abuf[h, :, pl.ds(off, BLK)] = a.astype(BF16)
                    runs[h] = runs[h] + sbuf[h, pl.ds(off, BLK), BLK:]
            return tuple(runs)

        zero = jnp.zeros((BLK, BLK), F32)
        runs = lax.fori_loop(0, nchunk, a_pass, (zero, zero))
        for h in range(2):
            tot_ref[:, hsl[h]] = runs[h][:, :HDIM]
            o_ref[:, hsl[h]] = _dot(abuf[h], vb[:, hsl[h]])
        if fused:
            pl.when(jnp.logical_and(hp == nh - 1, qi == nq - 1))(g.finish)

    out_blk = pl.BlockSpec((BLK, LANES), lambda h, i: (i, h))
    hbm = pl.BlockSpec(memory_space=pl.ANY)
    in_specs = [pl.BlockSpec((BLK, LANES), lambda h, i: (i, COL_BQ // LANES + h)),
                pl.BlockSpec((t, LANES), lambda h, i: (0, h)),
                pl.BlockSpec((t, LANES), lambda h, i: (0, B_W // LANES + h))]
    out_shape = [jax.ShapeDtypeStruct((t, B_W), F32)] * 2
    scratch = [pltpu.VMEM((2, BLK, t), F32), pltpu.VMEM((2, t, BLK), BF16),
               pltpu.VMEM((2, t, 2 * BLK), F32), pltpu.VMEM((2, BLK, t), BF16)]
    if fused:
        out_shape = out_shape + _gathered_shapes(gather)
    return pl.pallas_call(
        body, name=name, grid=(nh, nq),
        in_specs=in_specs + [hbm] * n,
        out_specs=[out_blk, out_blk] + [hbm] * n,
        out_shape=out_shape,
        scratch_shapes=scratch + (_comm_sems(n) if fused else []),
        compiler_params=_cparams(("arbitrary", "arbitrary")),
    )(p, kv, kv, *(gather if fused else []))


def _sb_bwd(p, kv, tot, do, name, exchange=None):
    t = p.shape[0]
    nq = t // BLK
    nh = B_HEADS // 2
    cw = SB_CHUNK * BLK
    fused = exchange is not None
    n = len(exchange) if fused else 0

    def body(*refs):
        q_ref, kb, vb, tot_ref, do_ref = refs[:5]
        dq_ref, dk_ref, dv_ref = refs[5 + n:8 + n]
        zbuf, dabuf, lbuf, stage, sbuf, abuf, dzbuf, dkt, dvt = refs[8 + 2 * n:17 + 2 * n]
        hp = pl.program_id(0)
        qi = pl.program_id(1)
        if fused:
            ex = _Many(_Exchange, refs[5:5 + n], refs[8 + n:8 + 2 * n], *refs[17 + 2 * n:])
            pl.when(jnp.logical_and(hp == 0, qi == 0))(ex.start)

        @pl.when(qi == 0)
        def _():
            dkt[...] = jnp.zeros_like(dkt)
            dvt[...] = jnp.zeros_like(dvt)
            dzbuf[...] = jnp.zeros_like(dzbuf)
            abuf[...] = jnp.zeros_like(abuf)

        row = lax.broadcasted_iota(jnp.int32, (BLK, BLK), 0)
        col = lax.broadcasted_iota(jnp.int32, (BLK, BLK), 1)
        sums = _sb_sum_matrix(lambda sp, s: sp <= s)
        hsl = [slice(HDIM * h, HDIM * (h + 1)) for h in range(2)]
        dob = [do_ref[:, hsl[h]].astype(BF16) for h in range(2)]
        total =[jnp.concatenate([tot_ref[:, hsl[h]], tot_ref[:, hsl[h]]], axis=1) for h in range(2)]
        nchunk = qi // SB_CHUNK + 1
        for h in range(2):
            zbuf[h] = _dot_nt((q_ref[:, hsl[h]] * SCALE).astype(BF16), kb[:, hsl[h]])
            dabuf[h] = _dot_nt(dob[h], vb[:, hsl[h]])

        col_minus_row = col - row

        def causal(j):
            return col_minus_row < (qi - j) * BLK

        def blocks(c):
            for b in range(SB_CHUNK):
                j = c * SB_CHUNK + b
                yield j, pl.ds(pl.multiple_of(j * BLK, BLK), BLK)

        def l_pass(c, carry):
            for j, blk_ in blocks(c):
                mask = causal(j)
                for h in range(2):
                    lm = jnp.where(mask, -_softplus(zbuf[h, :, blk_]), 0.0)
                    lbuf[h, :, blk_] = lm
                    stage[h, blk_, :] = lm.astype(BF16)
            return carry

        lax.fori_loop(0, nchunk, l_pass, 0)

        def sum_pass():
            def run_(c, carry):
                rows = pl.ds(pl.multiple_of(c * cw, cw), cw)
                for h in range(2):
                    sbuf[h, rows, :] = _dot(stage[h, rows, :], sums)
                return carry
            lax.fori_loop(0, nchunk, run_, 0)

        sum_pass()

        def g_pass(c, carry):
            runs = list(carry)
            for j, blk_ in blocks(c):
                mask = causal(j)
                for h in range(2):
                    upto = sbuf[h, blk_, :]
                    log_a = zbuf[h, :, blk_] + lbuf[h, :, blk_] + (total[h] - runs[h] - upto)
                    a = jnp.where(mask, jnp.exp(log_a), 0.0)
                    abuf[h, :, blk_] = a.astype(BF16)
                    g = a * dabuf[h, :, blk_]
                    dabuf[h, :, blk_] = g
                    stage[h, blk_, :] = g.astype(BF16)
                    runs[h] = runs[h] + _lanes(upto[:, BLK - 1:BLK])
            return tuple(runs)

        zero = jnp.zeros((BLK, BLK), F32)
        lax.fori_loop(0, nchunk, g_pass, (zero, zero))
        sum_pass()

        def dz_pass(c, carry):
            runs = list(carry)
            for j, blk_ in blocks(c):
                mask = causal(j)
                for h in range(2):
                    lm = lbuf[h, :, blk_]
                    g = dabuf[h, :, blk_]
                    upto = sbuf[h, blk_, :]
                    before = runs[h] + upto - g
                    dz = jnp.where(mask, g * jnp.exp(lm) - jnp.exp(zbuf[h, :, blk_] + lm) * before, 0.0)
                    dzbuf[h, :, blk_] = (dz * SCALE).astype(BF16)
                    runs[h] = runs[h] + _lanes(upto[:, BLK - 1:BLK])
            return tuple(runs)

        lax.fori_loop(0, nchunk, dz_pass, (zero, zero))
        for h in range(2):
            dq_ref[:, hsl[h]] = _dot(dzbuf[h], kb[:, hsl[h]])
        q_t = q_ref[...].T.astype(BF16)
        do_t = do_ref[...].T.astype(BF16)
        for h in range(2):
            dkt[hsl[h], :] += _dot(q_t[hsl[h], :], dzbuf[h])
            dvt[hsl[h], :] += _dot(do_t[hsl[h], :], abuf[h])

        @pl.when(qi == nq - 1)
        def _():
            dk_ref[...] = dkt[...].T
            dv_ref[...] = dvt[...].T

        if fused:
            pl.when(jnp.logical_and(hp == nh - 1, qi == nq - 1))(ex.finish)

    blk = lambda h, i: (i, h)
    whole = lambda h, i: (0, h)
    hbm = pl.BlockSpec(memory_space=pl.ANY)
    in_specs = [pl.BlockSpec((BLK, LANES), lambda h, i: (i, COL_BQ // LANES + h)),
                pl.BlockSpec((t, LANES), lambda h, i: (0, h)),
                pl.BlockSpec((t, LANES), lambda h, i: (0, B_W // LANES + h)),
                pl.BlockSpec((BLK, LANES), blk), pl.BlockSpec((BLK, LANES), blk)]
    out_specs = [pl.BlockSpec((BLK, LANES), blk), pl.BlockSpec((t, LANES), whole), pl.BlockSpec((t, LANES), whole)]
    out_shape = [jax.ShapeDtypeStruct((t, B_W), F32)] * 3
    scratch = [pltpu.VMEM((2, BLK, t), F32), pltpu.VMEM((2, BLK, t), F32), pltpu.VMEM((2, BLK, t), F32),
               pltpu.VMEM((2, t, BLK), BF16), pltpu.VMEM((2, t, BLK), F32), pltpu.VMEM((2, BLK, t), BF16),
               pltpu.VMEM((2, BLK, t), BF16), pltpu.VMEM((LANES, t), F32), pltpu.VMEM((LANES, t), F32)]
    if fused:
        out_shape = out_shape + [jax.ShapeDtypeStruct(e.shape, e.dtype) for e in exchange]
    return pl.pallas_call(
        body, name=name, grid=(nh, nq),
        in_specs=in_specs + [hbm] * n,
        out_specs=out_specs + [hbm] * n,
        out_shape=out_shape,
        scratch_shapes=scratch + (_comm_sems(n) if fused else []),
        compiler_params=_cparams(("arbitrary", "arbitrary")),
    )(p, kv, kv, tot, do, *(exchange if fused else []))


def _alibi_slopes(n):
    def pow2(m):
        start = 2.0 ** (-8.0 / m)
        return [start ** (i + 1) for i in range(m)]
    if math.log2(n).is_integer():
        s = pow2(n)
    else:
        c = 2 ** int(math.floor(math.log2(n)))
        s = pow2(c) + pow2(2 * c)[0::2][: n - c]
    return sorted(s, reverse=True)


def _dil_scores(qh, kh, sl, prev, exists=None):
    row = lax.broadcasted_iota(jnp.int32, (BLK, BLK), 0)
    col = lax.broadcasted_iota(jnp.int32, (BLK, BLK), 1)
    dist = row - col + (BLK if prev else 0)
    if prev:
        valid = (col - row) >= jnp.where(exists, 0, 2 * BLK)
    else:
        valid = col <= row
    s = _dot_nt(qh, kh) - sl * dist.astype(F32)
    return s, valid


DIL_RESIDUES = 2


def _dil_rows(rho, r):
    return pl.ds(rho, BLK, stride=r) if r > 1 else pl.ds(0, BLK)


def _dil_fwd(p, gi, name):
    t = p.shape[0]
    _, r = C_GROUPS[gi]
    sbr = BLK * r
    nsb = t // sbr
    slope_cols = _slope_cols(gi)

    def body(q_ref, kc_ref, kp_ref, vc_ref, vp_ref, sl_ref, o_ref, lse_ref):
        i = pl.program_id(1)

        hsl = [slice(HDIM * h, HDIM * (h + 1)) for h in range(2)]
        sl = [sl_ref[:, HDIM * h:HDIM * h + 1] for h in range(2)]
        per_trip = min(r, DIL_RESIDUES)

        def residues(it, carry):
            rows = [_dil_rows(it * per_trip + dr, r) for dr in range(per_trip)]
            units = [(dr, h) for dr in range(per_trip) for h in range(2)]
            blocks = {dr: [ref[rows[dr], :] for ref in (q_ref, kc_ref, kp_ref, vc_ref, vp_ref)] for dr in range(per_trip)}
            qh = {u: (blocks[u[0]][0][:, hsl[u[1]]] * SCALE).astype(BF16) for u in units}
            sc = {u: _dil_scores(qh[u], blocks[u[0]][1][:, hsl[u[1]]].astype(BF16), sl[u[1]], False) for u in units}
            sp = {u: _dil_scores(qh[u], blocks[u[0]][2][:, hsl[u[1]]].astype(BF16), sl[u[1]], True, i > 0) for u in units}
            pc, pp, den, lse = {}, {}, {}, {}
            for u in units:
                s_c = jnp.where(sc[u][1], sc[u][0], NEG_BIG)
                s_p = jnp.where(sp[u][1], sp[u][0], NEG_BIG)
                m = jnp.maximum(jnp.max(s_c, axis=1, keepdims=True), jnp.max(s_p, axis=1, keepdims=True))
                pc[u] = jnp.exp(s_c - m)
                pp[u] = jnp.exp(s_p - m)
                den[u] = jnp.sum(pc[u], axis=1, keepdims=True) + jnp.sum(pp[u], axis=1, keepdims=True)
                lse[u] = jnp.broadcast_to(m + jnp.log(den[u]), (BLK, HDIM))
            o = {u: (_dot(pc[u].astype(BF16), blocks[u[0]][3][:, hsl[u[1]]].astype(BF16))
                     + _dot(pp[u].astype(BF16), blocks[u[0]][4][:, hsl[u[1]]].astype(BF16))) / den[u] for u in units}
            for dr in range(per_trip):
                o_ref[rows[dr], :] = jnp.concatenate([o[dr, 0], o[dr, 1]], axis=1)
                lse_ref[rows[dr], :] = jnp.concatenate([lse[dr, 0], lse[dr, 1]], axis=1)
            return carry

        lax.fori_loop(0, r // per_trip, residues, 0)

    def at(col0, pick):
        return pl.BlockSpec((sbr, LANES), lambda c, i: (pick(i), col0 // LANES + c))

    cur = lambda i: i
    prv = lambda i: jnp.maximum(i - 1, 0)
    cq, ck, cv = COL_CQ + gi * C_OUT, COL_CK + gi * C_OUT, COL_CV + gi * C_OUT
    out = pl.BlockSpec((sbr, LANES), lambda c, i: (i, c))
    return pl.pallas_call(
        body, name=name, grid=(C_OUT // LANES, nsb),
        in_specs=[at(cq, cur), at(ck, cur), at(ck, prv), at(cv, cur), at(cv, prv),
                  pl.BlockSpec((1, LANES), lambda c, i: (0, c))],
        out_specs=[out, out], out_shape=[jax.ShapeDtypeStruct((t, C_OUT), F32)] * 2,
        compiler_params=_cparams(("parallel", "parallel")),
    )(p, p, p, p, p, slope_cols)


def _dil_bwd(p, do, o, lse, gi, name):
    t = p.shape[0]
    _, r = C_GROUPS[gi]
    sbr = BLK * r
    nsb = t // sbr
    slope_cols = _slope_cols(gi)

    def body(q_ref, qn_ref, kc_ref, kp_ref, vc_ref, vp_ref, do_ref, don_ref, o_ref, on_ref, l_ref, ln_ref, sl_ref,
             dq_ref, dk_ref, dv_ref):
        i = pl.program_id(1)
        has_prev = i > 0
        has_next = i < nsb - 1

        hsl = [slice(HDIM * h, HDIM * (h + 1)) for h in range(2)]
        sl = [sl_ref[:, HDIM * h:HDIM * h + 1] for h in range(2)]
        per_trip = min(r, DIL_RESIDUES)
        in_refs = (q_ref, qn_ref, kc_ref, kp_ref, vc_ref, vp_ref, do_ref, don_ref, o_ref, on_ref, l_ref, ln_ref)

        def residues(it, carry):
            rows = [_dil_rows(it * per_trip + dr, r) for dr in range(per_trip)]
            units = [(dr, h) for dr in range(per_trip) for h in range(2)]
            blocks = {dr: [ref[rows[dr], :] for ref in in_refs] for dr in range(per_trip)}
            part = lambda u, k: blocks[u[0]][k][:, hsl[u[1]]]
            qb = {u: part(u, 0).astype(BF16) for u in units}
            qnb = {u: part(u, 1).astype(BF16) for u in units}
            qh = {u: (part(u, 0) * SCALE).astype(BF16) for u in units}
            qnh = {u: (part(u, 1) * SCALE).astype(BF16) for u in units}
            kc = {u: part(u, 2).astype(BF16) for u in units}
            kp = {u: part(u, 3).astype(BF16) for u in units}
            vc = {u: part(u, 4).astype(BF16) for u in units}
            vp = {u: part(u, 5).astype(BF16) for u in units}
            dob = {u: part(u, 6).astype(BF16) for u in units}
            donb = {u: part(u, 7).astype(BF16) for u in units}
            delta = {u: jnp.sum(part(u, 6) * part(u, 8), axis=1, keepdims=True) for u in units}
            deltan = {u: jnp.sum(part(u, 7) * part(u, 9), axis=1, keepdims=True) for u in units}
            lse_c = {u: part(u, 10)[:, :1] for u in units}
            lse_n = {u: part(u, 11)[:, :1] for u in units}
            s_cc = {u: _dil_scores(qh[u], kc[u], sl[u[1]], False) for u in units}
            s_cp = {u: _dil_scores(qh[u], kp[u], sl[u[1]], True, has_prev) for u in units}
            s_nc = {u: _dil_scores(qnh[u], kc[u], sl[u[1]], True, has_next) for u in units}
            da_cc = {u: _dot_nt(dob[u], vc[u]) for u in units}
            da_cp = {u: _dot_nt(dob[u], vp[u]) for u in units}
            da_nc = {u: _dot_nt(donb[u], vc[u]) for u in units}

            def prob(s_ok, lse_col):
                s, ok = s_ok
                return jnp.where(ok, jnp.exp(jnp.where(ok, s, NEG_BIG) - lse_col), 0.0)

            p_cc = {u: prob(s_cc[u], lse_c[u]) for u in units}
            p_cp = {u: prob(s_cp[u], lse_c[u]) for u in units}
            p_nc = {u: prob(s_nc[u], lse_n[u]) for u in units}
            ds_cc = {u: (p_cc[u] * (da_cc[u] - delta[u]) * SCALE).astype(BF16) for u in units}
            ds_cp = {u: (p_cp[u] * (da_cp[u] - delta[u]) * SCALE).astype(BF16) for u in units}
            ds_nc = {u: (p_nc[u] * (da_nc[u] - deltan[u]) * SCALE).astype(BF16) for u in units}
            dq = {u: _dot(ds_cc[u], kc[u]) + _dot(ds_cp[u], kp[u]) for u in units}
            dk = {u: _dot_tn(ds_cc[u], qb[u]) + _dot_tn(ds_nc[u], qnb[u]) for u in units}
            dv = {u: _dot_tn(p_cc[u].astype(BF16), dob[u]) + _dot_tn(p_nc[u].astype(BF16), donb[u]) for u in units}
            for dr in range(per_trip):
                dq_ref[rows[dr], :] = jnp.concatenate([dq[dr, 0], dq[dr, 1]], axis=1)
                dk_ref[rows[dr], :] = jnp.concatenate([dk[dr, 0], dk[dr, 1]], axis=1)
                dv_ref[rows[dr], :] = jnp.concatenate([dv[dr, 0], dv[dr, 1]], axis=1)
            return carry

        lax.fori_loop(0, r // per_trip, residues, 0)

    def at(col0, pick):
        return pl.BlockSpec((sbr, LANES), lambda c, i: (pick(i), col0 // LANES + c))

    cur = lambda i: i
    prv = lambda i: jnp.maximum(i - 1, 0)
    nxt = lambda i: jnp.minimum(i + 1, nsb - 1)
    cq, ck, cv = COL_CQ + gi * C_OUT, COL_CK + gi * C_OUT, COL_CV + gi * C_OUT
    return pl.pallas_call(
        body, name=name, grid=(C_OUT // LANES, nsb),
        in_specs=[at(cq, cur), at(cq, nxt), at(ck, cur), at(ck, prv), at(cv, cur), at(cv, prv),
                  at(0, cur), at(0, nxt), at(0, cur), at(0, nxt), at(0, cur), at(0, nxt),
                  pl.BlockSpec((1, LANES), lambda c, i: (0, c))],
        out_specs=[at(0, cur)] * 3, out_shape=[jax.ShapeDtypeStruct((t, C_OUT), F32)] * 3,
        compiler_params=_cparams(("parallel", "parallel")),
    )(p, p, p, p, p, p, do, do, o, o, lse, lse, slope_cols)


def _dil_merge(os_, ls_, name):
    t, w = os_[0].shape
    tr = _rows(t)

    def body(o0, o1, o2, l0, l1, l2, y_ref, lse_ref):
        a, b, c = l0[...], l1[...], l2[...]
        m = jnp.maximum(jnp.maximum(a, b), c)
        ea, eb, ec = jnp.exp(a - m), jnp.exp(b - m), jnp.exp(c - m)
        den = ea + eb + ec
        y_ref[...] = (ea * o0[...] + eb * o1[...] + ec * o2[...]) / den
        lse_ref[...] = m + jnp.log(den)

    row = pl.BlockSpec((tr, w), lambda i: (i, 0))
    return pl.pallas_call(
        body, name=name, grid=(t // tr,), in_specs=[row] * 6, out_specs=[row, row],
        out_shape=[jax.ShapeDtypeStruct((t, w), F32)] * 2, compiler_params=_cparams(("parallel",)),
    )(*os_, *ls_)


def _gate_fwd(ys, gl, ws, name):
    t = gl.shape[0]
    d = gl.shape[1] // N_BRANCH
    tr = _rows(t)

    def body(ya, yb, yc, gl_ref, wa, wb, wc, m_ref):
        acc = None
        for i, (y, w) in enumerate(((ya, wa), (yb, wb), (yc, wc))):
            z = _dot(y[...].astype(BF16), w[...])
            term = jax.nn.sigmoid(gl_ref[:, i * d:(i + 1) * d]) * z
            acc = term if acc is None else acc + term
        m_ref[...] = acc.astype(m_ref.dtype)

    rows = [pl.BlockSpec((tr, y.shape[1]), lambda i: (i, 0)) for y in ys]
    wsp = [pl.BlockSpec(w.shape, lambda i: (0, 0)) for w in ws]
    return pl.pallas_call(
        body, name=name, grid=(t // tr,),
        in_specs=rows + [pl.BlockSpec((tr, N_BRANCH * d), lambda i: (i, 0))] + wsp,
        out_specs=pl.BlockSpec((tr, d), lambda i: (i, 0)), out_shape=jax.ShapeDtypeStruct((t, d), BF16),
        compiler_params=_cparams(("parallel",)),
    )(*ys, gl, *ws)


def _gate_bwd(dm, ys, gl, ws, name):
    t = gl.shape[0]
    d = gl.shape[1] // N_BRANCH
    tr = _rows(t)

    def body(dm_ref, ya, yb, yc, gl_ref, wa, wb, wc, dya, dyb, dyc, dgl_ref, dwa, dwb, dwc):
        step = pl.program_id(0)
        dmv = dm_ref[...].astype(F32)
        for i, (y, w, dy, dw) in enumerate(((ya, wa, dya, dwa), (yb, wb, dyb, dwb), (yc, wc, dyc, dwc))):
            yb16 = y[...].astype(BF16)
            z = _dot(yb16, w[...])
            sg = jax.nn.sigmoid(gl_ref[:, i * d:(i + 1) * d])
            dgl_ref[:, i * d:(i + 1) * d] = dmv * z * sg * (1.0 - sg)
            e = (dmv * sg).astype(BF16)
            dy[...] = _dot_nt(e, w[...])
            contrib = _dot_tn(yb16, e)

            @pl.when(step == 0)
            def _(dw=dw, contrib=contrib):
                dw[...] = contrib

            @pl.when(step > 0)
            def _(dw=dw, contrib=contrib):
                dw[...] += contrib

    rows = [pl.BlockSpec((tr, y.shape[1]), lambda i: (i, 0)) for y in ys]
    wsp = [pl.BlockSpec(w.shape, lambda i: (0, 0)) for w in ws]
    gsp = pl.BlockSpec((tr, N_BRANCH * d), lambda i: (i, 0))
    return pl.pallas_call(
        body, name=name, grid=(t // tr,),
        in_specs=[pl.BlockSpec((tr, d), lambda i: (i, 0))] + rows + [gsp] + wsp,
        out_specs=rows + [gsp] + wsp,
        out_shape=[jax.ShapeDtypeStruct(y.shape, F32) for y in ys] + [jax.ShapeDtypeStruct(gl.shape, F32)]
        + [jax.ShapeDtypeStruct(w.shape, F32) for w in ws],
        compiler_params=_cparams(("arbitrary",)),
    )(dm, *ys, gl, *ws)


def _adamw(w, m, v, gparts, name):
    r, c = w.shape
    n = gparts.shape[0]
    br = LANES if r % LANES == 0 else r
    c1 = 1.0 - ADAM_B1 ** ADAM_STEP
    c2 = 1.0 - ADAM_B2 ** ADAM_STEP

    def body(w_ref, m_ref, v_ref, g_ref, go_ref, d_ref, mo_ref, vo_ref):
        g = g_ref[0].astype(F32)
        for i in range(1, n):
            g = g + g_ref[i].astype(F32)
        mn = ADAM_B1 * m_ref[...] + (1.0 - ADAM_B1) * g
        vn = ADAM_B2 * v_ref[...] + (1.0 - ADAM_B2) * (g * g)
        go_ref[...] = g
        mo_ref[...] = mn
        vo_ref[...] = vn
        d_ref[...] = -ADAM_LR * ((mn / c1) / (jnp.sqrt(vn / c2) + ADAM_EPS) + ADAM_WD * w_ref[...])

    blk = pl.BlockSpec((br, c), lambda i: (i, 0))
    return pl.pallas_call(
        body, name=name, grid=(r // br,),
        in_specs=[blk, blk, blk, pl.BlockSpec((n, br, c), lambda i: (0, i, 0))],
        out_specs=[blk] * 4, out_shape=[jax.ShapeDtypeStruct((r, c), F32)] * 4,
        compiler_params=_cparams(("parallel",)),
    )(w, m, v, gparts)


def _my_coords():
    return lax.axis_index("x"), lax.axis_index("y"), lax.axis_index("c")


COMM_SEMS = [pltpu.SemaphoreType.DMA((7,)), pltpu.SemaphoreType.DMA((7,)), pltpu.SemaphoreType.DMA]


class _Gather:
    def __init__(self, x_ref, out_ref, send_sems, recv_sems, local_sem):
        self.x_ref, self.out_ref = x_ref, out_ref
        self.send_sems, self.recv_sems, self.local_sem = send_sems, recv_sems, local_sem
        self.m_per = x_ref.shape[0]
        x, y, c = _my_coords()
        self.c = c
        self.me, self.sibling = (x, y, c), (x, y, 1 - c)
        self.chips = [(1 - x, y), (x, 1 - y), (1 - x, 1 - y)]

    def rows(self, px, py, pc):
        return self.out_ref.at[pl.ds((4 * px + 2 * py + pc) * self.m_per, self.m_per), :]

    def copy(self, k, block, to, src=None):
        return pltpu.make_async_remote_copy(
            src_ref=self.rows(*block) if src is None else src, dst_ref=self.rows(*block),
            send_sem=self.send_sems.at[k], recv_sem=self.recv_sems.at[k], device_id=to, device_id_type=MESH)

    def mine(self):
        return pltpu.make_async_copy(self.x_ref, self.rows(*self.me), self.local_sem)

    def first(self):
        out = [self.copy(0, self.me, self.sibling, src=self.x_ref)]
        return out + [self.copy(1 + j, self.me, (*chip, self.c), src=self.x_ref) for j, chip in enumerate(self.chips)]

    def passed(self):
        return [self.copy(4 + j, (*chip, self.c), self.sibling) for j, chip in enumerate(self.chips)]

    def start(self):
        self.mine().start()
        for cp in self.first():
            cp.start()

    def forward(self):
        passed = self.passed()
        for j, chip in enumerate(self.chips):
            self.copy(1 + j, (*chip, self.c), self.me).wait_recv()
            passed[j].start()

    def finish(self):
        self.copy(0, self.sibling, self.me).wait_recv()
        for j, chip in enumerate(self.chips):
            self.copy(4 + j, (*chip, 1 - self.c), self.me).wait_recv()
        for cp in self.first() + self.passed():
            cp.wait_send()
        self.mine().wait()


class _Exchange:
    def __init__(self, send_ref, recv_ref, send_sems, recv_sems, local_sem):
        self.send_ref, self.recv_ref = send_ref, recv_ref
        self.send_sems, self.recv_sems, self.local_sem = send_sems, recv_sems, local_sem
        x, y, c = _my_coords()
        self.me = 4 * x + 2 * y + c
        self.peers = []
        for k in range(1, N_DEV):
            px = 1 - x if k & 4 else x
            py = 1 - y if k & 2 else y
            pc = 1 - c if k & 1 else c
            self.peers.append((4 * px + 2 * py + pc, (px, py, pc)))

    def mine(self):
        return pltpu.make_async_copy(self.send_ref.at[self.me], self.recv_ref.at[self.me], self.local_sem)

    def copy(self, k, src_slot, dst_slot):
        return pltpu.make_async_remote_copy(
            src_ref=self.send_ref.at[src_slot], dst_ref=self.recv_ref.at[dst_slot],
            send_sem=self.send_sems.at[k], recv_sem=self.recv_sems.at[k],
            device_id=self.peers[k][1], device_id_type=MESH)

    def start(self):
        self.mine().start()
        for k, (peer, _) in enumerate(self.peers):
            self.copy(k, peer, self.me).start()

    def finish(self):
        for k, (peer, _) in enumerate(self.peers):
            self.copy(k, peer, self.me).wait_send()
            self.copy(k, self.me, peer).wait_recv()
        self.mine().wait()


def _all_gather(x_shard, in_vmem, with_sum, name):
    m_per, n = x_shard.shape

    def body(x_ref, out_ref, *rest):
        if with_sum:
            sum_ref, send_sems, recv_sems, local_sem = rest
        else:
            send_sems, recv_sems, local_sem = rest
        g = _Gather(x_ref, out_ref, send_sems, recv_sems, local_sem)
        g.start()
        g.forward()
        g.finish()
        if with_sum:
            acc = out_ref[pl.ds(0, m_per), :]
            for d in range(1, N_DEV):
                acc = acc + out_ref[pl.ds(d * m_per, m_per), :]
            sum_ref[...] = acc

    space = pltpu.VMEM if in_vmem else pl.ANY
    out_shape = [jax.ShapeDtypeStruct((N_DEV * m_per, n), x_shard.dtype)]
    out_specs = [pl.BlockSpec(memory_space=space)]
    if with_sum:
        out_shape.append(jax.ShapeDtypeStruct((m_per, n), x_shard.dtype))
        out_specs.append(pl.BlockSpec(memory_space=pltpu.VMEM))
    res = pl.pallas_call(
        body, name=name, out_shape=out_shape, in_specs=[pl.BlockSpec(memory_space=space)], out_specs=out_specs,
        scratch_shapes=COMM_SEMS, compiler_params=pltpu.CompilerParams(vmem_limit_bytes=VMEM_LIMIT),
    )(x_shard)
    return res if with_sum else res[0]


def _comm_sems(n):
    return [pltpu.SemaphoreType.DMA((n, 7)), pltpu.SemaphoreType.DMA((n, 7)), pltpu.SemaphoreType.DMA((n,))]


class _Many:
    def __init__(self, kind, ins, outs, send_sems, recv_sems, local_sems):
        self.parts = [kind(i, o, send_sems.at[b], recv_sems.at[b], local_sems.at[b])
                      for b, (i, o) in enumerate(zip(ins, outs))]

    def start(self):
        for part in self.parts:
            part.start()

    def forward(self):
        for part in self.parts:
            part.forward()

    def finish(self):
        for part in self.parts:
            part.finish()


def _gathered_shapes(shards):
    return [jax.ShapeDtypeStruct((N_DEV * s.shape[0],) + s.shape[1:], s.dtype) for s in shards]


def _all_gather_many(shards, name):
    n = len(shards)

    def body(*refs):
        g = _Many(_Gather, refs[:n], refs[n:2 * n], *refs[2 * n:])
        g.start()
        g.forward()
        g.finish()

    hbm = pl.BlockSpec(memory_space=pl.ANY)
    return pl.pallas_call(body, name=name, out_shape=_gathered_shapes(shards), in_specs=[hbm] * n,
                          out_specs=[hbm] * n, scratch_shapes=_comm_sems(n))(*shards)


def _all_to_all_many(sends, name):
    n = len(sends)

    def body(*refs):
        ex = _Many(_Exchange, refs[:n], refs[n:2 * n], *refs[2 * n:])
        ex.start()
        ex.finish()

    hbm = pl.BlockSpec(memory_space=pl.ANY)
    return pl.pallas_call(body, name=name, out_shape=[jax.ShapeDtypeStruct(s.shape, s.dtype) for s in sends],
                          in_specs=[hbm] * n, out_specs=[hbm] * n, scratch_shapes=_comm_sems(n))(*sends)


def _row(v):
    return v.reshape(1, -1)


def _ffn_fwd(x, w_in, w_out, g_pre, g_post, m, res_w, tag):
    shift, scale, gate = m[0], m[1], m[2]
    mpre = _row(g_pre * (1.0 + scale))
    mpost = _row(res_w * gate * g_post)
    h = _rms_fwd(x, mpre, _row(shift), None, BF16, tag + "_pre")
    u = _matmul(h, w_in, out_dtype=BF16, name=tag + "_in")
    s = _swiglu_fwd(u, tag + "_act")
    y = _matmul(s, w_out, name=tag + "_out")
    x_new = _rms_fwd(y, mpost, jnp.zeros_like(mpost), x, F32, tag + "_post")
    return x_new, (x, h, u, s, y, mpre, mpost)


def _sub_bwd_post(dx_new, y, mpost, g_post, gate, res_w, tag):
    dy, c1, _ = _rms_bwd(dx_new, y, mpost, None, BF16, tag + "_post_bwd")
    c1 = c1[0]
    return dy, c1 * res_w * g_post, c1 * res_w * gate


def _sub_bwd_pre(dh, x, mpre, dx_new, g_pre, scale, tag):
    dx, c2, c3 = _rms_bwd(dh, x, mpre, dx_new, F32, tag + "_pre_bwd")
    c2, c3 = c2[0], c3[0]
    return dx, c3, c2 * g_pre, c2 * (1.0 + scale)


def _ffn_bwd(dx_new, saved, w_in, w_out, g_pre, g_post, m, res_w, tag):
    x, h, u, s, y, mpre, mpost = saved
    scale, gate = m[1], m[2]
    dy, dgate, dg_post = _sub_bwd_post(dx_new, y, mpost, g_post, gate, res_w, tag)
    ds = _matmul(dy, w_out, tb=True, out_dtype=BF16, name=tag + "_out_dx")
    dw_out = _matmul(s, dy, ta=True, out_dtype=BF16, name=tag + "_out_dw")
    du = _swiglu_bwd(u, ds, tag + "_act_bwd")
    dh = _matmul(du, w_in, tb=True, name=tag + "_in_dx")
    dw_in = _matmul(h, du, ta=True, out_dtype=BF16, name=tag + "_in_dw")
    dx, dshift, dscale, dg_pre = _sub_bwd_pre(dh, x, mpre, dx_new, g_pre, scale, tag)
    return dx, dw_in, dw_out, jnp.stack([dshift, dscale, dgate]), dg_pre, dg_post


def _slope_cols(gi):
    _, r = C_GROUPS[gi]
    sl = jnp.asarray(_alibi_slopes(C_HEADS)[gi * C_HPG:(gi + 1) * C_HPG], F32) * float(r)
    return jnp.repeat(sl, HDIM).reshape(1, C_OUT)


def _mix_fwd(x, w, g_pre, g_post, m, lb, hn, tag, gather=None):
    t, d = x.shape
    shift, scale, gate = m[0], m[1], m[2]
    mpre = _row(g_pre * (1.0 + scale))
    mpost = _row(gate * g_post)
    h = _rms_fwd(x, mpre, _row(shift), None, BF16, tag + "_pre")
    p = _matmul(h, w["w_in"], name=tag + "_in")
    hn2 = _row(jnp.tile(hn, 2))
    ya, oa, states = _hgrn_fwd(p, _row(lb), hn2, tag + "_hgrn")
    kv = p[:, COL_BK:COL_CQ].astype(BF16)
    if gather is None:
        (yb, sb_tot), gathered = _sb_fwd(p, kv, tag + "_sb"), None
    else:
        res = _sb_fwd(p, kv, tag + "_sb_gather", gather)
        yb, sb_tot, gathered = res[0], res[1], list(res[2:])
    og, lg = zip(*[_dil_fwd(p, gi, tag + "_dil%d" % gi) for gi in range(len(C_GROUPS))])
    yc, lse_c = _dil_merge(og, lg, tag + "_dil_merge")
    gl = p[:, COL_GATE:]
    ws = (w["w_branch_a"], w["w_branch_b"], w["w_branch_c"])
    merged = _gate_fwd((ya, yb, yc), gl, ws, tag + "_gate")
    y = _matmul(merged, w["w_out"], name=tag + "_out")
    x_new = _rms_fwd(y, mpost, jnp.zeros_like(mpost), x, F32, tag + "_post")
    return x_new, (x, h, p, hn2, ya, oa, states, yb, kv, sb_tot, yc, lse_c, gl, merged, y, mpre, mpost), gathered


def _mix_bwd(dx_new, saved, w, g_pre, g_post, m, lb, tag, exchange=None):
    x, h, p, hn2, ya, oa, states, yb, kv, sb_tot, yc, lse_c, gl, merged, y, mpre, mpost = saved
    t = x.shape[0]
    scale, gate = m[1], m[2]
    dy, dgate, dg_post = _sub_bwd_post(dx_new, y, mpost, g_post, gate, 1.0, tag)
    dmerged = _matmul(dy, w["w_out"], tb=True, out_dtype=BF16, name=tag + "_out_dx")
    dw_out = _matmul(merged, dy, ta=True, out_dtype=BF16, name=tag + "_out_dw")
    ws = (w["w_branch_a"], w["w_branch_b"], w["w_branch_c"])
    dya, dyb, dyc, dgl, dwa, dwb, dwc = _gate_bwd(dmerged, (ya, yb, yc), gl, ws, tag + "_gate_bwd")
    dqa, dfa, dia, dga, dlb, dhn = _hgrn_bwd(p, _row(lb), hn2, oa, states, dya, tag + "_hgrn_bwd")
    if exchange is None:
        (dbq, dbk, dbv), received = _sb_bwd(p, kv, sb_tot, dyb, tag + "_sb_bwd"), None
    else:
        res = _sb_bwd(p, kv, sb_tot, dyb, tag + "_sb_bwd_exchange", exchange)
        dbq, dbk, dbv, received = res[0], res[1], res[2], list(res[3:])
    dcq, dck, dcv = zip(*[_dil_bwd(p, dyc, yc, lse_c, gi, tag + "_dil%d_bwd" % gi) for gi in range(len(C_GROUPS))])
    dp = jnp.concatenate([dqa, dfa, dia, dga, dbq, dbk, dbv, *dcq, *dck, *dcv, dgl], axis=1).astype(BF16)
    dh = _matmul(dp, w["w_in"], tb=True, name=tag + "_in_dx")
    dw_in = _matmul(h, dp, ta=True, out_dtype=BF16, name=tag + "_in_dw")
    dx, dshift, dscale, dg_pre = _sub_bwd_pre(dh, x, mpre, dx_new, g_pre, scale, tag)
    dhn_v = jnp.sum(dhn, axis=(0, 1))
    dhn_v = dhn_v[:A_VDIM] + dhn_v[A_VDIM:]
    dws = dict(w_in=dw_in, w_out=dw_out, w_branch_a=dwa.astype(BF16), w_branch_b=dwb.astype(BF16),
               w_branch_c=dwc.astype(BF16))
    return dx, dws, jnp.stack([dshift, dscale, dgate]), dg_pre, dg_post, dlb[0], dhn_v, received


class _LocalWeights:
    def __init__(self, wts):
        self.wts = wts

    def first(self):
        return None

    def shard(self, l):
        return None

    def layer(self, l, gathered):
        return {k: v[l] for k, v in self.wts.items()}

    def pack(self, l, dws):
        return dws

    def last(self, packed):
        return packed


class _ShardedWeights:
    def __init__(self, shards):
        self.shards = shards

    def shard(self, l):
        return [self.shards[k][l].astype(BF16) for k in BIG_WEIGHTS]

    def first(self):
        return _all_gather_many(self.shard(0), "weights_all_gather")

    def layer(self, l, gathered):
        out = {}
        for k, got in zip(BIG_WEIGHTS, gathered):
            _, r, c = self.shards[k].shape
            out[k] = got if k in ROW_SHARDED else got.reshape(N_DEV, r, c).transpose(1, 0, 2).reshape(r, N_DEV * c)
        return out

    def pack(self, l, dws):
        out = []
        for k in BIG_WEIGHTS:
            _, r, c = self.shards[k].shape
            g = dws[k]
            out.append(g.reshape(N_DEV, r, c) if k in ROW_SHARDED else g.reshape(r, N_DEV, c).transpose(1, 0, 2))
        return out

    def last(self, packed):
        return _all_to_all_many(packed, "grads_all_to_all")

    def partial_sums(self, received):
        return {k: jnp.concatenate([rec[b] for rec in received], axis=1) for b, k in enumerate(BIG_WEIGHTS)}


def _local_step(x, target, mod, norm_g, lb_all, hnorm, supply):
    depth = mod.shape[0]
    d = x.shape[1]
    saved, wls = [], []
    gathered = supply.first()
    for l in range(depth):
        wl = supply.layer(l, gathered)
        wls.append(wl)
        x, s0 = _ffn_fwd(x, wl["ffn1_w_in"], wl["ffn1_w_out"], norm_g[l, 0], norm_g[l, 1], mod[l, 0], 0.5, "ffn1")
        nxt = supply.shard(l + 1) if l + 1 < depth else None
        x, s1, gathered = _mix_fwd(x, wl, norm_g[l, 2], norm_g[l, 3], mod[l, 1], lb_all[l], hnorm[l], "mix", nxt)
        x, s2 = _ffn_fwd(x, wl["ffn2_w_in"], wl["ffn2_w_out"], norm_g[l, 4], norm_g[l, 5], mod[l, 2], 0.5, "ffn2")
        saved.append((s0, s1, s2))
    dx, sq = _loss_head(x, target, "loss_head")
    loss = 0.5 * jnp.sum(sq) / d
    dmod, dng, dlb, dhn = [], [], [], []
    returned = [None] * depth
    pending = None
    for l in reversed(range(depth)):
        wl = wls[l]
        s0, s1, s2 = saved[l]
        dx, dwi2, dwo2, dm2, dgp2, dgq2 = _ffn_bwd(dx, s2, wl["ffn2_w_in"], wl["ffn2_w_out"], norm_g[l, 4],
                                                   norm_g[l, 5], mod[l, 2], 0.5, "ffn2")
        fuse = pending is not None and isinstance(supply, _ShardedWeights)
        dx, dwm, dm1, dgp1, dgq1, dlb_l, dhn_l, received = _mix_bwd(
            dx, s1, wl, norm_g[l, 2], norm_g[l, 3], mod[l, 1], lb_all[l], "mix", pending if fuse else None)
        if pending is not None:
            returned[l + 1] = received if fuse else pending
        dx, dwi1, dwo1, dm0, dgp0, dgq0 = _ffn_bwd(dx, s0, wl["ffn1_w_in"], wl["ffn1_w_out"], norm_g[l, 0],
                                                   norm_g[l, 1], mod[l, 0], 0.5, "ffn1")
        dmod.append(jnp.stack([dm0, dm1, dm2]))
        dng.append(jnp.stack([dgp0, dgq0, dgp1, dgq1, dgp2, dgq2]))
        dlb.append(dlb_l)
        dhn.append(dhn_l)
        pending = supply.pack(l, dict(dwm, ffn1_w_in=dwi1, ffn1_w_out=dwo1, ffn2_w_in=dwi2, ffn2_w_out=dwo2))
    returned[0] = supply.last(pending)
    rev = lambda lst: jnp.stack(lst[::-1])
    return loss, dx, rev(dmod), rev(dng), rev(dlb), rev(dhn), returned


def _lb_all(logits):
    lb_p = jax.nn.softmax(logits.astype(F32), axis=0)
    return jnp.cumsum(lb_p, axis=0) - lb_p[0:1]


def _pad_rows(a, rows):
    return jnp.pad(a, ((0, rows - a.shape[0]), (0, 0)))


def kernel(x, c, w_ada, b_ada, norm_g, ffn1_w_in, ffn1_w_out, w_in, hgrn_lb_logits, hgrn_norm_g, w_branch_a, w_branch_b, w_branch_c, w_out, ffn2_w_in, ffn2_w_out, loss_target, m_w_ada, m_b_ada, m_norm_g, m_ffn1_w_in, m_ffn1_w_out, m_w_in, m_hgrn_lb_logits, m_hgrn_norm_g, m_w_branch_a, m_w_branch_b, m_w_branch_c, m_w_out, m_ffn2_w_in, m_ffn2_w_out, v_w_ada, v_b_ada, v_norm_g, v_ffn1_w_in, v_ffn1_w_out, v_w_in, v_hgrn_lb_logits, v_hgrn_norm_g, v_w_branch_a, v_w_branch_b, v_w_branch_c, v_w_out, v_ffn2_w_in, v_ffn2_w_out):
    weights = dict(w_ada=w_ada, b_ada=b_ada, norm_g=norm_g, ffn1_w_in=ffn1_w_in, ffn1_w_out=ffn1_w_out, w_in=w_in,
                   hgrn_lb_logits=hgrn_lb_logits, hgrn_norm_g=hgrn_norm_g, w_branch_a=w_branch_a,
                   w_branch_b=w_branch_b, w_branch_c=w_branch_c, w_out=w_out, ffn2_w_in=ffn2_w_in,
                   ffn2_w_out=ffn2_w_out)
    mom1 = dict(w_ada=m_w_ada, b_ada=m_b_ada, norm_g=m_norm_g, ffn1_w_in=m_ffn1_w_in, ffn1_w_out=m_ffn1_w_out,
                w_in=m_w_in, hgrn_lb_logits=m_hgrn_lb_logits, hgrn_norm_g=m_hgrn_norm_g, w_branch_a=m_w_branch_a,
                w_branch_b=m_w_branch_b, w_branch_c=m_w_branch_c, w_out=m_w_out, ffn2_w_in=m_ffn2_w_in,
                ffn2_w_out=m_ffn2_w_out)
    mom2 = dict(w_ada=v_w_ada, b_ada=v_b_ada, norm_g=v_norm_g, ffn1_w_in=v_ffn1_w_in, ffn1_w_out=v_ffn1_w_out,
                w_in=v_w_in, hgrn_lb_logits=v_hgrn_lb_logits, hgrn_norm_g=v_hgrn_norm_g, w_branch_a=v_w_branch_a,
                w_branch_b=v_w_branch_b, w_branch_c=v_w_branch_c, w_out=v_w_out, ffn2_w_in=v_ffn2_w_in,
                ffn2_w_out=v_ffn2_w_out)
    order = list(weights)
    depth, d, ada_cols = w_ada.shape
    nd = d // LANES
    xi, yi, ci = _my_coords()
    me = 4 * xi + 2 * yi + ci

    small = jnp.concatenate([c.reshape(nd, LANES), norm_g.reshape(depth * 6, LANES)], axis=0)
    g1 = _all_gather(small, True, False, "small_all_gather").reshape(N_DEV, small.shape[0], LANES)
    c_act = _silu(g1[:, :nd].reshape(N_DEV, d))
    norm_full = g1[:, nd:].reshape(N_DEV, depth, 6, LANES).transpose(1, 2, 0, 3).reshape(depth, 6, d)

    c_pad = _pad_rows(c_act, 16)
    mod_sh = jnp.stack([_matmul(c_pad, w_ada[l], name="ada_mod")[:N_DEV]
                        + lax.dynamic_slice_in_dim(b_ada[l], me * ada_cols, ada_cols)[None]
                        for l in range(depth)])
    g2 = _all_gather(mod_sh.reshape(-1, LANES), True, False, "mod_all_gather")
    g2 = g2.reshape(N_DEV, depth, N_DEV, ada_cols)
    mod = lax.dynamic_index_in_dim(g2, me, axis=2, keepdims=False)
    mod = mod.transpose(1, 0, 2).reshape(depth, 3, 3, d)

    supply = _ShardedWeights({k: weights[k] for k in BIG_WEIGHTS})
    lb_all, lb_vjp = jax.vjp(_lb_all, hgrn_lb_logits)

    loss, dx, dmod, dng, dlb, dhn, received = _local_step(x[0], loss_target[0], mod, norm_full, lb_all,
                                                          hgrn_norm_g, supply)
    loss = lax.psum(loss, ("x", "y", "c"))

    dhn_pad = jnp.pad(dhn.reshape(-1), (0, 8 * LANES - dhn.size))
    pieces = [dmod.reshape(-1), dng.reshape(-1), dlb.reshape(-1), dhn_pad]
    sizes = [p_.size for p_ in pieces]
    smallg = jnp.concatenate(pieces).reshape(-1, LANES)
    g3, gsum = _all_gather(smallg, True, True, "small_grads_all_gather")
    g3 = g3.reshape(N_DEV, -1)
    gsum = gsum.reshape(-1)
    dmod_all = g3[:, :sizes[0]].reshape(N_DEV, depth, 9 * d)
    o1 = sizes[0]
    grads = {}
    grads["b_ada"] = gsum[:o1].reshape(depth, 9 * d)
    dng_sum = gsum[o1:o1 + sizes[1]].reshape(depth, 6, nd, LANES)
    grads["norm_g"] = lax.dynamic_index_in_dim(dng_sum, me, axis=2, keepdims=False)
    o2 = o1 + sizes[1]
    dlb_sum = gsum[o2:o2 + sizes[2]].reshape(depth, A_QK)
    grads["hgrn_lb_logits"] = lb_vjp(dlb_sum)[0]
    o3 = o2 + sizes[2]
    grads["hgrn_norm_g"] = gsum[o3:o3 + dhn.size].reshape(depth, A_VDIM)
    dmod_mine = lax.dynamic_slice_in_dim(dmod_all, me * ada_cols, ada_cols, axis=2)
    grads["w_ada"] = jnp.stack([_matmul(c_pad, _pad_rows(dmod_mine[:, l], 16), ta=True, name="ada_dw")
                                for l in range(depth)])

    gparts = supply.partial_sums(received)

    outs = {}
    for k in order:
        w = weights[k]
        w2 = w.reshape(-1, w.shape[-1])
        gp = gparts[k] if k in gparts else grads[k].reshape((1,) + w2.shape)
        res = _adamw(w2, mom1[k].reshape(w2.shape), mom2[k].reshape(w2.shape), gp, "adamw")
        outs[k] = [r.reshape(w.shape) for r in res]
    return (loss, dx[None], *[outs[k][0] for k in order], *[outs[k][1] for k in order],
            *[outs[k][2] for k in order], *[outs[k][3] for k in order])
```

```python
import functools
import math

import jax
import jax.numpy as jnp
from jax import lax
from jax.experimental import pallas as pl
from jax.experimental.pallas import tpu as pltpu

F32 = jnp.float32
BF16 = jnp.bfloat16

A_HEADS, A_KDIM, A_VDIM, A_CHUNK = 6, 128, 64, 64
B_HEADS, HDIM = 6, 64
C_GROUPS = ((128, 1), (512, 4), (2048, 16))
C_HPG = 4
C_HEADS = C_HPG * len(C_GROUPS)
N_BRANCH = 3
EPS = 1e-6
NEG_BIG = -1e30
TINY = 1e-30
A_QK = A_HEADS * A_KDIM
A_V = A_HEADS * A_VDIM
B_W = B_HEADS * HDIM
C_W = C_HEADS * HDIM
C_OUT = C_HPG * HDIM
COL_AQ, COL_AF, COL_AI, COL_AG = 0, A_QK, 2 * A_QK, 2 * A_QK + A_V
COL_BQ = 2 * A_QK + 2 * A_V
COL_BK, COL_BV = COL_BQ + B_W, COL_BQ + 2 * B_W
COL_CQ = COL_BQ + 3 * B_W
COL_CK, COL_CV = COL_CQ + C_W, COL_CQ + 2 * C_W
COL_GATE = COL_CQ + 3 * C_W

ADAM_LR, ADAM_B1, ADAM_B2, ADAM_EPS, ADAM_WD, ADAM_STEP = 0.001, 0.9, 0.999, 1e-08, 0.01, 10

N_DEV = 8
LANES = 128
VMEM_LIMIT = 48 * 1024 * 1024
MATMUL_VMEM_BUDGET = 28 * 1024 * 1024
SUB = 16
EXP_CLAMP = 80.0
MESH = pl.DeviceIdType.MESH

BIG_WEIGHTS = ("ffn1_w_in", "ffn1_w_out", "w_in", "w_branch_a", "w_branch_b", "w_branch_c", "w_out",
               "ffn2_w_in", "ffn2_w_out")
ROW_SHARDED = ("ffn1_w_out", "w_out", "ffn2_w_out")


def _cparams(sem):
    return pltpu.CompilerParams(dimension_semantics=sem, vmem_limit_bytes=VMEM_LIMIT)


def _tile(n, cap):
    best, t = None, LANES
    while t <= min(n, cap):
        if n % t == 0:
            best = t
        t += LANES
    return best or n


def _rows(t, cap=256):
    r = cap
    while t % r:
        r //= 2
    return r


def _divisors(n):
    return [t for t in range(LANES, n + 1, LANES) if n % t == 0] or [n]


def _matmul_tiles(m, n, k, a_size, b_size, o_size):
    best, best_key = None, None
    for tm in _divisors(m):
        for tn in _divisors(n):
            for tk in _divisors(k):
                if tm > 1024 or tn > 3072 or tk > 4096:
                    continue
                cast = (tm * tk * 2 if a_size > 2 else 0) + (tk * tn * 2 if b_size > 2 else 0)
                need = 2 * (tm * tk * a_size + tk * tn * b_size + tm * tn * o_size) + 2 * tm * tn * 4 + cast
                if need > MATMUL_VMEM_BUDGET:
                    continue
                key = (tm * tn * tk, tk)
                if best_key is None or key > best_key:
                    best, best_key = (tm, tn, tk), key
    return best


def _dot(a, b):
    return jnp.dot(a, b, preferred_element_type=F32)


def _dot_nt(a, b):
    return lax.dot_general(a, b, (((1,), (1,)), ((), ())), preferred_element_type=F32)


def _dot_tn(a, b):
    return lax.dot_general(a, b, (((0,), (0,)), ((), ())), preferred_element_type=F32)


def _split3(x):
    h = x.astype(BF16)
    r = x - h.astype(F32)
    m = r.astype(BF16)
    lo = (r - m.astype(F32)).astype(BF16)
    return h, m, lo


def _ones_left(mat01, x):
    h, m, lo = _split3(x)
    return _dot(mat01, h) + _dot(mat01, m) + _dot(mat01, lo)


def _silu(x):
    return x * jax.nn.sigmoid(x)


def _dsilu(x):
    s = jax.nn.sigmoid(x)
    return s * (1.0 + x * (1.0 - s))


def _matmul(a, b, *, ta=False, tb=False, out_dtype=F32, name):
    if ta:
        kdim, m = a.shape
    else:
        m, kdim = a.shape
    n = b.shape[0] if tb else b.shape[1]
    tm, tn, tk = _matmul_tiles(m, n, kdim, a.dtype.itemsize, b.dtype.itemsize, jnp.dtype(out_dtype).itemsize)
    nk = kdim // tk
    ni, nj = m // tm, n // tn
    a_bytes, b_bytes = m * kdim * a.dtype.itemsize, kdim * n * b.dtype.itemsize
    j_outer = nk == 1 and (b_bytes + a_bytes * nj) < (a_bytes + b_bytes * ni)
    dims = (((0 if ta else 1,), (1 if tb else 0,)), ((), ()))

    def body(a_ref, b_ref, o_ref, *scratch):
        p = lax.dot_general(a_ref[...].astype(BF16), b_ref[...].astype(BF16), dims, preferred_element_type=F32)
        if nk == 1:
            o_ref[...] = p.astype(o_ref.dtype)
            return
        acc = scratch[0]
        k = pl.program_id(2)

        @pl.when(k == 0)
        def _():
            acc[...] = p

        @pl.when(k > 0)
        def _():
            acc[...] += p

        @pl.when(k == nk - 1)
        def _():
            o_ref[...] = acc[...].astype(o_ref.dtype)

    def spec(shape, pick):
        if j_outer:
            return pl.BlockSpec(shape, lambda j, i, k: pick(i, j, k))
        return pl.BlockSpec(shape, lambda i, j, k: pick(i, j, k))

    a_spec = spec((tk, tm), lambda i, j, k: (k, i)) if ta else spec((tm, tk), lambda i, j, k: (i, k))
    b_spec = spec((tn, tk), lambda i, j, k: (j, k)) if tb else spec((tk, tn), lambda i, j, k: (k, j))
    return pl.pallas_call(
        body, name=name, grid=(nj, ni, nk) if j_outer else (ni, nj, nk), in_specs=[a_spec, b_spec],
        out_specs=spec((tm, tn), lambda i, j, k: (i, j)),
        out_shape=jax.ShapeDtypeStruct((m, n), out_dtype),
        scratch_shapes=[pltpu.VMEM((tm, tn), F32)] if nk > 1 else [],
        compiler_params=_cparams(("parallel", "parallel", "arbitrary")),
    )(a, b)


def _rms_fwd(z, mcol, acol, res, out_dtype, name):
    t, d = z.shape
    tr = _rows(t)
    has_res = res is not None

    def body(*refs):
        if has_res:
            z_ref, m_ref, a_ref, r_ref, o_ref = refs
        else:
            z_ref, m_ref, a_ref, o_ref = refs
        zf = z_ref[...]
        r = lax.rsqrt(jnp.mean(zf * zf, axis=-1, keepdims=True) + EPS)
        y = zf * r * m_ref[...] + a_ref[...]
        if has_res:
            y = r_ref[...] + y
        o_ref[...] = y.astype(o_ref.dtype)

    row = pl.BlockSpec((tr, d), lambda i: (i, 0))
    col = pl.BlockSpec((1, d), lambda i: (0, 0))
    ins = [z, mcol, acol] + ([res] if has_res else [])
    return pl.pallas_call(
        body, name=name, grid=(t // tr,), in_specs=[row, col, col] + ([row] if has_res else []),
        out_specs=row, out_shape=jax.ShapeDtypeStruct((t, d), out_dtype),
        compiler_params=_cparams(("parallel",)),
    )(*ins)


def _rms_bwd(d_out, z, mcol, dres, out_dtype, name):
    t, d = z.shape
    tr = _rows(t)
    has_res = dres is not None

    def body(*refs):
        if has_res:
            d_ref, z_ref, m_ref, r_ref, o_ref, s1_ref, s2_ref = refs
        else:
            d_ref, z_ref, m_ref, o_ref, s1_ref, s2_ref = refs
        i = pl.program_id(0)
        zf = z_ref[...]
        r = lax.rsqrt(jnp.mean(zf * zf, axis=-1, keepdims=True) + EPS)
        zh = zf * r
        df = d_ref[...].astype(F32)
        dzh = df * m_ref[...]
        dz = r * (dzh - zh * jnp.mean(dzh * zh, axis=-1, keepdims=True))
        if has_res:
            dz = dz + r_ref[...]
        o_ref[...] = dz.astype(o_ref.dtype)
        s1 = jnp.sum(df * zh, axis=0, keepdims=True)
        s2 = jnp.sum(df, axis=0, keepdims=True)

        @pl.when(i == 0)
        def _():
            s1_ref[...] = s1
            s2_ref[...] = s2

        @pl.when(i > 0)
        def _():
            s1_ref[...] += s1
            s2_ref[...] += s2

    row = pl.BlockSpec((tr, d), lambda i: (i, 0))
    col = pl.BlockSpec((1, d), lambda i: (0, 0))
    ins = [d_out, z, mcol] + ([dres] if has_res else [])
    return pl.pallas_call(
        body, name=name, grid=(t // tr,), in_specs=[row, row, col] + ([row] if has_res else []),
        out_specs=[row, col, col],
        out_shape=[jax.ShapeDtypeStruct((t, d), out_dtype), jax.ShapeDtypeStruct((1, d), F32),
                   jax.ShapeDtypeStruct((1, d), F32)],
        compiler_params=_cparams(("arbitrary",)),
    )(*ins)


def _swiglu_fwd(u, name):
    t, f2 = u.shape
    f = f2 // 2
    tr = _rows(t)

    def body(u_ref, s_ref):
        a = u_ref[:, :f].astype(F32)
        b = u_ref[:, f:].astype(F32)
        s_ref[...] = (_silu(a) * b).astype(s_ref.dtype)

    return pl.pallas_call(
        body, name=name, grid=(t // tr,), in_specs=[pl.BlockSpec((tr, f2), lambda i: (i, 0))],
        out_specs=pl.BlockSpec((tr, f), lambda i: (i, 0)), out_shape=jax.ShapeDtypeStruct((t, f), BF16),
        compiler_params=_cparams(("parallel",)),
    )(u)


def _swiglu_bwd(u, ds, name):
    t, f2 = u.shape
    f = f2 // 2
    tr = _rows(t)

    def body(u_ref, ds_ref, du_ref):
        a = u_ref[:, :f].astype(F32)
        b = u_ref[:, f:].astype(F32)
        g = ds_ref[...].astype(F32)
        du_ref[:, :f] = (g * b * _dsilu(a)).astype(du_ref.dtype)
        du_ref[:, f:] = (g * _silu(a)).astype(du_ref.dtype)

    return pl.pallas_call(
        body, name=name, grid=(t // tr,),
        in_specs=[pl.BlockSpec((tr, f2), lambda i: (i, 0)), pl.BlockSpec((tr, f), lambda i: (i, 0))],
        out_specs=pl.BlockSpec((tr, f2), lambda i: (i, 0)), out_shape=jax.ShapeDtypeStruct((t, f2), BF16),
        compiler_params=_cparams(("parallel",)),
    )(u, ds)


def _loss_head(y, target, name):
    t, d = y.shape
    tr = _rows(t)

    def body(y_ref, t_ref, dy_ref, sq_ref):
        i = pl.program_id(0)
        e = y_ref[...] - t_ref[...]
        dy_ref[...] = e * (1.0 / d)
        s = jnp.sum(e * e, axis=0, keepdims=True)

        @pl.when(i == 0)
        def _():
            sq_ref[...] = s

        @pl.when(i > 0)
        def _():
            sq_ref[...] += s

    row = pl.BlockSpec((tr, d), lambda i: (i, 0))
    col = pl.BlockSpec((1, d), lambda i: (0, 0))
    return pl.pallas_call(
        body, name=name, grid=(t // tr,), in_specs=[row, row], out_specs=[row, col],
        out_shape=[jax.ShapeDtypeStruct((t, d), F32), jax.ShapeDtypeStruct((1, d), F32)],
        compiler_params=_cparams(("arbitrary",)),
    )(y, target)


def _hgrn_consts():
    c = A_CHUNK
    shift = SUB.bit_length() - 1
    r = lax.broadcasted_iota(jnp.int32, (c, c), 0)
    s = lax.broadcasted_iota(jnp.int32, (c, c), 1)
    sub_r = lax.shift_right_logical(r, shift)
    incl = s <= r
    masks = [jnp.logical_and(sub_r == i, incl) for i in range(c // SUB)]
    rev_incl = jnp.where(s >= r, 1.0, 0.0).astype(BF16)
    r2 = lax.broadcasted_iota(jnp.int32, (2 * c + 8, c), 0)
    s2 = lax.broadcasted_iota(jnp.int32, (2 * c + 8, c), 1)
    sub_start = lax.shift_left(lax.shift_right_logical(r2 - c, shift), shift)
    running = jnp.where(s2 <= r2, 1.0, 0.0)
    before = jnp.where(s2 < sub_start, 1.0, 0.0)
    stack = jnp.where(r2 < c, running, jnp.where(r2 < 2 * c, before, 1.0)).astype(BF16)
    return stack, masks, incl, rev_incl


def _hgrn_gates(q_raw, f_raw, lbv, stack):
    sg = jax.nn.sigmoid(f_raw)
    sgn = jax.nn.sigmoid(-f_raw)
    f = lbv + (1.0 - lbv) * sg
    logf = jnp.log(jnp.maximum(f, TINY))
    return dict(sg=sg, sgn=sgn, f=f, k=(1.0 - lbv) * sgn, q=_silu(q_raw), bb=_ones_left(stack, logf))


def _hgrn_chunk(q_raw, f_raw, lbv, stack):
    return _hgrn_decays(_hgrn_gates(q_raw, f_raw, lbv, stack))


def _hgrn_decays(gates):
    c = A_CHUNK
    sg, sgn, f, k, q, bb = (gates[n] for n in ("sg", "sgn", "f", "k", "q", "bb"))
    b = bb[:c]
    bsrow = bb[c:2 * c]
    b_end = bb[2 * c:2 * c + 1]
    e_sub = jnp.exp(b - bsrow)
    e_b = jnp.exp(b)
    e_end = jnp.exp(b_end - b)
    qs = q * e_sub
    q_in = q * e_b
    kend = k * e_end
    kfac = [jnp.exp(jnp.minimum(bsrow[i * SUB:i * SUB + 1] - b, EXP_CLAMP)) for i in range(c // SUB)]
    return dict(sg=sg, sgn=sgn, f=f, k=k, q=q, b=b, b_end=b_end, e_sub=e_sub, e_b=e_b, e_end=e_end,
                qs=qs, q_in=q_in, kend=kend, kfac=kfac)


def _hgrn_scores(ch, masks):
    qs_b = ch["qs"].astype(BF16)
    a = None
    for i, mk in enumerate(masks):
        ki = (ch["k"] * ch["kfac"][i]).astype(BF16)
        part = jnp.where(mk, _dot_nt(qs_b, ki), 0.0)
        a = part if a is None else a + part
    return a


def _hgrn_fwd(p, lb, hn2, name):
    t = p.shape[0]
    tb = _rows(t)
    nt = t // tb
    nc = tb // A_CHUNK
    c = A_CHUNK

    def body(q_ref, f_ref, i_ref, g_ref, lb_ref, hn_ref, y_ref, o_ref, st_ref, s_scr):
        j = pl.program_id(1)

        @pl.when(j == 0)
        def _():
            s_scr[...] = jnp.zeros_like(s_scr)

        stack, masks, _, _ = _hgrn_consts()
        units = [(ci, hh) for ci in range(nc) for hh in range(2)]
        lsl = [slice(A_KDIM * hh, A_KDIM * (hh + 1)) for hh in range(2)]
        hsl = [slice(A_VDIM * hh, A_VDIM * (hh + 1)) for hh in range(2)]
        rows = [pl.ds(ci * c, c) for ci in range(nc)]
        gates = {u: _hgrn_gates(q_ref[rows[u[0]], lsl[u[1]]], f_ref[rows[u[0]], lsl[u[1]]], lb_ref[:, lsl[u[1]]], stack)
                 for u in units}
        ch = {u: _hgrn_decays(gates[u]) for u in units}
        v = {u: i_ref[rows[u[0]], hsl[u[1]]].astype(BF16) for u in units}
        a = {u: _hgrn_scores(ch[u], masks).astype(BF16) for u in units}
        grow = {u: _dot_tn(v[u], ch[u]["kend"].astype(BF16)) for u in units}
        states = [s_scr[0], s_scr[1]]
        entering = {}
        for ci, hh in units:
            entering[ci, hh] = states[hh]
            st_ref[hh, ci] = states[hh]
            states[hh] = states[hh] * jnp.exp(ch[ci, hh]["b_end"]) + grow[ci, hh]
        s_scr[0] = states[0]
        s_scr[1] = states[1]
        for u in units:
            o_ref[rows[u[0]], hsl[u[1]]] = (_dot_nt(ch[u]["q_in"].astype(BF16), entering[u].astype(BF16))
                                            + _dot(a[u], v[u]))
        for hh in range(2):
            hsl = slice(A_VDIM * hh, A_VDIM * (hh + 1))
            o = o_ref[:, hsl]
            r = lax.rsqrt(jnp.mean(o * o, axis=-1, keepdims=True) + EPS)
            y_ref[:, hsl] = (o * r * hn_ref[:, hsl] * _silu(g_ref[:, hsl])).astype(y_ref.dtype)

    w2 = 2 * A_KDIM
    return pl.pallas_call(
        body, name=name, grid=(A_HEADS // 2, nt),
        in_specs=[pl.BlockSpec((tb, w2), lambda h, j: (j, COL_AQ // w2 + h)),
                  pl.BlockSpec((tb, w2), lambda h, j: (j, COL_AF // w2 + h)),
                  pl.BlockSpec((tb, LANES), lambda h, j: (j, COL_AI // LANES + h)),
                  pl.BlockSpec((tb, LANES), lambda h, j: (j, COL_AG // LANES + h)),
                  pl.BlockSpec((1, w2), lambda h, j: (0, h)),
                  pl.BlockSpec((1, LANES), lambda h, j: (0, 0))],
        out_specs=[pl.BlockSpec((tb, LANES), lambda h, j: (j, h)),
                   pl.BlockSpec((tb, LANES), lambda h, j: (j, h)),
                   pl.BlockSpec((2, nc, A_VDIM, A_KDIM), lambda h, j: (h, j, 0, 0))],
        out_shape=[jax.ShapeDtypeStruct((t, A_V), BF16), jax.ShapeDtypeStruct((t, A_V), F32),
                   jax.ShapeDtypeStruct((A_HEADS, t // c, A_VDIM, A_KDIM), F32)],
        scratch_shapes=[pltpu.VMEM((2, A_VDIM, A_KDIM), F32)],
        compiler_params=_cparams(("parallel", "arbitrary")),
    )(p, p, p, p, lb, hn2)


def _hgrn_bwd(p, lb, hn2, o_raw, states, dya, name):
    t = p.shape[0]
    tb = _rows(t)
    nt = t // tb
    nc = tb // A_CHUNK
    c = A_CHUNK

    def body(q_ref, f_ref, i_ref, g_ref, lb_ref, hn_ref, o_ref, st_ref, dy_ref,
             dq_ref, df_ref, di_ref, dg_ref, dlb_ref, dhn_ref, ds_scr, do_scr):
        j = pl.program_id(1)

        @pl.when(j == 0)
        def _():
            ds_scr[...] = jnp.zeros_like(ds_scr)
            dlb_ref[...] = jnp.zeros_like(dlb_ref)
            dhn_ref[...] = jnp.zeros_like(dhn_ref)

        stack, masks, incl, rev_incl = _hgrn_consts()
        for hh in range(2):
            hsl = slice(A_VDIM * hh, A_VDIM * (hh + 1))
            o = o_ref[:, hsl]
            g = g_ref[:, hsl]
            dy = dy_ref[:, hsl].astype(F32)
            hn = hn_ref[:, hsl]
            r = lax.rsqrt(jnp.mean(o * o, axis=-1, keepdims=True) + EPS)
            oh = o * r
            sgate = _silu(g)
            dg_ref[:, hsl] = dy * oh * hn * _dsilu(g)
            dhn_ref[0, :, hsl] += jnp.sum(dy * oh * sgate, axis=0, keepdims=True)
            doh = dy * hn * sgate
            do_scr[:, hsl] = r * (doh - oh * jnp.mean(doh * oh, axis=-1, keepdims=True))

        units = [(ci, hh) for ci in reversed(range(nc)) for hh in range(2)]
        lsl = [slice(A_KDIM * hh, A_KDIM * (hh + 1)) for hh in range(2)]
        hsl = [slice(A_VDIM * hh, A_VDIM * (hh + 1)) for hh in range(2)]
        rows = [pl.ds(ci * c, c) for ci in range(nc)]
        q_raw = {u: q_ref[rows[u[0]], lsl[u[1]]] for u in units}
        gates = {u: _hgrn_gates(q_raw[u], f_ref[rows[u[0]], lsl[u[1]]], lb_ref[:, lsl[u[1]]], stack) for u in units}
        ch = {u: _hgrn_decays(gates[u]) for u in units}
        v = {u: i_ref[rows[u[0]], hsl[u[1]]].astype(BF16) for u in units}
        do_b = {u: do_scr[rows[u[0]], hsl[u[1]]].astype(BF16) for u in units}
        st = {u: st_ref[u[1], u[0]] for u in units}
        qs_b = {u: ch[u]["qs"].astype(BF16) for u in units}
        a_b = {u: _hgrn_scores(ch[u], masks).astype(BF16) for u in units}
        da = {u: jnp.where(incl, _dot_nt(do_b[u], v[u]), 0.0) for u in units}
        dq_x = {u: _dot(do_b[u], st[u].astype(BF16)) for u in units}
        grow = {u: _dot_tn(do_b[u], ch[u]["q_in"].astype(BF16)) for u in units}
        dstates = [ds_scr[0], ds_scr[1]]
        leaving = {}
        for ci, hh in units:
            leaving[ci, hh] = dstates[hh]
            dstates[hh] = dstates[hh] * jnp.exp(ch[ci, hh]["b_end"]) + grow[ci, hh]
        for hh in range(2):
            ds_scr[hh] = dstates[hh]
        dst_b = {u: leaving[u].astype(BF16) for u in units}
        dv = {u: _dot_tn(a_b[u], do_b[u]) + _dot_nt(ch[u]["kend"].astype(BF16), dst_b[u]) for u in units}
        dk_x = {u: _dot(v[u], dst_b[u]) for u in units}
        dlb_acc = [jnp.zeros((1, A_KDIM), F32), jnp.zeros((1, A_KDIM), F32)]
        for u in units:
            ci, hh = u
            cu = ch[u]
            lbv = lb_ref[:, lsl[hh]]
            dq_i = None
            dk_i = None
            kdk_i = None
            for i, mk in enumerate(masks):
                dam = jnp.where(mk, da[u], 0.0).astype(BF16)
                ki = (cu["k"] * cu["kfac"][i]).astype(BF16)
                pq = _dot(dam, ki)
                pk = _dot_tn(dam, qs_b[u])
                dq_i = pq if dq_i is None else dq_i + pq
                dk_i = cu["kfac"][i] * pk if dk_i is None else dk_i + cu["kfac"][i] * pk
                kdk_i = ki.astype(F32) * pk if kdk_i is None else kdk_i + ki.astype(F32) * pk
            dq = cu["e_sub"] * dq_i + cu["e_b"] * dq_x[u]
            dk = dk_i + cu["e_end"] * dk_x[u]
            kx = cu["kend"] * dk_x[u]
            db = (qs_b[u].astype(F32) * dq_i + cu["q_in"] * dq_x[u]) - (kdk_i + kx)
            later = (jnp.exp(cu["b_end"]) * jnp.sum(leaving[u] * st[u], axis=0, keepdims=True)
                     + jnp.sum(kx, axis=0, keepdims=True))
            dlogf = later + _ones_left(rev_incl, db)
            dfv = jnp.where(cu["f"] > TINY, dlogf / cu["f"], 0.0)
            dq_ref[rows[ci], lsl[hh]] = dq * _dsilu(q_raw[u])
            df_ref[rows[ci], lsl[hh]] = (1.0 - lbv) * cu["sg"] * cu["sgn"] * (dfv - dk)
            dlb_acc[hh] = dlb_acc[hh] + jnp.sum(dfv * (1.0 - cu["sg"]) - dk * cu["sgn"], axis=0, keepdims=True)
            di_ref[rows[ci], hsl[hh]] = dv[u]
        for hh in range(2):
            dlb_ref[:, A_KDIM * hh:A_KDIM * (hh + 1)] += dlb_acc[hh]

    w2 = 2 * A_KDIM
    rev = lambda j: nt - 1 - j
    return pl.pallas_call(
        body, name=name, grid=(A_HEADS // 2, nt),
        in_specs=[pl.BlockSpec((tb, w2), lambda h, j: (rev(j), COL_AQ // w2 + h)),
                  pl.BlockSpec((tb, w2), lambda h, j: (rev(j), COL_AF // w2 + h)),
                  pl.BlockSpec((tb, LANES), lambda h, j: (rev(j), COL_AI // LANES + h)),
                  pl.BlockSpec((tb, LANES), lambda h, j: (rev(j), COL_AG // LANES + h)),
                  pl.BlockSpec((1, w2), lambda h, j: (0, h)),
                  pl.BlockSpec((1, LANES), lambda h, j: (0, 0)),
                  pl.BlockSpec((tb, LANES), lambda h, j: (rev(j), h)),
                  pl.BlockSpec((2, nc, A_VDIM, A_KDIM), lambda h, j: (h, rev(j), 0, 0)),
                  pl.BlockSpec((tb, LANES), lambda h, j: (rev(j), h))],
        out_specs=[pl.BlockSpec((tb, w2), lambda h, j: (rev(j), h)),
                   pl.BlockSpec((tb, w2), lambda h, j: (rev(j), h)),
                   pl.BlockSpec((tb, LANES), lambda h, j: (rev(j), h)),
                   pl.BlockSpec((tb, LANES), lambda h, j: (rev(j), h)),
                   pl.BlockSpec((1, w2), lambda h, j: (0, h)),
                   pl.BlockSpec((1, 1, LANES), lambda h, j: (h, 0, 0))],
        out_shape=[jax.ShapeDtypeStruct((t, A_QK), F32), jax.ShapeDtypeStruct((t, A_QK), F32),
                   jax.ShapeDtypeStruct((t, A_V), F32), jax.ShapeDtypeStruct((t, A_V), F32),
                   jax.ShapeDtypeStruct((1, A_QK), F32), jax.ShapeDtypeStruct((A_HEADS // 2, 1, LANES), F32)],
        scratch_shapes=[pltpu.VMEM((2, A_VDIM, A_KDIM), F32), pltpu.VMEM((tb, LANES), F32)],
        compiler_params=_cparams(("parallel", "arbitrary")),
    )(p, p, p, p, lb, hn2, o_raw, states, dya)


BLK = 128
SCALE = HDIM ** -0.5
SB_CHUNK = 4


def _softplus(z):
    return jnp.maximum(z, 0.0) + jnp.log(1.0 + jnp.exp(-jnp.abs(z)))


def _sb_sum_matrix(keep, with_total=False):
    width = 2 * BLK if with_total else BLK
    sp = lax.broadcasted_iota(jnp.int32, (BLK, width), 0)
    s = lax.broadcasted_iota(jnp.int32, (BLK, width), 1)
    return jnp.where(jnp.logical_or(s >= BLK, keep(sp, s)), 1.0, 0.0).astype(BF16)


def _lanes(col):
    return jnp.broadcast_to(col, (BLK, BLK))


def _sb_fwd(p, kv, name, gather=None):
    t = p.shape[0]
    nq = t // BLK
    nh = B_HEADS // 2
    cw = SB_CHUNK * BLK
    fused = gather is not None
    n = len(gather) if fused else 0

    def body(*refs):
        q_ref, kb, vb = refs[:3]
        o_ref, tot_ref = refs[3 + n:5 + n]
        zbuf, stage, sbuf, abuf = refs[5 + 2 * n:9 + 2 * n]
        hp = pl.program_id(0)
        qi = pl.program_id(1)
        if fused:
            g = _Many(_Gather, refs[3:3 + n], refs[5 + n:5 + 2 * n], *refs[9 + 2 * n:])
            pl.when(jnp.logical_and(hp == 0, qi == 0))(g.start)
            pl.when(jnp.logical_and(hp == nh - 1, qi == 0))(g.forward)

        @pl.when(qi == 0)
        def _():
            abuf[...] = jnp.zeros_like(abuf)

        row = lax.broadcasted_iota(jnp.int32, (BLK, BLK), 0)
        col = lax.broadcasted_iota(jnp.int32, (BLK, BLK), 1)
        sums = _sb_sum_matrix(lambda sp, s: sp >= s, True)
        hsl = [slice(HDIM * h, HDIM * (h + 1)) for h in range(2)]
        nchunk = qi // SB_CHUNK + 1
        for h in range(2):
            zbuf[h] = _dot_nt((q_ref[:, hsl[h]] * SCALE).astype(BF16), kb[:, hsl[h]])

        col_minus_row = col - row

        def causal(j):
            return col_minus_row < (qi - j) * BLK

        def l_pass(c, carry):
            for b in range(SB_CHUNK):
                j = c * SB_CHUNK + b
                off = pl.multiple_of(j * BLK, BLK)
                mask = causal(j)
                for h in range(2):
                    lm = jnp.where(mask, -_softplus(zbuf[h, :, pl.ds(off, BLK)]), 0.0)
                    stage[h, pl.ds(off, BLK), :] = lm.astype(BF16)
            return carry

        lax.fori_loop(0, nchunk, l_pass, 0)

        def sum_pass(c, carry):
            rows = pl.ds(pl.multiple_of(c * cw, cw), cw)
            for h in range(2):
                sbuf[h, rows, :] = _dot(stage[h, rows, :], sums)
            return carry

        lax.fori_loop(0, nchunk, sum_pass, 0)

        def a_pass(it, carry):
            c = nchunk - 1 - it
            runs = list(carry)
            for b in reversed(range(SB_CHUNK)):
                j = c * SB_CHUNK + b
                off = pl.multiple_of(j * BLK, BLK)
                mask = causal(j)
                for h in range(2):
                    s = sbuf[h, pl.ds(off, BLK), :BLK]
                    a = jnp.where(mask, jnp.exp(zbuf[h, :, pl.ds(off, BLK)] + s + runs[h]), 0.0)
                    abuf[h, :, pl.ds(off, BLK)] = a.astype(BF16)
                    runs[h] = runs[h] + sbuf[h, pl.ds(off, BLK), BLK:]
            return tuple(runs)

        zero = jnp.zeros((BLK, BLK), F32)
        runs = lax.fori_loop(0, nchunk, a_pass, (zero, zero))
        for h in range(2):
            tot_ref[:, hsl[h]] = runs[h][:, :HDIM]
            o_ref[:, hsl[h]] = _dot(abuf[h], vb[:, hsl[h]])
        if fused:
            pl.when(jnp.logical_and(hp == nh - 1, qi == nq - 1))(g.finish)

    out_blk = pl.BlockSpec((BLK, LANES), lambda h, i: (i, h))
    hbm = pl.BlockSpec(memory_space=pl.ANY)
    in_specs = [pl.BlockSpec((BLK, LANES), lambda h, i: (i, COL_BQ // LANES + h)),
                pl.BlockSpec((t, LANES), lambda h, i: (0, h)),
                pl.BlockSpec((t, LANES), lambda h, i: (0, B_W // LANES + h))]
    out_shape = [jax.ShapeDtypeStruct((t, B_W), F32)] * 2
    scratch = [pltpu.VMEM((2, BLK, t), F32), pltpu.VMEM((2, t, BLK), BF16),
               pltpu.VMEM((2, t, 2 * BLK), F32), pltpu.VMEM((2, BLK, t), BF16)]
    if fused:
        out_shape = out_shape + _gathered_shapes(gather)
    return pl.pallas_call(
        body, name=name, grid=(nh, nq),
        in_specs=in_specs + [hbm] * n,
        out_specs=[out_blk, out_blk] + [hbm] * n,
        out_shape=out_shape,
        scratch_shapes=scratch + (_comm_sems(n) if fused else []),
        compiler_params=_cparams(("arbitrary", "arbitrary")),
    )(p, kv, kv, *(gather if fused else []))


def _sb_bwd(p, kv, tot, do, name, exchange=None):
    t = p.shape[0]
    nq = t // BLK
    nh = B_HEADS // 2
    cw = SB_CHUNK * BLK
    fused = exchange is not None
    n = len(exchange) if fused else 0

    def body(*refs):
        q_ref, kb, vb, tot_ref, do_ref = refs[:5]
        dq_ref, dk_ref, dv_ref = refs[5 + n:8 + n]
        zbuf, dabuf, lbuf, stage, sbuf, abuf, dzbuf, dkt, dvt = refs[8 + 2 * n:17 + 2 * n]
        hp = pl.program_id(0)
        qi = pl.program_id(1)
        if fused:
            ex = _Many(_Exchange, refs[5:5 + n], refs[8 + n:8 + 2 * n], *refs[17 + 2 * n:])
            pl.when(jnp.logical_and(hp == 0, qi == 0))(ex.start)

        @pl.when(qi == 0)
        def _():
            dkt[...] = jnp.zeros_like(dkt)
            dvt[...] = jnp.zeros_like(dvt)
            dzbuf[...] = jnp.zeros_like(dzbuf)
            abuf[...] = jnp.zeros_like(abuf)

        row = lax.broadcasted_iota(jnp.int32, (BLK, BLK), 0)
        col = lax.broadcasted_iota(jnp.int32, (BLK, BLK), 1)
        sums = _sb_sum_matrix(lambda sp, s: sp <= s)
        hsl = [slice(HDIM * h, HDIM * (h + 1)) for h in range(2)]
        dob = [do_ref[:, hsl[h]].astype(BF16) for h in range(2)]
        total =[jnp.concatenate([tot_ref[:, hsl[h]], tot_ref[:, hsl[h]]], axis=1) for h in range(2)]
        nchunk = qi // SB_CHUNK + 1
        for h in range(2):
            zbuf[h] = _dot_nt((q_ref[:, hsl[h]] * SCALE).astype(BF16), kb[:, hsl[h]])
            dabuf[h] = _dot_nt(dob[h], vb[:, hsl[h]])

        col_minus_row = col - row

        def causal(j):
            return col_minus_row < (qi - j) * BLK

        def blocks(c):
            for b in range(SB_CHUNK):
                j = c * SB_CHUNK + b
                yield j, pl.ds(pl.multiple_of(j * BLK, BLK), BLK)

        def l_pass(c, carry):
            for j, blk_ in blocks(c):
                mask = causal(j)
                for h in range(2):
                    lm = jnp.where(mask, -_softplus(zbuf[h, :, blk_]), 0.0)
                    lbuf[h, :, blk_] = lm
                    stage[h, blk_, :] = lm.astype(BF16)
            return carry

        lax.fori_loop(0, nchunk, l_pass, 0)

        def sum_pass():
            def run_(c, carry):
                rows = pl.ds(pl.multiple_of(c * cw, cw), cw)
                for h in range(2):
                    sbuf[h, rows, :] = _dot(stage[h, rows, :], sums)
                return carry
            lax.fori_loop(0, nchunk, run_, 0)

        sum_pass()

        def g_pass(c, carry):
            runs = list(carry)
            for j, blk_ in blocks(c):
                mask = causal(j)
                for h in range(2):
                    upto = sbuf[h, blk_, :]
                    log_a = zbuf[h, :, blk_] + lbuf[h, :, blk_] + (total[h] - runs[h] - upto)
                    a = jnp.where(mask, jnp.exp(log_a), 0.0)
                    abuf[h, :, blk_] = a.astype(BF16)
                    g = a * dabuf[h, :, blk_]
                    dabuf[h, :, blk_] = g
                    stage[h, blk_, :] = g.astype(BF16)
                    runs[h] = runs[h] + _lanes(upto[:, BLK - 1:BLK])
            return tuple(runs)

        zero = jnp.zeros((BLK, BLK), F32)
        lax.fori_loop(0, nchunk, g_pass, (zero, zero))
        sum_pass()

        def dz_pass(c, carry):
            runs = list(carry)
            for j, blk_ in blocks(c):
                mask = causal(j)
                for h in range(2):
                    lm = lbuf[h, :, blk_]
                    g = dabuf[h, :, blk_]
                    upto = sbuf[h, blk_, :]
                    before = runs[h] + upto - g
                    dz = jnp.where(mask, g * jnp.exp(lm) - jnp.exp(zbuf[h, :, blk_] + lm) * before, 0.0)
                    dzbuf[h, :, blk_] = (dz * SCALE).astype(BF16)
                    runs[h] = runs[h] + _lanes(upto[:, BLK - 1:BLK])
            return tuple(runs)

        lax.fori_loop(0, nchunk, dz_pass, (zero, zero))
        for h in range(2):
            dq_ref[:, hsl[h]] = _dot(dzbuf[h], kb[:, hsl[h]])
        q_t = q_ref[...].T.astype(BF16)
        do_t = do_ref[...].T.astype(BF16)
        for h in range(2):
            dkt[hsl[h], :] += _dot(q_t[hsl[h], :], dzbuf[h])
            dvt[hsl[h], :] += _dot(do_t[hsl[h], :], abuf[h])

        @pl.when(qi == nq - 1)
        def _():
            dk_ref[...] = dkt[...].T
            dv_ref[...] = dvt[...].T

        if fused:
            pl.when(jnp.logical_and(hp == nh - 1, qi == nq - 1))(ex.finish)

    blk = lambda h, i: (i, h)
    whole = lambda h, i: (0, h)
    hbm = pl.BlockSpec(memory_space=pl.ANY)
    in_specs = [pl.BlockSpec((BLK, LANES), lambda h, i: (i, COL_BQ // LANES + h)),
                pl.BlockSpec((t, LANES), lambda h, i: (0, h)),
                pl.BlockSpec((t, LANES), lambda h, i: (0, B_W // LANES + h)),
                pl.BlockSpec((BLK, LANES), blk), pl.BlockSpec((BLK, LANES), blk)]
    out_specs = [pl.BlockSpec((BLK, LANES), blk), pl.BlockSpec((t, LANES), whole), pl.BlockSpec((t, LANES), whole)]
    out_shape = [jax.ShapeDtypeStruct((t, B_W), F32)] * 3
    scratch = [pltpu.VMEM((2, BLK, t), F32), pltpu.VMEM((2, BLK, t), F32), pltpu.VMEM((2, BLK, t), F32),
               pltpu.VMEM((2, t, BLK), BF16), pltpu.VMEM((2, t, BLK), F32), pltpu.VMEM((2, BLK, t), BF16),
               pltpu.VMEM((2, BLK, t), BF16), pltpu.VMEM((LANES, t), F32), pltpu.VMEM((LANES, t), F32)]
    if fused:
        out_shape = out_shape + [jax.ShapeDtypeStruct(e.shape, e.dtype) for e in exchange]
    return pl.pallas_call(
        body, name=name, grid=(nh, nq),
        in_specs=in_specs + [hbm] * n,
        out_specs=out_specs + [hbm] * n,
        out_shape=out_shape,
        scratch_shapes=scratch + (_comm_sems(n) if fused else []),
        compiler_params=_cparams(("arbitrary", "arbitrary")),
    )(p, kv, kv, tot, do, *(exchange if fused else []))


def _alibi_slopes(n):
    def pow2(m):
        start = 2.0 ** (-8.0 / m)
        return [start ** (i + 1) for i in range(m)]
    if math.log2(n).is_integer():
        s = pow2(n)
    else:
        c = 2 ** int(math.floor(math.log2(n)))
        s = pow2(c) + pow2(2 * c)[0::2][: n - c]
    return sorted(s, reverse=True)


def _dil_scores(qh, kh, sl, prev, exists=None):
    row = lax.broadcasted_iota(jnp.int32, (BLK, BLK), 0)
    col = lax.broadcasted_iota(jnp.int32, (BLK, BLK), 1)
    dist = row - col + (BLK if prev else 0)
    if prev:
        valid = (col - row) >= jnp.where(exists, 0, 2 * BLK)
    else:
        valid = col <= row
    s = _dot_nt(qh, kh) - sl * dist.astype(F32)
    return s, valid


DIL_UNITS = 2


def _dil_plan(r):
    per_trip = min(r, DIL_UNITS)
    return per_trip, DIL_UNITS // per_trip


def _dil_rows(b, rho, r):
    return pl.ds(b * BLK * r + rho, BLK, stride=r) if r > 1 else pl.ds(b * BLK, BLK)


def _dil_fwd(p, gi, name):
    t = p.shape[0]
    _, r = C_GROUPS[gi]
    per_trip, nsub = _dil_plan(r)
    sbr = BLK * r * nsub
    nsb = t // sbr
    slope_cols = _slope_cols(gi)

    def body(q_ref, kc_ref, kp_ref, vc_ref, vp_ref, sl_ref, o_ref, lse_ref):
        i = pl.program_id(1)

        hsl = [slice(HDIM * h, HDIM * (h + 1)) for h in range(2)]
        sl = [sl_ref[:, HDIM * h:HDIM * h + 1] for h in range(2)]

        def residues(it, carry):
            pairs = [(b, dr) for b in range(nsub) for dr in range(per_trip)]
            units = [(pr, h) for pr in pairs for h in range(2)]
            rows = {(b, dr): _dil_rows(b, it * per_trip + dr, r) for b, dr in pairs}
            blocks, prev_exists = {}, {}
            for b, dr in pairs:
                rw = rows[b, dr]
                if b == 0:
                    before = _dil_rows(nsub - 1, it * per_trip + dr, r)
                    kp, vp, prev_exists[b, dr] = kp_ref[before, :], vp_ref[before, :], i > 0
                else:
                    before = rows[b - 1, dr]
                    kp, vp, prev_exists[b, dr] = kc_ref[before, :], vc_ref[before, :], True
                blocks[b, dr] = [q_ref[rw, :], kc_ref[rw, :], kp, vc_ref[rw, :], vp]
            qh = {u: (blocks[u[0]][0][:, hsl[u[1]]] * SCALE).astype(BF16) for u in units}
            sc = {u: _dil_scores(qh[u], blocks[u[0]][1][:, hsl[u[1]]].astype(BF16), sl[u[1]], False) for u in units}
            sp = {u: _dil_scores(qh[u], blocks[u[0]][2][:, hsl[u[1]]].astype(BF16), sl[u[1]], True, prev_exists[u[0]])
                  for u in units}
            pc, pp, den, lse = {}, {}, {}, {}
            for u in units:
                s_c = jnp.where(sc[u][1], sc[u][0], NEG_BIG)
                s_p = jnp.where(sp[u][1], sp[u][0], NEG_BIG)
                m = jnp.maximum(jnp.max(s_c, axis=1, keepdims=True), jnp.max(s_p, axis=1, keepdims=True))
                pc[u] = jnp.exp(s_c - m)
                pp[u] = jnp.exp(s_p - m)
                den[u] = jnp.sum(pc[u], axis=1, keepdims=True) + jnp.sum(pp[u], axis=1, keepdims=True)
                lse[u] = jnp.broadcast_to(m + jnp.log(den[u]), (BLK, HDIM))
            o = {u: (_dot(pc[u].astype(BF16), blocks[u[0]][3][:, hsl[u[1]]].astype(BF16))
                     + _dot(pp[u].astype(BF16), blocks[u[0]][4][:, hsl[u[1]]].astype(BF16))) / den[u] for u in units}
            for pr in pairs:
                o_ref[rows[pr], :] = jnp.concatenate([o[pr, 0], o[pr, 1]], axis=1)
                lse_ref[rows[pr], :] = jnp.concatenate([lse[pr, 0], lse[pr, 1]], axis=1)
            return carry

        lax.fori_loop(0, r // per_trip, residues, 0)

    def at(col0, pick):
        return pl.BlockSpec((sbr, LANES), lambda c, i: (pick(i), col0 // LANES + c))

    cur = lambda i: i
    prv = lambda i: jnp.maximum(i - 1, 0)
    cq, ck, cv = COL_CQ + gi * C_OUT, COL_CK + gi * C_OUT, COL_CV + gi * C_OUT
    out = pl.BlockSpec((sbr, LANES), lambda c, i: (i, c))
    return pl.pallas_call(
        body, name=name, grid=(C_OUT // LANES, nsb),
        in_specs=[at(cq, cur), at(ck, cur), at(ck, prv), at(cv, cur), at(cv, prv),
                  pl.BlockSpec((1, LANES), lambda c, i: (0, c))],
        out_specs=[out, out], out_shape=[jax.ShapeDtypeStruct((t, C_OUT), F32)] * 2,
        compiler_params=_cparams(("parallel", "parallel")),
    )(p, p, p, p, p, slope_cols)


def _dil_bwd(p, do, o, lse, gi, name):
    t = p.shape[0]
    _, r = C_GROUPS[gi]
    per_trip, nsub = _dil_plan(r)
    sbr = BLK * r * nsub
    nsb = t // sbr
    slope_cols = _slope_cols(gi)

    def body(q_ref, qn_ref, kc_ref, kp_ref, vc_ref, vp_ref, do_ref, don_ref, o_ref, on_ref, l_ref, ln_ref, sl_ref,
             dq_ref, dk_ref, dv_ref):
        i = pl.program_id(1)

        hsl = [slice(HDIM * h, HDIM * (h + 1)) for h in range(2)]
        sl = [sl_ref[:, HDIM * h:HDIM * h + 1] for h in range(2)]

        def residues(it, carry):
            pairs = [(b, dr) for b in range(nsub) for dr in range(per_trip)]
            units = [(pr, h) for pr in pairs for h in range(2)]
            rows = {(b, dr): _dil_rows(b, it * per_trip + dr, r) for b, dr in pairs}
            blocks, has_prev, has_next = {}, {}, {}
            for b, dr in pairs:
                rw = rows[b, dr]
                if b == 0:
                    before = _dil_rows(nsub - 1, it * per_trip + dr, r)
                    kp, vp, has_prev[b, dr] = kp_ref[before, :], vp_ref[before, :], i > 0
                else:
                    kp, vp, has_prev[b, dr] = kc_ref[rows[b - 1, dr], :], vc_ref[rows[b - 1, dr], :], True
                if b == nsub - 1:
                    after = _dil_rows(0, it * per_trip + dr, r)
                    nxt = [ref[after, :] for ref in (qn_ref, don_ref, on_ref, ln_ref)]
                    has_next[b, dr] = i < nsb - 1
                else:
                    nxt = [ref[rows[b + 1, dr], :] for ref in (q_ref, do_ref, o_ref, l_ref)]
                    has_next[b, dr] = True
                blocks[b, dr] = [q_ref[rw, :], nxt[0], kc_ref[rw, :], kp, vc_ref[rw, :], vp, do_ref[rw, :], nxt[1],
                                 o_ref[rw, :], nxt[2], l_ref[rw, :], nxt[3]]
            part = lambda u, k: blocks[u[0]][k][:, hsl[u[1]]]
            qb = {u: part(u, 0).astype(BF16) for u in units}
            qnb = {u: part(u, 1).astype(BF16) for u in units}
            qh = {u: (part(u, 0) * SCALE).astype(BF16) for u in units}
            qnh = {u: (part(u, 1) * SCALE).astype(BF16) for u in units}
            kc = {u: part(u, 2).astype(BF16) for u in units}
            kp = {u: part(u, 3).astype(BF16) for u in units}
            vc = {u: part(u, 4).astype(BF16) for u in units}
            vp = {u: part(u, 5).astype(BF16) for u in units}
            dob = {u: part(u, 6).astype(BF16) for u in units}
            donb = {u: part(u, 7).astype(BF16) for u in units}
            delta = {u: jnp.sum(part(u, 6) * part(u, 8), axis=1, keepdims=True) for u in units}
            deltan = {u: jnp.sum(part(u, 7) * part(u, 9), axis=1, keepdims=True) for u in units}
            lse_c = {u: part(u, 10)[:, :1] for u in units}
            lse_n = {u: part(u, 11)[:, :1] for u in units}
            s_cc = {u: _dil_scores(qh[u], kc[u], sl[u[1]], False) for u in units}
            s_cp = {u: _dil_scores(qh[u], kp[u], sl[u[1]], True, has_prev[u[0]]) for u in units}
            s_nc = {u: _dil_scores(qnh[u], kc[u], sl[u[1]], True, has_next[u[0]]) for u in units}
            da_cc = {u: _dot_nt(dob[u], vc[u]) for u in units}
            da_cp = {u: _dot_nt(dob[u], vp[u]) for u in units}
            da_nc = {u: _dot_nt(donb[u], vc[u]) for u in units}

            def prob(s_ok, lse_col):
                s, ok = s_ok
                return jnp.where(ok, jnp.exp(jnp.where(ok, s, NEG_BIG) - lse_col), 0.0)

            p_cc = {u: prob(s_cc[u], lse_c[u]) for u in units}
            p_cp = {u: prob(s_cp[u], lse_c[u]) for u in units}
            p_nc = {u: prob(s_nc[u], lse_n[u]) for u in units}
            ds_cc = {u: (p_cc[u] * (da_cc[u] - delta[u]) * SCALE).astype(BF16) for u in units}
            ds_cp = {u: (p_cp[u] * (da_cp[u] - delta[u]) * SCALE).astype(BF16) for u in units}
            ds_nc = {u: (p_nc[u] * (da_nc[u] - deltan[u]) * SCALE).astype(BF16) for u in units}
            dq = {u: _dot(ds_cc[u], kc[u]) + _dot(ds_cp[u], kp[u]) for u in units}
            dk = {u: _dot_tn(ds_cc[u], qb[u]) + _dot_tn(ds_nc[u], qnb[u]) for u in units}
            dv = {u: _dot_tn(p_cc[u].astype(BF16), dob[u]) + _dot_tn(p_nc[u].astype(BF16), donb[u]) for u in units}
            for pr in pairs:
                dq_ref[rows[pr], :] = jnp.concatenate([dq[pr, 0], dq[pr, 1]], axis=1)
                dk_ref[rows[pr], :] = jnp.concatenate([dk[pr, 0], dk[pr, 1]], axis=1)
                dv_ref[rows[pr], :] = jnp.concatenate([dv[pr, 0], dv[pr, 1]], axis=1)
            return carry

        lax.fori_loop(0, r // per_trip, residues, 0)

    def at(col0, pick):
        return pl.BlockSpec((sbr, LANES), lambda c, i: (pick(i), col0 // LANES + c))

    cur = lambda i: i
    prv = lambda i: jnp.maximum(i - 1, 0)
    nxt = lambda i: jnp.minimum(i + 1, nsb - 1)
    cq, ck, cv = COL_CQ + gi * C_OUT, COL_CK + gi * C_OUT, COL_CV + gi * C_OUT
    return pl.pallas_call(
        body, name=name, grid=(C_OUT // LANES, nsb),
        in_specs=[at(cq, cur), at(cq, nxt), at(ck, cur), at(ck, prv), at(cv, cur), at(cv, prv),
                  at(0, cur), at(0, nxt), at(0, cur), at(0, nxt), at(0, cur), at(0, nxt),
                  pl.BlockSpec((1, LANES), lambda c, i: (0, c))],
        out_specs=[at(0, cur)] * 3, out_shape=[jax.ShapeDtypeStruct((t, C_OUT), F32)] * 3,
        compiler_params=_cparams(("parallel", "parallel")),
    )(p, p, p, p, p, p, do, do, o, o, lse, lse, slope_cols)


def _dil_merge(os_, ls_, name):
    t, w = os_[0].shape
    tr = _rows(t)

    def body(o0, o1, o2, l0, l1, l2, y_ref, lse_ref):
        a, b, c = l0[...], l1[...], l2[...]
        m = jnp.maximum(jnp.maximum(a, b), c)
        ea, eb, ec = jnp.exp(a - m), jnp.exp(b - m), jnp.exp(c - m)
        den = ea + eb + ec
        y_ref[...] = (ea * o0[...] + eb * o1[...] + ec * o2[...]) / den
        lse_ref[...] = m + jnp.log(den)

    row = pl.BlockSpec((tr, w), lambda i: (i, 0))
    return pl.pallas_call(
        body, name=name, grid=(t // tr,), in_specs=[row] * 6, out_specs=[row, row],
        out_shape=[jax.ShapeDtypeStruct((t, w), F32)] * 2, compiler_params=_cparams(("parallel",)),
    )(*os_, *ls_)


def _gate_fwd(ys, gl, ws, name):
    t = gl.shape[0]
    d = gl.shape[1] // N_BRANCH
    tr = _rows(t)

    def body(ya, yb, yc, gl_ref, wa, wb, wc, m_ref):
        acc = None
        for i, (y, w) in enumerate(((ya, wa), (yb, wb), (yc, wc))):
            z = _dot(y[...].astype(BF16), w[...])
            term = jax.nn.sigmoid(gl_ref[:, i * d:(i + 1) * d]) * z
            acc = term if acc is None else acc + term
        m_ref[...] = acc.astype(m_ref.dtype)

    rows = [pl.BlockSpec((tr, y.shape[1]), lambda i: (i, 0)) for y in ys]
    wsp = [pl.BlockSpec(w.shape, lambda i: (0, 0)) for w in ws]
    return pl.pallas_call(
        body, name=name, grid=(t // tr,),
        in_specs=rows + [pl.BlockSpec((tr, N_BRANCH * d), lambda i: (i, 0))] + wsp,
        out_specs=pl.BlockSpec((tr, d), lambda i: (i, 0)), out_shape=jax.ShapeDtypeStruct((t, d), BF16),
        compiler_params=_cparams(("parallel",)),
    )(*ys, gl, *ws)


def _gate_bwd(dm, ys, gl, ws, name):
    t = gl.shape[0]
    d = gl.shape[1] // N_BRANCH
    tr = _rows(t)

    def body(dm_ref, ya, yb, yc, gl_ref, wa, wb, wc, dya, dyb, dyc, dgl_ref, dwa, dwb, dwc):
        step = pl.program_id(0)
        dmv = dm_ref[...].astype(F32)
        for i, (y, w, dy, dw) in enumerate(((ya, wa, dya, dwa), (yb, wb, dyb, dwb), (yc, wc, dyc, dwc))):
            yb16 = y[...].astype(BF16)
            z = _dot(yb16, w[...])
            sg = jax.nn.sigmoid(gl_ref[:, i * d:(i + 1) * d])
            dgl_ref[:, i * d:(i + 1) * d] = dmv * z * sg * (1.0 - sg)
            e = (dmv * sg).astype(BF16)
            dy[...] = _dot_nt(e, w[...])
            contrib = _dot_tn(yb16, e)

            @pl.when(step == 0)
            def _(dw=dw, contrib=contrib):
                dw[...] = contrib

            @pl.when(step > 0)
            def _(dw=dw, contrib=contrib):
                dw[...] += contrib

    rows = [pl.BlockSpec((tr, y.shape[1]), lambda i: (i, 0)) for y in ys]
    wsp = [pl.BlockSpec(w.shape, lambda i: (0, 0)) for w in ws]
    gsp = pl.BlockSpec((tr, N_BRANCH * d), lambda i: (i, 0))
    return pl.pallas_call(
        body, name=name, grid=(t // tr,),
        in_specs=[pl.BlockSpec((tr, d), lambda i: (i, 0))] + rows + [gsp] + wsp,
        out_specs=rows + [gsp] + wsp,
        out_shape=[jax.ShapeDtypeStruct(y.shape, F32) for y in ys] + [jax.ShapeDtypeStruct(gl.shape, F32)]
        + [jax.ShapeDtypeStruct(w.shape, F32) for w in ws],
        compiler_params=_cparams(("arbitrary",)),
    )(dm, *ys, gl, *ws)


def _adamw(w, m, v, gparts, name):
    r, c = w.shape
    n = gparts.shape[0]
    br = LANES if r % LANES == 0 else r
    c1 = 1.0 - ADAM_B1 ** ADAM_STEP
    c2 = 1.0 - ADAM_B2 ** ADAM_STEP

    def body(w_ref, m_ref, v_ref, g_ref, go_ref, d_ref, mo_ref, vo_ref):
        g = g_ref[0].astype(F32)
        for i in range(1, n):
            g = g + g_ref[i].astype(F32)
        mn = ADAM_B1 * m_ref[...] + (1.0 - ADAM_B1) * g
        vn = ADAM_B2 * v_ref[...] + (1.0 - ADAM_B2) * (g * g)
        go_ref[...] = g
        mo_ref[...] = mn
        vo_ref[...] = vn
        d_ref[...] = -ADAM_LR * ((mn / c1) / (jnp.sqrt(vn / c2) + ADAM_EPS) + ADAM_WD * w_ref[...])

    blk = pl.BlockSpec((br, c), lambda i: (i, 0))
    return pl.pallas_call(
        body, name=name, grid=(r // br,),
        in_specs=[blk, blk, blk, pl.BlockSpec((n, br, c), lambda i: (0, i, 0))],
        out_specs=[blk] * 4, out_shape=[jax.ShapeDtypeStruct((r, c), F32)] * 4,
        compiler_params=_cparams(("parallel",)),
    )(w, m, v, gparts)


def _my_coords():
    return lax.axis_index("x"), lax.axis_index("y"), lax.axis_index("c")


COMM_SEMS = [pltpu.SemaphoreType.DMA((7,)), pltpu.SemaphoreType.DMA((7,)), pltpu.SemaphoreType.DMA]


class _Gather:
    def __init__(self, x_ref, out_ref, send_sems, recv_sems, local_sem):
        self.x_ref, self.out_ref = x_ref, out_ref
        self.send_sems, self.recv_sems, self.local_sem = send_sems, recv_sems, local_sem
        self.m_per = x_ref.shape[0]
        x, y, c = _my_coords()
        self.c = c
        self.me, self.sibling = (x, y, c), (x, y, 1 - c)
        self.chips = [(1 - x, y), (x, 1 - y), (1 - x, 1 - y)]

    def rows(self, px, py, pc):
        return self.out_ref.at[pl.ds((4 * px + 2 * py + pc) * self.m_per, self.m_per), :]

    def copy(self, k, block, to, src=None):
        return pltpu.make_async_remote_copy(
            src_ref=self.rows(*block) if src is None else src, dst_ref=self.rows(*block),
            send_sem=self.send_sems.at[k], recv_sem=self.recv_sems.at[k], device_id=to, device_id_type=MESH)

    def mine(self):
        return pltpu.make_async_copy(self.x_ref, self.rows(*self.me), self.local_sem)

    def first(self):
        out = [self.copy(0, self.me, self.sibling, src=self.x_ref)]
        return out + [self.copy(1 + j, self.me, (*chip, self.c), src=self.x_ref) for j, chip in enumerate(self.chips)]

    def passed(self):
        return [self.copy(4 + j, (*chip, self.c), self.sibling) for j, chip in enumerate(self.chips)]

    def start(self):
        self.mine().start()
        for cp in self.first():
            cp.start()

    def forward(self):
        passed = self.passed()
        for j, chip in enumerate(self.chips):
            self.copy(1 + j, (*chip, self.c), self.me).wait_recv()
            passed[j].start()

    def finish(self):
        self.copy(0, self.sibling, self.me).wait_recv()
        for j, chip in enumerate(self.chips):
            self.copy(4 + j, (*chip, 1 - self.c), self.me).wait_recv()
        for cp in self.first() + self.passed():
            cp.wait_send()
        self.mine().wait()


class _Exchange:
    def __init__(self, send_ref, recv_ref, send_sems, recv_sems, local_sem):
        self.send_ref, self.recv_ref = send_ref, recv_ref
        self.send_sems, self.recv_sems, self.local_sem = send_sems, recv_sems, local_sem
        x, y, c = _my_coords()
        self.me = 4 * x + 2 * y + c
        self.peers = []
        for k in range(1, N_DEV):
            px = 1 - x if k & 4 else x
            py = 1 - y if k & 2 else y
            pc = 1 - c if k & 1 else c
            self.peers.append((4 * px + 2 * py + pc, (px, py, pc)))

    def mine(self):
        return pltpu.make_async_copy(self.send_ref.at[self.me], self.recv_ref.at[self.me], self.local_sem)

    def copy(self, k, src_slot, dst_slot):
        return pltpu.make_async_remote_copy(
            src_ref=self.send_ref.at[src_slot], dst_ref=self.recv_ref.at[dst_slot],
            send_sem=self.send_sems.at[k], recv_sem=self.recv_sems.at[k],
            device_id=self.peers[k][1], device_id_type=MESH)

    def start(self):
        self.mine().start()
        for k, (peer, _) in enumerate(self.peers):
            self.copy(k, peer, self.me).start()

    def finish(self):
        for k, (peer, _) in enumerate(self.peers):
            self.copy(k, peer, self.me).wait_send()
            self.copy(k, self.me, peer).wait_recv()
        self.mine().wait()


def _all_gather(x_shard, in_vmem, with_sum, name):
    m_per, n = x_shard.shape

    def body(x_ref, out_ref, *rest):
        if with_sum:
            sum_ref, send_sems, recv_sems, local_sem = rest
        else:
            send_sems, recv_sems, local_sem = rest
        g = _Gather(x_ref, out_ref, send_sems, recv_sems, local_sem)
        g.start()
        g.forward()
        g.finish()
        if with_sum:
            acc = out_ref[pl.ds(0, m_per), :]
            for d in range(1, N_DEV):
                acc = acc + out_ref[pl.ds(d * m_per, m_per), :]
            sum_ref[...] = acc

    space = pltpu.VMEM if in_vmem else pl.ANY
    out_shape = [jax.ShapeDtypeStruct((N_DEV * m_per, n), x_shard.dtype)]
    out_specs = [pl.BlockSpec(memory_space=space)]
    if with_sum:
        out_shape.append(jax.ShapeDtypeStruct((m_per, n), x_shard.dtype))
        out_specs.append(pl.BlockSpec(memory_space=pltpu.VMEM))
    res = pl.pallas_call(
        body, name=name, out_shape=out_shape, in_specs=[pl.BlockSpec(memory_space=space)], out_specs=out_specs,
        scratch_shapes=COMM_SEMS, compiler_params=pltpu.CompilerParams(vmem_limit_bytes=VMEM_LIMIT),
    )(x_shard)
    return res if with_sum else res[0]


def _comm_sems(n):
    return [pltpu.SemaphoreType.DMA((n, 7)), pltpu.SemaphoreType.DMA((n, 7)), pltpu.SemaphoreType.DMA((n,))]


class _Many:
    def __init__(self, kind, ins, outs, send_sems, recv_sems, local_sems):
        self.parts = [kind(i, o, send_sems.at[b], recv_sems.at[b], local_sems.at[b])
                      for b, (i, o) in enumerate(zip(ins, outs))]

    def start(self):
        for part in self.parts:
            part.start()

    def forward(self):
        for part in self.parts:
            part.forward()

    def finish(self):
        for part in self.parts:
            part.finish()


def _gathered_shapes(shards):
    return [jax.ShapeDtypeStruct((N_DEV * s.shape[0],) + s.shape[1:], s.dtype) for s in shards]


def _all_gather_many(shards, name):
    n = len(shards)

    def body(*refs):
        g = _Many(_Gather, refs[:n], refs[n:2 * n], *refs[2 * n:])
        g.start()
        g.forward()
        g.finish()

    hbm = pl.BlockSpec(memory_space=pl.ANY)
    return pl.pallas_call(body, name=name, out_shape=_gathered_shapes(shards), in_specs=[hbm] * n,
                          out_specs=[hbm] * n, scratch_shapes=_comm_sems(n))(*shards)


def _all_to_all_many(sends, name):
    n = len(sends)

    def body(*refs):
        ex = _Many(_Exchange, refs[:n], refs[n:2 * n], *refs[2 * n:])
        ex.start()
        ex.finish()

    hbm = pl.BlockSpec(memory_space=pl.ANY)
    return pl.pallas_call(body, name=name, out_shape=[jax.ShapeDtypeStruct(s.shape, s.dtype) for s in sends],
                          in_specs=[hbm] * n, out_specs=[hbm] * n, scratch_shapes=_comm_sems(n))(*sends)


def _row(v):
    return v.reshape(1, -1)


def _ffn_fwd(x, w_in, w_out, g_pre, g_post, m, res_w, tag):
    shift, scale, gate = m[0], m[1], m[2]
    mpre = _row(g_pre * (1.0 + scale))
    mpost = _row(res_w * gate * g_post)
    h = _rms_fwd(x, mpre, _row(shift), None, BF16, tag + "_pre")
    u = _matmul(h, w_in, out_dtype=BF16, name=tag + "_in")
    s = _swiglu_fwd(u, tag + "_act")
    y = _matmul(s, w_out, name=tag + "_out")
    x_new = _rms_fwd(y, mpost, jnp.zeros_like(mpost), x, F32, tag + "_post")
    return x_new, (x, h, u, s, y, mpre, mpost)


def _sub_bwd_post(dx_new, y, mpost, g_post, gate, res_w, tag):
    dy, c1, _ = _rms_bwd(dx_new, y, mpost, None, BF16, tag + "_post_bwd")
    c1 = c1[0]
    return dy, c1 * res_w * g_post, c1 * res_w * gate


def _sub_bwd_pre(dh, x, mpre, dx_new, g_pre, scale, tag):
    dx, c2, c3 = _rms_bwd(dh, x, mpre, dx_new, F32, tag + "_pre_bwd")
    c2, c3 = c2[0], c3[0]
    return dx, c3, c2 * g_pre, c2 * (1.0 + scale)


def _ffn_bwd(dx_new, saved, w_in, w_out, g_pre, g_post, m, res_w, tag):
    x, h, u, s, y, mpre, mpost = saved
    scale, gate = m[1], m[2]
    dy, dgate, dg_post = _sub_bwd_post(dx_new, y, mpost, g_post, gate, res_w, tag)
    ds = _matmul(dy, w_out, tb=True, out_dtype=BF16, name=tag + "_out_dx")
    dw_out = _matmul(s, dy, ta=True, out_dtype=BF16, name=tag + "_out_dw")
    du = _swiglu_bwd(u, ds, tag + "_act_bwd")
    dh = _matmul(du, w_in, tb=True, name=tag + "_in_dx")
    dw_in = _matmul(h, du, ta=True, out_dtype=BF16, name=tag + "_in_dw")
    dx, dshift, dscale, dg_pre = _sub_bwd_pre(dh, x, mpre, dx_new, g_pre, scale, tag)
    return dx, dw_in, dw_out, jnp.stack([dshift, dscale, dgate]), dg_pre, dg_post


def _slope_cols(gi):
    _, r = C_GROUPS[gi]
    sl = jnp.asarray(_alibi_slopes(C_HEADS)[gi * C_HPG:(gi + 1) * C_HPG], F32) * float(r)
    return jnp.repeat(sl, HDIM).reshape(1, C_OUT)


def _mix_fwd(x, w, g_pre, g_post, m, lb, hn, tag, gather=None):
    t, d = x.shape
    shift, scale, gate = m[0], m[1], m[2]
    mpre = _row(g_pre * (1.0 + scale))
    mpost = _row(gate * g_post)
    h = _rms_fwd(x, mpre, _row(shift), None, BF16, tag + "_pre")
    p = _matmul(h, w["w_in"], name=tag + "_in")
    hn2 = _row(jnp.tile(hn, 2))
    ya, oa, states = _hgrn_fwd(p, _row(lb), hn2, tag + "_hgrn")
    kv = p[:, COL_BK:COL_CQ].astype(BF16)
    if gather is None:
        (yb, sb_tot), gathered = _sb_fwd(p, kv, tag + "_sb"), None
    else:
        res = _sb_fwd(p, kv, tag + "_sb_gather", gather)
        yb, sb_tot, gathered = res[0], res[1], list(res[2:])
    og, lg = zip(*[_dil_fwd(p, gi, tag + "_dil%d" % gi) for gi in range(len(C_GROUPS))])
    yc, lse_c = _dil_merge(og, lg, tag + "_dil_merge")
    gl = p[:, COL_GATE:]
    ws = (w["w_branch_a"], w["w_branch_b"], w["w_branch_c"])
    merged = _gate_fwd((ya, yb, yc), gl, ws, tag + "_gate")
    y = _matmul(merged, w["w_out"], name=tag + "_out")
    x_new = _rms_fwd(y, mpost, jnp.zeros_like(mpost), x, F32, tag + "_post")
    return x_new, (x, h, p, hn2, ya, oa, states, yb, kv, sb_tot, yc, lse_c, gl, merged, y, mpre, mpost), gathered


def _mix_bwd(dx_new, saved, w, g_pre, g_post, m, lb, tag, exchange=None):
    x, h, p, hn2, ya, oa, states, yb, kv, sb_tot, yc, lse_c, gl, merged, y, mpre, mpost = saved
    t = x.shape[0]
    scale, gate = m[1], m[2]
    dy, dgate, dg_post = _sub_bwd_post(dx_new, y, mpost, g_post, gate, 1.0, tag)
    dmerged = _matmul(dy, w["w_out"], tb=True, out_dtype=BF16, name=tag + "_out_dx")
    dw_out = _matmul(merged, dy, ta=True, out_dtype=BF16, name=tag + "_out_dw")
    ws = (w["w_branch_a"], w["w_branch_b"], w["w_branch_c"])
    dya, dyb, dyc, dgl, dwa, dwb, dwc = _gate_bwd(dmerged, (ya, yb, yc), gl, ws, tag + "_gate_bwd")
    dqa, dfa, dia, dga, dlb, dhn = _hgrn_bwd(p, _row(lb), hn2, oa, states, dya, tag + "_hgrn_bwd")
    if exchange is None:
        (dbq, dbk, dbv), received = _sb_bwd(p, kv, sb_tot, dyb, tag + "_sb_bwd"), None
    else:
        res = _sb_bwd(p, kv, sb_tot, dyb, tag + "_sb_bwd_exchange", exchange)
        dbq, dbk, dbv, received = res[0], res[1], res[2], list(res[3:])
    dcq, dck, dcv = zip(*[_dil_bwd(p, dyc, yc, lse_c, gi, tag + "_dil%d_bwd" % gi) for gi in range(len(C_GROUPS))])
    dp = jnp.concatenate([dqa, dfa, dia, dga, dbq, dbk, dbv, *dcq, *dck, *dcv, dgl], axis=1).astype(BF16)
    dh = _matmul(dp, w["w_in"], tb=True, name=tag + "_in_dx")
    dw_in = _matmul(h, dp, ta=True, out_dtype=BF16, name=tag + "_in_dw")
    dx, dshift, dscale, dg_pre = _sub_bwd_pre(dh, x, mpre, dx_new, g_pre, scale, tag)
    dhn_v = jnp.sum(dhn, axis=(0, 1))
    dhn_v = dhn_v[:A_VDIM] + dhn_v[A_VDIM:]
    dws = dict(w_in=dw_in, w_out=dw_out, w_branch_a=dwa.astype(BF16), w_branch_b=dwb.astype(BF16),
               w_branch_c=dwc.astype(BF16))
    return dx, dws, jnp.stack([dshift, dscale, dgate]), dg_pre, dg_post, dlb[0], dhn_v, received


class _LocalWeights:
    def __init__(self, wts):
        self.wts = wts

    def first(self):
        return None

    def shard(self, l):
        return None

    def layer(self, l, gathered):
        return {k: v[l] for k, v in self.wts.items()}

    fused = False

    def pack(self, names, dws):
        return [dws[k] for k in names]

    def last(self, packed):
        return packed


class _ShardedWeights:
    def __init__(self, shards):
        self.shards = shards

    def shard(self, l):
        return [self.shards[k][l].astype(BF16) for k in BIG_WEIGHTS]

    def first(self):
        return _all_gather_many(self.shard(0), "weights_all_gather")

    def layer(self, l, gathered):
        out = {}
        for k, got in zip(BIG_WEIGHTS, gathered):
            _, r, c = self.shards[k].shape
            out[k] = got if k in ROW_SHARDED else got.reshape(N_DEV, r, c).transpose(1, 0, 2).reshape(r, N_DEV * c)
        return out

    fused = True

    def pack(self, names, dws):
        out = []
        for k in names:
            _, r, c = self.shards[k].shape
            g = dws[k]
            out.append(g.reshape(N_DEV, r, c) if k in ROW_SHARDED else g.reshape(r, N_DEV, c).transpose(1, 0, 2))
        return out

    def last(self, packed):
        return _all_to_all_many(packed, "grads_all_to_all")

    def partial_sums(self, received, depth):
        return {k: jnp.concatenate([received[l, k] for l in range(depth)], axis=1) for k in BIG_WEIGHTS}


def _local_step(x, target, mod, norm_g, lb_all, hnorm, supply):
    depth = mod.shape[0]
    d = x.shape[1]
    saved, wls = [], []
    gathered = supply.first()
    for l in range(depth):
        wl = supply.layer(l, gathered)
        wls.append(wl)
        x, s0 = _ffn_fwd(x, wl["ffn1_w_in"], wl["ffn1_w_out"], norm_g[l, 0], norm_g[l, 1], mod[l, 0], 0.5, "ffn1")
        nxt = supply.shard(l + 1) if l + 1 < depth else None
        x, s1, gathered = _mix_fwd(x, wl, norm_g[l, 2], norm_g[l, 3], mod[l, 1], lb_all[l], hnorm[l], "mix", nxt)
        x, s2 = _ffn_fwd(x, wl["ffn2_w_in"], wl["ffn2_w_out"], norm_g[l, 4], norm_g[l, 5], mod[l, 2], 0.5, "ffn2")
        saved.append((s0, s1, s2))
    dx, sq = _loss_head(x, target, "loss_head")
    loss = 0.5 * jnp.sum(sq) / d
    dmod, dng, dlb, dhn = [], [], [], []
    early = ("ffn2_w_in", "ffn2_w_out")
    late = tuple(k for k in BIG_WEIGHTS if k not in early)
    returned = {}
    waiting = []
    for l in reversed(range(depth)):
        wl = wls[l]
        s0, s1, s2 = saved[l]
        dx, dwi2, dwo2, dm2, dgp2, dgq2 = _ffn_bwd(dx, s2, wl["ffn2_w_in"], wl["ffn2_w_out"], norm_g[l, 4],
                                                   norm_g[l, 5], mod[l, 2], 0.5, "ffn2")
        waiting += zip([(l, k) for k in early], supply.pack(early, dict(ffn2_w_in=dwi2, ffn2_w_out=dwo2)))
        keys, bufs = [k for k, _ in waiting], [b for _, b in waiting]
        dx, dwm, dm1, dgp1, dgq1, dlb_l, dhn_l, received = _mix_bwd(
            dx, s1, wl, norm_g[l, 2], norm_g[l, 3], mod[l, 1], lb_all[l], "mix", bufs if supply.fused else None)
        returned.update(zip(keys, received if supply.fused else bufs))
        dx, dwi1, dwo1, dm0, dgp0, dgq0 = _ffn_bwd(dx, s0, wl["ffn1_w_in"], wl["ffn1_w_out"], norm_g[l, 0],
                                                   norm_g[l, 1], mod[l, 0], 0.5, "ffn1")
        dmod.append(jnp.stack([dm0, dm1, dm2]))
        dng.append(jnp.stack([dgp0, dgq0, dgp1, dgq1, dgp2, dgq2]))
        dlb.append(dlb_l)
        dhn.append(dhn_l)
        waiting = list(zip([(l, k) for k in late], supply.pack(late, dict(dwm, ffn1_w_in=dwi1, ffn1_w_out=dwo1))))
    returned.update(zip([k for k, _ in waiting], supply.last([b for _, b in waiting])))
    rev = lambda lst: jnp.stack(lst[::-1])
    return loss, dx, rev(dmod), rev(dng), rev(dlb), rev(dhn), returned


def _lb_all(logits):
    lb_p = jax.nn.softmax(logits.astype(F32), axis=0)
    return jnp.cumsum(lb_p, axis=0) - lb_p[0:1]


def _pad_rows(a, rows):
    return jnp.pad(a, ((0, rows - a.shape[0]), (0, 0)))


def kernel(x, c, w_ada, b_ada, norm_g, ffn1_w_in, ffn1_w_out, w_in, hgrn_lb_logits, hgrn_norm_g, w_branch_a, w_branch_b, w_branch_c, w_out, ffn2_w_in, ffn2_w_out, loss_target, m_w_ada, m_b_ada, m_norm_g, m_ffn1_w_in, m_ffn1_w_out, m_w_in, m_hgrn_lb_logits, m_hgrn_norm_g, m_w_branch_a, m_w_branch_b, m_w_branch_c, m_w_out, m_ffn2_w_in, m_ffn2_w_out, v_w_ada, v_b_ada, v_norm_g, v_ffn1_w_in, v_ffn1_w_out, v_w_in, v_hgrn_lb_logits, v_hgrn_norm_g, v_w_branch_a, v_w_branch_b, v_w_branch_c, v_w_out, v_ffn2_w_in, v_ffn2_w_out):
    weights = dict(w_ada=w_ada, b_ada=b_ada, norm_g=norm_g, ffn1_w_in=ffn1_w_in, ffn1_w_out=ffn1_w_out, w_in=w_in,
                   hgrn_lb_logits=hgrn_lb_logits, hgrn_norm_g=hgrn_norm_g, w_branch_a=w_branch_a,
                   w_branch_b=w_branch_b, w_branch_c=w_branch_c, w_out=w_out, ffn2_w_in=ffn2_w_in,
                   ffn2_w_out=ffn2_w_out)
    mom1 = dict(w_ada=m_w_ada, b_ada=m_b_ada, norm_g=m_norm_g, ffn1_w_in=m_ffn1_w_in, ffn1_w_out=m_ffn1_w_out,
                w_in=m_w_in, hgrn_lb_logits=m_hgrn_lb_logits, hgrn_norm_g=m_hgrn_norm_g, w_branch_a=m_w_branch_a,
                w_branch_b=m_w_branch_b, w_branch_c=m_w_branch_c, w_out=m_w_out, ffn2_w_in=m_ffn2_w_in,
                ffn2_w_out=m_ffn2_w_out)
    mom2 = dict(w_ada=v_w_ada, b_ada=v_b_ada, norm_g=v_norm_g, ffn1_w_in=v_ffn1_w_in, ffn1_w_out=v_ffn1_w_out,
                w_in=v_w_in, hgrn_lb_logits=v_hgrn_lb_logits, hgrn_norm_g=v_hgrn_norm_g, w_branch_a=v_w_branch_a,
                w_branch_b=v_w_branch_b, w_branch_c=v_w_branch_c, w_out=v_w_out, ffn2_w_in=v_ffn2_w_in,
                ffn2_w_out=v_ffn2_w_out)
    order = list(weights)
    depth, d, ada_cols = w_ada.shape
    nd = d // LANES
    xi, yi, ci = _my_coords()
    me = 4 * xi + 2 * yi + ci

    small = jnp.concatenate([c.reshape(nd, LANES), norm_g.reshape(depth * 6, LANES)], axis=0)
    g1 = _all_gather(small, True, False, "small_all_gather").reshape(N_DEV, small.shape[0], LANES)
    c_act = _silu(g1[:, :nd].reshape(N_DEV, d))
    norm_full = g1[:, nd:].reshape(N_DEV, depth, 6, LANES).transpose(1, 2, 0, 3).reshape(depth, 6, d)

    c_pad = _pad_rows(c_act, 16)
    mod_sh = jnp.stack([_matmul(c_pad, w_ada[l], name="ada_mod")[:N_DEV]
                        + lax.dynamic_slice_in_dim(b_ada[l], me * ada_cols, ada_cols)[None]
                        for l in range(depth)])
    g2 = _all_gather(mod_sh.reshape(-1, LANES), True, False, "mod_all_gather")
    g2 = g2.reshape(N_DEV, depth, N_DEV, ada_cols)
    mod = lax.dynamic_index_in_dim(g2, me, axis=2, keepdims=False)
    mod = mod.transpose(1, 0, 2).reshape(depth, 3, 3, d)

    supply = _ShardedWeights({k: weights[k] for k in BIG_WEIGHTS})
    lb_all, lb_vjp = jax.vjp(_lb_all, hgrn_lb_logits)

    loss, dx, dmod, dng, dlb, dhn, received = _local_step(x[0], loss_target[0], mod, norm_full, lb_all,
                                                          hgrn_norm_g, supply)
    loss = lax.psum(loss, ("x", "y", "c"))

    dhn_pad = jnp.pad(dhn.reshape(-1), (0, 8 * LANES - dhn.size))
    pieces = [dmod.reshape(-1), dng.reshape(-1), dlb.reshape(-1), dhn_pad]
    sizes = [p_.size for p_ in pieces]
    smallg = jnp.concatenate(pieces).reshape(-1, LANES)
    g3, gsum = _all_gather(smallg, True, True, "small_grads_all_gather")
    g3 = g3.reshape(N_DEV, -1)
    gsum = gsum.reshape(-1)
    dmod_all = g3[:, :sizes[0]].reshape(N_DEV, depth, 9 * d)
    o1 = sizes[0]
    grads = {}
    grads["b_ada"] = gsum[:o1].reshape(depth, 9 * d)
    dng_sum = gsum[o1:o1 + sizes[1]].reshape(depth, 6, nd, LANES)
    grads["norm_g"] = lax.dynamic_index_in_dim(dng_sum, me, axis=2, keepdims=False)
    o2 = o1 + sizes[1]
    dlb_sum = gsum[o2:o2 + sizes[2]].reshape(depth, A_QK)
    grads["hgrn_lb_logits"] = lb_vjp(dlb_sum)[0]
    o3 = o2 + sizes[2]
    grads["hgrn_norm_g"] = gsum[o3:o3 + dhn.size].reshape(depth, A_VDIM)
    dmod_mine = lax.dynamic_slice_in_dim(dmod_all, me * ada_cols, ada_cols, axis=2)
    grads["w_ada"] = jnp.stack([_matmul(c_pad, _pad_rows(dmod_mine[:, l], 16), ta=True, name="ada_dw")
                                for l in range(depth)])

    gparts = supply.partial_sums(received, depth)

    outs = {}
    for k in order:
        w = weights[k]
        w2 = w.reshape(-1, w.shape[-1])
        gp = gparts[k] if k in gparts else grads[k].reshape((1,) + w2.shape)
        res = _adamw(w2, mom1[k].reshape(w2.shape), mom2[k].reshape(w2.shape), gp, "adamw")
        outs[k] = [r.reshape(w.shape) for r in res]
    return (loss, dx[None], *[outs[k][0] for k in order], *[outs[k][1] for k in order],
            *[outs[k][2] for k in order], *[outs[k][3] for k in order])
```

```python
import functools
import math

import jax
import jax.numpy as jnp
from jax import lax
from jax.experimental import pallas as pl
from jax.experimental.pallas import tpu as pltpu

F32 = jnp.float32
BF16 = jnp.bfloat16

A_HEADS, A_KDIM, A_VDIM, A_CHUNK = 6, 128, 64, 64
B_HEADS, HDIM = 6, 64
C_GROUPS = ((128, 1), (512, 4), (2048, 16))
C_HPG = 4
C_HEADS = C_HPG * len(C_GROUPS)
N_BRANCH = 3
EPS = 1e-6
NEG_BIG = -1e30
TINY = 1e-30
A_QK = A_HEADS * A_KDIM
A_V = A_HEADS * A_VDIM
B_W = B_HEADS * HDIM
C_W = C_HEADS * HDIM
C_OUT = C_HPG * HDIM
COL_AQ, COL_AF, COL_AI, COL_AG = 0, A_QK, 2 * A_QK, 2 * A_QK + A_V
COL_BQ = 2 * A_QK + 2 * A_V
COL_BK, COL_BV = COL_BQ + B_W, COL_BQ + 2 * B_W
COL_CQ = COL_BQ + 3 * B_W
COL_CK, COL_CV = COL_CQ + C_W, COL_CQ + 2 * C_W
COL_GATE = COL_CQ + 3 * C_W

ADAM_LR, ADAM_B1, ADAM_B2, ADAM_EPS, ADAM_WD, ADAM_STEP = 0.001, 0.9, 0.999, 1e-08, 0.01, 10

N_DEV = 8
LANES = 128
VMEM_LIMIT = 48 * 1024 * 1024
MATMUL_VMEM_BUDGET = 28 * 1024 * 1024
SUB = 16
EXP_CLAMP = 80.0
MESH = pl.DeviceIdType.MESH

BIG_WEIGHTS = ("ffn1_w_in", "ffn1_w_out", "w_in", "w_branch_a", "w_branch_b", "w_branch_c", "w_out",
               "ffn2_w_in", "ffn2_w_out")
ROW_SHARDED = ("ffn1_w_out", "w_out", "ffn2_w_out")


def _cparams(sem):
    return pltpu.CompilerParams(dimension_semantics=sem, vmem_limit_bytes=VMEM_LIMIT)


def _tile(n, cap):
    best, t = None, LANES
    while t <= min(n, cap):
        if n % t == 0:
            best = t
        t += LANES
    return best or n


def _rows(t, cap=256):
    r = cap
    while t % r:
        r //= 2
    return r


def _divisors(n):
    return [t for t in range(LANES, n + 1, LANES) if n % t == 0] or [n]


def _matmul_tiles(m, n, k, a_size, b_size, o_size):
    best, best_key = None, None
    for tm in _divisors(m):
        for tn in _divisors(n):
            for tk in _divisors(k):
                if tm > 1024 or tn > 3072 or tk > 4096:
                    continue
                cast = (tm * tk * 2 if a_size > 2 else 0) + (tk * tn * 2 if b_size > 2 else 0)
                need = 2 * (tm * tk * a_size + tk * tn * b_size + tm * tn * o_size) + 2 * tm * tn * 4 + cast
                if need > MATMUL_VMEM_BUDGET:
                    continue
                key = (tm * tn * tk, tk)
                if best_key is None or key > best_key:
                    best, best_key = (tm, tn, tk), key
    return best


def _dot(a, b):
    return jnp.dot(a, b, preferred_element_type=F32)


def _dot_nt(a, b):
    return lax.dot_general(a, b, (((1,), (1,)), ((), ())), preferred_element_type=F32)


def _dot_tn(a, b):
    return lax.dot_general(a, b, (((0,), (0,)), ((), ())), preferred_element_type=F32)


def _split3(x):
    h = x.astype(BF16)
    r = x - h.astype(F32)
    m = r.astype(BF16)
    lo = (r - m.astype(F32)).astype(BF16)
    return h, m, lo


def _ones_left(mat01, x):
    h, m, lo = _split3(x)
    return _dot(mat01, h) + _dot(mat01, m) + _dot(mat01, lo)


def _silu(x):
    return x * jax.nn.sigmoid(x)


def _dsilu(x):
    s = jax.nn.sigmoid(x)
    return s * (1.0 + x * (1.0 - s))


def _matmul(a, b, *, ta=False, tb=False, out_dtype=F32, name):
    if ta:
        kdim, m = a.shape
    else:
        m, kdim = a.shape
    n = b.shape[0] if tb else b.shape[1]
    tm, tn, tk = _matmul_tiles(m, n, kdim, a.dtype.itemsize, b.dtype.itemsize, jnp.dtype(out_dtype).itemsize)
    nk = kdim // tk
    ni, nj = m // tm, n // tn
    a_bytes, b_bytes = m * kdim * a.dtype.itemsize, kdim * n * b.dtype.itemsize
    j_outer = nk == 1 and (b_bytes + a_bytes * nj) < (a_bytes + b_bytes * ni)
    dims = (((0 if ta else 1,), (1 if tb else 0,)), ((), ()))

    def body(a_ref, b_ref, o_ref, *scratch):
        p = lax.dot_general(a_ref[...].astype(BF16), b_ref[...].astype(BF16), dims, preferred_element_type=F32)
        if nk == 1:
            o_ref[...] = p.astype(o_ref.dtype)
            return
        acc = scratch[0]
        k = pl.program_id(2)

        @pl.when(k == 0)
        def _():
            acc[...] = p

        @pl.when(k > 0)
        def _():
            acc[...] += p

        @pl.when(k == nk - 1)
        def _():
            o_ref[...] = acc[...].astype(o_ref.dtype)

    def spec(shape, pick):
        if j_outer:
            return pl.BlockSpec(shape, lambda j, i, k: pick(i, j, k))
        return pl.BlockSpec(shape, lambda i, j, k: pick(i, j, k))

    a_spec = spec((tk, tm), lambda i, j, k: (k, i)) if ta else spec((tm, tk), lambda i, j, k: (i, k))
    b_spec = spec((tn, tk), lambda i, j, k: (j, k)) if tb else spec((tk, tn), lambda i, j, k: (k, j))
    return pl.pallas_call(
        body, name=name, grid=(nj, ni, nk) if j_outer else (ni, nj, nk), in_specs=[a_spec, b_spec],
        out_specs=spec((tm, tn), lambda i, j, k: (i, j)),
        out_shape=jax.ShapeDtypeStruct((m, n), out_dtype),
        scratch_shapes=[pltpu.VMEM((tm, tn), F32)] if nk > 1 else [],
        compiler_params=_cparams(("parallel", "parallel", "arbitrary")),
    )(a, b)


def _rms_fwd(z, mcol, acol, res, out_dtype, name):
    t, d = z.shape
    tr = _rows(t)
    has_res = res is not None

    def body(*refs):
        if has_res:
            z_ref, m_ref, a_ref, r_ref, o_ref = refs
        else:
            z_ref, m_ref, a_ref, o_ref = refs
        zf = z_ref[...]
        r = lax.rsqrt(jnp.mean(zf * zf, axis=-1, keepdims=True) + EPS)
        y = zf * r * m_ref[...] + a_ref[...]
        if has_res:
            y = r_ref[...] + y
        o_ref[...] = y.astype(o_ref.dtype)

    row = pl.BlockSpec((tr, d), lambda i: (i, 0))
    col = pl.BlockSpec((1, d), lambda i: (0, 0))
    ins = [z, mcol, acol] + ([res] if has_res else [])
    return pl.pallas_call(
        body, name=name, grid=(t // tr,), in_specs=[row, col, col] + ([row] if has_res else []),
        out_specs=row, out_shape=jax.ShapeDtypeStruct((t, d), out_dtype),
        compiler_params=_cparams(("parallel",)),
    )(*ins)


def _rms_bwd(d_out, z, mcol, dres, out_dtype, name):
    t, d = z.shape
    tr = _rows(t)
    has_res = dres is not None

    def body(*refs):
        if has_res:
            d_ref, z_ref, m_ref, r_ref, o_ref, s1_ref, s2_ref = refs
        else:
            d_ref, z_ref, m_ref, o_ref, s1_ref, s2_ref = refs
        i = pl.program_id(0)
        zf = z_ref[...]
        r = lax.rsqrt(jnp.mean(zf * zf, axis=-1, keepdims=True) + EPS)
        zh = zf * r
        df = d_ref[...].astype(F32)
        dzh = df * m_ref[...]
        dz = r * (dzh - zh * jnp.mean(dzh * zh, axis=-1, keepdims=True))
        if has_res:
            dz = dz + r_ref[...]
        o_ref[...] = dz.astype(o_ref.dtype)
        s1 = jnp.sum(df * zh, axis=0, keepdims=True)
        s2 = jnp.sum(df, axis=0, keepdims=True)

        @pl.when(i == 0)
        def _():
            s1_ref[...] = s1
            s2_ref[...] = s2

        @pl.when(i > 0)
        def _():
            s1_ref[...] += s1
            s2_ref[...] += s2

    row = pl.BlockSpec((tr, d), lambda i: (i, 0))
    col = pl.BlockSpec((1, d), lambda i: (0, 0))
    ins = [d_out, z, mcol] + ([dres] if has_res else [])
    return pl.pallas_call(
        body, name=name, grid=(t // tr,), in_specs=[row, row, col] + ([row] if has_res else []),
        out_specs=[row, col, col],
        out_shape=[jax.ShapeDtypeStruct((t, d), out_dtype), jax.ShapeDtypeStruct((1, d), F32),
                   jax.ShapeDtypeStruct((1, d), F32)],
        compiler_params=_cparams(("arbitrary",)),
    )(*ins)


FFN_IN_TILE = (512, 1408)


def _ffn_in_swiglu(h, w_in, name):
    t, d = h.shape
    f = w_in.shape[1] // 2
    tm, tn = _tile(t, FFN_IN_TILE[0]), _tile(f, FFN_IN_TILE[1])
    nj = f // tn

    def body(h_ref, wa_ref, wb_ref, a_ref, b_ref, s_ref):
        hv = h_ref[...].astype(BF16)
        a = _dot(hv, wa_ref[...].astype(BF16))
        b = _dot(hv, wb_ref[...].astype(BF16))
        a_ref[...] = a.astype(BF16)
        b_ref[...] = b.astype(BF16)
        s_ref[...] = (_silu(a) * b).astype(BF16)

    out = pl.BlockSpec((tm, tn), lambda j, i: (i, j))
    return pl.pallas_call(
        body, name=name, grid=(nj, t // tm),
        in_specs=[pl.BlockSpec((tm, d), lambda j, i: (i, 0)), pl.BlockSpec((d, tn), lambda j, i: (0, j)),
                  pl.BlockSpec((d, tn), lambda j, i: (0, j + nj))],
        out_specs=[out, out, out], out_shape=[jax.ShapeDtypeStruct((t, f), BF16)] * 3,
        compiler_params=_cparams(("parallel", "parallel")),
    )(h, w_in, w_in)


def _swiglu_bwd(ua, ub, ds, name):
    t, f = ua.shape
    tr = _rows(t)

    def body(a_ref, b_ref, ds_ref, du_ref):
        a = a_ref[...].astype(F32)
        b = b_ref[...].astype(F32)
        g = ds_ref[...].astype(F32)
        du_ref[:, :f] = (g * b * _dsilu(a)).astype(du_ref.dtype)
        du_ref[:, f:] = (g * _silu(a)).astype(du_ref.dtype)

    half = pl.BlockSpec((tr, f), lambda i: (i, 0))
    return pl.pallas_call(
        body, name=name, grid=(t // tr,), in_specs=[half, half, half],
        out_specs=pl.BlockSpec((tr, 2 * f), lambda i: (i, 0)), out_shape=jax.ShapeDtypeStruct((t, 2 * f), BF16),
        compiler_params=_cparams(("parallel",)),
    )(ua, ub, ds)


def _loss_head(y, target, name):
    t, d = y.shape
    tr = _rows(t)

    def body(y_ref, t_ref, dy_ref, sq_ref):
        i = pl.program_id(0)
        e = y_ref[...] - t_ref[...]
        dy_ref[...] = e * (1.0 / d)
        s = jnp.sum(e * e, axis=0, keepdims=True)

        @pl.when(i == 0)
        def _():
            sq_ref[...] = s

        @pl.when(i > 0)
        def _():
            sq_ref[...] += s

    row = pl.BlockSpec((tr, d), lambda i: (i, 0))
    col = pl.BlockSpec((1, d), lambda i: (0, 0))
    return pl.pallas_call(
        body, name=name, grid=(t // tr,), in_specs=[row, row], out_specs=[row, col],
        out_shape=[jax.ShapeDtypeStruct((t, d), F32), jax.ShapeDtypeStruct((1, d), F32)],
        compiler_params=_cparams(("arbitrary",)),
    )(y, target)


def _hgrn_consts():
    c = A_CHUNK
    shift = SUB.bit_length() - 1
    r = lax.broadcasted_iota(jnp.int32, (c, c), 0)
    s = lax.broadcasted_iota(jnp.int32, (c, c), 1)
    sub_r = lax.shift_right_logical(r, shift)
    incl = s <= r
    masks = [jnp.logical_and(sub_r == i, incl) for i in range(c // SUB)]
    rev_incl = jnp.where(s >= r, 1.0, 0.0).astype(BF16)
    r2 = lax.broadcasted_iota(jnp.int32, (2 * c + 8, c), 0)
    s2 = lax.broadcasted_iota(jnp.int32, (2 * c + 8, c), 1)
    sub_start = lax.shift_left(lax.shift_right_logical(r2 - c, shift), shift)
    running = jnp.where(s2 <= r2, 1.0, 0.0)
    before = jnp.where(s2 < sub_start, 1.0, 0.0)
    stack = jnp.where(r2 < c, running, jnp.where(r2 < 2 * c, before, 1.0)).astype(BF16)
    return stack, masks, incl, rev_incl


def _hgrn_gates(q_raw, f_raw, lbv, stack):
    sg = jax.nn.sigmoid(f_raw)
    sgn = jax.nn.sigmoid(-f_raw)
    f = lbv + (1.0 - lbv) * sg
    logf = jnp.log(jnp.maximum(f, TINY))
    return dict(sg=sg, sgn=sgn, f=f, k=(1.0 - lbv) * sgn, q=_silu(q_raw), bb=_ones_left(stack, logf))


def _hgrn_chunk(q_raw, f_raw, lbv, stack):
    return _hgrn_decays(_hgrn_gates(q_raw, f_raw, lbv, stack))


def _hgrn_decays(gates):
    c = A_CHUNK
    sg, sgn, f, k, q, bb = (gates[n] for n in ("sg", "sgn", "f", "k", "q", "bb"))
    b = bb[:c]
    bsrow = bb[c:2 * c]
    b_end = bb[2 * c:2 * c + 1]
    e_sub = jnp.exp(b - bsrow)
    e_b = jnp.exp(b)
    e_end = jnp.exp(b_end - b)
    qs = q * e_sub
    q_in = q * e_b
    kend = k * e_end
    kfac = [jnp.exp(jnp.minimum(bsrow[i * SUB:i * SUB + 1] - b, EXP_CLAMP)) for i in range(c // SUB)]
    return dict(sg=sg, sgn=sgn, f=f, k=k, q=q, b=b, b_end=b_end, e_sub=e_sub, e_b=e_b, e_end=e_end,
                qs=qs, q_in=q_in, kend=kend, kfac=kfac)


def _hgrn_scores(ch, masks):
    qs_b = ch["qs"].astype(BF16)
    a = None
    for i, mk in enumerate(masks):
        ki = (ch["k"] * ch["kfac"][i]).astype(BF16)
        part = jnp.where(mk, _dot_nt(qs_b, ki), 0.0)
        a = part if a is None else a + part
    return a


def _hgrn_fwd(p, lb, hn2, name):
    t = p.shape[0]
    tb = _rows(t)
    nt = t // tb
    nc = tb // A_CHUNK
    c = A_CHUNK

    def body(q_ref, f_ref, i_ref, g_ref, lb_ref, hn_ref, y_ref, o_ref, st_ref, s_scr):
        j = pl.program_id(1)

        @pl.when(j == 0)
        def _():
            s_scr[...] = jnp.zeros_like(s_scr)

        stack, masks, _, _ = _hgrn_consts()
        units = [(ci, hh) for ci in range(nc) for hh in range(2)]
        lsl = [slice(A_KDIM * hh, A_KDIM * (hh + 1)) for hh in range(2)]
        hsl = [slice(A_VDIM * hh, A_VDIM * (hh + 1)) for hh in range(2)]
        rows = [pl.ds(ci * c, c) for ci in range(nc)]
        gates = {u: _hgrn_gates(q_ref[rows[u[0]], lsl[u[1]]], f_ref[rows[u[0]], lsl[u[1]]], lb_ref[:, lsl[u[1]]], stack)
                 for u in units}
        ch = {u: _hgrn_decays(gates[u]) for u in units}
        v = {u: i_ref[rows[u[0]], hsl[u[1]]].astype(BF16) for u in units}
        a = {u: _hgrn_scores(ch[u], masks).astype(BF16) for u in units}
        grow = {u: _dot_tn(v[u], ch[u]["kend"].astype(BF16)) for u in units}
        states = [s_scr[0], s_scr[1]]
        entering = {}
        for ci, hh in units:
            entering[ci, hh] = states[hh]
            st_ref[hh, ci] = states[hh]
            states[hh] = states[hh] * jnp.exp(ch[ci, hh]["b_end"]) + grow[ci, hh]
        s_scr[0] = states[0]
        s_scr[1] = states[1]
        for u in units:
            o_ref[rows[u[0]], hsl[u[1]]] = (_dot_nt(ch[u]["q_in"].astype(BF16), entering[u].astype(BF16))
                                            + _dot(a[u], v[u]))
        for hh in range(2):
            hsl = slice(A_VDIM * hh, A_VDIM * (hh + 1))
            o = o_ref[:, hsl]
            r = lax.rsqrt(jnp.mean(o * o, axis=-1, keepdims=True) + EPS)
            y_ref[:, hsl] = (o * r * hn_ref[:, hsl] * _silu(g_ref[:, hsl])).astype(y_ref.dtype)

    w2 = 2 * A_KDIM
    return pl.pallas_call(
        body, name=name, grid=(A_HEADS // 2, nt),
        in_specs=[pl.BlockSpec((tb, w2), lambda h, j: (j, COL_AQ // w2 + h)),
                  pl.BlockSpec((tb, w2), lambda h, j: (j, COL_AF // w2 + h)),
                  pl.BlockSpec((tb, LANES), lambda h, j: (j, COL_AI // LANES + h)),
                  pl.BlockSpec((tb, LANES), lambda h, j: (j, COL_AG // LANES + h)),
                  pl.BlockSpec((1, w2), lambda h, j: (0, h)),
                  pl.BlockSpec((1, LANES), lambda h, j: (0, 0))],
        out_specs=[pl.BlockSpec((tb, LANES), lambda h, j: (j, h)),
                   pl.BlockSpec((tb, LANES), lambda h, j: (j, h)),
                   pl.BlockSpec((2, nc, A_VDIM, A_KDIM), lambda h, j: (h, j, 0, 0))],
        out_shape=[jax.ShapeDtypeStruct((t, A_V), BF16), jax.ShapeDtypeStruct((t, A_V), F32),
                   jax.ShapeDtypeStruct((A_HEADS, t // c, A_VDIM, A_KDIM), F32)],
        scratch_shapes=[pltpu.VMEM((2, A_VDIM, A_KDIM), F32)],
        compiler_params=_cparams(("parallel", "arbitrary")),
    )(p, p, p, p, lb, hn2)


def _hgrn_bwd(p, lb, hn2, o_raw, states, dya, name):
    t = p.shape[0]
    tb = _rows(t)
    nt = t // tb
    nc = tb // A_CHUNK
    c = A_CHUNK

    def body(q_ref, f_ref, i_ref, g_ref, lb_ref, hn_ref, o_ref, st_ref, dy_ref,
             dq_ref, df_ref, di_ref, dg_ref, dlb_ref, dhn_ref, ds_scr, do_scr):
        j = pl.program_id(1)

        @pl.when(j == 0)
        def _():
            ds_scr[...] = jnp.zeros_like(ds_scr)
            dlb_ref[...] = jnp.zeros_like(dlb_ref)
            dhn_ref[...] = jnp.zeros_like(dhn_ref)

        stack, masks, incl, rev_incl = _hgrn_consts()
        for hh in range(2):
            hsl = slice(A_VDIM * hh, A_VDIM * (hh + 1))
            o = o_ref[:, hsl]
            g = g_ref[:, hsl]
            dy = dy_ref[:, hsl].astype(F32)
            hn = hn_ref[:, hsl]
            r = lax.rsqrt(jnp.mean(o * o, axis=-1, keepdims=True) + EPS)
            oh = o * r
            sgate = _silu(g)
            dg_ref[:, hsl] = dy * oh * hn * _dsilu(g)
            dhn_ref[0, :, hsl] += jnp.sum(dy * oh * sgate, axis=0, keepdims=True)
            doh = dy * hn * sgate
            do_scr[:, hsl] = r * (doh - oh * jnp.mean(doh * oh, axis=-1, keepdims=True))

        units = [(ci, hh) for ci in reversed(range(nc)) for hh in range(2)]
        lsl = [slice(A_KDIM * hh, A_KDIM * (hh + 1)) for hh in range(2)]
        hsl = [slice(A_VDIM * hh, A_VDIM * (hh + 1)) for hh in range(2)]
        rows = [pl.ds(ci * c, c) for ci in range(nc)]
        q_raw = {u: q_ref[rows[u[0]], lsl[u[1]]] for u in units}
        gates = {u: _hgrn_gates(q_raw[u], f_ref[rows[u[0]], lsl[u[1]]], lb_ref[:, lsl[u[1]]], stack) for u in units}
        ch = {u: _hgrn_decays(gates[u]) for u in units}
        v = {u: i_ref[rows[u[0]], hsl[u[1]]].astype(BF16) for u in units}
        do_b = {u: do_scr[rows[u[0]], hsl[u[1]]].astype(BF16) for u in units}
        st = {u: st_ref[u[1], u[0]] for u in units}
        qs_b = {u: ch[u]["qs"].astype(BF16) for u in units}
        a_b = {u: _hgrn_scores(ch[u], masks).astype(BF16) for u in units}
        da = {u: jnp.where(incl, _dot_nt(do_b[u], v[u]), 0.0) for u in units}
        dq_x = {u: _dot(do_b[u], st[u].astype(BF16)) for u in units}
        grow = {u: _dot_tn(do_b[u], ch[u]["q_in"].astype(BF16)) for u in units}
        dstates = [ds_scr[0], ds_scr[1]]
        leaving = {}
        for ci, hh in units:
            leaving[ci, hh] = dstates[hh]
            dstates[hh] = dstates[hh] * jnp.exp(ch[ci, hh]["b_end"]) + grow[ci, hh]
        for hh in range(2):
            ds_scr[hh] = dstates[hh]
        dst_b = {u: leaving[u].astype(BF16) for u in units}
        dv = {u: _dot_tn(a_b[u], do_b[u]) + _dot_nt(ch[u]["kend"].astype(BF16), dst_b[u]) for u in units}
        dk_x = {u: _dot(v[u], dst_b[u]) for u in units}
        dlb_acc = [jnp.zeros((1, A_KDIM), F32), jnp.zeros((1, A_KDIM), F32)]
        for u in units:
            ci, hh = u
            cu = ch[u]
            lbv = lb_ref[:, lsl[hh]]
            dq_i = None
            dk_i = None
            kdk_i = None
            for i, mk in enumerate(masks):
                dam = jnp.where(mk, da[u], 0.0).astype(BF16)
                ki = (cu["k"] * cu["kfac"][i]).astype(BF16)
                pq = _dot(dam, ki)
                pk = _dot_tn(dam, qs_b[u])
                dq_i = pq if dq_i is None else dq_i + pq
                dk_i = cu["kfac"][i] * pk if dk_i is None else dk_i + cu["kfac"][i] * pk
                kdk_i = ki.astype(F32) * pk if kdk_i is None else kdk_i + ki.astype(F32) * pk
            dq = cu["e_sub"] * dq_i + cu["e_b"] * dq_x[u]
            dk = dk_i + cu["e_end"] * dk_x[u]
            kx = cu["kend"] * dk_x[u]
            db = (qs_b[u].astype(F32) * dq_i + cu["q_in"] * dq_x[u]) - (kdk_i + kx)
            later = (jnp.exp(cu["b_end"]) * jnp.sum(leaving[u] * st[u], axis=0, keepdims=True)
                     + jnp.sum(kx, axis=0, keepdims=True))
            dlogf = later + _ones_left(rev_incl, db)
            dfv = jnp.where(cu["f"] > TINY, dlogf / cu["f"], 0.0)
            dq_ref[rows[ci], lsl[hh]] = dq * _dsilu(q_raw[u])
            df_ref[rows[ci], lsl[hh]] = (1.0 - lbv) * cu["sg"] * cu["sgn"] * (dfv - dk)
            dlb_acc[hh] = dlb_acc[hh] + jnp.sum(dfv * (1.0 - cu["sg"]) - dk * cu["sgn"], axis=0, keepdims=True)
            di_ref[rows[ci], hsl[hh]] = dv[u]
        for hh in range(2):
            dlb_ref[:, A_KDIM * hh:A_KDIM * (hh + 1)] += dlb_acc[hh]

    w2 = 2 * A_KDIM
    rev = lambda j: nt - 1 - j
    return pl.pallas_call(
        body, name=name, grid=(A_HEADS // 2, nt),
        in_specs=[pl.BlockSpec((tb, w2), lambda h, j: (rev(j), COL_AQ // w2 + h)),
                  pl.BlockSpec((tb, w2), lambda h, j: (rev(j), COL_AF // w2 + h)),
                  pl.BlockSpec((tb, LANES), lambda h, j: (rev(j), COL_AI // LANES + h)),
                  pl.BlockSpec((tb, LANES), lambda h, j: (rev(j), COL_AG // LANES + h)),
                  pl.BlockSpec((1, w2), lambda h, j: (0, h)),
                  pl.BlockSpec((1, LANES), lambda h, j: (0, 0)),
                  pl.BlockSpec((tb, LANES), lambda h, j: (rev(j), h)),
                  pl.BlockSpec((2, nc, A_VDIM, A_KDIM), lambda h, j: (h, rev(j), 0, 0)),
                  pl.BlockSpec((tb, LANES), lambda h, j: (rev(j), h))],
        out_specs=[pl.BlockSpec((tb, w2), lambda h, j: (rev(j), h)),
                   pl.BlockSpec((tb, w2), lambda h, j: (rev(j), h)),
                   pl.BlockSpec((tb, LANES), lambda h, j: (rev(j), h)),
                   pl.BlockSpec((tb, LANES), lambda h, j: (rev(j), h)),
                   pl.BlockSpec((1, w2), lambda h, j: (0, h)),
                   pl.BlockSpec((1, 1, LANES), lambda h, j: (h, 0, 0))],
        out_shape=[jax.ShapeDtypeStruct((t, A_QK), F32), jax.ShapeDtypeStruct((t, A_QK), F32),
                   jax.ShapeDtypeStruct((t, A_V), F32), jax.ShapeDtypeStruct((t, A_V), F32),
                   jax.ShapeDtypeStruct((1, A_QK), F32), jax.ShapeDtypeStruct((A_HEADS // 2, 1, LANES), F32)],
        scratch_shapes=[pltpu.VMEM((2, A_VDIM, A_KDIM), F32), pltpu.VMEM((tb, LANES), F32)],
        compiler_params=_cparams(("parallel", "arbitrary")),
    )(p, p, p, p, lb, hn2, o_raw, states, dya)


BLK = 128
SCALE = HDIM ** -0.5
SB_CHUNK = 4


def _softplus(z):
    return jnp.maximum(z, 0.0) + jnp.log(1.0 + jnp.exp(-jnp.abs(z)))


def _sb_sum_matrix(keep, with_total=False):
    width = 2 * BLK if with_total else BLK
    sp = lax.broadcasted_iota(jnp.int32, (BLK, width), 0)
    s = lax.broadcasted_iota(jnp.int32, (BLK, width), 1)
    return jnp.where(jnp.logical_or(s >= BLK, keep(sp, s)), 1.0, 0.0).astype(BF16)


def _lanes(col):
    return jnp.broadcast_to(col, (BLK, BLK))


def _sb_fwd(p, kv, name, gather=None):
    t = p.shape[0]
    nq = t // BLK
    nh = B_HEADS // 2
    cw = SB_CHUNK * BLK
    fused = gather is not None
    n = len(gather) if fused else 0

    def body(*refs):
        q_ref, kb, vb = refs[:3]
        o_ref, tot_ref = refs[3 + n:5 + n]
        zbuf, stage, sbuf, abuf = refs[5 + 2 * n:9 + 2 * n]
        hp = pl.program_id(0)
        qi = pl.program_id(1)
        if fused:
            g = _Many(_Gather, refs[3:3 + n], refs[5 + n:5 + 2 * n], *refs[9 + 2 * n:])
            pl.when(jnp.logical_and(hp == 0, qi == 0))(g.start)
            pl.when(jnp.logical_and(hp == nh - 1, qi == 0))(g.forward)

        @pl.when(qi == 0)
        def _():
            abuf[...] = jnp.zeros_like(abuf)

        row = lax.broadcasted_iota(jnp.int32, (BLK, BLK), 0)
        col = lax.broadcasted_iota(jnp.int32, (BLK, BLK), 1)
        sums = _sb_sum_matrix(lambda sp, s: sp >= s, True)
        hsl = [slice(HDIM * h, HDIM * (h + 1)) for h in range(2)]
        nchunk = qi // SB_CHUNK + 1
        for h in range(2):
            zbuf[h] = _dot_nt((q_ref[:, hsl[h]] * SCALE).astype(BF16), kb[:, hsl[h]])

        col_minus_row = col - row

        def causal(j):
            return col_minus_row < (qi - j) * BLK

        def l_pass(c, carry):
            for b in range(SB_CHUNK):
                j = c * SB_CHUNK + b
                off = pl.multiple_of(j * BLK, BLK)
                mask = causal(j)
                for h in range(2):
                    lm = jnp.where(mask, -_softplus(zbuf[h, :, pl.ds(off, BLK)]), 0.0)
                    stage[h, pl.ds(off, BLK), :] = lm.astype(BF16)
            return carry

        lax.fori_loop(0, nchunk, l_pass, 0)

        def sum_pass(c, carry):
            rows = pl.ds(pl.multiple_of(c * cw, cw), cw)
            for h in range(2):
                sbuf[h, rows, :] = _dot(stage[h, rows, :], sums)
            return carry

        lax.fori_loop(0, nchunk, sum_pass, 0)

        def a_pass(it, carry):
            c = nchunk - 1 - it
            runs = list(carry)
            for b in reversed(range(SB_CHUNK)):
                j = c * SB_CHUNK + b
                off = pl.multiple_of(j * BLK, BLK)
                mask = causal(j)
                for h in range(2):
                    s = sbuf[h, pl.ds(off, BLK), :BLK]
                    a = jnp.where(mask, jnp.exp(zbuf[h, :, pl.ds(off, BLK)] + s + runs[h]), 0.0)
                    abuf[h, :, pl.ds(off, BLK)] = a.astype(BF16)
                    runs[h] = runs[h] + sbuf[h, pl.ds(off, BLK), BLK:]
            return tuple(runs)

        zero = jnp.zeros((BLK, BLK), F32)
        runs = lax.fori_loop(0, nchunk, a_pass, (zero, zero))
        for h in range(2):
            tot_ref[:, hsl[h]] = runs[h][:, :HDIM]
            o_ref[:, hsl[h]] = _dot(abuf[h], vb[:, hsl[h]])
        if fused:
            pl.when(jnp.logical_and(hp == nh - 1, qi == nq - 1))(g.finish)

    out_blk = pl.BlockSpec((BLK, LANES), lambda h, i: (i, h))
    hbm = pl.BlockSpec(memory_space=pl.ANY)
    in_specs = [pl.BlockSpec((BLK, LANES), lambda h, i: (i, COL_BQ // LANES + h)),
                pl.BlockSpec((t, LANES), lambda h, i: (0, h)),
                pl.BlockSpec((t, LANES), lambda h, i: (0, B_W // LANES + h))]
    out_shape = [jax.ShapeDtypeStruct((t, B_W), F32)] * 2
    scratch = [pltpu.VMEM((2, BLK, t), F32), pltpu.VMEM((2, t, BLK), BF16),
               pltpu.VMEM((2, t, 2 * BLK), F32), pltpu.VMEM((2, BLK, t), BF16)]
    if fused:
        out_shape = out_shape + _gathered_shapes(gather)
    return pl.pallas_call(
        body, name=name, grid=(nh, nq),
        in_specs=in_specs + [hbm] * n,
        out_specs=[out_blk, out_blk] + [hbm] * n,
        out_shape=out_shape,
        scratch_shapes=scratch + (_comm_sems(n) if fused else []),
        compiler_params=_cparams(("arbitrary", "arbitrary")),
    )(p, kv, kv, *(gather if fused else []))


def _sb_bwd(p, kv, tot, do, name, exchange=None):
    t = p.shape[0]
    nq = t // BLK
    nh = B_HEADS // 2
    cw = SB_CHUNK * BLK
    fused = exchange is not None
    n = len(exchange) if fused else 0

    def body(*refs):
        q_ref, kb, vb, tot_ref, do_ref = refs[:5]
        dq_ref, dk_ref, dv_ref = refs[5 + n:8 + n]
        zbuf, dabuf, lbuf, stage, sbuf, abuf, dzbuf, dkt, dvt = refs[8 + 2 * n:17 + 2 * n]
        hp = pl.program_id(0)
        qi = pl.program_id(1)
        if fused:
            ex = _Many(_Exchange, refs[5:5 + n], refs[8 + n:8 + 2 * n], *refs[17 + 2 * n:])
            pl.when(jnp.logical_and(hp == 0, qi == 0))(ex.start)

        @pl.when(qi == 0)
        def _():
            dkt[...] = jnp.zeros_like(dkt)
            dvt[...] = jnp.zeros_like(dvt)
            dzbuf[...] = jnp.zeros_like(dzbuf)
            abuf[...] = jnp.zeros_like(abuf)

        row = lax.broadcasted_iota(jnp.int32, (BLK, BLK), 0)
        col = lax.broadcasted_iota(jnp.int32, (BLK, BLK), 1)
        sums = _sb_sum_matrix(lambda sp, s: sp <= s)
        hsl = [slice(HDIM * h, HDIM * (h + 1)) for h in range(2)]
        dob = [do_ref[:, hsl[h]].astype(BF16) for h in range(2)]
        total =[jnp.concatenate([tot_ref[:, hsl[h]], tot_ref[:, hsl[h]]], axis=1) for h in range(2)]
        nchunk = qi // SB_CHUNK + 1
        for h in range(2):
            zbuf[h] = _dot_nt((q_ref[:, hsl[h]] * SCALE).astype(BF16), kb[:, hsl[h]])
            dabuf[h] = _dot_nt(dob[h], vb[:, hsl[h]])

        col_minus_row = col - row

        def causal(j):
            return col_minus_row < (qi - j) * BLK

        def blocks(c):
            for b in range(SB_CHUNK):
                j = c * SB_CHUNK + b
                yield j, pl.ds(pl.multiple_of(j * BLK, BLK), BLK)

        def l_pass(c, carry):
            for j, blk_ in blocks(c):
                mask = causal(j)
                for h in range(2):
                    lm = jnp.where(mask, -_softplus(zbuf[h, :, blk_]), 0.0)
                    lbuf[h, :, blk_] = lm
                    stage[h, blk_, :] = lm.astype(BF16)
            return carry

        lax.fori_loop(0, nchunk, l_pass, 0)

        def sum_pass():
            def run_(c, carry):
                rows = pl.ds(pl.multiple_of(c * cw, cw), cw)
                for h in range(2):
                    sbuf[h, rows, :] = _dot(stage[h, rows, :], sums)
                return carry
            lax.fori_loop(0, nchunk, run_, 0)

        sum_pass()

        def g_pass(c, carry):
            runs = list(carry)
            for j, blk_ in blocks(c):
                mask = causal(j)
                for h in range(2):
                    upto = sbuf[h, blk_, :]
                    log_a = zbuf[h, :, blk_] + lbuf[h, :, blk_] + (total[h] - runs[h] - upto)
                    a = jnp.where(mask, jnp.exp(log_a), 0.0)
                    abuf[h, :, blk_] = a.astype(BF16)
                    g = a * dabuf[h, :, blk_]
                    dabuf[h, :, blk_] = g
                    stage[h, blk_, :] = g.astype(BF16)
                    runs[h] = runs[h] + _lanes(upto[:, BLK - 1:BLK])
            return tuple(runs)

        zero = jnp.zeros((BLK, BLK), F32)
        lax.fori_loop(0, nchunk, g_pass, (zero, zero))
        sum_pass()

        def dz_pass(c, carry):
            runs = list(carry)
            for j, blk_ in blocks(c):
                mask = causal(j)
                for h in range(2):
                    lm = lbuf[h, :, blk_]
                    g = dabuf[h, :, blk_]
                    upto = sbuf[h, blk_, :]
                    before = runs[h] + upto - g
                    dz = jnp.where(mask, g * jnp.exp(lm) - jnp.exp(zbuf[h, :, blk_] + lm) * before, 0.0)
                    dzbuf[h, :, blk_] = (dz * SCALE).astype(BF16)
                    runs[h] = runs[h] + _lanes(upto[:, BLK - 1:BLK])
            return tuple(runs)

        lax.fori_loop(0, nchunk, dz_pass, (zero, zero))
        for h in range(2):
            dq_ref[:, hsl[h]] = _dot(dzbuf[h], kb[:, hsl[h]])
        q_t = q_ref[...].T.astype(BF16)
        do_t = do_ref[...].T.astype(BF16)
        for h in range(2):
            dkt[hsl[h], :] += _dot(q_t[hsl[h], :], dzbuf[h])
            dvt[hsl[h], :] += _dot(do_t[hsl[h], :], abuf[h])

        @pl.when(qi == nq - 1)
        def _():
            dk_ref[...] = dkt[...].T
            dv_ref[...] = dvt[...].T

        if fused:
            pl.when(jnp.logical_and(hp == nh - 1, qi == nq - 1))(ex.finish)

    blk = lambda h, i: (i, h)
    whole = lambda h, i: (0, h)
    hbm = pl.BlockSpec(memory_space=pl.ANY)
    in_specs = [pl.BlockSpec((BLK, LANES), lambda h, i: (i, COL_BQ // LANES + h)),
                pl.BlockSpec((t, LANES), lambda h, i: (0, h)),
                pl.BlockSpec((t, LANES), lambda h, i: (0, B_W // LANES + h)),
                pl.BlockSpec((BLK, LANES), blk), pl.BlockSpec((BLK, LANES), blk)]
    out_specs = [pl.BlockSpec((BLK, LANES), blk), pl.BlockSpec((t, LANES), whole), pl.BlockSpec((t, LANES), whole)]
    out_shape = [jax.ShapeDtypeStruct((t, B_W), F32)] * 3
    scratch = [pltpu.VMEM((2, BLK, t), F32), pltpu.VMEM((2, BLK, t), F32), pltpu.VMEM((2, BLK, t), F32),
               pltpu.VMEM((2, t, BLK), BF16), pltpu.VMEM((2, t, BLK), F32), pltpu.VMEM((2, BLK, t), BF16),
               pltpu.VMEM((2, BLK, t), BF16), pltpu.VMEM((LANES, t), F32), pltpu.VMEM((LANES, t), F32)]
    if fused:
        out_shape = out_shape + [jax.ShapeDtypeStruct(e.shape, e.dtype) for e in exchange]
    return pl.pallas_call(
        body, name=name, grid=(nh, nq),
        in_specs=in_specs + [hbm] * n,
        out_specs=out_specs + [hbm] * n,
        out_shape=out_shape,
        scratch_shapes=scratch + (_comm_sems(n) if fused else []),
        compiler_params=_cparams(("arbitrary", "arbitrary")),
    )(p, kv, kv, tot, do, *(exchange if fused else []))


def _alibi_slopes(n):
    def pow2(m):
        start = 2.0 ** (-8.0 / m)
        return [start ** (i + 1) for i in range(m)]
    if math.log2(n).is_integer():
        s = pow2(n)
    else:
        c = 2 ** int(math.floor(math.log2(n)))
        s = pow2(c) + pow2(2 * c)[0::2][: n - c]
    return sorted(s, reverse=True)


def _dil_scores(qh, kh, sl, prev, exists=None):
    row = lax.broadcasted_iota(jnp.int32, (BLK, BLK), 0)
    col = lax.broadcasted_iota(jnp.int32, (BLK, BLK), 1)
    dist = row - col + (BLK if prev else 0)
    if prev:
        valid = (col - row) >= jnp.where(exists, 0, 2 * BLK)
    else:
        valid = col <= row
    s = _dot_nt(qh, kh) - sl * dist.astype(F32)
    return s, valid


DIL_UNITS = 2


def _dil_plan(r):
    per_trip = min(r, DIL_UNITS)
    return per_trip, DIL_UNITS // per_trip


def _dil_rows(b, rho, r):
    return pl.ds(b * BLK * r + rho, BLK, stride=r) if r > 1 else pl.ds(b * BLK, BLK)


def _dil_fwd(p, gi, name):
    t = p.shape[0]
    _, r = C_GROUPS[gi]
    per_trip, nsub = _dil_plan(r)
    sbr = BLK * r * nsub
    nsb = t // sbr
    slope_cols = _slope_cols(gi)

    def body(q_ref, kc_ref, kp_ref, vc_ref, vp_ref, sl_ref, o_ref, lse_ref):
        i = pl.program_id(1)

        hsl = [slice(HDIM * h, HDIM * (h + 1)) for h in range(2)]
        sl = [sl_ref[:, HDIM * h:HDIM * h + 1] for h in range(2)]

        def residues(it, carry):
            pairs = [(b, dr) for b in range(nsub) for dr in range(per_trip)]
            units = [(pr, h) for pr in pairs for h in range(2)]
            rows = {(b, dr): _dil_rows(b, it * per_trip + dr, r) for b, dr in pairs}
            blocks, prev_exists = {}, {}
            for b, dr in pairs:
                rw = rows[b, dr]
                if b == 0:
                    before = _dil_rows(nsub - 1, it * per_trip + dr, r)
                    kp, vp, prev_exists[b, dr] = kp_ref[before, :], vp_ref[before, :], i > 0
                else:
                    before = rows[b - 1, dr]
                    kp, vp, prev_exists[b, dr] = kc_ref[before, :], vc_ref[before, :], True
                blocks[b, dr] = [q_ref[rw, :], kc_ref[rw, :], kp, vc_ref[rw, :], vp]
            qh = {u: (blocks[u[0]][0][:, hsl[u[1]]] * SCALE).astype(BF16) for u in units}
            sc = {u: _dil_scores(qh[u], blocks[u[0]][1][:, hsl[u[1]]].astype(BF16), sl[u[1]], False) for u in units}
            sp = {u: _dil_scores(qh[u], blocks[u[0]][2][:, hsl[u[1]]].astype(BF16), sl[u[1]], True, prev_exists[u[0]])
                  for u in units}
            pc, pp, den, lse = {}, {}, {}, {}
            for u in units:
                s_c = jnp.where(sc[u][1], sc[u][0], NEG_BIG)
                s_p = jnp.where(sp[u][1], sp[u][0], NEG_BIG)
                m = jnp.maximum(jnp.max(s_c, axis=1, keepdims=True), jnp.max(s_p, axis=1, keepdims=True))
                pc[u] = jnp.exp(s_c - m)
                pp[u] = jnp.exp(s_p - m)
                den[u] = jnp.sum(pc[u], axis=1, keepdims=True) + jnp.sum(pp[u], axis=1, keepdims=True)
                lse[u] = jnp.broadcast_to(m + jnp.log(den[u]), (BLK, HDIM))
            o = {u: (_dot(pc[u].astype(BF16), blocks[u[0]][3][:, hsl[u[1]]].astype(BF16))
                     + _dot(pp[u].astype(BF16), blocks[u[0]][4][:, hsl[u[1]]].astype(BF16))) / den[u] for u in units}
            for pr in pairs:
                o_ref[rows[pr], :] = jnp.concatenate([o[pr, 0], o[pr, 1]], axis=1)
                lse_ref[rows[pr], :] = jnp.concatenate([lse[pr, 0], lse[pr, 1]], axis=1)
            return carry

        lax.fori_loop(0, r // per_trip, residues, 0)

    def at(col0, pick):
        return pl.BlockSpec((sbr, LANES), lambda c, i: (pick(i), col0 // LANES + c))

    cur = lambda i: i
    prv = lambda i: jnp.maximum(i - 1, 0)
    cq, ck, cv = COL_CQ + gi * C_OUT, COL_CK + gi * C_OUT, COL_CV + gi * C_OUT
    out = pl.BlockSpec((sbr, LANES), lambda c, i: (i, c))
    return pl.pallas_call(
        body, name=name, grid=(C_OUT // LANES, nsb),
        in_specs=[at(cq, cur), at(ck, cur), at(ck, prv), at(cv, cur), at(cv, prv),
                  pl.BlockSpec((1, LANES), lambda c, i: (0, c))],
        out_specs=[out, out], out_shape=[jax.ShapeDtypeStruct((t, C_OUT), F32)] * 2,
        compiler_params=_cparams(("parallel", "parallel")),
    )(p, p, p, p, p, slope_cols)


def _dil_bwd(p, do, o, lse, gi, name):
    t = p.shape[0]
    _, r = C_GROUPS[gi]
    per_trip, nsub = _dil_plan(r)
    sbr = BLK * r * nsub
    nsb = t // sbr
    slope_cols = _slope_cols(gi)

    def body(q_ref, qn_ref, kc_ref, kp_ref, vc_ref, vp_ref, do_ref, don_ref, o_ref, on_ref, l_ref, ln_ref, sl_ref,
             dq_ref, dk_ref, dv_ref):
        i = pl.program_id(1)

        hsl = [slice(HDIM * h, HDIM * (h + 1)) for h in range(2)]
        sl = [sl_ref[:, HDIM * h:HDIM * h + 1] for h in range(2)]

        def residues(it, carry):
            pairs = [(b, dr) for b in range(nsub) for dr in range(per_trip)]
            units = [(pr, h) for pr in pairs for h in range(2)]
            rows = {(b, dr): _dil_rows(b, it * per_trip + dr, r) for b, dr in pairs}
            blocks, has_prev, has_next = {}, {}, {}
            for b, dr in pairs:
                rw = rows[b, dr]
                if b == 0:
                    before = _dil_rows(nsub - 1, it * per_trip + dr, r)
                    kp, vp, has_prev[b, dr] = kp_ref[before, :], vp_ref[before, :], i > 0
                else:
                    kp, vp, has_prev[b, dr] = kc_ref[rows[b - 1, dr], :], vc_ref[rows[b - 1, dr], :], True
                if b == nsub - 1:
                    after = _dil_rows(0, it * per_trip + dr, r)
                    nxt = [ref[after, :] for ref in (qn_ref, don_ref, on_ref, ln_ref)]
                    has_next[b, dr] = i < nsb - 1
                else:
                    nxt = [ref[rows[b + 1, dr], :] for ref in (q_ref, do_ref, o_ref, l_ref)]
                    has_next[b, dr] = True
                blocks[b, dr] = [q_ref[rw, :], nxt[0], kc_ref[rw, :], kp, vc_ref[rw, :], vp, do_ref[rw, :], nxt[1],
                                 o_ref[rw, :], nxt[2], l_ref[rw, :], nxt[3]]
            part = lambda u, k: blocks[u[0]][k][:, hsl[u[1]]]
            qb = {u: part(u, 0).astype(BF16) for u in units}
            qnb = {u: part(u, 1).astype(BF16) for u in units}
            qh = {u: (part(u, 0) * SCALE).astype(BF16) for u in units}
            qnh = {u: (part(u, 1) * SCALE).astype(BF16) for u in units}
            kc = {u: part(u, 2).astype(BF16) for u in units}
            kp = {u: part(u, 3).astype(BF16) for u in units}
            vc = {u: part(u, 4).astype(BF16) for u in units}
            vp = {u: part(u, 5).astype(BF16) for u in units}
            dob = {u: part(u, 6).astype(BF16) for u in units}
            donb = {u: part(u, 7).astype(BF16) for u in units}
            delta = {u: jnp.sum(part(u, 6) * part(u, 8), axis=1, keepdims=True) for u in units}
            deltan = {u: jnp.sum(part(u, 7) * part(u, 9), axis=1, keepdims=True) for u in units}
            lse_c = {u: part(u, 10)[:, :1] for u in units}
            lse_n = {u: part(u, 11)[:, :1] for u in units}
            s_cc = {u: _dil_scores(qh[u], kc[u], sl[u[1]], False) for u in units}
            s_cp = {u: _dil_scores(qh[u], kp[u], sl[u[1]], True, has_prev[u[0]]) for u in units}
            s_nc = {u: _dil_scores(qnh[u], kc[u], sl[u[1]], True, has_next[u[0]]) for u in units}
            da_cc = {u: _dot_nt(dob[u], vc[u]) for u in units}
            da_cp = {u: _dot_nt(dob[u], vp[u]) for u in units}
            da_nc = {u: _dot_nt(donb[u], vc[u]) for u in units}

            def prob(s_ok, lse_col):
                s, ok = s_ok
                return jnp.where(ok, jnp.exp(jnp.where(ok, s, NEG_BIG) - lse_col), 0.0)

            p_cc = {u: prob(s_cc[u], lse_c[u]) for u in units}
            p_cp = {u: prob(s_cp[u], lse_c[u]) for u in units}
            p_nc = {u: prob(s_nc[u], lse_n[u]) for u in units}
            ds_cc = {u: (p_cc[u] * (da_cc[u] - delta[u]) * SCALE).astype(BF16) for u in units}
            ds_cp = {u: (p_cp[u] * (da_cp[u] - delta[u]) * SCALE).astype(BF16) for u in units}
            ds_nc = {u: (p_nc[u] * (da_nc[u] - deltan[u]) * SCALE).astype(BF16) for u in units}
            dq = {u: _dot(ds_cc[u], kc[u]) + _dot(ds_cp[u], kp[u]) for u in units}
            dk = {u: _dot_tn(ds_cc[u], qb[u]) + _dot_tn(ds_nc[u], qnb[u]) for u in units}
            dv = {u: _dot_tn(p_cc[u].astype(BF16), dob[u]) + _dot_tn(p_nc[u].astype(BF16), donb[u]) for u in units}
            for pr in pairs:
                dq_ref[rows[pr], :] = jnp.concatenate([dq[pr, 0], dq[pr, 1]], axis=1)
                dk_ref[rows[pr], :] = jnp.concatenate([dk[pr, 0], dk[pr, 1]], axis=1)
                dv_ref[rows[pr], :] = jnp.concatenate([dv[pr, 0], dv[pr, 1]], axis=1)
            return carry

        lax.fori_loop(0, r // per_trip, residues, 0)

    def at(col0, pick):
        return pl.BlockSpec((sbr, LANES), lambda c, i: (pick(i), col0 // LANES + c))

    cur = lambda i: i
    prv = lambda i: jnp.maximum(i - 1, 0)
    nxt = lambda i: jnp.minimum(i + 1, nsb - 1)
    cq, ck, cv = COL_CQ + gi * C_OUT, COL_CK + gi * C_OUT, COL_CV + gi * C_OUT
    return pl.pallas_call(
        body, name=name, grid=(C_OUT // LANES, nsb),
        in_specs=[at(cq, cur), at(cq, nxt), at(ck, cur), at(ck, prv), at(cv, cur), at(cv, prv),
                  at(0, cur), at(0, nxt), at(0, cur), at(0, nxt), at(0, cur), at(0, nxt),
                  pl.BlockSpec((1, LANES), lambda c, i: (0, c))],
        out_specs=[at(0, cur)] * 3, out_shape=[jax.ShapeDtypeStruct((t, C_OUT), F32)] * 3,
        compiler_params=_cparams(("parallel", "parallel")),
    )(p, p, p, p, p, p, do, do, o, o, lse, lse, slope_cols)


def _dil_merge(os_, ls_, name):
    t, w = os_[0].shape
    tr = _rows(t)

    def body(o0, o1, o2, l0, l1, l2, y_ref, lse_ref):
        a, b, c = l0[...], l1[...], l2[...]
        m = jnp.maximum(jnp.maximum(a, b), c)
        ea, eb, ec = jnp.exp(a - m), jnp.exp(b - m), jnp.exp(c - m)
        den = ea + eb + ec
        y_ref[...] = (ea * o0[...] + eb * o1[...] + ec * o2[...]) / den
        lse_ref[...] = m + jnp.log(den)

    row = pl.BlockSpec((tr, w), lambda i: (i, 0))
    return pl.pallas_call(
        body, name=name, grid=(t // tr,), in_specs=[row] * 6, out_specs=[row, row],
        out_shape=[jax.ShapeDtypeStruct((t, w), F32)] * 2, compiler_params=_cparams(("parallel",)),
    )(*os_, *ls_)


def _gate_fwd(ys, gl, ws, name):
    t = gl.shape[0]
    d = gl.shape[1] // N_BRANCH
    tr = _rows(t)

    def body(ya, yb, yc, gl_ref, wa, wb, wc, m_ref):
        acc = None
        for i, (y, w) in enumerate(((ya, wa), (yb, wb), (yc, wc))):
            z = _dot(y[...].astype(BF16), w[...])
            term = jax.nn.sigmoid(gl_ref[:, i * d:(i + 1) * d]) * z
            acc = term if acc is None else acc + term
        m_ref[...] = acc.astype(m_ref.dtype)

    rows = [pl.BlockSpec((tr, y.shape[1]), lambda i: (i, 0)) for y in ys]
    wsp = [pl.BlockSpec(w.shape, lambda i: (0, 0)) for w in ws]
    return pl.pallas_call(
        body, name=name, grid=(t // tr,),
        in_specs=rows + [pl.BlockSpec((tr, N_BRANCH * d), lambda i: (i, 0))] + wsp,
        out_specs=pl.BlockSpec((tr, d), lambda i: (i, 0)), out_shape=jax.ShapeDtypeStruct((t, d), BF16),
        compiler_params=_cparams(("parallel",)),
    )(*ys, gl, *ws)


def _gate_bwd(dm, ys, gl, ws, name):
    t = gl.shape[0]
    d = gl.shape[1] // N_BRANCH
    tr = _rows(t)

    def body(dm_ref, ya, yb, yc, gl_ref, wa, wb, wc, dya, dyb, dyc, dgl_ref, dwa, dwb, dwc):
        step = pl.program_id(0)
        dmv = dm_ref[...].astype(F32)
        for i, (y, w, dy, dw) in enumerate(((ya, wa, dya, dwa), (yb, wb, dyb, dwb), (yc, wc, dyc, dwc))):
            yb16 = y[...].astype(BF16)
            z = _dot(yb16, w[...])
            sg = jax.nn.sigmoid(gl_ref[:, i * d:(i + 1) * d])
            dgl_ref[:, i * d:(i + 1) * d] = dmv * z * sg * (1.0 - sg)
            e = (dmv * sg).astype(BF16)
            dy[...] = _dot_nt(e, w[...])
            contrib = _dot_tn(yb16, e)

            @pl.when(step == 0)
            def _(dw=dw, contrib=contrib):
                dw[...] = contrib

            @pl.when(step > 0)
            def _(dw=dw, contrib=contrib):
                dw[...] += contrib

    rows = [pl.BlockSpec((tr, y.shape[1]), lambda i: (i, 0)) for y in ys]
    wsp = [pl.BlockSpec(w.shape, lambda i: (0, 0)) for w in ws]
    gsp = pl.BlockSpec((tr, N_BRANCH * d), lambda i: (i, 0))
    return pl.pallas_call(
        body, name=name, grid=(t // tr,),
        in_specs=[pl.BlockSpec((tr, d), lambda i: (i, 0))] + rows + [gsp] + wsp,
        out_specs=rows + [gsp] + wsp,
        out_shape=[jax.ShapeDtypeStruct(y.shape, F32) for y in ys] + [jax.ShapeDtypeStruct(gl.shape, F32)]
        + [jax.ShapeDtypeStruct(w.shape, F32) for w in ws],
        compiler_params=_cparams(("arbitrary",)),
    )(dm, *ys, gl, *ws)


def _adamw(w, m, v, gparts, name):
    r, c = w.shape
    n = gparts.shape[0]
    br = LANES if r % LANES == 0 else r
    c1 = 1.0 - ADAM_B1 ** ADAM_STEP
    c2 = 1.0 - ADAM_B2 ** ADAM_STEP

    def body(w_ref, m_ref, v_ref, g_ref, go_ref, d_ref, mo_ref, vo_ref):
        g = g_ref[0].astype(F32)
        for i in range(1, n):
            g = g + g_ref[i].astype(F32)
        mn = ADAM_B1 * m_ref[...] + (1.0 - ADAM_B1) * g
        vn = ADAM_B2 * v_ref[...] + (1.0 - ADAM_B2) * (g * g)
        go_ref[...] = g
        mo_ref[...] = mn
        vo_ref[...] = vn
        d_ref[...] = -ADAM_LR * ((mn / c1) / (jnp.sqrt(vn / c2) + ADAM_EPS) + ADAM_WD * w_ref[...])

    blk = pl.BlockSpec((br, c), lambda i: (i, 0))
    return pl.pallas_call(
        body, name=name, grid=(r // br,),
        in_specs=[blk, blk, blk, pl.BlockSpec((n, br, c), lambda i: (0, i, 0))],
        out_specs=[blk] * 4, out_shape=[jax.ShapeDtypeStruct((r, c), F32)] * 4,
        compiler_params=_cparams(("parallel",)),
    )(w, m, v, gparts)


def _my_coords():
    return lax.axis_index("x"), lax.axis_index("y"), lax.axis_index("c")


COMM_SEMS = [pltpu.SemaphoreType.DMA((7,)), pltpu.SemaphoreType.DMA((7,)), pltpu.SemaphoreType.DMA]


class _Gather:
    def __init__(self, x_ref, out_ref, send_sems, recv_sems, local_sem):
        self.x_ref, self.out_ref = x_ref, out_ref
        self.send_sems, self.recv_sems, self.local_sem = send_sems, recv_sems, local_sem
        self.m_per = x_ref.shape[0]
        x, y, c = _my_coords()
        self.c = c
        self.me, self.sibling = (x, y, c), (x, y, 1 - c)
        self.chips = [(1 - x, y), (x, 1 - y), (1 - x, 1 - y)]

    def rows(self, px, py, pc):
        return self.out_ref.at[pl.ds((4 * px + 2 * py + pc) * self.m_per, self.m_per), :]

    def copy(self, k, block, to, src=None):
        return pltpu.make_async_remote_copy(
            src_ref=self.rows(*block) if src is None else src, dst_ref=self.rows(*block),
            send_sem=self.send_sems.at[k], recv_sem=self.recv_sems.at[k], device_id=to, device_id_type=MESH)

    def mine(self):
        return pltpu.make_async_copy(self.x_ref, self.rows(*self.me), self.local_sem)

    def first(self):
        out = [self.copy(0, self.me, self.sibling, src=self.x_ref)]
        return out + [self.copy(1 + j, self.me, (*chip, self.c), src=self.x_ref) for j, chip in enumerate(self.chips)]

    def passed(self):
        return [self.copy(4 + j, (*chip, self.c), self.sibling) for j, chip in enumerate(self.chips)]

    def start(self):
        self.mine().start()
        for cp in self.first():
            cp.start()

    def forward(self):
        passed = self.passed()
        for j, chip in enumerate(self.chips):
            self.copy(1 + j, (*chip, self.c), self.me).wait_recv()
            passed[j].start()

    def finish(self):
        self.copy(0, self.sibling, self.me).wait_recv()
        for j, chip in enumerate(self.chips):
            self.copy(4 + j, (*chip, 1 - self.c), self.me).wait_recv()
        for cp in self.first() + self.passed():
            cp.wait_send()
        self.mine().wait()


class _Exchange:
    def __init__(self, send_ref, recv_ref, send_sems, recv_sems, local_sem):
        self.send_ref, self.recv_ref = send_ref, recv_ref
        self.send_sems, self.recv_sems, self.local_sem = send_sems, recv_sems, local_sem
        x, y, c = _my_coords()
        self.me = 4 * x + 2 * y + c
        self.peers = []
        for k in range(1, N_DEV):
            px = 1 - x if k & 4 else x
            py = 1 - y if k & 2 else y
            pc = 1 - c if k & 1 else c
            self.peers.append((4 * px + 2 * py + pc, (px, py, pc)))

    def mine(self):
        return pltpu.make_async_copy(self.send_ref.at[self.me], self.recv_ref.at[self.me], self.local_sem)

    def copy(self, k, src_slot, dst_slot):
        return pltpu.make_async_remote_copy(
            src_ref=self.send_ref.at[src_slot], dst_ref=self.recv_ref.at[dst_slot],
            send_sem=self.send_sems.at[k], recv_sem=self.recv_sems.at[k],
            device_id=self.peers[k][1], device_id_type=MESH)

    def start(self):
        self.mine().start()
        for k, (peer, _) in enumerate(self.peers):
            self.copy(k, peer, self.me).start()

    def finish(self):
        for k, (peer, _) in enumerate(self.peers):
            self.copy(k, peer, self.me).wait_send()
            self.copy(k, self.me, peer).wait_recv()
        self.mine().wait()


def _all_gather(x_shard, in_vmem, with_sum, name):
    m_per, n = x_shard.shape

    def body(x_ref, out_ref, *rest):
        if with_sum:
            sum_ref, send_sems, recv_sems, local_sem = rest
        else:
            send_sems, recv_sems, local_sem = rest
        g = _Gather(x_ref, out_ref, send_sems, recv_sems, local_sem)
        g.start()
        g.forward()
        g.finish()
        if with_sum:
            acc = out_ref[pl.ds(0, m_per), :]
            for d in range(1, N_DEV):
                acc = acc + out_ref[pl.ds(d * m_per, m_per), :]
            sum_ref[...] = acc

    space = pltpu.VMEM if in_vmem else pl.ANY
    out_shape = [jax.ShapeDtypeStruct((N_DEV * m_per, n), x_shard.dtype)]
    out_specs = [pl.BlockSpec(memory_space=space)]
    if with_sum:
        out_shape.append(jax.ShapeDtypeStruct((m_per, n), x_shard.dtype))
        out_specs.append(pl.BlockSpec(memory_space=pltpu.VMEM))
    res = pl.pallas_call(
        body, name=name, out_shape=out_shape, in_specs=[pl.BlockSpec(memory_space=space)], out_specs=out_specs,
        scratch_shapes=COMM_SEMS, compiler_params=pltpu.CompilerParams(vmem_limit_bytes=VMEM_LIMIT),
    )(x_shard)
    return res if with_sum else res[0]


def _comm_sems(n):
    return [pltpu.SemaphoreType.DMA((n, 7)), pltpu.SemaphoreType.DMA((n, 7)), pltpu.SemaphoreType.DMA((n,))]


class _Many:
    def __init__(self, kind, ins, outs, send_sems, recv_sems, local_sems):
        self.parts = [kind(i, o, send_sems.at[b], recv_sems.at[b], local_sems.at[b])
                      for b, (i, o) in enumerate(zip(ins, outs))]

    def start(self):
        for part in self.parts:
            part.start()

    def forward(self):
        for part in self.parts:
            part.forward()

    def finish(self):
        for part in self.parts:
            part.finish()


def _gathered_shapes(shards):
    return [jax.ShapeDtypeStruct((N_DEV * s.shape[0],) + s.shape[1:], s.dtype) for s in shards]


def _all_gather_many(shards, name):
    n = len(shards)

    def body(*refs):
        g = _Many(_Gather, refs[:n], refs[n:2 * n], *refs[2 * n:])
        g.start()
        g.forward()
        g.finish()

    hbm = pl.BlockSpec(memory_space=pl.ANY)
    return pl.pallas_call(body, name=name, out_shape=_gathered_shapes(shards), in_specs=[hbm] * n,
                          out_specs=[hbm] * n, scratch_shapes=_comm_sems(n))(*shards)


def _all_to_all_many(sends, name):
    n = len(sends)

    def body(*refs):
        ex = _Many(_Exchange, refs[:n], refs[n:2 * n], *refs[2 * n:])
        ex.start()
        ex.finish()

    hbm = pl.BlockSpec(memory_space=pl.ANY)
    return pl.pallas_call(body, name=name, out_shape=[jax.ShapeDtypeStruct(s.shape, s.dtype) for s in sends],
                          in_specs=[hbm] * n, out_specs=[hbm] * n, scratch_shapes=_comm_sems(n))(*sends)


def _row(v):
    return v.reshape(1, -1)


def _ffn_fwd(x, w_in, w_out, g_pre, g_post, m, res_w, tag):
    shift, scale, gate = m[0], m[1], m[2]
    mpre = _row(g_pre * (1.0 + scale))
    mpost = _row(res_w * gate * g_post)
    h = _rms_fwd(x, mpre, _row(shift), None, BF16, tag + "_pre")
    ua, ub, s = _ffn_in_swiglu(h, w_in, tag + "_in")
    y = _matmul(s, w_out, name=tag + "_out")
    x_new = _rms_fwd(y, mpost, jnp.zeros_like(mpost), x, F32, tag + "_post")
    return x_new, (x, h, ua, ub, s, y, mpre, mpost)


def _sub_bwd_post(dx_new, y, mpost, g_post, gate, res_w, tag):
    dy, c1, _ = _rms_bwd(dx_new, y, mpost, None, BF16, tag + "_post_bwd")
    c1 = c1[0]
    return dy, c1 * res_w * g_post, c1 * res_w * gate


def _sub_bwd_pre(dh, x, mpre, dx_new, g_pre, scale, tag):
    dx, c2, c3 = _rms_bwd(dh, x, mpre, dx_new, F32, tag + "_pre_bwd")
    c2, c3 = c2[0], c3[0]
    return dx, c3, c2 * g_pre, c2 * (1.0 + scale)


def _ffn_bwd(dx_new, saved, w_in, w_out, g_pre, g_post, m, res_w, tag):
    x, h, ua, ub, s, y, mpre, mpost = saved
    scale, gate = m[1], m[2]
    dy, dgate, dg_post = _sub_bwd_post(dx_new, y, mpost, g_post, gate, res_w, tag)
    ds = _matmul(dy, w_out, tb=True, out_dtype=BF16, name=tag + "_out_dx")
    dw_out = _matmul(s, dy, ta=True, out_dtype=BF16, name=tag + "_out_dw")
    du = _swiglu_bwd(ua, ub, ds, tag + "_act_bwd")
    dh = _matmul(du, w_in, tb=True, name=tag + "_in_dx")
    dw_in = _matmul(h, du, ta=True, out_dtype=BF16, name=tag + "_in_dw")
    dx, dshift, dscale, dg_pre = _sub_bwd_pre(dh, x, mpre, dx_new, g_pre, scale, tag)
    return dx, dw_in, dw_out, jnp.stack([dshift, dscale, dgate]), dg_pre, dg_post


def _slope_cols(gi):
    _, r = C_GROUPS[gi]
    sl = jnp.asarray(_alibi_slopes(C_HEADS)[gi * C_HPG:(gi + 1) * C_HPG], F32) * float(r)
    return jnp.repeat(sl, HDIM).reshape(1, C_OUT)


def _mix_fwd(x, w, g_pre, g_post, m, lb, hn, tag, gather=None):
    t, d = x.shape
    shift, scale, gate = m[0], m[1], m[2]
    mpre = _row(g_pre * (1.0 + scale))
    mpost = _row(gate * g_post)
    h = _rms_fwd(x, mpre, _row(shift), None, BF16, tag + "_pre")
    p = _matmul(h, w["w_in"], name=tag + "_in")
    hn2 = _row(jnp.tile(hn, 2))
    ya, oa, states = _hgrn_fwd(p, _row(lb), hn2, tag + "_hgrn")
    kv = p[:, COL_BK:COL_CQ].astype(BF16)
    if gather is None:
        (yb, sb_tot), gathered = _sb_fwd(p, kv, tag + "_sb"), None
    else:
        res = _sb_fwd(p, kv, tag + "_sb_gather", gather)
        yb, sb_tot, gathered = res[0], res[1], list(res[2:])
    og, lg = zip(*[_dil_fwd(p, gi, tag + "_dil%d" % gi) for gi in range(len(C_GROUPS))])
    yc, lse_c = _dil_merge(og, lg, tag + "_dil_merge")
    gl = p[:, COL_GATE:]
    ws = (w["w_branch_a"], w["w_branch_b"], w["w_branch_c"])
    merged = _gate_fwd((ya, yb, yc), gl, ws, tag + "_gate")
    y = _matmul(merged, w["w_out"], name=tag + "_out")
    x_new = _rms_fwd(y, mpost, jnp.zeros_like(mpost), x, F32, tag + "_post")
    return x_new, (x, h, p, hn2, ya, oa, states, yb, kv, sb_tot, yc, lse_c, gl, merged, y, mpre, mpost), gathered


def _mix_bwd(dx_new, saved, w, g_pre, g_post, m, lb, tag, exchange=None):
    x, h, p, hn2, ya, oa, states, yb, kv, sb_tot, yc, lse_c, gl, merged, y, mpre, mpost = saved
    t = x.shape[0]
    scale, gate = m[1], m[2]
    dy, dgate, dg_post = _sub_bwd_post(dx_new, y, mpost, g_post, gate, 1.0, tag)
    dmerged = _matmul(dy, w["w_out"], tb=True, out_dtype=BF16, name=tag + "_out_dx")
    dw_out = _matmul(merged, dy, ta=True, out_dtype=BF16, name=tag + "_out_dw")
    ws = (w["w_branch_a"], w["w_branch_b"], w["w_branch_c"])
    dya, dyb, dyc, dgl, dwa, dwb, dwc = _gate_bwd(dmerged, (ya, yb, yc), gl, ws, tag + "_gate_bwd")
    dqa, dfa, dia, dga, dlb, dhn = _hgrn_bwd(p, _row(lb), hn2, oa, states, dya, tag + "_hgrn_bwd")
    if exchange is None:
        (dbq, dbk, dbv), received = _sb_bwd(p, kv, sb_tot, dyb, tag + "_sb_bwd"), None
    else:
        res = _sb_bwd(p, kv, sb_tot, dyb, tag + "_sb_bwd_exchange", exchange)
        dbq, dbk, dbv, received = res[0], res[1], res[2], list(res[3:])
    dcq, dck, dcv = zip(*[_dil_bwd(p, dyc, yc, lse_c, gi, tag + "_dil%d_bwd" % gi) for gi in range(len(C_GROUPS))])
    dp = jnp.concatenate([dqa, dfa, dia, dga, dbq, dbk, dbv, *dcq, *dck, *dcv, dgl], axis=1).astype(BF16)
    dh = _matmul(dp, w["w_in"], tb=True, name=tag + "_in_dx")
    dw_in = _matmul(h, dp, ta=True, out_dtype=BF16, name=tag + "_in_dw")
    dx, dshift, dscale, dg_pre = _sub_bwd_pre(dh, x, mpre, dx_new, g_pre, scale, tag)
    dhn_v = jnp.sum(dhn, axis=(0, 1))
    dhn_v = dhn_v[:A_VDIM] + dhn_v[A_VDIM:]
    dws = dict(w_in=dw_in, w_out=dw_out, w_branch_a=dwa.astype(BF16), w_branch_b=dwb.astype(BF16),
               w_branch_c=dwc.astype(BF16))
    return dx, dws, jnp.stack([dshift, dscale, dgate]), dg_pre, dg_post, dlb[0], dhn_v, received


class _LocalWeights:
    def __init__(self, wts):
        self.wts = wts

    def first(self):
        return None

    def shard(self, l):
        return None

    def layer(self, l, gathered):
        return {k: v[l] for k, v in self.wts.items()}

    fused = False

    def pack(self, names, dws):
        return [dws[k] for k in names]

    def last(self, packed):
        return packed


class _ShardedWeights:
    def __init__(self, shards):
        self.shards = shards

    def shard(self, l):
        return [self.shards[k][l].astype(BF16) for k in BIG_WEIGHTS]

    def first(self):
        return _all_gather_many(self.shard(0), "weights_all_gather")

    def layer(self, l, gathered):
        out = {}
        for k, got in zip(BIG_WEIGHTS, gathered):
            _, r, c = self.shards[k].shape
            out[k] = got if k in ROW_SHARDED else got.reshape(N_DEV, r, c).transpose(1, 0, 2).reshape(r, N_DEV * c)
        return out

    fused = True

    def pack(self, names, dws):
        out = []
        for k in names:
            _, r, c = self.shards[k].shape
            g = dws[k]
            out.append(g.reshape(N_DEV, r, c) if k in ROW_SHARDED else g.reshape(r, N_DEV, c).transpose(1, 0, 2))
        return out

    def last(self, packed):
        return _all_to_all_many(packed, "grads_all_to_all")

    def partial_sums(self, received, depth):
        return {k: jnp.concatenate([received[l, k] for l in range(depth)], axis=1) for k in BIG_WEIGHTS}


def _local_step(x, target, mod, norm_g, lb_all, hnorm, supply):
    depth = mod.shape[0]
    d = x.shape[1]
    saved, wls = [], []
    gathered = supply.first()
    for l in range(depth):
        wl = supply.layer(l, gathered)
        wls.append(wl)
        x, s0 = _ffn_fwd(x, wl["ffn1_w_in"], wl["ffn1_w_out"], norm_g[l, 0], norm_g[l, 1], mod[l, 0], 0.5, "ffn1")
        nxt = supply.shard(l + 1) if l + 1 < depth else None
        x, s1, gathered = _mix_fwd(x, wl, norm_g[l, 2], norm_g[l, 3], mod[l, 1], lb_all[l], hnorm[l], "mix", nxt)
        x, s2 = _ffn_fwd(x, wl["ffn2_w_in"], wl["ffn2_w_out"], norm_g[l, 4], norm_g[l, 5], mod[l, 2], 0.5, "ffn2")
        saved.append((s0, s1, s2))
    dx, sq = _loss_head(x, target, "loss_head")
    loss = 0.5 * jnp.sum(sq) / d
    dmod, dng, dlb, dhn = [], [], [], []
    early = ("ffn2_w_in", "ffn2_w_out")
    late = tuple(k for k in BIG_WEIGHTS if k not in early)
    returned = {}
    waiting = []
    for l in reversed(range(depth)):
        wl = wls[l]
        s0, s1, s2 = saved[l]
        dx, dwi2, dwo2, dm2, dgp2, dgq2 = _ffn_bwd(dx, s2, wl["ffn2_w_in"], wl["ffn2_w_out"], norm_g[l, 4],
                                                   norm_g[l, 5], mod[l, 2], 0.5, "ffn2")
        waiting += zip([(l, k) for k in early], supply.pack(early, dict(ffn2_w_in=dwi2, ffn2_w_out=dwo2)))
        keys, bufs = [k for k, _ in waiting], [b for _, b in waiting]
        dx, dwm, dm1, dgp1, dgq1, dlb_l, dhn_l, received = _mix_bwd(
            dx, s1, wl, norm_g[l, 2], norm_g[l, 3], mod[l, 1], lb_all[l], "mix", bufs if supply.fused else None)
        returned.update(zip(keys, received if supply.fused else bufs))
        dx, dwi1, dwo1, dm0, dgp0, dgq0 = _ffn_bwd(dx, s0, wl["ffn1_w_in"], wl["ffn1_w_out"], norm_g[l, 0],
                                                   norm_g[l, 1], mod[l, 0], 0.5, "ffn1")
        dmod.append(jnp.stack([dm0, dm1, dm2]))
        dng.append(jnp.stack([dgp0, dgq0, dgp1, dgq1, dgp2, dgq2]))
        dlb.append(dlb_l)
        dhn.append(dhn_l)
        waiting = list(zip([(l, k) for k in late], supply.pack(late, dict(dwm, ffn1_w_in=dwi1, ffn1_w_out=dwo1))))
    returned.update(zip([k for k, _ in waiting], supply.last([b for _, b in waiting])))
    rev = lambda lst: jnp.stack(lst[::-1])
    return loss, dx, rev(dmod), rev(dng), rev(dlb), rev(dhn), returned


def _lb_all(logits):
    lb_p = jax.nn.softmax(logits.astype(F32), axis=0)
    return jnp.cumsum(lb_p, axis=0) - lb_p[0:1]


def _pad_rows(a, rows):
    return jnp.pad(a, ((0, rows - a.shape[0]), (0, 0)))


def kernel(x, c, w_ada, b_ada, norm_g, ffn1_w_in, ffn1_w_out, w_in, hgrn_lb_logits, hgrn_norm_g, w_branch_a, w_branch_b, w_branch_c, w_out, ffn2_w_in, ffn2_w_out, loss_target, m_w_ada, m_b_ada, m_norm_g, m_ffn1_w_in, m_ffn1_w_out, m_w_in, m_hgrn_lb_logits, m_hgrn_norm_g, m_w_branch_a, m_w_branch_b, m_w_branch_c, m_w_out, m_ffn2_w_in, m_ffn2_w_out, v_w_ada, v_b_ada, v_norm_g, v_ffn1_w_in, v_ffn1_w_out, v_w_in, v_hgrn_lb_logits, v_hgrn_norm_g, v_w_branch_a, v_w_branch_b, v_w_branch_c, v_w_out, v_ffn2_w_in, v_ffn2_w_out):
    weights = dict(w_ada=w_ada, b_ada=b_ada, norm_g=norm_g, ffn1_w_in=ffn1_w_in, ffn1_w_out=ffn1_w_out, w_in=w_in,
                   hgrn_lb_logits=hgrn_lb_logits, hgrn_norm_g=hgrn_norm_g, w_branch_a=w_branch_a,
                   w_branch_b=w_branch_b, w_branch_c=w_branch_c, w_out=w_out, ffn2_w_in=ffn2_w_in,
                   ffn2_w_out=ffn2_w_out)
    mom1 = dict(w_ada=m_w_ada, b_ada=m_b_ada, norm_g=m_norm_g, ffn1_w_in=m_ffn1_w_in, ffn1_w_out=m_ffn1_w_out,
                w_in=m_w_in, hgrn_lb_logits=m_hgrn_lb_logits, hgrn_norm_g=m_hgrn_norm_g, w_branch_a=m_w_branch_a,
                w_branch_b=m_w_branch_b, w_branch_c=m_w_branch_c, w_out=m_w_out, ffn2_w_in=m_ffn2_w_in,
                ffn2_w_out=m_ffn2_w_out)
    mom2 = dict(w_ada=v_w_ada, b_ada=v_b_ada, norm_g=v_norm_g, ffn1_w_in=v_ffn1_w_in, ffn1_w_out=v_ffn1_w_out,
                w_in=v_w_in, hgrn_lb_logits=v_hgrn_lb_logits, hgrn_norm_g=v_hgrn_norm_g, w_branch_a=v_w_branch_a,
                w_branch_b=v_w_branch_b, w_branch_c=v_w_branch_c, w_out=v_w_out, ffn2_w_in=v_ffn2_w_in,
                ffn2_w_out=v_ffn2_w_out)
    order = list(weights)
    depth, d, ada_cols = w_ada.shape
    nd = d // LANES
    xi, yi, ci = _my_coords()
    me = 4 * xi + 2 * yi + ci

    small = jnp.concatenate([c.reshape(nd, LANES), norm_g.reshape(depth * 6, LANES)], axis=0)
    g1 = _all_gather(small, True, False, "small_all_gather").reshape(N_DEV, small.shape[0], LANES)
    c_act = _silu(g1[:, :nd].reshape(N_DEV, d))
    norm_full = g1[:, nd:].reshape(N_DEV, depth, 6, LANES).transpose(1, 2, 0, 3).reshape(depth, 6, d)

    c_pad = _pad_rows(c_act, 16)
    mod_sh = jnp.stack([_matmul(c_pad, w_ada[l], name="ada_mod")[:N_DEV]
                        + lax.dynamic_slice_in_dim(b_ada[l], me * ada_cols, ada_cols)[None]
                        for l in range(depth)])
    g2 = _all_gather(mod_sh.reshape(-1, LANES), True, False, "mod_all_gather")
    g2 = g2.reshape(N_DEV, depth, N_DEV, ada_cols)
    mod = lax.dynamic_index_in_dim(g2, me, axis=2, keepdims=False)
    mod = mod.transpose(1, 0, 2).reshape(depth, 3, 3, d)

    supply = _ShardedWeights({k: weights[k] for k in BIG_WEIGHTS})
    lb_all, lb_vjp = jax.vjp(_lb_all, hgrn_lb_logits)

    loss, dx, dmod, dng, dlb, dhn, received = _local_step(x[0], loss_target[0], mod, norm_full, lb_all,
                                                          hgrn_norm_g, supply)
    loss = lax.psum(loss, ("x", "y", "c"))

    dhn_pad = jnp.pad(dhn.reshape(-1), (0, 8 * LANES - dhn.size))
    pieces = [dmod.reshape(-1), dng.reshape(-1), dlb.reshape(-1), dhn_pad]
    sizes = [p_.size for p_ in pieces]
    smallg = jnp.concatenate(pieces).reshape(-1, LANES)
    g3, gsum = _all_gather(smallg, True, True, "small_grads_all_gather")
    g3 = g3.reshape(N_DEV, -1)
    gsum = gsum.reshape(-1)
    dmod_all = g3[:, :sizes[0]].reshape(N_DEV, depth, 9 * d)
    o1 = sizes[0]
    grads = {}
    grads["b_ada"] = gsum[:o1].reshape(depth, 9 * d)
    dng_sum = gsum[o1:o1 + sizes[1]].reshape(depth, 6, nd, LANES)
    grads["norm_g"] = lax.dynamic_index_in_dim(dng_sum, me, axis=2, keepdims=False)
    o2 = o1 + sizes[1]
    dlb_sum = gsum[o2:o2 + sizes[2]].reshape(depth, A_QK)
    grads["hgrn_lb_logits"] = lb_vjp(dlb_sum)[0]
    o3 = o2 + sizes[2]
    grads["hgrn_norm_g"] = gsum[o3:o3 + dhn.size].reshape(depth, A_VDIM)
    dmod_mine = lax.dynamic_slice_in_dim(dmod_all, me * ada_cols, ada_cols, axis=2)
    grads["w_ada"] = jnp.stack([_matmul(c_pad, _pad_rows(dmod_mine[:, l], 16), ta=True, name="ada_dw")
                                for l in range(depth)])

    gparts = supply.partial_sums(received, depth)

    outs = {}
    for k in order:
        w = weights[k]
        w2 = w.reshape(-1, w.shape[-1])
        gp = gparts[k] if k in gparts else grads[k].reshape((1,) + w2.shape)
        res = _adamw(w2, mom1[k].reshape(w2.shape), mom2[k].reshape(w2.shape), gp, "adamw")
        outs[k] = [r.reshape(w.shape) for r in res]
    return (loss, dx[None], *[outs[k][0] for k in order], *[outs[k][1] for k in order],
            *[outs[k][2] for k in order], *[outs[k][3] for k in order])
```

```python
import functools
import math

import jax
import jax.numpy as jnp
from jax import lax
from jax.experimental import pallas as pl
from jax.experimental.pallas import tpu as pltpu

F32 = jnp.float32
BF16 = jnp.bfloat16

A_HEADS, A_KDIM, A_VDIM, A_CHUNK = 6, 128, 64, 64
B_HEADS, HDIM = 6, 64
C_GROUPS = ((128, 1), (512, 4), (2048, 16))
C_HPG = 4
C_HEADS = C_HPG * len(C_GROUPS)
N_BRANCH = 3
EPS = 1e-6
NEG_BIG = -1e30
TINY = 1e-30
A_QK = A_HEADS * A_KDIM
A_V = A_HEADS * A_VDIM
B_W = B_HEADS * HDIM
C_W = C_HEADS * HDIM
C_OUT = C_HPG * HDIM
COL_AQ, COL_AF, COL_AI, COL_AG = 0, A_QK, 2 * A_QK, 2 * A_QK + A_V
COL_BQ = 2 * A_QK + 2 * A_V
COL_BK, COL_BV = COL_BQ + B_W, COL_BQ + 2 * B_W
COL_CQ = COL_BQ + 3 * B_W
COL_CK, COL_CV = COL_CQ + C_W, COL_CQ + 2 * C_W
COL_GATE = COL_CQ + 3 * C_W

ADAM_LR, ADAM_B1, ADAM_B2, ADAM_EPS, ADAM_WD, ADAM_STEP = 0.001, 0.9, 0.999, 1e-08, 0.01, 10

N_DEV = 8
LANES = 128
VMEM_LIMIT = 48 * 1024 * 1024
MATMUL_VMEM_BUDGET = 28 * 1024 * 1024
SUB = 16
EXP_CLAMP = 80.0
MESH = pl.DeviceIdType.MESH

BIG_WEIGHTS = ("ffn1_w_in", "ffn1_w_out", "w_in", "w_branch_a", "w_branch_b", "w_branch_c", "w_out",
               "ffn2_w_in", "ffn2_w_out")
ROW_SHARDED = ("ffn1_w_out", "w_out", "ffn2_w_out")


def _cparams(sem):
    return pltpu.CompilerParams(dimension_semantics=sem, vmem_limit_bytes=VMEM_LIMIT)


def _tile(n, cap):
    best, t = None, LANES
    while t <= min(n, cap):
        if n % t == 0:
            best = t
        t += LANES
    return best or n


def _rows(t, cap=256):
    r = cap
    while t % r:
        r //= 2
    return r


def _divisors(n):
    return [t for t in range(LANES, n + 1, LANES) if n % t == 0] or [n]


def _matmul_tiles(m, n, k, a_size, b_size, o_size):
    best, best_key = None, None
    for tm in _divisors(m):
        for tn in _divisors(n):
            for tk in _divisors(k):
                if tm > 1024 or tn > 3072 or tk > 4096:
                    continue
                cast = (tm * tk * 2 if a_size > 2 else 0) + (tk * tn * 2 if b_size > 2 else 0)
                need = 2 * (tm * tk * a_size + tk * tn * b_size + tm * tn * o_size) + 2 * tm * tn * 4 + cast
                if need > MATMUL_VMEM_BUDGET:
                    continue
                key = (tm * tn * tk, tk)
                if best_key is None or key > best_key:
                    best, best_key = (tm, tn, tk), key
    return best


def _dot(a, b):
    return jnp.dot(a, b, preferred_element_type=F32)


def _dot_nt(a, b):
    return lax.dot_general(a, b, (((1,), (1,)), ((), ())), preferred_element_type=F32)


def _dot_tn(a, b):
    return lax.dot_general(a, b, (((0,), (0,)), ((), ())), preferred_element_type=F32)


def _split3(x):
    h = x.astype(BF16)
    r = x - h.astype(F32)
    m = r.astype(BF16)
    lo = (r - m.astype(F32)).astype(BF16)
    return h, m, lo


def _ones_left(mat01, x):
    h, m, lo = _split3(x)
    return _dot(mat01, h) + _dot(mat01, m) + _dot(mat01, lo)


def _silu(x):
    return x * jax.nn.sigmoid(x)


def _dsilu(x):
    s = jax.nn.sigmoid(x)
    return s * (1.0 + x * (1.0 - s))


def _matmul(a, b, *, ta=False, tb=False, out_dtype=F32, name):
    if ta:
        kdim, m = a.shape
    else:
        m, kdim = a.shape
    n = b.shape[0] if tb else b.shape[1]
    tm, tn, tk = _matmul_tiles(m, n, kdim, a.dtype.itemsize, b.dtype.itemsize, jnp.dtype(out_dtype).itemsize)
    nk = kdim // tk
    ni, nj = m // tm, n // tn
    a_bytes, b_bytes = m * kdim * a.dtype.itemsize, kdim * n * b.dtype.itemsize
    j_outer = nk == 1 and (b_bytes + a_bytes * nj) < (a_bytes + b_bytes * ni)
    dims = (((0 if ta else 1,), (1 if tb else 0,)), ((), ()))

    def body(a_ref, b_ref, o_ref, *scratch):
        p = lax.dot_general(a_ref[...].astype(BF16), b_ref[...].astype(BF16), dims, preferred_element_type=F32)
        if nk == 1:
            o_ref[...] = p.astype(o_ref.dtype)
            return
        acc = scratch[0]
        k = pl.program_id(2)

        @pl.when(k == 0)
        def _():
            acc[...] = p

        @pl.when(k > 0)
        def _():
            acc[...] += p

        @pl.when(k == nk - 1)
        def _():
            o_ref[...] = acc[...].astype(o_ref.dtype)

    def spec(shape, pick):
        if j_outer:
            return pl.BlockSpec(shape, lambda j, i, k: pick(i, j, k))
        return pl.BlockSpec(shape, lambda i, j, k: pick(i, j, k))

    a_spec = spec((tk, tm), lambda i, j, k: (k, i)) if ta else spec((tm, tk), lambda i, j, k: (i, k))
    b_spec = spec((tn, tk), lambda i, j, k: (j, k)) if tb else spec((tk, tn), lambda i, j, k: (k, j))
    return pl.pallas_call(
        body, name=name, grid=(nj, ni, nk) if j_outer else (ni, nj, nk), in_specs=[a_spec, b_spec],
        out_specs=spec((tm, tn), lambda i, j, k: (i, j)),
        out_shape=jax.ShapeDtypeStruct((m, n), out_dtype),
        scratch_shapes=[pltpu.VMEM((tm, tn), F32)] if nk > 1 else [],
        compiler_params=_cparams(("parallel", "parallel", "arbitrary")),
    )(a, b)


def _rms_fwd(z, mcol, acol, res, out_dtype, name):
    t, d = z.shape
    tr = _rows(t)
    has_res = res is not None

    def body(*refs):
        if has_res:
            z_ref, m_ref, a_ref, r_ref, o_ref = refs
        else:
            z_ref, m_ref, a_ref, o_ref = refs
        zf = z_ref[...]
        r = lax.rsqrt(jnp.mean(zf * zf, axis=-1, keepdims=True) + EPS)
        y = zf * r * m_ref[...] + a_ref[...]
        if has_res:
            y = r_ref[...] + y
        o_ref[...] = y.astype(o_ref.dtype)

    row = pl.BlockSpec((tr, d), lambda i: (i, 0))
    col = pl.BlockSpec((1, d), lambda i: (0, 0))
    ins = [z, mcol, acol] + ([res] if has_res else [])
    return pl.pallas_call(
        body, name=name, grid=(t // tr,), in_specs=[row, col, col] + ([row] if has_res else []),
        out_specs=row, out_shape=jax.ShapeDtypeStruct((t, d), out_dtype),
        compiler_params=_cparams(("parallel",)),
    )(*ins)


def _rms_bwd(d_out, z, mcol, dres, out_dtype, name):
    t, d = z.shape
    tr = _rows(t)
    has_res = dres is not None

    def body(*refs):
        if has_res:
            d_ref, z_ref, m_ref, r_ref, o_ref, s1_ref, s2_ref = refs
        else:
            d_ref, z_ref, m_ref, o_ref, s1_ref, s2_ref = refs
        i = pl.program_id(0)
        zf = z_ref[...]
        r = lax.rsqrt(jnp.mean(zf * zf, axis=-1, keepdims=True) + EPS)
        zh = zf * r
        df = d_ref[...].astype(F32)
        dzh = df * m_ref[...]
        dz = r * (dzh - zh * jnp.mean(dzh * zh, axis=-1, keepdims=True))
        if has_res:
            dz = dz + r_ref[...]
        o_ref[...] = dz.astype(o_ref.dtype)
        s1 = jnp.sum(df * zh, axis=0, keepdims=True)
        s2 = jnp.sum(df, axis=0, keepdims=True)

        @pl.when(i == 0)
        def _():
            s1_ref[...] = s1
            s2_ref[...] = s2

        @pl.when(i > 0)
        def _():
            s1_ref[...] += s1
            s2_ref[...] += s2

    row = pl.BlockSpec((tr, d), lambda i: (i, 0))
    col = pl.BlockSpec((1, d), lambda i: (0, 0))
    ins = [d_out, z, mcol] + ([dres] if has_res else [])
    return pl.pallas_call(
        body, name=name, grid=(t // tr,), in_specs=[row, row, col] + ([row] if has_res else []),
        out_specs=[row, col, col],
        out_shape=[jax.ShapeDtypeStruct((t, d), out_dtype), jax.ShapeDtypeStruct((1, d), F32),
                   jax.ShapeDtypeStruct((1, d), F32)],
        compiler_params=_cparams(("arbitrary",)),
    )(*ins)


FFN_IN_TILE = (512, 1408)


def _ffn_in_swiglu(h, w_in, name):
    t, d = h.shape
    f = w_in.shape[1] // 2
    tm, tn = _tile(t, FFN_IN_TILE[0]), _tile(f, FFN_IN_TILE[1])
    nj = f // tn

    def body(h_ref, wa_ref, wb_ref, a_ref, b_ref, s_ref):
        hv = h_ref[...].astype(BF16)
        a = _dot(hv, wa_ref[...].astype(BF16))
        b = _dot(hv, wb_ref[...].astype(BF16))
        a_ref[...] = a.astype(BF16)
        b_ref[...] = b.astype(BF16)
        s_ref[...] = (_silu(a) * b).astype(BF16)

    out = pl.BlockSpec((tm, tn), lambda j, i: (i, j))
    return pl.pallas_call(
        body, name=name, grid=(nj, t // tm),
        in_specs=[pl.BlockSpec((tm, d), lambda j, i: (i, 0)), pl.BlockSpec((d, tn), lambda j, i: (0, j)),
                  pl.BlockSpec((d, tn), lambda j, i: (0, j + nj))],
        out_specs=[out, out, out], out_shape=[jax.ShapeDtypeStruct((t, f), BF16)] * 3,
        compiler_params=_cparams(("parallel", "parallel")),
    )(h, w_in, w_in)


def _swiglu_bwd(ua, ub, ds, name):
    t, f = ua.shape
    tr = _rows(t)

    def body(a_ref, b_ref, ds_ref, du_ref):
        a = a_ref[...].astype(F32)
        b = b_ref[...].astype(F32)
        g = ds_ref[...].astype(F32)
        du_ref[:, :f] = (g * b * _dsilu(a)).astype(du_ref.dtype)
        du_ref[:, f:] = (g * _silu(a)).astype(du_ref.dtype)

    half = pl.BlockSpec((tr, f), lambda i: (i, 0))
    return pl.pallas_call(
        body, name=name, grid=(t // tr,), in_specs=[half, half, half],
        out_specs=pl.BlockSpec((tr, 2 * f), lambda i: (i, 0)), out_shape=jax.ShapeDtypeStruct((t, 2 * f), BF16),
        compiler_params=_cparams(("parallel",)),
    )(ua, ub, ds)


def _loss_head(y, target, name):
    t, d = y.shape
    tr = _rows(t)

    def body(y_ref, t_ref, dy_ref, sq_ref):
        i = pl.program_id(0)
        e = y_ref[...] - t_ref[...]
        dy_ref[...] = e * (1.0 / d)
        s = jnp.sum(e * e, axis=0, keepdims=True)

        @pl.when(i == 0)
        def _():
            sq_ref[...] = s

        @pl.when(i > 0)
        def _():
            sq_ref[...] += s

    row = pl.BlockSpec((tr, d), lambda i: (i, 0))
    col = pl.BlockSpec((1, d), lambda i: (0, 0))
    return pl.pallas_call(
        body, name=name, grid=(t // tr,), in_specs=[row, row], out_specs=[row, col],
        out_shape=[jax.ShapeDtypeStruct((t, d), F32), jax.ShapeDtypeStruct((1, d), F32)],
        compiler_params=_cparams(("arbitrary",)),
    )(y, target)


def _hgrn_consts():
    c = A_CHUNK
    shift = SUB.bit_length() - 1
    r = lax.broadcasted_iota(jnp.int32, (c, c), 0)
    s = lax.broadcasted_iota(jnp.int32, (c, c), 1)
    sub_r = lax.shift_right_logical(r, shift)
    incl = s <= r
    masks = [jnp.logical_and(sub_r == i, incl) for i in range(c // SUB)]
    rev_incl = jnp.where(s >= r, 1.0, 0.0).astype(BF16)
    r2 = lax.broadcasted_iota(jnp.int32, (2 * c + 8, c), 0)
    s2 = lax.broadcasted_iota(jnp.int32, (2 * c + 8, c), 1)
    sub_start = lax.shift_left(lax.shift_right_logical(r2 - c, shift), shift)
    running = jnp.where(s2 <= r2, 1.0, 0.0)
    before = jnp.where(s2 < sub_start, 1.0, 0.0)
    stack = jnp.where(r2 < c, running, jnp.where(r2 < 2 * c, before, 1.0)).astype(BF16)
    return stack, masks, incl, rev_incl


def _hgrn_gates(q_raw, f_raw, lbv, stack):
    sg = jax.nn.sigmoid(f_raw)
    sgn = jax.nn.sigmoid(-f_raw)
    f = lbv + (1.0 - lbv) * sg
    logf = jnp.log(jnp.maximum(f, TINY))
    return dict(sg=sg, sgn=sgn, f=f, k=(1.0 - lbv) * sgn, q=_silu(q_raw), bb=_ones_left(stack, logf))


def _hgrn_chunk(q_raw, f_raw, lbv, stack):
    return _hgrn_decays(_hgrn_gates(q_raw, f_raw, lbv, stack))


def _hgrn_decays(gates):
    c = A_CHUNK
    sg, sgn, f, k, q, bb = (gates[n] for n in ("sg", "sgn", "f", "k", "q", "bb"))
    b = bb[:c]
    bsrow = bb[c:2 * c]
    b_end = bb[2 * c:2 * c + 1]
    e_sub = jnp.exp(b - bsrow)
    e_b = jnp.exp(b)
    e_end = jnp.exp(b_end - b)
    qs = q * e_sub
    q_in = q * e_b
    kend = k * e_end
    kfac = [jnp.exp(jnp.minimum(bsrow[i * SUB:i * SUB + 1] - b, EXP_CLAMP)) for i in range(c // SUB)]
    return dict(sg=sg, sgn=sgn, f=f, k=k, q=q, b=b, b_end=b_end, e_sub=e_sub, e_b=e_b, e_end=e_end,
                qs=qs, q_in=q_in, kend=kend, kfac=kfac)


def _hgrn_scores(ch, masks):
    qs_b = ch["qs"].astype(BF16)
    a = None
    for i, mk in enumerate(masks):
        ki = (ch["k"] * ch["kfac"][i]).astype(BF16)
        part = jnp.where(mk, _dot_nt(qs_b, ki), 0.0)
        a = part if a is None else a + part
    return a


def _hgrn_fwd(p, lb, hn2, name):
    t = p.shape[0]
    tb = _rows(t)
    nt = t // tb
    nc = tb // A_CHUNK
    c = A_CHUNK

    def body(q_ref, f_ref, i_ref, g_ref, lb_ref, hn_ref, y_ref, o_ref, st_ref, s_scr):
        j = pl.program_id(1)

        @pl.when(j == 0)
        def _():
            s_scr[...] = jnp.zeros_like(s_scr)

        stack, masks, _, _ = _hgrn_consts()
        units = [(ci, hh) for ci in range(nc) for hh in range(2)]
        lsl = [slice(A_KDIM * hh, A_KDIM * (hh + 1)) for hh in range(2)]
        hsl = [slice(A_VDIM * hh, A_VDIM * (hh + 1)) for hh in range(2)]
        rows = [pl.ds(ci * c, c) for ci in range(nc)]
        gates = {u: _hgrn_gates(q_ref[rows[u[0]], lsl[u[1]]], f_ref[rows[u[0]], lsl[u[1]]], lb_ref[:, lsl[u[1]]], stack)
                 for u in units}
        ch = {u: _hgrn_decays(gates[u]) for u in units}
        v = {u: i_ref[rows[u[0]], hsl[u[1]]].astype(BF16) for u in units}
        a = {u: _hgrn_scores(ch[u], masks).astype(BF16) for u in units}
        grow = {u: _dot_tn(v[u], ch[u]["kend"].astype(BF16)) for u in units}
        states = [s_scr[0], s_scr[1]]
        entering = {}
        for ci, hh in units:
            entering[ci, hh] = states[hh]
            st_ref[hh, ci] = states[hh]
            states[hh] = states[hh] * jnp.exp(ch[ci, hh]["b_end"]) + grow[ci, hh]
        s_scr[0] = states[0]
        s_scr[1] = states[1]
        for u in units:
            o_ref[rows[u[0]], hsl[u[1]]] = (_dot_nt(ch[u]["q_in"].astype(BF16), entering[u].astype(BF16))
                                            + _dot(a[u], v[u]))
        for hh in range(2):
            hsl = slice(A_VDIM * hh, A_VDIM * (hh + 1))
            o = o_ref[:, hsl]
            r = lax.rsqrt(jnp.mean(o * o, axis=-1, keepdims=True) + EPS)
            y_ref[:, hsl] = (o * r * hn_ref[:, hsl] * _silu(g_ref[:, hsl])).astype(y_ref.dtype)

    w2 = 2 * A_KDIM
    return pl.pallas_call(
        body, name=name, grid=(A_HEADS // 2, nt),
        in_specs=[pl.BlockSpec((tb, w2), lambda h, j: (j, COL_AQ // w2 + h)),
                  pl.BlockSpec((tb, w2), lambda h, j: (j, COL_AF // w2 + h)),
                  pl.BlockSpec((tb, LANES), lambda h, j: (j, COL_AI // LANES + h)),
                  pl.BlockSpec((tb, LANES), lambda h, j: (j, COL_AG // LANES + h)),
                  pl.BlockSpec((1, w2), lambda h, j: (0, h)),
                  pl.BlockSpec((1, LANES), lambda h, j: (0, 0))],
        out_specs=[pl.BlockSpec((tb, LANES), lambda h, j: (j, h)),
                   pl.BlockSpec((tb, LANES), lambda h, j: (j, h)),
                   pl.BlockSpec((2, nc, A_VDIM, A_KDIM), lambda h, j: (h, j, 0, 0))],
        out_shape=[jax.ShapeDtypeStruct((t, A_V), BF16), jax.ShapeDtypeStruct((t, A_V), F32),
                   jax.ShapeDtypeStruct((A_HEADS, t // c, A_VDIM, A_KDIM), F32)],
        scratch_shapes=[pltpu.VMEM((2, A_VDIM, A_KDIM), F32)],
        compiler_params=_cparams(("parallel", "arbitrary")),
    )(p, p, p, p, lb, hn2)


def _hgrn_bwd(p, lb, hn2, o_raw, states, dya, name):
    t = p.shape[0]
    tb = _rows(t)
    nt = t // tb
    nc = tb // A_CHUNK
    c = A_CHUNK

    def body(q_ref, f_ref, i_ref, g_ref, lb_ref, hn_ref, o_ref, st_ref, dy_ref,
             dq_ref, df_ref, di_ref, dg_ref, dlb_ref, dhn_ref, ds_scr, do_scr):
        j = pl.program_id(1)

        @pl.when(j == 0)
        def _():
            ds_scr[...] = jnp.zeros_like(ds_scr)
            dlb_ref[...] = jnp.zeros_like(dlb_ref)
            dhn_ref[...] = jnp.zeros_like(dhn_ref)

        stack, masks, incl, rev_incl = _hgrn_consts()
        for hh in range(2):
            hsl = slice(A_VDIM * hh, A_VDIM * (hh + 1))
            o = o_ref[:, hsl]
            g = g_ref[:, hsl]
            dy = dy_ref[:, hsl].astype(F32)
            hn = hn_ref[:, hsl]
            r = lax.rsqrt(jnp.mean(o * o, axis=-1, keepdims=True) + EPS)
            oh = o * r
            sgate = _silu(g)
            dg_ref[:, hsl] = dy * oh * hn * _dsilu(g)
            dhn_ref[0, :, hsl] += jnp.sum(dy * oh * sgate, axis=0, keepdims=True)
            doh = dy * hn * sgate
            do_scr[:, hsl] = r * (doh - oh * jnp.mean(doh * oh, axis=-1, keepdims=True))

        units = [(ci, hh) for ci in reversed(range(nc)) for hh in range(2)]
        lsl = [slice(A_KDIM * hh, A_KDIM * (hh + 1)) for hh in range(2)]
        hsl = [slice(A_VDIM * hh, A_VDIM * (hh + 1)) for hh in range(2)]
        rows = [pl.ds(ci * c, c) for ci in range(nc)]
        q_raw = {u: q_ref[rows[u[0]], lsl[u[1]]] for u in units}
        gates = {u: _hgrn_gates(q_raw[u], f_ref[rows[u[0]], lsl[u[1]]], lb_ref[:, lsl[u[1]]], stack) for u in units}
        ch = {u: _hgrn_decays(gates[u]) for u in units}
        v = {u: i_ref[rows[u[0]], hsl[u[1]]].astype(BF16) for u in units}
        do_b = {u: do_scr[rows[u[0]], hsl[u[1]]].astype(BF16) for u in units}
        st = {u: st_ref[u[1], u[0]] for u in units}
        qs_b = {u: ch[u]["qs"].astype(BF16) for u in units}
        a_b = {u: _hgrn_scores(ch[u], masks).astype(BF16) for u in units}
        da = {u: jnp.where(incl, _dot_nt(do_b[u], v[u]), 0.0) for u in units}
        dq_x = {u: _dot(do_b[u], st[u].astype(BF16)) for u in units}
        grow = {u: _dot_tn(do_b[u], ch[u]["q_in"].astype(BF16)) for u in units}
        dstates = [ds_scr[0], ds_scr[1]]
        leaving = {}
        for ci, hh in units:
            leaving[ci, hh] = dstates[hh]
            dstates[hh] = dstates[hh] * jnp.exp(ch[ci, hh]["b_end"]) + grow[ci, hh]
        for hh in range(2):
            ds_scr[hh] = dstates[hh]
        dst_b = {u: leaving[u].astype(BF16) for u in units}
        dv = {u: _dot_tn(a_b[u], do_b[u]) + _dot_nt(ch[u]["kend"].astype(BF16), dst_b[u]) for u in units}
        dk_x = {u: _dot(v[u], dst_b[u]) for u in units}
        dlb_acc = [jnp.zeros((1, A_KDIM), F32), jnp.zeros((1, A_KDIM), F32)]
        for u in units:
            ci, hh = u
            cu = ch[u]
            lbv = lb_ref[:, lsl[hh]]
            dq_i = None
            dk_i = None
            kdk_i = None
            for i, mk in enumerate(masks):
                dam = jnp.where(mk, da[u], 0.0).astype(BF16)
                ki = (cu["k"] * cu["kfac"][i]).astype(BF16)
                pq = _dot(dam, ki)
                pk = _dot_tn(dam, qs_b[u])
                dq_i = pq if dq_i is None else dq_i + pq
                dk_i = cu["kfac"][i] * pk if dk_i is None else dk_i + cu["kfac"][i] * pk
                kdk_i = ki.astype(F32) * pk if kdk_i is None else kdk_i + ki.astype(F32) * pk
            dq = cu["e_sub"] * dq_i + cu["e_b"] * dq_x[u]
            dk = dk_i + cu["e_end"] * dk_x[u]
            kx = cu["kend"] * dk_x[u]
            db = (qs_b[u].astype(F32) * dq_i + cu["q_in"] * dq_x[u]) - (kdk_i + kx)
            later = (jnp.exp(cu["b_end"]) * jnp.sum(leaving[u] * st[u], axis=0, keepdims=True)
                     + jnp.sum(kx, axis=0, keepdims=True))
            dlogf = later + _ones_left(rev_incl, db)
            dfv = jnp.where(cu["f"] > TINY, dlogf / cu["f"], 0.0)
            dq_ref[rows[ci], lsl[hh]] = dq * _dsilu(q_raw[u])
            df_ref[rows[ci], lsl[hh]] = (1.0 - lbv) * cu["sg"] * cu["sgn"] * (dfv - dk)
            dlb_acc[hh] = dlb_acc[hh] + jnp.sum(dfv * (1.0 - cu["sg"]) - dk * cu["sgn"], axis=0, keepdims=True)
            di_ref[rows[ci], hsl[hh]] = dv[u]
        for hh in range(2):
            dlb_ref[:, A_KDIM * hh:A_KDIM * (hh + 1)] += dlb_acc[hh]

    w2 = 2 * A_KDIM
    rev = lambda j: nt - 1 - j
    return pl.pallas_call(
        body, name=name, grid=(A_HEADS // 2, nt),
        in_specs=[pl.BlockSpec((tb, w2), lambda h, j: (rev(j), COL_AQ // w2 + h)),
                  pl.BlockSpec((tb, w2), lambda h, j: (rev(j), COL_AF // w2 + h)),
                  pl.BlockSpec((tb, LANES), lambda h, j: (rev(j), COL_AI // LANES + h)),
                  pl.BlockSpec((tb, LANES), lambda h, j: (rev(j), COL_AG // LANES + h)),
                  pl.BlockSpec((1, w2), lambda h, j: (0, h)),
                  pl.BlockSpec((1, LANES), lambda h, j: (0, 0)),
                  pl.BlockSpec((tb, LANES), lambda h, j: (rev(j), h)),
                  pl.BlockSpec((2, nc, A_VDIM, A_KDIM), lambda h, j: (h, rev(j), 0, 0)),
                  pl.BlockSpec((tb, LANES), lambda h, j: (rev(j), h))],
        out_specs=[pl.BlockSpec((tb, w2), lambda h, j: (rev(j), h)),
                   pl.BlockSpec((tb, w2), lambda h, j: (rev(j), h)),
                   pl.BlockSpec((tb, LANES), lambda h, j: (rev(j), h)),
                   pl.BlockSpec((tb, LANES), lambda h, j: (rev(j), h)),
                   pl.BlockSpec((1, w2), lambda h, j: (0, h)),
                   pl.BlockSpec((1, 1, LANES), lambda h, j: (h, 0, 0))],
        out_shape=[jax.ShapeDtypeStruct((t, A_QK), F32), jax.ShapeDtypeStruct((t, A_QK), F32),
                   jax.ShapeDtypeStruct((t, A_V), F32), jax.ShapeDtypeStruct((t, A_V), F32),
                   jax.ShapeDtypeStruct((1, A_QK), F32), jax.ShapeDtypeStruct((A_HEADS // 2, 1, LANES), F32)],
        scratch_shapes=[pltpu.VMEM((2, A_VDIM, A_KDIM), F32), pltpu.VMEM((tb, LANES), F32)],
        compiler_params=_cparams(("parallel", "arbitrary")),
    )(p, p, p, p, lb, hn2, o_raw, states, dya)


BLK = 128
SCALE = HDIM ** -0.5
SB_CHUNK = 4


def _softplus(z):
    return jnp.maximum(z, 0.0) + jnp.log(1.0 + jnp.exp(-jnp.abs(z)))


def _sb_sum_matrix(keep, with_total=False):
    width = 2 * BLK if with_total else BLK
    sp = lax.broadcasted_iota(jnp.int32, (BLK, width), 0)
    s = lax.broadcasted_iota(jnp.int32, (BLK, width), 1)
    return jnp.where(jnp.logical_or(s >= BLK, keep(sp, s)), 1.0, 0.0).astype(BF16)


def _lanes(col):
    return jnp.broadcast_to(col, (BLK, BLK))


def _sb_fwd(p, kv, name, gather=None):
    t = p.shape[0]
    nq = t // BLK
    nh = B_HEADS // 2
    cw = SB_CHUNK * BLK
    fused = gather is not None
    n = len(gather) if fused else 0

    def body(*refs):
        q_ref, kb, vb = refs[:3]
        o_ref, tot_ref = refs[3 + n:5 + n]
        zbuf, stage, sbuf, abuf = refs[5 + 2 * n:9 + 2 * n]
        hp = pl.program_id(0)
        qi = pl.program_id(1)
        if fused:
            g = _Many(_Gather, refs[3:3 + n], refs[5 + n:5 + 2 * n], *refs[9 + 2 * n:])
            pl.when(jnp.logical_and(hp == 0, qi == 0))(g.start)
            pl.when(jnp.logical_and(hp == nh - 1, qi == 0))(g.forward)

        @pl.when(qi == 0)
        def _():
            abuf[...] = jnp.zeros_like(abuf)

        row = lax.broadcasted_iota(jnp.int32, (BLK, BLK), 0)
        col = lax.broadcasted_iota(jnp.int32, (BLK, BLK), 1)
        sums = _sb_sum_matrix(lambda sp, s: sp >= s, True)
        hsl = [slice(HDIM * h, HDIM * (h + 1)) for h in range(2)]
        nchunk = qi // SB_CHUNK + 1
        for h in range(2):
            zbuf[h] = _dot_nt((q_ref[:, hsl[h]] * SCALE).astype(BF16), kb[:, hsl[h]])

        col_minus_row = col - row

        def causal(j):
            return col_minus_row < (qi - j) * BLK

        def l_pass(c, carry):
            for b in range(SB_CHUNK):
                j = c * SB_CHUNK + b
                off = pl.multiple_of(j * BLK, BLK)
                mask = causal(j)
                for h in range(2):
                    lm = jnp.where(mask, -_softplus(zbuf[h, :, pl.ds(off, BLK)]), 0.0)
                    stage[h, pl.ds(off, BLK), :] = lm.astype(BF16)
            return carry

        lax.fori_loop(0, nchunk, l_pass, 0)

        def sum_pass(c, carry):
            rows = pl.ds(pl.multiple_of(c * cw, cw), cw)
            for h in range(2):
                sbuf[h, rows, :] = _dot(stage[h, rows, :], sums)
            return carry

        lax.fori_loop(0, nchunk, sum_pass, 0)

        def a_pass(it, carry):
            c = nchunk - 1 - it
            runs = list(carry)
            for b in reversed(range(SB_CHUNK)):
                j = c * SB_CHUNK + b
                off = pl.multiple_of(j * BLK, BLK)
                mask = causal(j)
                for h in range(2):
                    s = sbuf[h, pl.ds(off, BLK), :BLK]
                    a = jnp.where(mask, jnp.exp(zbuf[h, :, pl.ds(off, BLK)] + s + runs[h]), 0.0)
                    abuf[h, :, pl.ds(off, BLK)] = a.astype(BF16)
                    runs[h] = runs[h] + sbuf[h, pl.ds(off, BLK), BLK:]
            return tuple(runs)

        zero = jnp.zeros((BLK, BLK), F32)
        runs = lax.fori_loop(0, nchunk, a_pass, (zero, zero))
        for h in range(2):
            tot_ref[:, hsl[h]] = runs[h][:, :HDIM]
            o_ref[:, hsl[h]] = _dot(abuf[h], vb[:, hsl[h]])
        if fused:
            pl.when(jnp.logical_and(hp == nh - 1, qi == nq - 1))(g.finish)

    out_blk = pl.BlockSpec((BLK, LANES), lambda h, i: (i, h))
    hbm = pl.BlockSpec(memory_space=pl.ANY)
    in_specs = [pl.BlockSpec((BLK, LANES), lambda h, i: (i, COL_BQ // LANES + h)),
                pl.BlockSpec((t, LANES), lambda h, i: (0, h)),
                pl.BlockSpec((t, LANES), lambda h, i: (0, B_W // LANES + h))]
    out_shape = [jax.ShapeDtypeStruct((t, B_W), F32)] * 2
    scratch = [pltpu.VMEM((2, BLK, t), F32), pltpu.VMEM((2, t, BLK), BF16),
               pltpu.VMEM((2, t, 2 * BLK), F32), pltpu.VMEM((2, BLK, t), BF16)]
    if fused:
        out_shape = out_shape + _gathered_shapes(gather)
    return pl.pallas_call(
        body, name=name, grid=(nh, nq),
        in_specs=in_specs + [hbm] * n,
        out_specs=[out_blk, out_blk] + [hbm] * n,
        out_shape=out_shape,
        scratch_shapes=scratch + (_comm_sems(n) if fused else []),
        compiler_params=_cparams(("arbitrary", "arbitrary")),
    )(p, kv, kv, *(gather if fused else []))


def _sb_bwd(p, kv, tot, do, name, exchange=None):
    t = p.shape[0]
    nq = t // BLK
    nh = B_HEADS // 2
    cw = SB_CHUNK * BLK
    fused = exchange is not None
    n = len(exchange) if fused else 0

    def body(*refs):
        q_ref, kb, vb, tot_ref, do_ref = refs[:5]
        dq_ref, dk_ref, dv_ref = refs[5 + n:8 + n]
        zbuf, dabuf, lbuf, stage, sbuf, abuf, dzbuf, dkt, dvt = refs[8 + 2 * n:17 + 2 * n]
        hp = pl.program_id(0)
        qi = pl.program_id(1)
        if fused:
            ex = _Many(_Exchange, refs[5:5 + n], refs[8 + n:8 + 2 * n], *refs[17 + 2 * n:])
            pl.when(jnp.logical_and(hp == 0, qi == 0))(ex.start)

        @pl.when(qi == 0)
        def _():
            dkt[...] = jnp.zeros_like(dkt)
            dvt[...] = jnp.zeros_like(dvt)
            dzbuf[...] = jnp.zeros_like(dzbuf)
            abuf[...] = jnp.zeros_like(abuf)

        row = lax.broadcasted_iota(jnp.int32, (BLK, BLK), 0)
        col = lax.broadcasted_iota(jnp.int32, (BLK, BLK), 1)
        sums = _sb_sum_matrix(lambda sp, s: sp <= s)
        hsl = [slice(HDIM * h, HDIM * (h + 1)) for h in range(2)]
        dob = [do_ref[:, hsl[h]].astype(BF16) for h in range(2)]
        total =[jnp.concatenate([tot_ref[:, hsl[h]], tot_ref[:, hsl[h]]], axis=1) for h in range(2)]
        nchunk = qi // SB_CHUNK + 1
        for h in range(2):
            zbuf[h] = _dot_nt((q_ref[:, hsl[h]] * SCALE).astype(BF16), kb[:, hsl[h]])
            dabuf[h] = _dot_nt(dob[h], vb[:, hsl[h]])

        col_minus_row = col - row

        def causal(j):
            return col_minus_row < (qi - j) * BLK

        def blocks(c):
            for b in range(SB_CHUNK):
                j = c * SB_CHUNK + b
                yield j, pl.ds(pl.multiple_of(j * BLK, BLK), BLK)

        def l_pass(c, carry):
            for j, blk_ in blocks(c):
                mask = causal(j)
                for h in range(2):
                    lm = jnp.where(mask, -_softplus(zbuf[h, :, blk_]), 0.0)
                    lbuf[h, :, blk_] = lm
                    stage[h, blk_, :] = lm.astype(BF16)
            return carry

        lax.fori_loop(0, nchunk, l_pass, 0)

        def sum_pass():
            def run_(c, carry):
                rows = pl.ds(pl.multiple_of(c * cw, cw), cw)
                for h in range(2):
                    sbuf[h, rows, :] = _dot(stage[h, rows, :], sums)
                return carry
            lax.fori_loop(0, nchunk, run_, 0)

        sum_pass()

        def g_pass(c, carry):
            runs = list(carry)
            for j, blk_ in blocks(c):
                mask = causal(j)
                for h in range(2):
                    upto = sbuf[h, blk_, :]
                    log_a = zbuf[h, :, blk_] + lbuf[h, :, blk_] + (total[h] - runs[h] - upto)
                    a = jnp.where(mask, jnp.exp(log_a), 0.0)
                    abuf[h, :, blk_] = a.astype(BF16)
                    g = a * dabuf[h, :, blk_]
                    dabuf[h, :, blk_] = g
                    stage[h, blk_, :] = g.astype(BF16)
                    runs[h] = runs[h] + _lanes(upto[:, BLK - 1:BLK])
            return tuple(runs)

        zero = jnp.zeros((BLK, BLK), F32)
        lax.fori_loop(0, nchunk, g_pass, (zero, zero))
        sum_pass()

        def dz_pass(c, carry):
            runs = list(carry)
            for j, blk_ in blocks(c):
                mask = causal(j)
                for h in range(2):
                    lm = lbuf[h, :, blk_]
                    g = dabuf[h, :, blk_]
                    upto = sbuf[h, blk_, :]
                    before = runs[h] + upto - g
                    dz = jnp.where(mask, g * jnp.exp(lm) - jnp.exp(zbuf[h, :, blk_] + lm) * before, 0.0)
                    dzbuf[h, :, blk_] = (dz * SCALE).astype(BF16)
                    runs[h] = runs[h] + _lanes(upto[:, BLK - 1:BLK])
            return tuple(runs)

        lax.fori_loop(0, nchunk, dz_pass, (zero, zero))
        for h in range(2):
            dq_ref[:, hsl[h]] = _dot(dzbuf[h], kb[:, hsl[h]])
        q_t = q_ref[...].T.astype(BF16)
        do_t = do_ref[...].T.astype(BF16)
        for h in range(2):
            dkt[hsl[h], :] += _dot(q_t[hsl[h], :], dzbuf[h])
            dvt[hsl[h], :] += _dot(do_t[hsl[h], :], abuf[h])

        @pl.when(qi == nq - 1)
        def _():
            dk_ref[...] = dkt[...].T
            dv_ref[...] = dvt[...].T

        if fused:
            pl.when(jnp.logical_and(hp == nh - 1, qi == nq - 1))(ex.finish)

    blk = lambda h, i: (i, h)
    whole = lambda h, i: (0, h)
    hbm = pl.BlockSpec(memory_space=pl.ANY)
    in_specs = [pl.BlockSpec((BLK, LANES), lambda h, i: (i, COL_BQ // LANES + h)),
                pl.BlockSpec((t, LANES), lambda h, i: (0, h)),
                pl.BlockSpec((t, LANES), lambda h, i: (0, B_W // LANES + h)),
                pl.BlockSpec((BLK, LANES), blk), pl.BlockSpec((BLK, LANES), blk)]
    out_specs = [pl.BlockSpec((BLK, LANES), blk), pl.BlockSpec((t, LANES), whole), pl.BlockSpec((t, LANES), whole)]
    out_shape = [jax.ShapeDtypeStruct((t, B_W), F32)] * 3
    scratch = [pltpu.VMEM((2, BLK, t), F32), pltpu.VMEM((2, BLK, t), F32), pltpu.VMEM((2, BLK, t), F32),
               pltpu.VMEM((2, t, BLK), BF16), pltpu.VMEM((2, t, BLK), F32), pltpu.VMEM((2, BLK, t), BF16),
               pltpu.VMEM((2, BLK, t), BF16), pltpu.VMEM((LANES, t), F32), pltpu.VMEM((LANES, t), F32)]
    if fused:
        out_shape = out_shape + [jax.ShapeDtypeStruct(e.shape, e.dtype) for e in exchange]
    return pl.pallas_call(
        body, name=name, grid=(nh, nq),
        in_specs=in_specs + [hbm] * n,
        out_specs=out_specs + [hbm] * n,
        out_shape=out_shape,
        scratch_shapes=scratch + (_comm_sems(n) if fused else []),
        compiler_params=_cparams(("arbitrary", "arbitrary")),
    )(p, kv, kv, tot, do, *(exchange if fused else []))


def _alibi_slopes(n):
    def pow2(m):
        start = 2.0 ** (-8.0 / m)
        return [start ** (i + 1) for i in range(m)]
    if math.log2(n).is_integer():
        s = pow2(n)
    else:
        c = 2 ** int(math.floor(math.log2(n)))
        s = pow2(c) + pow2(2 * c)[0::2][: n - c]
    return sorted(s, reverse=True)


def _dil_scores(qh, kh, sl, prev, exists=None):
    row = lax.broadcasted_iota(jnp.int32, (BLK, BLK), 0)
    col = lax.broadcasted_iota(jnp.int32, (BLK, BLK), 1)
    dist = row - col + (BLK if prev else 0)
    if prev:
        valid = (col - row) >= jnp.where(exists, 0, 2 * BLK)
    else:
        valid = col <= row
    s = _dot_nt(qh, kh) - sl * dist.astype(F32)
    return s, valid


DIL_UNITS = 2


def _dil_plan(r):
    per_trip = min(r, DIL_UNITS)
    return per_trip, DIL_UNITS // per_trip


def _dil_rows(b, rho, r):
    return pl.ds(b * BLK * r + rho, BLK, stride=r) if r > 1 else pl.ds(b * BLK, BLK)


def _dil_fwd(p, gi, name):
    t = p.shape[0]
    _, r = C_GROUPS[gi]
    per_trip, nsub = _dil_plan(r)
    sbr = BLK * r * nsub
    nsb = t // sbr
    slope_cols = _slope_cols(gi)

    def body(q_ref, kc_ref, kp_ref, vc_ref, vp_ref, sl_ref, o_ref, lse_ref):
        i = pl.program_id(1)

        hsl = [slice(HDIM * h, HDIM * (h + 1)) for h in range(2)]
        sl = [sl_ref[:, HDIM * h:HDIM * h + 1] for h in range(2)]

        def residues(it, carry):
            pairs = [(b, dr) for b in range(nsub) for dr in range(per_trip)]
            units = [(pr, h) for pr in pairs for h in range(2)]
            rows = {(b, dr): _dil_rows(b, it * per_trip + dr, r) for b, dr in pairs}
            blocks, prev_exists = {}, {}
            for b, dr in pairs:
                rw = rows[b, dr]
                if b == 0:
                    before = _dil_rows(nsub - 1, it * per_trip + dr, r)
                    kp, vp, prev_exists[b, dr] = kp_ref[before, :], vp_ref[before, :], i > 0
                else:
                    before = rows[b - 1, dr]
                    kp, vp, prev_exists[b, dr] = kc_ref[before, :], vc_ref[before, :], True
                blocks[b, dr] = [q_ref[rw, :], kc_ref[rw, :], kp, vc_ref[rw, :], vp]
            qh = {u: (blocks[u[0]][0][:, hsl[u[1]]] * SCALE).astype(BF16) for u in units}
            sc = {u: _dil_scores(qh[u], blocks[u[0]][1][:, hsl[u[1]]].astype(BF16), sl[u[1]], False) for u in units}
            sp = {u: _dil_scores(qh[u], blocks[u[0]][2][:, hsl[u[1]]].astype(BF16), sl[u[1]], True, prev_exists[u[0]])
                  for u in units}
            pc, pp, den, lse = {}, {}, {}, {}
            for u in units:
                s_c = jnp.where(sc[u][1], sc[u][0], NEG_BIG)
                s_p = jnp.where(sp[u][1], sp[u][0], NEG_BIG)
                m = jnp.maximum(jnp.max(s_c, axis=1, keepdims=True), jnp.max(s_p, axis=1, keepdims=True))
                pc[u] = jnp.exp(s_c - m)
                pp[u] = jnp.exp(s_p - m)
                den[u] = jnp.sum(pc[u], axis=1, keepdims=True) + jnp.sum(pp[u], axis=1, keepdims=True)
                lse[u] = jnp.broadcast_to(m + jnp.log(den[u]), (BLK, HDIM))
            o = {u: (_dot(pc[u].astype(BF16), blocks[u[0]][3][:, hsl[u[1]]].astype(BF16))
                     + _dot(pp[u].astype(BF16), blocks[u[0]][4][:, hsl[u[1]]].astype(BF16))) / den[u] for u in units}
            for pr in pairs:
                o_ref[rows[pr], :] = jnp.concatenate([o[pr, 0], o[pr, 1]], axis=1)
                lse_ref[rows[pr], :] = jnp.concatenate([lse[pr, 0], lse[pr, 1]], axis=1)
            return carry

        lax.fori_loop(0, r // per_trip, residues, 0)

    def at(col0, pick):
        return pl.BlockSpec((sbr, LANES), lambda c, i: (pick(i), col0 // LANES + c))

    cur = lambda i: i
    prv = lambda i: jnp.maximum(i - 1, 0)
    cq, ck, cv = COL_CQ + gi * C_OUT, COL_CK + gi * C_OUT, COL_CV + gi * C_OUT
    out = pl.BlockSpec((sbr, LANES), lambda c, i: (i, c))
    return pl.pallas_call(
        body, name=name, grid=(C_OUT // LANES, nsb),
        in_specs=[at(cq, cur), at(ck, cur), at(ck, prv), at(cv, cur), at(cv, prv),
                  pl.BlockSpec((1, LANES), lambda c, i: (0, c))],
        out_specs=[out, out], out_shape=[jax.ShapeDtypeStruct((t, C_OUT), F32)] * 2,
        compiler_params=_cparams(("parallel", "parallel")),
    )(p, p, p, p, p, slope_cols)


def _dil_bwd(p, do, o, lse, gi, name):
    t = p.shape[0]
    _, r = C_GROUPS[gi]
    per_trip, nsub = _dil_plan(r)
    sbr = BLK * r * nsub
    nsb = t // sbr
    slope_cols = _slope_cols(gi)

    def body(q_ref, qn_ref, kc_ref, kp_ref, vc_ref, vp_ref, do_ref, don_ref, o_ref, on_ref, l_ref, ln_ref, sl_ref,
             dq_ref, dk_ref, dv_ref):
        i = pl.program_id(1)

        hsl = [slice(HDIM * h, HDIM * (h + 1)) for h in range(2)]
        sl = [sl_ref[:, HDIM * h:HDIM * h + 1] for h in range(2)]

        def residues(it, carry):
            pairs = [(b, dr) for b in range(nsub) for dr in range(per_trip)]
            units = [(pr, h) for pr in pairs for h in range(2)]
            rows = {(b, dr): _dil_rows(b, it * per_trip + dr, r) for b, dr in pairs}
            blocks, has_prev, has_next = {}, {}, {}
            for b, dr in pairs:
                rw = rows[b, dr]
                if b == 0:
                    before = _dil_rows(nsub - 1, it * per_trip + dr, r)
                    kp, vp, has_prev[b, dr] = kp_ref[before, :], vp_ref[before, :], i > 0
                else:
                    kp, vp, has_prev[b, dr] = kc_ref[rows[b - 1, dr], :], vc_ref[rows[b - 1, dr], :], True
                if b == nsub - 1:
                    after = _dil_rows(0, it * per_trip + dr, r)
                    nxt = [ref[after, :] for ref in (qn_ref, don_ref, on_ref, ln_ref)]
                    has_next[b, dr] = i < nsb - 1
                else:
                    nxt = [ref[rows[b + 1, dr], :] for ref in (q_ref, do_ref, o_ref, l_ref)]
                    has_next[b, dr] = True
                blocks[b, dr] = [q_ref[rw, :], nxt[0], kc_ref[rw, :], kp, vc_ref[rw, :], vp, do_ref[rw, :], nxt[1],
                                 o_ref[rw, :], nxt[2], l_ref[rw, :], nxt[3]]
            part = lambda u, k: blocks[u[0]][k][:, hsl[u[1]]]
            qb = {u: part(u, 0).astype(BF16) for u in units}
            qnb = {u: part(u, 1).astype(BF16) for u in units}
            qh = {u: (part(u, 0) * SCALE).astype(BF16) for u in units}
            qnh = {u: (part(u, 1) * SCALE).astype(BF16) for u in units}
            kc = {u: part(u, 2).astype(BF16) for u in units}
            kp = {u: part(u, 3).astype(BF16) for u in units}
            vc = {u: part(u, 4).astype(BF16) for u in units}
            vp = {u: part(u, 5).astype(BF16) for u in units}
            dob = {u: part(u, 6).astype(BF16) for u in units}
            donb = {u: part(u, 7).astype(BF16) for u in units}
            delta = {u: jnp.sum(part(u, 6) * part(u, 8), axis=1, keepdims=True) for u in units}
            deltan = {u: jnp.sum(part(u, 7) * part(u, 9), axis=1, keepdims=True) for u in units}
            lse_c = {u: part(u, 10)[:, :1] for u in units}
            lse_n = {u: part(u, 11)[:, :1] for u in units}
            s_cc = {u: _dil_scores(qh[u], kc[u], sl[u[1]], False) for u in units}
            s_cp = {u: _dil_scores(qh[u], kp[u], sl[u[1]], True, has_prev[u[0]]) for u in units}
            s_nc = {u: _dil_scores(qnh[u], kc[u], sl[u[1]], True, has_next[u[0]]) for u in units}
            da_cc = {u: _dot_nt(dob[u], vc[u]) for u in units}
            da_cp = {u: _dot_nt(dob[u], vp[u]) for u in units}
            da_nc = {u: _dot_nt(donb[u], vc[u]) for u in units}

            def prob(s_ok, lse_col):
                s, ok = s_ok
                return jnp.where(ok, jnp.exp(jnp.where(ok, s, NEG_BIG) - lse_col), 0.0)

            p_cc = {u: prob(s_cc[u], lse_c[u]) for u in units}
            p_cp = {u: prob(s_cp[u], lse_c[u]) for u in units}
            p_nc = {u: prob(s_nc[u], lse_n[u]) for u in units}
            ds_cc = {u: (p_cc[u] * (da_cc[u] - delta[u]) * SCALE).astype(BF16) for u in units}
            ds_cp = {u: (p_cp[u] * (da_cp[u] - delta[u]) * SCALE).astype(BF16) for u in units}
            ds_nc = {u: (p_nc[u] * (da_nc[u] - deltan[u]) * SCALE).astype(BF16) for u in units}
            dq = {u: _dot(ds_cc[u], kc[u]) + _dot(ds_cp[u], kp[u]) for u in units}
            dk = {u: _dot_tn(ds_cc[u], qb[u]) + _dot_tn(ds_nc[u], qnb[u]) for u in units}
            dv = {u: _dot_tn(p_cc[u].astype(BF16), dob[u]) + _dot_tn(p_nc[u].astype(BF16), donb[u]) for u in units}
            for pr in pairs:
                dq_ref[rows[pr], :] = jnp.concatenate([dq[pr, 0], dq[pr, 1]], axis=1)
                dk_ref[rows[pr], :] = jnp.concatenate([dk[pr, 0], dk[pr, 1]], axis=1)
                dv_ref[rows[pr], :] = jnp.concatenate([dv[pr, 0], dv[pr, 1]], axis=1)
            return carry

        lax.fori_loop(0, r // per_trip, residues, 0)

    def at(col0, pick):
        return pl.BlockSpec((sbr, LANES), lambda c, i: (pick(i), col0 // LANES + c))

    cur = lambda i: i
    prv = lambda i: jnp.maximum(i - 1, 0)
    nxt = lambda i: jnp.minimum(i + 1, nsb - 1)
    cq, ck, cv = COL_CQ + gi * C_OUT, COL_CK + gi * C_OUT, COL_CV + gi * C_OUT
    return pl.pallas_call(
        body, name=name, grid=(C_OUT // LANES, nsb),
        in_specs=[at(cq, cur), at(cq, nxt), at(ck, cur), at(ck, prv), at(cv, cur), at(cv, prv),
                  at(0, cur), at(0, nxt), at(0, cur), at(0, nxt), at(0, cur), at(0, nxt),
                  pl.BlockSpec((1, LANES), lambda c, i: (0, c))],
        out_specs=[at(0, cur)] * 3, out_shape=[jax.ShapeDtypeStruct((t, C_OUT), F32)] * 3,
        compiler_params=_cparams(("parallel", "parallel")),
    )(p, p, p, p, p, p, do, do, o, o, lse, lse, slope_cols)


def _dil_merge(os_, ls_, name):
    t, w = os_[0].shape
    tr = _rows(t)

    def body(o0, o1, o2, l0, l1, l2, y_ref, lse_ref):
        a, b, c = l0[...], l1[...], l2[...]
        m = jnp.maximum(jnp.maximum(a, b), c)
        ea, eb, ec = jnp.exp(a - m), jnp.exp(b - m), jnp.exp(c - m)
        den = ea + eb + ec
        y_ref[...] = (ea * o0[...] + eb * o1[...] + ec * o2[...]) / den
        lse_ref[...] = m + jnp.log(den)

    row = pl.BlockSpec((tr, w), lambda i: (i, 0))
    return pl.pallas_call(
        body, name=name, grid=(t // tr,), in_specs=[row] * 6, out_specs=[row, row],
        out_shape=[jax.ShapeDtypeStruct((t, w), F32)] * 2, compiler_params=_cparams(("parallel",)),
    )(*os_, *ls_)


def _gate_fwd(ys, gl, ws, name):
    t = gl.shape[0]
    d = gl.shape[1] // N_BRANCH
    tr = _rows(t)

    def body(ya, yb, yc, gl_ref, wa, wb, wc, m_ref):
        acc = None
        for i, (y, w) in enumerate(((ya, wa), (yb, wb), (yc, wc))):
            z = _dot(y[...].astype(BF16), w[...])
            term = jax.nn.sigmoid(gl_ref[:, i * d:(i + 1) * d]) * z
            acc = term if acc is None else acc + term
        m_ref[...] = acc.astype(m_ref.dtype)

    rows = [pl.BlockSpec((tr, y.shape[1]), lambda i: (i, 0)) for y in ys]
    wsp = [pl.BlockSpec(w.shape, lambda i: (0, 0)) for w in ws]
    return pl.pallas_call(
        body, name=name, grid=(t // tr,),
        in_specs=rows + [pl.BlockSpec((tr, N_BRANCH * d), lambda i: (i, 0))] + wsp,
        out_specs=pl.BlockSpec((tr, d), lambda i: (i, 0)), out_shape=jax.ShapeDtypeStruct((t, d), BF16),
        compiler_params=_cparams(("parallel",)),
    )(*ys, gl, *ws)


def _gate_bwd(dm, ys, gl, ws, name):
    t = gl.shape[0]
    d = gl.shape[1] // N_BRANCH
    tr = _rows(t)

    def body(dm_ref, ya, yb, yc, gl_ref, wa, wb, wc, dya, dyb, dyc, dgl_ref, dwa, dwb, dwc):
        step = pl.program_id(0)
        dmv = dm_ref[...].astype(F32)
        for i, (y, w, dy, dw) in enumerate(((ya, wa, dya, dwa), (yb, wb, dyb, dwb), (yc, wc, dyc, dwc))):
            yb16 = y[...].astype(BF16)
            z = _dot(yb16, w[...])
            sg = jax.nn.sigmoid(gl_ref[:, i * d:(i + 1) * d])
            dgl_ref[:, i * d:(i + 1) * d] = dmv * z * sg * (1.0 - sg)
            e = (dmv * sg).astype(BF16)
            dy[...] = _dot_nt(e, w[...])
            contrib = _dot_tn(yb16, e)

            @pl.when(step == 0)
            def _(dw=dw, contrib=contrib):
                dw[...] = contrib

            @pl.when(step > 0)
            def _(dw=dw, contrib=contrib):
                dw[...] += contrib

    rows = [pl.BlockSpec((tr, y.shape[1]), lambda i: (i, 0)) for y in ys]
    wsp = [pl.BlockSpec(w.shape, lambda i: (0, 0)) for w in ws]
    gsp = pl.BlockSpec((tr, N_BRANCH * d), lambda i: (i, 0))
    return pl.pallas_call(
        body, name=name, grid=(t // tr,),
        in_specs=[pl.BlockSpec((tr, d), lambda i: (i, 0))] + rows + [gsp] + wsp,
        out_specs=rows + [gsp] + wsp,
        out_shape=[jax.ShapeDtypeStruct(y.shape, F32) for y in ys] + [jax.ShapeDtypeStruct(gl.shape, F32)]
        + [jax.ShapeDtypeStruct(w.shape, F32) for w in ws],
        compiler_params=_cparams(("arbitrary",)),
    )(dm, *ys, gl, *ws)


def _adamw(w, m, v, gparts, name):
    depth = len(gparts)
    n, r, c = gparts[0].shape
    br = max(b for b in range(8, min(r, LANES) + 1, 8) if r % b == 0) if r % 8 == 0 else r
    nb = r // br
    c1 = 1.0 - ADAM_B1 ** ADAM_STEP
    c2 = 1.0 - ADAM_B2 ** ADAM_STEP

    def body(w_ref, m_ref, v_ref, *rest):
        g_refs, (go_ref, d_ref, mo_ref, vo_ref) = rest[:depth], rest[depth:]
        li = pl.program_id(0)

        def update(g_ref):
            g = g_ref[0].astype(F32)
            for i in range(1, n):
                g = g + g_ref[i].astype(F32)
            mn = ADAM_B1 * m_ref[...] + (1.0 - ADAM_B1) * g
            vn = ADAM_B2 * v_ref[...] + (1.0 - ADAM_B2) * (g * g)
            go_ref[...] = g
            mo_ref[...] = mn
            vo_ref[...] = vn
            d_ref[...] = -ADAM_LR * ((mn / c1) / (jnp.sqrt(vn / c2) + ADAM_EPS) + ADAM_WD * w_ref[...])

        for l in range(depth):
            pl.when(li == l)(functools.partial(update, g_refs[l]))

    def g_spec(l):
        return pl.BlockSpec((n, br, c), lambda li, i: (0, jnp.where(li == l, i, jnp.where(li < l, 0, nb - 1)), 0))

    blk = pl.BlockSpec((br, c), lambda li, i: (li * nb + i, 0))
    return pl.pallas_call(
        body, name=name, grid=(depth, nb),
        in_specs=[blk, blk, blk] + [g_spec(l) for l in range(depth)],
        out_specs=[blk] * 4, out_shape=[jax.ShapeDtypeStruct((depth * r, c), F32)] * 4,
        compiler_params=_cparams(("arbitrary", "arbitrary")),
    )(w, m, v, *gparts)


def _my_coords():
    return lax.axis_index("x"), lax.axis_index("y"), lax.axis_index("c")


COMM_SEMS = [pltpu.SemaphoreType.DMA((7,)), pltpu.SemaphoreType.DMA((7,)), pltpu.SemaphoreType.DMA]


class _Gather:
    def __init__(self, x_ref, out_ref, send_sems, recv_sems, local_sem):
        self.x_ref, self.out_ref = x_ref, out_ref
        self.send_sems, self.recv_sems, self.local_sem = send_sems, recv_sems, local_sem
        self.m_per = x_ref.shape[0]
        x, y, c = _my_coords()
        self.c = c
        self.me, self.sibling = (x, y, c), (x, y, 1 - c)
        self.chips = [(1 - x, y), (x, 1 - y), (1 - x, 1 - y)]

    def rows(self, px, py, pc):
        return self.out_ref.at[pl.ds((4 * px + 2 * py + pc) * self.m_per, self.m_per), :]

    def copy(self, k, block, to, src=None):
        return pltpu.make_async_remote_copy(
            src_ref=self.rows(*block) if src is None else src, dst_ref=self.rows(*block),
            send_sem=self.send_sems.at[k], recv_sem=self.recv_sems.at[k], device_id=to, device_id_type=MESH)

    def mine(self):
        return pltpu.make_async_copy(self.x_ref, self.rows(*self.me), self.local_sem)

    def first(self):
        out = [self.copy(0, self.me, self.sibling, src=self.x_ref)]
        return out + [self.copy(1 + j, self.me, (*chip, self.c), src=self.x_ref) for j, chip in enumerate(self.chips)]

    def passed(self):
        return [self.copy(4 + j, (*chip, self.c), self.sibling) for j, chip in enumerate(self.chips)]

    def start(self):
        self.mine().start()
        for cp in self.first():
            cp.start()

    def forward(self):
        passed = self.passed()
        for j, chip in enumerate(self.chips):
            self.copy(1 + j, (*chip, self.c), self.me).wait_recv()
            passed[j].start()

    def finish(self):
        self.copy(0, self.sibling, self.me).wait_recv()
        for j, chip in enumerate(self.chips):
            self.copy(4 + j, (*chip, 1 - self.c), self.me).wait_recv()
        for cp in self.first() + self.passed():
            cp.wait_send()
        self.mine().wait()


class _Exchange:
    def __init__(self, send_ref, recv_ref, send_sems, recv_sems, local_sem):
        self.send_ref, self.recv_ref = send_ref, recv_ref
        self.send_sems, self.recv_sems, self.local_sem = send_sems, recv_sems, local_sem
        x, y, c = _my_coords()
        self.me = 4 * x + 2 * y + c
        self.peers = []
        for k in range(1, N_DEV):
            px = 1 - x if k & 4 else x
            py = 1 - y if k & 2 else y
            pc = 1 - c if k & 1 else c
            self.peers.append((4 * px + 2 * py + pc, (px, py, pc)))

    def mine(self):
        return pltpu.make_async_copy(self.send_ref.at[self.me], self.recv_ref.at[self.me], self.local_sem)

    def copy(self, k, src_slot, dst_slot):
        return pltpu.make_async_remote_copy(
            src_ref=self.send_ref.at[src_slot], dst_ref=self.recv_ref.at[dst_slot],
            send_sem=self.send_sems.at[k], recv_sem=self.recv_sems.at[k],
            device_id=self.peers[k][1], device_id_type=MESH)

    def start(self):
        self.mine().start()
        for k, (peer, _) in enumerate(self.peers):
            self.copy(k, peer, self.me).start()

    def finish(self):
        for k, (peer, _) in enumerate(self.peers):
            self.copy(k, peer, self.me).wait_send()
            self.copy(k, self.me, peer).wait_recv()
        self.mine().wait()


def _all_gather(x_shard, in_vmem, with_sum, name):
    m_per, n = x_shard.shape

    def body(x_ref, out_ref, *rest):
        if with_sum:
            sum_ref, send_sems, recv_sems, local_sem = rest
        else:
            send_sems, recv_sems, local_sem = rest
        g = _Gather(x_ref, out_ref, send_sems, recv_sems, local_sem)
        g.start()
        g.forward()
        g.finish()
        if with_sum:
            acc = out_ref[pl.ds(0, m_per), :]
            for d in range(1, N_DEV):
                acc = acc + out_ref[pl.ds(d * m_per, m_per), :]
            sum_ref[...] = acc

    space = pltpu.VMEM if in_vmem else pl.ANY
    out_shape = [jax.ShapeDtypeStruct((N_DEV * m_per, n), x_shard.dtype)]
    out_specs = [pl.BlockSpec(memory_space=space)]
    if with_sum:
        out_shape.append(jax.ShapeDtypeStruct((m_per, n), x_shard.dtype))
        out_specs.append(pl.BlockSpec(memory_space=pltpu.VMEM))
    res = pl.pallas_call(
        body, name=name, out_shape=out_shape, in_specs=[pl.BlockSpec(memory_space=space)], out_specs=out_specs,
        scratch_shapes=COMM_SEMS, compiler_params=pltpu.CompilerParams(vmem_limit_bytes=VMEM_LIMIT),
    )(x_shard)
    return res if with_sum else res[0]


def _comm_sems(n):
    return [pltpu.SemaphoreType.DMA((n, 7)), pltpu.SemaphoreType.DMA((n, 7)), pltpu.SemaphoreType.DMA((n,))]


class _Many:
    def __init__(self, kind, ins, outs, send_sems, recv_sems, local_sems):
        self.parts = [kind(i, o, send_sems.at[b], recv_sems.at[b], local_sems.at[b])
                      for b, (i, o) in enumerate(zip(ins, outs))]

    def start(self):
        for part in self.parts:
            part.start()

    def forward(self):
        for part in self.parts:
            part.forward()

    def finish(self):
        for part in self.parts:
            part.finish()


def _gathered_shapes(shards):
    return [jax.ShapeDtypeStruct((N_DEV * s.shape[0],) + s.shape[1:], s.dtype) for s in shards]


def _all_gather_many(shards, name):
    n = len(shards)

    def body(*refs):
        g = _Many(_Gather, refs[:n], refs[n:2 * n], *refs[2 * n:])
        g.start()
        g.forward()
        g.finish()

    hbm = pl.BlockSpec(memory_space=pl.ANY)
    return pl.pallas_call(body, name=name, out_shape=_gathered_shapes(shards), in_specs=[hbm] * n,
                          out_specs=[hbm] * n, scratch_shapes=_comm_sems(n))(*shards)


def _all_to_all_many(sends, name):
    n = len(sends)

    def body(*refs):
        ex = _Many(_Exchange, refs[:n], refs[n:2 * n], *refs[2 * n:])
        ex.start()
        ex.finish()

    hbm = pl.BlockSpec(memory_space=pl.ANY)
    return pl.pallas_call(body, name=name, out_shape=[jax.ShapeDtypeStruct(s.shape, s.dtype) for s in sends],
                          in_specs=[hbm] * n, out_specs=[hbm] * n, scratch_shapes=_comm_sems(n))(*sends)


def _row(v):
    return v.reshape(1, -1)


def _ffn_fwd(x, w_in, w_out, g_pre, g_post, m, res_w, tag):
    shift, scale, gate = m[0], m[1], m[2]
    mpre = _row(g_pre * (1.0 + scale))
    mpost = _row(res_w * gate * g_post)
    h = _rms_fwd(x, mpre, _row(shift), None, BF16, tag + "_pre")
    ua, ub, s = _ffn_in_swiglu(h, w_in, tag + "_in")
    y = _matmul(s, w_out, name=tag + "_out")
    x_new = _rms_fwd(y, mpost, jnp.zeros_like(mpost), x, F32, tag + "_post")
    return x_new, (x, h, ua, ub, s, y, mpre, mpost)


def _sub_bwd_post(dx_new, y, mpost, g_post, gate, res_w, tag):
    dy, c1, _ = _rms_bwd(dx_new, y, mpost, None, BF16, tag + "_post_bwd")
    c1 = c1[0]
    return dy, c1 * res_w * g_post, c1 * res_w * gate


def _sub_bwd_pre(dh, x, mpre, dx_new, g_pre, scale, tag):
    dx, c2, c3 = _rms_bwd(dh, x, mpre, dx_new, F32, tag + "_pre_bwd")
    c2, c3 = c2[0], c3[0]
    return dx, c3, c2 * g_pre, c2 * (1.0 + scale)


def _ffn_bwd(dx_new, saved, w_in, w_out, g_pre, g_post, m, res_w, tag):
    x, h, ua, ub, s, y, mpre, mpost = saved
    scale, gate = m[1], m[2]
    dy, dgate, dg_post = _sub_bwd_post(dx_new, y, mpost, g_post, gate, res_w, tag)
    ds = _matmul(dy, w_out, tb=True, out_dtype=BF16, name=tag + "_out_dx")
    dw_out = _matmul(s, dy, ta=True, out_dtype=BF16, name=tag + "_out_dw")
    du = _swiglu_bwd(ua, ub, ds, tag + "_act_bwd")
    dh = _matmul(du, w_in, tb=True, name=tag + "_in_dx")
    dw_in = _matmul(h, du, ta=True, out_dtype=BF16, name=tag + "_in_dw")
    dx, dshift, dscale, dg_pre = _sub_bwd_pre(dh, x, mpre, dx_new, g_pre, scale, tag)
    return dx, dw_in, dw_out, jnp.stack([dshift, dscale, dgate]), dg_pre, dg_post


def _slope_cols(gi):
    _, r = C_GROUPS[gi]
    sl = jnp.asarray(_alibi_slopes(C_HEADS)[gi * C_HPG:(gi + 1) * C_HPG], F32) * float(r)
    return jnp.repeat(sl, HDIM).reshape(1, C_OUT)


def _mix_fwd(x, w, g_pre, g_post, m, lb, hn, tag, gather=None):
    t, d = x.shape
    shift, scale, gate = m[0], m[1], m[2]
    mpre = _row(g_pre * (1.0 + scale))
    mpost = _row(gate * g_post)
    h = _rms_fwd(x, mpre, _row(shift), None, BF16, tag + "_pre")
    p = _matmul(h, w["w_in"], name=tag + "_in")
    hn2 = _row(jnp.tile(hn, 2))
    ya, oa, states = _hgrn_fwd(p, _row(lb), hn2, tag + "_hgrn")
    kv = p[:, COL_BK:COL_CQ].astype(BF16)
    if gather is None:
        (yb, sb_tot), gathered = _sb_fwd(p, kv, tag + "_sb"), None
    else:
        res = _sb_fwd(p, kv, tag + "_sb_gather", gather)
        yb, sb_tot, gathered = res[0], res[1], list(res[2:])
    og, lg = zip(*[_dil_fwd(p, gi, tag + "_dil%d" % gi) for gi in range(len(C_GROUPS))])
    yc, lse_c = _dil_merge(og, lg, tag + "_dil_merge")
    gl = p[:, COL_GATE:]
    ws = (w["w_branch_a"], w["w_branch_b"], w["w_branch_c"])
    merged = _gate_fwd((ya, yb, yc), gl, ws, tag + "_gate")
    y = _matmul(merged, w["w_out"], name=tag + "_out")
    x_new = _rms_fwd(y, mpost, jnp.zeros_like(mpost), x, F32, tag + "_post")
    return x_new, (x, h, p, hn2, ya, oa, states, yb, kv, sb_tot, yc, lse_c, gl, merged, y, mpre, mpost), gathered


def _mix_bwd(dx_new, saved, w, g_pre, g_post, m, lb, tag, exchange=None):
    x, h, p, hn2, ya, oa, states, yb, kv, sb_tot, yc, lse_c, gl, merged, y, mpre, mpost = saved
    t = x.shape[0]
    scale, gate = m[1], m[2]
    dy, dgate, dg_post = _sub_bwd_post(dx_new, y, mpost, g_post, gate, 1.0, tag)
    dmerged = _matmul(dy, w["w_out"], tb=True, out_dtype=BF16, name=tag + "_out_dx")
    dw_out = _matmul(merged, dy, ta=True, out_dtype=BF16, name=tag + "_out_dw")
    ws = (w["w_branch_a"], w["w_branch_b"], w["w_branch_c"])
    dya, dyb, dyc, dgl, dwa, dwb, dwc = _gate_bwd(dmerged, (ya, yb, yc), gl, ws, tag + "_gate_bwd")
    dqa, dfa, dia, dga, dlb, dhn = _hgrn_bwd(p, _row(lb), hn2, oa, states, dya, tag + "_hgrn_bwd")
    if exchange is None:
        (dbq, dbk, dbv), received = _sb_bwd(p, kv, sb_tot, dyb, tag + "_sb_bwd"), None
    else:
        res = _sb_bwd(p, kv, sb_tot, dyb, tag + "_sb_bwd_exchange", exchange)
        dbq, dbk, dbv, received = res[0], res[1], res[2], list(res[3:])
    dcq, dck, dcv = zip(*[_dil_bwd(p, dyc, yc, lse_c, gi, tag + "_dil%d_bwd" % gi) for gi in range(len(C_GROUPS))])
    dp = jnp.concatenate([dqa, dfa, dia, dga, dbq, dbk, dbv, *dcq, *dck, *dcv, dgl], axis=1).astype(BF16)
    dh = _matmul(dp, w["w_in"], tb=True, name=tag + "_in_dx")
    dw_in = _matmul(h, dp, ta=True, out_dtype=BF16, name=tag + "_in_dw")
    dx, dshift, dscale, dg_pre = _sub_bwd_pre(dh, x, mpre, dx_new, g_pre, scale, tag)
    dhn_v = jnp.sum(dhn, axis=(0, 1))
    dhn_v = dhn_v[:A_VDIM] + dhn_v[A_VDIM:]
    dws = dict(w_in=dw_in, w_out=dw_out, w_branch_a=dwa.astype(BF16), w_branch_b=dwb.astype(BF16),
               w_branch_c=dwc.astype(BF16))
    return dx, dws, jnp.stack([dshift, dscale, dgate]), dg_pre, dg_post, dlb[0], dhn_v, received


class _LocalWeights:
    def __init__(self, wts):
        self.wts = wts

    def first(self):
        return None

    def shard(self, l):
        return None

    def layer(self, l, gathered):
        return {k: v[l] for k, v in self.wts.items()}

    fused = False

    def pack(self, names, dws):
        return [dws[k] for k in names]

    def last(self, packed):
        return packed


class _ShardedWeights:
    def __init__(self, shards):
        self.shards = shards

    def shard(self, l):
        return [self.shards[k][l].astype(BF16) for k in BIG_WEIGHTS]

    def first(self):
        return _all_gather_many(self.shard(0), "weights_all_gather")

    def layer(self, l, gathered):
        out = {}
        for k, got in zip(BIG_WEIGHTS, gathered):
            _, r, c = self.shards[k].shape
            out[k] = got if k in ROW_SHARDED else got.reshape(N_DEV, r, c).transpose(1, 0, 2).reshape(r, N_DEV * c)
        return out

    fused = True

    def pack(self, names, dws):
        out = []
        for k in names:
            _, r, c = self.shards[k].shape
            g = dws[k]
            out.append(g.reshape(N_DEV, r, c) if k in ROW_SHARDED else g.reshape(r, N_DEV, c).transpose(1, 0, 2))
        return out

    def last(self, packed):
        return _all_to_all_many(packed, "grads_all_to_all")


def _local_step(x, target, mod, norm_g, lb_all, hnorm, supply):
    depth = mod.shape[0]
    d = x.shape[1]
    saved, wls = [], []
    gathered = supply.first()
    for l in range(depth):
        wl = supply.layer(l, gathered)
        wls.append(wl)
        x, s0 = _ffn_fwd(x, wl["ffn1_w_in"], wl["ffn1_w_out"], norm_g[l, 0], norm_g[l, 1], mod[l, 0], 0.5, "ffn1")
        nxt = supply.shard(l + 1) if l + 1 < depth else None
        x, s1, gathered = _mix_fwd(x, wl, norm_g[l, 2], norm_g[l, 3], mod[l, 1], lb_all[l], hnorm[l], "mix", nxt)
        x, s2 = _ffn_fwd(x, wl["ffn2_w_in"], wl["ffn2_w_out"], norm_g[l, 4], norm_g[l, 5], mod[l, 2], 0.5, "ffn2")
        saved.append((s0, s1, s2))
    dx, sq = _loss_head(x, target, "loss_head")
    loss = 0.5 * jnp.sum(sq) / d
    dmod, dng, dlb, dhn = [], [], [], []
    early = ("ffn2_w_in", "ffn2_w_out")
    late = tuple(k for k in BIG_WEIGHTS if k not in early)
    returned = {}
    waiting = []
    for l in reversed(range(depth)):
        wl = wls[l]
        s0, s1, s2 = saved[l]
        dx, dwi2, dwo2, dm2, dgp2, dgq2 = _ffn_bwd(dx, s2, wl["ffn2_w_in"], wl["ffn2_w_out"], norm_g[l, 4],
                                                   norm_g[l, 5], mod[l, 2], 0.5, "ffn2")
        waiting += zip([(l, k) for k in early], supply.pack(early, dict(ffn2_w_in=dwi2, ffn2_w_out=dwo2)))
        keys, bufs = [k for k, _ in waiting], [b for _, b in waiting]
        dx, dwm, dm1, dgp1, dgq1, dlb_l, dhn_l, received = _mix_bwd(
            dx, s1, wl, norm_g[l, 2], norm_g[l, 3], mod[l, 1], lb_all[l], "mix", bufs if supply.fused else None)
        returned.update(zip(keys, received if supply.fused else bufs))
        dx, dwi1, dwo1, dm0, dgp0, dgq0 = _ffn_bwd(dx, s0, wl["ffn1_w_in"], wl["ffn1_w_out"], norm_g[l, 0],
                                                   norm_g[l, 1], mod[l, 0], 0.5, "ffn1")
        dmod.append(jnp.stack([dm0, dm1, dm2]))
        dng.append(jnp.stack([dgp0, dgq0, dgp1, dgq1, dgp2, dgq2]))
        dlb.append(dlb_l)
        dhn.append(dhn_l)
        waiting = list(zip([(l, k) for k in late], supply.pack(late, dict(dwm, ffn1_w_in=dwi1, ffn1_w_out=dwo1))))
    returned.update(zip([k for k, _ in waiting], supply.last([b for _, b in waiting])))
    rev = lambda lst: jnp.stack(lst[::-1])
    return loss, dx, rev(dmod), rev(dng), rev(dlb), rev(dhn), returned


def _lb_all(logits):
    lb_p = jax.nn.softmax(logits.astype(F32), axis=0)
    return jnp.cumsum(lb_p, axis=0) - lb_p[0:1]


def _pad_rows(a, rows):
    return jnp.pad(a, ((0, rows - a.shape[0]), (0, 0)))


def kernel(x, c, w_ada, b_ada, norm_g, ffn1_w_in, ffn1_w_out, w_in, hgrn_lb_logits, hgrn_norm_g, w_branch_a, w_branch_b, w_branch_c, w_out, ffn2_w_in, ffn2_w_out, loss_target, m_w_ada, m_b_ada, m_norm_g, m_ffn1_w_in, m_ffn1_w_out, m_w_in, m_hgrn_lb_logits, m_hgrn_norm_g, m_w_branch_a, m_w_branch_b, m_w_branch_c, m_w_out, m_ffn2_w_in, m_ffn2_w_out, v_w_ada, v_b_ada, v_norm_g, v_ffn1_w_in, v_ffn1_w_out, v_w_in, v_hgrn_lb_logits, v_hgrn_norm_g, v_w_branch_a, v_w_branch_b, v_w_branch_c, v_w_out, v_ffn2_w_in, v_ffn2_w_out):
    weights = dict(w_ada=w_ada, b_ada=b_ada, norm_g=norm_g, ffn1_w_in=ffn1_w_in, ffn1_w_out=ffn1_w_out, w_in=w_in,
                   hgrn_lb_logits=hgrn_lb_logits, hgrn_norm_g=hgrn_norm_g, w_branch_a=w_branch_a,
                   w_branch_b=w_branch_b, w_branch_c=w_branch_c, w_out=w_out, ffn2_w_in=ffn2_w_in,
                   ffn2_w_out=ffn2_w_out)
    mom1 = dict(w_ada=m_w_ada, b_ada=m_b_ada, norm_g=m_norm_g, ffn1_w_in=m_ffn1_w_in, ffn1_w_out=m_ffn1_w_out,
                w_in=m_w_in, hgrn_lb_logits=m_hgrn_lb_logits, hgrn_norm_g=m_hgrn_norm_g, w_branch_a=m_w_branch_a,
                w_branch_b=m_w_branch_b, w_branch_c=m_w_branch_c, w_out=m_w_out, ffn2_w_in=m_ffn2_w_in,
                ffn2_w_out=m_ffn2_w_out)
    mom2 = dict(w_ada=v_w_ada, b_ada=v_b_ada, norm_g=v_norm_g, ffn1_w_in=v_ffn1_w_in, ffn1_w_out=v_ffn1_w_out,
                w_in=v_w_in, hgrn_lb_logits=v_hgrn_lb_logits, hgrn_norm_g=v_hgrn_norm_g, w_branch_a=v_w_branch_a,
                w_branch_b=v_w_branch_b, w_branch_c=v_w_branch_c, w_out=v_w_out, ffn2_w_in=v_ffn2_w_in,
                ffn2_w_out=v_ffn2_w_out)
    order = list(weights)
    depth, d, ada_cols = w_ada.shape
    nd = d // LANES
    xi, yi, ci = _my_coords()
    me = 4 * xi + 2 * yi + ci

    small = jnp.concatenate([c.reshape(nd, LANES), norm_g.reshape(depth * 6, LANES)], axis=0)
    g1 = _all_gather(small, True, False, "small_all_gather").reshape(N_DEV, small.shape[0], LANES)
    c_act = _silu(g1[:, :nd].reshape(N_DEV, d))
    norm_full = g1[:, nd:].reshape(N_DEV, depth, 6, LANES).transpose(1, 2, 0, 3).reshape(depth, 6, d)

    c_pad = _pad_rows(c_act, 16)
    mod_sh = jnp.stack([_matmul(c_pad, w_ada[l], name="ada_mod")[:N_DEV]
                        + lax.dynamic_slice_in_dim(b_ada[l], me * ada_cols, ada_cols)[None]
                        for l in range(depth)])
    g2 = _all_gather(mod_sh.reshape(-1, LANES), True, False, "mod_all_gather")
    g2 = g2.reshape(N_DEV, depth, N_DEV, ada_cols)
    mod = lax.dynamic_index_in_dim(g2, me, axis=2, keepdims=False)
    mod = mod.transpose(1, 0, 2).reshape(depth, 3, 3, d)

    supply = _ShardedWeights({k: weights[k] for k in BIG_WEIGHTS})
    lb_all, lb_vjp = jax.vjp(_lb_all, hgrn_lb_logits)

    loss, dx, dmod, dng, dlb, dhn, received = _local_step(x[0], loss_target[0], mod, norm_full, lb_all,
                                                          hgrn_norm_g, supply)
    loss = lax.psum(loss, ("x", "y", "c"))

    dhn_pad = jnp.pad(dhn.reshape(-1), (0, 8 * LANES - dhn.size))
    pieces = [dmod.reshape(-1), dng.reshape(-1), dlb.reshape(-1), dhn_pad]
    sizes = [p_.size for p_ in pieces]
    smallg = jnp.concatenate(pieces).reshape(-1, LANES)
    g3, gsum = _all_gather(smallg, True, True, "small_grads_all_gather")
    g3 = g3.reshape(N_DEV, -1)
    gsum = gsum.reshape(-1)
    dmod_all = g3[:, :sizes[0]].reshape(N_DEV, depth, 9 * d)
    o1 = sizes[0]
    grads = {}
    grads["b_ada"] = gsum[:o1].reshape(depth, 9 * d)
    dng_sum = gsum[o1:o1 + sizes[1]].reshape(depth, 6, nd, LANES)
    grads["norm_g"] = lax.dynamic_index_in_dim(dng_sum, me, axis=2, keepdims=False)
    o2 = o1 + sizes[1]
    dlb_sum = gsum[o2:o2 + sizes[2]].reshape(depth, A_QK)
    grads["hgrn_lb_logits"] = lb_vjp(dlb_sum)[0]
    o3 = o2 + sizes[2]
    grads["hgrn_norm_g"] = gsum[o3:o3 + dhn.size].reshape(depth, A_VDIM)
    dmod_mine = lax.dynamic_slice_in_dim(dmod_all, me * ada_cols, ada_cols, axis=2)
    grads["w_ada"] = jnp.stack([_matmul(c_pad, _pad_rows(dmod_mine[:, l], 16), ta=True, name="ada_dw")
                                for l in range(depth)])

    outs = {}
    for k in order:
        w = weights[k]
        w2 = w.reshape(-1, w.shape[-1])
        if k in BIG_WEIGHTS:
            gp = [received[l, k] for l in range(depth)]
        else:
            gp = [grads[k].reshape((1,) + w2.shape)]
        res = _adamw(w2, mom1[k].reshape(w2.shape), mom2[k].reshape(w2.shape), gp, "adamw")
        outs[k] = [r.reshape(w.shape) for r in res]
    return (loss, dx[None], *[outs[k][0] for k in order], *[outs[k][1] for k in order],
            *[outs[k][2] for k in order], *[outs[k][3] for k in order])
```

```python
import functools
import math

import jax
import jax.numpy as jnp
from jax import lax
from jax.experimental import pallas as pl
from jax.experimental.pallas import tpu as pltpu

F32 = jnp.float32
BF16 = jnp.bfloat16

A_HEADS, A_KDIM, A_VDIM, A_CHUNK = 6, 128, 64, 64
B_HEADS, HDIM = 6, 64
C_GROUPS = ((128, 1), (512, 4), (2048, 16))
C_HPG = 4
C_HEADS = C_HPG * len(C_GROUPS)
N_BRANCH = 3
EPS = 1e-6
NEG_BIG = -1e30
TINY = 1e-30
A_QK = A_HEADS * A_KDIM
A_V = A_HEADS * A_VDIM
B_W = B_HEADS * HDIM
C_W = C_HEADS * HDIM
C_OUT = C_HPG * HDIM
COL_AQ, COL_AF, COL_AI, COL_AG = 0, A_QK, 2 * A_QK, 2 * A_QK + A_V
COL_BQ = 2 * A_QK + 2 * A_V
COL_BK, COL_BV = COL_BQ + B_W, COL_BQ + 2 * B_W
COL_CQ = COL_BQ + 3 * B_W
COL_CK, COL_CV = COL_CQ + C_W, COL_CQ + 2 * C_W
COL_GATE = COL_CQ + 3 * C_W

ADAM_LR, ADAM_B1, ADAM_B2, ADAM_EPS, ADAM_WD, ADAM_STEP = 0.001, 0.9, 0.999, 1e-08, 0.01, 10

N_DEV = 8
LANES = 128
VMEM_LIMIT = 48 * 1024 * 1024
MATMUL_VMEM_BUDGET = 28 * 1024 * 1024
SUB = 16
EXP_CLAMP = 80.0
MESH = pl.DeviceIdType.MESH

BIG_WEIGHTS = ("ffn1_w_in", "ffn1_w_out", "w_in", "w_branch_a", "w_branch_b", "w_branch_c", "w_out",
               "ffn2_w_in", "ffn2_w_out")
ROW_SHARDED = ("ffn1_w_out", "w_out", "ffn2_w_out")


def _cparams(sem):
    return pltpu.CompilerParams(dimension_semantics=sem, vmem_limit_bytes=VMEM_LIMIT)


def _tile(n, cap):
    best, t = None, LANES
    while t <= min(n, cap):
        if n % t == 0:
            best = t
        t += LANES
    return best or n


def _rows(t, cap=256):
    r = cap
    while t % r:
        r //= 2
    return r


def _divisors(n):
    return [t for t in range(LANES, n + 1, LANES) if n % t == 0] or [n]


def _matmul_tiles(m, n, k, a_size, b_size, o_size):
    best, best_key = None, None
    for tm in _divisors(m):
        for tn in _divisors(n):
            for tk in _divisors(k):
                if tm > 1024 or tn > 3072 or tk > 4096:
                    continue
                cast = (tm * tk * 2 if a_size > 2 else 0) + (tk * tn * 2 if b_size > 2 else 0)
                need = 2 * (tm * tk * a_size + tk * tn * b_size + tm * tn * o_size) + 2 * tm * tn * 4 + cast
                if need > MATMUL_VMEM_BUDGET:
                    continue
                key = (tm * tn * tk, tk)
                if best_key is None or key > best_key:
                    best, best_key = (tm, tn, tk), key
    return best


def _dot(a, b):
    return jnp.dot(a, b, preferred_element_type=F32)


def _dot_nt(a, b):
    return lax.dot_general(a, b, (((1,), (1,)), ((), ())), preferred_element_type=F32)


def _dot_tn(a, b):
    return lax.dot_general(a, b, (((0,), (0,)), ((), ())), preferred_element_type=F32)


def _split3(x):
    h = x.astype(BF16)
    r = x - h.astype(F32)
    m = r.astype(BF16)
    lo = (r - m.astype(F32)).astype(BF16)
    return h, m, lo


def _ones_left(mat01, x):
    h, m, lo = _split3(x)
    return _dot(mat01, h) + _dot(mat01, m) + _dot(mat01, lo)


def _silu(x):
    return x * jax.nn.sigmoid(x)


def _dsilu(x):
    s = jax.nn.sigmoid(x)
    return s * (1.0 + x * (1.0 - s))


def _matmul(a, b, *, ta=False, tb=False, out_dtype=F32, name):
    if ta:
        kdim, m = a.shape
    else:
        m, kdim = a.shape
    n = b.shape[0] if tb else b.shape[1]
    tm, tn, tk = _matmul_tiles(m, n, kdim, a.dtype.itemsize, b.dtype.itemsize, jnp.dtype(out_dtype).itemsize)
    nk = kdim // tk
    ni, nj = m // tm, n // tn
    a_bytes, b_bytes = m * kdim * a.dtype.itemsize, kdim * n * b.dtype.itemsize
    j_outer = nk == 1 and (b_bytes + a_bytes * nj) < (a_bytes + b_bytes * ni)
    dims = (((0 if ta else 1,), (1 if tb else 0,)), ((), ()))

    def body(a_ref, b_ref, o_ref, *scratch):
        p = lax.dot_general(a_ref[...].astype(BF16), b_ref[...].astype(BF16), dims, preferred_element_type=F32)
        if nk == 1:
            o_ref[...] = p.astype(o_ref.dtype)
            return
        acc = scratch[0]
        k = pl.program_id(2)

        @pl.when(k == 0)
        def _():
            acc[...] = p

        @pl.when(k > 0)
        def _():
            acc[...] += p

        @pl.when(k == nk - 1)
        def _():
            o_ref[...] = acc[...].astype(o_ref.dtype)

    def spec(shape, pick):
        if j_outer:
            return pl.BlockSpec(shape, lambda j, i, k: pick(i, j, k))
        return pl.BlockSpec(shape, lambda i, j, k: pick(i, j, k))

    a_spec = spec((tk, tm), lambda i, j, k: (k, i)) if ta else spec((tm, tk), lambda i, j, k: (i, k))
    b_spec = spec((tn, tk), lambda i, j, k: (j, k)) if tb else spec((tk, tn), lambda i, j, k: (k, j))
    return pl.pallas_call(
        body, name=name, grid=(nj, ni, nk) if j_outer else (ni, nj, nk), in_specs=[a_spec, b_spec],
        out_specs=spec((tm, tn), lambda i, j, k: (i, j)),
        out_shape=jax.ShapeDtypeStruct((m, n), out_dtype),
        scratch_shapes=[pltpu.VMEM((tm, tn), F32)] if nk > 1 else [],
        compiler_params=_cparams(("parallel", "parallel", "arbitrary")),
    )(a, b)


def _rms_fwd(z, mcol, acol, res, out_dtype, name):
    t, d = z.shape
    tr = _rows(t)
    has_res = res is not None

    def body(*refs):
        if has_res:
            z_ref, m_ref, a_ref, r_ref, o_ref = refs
        else:
            z_ref, m_ref, a_ref, o_ref = refs
        zf = z_ref[...]
        r = lax.rsqrt(jnp.mean(zf * zf, axis=-1, keepdims=True) + EPS)
        y = zf * r * m_ref[...] + a_ref[...]
        if has_res:
            y = r_ref[...] + y
        o_ref[...] = y.astype(o_ref.dtype)

    row = pl.BlockSpec((tr, d), lambda i: (i, 0))
    col = pl.BlockSpec((1, d), lambda i: (0, 0))
    ins = [z, mcol, acol] + ([res] if has_res else [])
    return pl.pallas_call(
        body, name=name, grid=(t // tr,), in_specs=[row, col, col] + ([row] if has_res else []),
        out_specs=row, out_shape=jax.ShapeDtypeStruct((t, d), out_dtype),
        compiler_params=_cparams(("parallel",)),
    )(*ins)


def _rms_bwd(d_out, z, mcol, dres, out_dtype, name):
    t, d = z.shape
    tr = _rows(t)
    has_res = dres is not None

    def body(*refs):
        if has_res:
            d_ref, z_ref, m_ref, r_ref, o_ref, s1_ref, s2_ref = refs
        else:
            d_ref, z_ref, m_ref, o_ref, s1_ref, s2_ref = refs
        i = pl.program_id(0)
        zf = z_ref[...]
        r = lax.rsqrt(jnp.mean(zf * zf, axis=-1, keepdims=True) + EPS)
        zh = zf * r
        df = d_ref[...].astype(F32)
        dzh = df * m_ref[...]
        dz = r * (dzh - zh * jnp.mean(dzh * zh, axis=-1, keepdims=True))
        if has_res:
            dz = dz + r_ref[...]
        o_ref[...] = dz.astype(o_ref.dtype)
        s1 = jnp.sum(df * zh, axis=0, keepdims=True)
        s2 = jnp.sum(df, axis=0, keepdims=True)

        @pl.when(i == 0)
        def _():
            s1_ref[...] = s1
            s2_ref[...] = s2

        @pl.when(i > 0)
        def _():
            s1_ref[...] += s1
            s2_ref[...] += s2

    row = pl.BlockSpec((tr, d), lambda i: (i, 0))
    col = pl.BlockSpec((1, d), lambda i: (0, 0))
    ins = [d_out, z, mcol] + ([dres] if has_res else [])
    return pl.pallas_call(
        body, name=name, grid=(t // tr,), in_specs=[row, row, col] + ([row] if has_res else []),
        out_specs=[row, col, col],
        out_shape=[jax.ShapeDtypeStruct((t, d), out_dtype), jax.ShapeDtypeStruct((1, d), F32),
                   jax.ShapeDtypeStruct((1, d), F32)],
        compiler_params=_cparams(("arbitrary",)),
    )(*ins)


FFN_IN_TILE = (512, 1408)


def _ffn_in_swiglu(h, w_in, name):
    t, d = h.shape
    f = w_in.shape[1] // 2
    tm, tn = _tile(t, FFN_IN_TILE[0]), _tile(f, FFN_IN_TILE[1])
    nj = f // tn

    def body(h_ref, wa_ref, wb_ref, a_ref, b_ref, s_ref):
        hv = h_ref[...].astype(BF16)
        a = _dot(hv, wa_ref[...].astype(BF16))
        b = _dot(hv, wb_ref[...].astype(BF16))
        a_ref[...] = a.astype(BF16)
        b_ref[...] = b.astype(BF16)
        s_ref[...] = (_silu(a) * b).astype(BF16)

    out = pl.BlockSpec((tm, tn), lambda j, i: (i, j))
    return pl.pallas_call(
        body, name=name, grid=(nj, t // tm),
        in_specs=[pl.BlockSpec((tm, d), lambda j, i: (i, 0)), pl.BlockSpec((d, tn), lambda j, i: (0, j)),
                  pl.BlockSpec((d, tn), lambda j, i: (0, j + nj))],
        out_specs=[out, out, out], out_shape=[jax.ShapeDtypeStruct((t, f), BF16)] * 3,
        compiler_params=_cparams(("parallel", "parallel")),
    )(h, w_in, w_in)


def _swiglu_bwd(ua, ub, ds, name):
    t, f = ua.shape
    tr = _rows(t)

    def body(a_ref, b_ref, ds_ref, du_ref):
        a = a_ref[...].astype(F32)
        b = b_ref[...].astype(F32)
        g = ds_ref[...].astype(F32)
        du_ref[:, :f] = (g * b * _dsilu(a)).astype(du_ref.dtype)
        du_ref[:, f:] = (g * _silu(a)).astype(du_ref.dtype)

    half = pl.BlockSpec((tr, f), lambda i: (i, 0))
    return pl.pallas_call(
        body, name=name, grid=(t // tr,), in_specs=[half, half, half],
        out_specs=pl.BlockSpec((tr, 2 * f), lambda i: (i, 0)), out_shape=jax.ShapeDtypeStruct((t, 2 * f), BF16),
        compiler_params=_cparams(("parallel",)),
    )(ua, ub, ds)


def _loss_head(y, target, name):
    t, d = y.shape
    tr = _rows(t)

    def body(y_ref, t_ref, dy_ref, sq_ref):
        i = pl.program_id(0)
        e = y_ref[...] - t_ref[...]
        dy_ref[...] = e * (1.0 / d)
        s = jnp.sum(e * e, axis=0, keepdims=True)

        @pl.when(i == 0)
        def _():
            sq_ref[...] = s

        @pl.when(i > 0)
        def _():
            sq_ref[...] += s

    row = pl.BlockSpec((tr, d), lambda i: (i, 0))
    col = pl.BlockSpec((1, d), lambda i: (0, 0))
    return pl.pallas_call(
        body, name=name, grid=(t // tr,), in_specs=[row, row], out_specs=[row, col],
        out_shape=[jax.ShapeDtypeStruct((t, d), F32), jax.ShapeDtypeStruct((1, d), F32)],
        compiler_params=_cparams(("arbitrary",)),
    )(y, target)


def _hgrn_consts():
    c = A_CHUNK
    shift = SUB.bit_length() - 1
    r = lax.broadcasted_iota(jnp.int32, (c, c), 0)
    s = lax.broadcasted_iota(jnp.int32, (c, c), 1)
    sub_r = lax.shift_right_logical(r, shift)
    incl = s <= r
    masks = [jnp.logical_and(sub_r == i, incl) for i in range(c // SUB)]
    rev_incl = jnp.where(s >= r, 1.0, 0.0).astype(BF16)
    r2 = lax.broadcasted_iota(jnp.int32, (2 * c + 8, c), 0)
    s2 = lax.broadcasted_iota(jnp.int32, (2 * c + 8, c), 1)
    sub_start = lax.shift_left(lax.shift_right_logical(r2 - c, shift), shift)
    running = jnp.where(s2 <= r2, 1.0, 0.0)
    before = jnp.where(s2 < sub_start, 1.0, 0.0)
    stack = jnp.where(r2 < c, running, jnp.where(r2 < 2 * c, before, 1.0)).astype(BF16)
    return stack, masks, incl, rev_incl


def _hgrn_gates(q_raw, f_raw, lbv, stack):
    sg = jax.nn.sigmoid(f_raw)
    sgn = jax.nn.sigmoid(-f_raw)
    f = lbv + (1.0 - lbv) * sg
    logf = jnp.log(jnp.maximum(f, TINY))
    return dict(sg=sg, sgn=sgn, f=f, k=(1.0 - lbv) * sgn, q=_silu(q_raw), bb=_ones_left(stack, logf))


def _hgrn_chunk(q_raw, f_raw, lbv, stack):
    return _hgrn_decays(_hgrn_gates(q_raw, f_raw, lbv, stack))


def _hgrn_decays(gates):
    c = A_CHUNK
    sg, sgn, f, k, q, bb = (gates[n] for n in ("sg", "sgn", "f", "k", "q", "bb"))
    b = bb[:c]
    bsrow = bb[c:2 * c]
    b_end = bb[2 * c:2 * c + 1]
    e_sub = jnp.exp(b - bsrow)
    e_b = jnp.exp(b)
    e_end = jnp.exp(b_end - b)
    qs = q * e_sub
    q_in = q * e_b
    kend = k * e_end
    kfac = [jnp.exp(jnp.minimum(bsrow[i * SUB:i * SUB + 1] - b, EXP_CLAMP)) for i in range(c // SUB)]
    return dict(sg=sg, sgn=sgn, f=f, k=k, q=q, b=b, b_end=b_end, e_sub=e_sub, e_b=e_b, e_end=e_end,
                qs=qs, q_in=q_in, kend=kend, kfac=kfac)


def _hgrn_scores(ch, masks):
    qs_b = ch["qs"].astype(BF16)
    a = None
    for i, mk in enumerate(masks):
        ki = (ch["k"] * ch["kfac"][i]).astype(BF16)
        part = jnp.where(mk, _dot_nt(qs_b, ki), 0.0)
        a = part if a is None else a + part
    return a


def _hgrn_fwd(p, lb, hn2, name):
    t = p.shape[0]
    tb = _rows(t)
    nt = t // tb
    nc = tb // A_CHUNK
    c = A_CHUNK

    def body(q_ref, f_ref, i_ref, g_ref, lb_ref, hn_ref, y_ref, o_ref, st_ref, s_scr):
        j = pl.program_id(1)

        @pl.when(j == 0)
        def _():
            s_scr[...] = jnp.zeros_like(s_scr)

        stack, masks, _, _ = _hgrn_consts()
        units = [(ci, hh) for ci in range(nc) for hh in range(2)]
        lsl = [slice(A_KDIM * hh, A_KDIM * (hh + 1)) for hh in range(2)]
        hsl = [slice(A_VDIM * hh, A_VDIM * (hh + 1)) for hh in range(2)]
        rows = [pl.ds(ci * c, c) for ci in range(nc)]
        gates = {u: _hgrn_gates(q_ref[rows[u[0]], lsl[u[1]]], f_ref[rows[u[0]], lsl[u[1]]], lb_ref[:, lsl[u[1]]], stack)
                 for u in units}
        ch = {u: _hgrn_decays(gates[u]) for u in units}
        v = {u: i_ref[rows[u[0]], hsl[u[1]]].astype(BF16) for u in units}
        a = {u: _hgrn_scores(ch[u], masks).astype(BF16) for u in units}
        grow = {u: _dot_tn(v[u], ch[u]["kend"].astype(BF16)) for u in units}
        states = [s_scr[0], s_scr[1]]
        entering = {}
        for ci, hh in units:
            entering[ci, hh] = states[hh]
            st_ref[hh, ci] = states[hh]
            states[hh] = states[hh] * jnp.exp(ch[ci, hh]["b_end"]) + grow[ci, hh]
        s_scr[0] = states[0]
        s_scr[1] = states[1]
        for u in units:
            o_ref[rows[u[0]], hsl[u[1]]] = (_dot_nt(ch[u]["q_in"].astype(BF16), entering[u].astype(BF16))
                                            + _dot(a[u], v[u]))
        for hh in range(2):
            hsl = slice(A_VDIM * hh, A_VDIM * (hh + 1))
            o = o_ref[:, hsl]
            r = lax.rsqrt(jnp.mean(o * o, axis=-1, keepdims=True) + EPS)
            y_ref[:, hsl] = (o * r * hn_ref[:, hsl] * _silu(g_ref[:, hsl])).astype(y_ref.dtype)

    w2 = 2 * A_KDIM
    return pl.pallas_call(
        body, name=name, grid=(A_HEADS // 2, nt),
        in_specs=[pl.BlockSpec((tb, w2), lambda h, j: (j, COL_AQ // w2 + h)),
                  pl.BlockSpec((tb, w2), lambda h, j: (j, COL_AF // w2 + h)),
                  pl.BlockSpec((tb, LANES), lambda h, j: (j, COL_AI // LANES + h)),
                  pl.BlockSpec((tb, LANES), lambda h, j: (j, COL_AG // LANES + h)),
                  pl.BlockSpec((1, w2), lambda h, j: (0, h)),
                  pl.BlockSpec((1, LANES), lambda h, j: (0, 0))],
        out_specs=[pl.BlockSpec((tb, LANES), lambda h, j: (j, h)),
                   pl.BlockSpec((tb, LANES), lambda h, j: (j, h)),
                   pl.BlockSpec((2, nc, A_VDIM, A_KDIM), lambda h, j: (h, j, 0, 0))],
        out_shape=[jax.ShapeDtypeStruct((t, A_V), BF16), jax.ShapeDtypeStruct((t, A_V), F32),
                   jax.ShapeDtypeStruct((A_HEADS, t // c, A_VDIM, A_KDIM), F32)],
        scratch_shapes=[pltpu.VMEM((2, A_VDIM, A_KDIM), F32)],
        compiler_params=_cparams(("parallel", "arbitrary")),
    )(p, p, p, p, lb, hn2)


def _hgrn_bwd(p, lb, hn2, o_raw, states, dya, name):
    t = p.shape[0]
    tb = _rows(t)
    nt = t // tb
    nc = tb // A_CHUNK
    c = A_CHUNK

    def body(q_ref, f_ref, i_ref, g_ref, lb_ref, hn_ref, o_ref, st_ref, dy_ref,
             dq_ref, df_ref, di_ref, dg_ref, dlb_ref, dhn_ref, ds_scr, do_scr):
        j = pl.program_id(1)

        @pl.when(j == 0)
        def _():
            ds_scr[...] = jnp.zeros_like(ds_scr)
            dlb_ref[...] = jnp.zeros_like(dlb_ref)
            dhn_ref[...] = jnp.zeros_like(dhn_ref)

        stack, masks, incl, rev_incl = _hgrn_consts()
        for hh in range(2):
            hsl = slice(A_VDIM * hh, A_VDIM * (hh + 1))
            o = o_ref[:, hsl]
            g = g_ref[:, hsl]
            dy = dy_ref[:, hsl].astype(F32)
            hn = hn_ref[:, hsl]
            r = lax.rsqrt(jnp.mean(o * o, axis=-1, keepdims=True) + EPS)
            oh = o * r
            sgate = _silu(g)
            dg_ref[:, hsl] = (dy * oh * hn * _dsilu(g)).astype(dg_ref.dtype)
            dhn_ref[0, :, hsl] += jnp.sum(dy * oh * sgate, axis=0, keepdims=True)
            doh = dy * hn * sgate
            do_scr[:, hsl] = r * (doh - oh * jnp.mean(doh * oh, axis=-1, keepdims=True))

        units = [(ci, hh) for ci in reversed(range(nc)) for hh in range(2)]
        lsl = [slice(A_KDIM * hh, A_KDIM * (hh + 1)) for hh in range(2)]
        hsl = [slice(A_VDIM * hh, A_VDIM * (hh + 1)) for hh in range(2)]
        rows = [pl.ds(ci * c, c) for ci in range(nc)]
        q_raw = {u: q_ref[rows[u[0]], lsl[u[1]]] for u in units}
        gates = {u: _hgrn_gates(q_raw[u], f_ref[rows[u[0]], lsl[u[1]]], lb_ref[:, lsl[u[1]]], stack) for u in units}
        ch = {u: _hgrn_decays(gates[u]) for u in units}
        v = {u: i_ref[rows[u[0]], hsl[u[1]]].astype(BF16) for u in units}
        do_b = {u: do_scr[rows[u[0]], hsl[u[1]]].astype(BF16) for u in units}
        st = {u: st_ref[u[1], u[0]] for u in units}
        qs_b = {u: ch[u]["qs"].astype(BF16) for u in units}
        a_b = {u: _hgrn_scores(ch[u], masks).astype(BF16) for u in units}
        da = {u: jnp.where(incl, _dot_nt(do_b[u], v[u]), 0.0) for u in units}
        dq_x = {u: _dot(do_b[u], st[u].astype(BF16)) for u in units}
        grow = {u: _dot_tn(do_b[u], ch[u]["q_in"].astype(BF16)) for u in units}
        dstates = [ds_scr[0], ds_scr[1]]
        leaving = {}
        for ci, hh in units:
            leaving[ci, hh] = dstates[hh]
            dstates[hh] = dstates[hh] * jnp.exp(ch[ci, hh]["b_end"]) + grow[ci, hh]
        for hh in range(2):
            ds_scr[hh] = dstates[hh]
        dst_b = {u: leaving[u].astype(BF16) for u in units}
        dv = {u: _dot_tn(a_b[u], do_b[u]) + _dot_nt(ch[u]["kend"].astype(BF16), dst_b[u]) for u in units}
        dk_x = {u: _dot(v[u], dst_b[u]) for u in units}
        dlb_acc = [jnp.zeros((1, A_KDIM), F32), jnp.zeros((1, A_KDIM), F32)]
        for u in units:
            ci, hh = u
            cu = ch[u]
            lbv = lb_ref[:, lsl[hh]]
            dq_i = None
            dk_i = None
            kdk_i = None
            for i, mk in enumerate(masks):
                dam = jnp.where(mk, da[u], 0.0).astype(BF16)
                ki = (cu["k"] * cu["kfac"][i]).astype(BF16)
                pq = _dot(dam, ki)
                pk = _dot_tn(dam, qs_b[u])
                dq_i = pq if dq_i is None else dq_i + pq
                dk_i = cu["kfac"][i] * pk if dk_i is None else dk_i + cu["kfac"][i] * pk
                kdk_i = ki.astype(F32) * pk if kdk_i is None else kdk_i + ki.astype(F32) * pk
            dq = cu["e_sub"] * dq_i + cu["e_b"] * dq_x[u]
            dk = dk_i + cu["e_end"] * dk_x[u]
            kx = cu["kend"] * dk_x[u]
            db = (qs_b[u].astype(F32) * dq_i + cu["q_in"] * dq_x[u]) - (kdk_i + kx)
            later = (jnp.exp(cu["b_end"]) * jnp.sum(leaving[u] * st[u], axis=0, keepdims=True)
                     + jnp.sum(kx, axis=0, keepdims=True))
            dlogf = later + _ones_left(rev_incl, db)
            dfv = jnp.where(cu["f"] > TINY, dlogf / cu["f"], 0.0)
            dq_ref[rows[ci], lsl[hh]] = (dq * _dsilu(q_raw[u])).astype(dq_ref.dtype)
            df_ref[rows[ci], lsl[hh]] = ((1.0 - lbv) * cu["sg"] * cu["sgn"] * (dfv - dk)).astype(df_ref.dtype)
            dlb_acc[hh] = dlb_acc[hh] + jnp.sum(dfv * (1.0 - cu["sg"]) - dk * cu["sgn"], axis=0, keepdims=True)
            di_ref[rows[ci], hsl[hh]] = dv[u].astype(di_ref.dtype)
        for hh in range(2):
            dlb_ref[:, A_KDIM * hh:A_KDIM * (hh + 1)] += dlb_acc[hh]

    w2 = 2 * A_KDIM
    rev = lambda j: nt - 1 - j
    return pl.pallas_call(
        body, name=name, grid=(A_HEADS // 2, nt),
        in_specs=[pl.BlockSpec((tb, w2), lambda h, j: (rev(j), COL_AQ // w2 + h)),
                  pl.BlockSpec((tb, w2), lambda h, j: (rev(j), COL_AF // w2 + h)),
                  pl.BlockSpec((tb, LANES), lambda h, j: (rev(j), COL_AI // LANES + h)),
                  pl.BlockSpec((tb, LANES), lambda h, j: (rev(j), COL_AG // LANES + h)),
                  pl.BlockSpec((1, w2), lambda h, j: (0, h)),
                  pl.BlockSpec((1, LANES), lambda h, j: (0, 0)),
                  pl.BlockSpec((tb, LANES), lambda h, j: (rev(j), h)),
                  pl.BlockSpec((2, nc, A_VDIM, A_KDIM), lambda h, j: (h, rev(j), 0, 0)),
                  pl.BlockSpec((tb, LANES), lambda h, j: (rev(j), h))],
        out_specs=[pl.BlockSpec((tb, w2), lambda h, j: (rev(j), h)),
                   pl.BlockSpec((tb, w2), lambda h, j: (rev(j), h)),
                   pl.BlockSpec((tb, LANES), lambda h, j: (rev(j), h)),
                   pl.BlockSpec((tb, LANES), lambda h, j: (rev(j), h)),
                   pl.BlockSpec((1, w2), lambda h, j: (0, h)),
                   pl.BlockSpec((1, 1, LANES), lambda h, j: (h, 0, 0))],
        out_shape=[jax.ShapeDtypeStruct((t, A_QK), BF16), jax.ShapeDtypeStruct((t, A_QK), BF16),
                   jax.ShapeDtypeStruct((t, A_V), BF16), jax.ShapeDtypeStruct((t, A_V), BF16),
                   jax.ShapeDtypeStruct((1, A_QK), F32), jax.ShapeDtypeStruct((A_HEADS // 2, 1, LANES), F32)],
        scratch_shapes=[pltpu.VMEM((2, A_VDIM, A_KDIM), F32), pltpu.VMEM((tb, LANES), F32)],
        compiler_params=_cparams(("parallel", "arbitrary")),
    )(p, p, p, p, lb, hn2, o_raw, states, dya)


BLK = 128
SCALE = HDIM ** -0.5
SB_CHUNK = 4


def _softplus(z):
    return jnp.maximum(z, 0.0) + jnp.log(1.0 + jnp.exp(-jnp.abs(z)))


def _sb_sum_matrix(keep, with_total=False):
    width = 2 * BLK if with_total else BLK
    sp = lax.broadcasted_iota(jnp.int32, (BLK, width), 0)
    s = lax.broadcasted_iota(jnp.int32, (BLK, width), 1)
    return jnp.where(jnp.logical_or(s >= BLK, keep(sp, s)), 1.0, 0.0).astype(BF16)


def _lanes(col):
    return jnp.broadcast_to(col, (BLK, BLK))


def _sb_fwd(p, kv, name, gather=None):
    t = p.shape[0]
    nq = t // BLK
    nh = B_HEADS // 2
    cw = SB_CHUNK * BLK
    fused = gather is not None
    n = len(gather) if fused else 0

    def body(*refs):
        q_ref, kb, vb = refs[:3]
        o_ref, tot_ref = refs[3 + n:5 + n]
        zbuf, stage, sbuf, abuf = refs[5 + 2 * n:9 + 2 * n]
        hp = pl.program_id(0)
        qi = pl.program_id(1)
        if fused:
            g = _Many(_Gather, refs[3:3 + n], refs[5 + n:5 + 2 * n], *refs[9 + 2 * n:])
            pl.when(jnp.logical_and(hp == 0, qi == 0))(g.start)
            pl.when(jnp.logical_and(hp == nh - 1, qi == 0))(g.forward)

        @pl.when(qi == 0)
        def _():
            abuf[...] = jnp.zeros_like(abuf)

        row = lax.broadcasted_iota(jnp.int32, (BLK, BLK), 0)
        col = lax.broadcasted_iota(jnp.int32, (BLK, BLK), 1)
        sums = _sb_sum_matrix(lambda sp, s: sp >= s, True)
        hsl = [slice(HDIM * h, HDIM * (h + 1)) for h in range(2)]
        nchunk = qi // SB_CHUNK + 1
        for h in range(2):
            zbuf[h] = _dot_nt((q_ref[:, hsl[h]] * SCALE).astype(BF16), kb[:, hsl[h]])

        col_minus_row = col - row

        def causal(j):
            return col_minus_row < (qi - j) * BLK

        def l_pass(c, carry):
            for b in range(SB_CHUNK):
                j = c * SB_CHUNK + b
                off = pl.multiple_of(j * BLK, BLK)
                mask = causal(j)
                for h in range(2):
                    lm = jnp.where(mask, -_softplus(zbuf[h, :, pl.ds(off, BLK)]), 0.0)
                    stage[h, pl.ds(off, BLK), :] = lm.astype(BF16)
            return carry

        lax.fori_loop(0, nchunk, l_pass, 0)

        def sum_pass(c, carry):
            rows = pl.ds(pl.multiple_of(c * cw, cw), cw)
            for h in range(2):
                sbuf[h, rows, :] = _dot(stage[h, rows, :], sums)
            return carry

        lax.fori_loop(0, nchunk, sum_pass, 0)

        def a_pass(it, carry):
            c = nchunk - 1 - it
            runs = list(carry)
            for b in reversed(range(SB_CHUNK)):
                j = c * SB_CHUNK + b
                off = pl.multiple_of(j * BLK, BLK)
                mask = causal(j)
                for h in range(2):
                    s = sbuf[h, pl.ds(off, BLK), :BLK]
                    a = jnp.where(mask, jnp.exp(zbuf[h, :, pl.ds(off, BLK)] + s + runs[h]), 0.0)
                    abuf[h, :, pl.ds(off, BLK)] = a.astype(BF16)
                    runs[h] = runs[h] + sbuf[h, pl.ds(off, BLK), BLK:]
            return tuple(runs)

        zero = jnp.zeros((BLK, BLK), F32)
        runs = lax.fori_loop(0, nchunk, a_pass, (zero, zero))
        for h in range(2):
            tot_ref[:, hsl[h]] = runs[h][:, :HDIM]
            o_ref[:, hsl[h]] = _dot(abuf[h], vb[:, hsl[h]])
        if fused:
            pl.when(jnp.logical_and(hp == nh - 1, qi == nq - 1))(g.finish)

    out_blk = pl.BlockSpec((BLK, LANES), lambda h, i: (i, h))
    hbm = pl.BlockSpec(memory_space=pl.ANY)
    in_specs = [pl.BlockSpec((BLK, LANES), lambda h, i: (i, COL_BQ // LANES + h)),
                pl.BlockSpec((t, LANES), lambda h, i: (0, h)),
                pl.BlockSpec((t, LANES), lambda h, i: (0, B_W // LANES + h))]
    out_shape = [jax.ShapeDtypeStruct((t, B_W), F32)] * 2
    scratch = [pltpu.VMEM((2, BLK, t), F32), pltpu.VMEM((2, t, BLK), BF16),
               pltpu.VMEM((2, t, 2 * BLK), F32), pltpu.VMEM((2, BLK, t), BF16)]
    if fused:
        out_shape = out_shape + _gathered_shapes(gather)
    return pl.pallas_call(
        body, name=name, grid=(nh, nq),
        in_specs=in_specs + [hbm] * n,
        out_specs=[out_blk, out_blk] + [hbm] * n,
        out_shape=out_shape,
        scratch_shapes=scratch + (_comm_sems(n) if fused else []),
        compiler_params=_cparams(("arbitrary", "arbitrary")),
    )(p, kv, kv, *(gather if fused else []))


def _sb_bwd(p, kv, tot, do, name, exchange=None):
    t = p.shape[0]
    nq = t // BLK
    nh = B_HEADS // 2
    cw = SB_CHUNK * BLK
    fused = exchange is not None
    n = len(exchange) if fused else 0

    def body(*refs):
        q_ref, kb, vb, tot_ref, do_ref = refs[:5]
        dq_ref, dk_ref, dv_ref = refs[5 + n:8 + n]
        zbuf, dabuf, lbuf, stage, sbuf, abuf, dzbuf, dkt, dvt = refs[8 + 2 * n:17 + 2 * n]
        hp = pl.program_id(0)
        qi = pl.program_id(1)
        if fused:
            ex = _Many(_Exchange, refs[5:5 + n], refs[8 + n:8 + 2 * n], *refs[17 + 2 * n:])
            pl.when(jnp.logical_and(hp == 0, qi == 0))(ex.start)

        @pl.when(qi == 0)
        def _():
            dkt[...] = jnp.zeros_like(dkt)
            dvt[...] = jnp.zeros_like(dvt)
            dzbuf[...] = jnp.zeros_like(dzbuf)
            abuf[...] = jnp.zeros_like(abuf)

        row = lax.broadcasted_iota(jnp.int32, (BLK, BLK), 0)
        col = lax.broadcasted_iota(jnp.int32, (BLK, BLK), 1)
        sums = _sb_sum_matrix(lambda sp, s: sp <= s)
        hsl = [slice(HDIM * h, HDIM * (h + 1)) for h in range(2)]
        dob = [do_ref[:, hsl[h]].astype(BF16) for h in range(2)]
        total =[jnp.concatenate([tot_ref[:, hsl[h]], tot_ref[:, hsl[h]]], axis=1) for h in range(2)]
        nchunk = qi // SB_CHUNK + 1
        for h in range(2):
            zbuf[h] = _dot_nt((q_ref[:, hsl[h]] * SCALE).astype(BF16), kb[:, hsl[h]])
            dabuf[h] = _dot_nt(dob[h], vb[:, hsl[h]])

        col_minus_row = col - row

        def causal(j):
            return col_minus_row < (qi - j) * BLK

        def blocks(c):
            for b in range(SB_CHUNK):
                j = c * SB_CHUNK + b
                yield j, pl.ds(pl.multiple_of(j * BLK, BLK), BLK)

        def l_pass(c, carry):
            for j, blk_ in blocks(c):
                mask = causal(j)
                for h in range(2):
                    lm = jnp.where(mask, -_softplus(zbuf[h, :, blk_]), 0.0)
                    lbuf[h, :, blk_] = lm
                    stage[h, blk_, :] = lm.astype(BF16)
            return carry

        lax.fori_loop(0, nchunk, l_pass, 0)

        def sum_pass():
            def run_(c, carry):
                rows = pl.ds(pl.multiple_of(c * cw, cw), cw)
                for h in range(2):
                    sbuf[h, rows, :] = _dot(stage[h, rows, :], sums)
                return carry
            lax.fori_loop(0, nchunk, run_, 0)

        sum_pass()

        def g_pass(c, carry):
            runs = list(carry)
            for j, blk_ in blocks(c):
                mask = causal(j)
                for h in range(2):
                    upto = sbuf[h, blk_, :]
                    log_a = zbuf[h, :, blk_] + lbuf[h, :, blk_] + (total[h] - runs[h] - upto)
                    a = jnp.where(mask, jnp.exp(log_a), 0.0)
                    abuf[h, :, blk_] = a.astype(BF16)
                    g = a * dabuf[h, :, blk_]
                    dabuf[h, :, blk_] = g
                    stage[h, blk_, :] = g.astype(BF16)
                    runs[h] = runs[h] + _lanes(upto[:, BLK - 1:BLK])
            return tuple(runs)

        zero = jnp.zeros((BLK, BLK), F32)
        lax.fori_loop(0, nchunk, g_pass, (zero, zero))
        sum_pass()

        def dz_pass(c, carry):
            runs = list(carry)
            for j, blk_ in blocks(c):
                mask = causal(j)
                for h in range(2):
                    lm = lbuf[h, :, blk_]
                    g = dabuf[h, :, blk_]
                    upto = sbuf[h, blk_, :]
                    before = runs[h] + upto - g
                    dz = jnp.where(mask, g * jnp.exp(lm) - jnp.exp(zbuf[h, :, blk_] + lm) * before, 0.0)
                    dzbuf[h, :, blk_] = (dz * SCALE).astype(BF16)
                    runs[h] = runs[h] + _lanes(upto[:, BLK - 1:BLK])
            return tuple(runs)

        lax.fori_loop(0, nchunk, dz_pass, (zero, zero))
        for h in range(2):
            dq_ref[:, hsl[h]] = _dot(dzbuf[h], kb[:, hsl[h]]).astype(dq_ref.dtype)
        q_t = q_ref[...].T.astype(BF16)
        do_t = do_ref[...].T.astype(BF16)
        for h in range(2):
            dkt[hsl[h], :] += _dot(q_t[hsl[h], :], dzbuf[h])
            dvt[hsl[h], :] += _dot(do_t[hsl[h], :], abuf[h])

        @pl.when(qi == nq - 1)
        def _():
            dk_ref[...] = dkt[...].T.astype(dk_ref.dtype)
            dv_ref[...] = dvt[...].T.astype(dv_ref.dtype)

        if fused:
            pl.when(jnp.logical_and(hp == nh - 1, qi == nq - 1))(ex.finish)

    blk = lambda h, i: (i, h)
    whole = lambda h, i: (0, h)
    hbm = pl.BlockSpec(memory_space=pl.ANY)
    in_specs = [pl.BlockSpec((BLK, LANES), lambda h, i: (i, COL_BQ // LANES + h)),
                pl.BlockSpec((t, LANES), lambda h, i: (0, h)),
                pl.BlockSpec((t, LANES), lambda h, i: (0, B_W // LANES + h)),
                pl.BlockSpec((BLK, LANES), blk), pl.BlockSpec((BLK, LANES), blk)]
    out_specs = [pl.BlockSpec((BLK, LANES), blk), pl.BlockSpec((t, LANES), whole), pl.BlockSpec((t, LANES), whole)]
    out_shape = [jax.ShapeDtypeStruct((t, B_W), BF16)] * 3
    scratch = [pltpu.VMEM((2, BLK, t), F32), pltpu.VMEM((2, BLK, t), F32), pltpu.VMEM((2, BLK, t), F32),
               pltpu.VMEM((2, t, BLK), BF16), pltpu.VMEM((2, t, BLK), F32), pltpu.VMEM((2, BLK, t), BF16),
               pltpu.VMEM((2, BLK, t), BF16), pltpu.VMEM((LANES, t), F32), pltpu.VMEM((LANES, t), F32)]
    if fused:
        out_shape = out_shape + [jax.ShapeDtypeStruct(e.shape, e.dtype) for e in exchange]
    return pl.pallas_call(
        body, name=name, grid=(nh, nq),
        in_specs=in_specs + [hbm] * n,
        out_specs=out_specs + [hbm] * n,
        out_shape=out_shape,
        scratch_shapes=scratch + (_comm_sems(n) if fused else []),
        compiler_params=_cparams(("arbitrary", "arbitrary")),
    )(p, kv, kv, tot, do, *(exchange if fused else []))


def _alibi_slopes(n):
    def pow2(m):
        start = 2.0 ** (-8.0 / m)
        return [start ** (i + 1) for i in range(m)]
    if math.log2(n).is_integer():
        s = pow2(n)
    else:
        c = 2 ** int(math.floor(math.log2(n)))
        s = pow2(c) + pow2(2 * c)[0::2][: n - c]
    return sorted(s, reverse=True)


def _dil_scores(qh, kh, sl, prev, exists=None):
    row = lax.broadcasted_iota(jnp.int32, (BLK, BLK), 0)
    col = lax.broadcasted_iota(jnp.int32, (BLK, BLK), 1)
    dist = row - col + (BLK if prev else 0)
    if prev:
        valid = (col - row) >= jnp.where(exists, 0, 2 * BLK)
    else:
        valid = col <= row
    s = _dot_nt(qh, kh) - sl * dist.astype(F32)
    return s, valid


DIL_UNITS = 2


def _dil_plan(r):
    per_trip = min(r, DIL_UNITS)
    return per_trip, DIL_UNITS // per_trip


def _dil_rows(b, rho, r):
    return pl.ds(b * BLK * r + rho, BLK, stride=r) if r > 1 else pl.ds(b * BLK, BLK)


def _dil_fwd(p, gi, name):
    t = p.shape[0]
    _, r = C_GROUPS[gi]
    per_trip, nsub = _dil_plan(r)
    sbr = BLK * r * nsub
    nsb = t // sbr
    slope_cols = _slope_cols(gi)

    def body(q_ref, kc_ref, kp_ref, vc_ref, vp_ref, sl_ref, o_ref, lse_ref):
        i = pl.program_id(1)

        hsl = [slice(HDIM * h, HDIM * (h + 1)) for h in range(2)]
        sl = [sl_ref[:, HDIM * h:HDIM * h + 1] for h in range(2)]

        def residues(it, carry):
            pairs = [(b, dr) for b in range(nsub) for dr in range(per_trip)]
            units = [(pr, h) for pr in pairs for h in range(2)]
            rows = {(b, dr): _dil_rows(b, it * per_trip + dr, r) for b, dr in pairs}
            blocks, prev_exists = {}, {}
            for b, dr in pairs:
                rw = rows[b, dr]
                if b == 0:
                    before = _dil_rows(nsub - 1, it * per_trip + dr, r)
                    kp, vp, prev_exists[b, dr] = kp_ref[before, :], vp_ref[before, :], i > 0
                else:
                    before = rows[b - 1, dr]
                    kp, vp, prev_exists[b, dr] = kc_ref[before, :], vc_ref[before, :], True
                blocks[b, dr] = [q_ref[rw, :], kc_ref[rw, :], kp, vc_ref[rw, :], vp]
            qh = {u: (blocks[u[0]][0][:, hsl[u[1]]] * SCALE).astype(BF16) for u in units}
            sc = {u: _dil_scores(qh[u], blocks[u[0]][1][:, hsl[u[1]]].astype(BF16), sl[u[1]], False) for u in units}
            sp = {u: _dil_scores(qh[u], blocks[u[0]][2][:, hsl[u[1]]].astype(BF16), sl[u[1]], True, prev_exists[u[0]])
                  for u in units}
            pc, pp, den, lse = {}, {}, {}, {}
            for u in units:
                s_c = jnp.where(sc[u][1], sc[u][0], NEG_BIG)
                s_p = jnp.where(sp[u][1], sp[u][0], NEG_BIG)
                m = jnp.maximum(jnp.max(s_c, axis=1, keepdims=True), jnp.max(s_p, axis=1, keepdims=True))
                pc[u] = jnp.exp(s_c - m)
                pp[u] = jnp.exp(s_p - m)
                den[u] = jnp.sum(pc[u], axis=1, keepdims=True) + jnp.sum(pp[u], axis=1, keepdims=True)
                lse[u] = jnp.broadcast_to(m + jnp.log(den[u]), (BLK, HDIM))
            o = {u: (_dot(pc[u].astype(BF16), blocks[u[0]][3][:, hsl[u[1]]].astype(BF16))
                     + _dot(pp[u].astype(BF16), blocks[u[0]][4][:, hsl[u[1]]].astype(BF16))) / den[u] for u in units}
            for pr in pairs:
                o_ref[rows[pr], :] = jnp.concatenate([o[pr, 0], o[pr, 1]], axis=1)
                lse_ref[rows[pr], :] = jnp.concatenate([lse[pr, 0], lse[pr, 1]], axis=1)
            return carry

        lax.fori_loop(0, r // per_trip, residues, 0)

    def at(col0, pick):
        return pl.BlockSpec((sbr, LANES), lambda c, i: (pick(i), col0 // LANES + c))

    cur = lambda i: i
    prv = lambda i: jnp.maximum(i - 1, 0)
    cq, ck, cv = COL_CQ + gi * C_OUT, COL_CK + gi * C_OUT, COL_CV + gi * C_OUT
    out = pl.BlockSpec((sbr, LANES), lambda c, i: (i, c))
    return pl.pallas_call(
        body, name=name, grid=(C_OUT // LANES, nsb),
        in_specs=[at(cq, cur), at(ck, cur), at(ck, prv), at(cv, cur), at(cv, prv),
                  pl.BlockSpec((1, LANES), lambda c, i: (0, c))],
        out_specs=[out, out], out_shape=[jax.ShapeDtypeStruct((t, C_OUT), F32)] * 2,
        compiler_params=_cparams(("parallel", "parallel")),
    )(p, p, p, p, p, slope_cols)


def _dil_bwd(p, do, o, lse, gi, name):
    t = p.shape[0]
    _, r = C_GROUPS[gi]
    per_trip, nsub = _dil_plan(r)
    sbr = BLK * r * nsub
    nsb = t // sbr
    slope_cols = _slope_cols(gi)

    def body(q_ref, qn_ref, kc_ref, kp_ref, vc_ref, vp_ref, do_ref, don_ref, o_ref, on_ref, l_ref, ln_ref, sl_ref,
             dq_ref, dk_ref, dv_ref):
        i = pl.program_id(1)

        hsl = [slice(HDIM * h, HDIM * (h + 1)) for h in range(2)]
        sl = [sl_ref[:, HDIM * h:HDIM * h + 1] for h in range(2)]

        def residues(it, carry):
            pairs = [(b, dr) for b in range(nsub) for dr in range(per_trip)]
            units = [(pr, h) for pr in pairs for h in range(2)]
            rows = {(b, dr): _dil_rows(b, it * per_trip + dr, r) for b, dr in pairs}
            blocks, has_prev, has_next = {}, {}, {}
            for b, dr in pairs:
                rw = rows[b, dr]
                if b == 0:
                    before = _dil_rows(nsub - 1, it * per_trip + dr, r)
                    kp, vp, has_prev[b, dr] = kp_ref[before, :], vp_ref[before, :], i > 0
                else:
                    kp, vp, has_prev[b, dr] = kc_ref[rows[b - 1, dr], :], vc_ref[rows[b - 1, dr], :], True
                if b == nsub - 1:
                    after = _dil_rows(0, it * per_trip + dr, r)
                    nxt = [ref[after, :] for ref in (qn_ref, don_ref, on_ref, ln_ref)]
                    has_next[b, dr] = i < nsb - 1
                else:
                    nxt = [ref[rows[b + 1, dr], :] for ref in (q_ref, do_ref, o_ref, l_ref)]
                    has_next[b, dr] = True
                blocks[b, dr] = [q_ref[rw, :], nxt[0], kc_ref[rw, :], kp, vc_ref[rw, :], vp, do_ref[rw, :], nxt[1],
                                 o_ref[rw, :], nxt[2], l_ref[rw, :], nxt[3]]
            part = lambda u, k: blocks[u[0]][k][:, hsl[u[1]]]
            qb = {u: part(u, 0).astype(BF16) for u in units}
            qnb = {u: part(u, 1).astype(BF16) for u in units}
            qh = {u: (part(u, 0) * SCALE).astype(BF16) for u in units}
            qnh = {u: (part(u, 1) * SCALE).astype(BF16) for u in units}
            kc = {u: part(u, 2).astype(BF16) for u in units}
            kp = {u: part(u, 3).astype(BF16) for u in units}
            vc = {u: part(u, 4).astype(BF16) for u in units}
            vp = {u: part(u, 5).astype(BF16) for u in units}
            dob = {u: part(u, 6).astype(BF16) for u in units}
            donb = {u: part(u, 7).astype(BF16) for u in units}
            delta = {u: jnp.sum(part(u, 6) * part(u, 8), axis=1, keepdims=True) for u in units}
            deltan = {u: jnp.sum(part(u, 7) * part(u, 9), axis=1, keepdims=True) for u in units}
            lse_c = {u: part(u, 10)[:, :1] for u in units}
            lse_n = {u: part(u, 11)[:, :1] for u in units}
            s_cc = {u: _dil_scores(qh[u], kc[u], sl[u[1]], False) for u in units}
            s_cp = {u: _dil_scores(qh[u], kp[u], sl[u[1]], True, has_prev[u[0]]) for u in units}
            s_nc = {u: _dil_scores(qnh[u], kc[u], sl[u[1]], True, has_next[u[0]]) for u in units}
            da_cc = {u: _dot_nt(dob[u], vc[u]) for u in units}
            da_cp = {u: _dot_nt(dob[u], vp[u]) for u in units}
            da_nc = {u: _dot_nt(donb[u], vc[u]) for u in units}

            def prob(s_ok, lse_col):
                s, ok = s_ok
                return jnp.where(ok, jnp.exp(jnp.where(ok, s, NEG_BIG) - lse_col), 0.0)

            p_cc = {u: prob(s_cc[u], lse_c[u]) for u in units}
            p_cp = {u: prob(s_cp[u], lse_c[u]) for u in units}
            p_nc = {u: prob(s_nc[u], lse_n[u]) for u in units}
            ds_cc = {u: (p_cc[u] * (da_cc[u] - delta[u]) * SCALE).astype(BF16) for u in units}
            ds_cp = {u: (p_cp[u] * (da_cp[u] - delta[u]) * SCALE).astype(BF16) for u in units}
            ds_nc = {u: (p_nc[u] * (da_nc[u] - deltan[u]) * SCALE).astype(BF16) for u in units}
            dq = {u: _dot(ds_cc[u], kc[u]) + _dot(ds_cp[u], kp[u]) for u in units}
            dk = {u: _dot_tn(ds_cc[u], qb[u]) + _dot_tn(ds_nc[u], qnb[u]) for u in units}
            dv = {u: _dot_tn(p_cc[u].astype(BF16), dob[u]) + _dot_tn(p_nc[u].astype(BF16), donb[u]) for u in units}
            for pr in pairs:
                dq_ref[rows[pr], :] = jnp.concatenate([dq[pr, 0], dq[pr, 1]], axis=1)
                dk_ref[rows[pr], :] = jnp.concatenate([dk[pr, 0], dk[pr, 1]], axis=1)
                dv_ref[rows[pr], :] = jnp.concatenate([dv[pr, 0], dv[pr, 1]], axis=1)
            return carry

        lax.fori_loop(0, r // per_trip, residues, 0)

    def at(col0, pick):
        return pl.BlockSpec((sbr, LANES), lambda c, i: (pick(i), col0 // LANES + c))

    cur = lambda i: i
    prv = lambda i: jnp.maximum(i - 1, 0)
    nxt = lambda i: jnp.minimum(i + 1, nsb - 1)
    cq, ck, cv = COL_CQ + gi * C_OUT, COL_CK + gi * C_OUT, COL_CV + gi * C_OUT
    return pl.pallas_call(
        body, name=name, grid=(C_OUT // LANES, nsb),
        in_specs=[at(cq, cur), at(cq, nxt), at(ck, cur), at(ck, prv), at(cv, cur), at(cv, prv),
                  at(0, cur), at(0, nxt), at(0, cur), at(0, nxt), at(0, cur), at(0, nxt),
                  pl.BlockSpec((1, LANES), lambda c, i: (0, c))],
        out_specs=[at(0, cur)] * 3, out_shape=[jax.ShapeDtypeStruct((t, C_OUT), F32)] * 3,
        compiler_params=_cparams(("parallel", "parallel")),
    )(p, p, p, p, p, p, do, do, o, o, lse, lse, slope_cols)


def _dil_merge(os_, ls_, name):
    t, w = os_[0].shape
    tr = _rows(t)

    def body(o0, o1, o2, l0, l1, l2, y_ref, lse_ref):
        a, b, c = l0[...], l1[...], l2[...]
        m = jnp.maximum(jnp.maximum(a, b), c)
        ea, eb, ec = jnp.exp(a - m), jnp.exp(b - m), jnp.exp(c - m)
        den = ea + eb + ec
        y_ref[...] = (ea * o0[...] + eb * o1[...] + ec * o2[...]) / den
        lse_ref[...] = m + jnp.log(den)

    row = pl.BlockSpec((tr, w), lambda i: (i, 0))
    return pl.pallas_call(
        body, name=name, grid=(t // tr,), in_specs=[row] * 6, out_specs=[row, row],
        out_shape=[jax.ShapeDtypeStruct((t, w), F32)] * 2, compiler_params=_cparams(("parallel",)),
    )(*os_, *ls_)


def _gate_fwd(ys, gl, ws, name):
    t = gl.shape[0]
    d = gl.shape[1] // N_BRANCH
    tr = _rows(t)

    def body(ya, yb, yc, gl_ref, wa, wb, wc, m_ref):
        acc = None
        for i, (y, w) in enumerate(((ya, wa), (yb, wb), (yc, wc))):
            z = _dot(y[...].astype(BF16), w[...])
            term = jax.nn.sigmoid(gl_ref[:, i * d:(i + 1) * d]) * z
            acc = term if acc is None else acc + term
        m_ref[...] = acc.astype(m_ref.dtype)

    rows = [pl.BlockSpec((tr, y.shape[1]), lambda i: (i, 0)) for y in ys]
    wsp = [pl.BlockSpec(w.shape, lambda i: (0, 0)) for w in ws]
    return pl.pallas_call(
        body, name=name, grid=(t // tr,),
        in_specs=rows + [pl.BlockSpec((tr, N_BRANCH * d), lambda i: (i, 0))] + wsp,
        out_specs=pl.BlockSpec((tr, d), lambda i: (i, 0)), out_shape=jax.ShapeDtypeStruct((t, d), BF16),
        compiler_params=_cparams(("parallel",)),
    )(*ys, gl, *ws)


def _gate_bwd(dm, ys, gl, ws, name):
    t = gl.shape[0]
    d = gl.shape[1] // N_BRANCH
    tr = _rows(t)

    def body(dm_ref, ya, yb, yc, gl_ref, wa, wb, wc, dya, dyb, dyc, dgl_ref, dwa, dwb, dwc):
        step = pl.program_id(0)
        dmv = dm_ref[...].astype(F32)
        for i, (y, w, dy, dw) in enumerate(((ya, wa, dya, dwa), (yb, wb, dyb, dwb), (yc, wc, dyc, dwc))):
            yb16 = y[...].astype(BF16)
            z = _dot(yb16, w[...])
            sg = jax.nn.sigmoid(gl_ref[:, i * d:(i + 1) * d])
            dgl_ref[:, i * d:(i + 1) * d] = (dmv * z * sg * (1.0 - sg)).astype(dgl_ref.dtype)
            e = (dmv * sg).astype(BF16)
            dy[...] = _dot_nt(e, w[...])
            contrib = _dot_tn(yb16, e)

            @pl.when(step == 0)
            def _(dw=dw, contrib=contrib):
                dw[...] = contrib

            @pl.when(step > 0)
            def _(dw=dw, contrib=contrib):
                dw[...] += contrib

    rows = [pl.BlockSpec((tr, y.shape[1]), lambda i: (i, 0)) for y in ys]
    wsp = [pl.BlockSpec(w.shape, lambda i: (0, 0)) for w in ws]
    gsp = pl.BlockSpec((tr, N_BRANCH * d), lambda i: (i, 0))
    return pl.pallas_call(
        body, name=name, grid=(t // tr,),
        in_specs=[pl.BlockSpec((tr, d), lambda i: (i, 0))] + rows + [gsp] + wsp,
        out_specs=rows + [gsp] + wsp,
        out_shape=[jax.ShapeDtypeStruct(y.shape, F32) for y in ys] + [jax.ShapeDtypeStruct(gl.shape, BF16)]
        + [jax.ShapeDtypeStruct(w.shape, F32) for w in ws],
        compiler_params=_cparams(("arbitrary",)),
    )(dm, *ys, gl, *ws)


def _adamw(w, m, v, gparts, name):
    depth = len(gparts)
    n, r, c = gparts[0].shape
    br = max(b for b in range(8, min(r, LANES) + 1, 8) if r % b == 0) if r % 8 == 0 else r
    nb = r // br
    c1 = 1.0 - ADAM_B1 ** ADAM_STEP
    c2 = 1.0 - ADAM_B2 ** ADAM_STEP

    def body(w_ref, m_ref, v_ref, *rest):
        g_refs, (go_ref, d_ref, mo_ref, vo_ref) = rest[:depth], rest[depth:]
        li = pl.program_id(0)

        def update(g_ref):
            g = g_ref[0].astype(F32)
            for i in range(1, n):
                g = g + g_ref[i].astype(F32)
            mn = ADAM_B1 * m_ref[...] + (1.0 - ADAM_B1) * g
            vn = ADAM_B2 * v_ref[...] + (1.0 - ADAM_B2) * (g * g)
            go_ref[...] = g
            mo_ref[...] = mn
            vo_ref[...] = vn
            d_ref[...] = -ADAM_LR * ((mn / c1) / (jnp.sqrt(vn / c2) + ADAM_EPS) + ADAM_WD * w_ref[...])

        for l in range(depth):
            pl.when(li == l)(functools.partial(update, g_refs[l]))

    def g_spec(l):
        return pl.BlockSpec((n, br, c), lambda li, i: (0, jnp.where(li == l, i, jnp.where(li < l, 0, nb - 1)), 0))

    blk = pl.BlockSpec((br, c), lambda li, i: (li * nb + i, 0))
    return pl.pallas_call(
        body, name=name, grid=(depth, nb),
        in_specs=[blk, blk, blk] + [g_spec(l) for l in range(depth)],
        out_specs=[blk] * 4, out_shape=[jax.ShapeDtypeStruct((depth * r, c), F32)] * 4,
        compiler_params=_cparams(("arbitrary", "arbitrary")),
    )(w, m, v, *gparts)


def _my_coords():
    return lax.axis_index("x"), lax.axis_index("y"), lax.axis_index("c")


COMM_SEMS = [pltpu.SemaphoreType.DMA((7,)), pltpu.SemaphoreType.DMA((7,)), pltpu.SemaphoreType.DMA]


class _Gather:
    def __init__(self, x_ref, out_ref, send_sems, recv_sems, local_sem):
        self.x_ref, self.out_ref = x_ref, out_ref
        self.send_sems, self.recv_sems, self.local_sem = send_sems, recv_sems, local_sem
        self.m_per = x_ref.shape[0]
        x, y, c = _my_coords()
        self.c = c
        self.me, self.sibling = (x, y, c), (x, y, 1 - c)
        self.chips = [(1 - x, y), (x, 1 - y), (1 - x, 1 - y)]

    def rows(self, px, py, pc):
        return self.out_ref.at[pl.ds((4 * px + 2 * py + pc) * self.m_per, self.m_per), :]

    def copy(self, k, block, to, src=None):
        return pltpu.make_async_remote_copy(
            src_ref=self.rows(*block) if src is None else src, dst_ref=self.rows(*block),
            send_sem=self.send_sems.at[k], recv_sem=self.recv_sems.at[k], device_id=to, device_id_type=MESH)

    def mine(self):
        return pltpu.make_async_copy(self.x_ref, self.rows(*self.me), self.local_sem)

    def first(self):
        out = [self.copy(0, self.me, self.sibling, src=self.x_ref)]
        return out + [self.copy(1 + j, self.me, (*chip, self.c), src=self.x_ref) for j, chip in enumerate(self.chips)]

    def passed(self):
        return [self.copy(4 + j, (*chip, self.c), self.sibling) for j, chip in enumerate(self.chips)]

    def start(self):
        self.mine().start()
        for cp in self.first():
            cp.start()

    def forward(self):
        passed = self.passed()
        for j, chip in enumerate(self.chips):
            self.copy(1 + j, (*chip, self.c), self.me).wait_recv()
            passed[j].start()

    def finish(self):
        self.copy(0, self.sibling, self.me).wait_recv()
        for j, chip in enumerate(self.chips):
            self.copy(4 + j, (*chip, 1 - self.c), self.me).wait_recv()
        for cp in self.first() + self.passed():
            cp.wait_send()
        self.mine().wait()


class _Exchange:
    def __init__(self, send_ref, recv_ref, send_sems, recv_sems, local_sem):
        self.send_ref, self.recv_ref = send_ref, recv_ref
        self.send_sems, self.recv_sems, self.local_sem = send_sems, recv_sems, local_sem
        x, y, c = _my_coords()
        self.me = 4 * x + 2 * y + c
        self.peers = []
        for k in range(1, N_DEV):
            px = 1 - x if k & 4 else x
            py = 1 - y if k & 2 else y
            pc = 1 - c if k & 1 else c
            self.peers.append((4 * px + 2 * py + pc, (px, py, pc)))

    def mine(self):
        return pltpu.make_async_copy(self.send_ref.at[self.me], self.recv_ref.at[self.me], self.local_sem)

    def copy(self, k, src_slot, dst_slot):
        return pltpu.make_async_remote_copy(
            src_ref=self.send_ref.at[src_slot], dst_ref=self.recv_ref.at[dst_slot],
            send_sem=self.send_sems.at[k], recv_sem=self.recv_sems.at[k],
            device_id=self.peers[k][1], device_id_type=MESH)

    def start(self):
        self.mine().start()
        for k, (peer, _) in enumerate(self.peers):
            self.copy(k, peer, self.me).start()

    def finish(self):
        for k, (peer, _) in enumerate(self.peers):
            self.copy(k, peer, self.me).wait_send()
            self.copy(k, self.me, peer).wait_recv()
        self.mine().wait()


def _all_gather(x_shard, in_vmem, with_sum, name):
    m_per, n = x_shard.shape

    def body(x_ref, out_ref, *rest):
        if with_sum:
            sum_ref, send_sems, recv_sems, local_sem = rest
        else:
            send_sems, recv_sems, local_sem = rest
        g = _Gather(x_ref, out_ref, send_sems, recv_sems, local_sem)
        g.start()
        g.forward()
        g.finish()
        if with_sum:
            acc = out_ref[pl.ds(0, m_per), :]
            for d in range(1, N_DEV):
                acc = acc + out_ref[pl.ds(d * m_per, m_per), :]
            sum_ref[...] = acc

    space = pltpu.VMEM if in_vmem else pl.ANY
    out_shape = [jax.ShapeDtypeStruct((N_DEV * m_per, n), x_shard.dtype)]
    out_specs = [pl.BlockSpec(memory_space=space)]
    if with_sum:
        out_shape.append(jax.ShapeDtypeStruct((m_per, n), x_shard.dtype))
        out_specs.append(pl.BlockSpec(memory_space=pltpu.VMEM))
    res = pl.pallas_call(
        body, name=name, out_shape=out_shape, in_specs=[pl.BlockSpec(memory_space=space)], out_specs=out_specs,
        scratch_shapes=COMM_SEMS, compiler_params=pltpu.CompilerParams(vmem_limit_bytes=VMEM_LIMIT),
    )(x_shard)
    return res if with_sum else res[0]


def _comm_sems(n):
    return [pltpu.SemaphoreType.DMA((n, 7)), pltpu.SemaphoreType.DMA((n, 7)), pltpu.SemaphoreType.DMA((n,))]


class _Many:
    def __init__(self, kind, ins, outs, send_sems, recv_sems, local_sems):
        self.parts = [kind(i, o, send_sems.at[b], recv_sems.at[b], local_sems.at[b])
                      for b, (i, o) in enumerate(zip(ins, outs))]

    def start(self):
        for part in self.parts:
            part.start()

    def forward(self):
        for part in self.parts:
            part.forward()

    def finish(self):
        for part in self.parts:
            part.finish()


def _gathered_shapes(shards):
    return [jax.ShapeDtypeStruct((N_DEV * s.shape[0],) + s.shape[1:], s.dtype) for s in shards]


def _all_gather_many(shards, name):
    n = len(shards)

    def body(*refs):
        g = _Many(_Gather, refs[:n], refs[n:2 * n], *refs[2 * n:])
        g.start()
        g.forward()
        g.finish()

    hbm = pl.BlockSpec(memory_space=pl.ANY)
    return pl.pallas_call(body, name=name, out_shape=_gathered_shapes(shards), in_specs=[hbm] * n,
                          out_specs=[hbm] * n, scratch_shapes=_comm_sems(n))(*shards)


def _all_to_all_many(sends, name):
    n = len(sends)

    def body(*refs):
        ex = _Many(_Exchange, refs[:n], refs[n:2 * n], *refs[2 * n:])
        ex.start()
        ex.finish()

    hbm = pl.BlockSpec(memory_space=pl.ANY)
    return pl.pallas_call(body, name=name, out_shape=[jax.ShapeDtypeStruct(s.shape, s.dtype) for s in sends],
                          in_specs=[hbm] * n, out_specs=[hbm] * n, scratch_shapes=_comm_sems(n))(*sends)


def _row(v):
    return v.reshape(1, -1)


def _ffn_fwd(x, w_in, w_out, g_pre, g_post, m, res_w, tag):
    shift, scale, gate = m[0], m[1], m[2]
    mpre = _row(g_pre * (1.0 + scale))
    mpost = _row(res_w * gate * g_post)
    h = _rms_fwd(x, mpre, _row(shift), None, BF16, tag + "_pre")
    ua, ub, s = _ffn_in_swiglu(h, w_in, tag + "_in")
    y = _matmul(s, w_out, name=tag + "_out")
    x_new = _rms_fwd(y, mpost, jnp.zeros_like(mpost), x, F32, tag + "_post")
    return x_new, (x, h, ua, ub, s, y, mpre, mpost)


def _sub_bwd_post(dx_new, y, mpost, g_post, gate, res_w, tag):
    dy, c1, _ = _rms_bwd(dx_new, y, mpost, None, BF16, tag + "_post_bwd")
    c1 = c1[0]
    return dy, c1 * res_w * g_post, c1 * res_w * gate


def _sub_bwd_pre(dh, x, mpre, dx_new, g_pre, scale, tag):
    dx, c2, c3 = _rms_bwd(dh, x, mpre, dx_new, F32, tag + "_pre_bwd")
    c2, c3 = c2[0], c3[0]
    return dx, c3, c2 * g_pre, c2 * (1.0 + scale)


def _ffn_bwd(dx_new, saved, w_in, w_out, g_pre, g_post, m, res_w, tag):
    x, h, ua, ub, s, y, mpre, mpost = saved
    scale, gate = m[1], m[2]
    dy, dgate, dg_post = _sub_bwd_post(dx_new, y, mpost, g_post, gate, res_w, tag)
    ds = _matmul(dy, w_out, tb=True, out_dtype=BF16, name=tag + "_out_dx")
    dw_out = _matmul(s, dy, ta=True, out_dtype=BF16, name=tag + "_out_dw")
    du = _swiglu_bwd(ua, ub, ds, tag + "_act_bwd")
    dh = _matmul(du, w_in, tb=True, name=tag + "_in_dx")
    dw_in = _matmul(h, du, ta=True, out_dtype=BF16, name=tag + "_in_dw")
    dx, dshift, dscale, dg_pre = _sub_bwd_pre(dh, x, mpre, dx_new, g_pre, scale, tag)
    return dx, dw_in, dw_out, jnp.stack([dshift, dscale, dgate]), dg_pre, dg_post


def _slope_cols(gi):
    _, r = C_GROUPS[gi]
    sl = jnp.asarray(_alibi_slopes(C_HEADS)[gi * C_HPG:(gi + 1) * C_HPG], F32) * float(r)
    return jnp.repeat(sl, HDIM).reshape(1, C_OUT)


def _mix_fwd(x, w, g_pre, g_post, m, lb, hn, tag, gather=None):
    t, d = x.shape
    shift, scale, gate = m[0], m[1], m[2]
    mpre = _row(g_pre * (1.0 + scale))
    mpost = _row(gate * g_post)
    h = _rms_fwd(x, mpre, _row(shift), None, BF16, tag + "_pre")
    p = _matmul(h, w["w_in"], name=tag + "_in")
    hn2 = _row(jnp.tile(hn, 2))
    ya, oa, states = _hgrn_fwd(p, _row(lb), hn2, tag + "_hgrn")
    kv = p[:, COL_BK:COL_CQ].astype(BF16)
    if gather is None:
        (yb, sb_tot), gathered = _sb_fwd(p, kv, tag + "_sb"), None
    else:
        res = _sb_fwd(p, kv, tag + "_sb_gather", gather)
        yb, sb_tot, gathered = res[0], res[1], list(res[2:])
    og, lg = zip(*[_dil_fwd(p, gi, tag + "_dil%d" % gi) for gi in range(len(C_GROUPS))])
    yc, lse_c = _dil_merge(og, lg, tag + "_dil_merge")
    gl = p[:, COL_GATE:]
    ws = (w["w_branch_a"], w["w_branch_b"], w["w_branch_c"])
    merged = _gate_fwd((ya, yb, yc), gl, ws, tag + "_gate")
    y = _matmul(merged, w["w_out"], name=tag + "_out")
    x_new = _rms_fwd(y, mpost, jnp.zeros_like(mpost), x, F32, tag + "_post")
    return x_new, (x, h, p, hn2, ya, oa, states, yb, kv, sb_tot, yc, lse_c, gl, merged, y, mpre, mpost), gathered


def _mix_bwd(dx_new, saved, w, g_pre, g_post, m, lb, tag, exchange=None):
    x, h, p, hn2, ya, oa, states, yb, kv, sb_tot, yc, lse_c, gl, merged, y, mpre, mpost = saved
    t = x.shape[0]
    scale, gate = m[1], m[2]
    dy, dgate, dg_post = _sub_bwd_post(dx_new, y, mpost, g_post, gate, 1.0, tag)
    dmerged = _matmul(dy, w["w_out"], tb=True, out_dtype=BF16, name=tag + "_out_dx")
    dw_out = _matmul(merged, dy, ta=True, out_dtype=BF16, name=tag + "_out_dw")
    ws = (w["w_branch_a"], w["w_branch_b"], w["w_branch_c"])
    dya, dyb, dyc, dgl, dwa, dwb, dwc = _gate_bwd(dmerged, (ya, yb, yc), gl, ws, tag + "_gate_bwd")
    dqa, dfa, dia, dga, dlb, dhn = _hgrn_bwd(p, _row(lb), hn2, oa, states, dya, tag + "_hgrn_bwd")
    if exchange is None:
        (dbq, dbk, dbv), received = _sb_bwd(p, kv, sb_tot, dyb, tag + "_sb_bwd"), None
    else:
        res = _sb_bwd(p, kv, sb_tot, dyb, tag + "_sb_bwd_exchange", exchange)
        dbq, dbk, dbv, received = res[0], res[1], res[2], list(res[3:])
    dcq, dck, dcv = zip(*[_dil_bwd(p, dyc, yc, lse_c, gi, tag + "_dil%d_bwd" % gi) for gi in range(len(C_GROUPS))])
    dil = [g.astype(BF16) for g in (*dcq, *dck, *dcv)]
    dp = jnp.concatenate([dqa, dfa, dia, dga, dbq, dbk, dbv, *dil, dgl], axis=1)
    dh = _matmul(dp, w["w_in"], tb=True, name=tag + "_in_dx")
    dw_in = _matmul(h, dp, ta=True, out_dtype=BF16, name=tag + "_in_dw")
    dx, dshift, dscale, dg_pre = _sub_bwd_pre(dh, x, mpre, dx_new, g_pre, scale, tag)
    dhn_v = jnp.sum(dhn, axis=(0, 1))
    dhn_v = dhn_v[:A_VDIM] + dhn_v[A_VDIM:]
    dws = dict(w_in=dw_in, w_out=dw_out, w_branch_a=dwa.astype(BF16), w_branch_b=dwb.astype(BF16),
               w_branch_c=dwc.astype(BF16))
    return dx, dws, jnp.stack([dshift, dscale, dgate]), dg_pre, dg_post, dlb[0], dhn_v, received


class _LocalWeights:
    def __init__(self, wts):
        self.wts = wts

    def first(self):
        return None

    def shard(self, l):
        return None

    def layer(self, l, gathered):
        return {k: v[l] for k, v in self.wts.items()}

    fused = False

    def pack(self, names, dws):
        return [dws[k] for k in names]

    def last(self, packed):
        return packed


class _ShardedWeights:
    def __init__(self, shards):
        self.shards = shards

    def shard(self, l):
        return [self.shards[k][l].astype(BF16) for k in BIG_WEIGHTS]

    def first(self):
        return _all_gather_many(self.shard(0), "weights_all_gather")

    def layer(self, l, gathered):
        out = {}
        for k, got in zip(BIG_WEIGHTS, gathered):
            _, r, c = self.shards[k].shape
            out[k] = got if k in ROW_SHARDED else got.reshape(N_DEV, r, c).transpose(1, 0, 2).reshape(r, N_DEV * c)
        return out

    fused = True

    def pack(self, names, dws):
        out = []
        for k in names:
            _, r, c = self.shards[k].shape
            g = dws[k]
            out.append(g.reshape(N_DEV, r, c) if k in ROW_SHARDED else g.reshape(r, N_DEV, c).transpose(1, 0, 2))
        return out

    def last(self, packed):
        return _all_to_all_many(packed, "grads_all_to_all")


def _local_step(x, target, mod, norm_g, lb_all, hnorm, supply):
    depth = mod.shape[0]
    d = x.shape[1]
    saved, wls = [], []
    gathered = supply.first()
    for l in range(depth):
        wl = supply.layer(l, gathered)
        wls.append(wl)
        x, s0 = _ffn_fwd(x, wl["ffn1_w_in"], wl["ffn1_w_out"], norm_g[l, 0], norm_g[l, 1], mod[l, 0], 0.5, "ffn1")
        nxt = supply.shard(l + 1) if l + 1 < depth else None
        x, s1, gathered = _mix_fwd(x, wl, norm_g[l, 2], norm_g[l, 3], mod[l, 1], lb_all[l], hnorm[l], "mix", nxt)
        x, s2 = _ffn_fwd(x, wl["ffn2_w_in"], wl["ffn2_w_out"], norm_g[l, 4], norm_g[l, 5], mod[l, 2], 0.5, "ffn2")
        saved.append((s0, s1, s2))
    dx, sq = _loss_head(x, target, "loss_head")
    loss = 0.5 * jnp.sum(sq) / d
    dmod, dng, dlb, dhn = [], [], [], []
    early = ("ffn2_w_in", "ffn2_w_out")
    late = tuple(k for k in BIG_WEIGHTS if k not in early)
    returned = {}
    waiting = []
    for l in reversed(range(depth)):
        wl = wls[l]
        s0, s1, s2 = saved[l]
        dx, dwi2, dwo2, dm2, dgp2, dgq2 = _ffn_bwd(dx, s2, wl["ffn2_w_in"], wl["ffn2_w_out"], norm_g[l, 4],
                                                   norm_g[l, 5], mod[l, 2], 0.5, "ffn2")
        waiting += zip([(l, k) for k in early], supply.pack(early, dict(ffn2_w_in=dwi2, ffn2_w_out=dwo2)))
        keys, bufs = [k for k, _ in waiting], [b for _, b in waiting]
        dx, dwm, dm1, dgp1, dgq1, dlb_l, dhn_l, received = _mix_bwd(
            dx, s1, wl, norm_g[l, 2], norm_g[l, 3], mod[l, 1], lb_all[l], "mix", bufs if supply.fused else None)
        returned.update(zip(keys, received if supply.fused else bufs))
        dx, dwi1, dwo1, dm0, dgp0, dgq0 = _ffn_bwd(dx, s0, wl["ffn1_w_in"], wl["ffn1_w_out"], norm_g[l, 0],
                                                   norm_g[l, 1], mod[l, 0], 0.5, "ffn1")
        dmod.append(jnp.stack([dm0, dm1, dm2]))
        dng.append(jnp.stack([dgp0, dgq0, dgp1, dgq1, dgp2, dgq2]))
        dlb.append(dlb_l)
        dhn.append(dhn_l)
        waiting = list(zip([(l, k) for k in late], supply.pack(late, dict(dwm, ffn1_w_in=dwi1, ffn1_w_out=dwo1))))
    returned.update(zip([k for k, _ in waiting], supply.last([b for _, b in waiting])))
    rev = lambda lst: jnp.stack(lst[::-1])
    return loss, dx, rev(dmod), rev(dng), rev(dlb), rev(dhn), returned


def _lb_all(logits):
    lb_p = jax.nn.softmax(logits.astype(F32), axis=0)
    return jnp.cumsum(lb_p, axis=0) - lb_p[0:1]


def _pad_rows(a, rows):
    return jnp.pad(a, ((0, rows - a.shape[0]), (0, 0)))


def kernel(x, c, w_ada, b_ada, norm_g, ffn1_w_in, ffn1_w_out, w_in, hgrn_lb_logits, hgrn_norm_g, w_branch_a, w_branch_b, w_branch_c, w_out, ffn2_w_in, ffn2_w_out, loss_target, m_w_ada, m_b_ada, m_norm_g, m_ffn1_w_in, m_ffn1_w_out, m_w_in, m_hgrn_lb_logits, m_hgrn_norm_g, m_w_branch_a, m_w_branch_b, m_w_branch_c, m_w_out, m_ffn2_w_in, m_ffn2_w_out, v_w_ada, v_b_ada, v_norm_g, v_ffn1_w_in, v_ffn1_w_out, v_w_in, v_hgrn_lb_logits, v_hgrn_norm_g, v_w_branch_a, v_w_branch_b, v_w_branch_c, v_w_out, v_ffn2_w_in, v_ffn2_w_out):
    weights = dict(w_ada=w_ada, b_ada=b_ada, norm_g=norm_g, ffn1_w_in=ffn1_w_in, ffn1_w_out=ffn1_w_out, w_in=w_in,
                   hgrn_lb_logits=hgrn_lb_logits, hgrn_norm_g=hgrn_norm_g, w_branch_a=w_branch_a,
                   w_branch_b=w_branch_b, w_branch_c=w_branch_c, w_out=w_out, ffn2_w_in=ffn2_w_in,
                   ffn2_w_out=ffn2_w_out)
    mom1 = dict(w_ada=m_w_ada, b_ada=m_b_ada, norm_g=m_norm_g, ffn1_w_in=m_ffn1_w_in, ffn1_w_out=m_ffn1_w_out,
                w_in=m_w_in, hgrn_lb_logits=m_hgrn_lb_logits, hgrn_norm_g=m_hgrn_norm_g, w_branch_a=m_w_branch_a,
                w_branch_b=m_w_branch_b, w_branch_c=m_w_branch_c, w_out=m_w_out, ffn2_w_in=m_ffn2_w_in,
                ffn2_w_out=m_ffn2_w_out)
    mom2 = dict(w_ada=v_w_ada, b_ada=v_b_ada, norm_g=v_norm_g, ffn1_w_in=v_ffn1_w_in, ffn1_w_out=v_ffn1_w_out,
                w_in=v_w_in, hgrn_lb_logits=v_hgrn_lb_logits, hgrn_norm_g=v_hgrn_norm_g, w_branch_a=v_w_branch_a,
                w_branch_b=v_w_branch_b, w_branch_c=v_w_branch_c, w_out=v_w_out, ffn2_w_in=v_ffn2_w_in,
                ffn2_w_out=v_ffn2_w_out)
    order = list(weights)
    depth, d, ada_cols = w_ada.shape
    nd = d // LANES
    xi, yi, ci = _my_coords()
    me = 4 * xi + 2 * yi + ci

    small = jnp.concatenate([c.reshape(nd, LANES), norm_g.reshape(depth * 6, LANES)], axis=0)
    g1 = _all_gather(small, True, False, "small_all_gather").reshape(N_DEV, small.shape[0], LANES)
    c_act = _silu(g1[:, :nd].reshape(N_DEV, d))
    norm_full = g1[:, nd:].reshape(N_DEV, depth, 6, LANES).transpose(1, 2, 0, 3).reshape(depth, 6, d)

    c_pad = _pad_rows(c_act, 16)
    mod_sh = jnp.stack([_matmul(c_pad, w_ada[l], name="ada_mod")[:N_DEV]
                        + lax.dynamic_slice_in_dim(b_ada[l], me * ada_cols, ada_cols)[None]
                        for l in range(depth)])
    g2 = _all_gather(mod_sh.reshape(-1, LANES), True, False, "mod_all_gather")
    g2 = g2.reshape(N_DEV, depth, N_DEV, ada_cols)
    mod = lax.dynamic_index_in_dim(g2, me, axis=2, keepdims=False)
    mod = mod.transpose(1, 0, 2).reshape(depth, 3, 3, d)

    supply = _ShardedWeights({k: weights[k] for k in BIG_WEIGHTS})
    lb_all, lb_vjp = jax.vjp(_lb_all, hgrn_lb_logits)

    loss, dx, dmod, dng, dlb, dhn, received = _local_step(x[0], loss_target[0], mod, norm_full, lb_all,
                                                          hgrn_norm_g, supply)
    loss = lax.psum(loss, ("x", "y", "c"))

    dhn_pad = jnp.pad(dhn.reshape(-1), (0, 8 * LANES - dhn.size))
    pieces = [dmod.reshape(-1), dng.reshape(-1), dlb.reshape(-1), dhn_pad]
    sizes = [p_.size for p_ in pieces]
    smallg = jnp.concatenate(pieces).reshape(-1, LANES)
    g3, gsum = _all_gather(smallg, True, True, "small_grads_all_gather")
    g3 = g3.reshape(N_DEV, -1)
    gsum = gsum.reshape(-1)
    dmod_all = g3[:, :sizes[0]].reshape(N_DEV, depth, 9 * d)
    o1 = sizes[0]
    grads = {}
    grads["b_ada"] = gsum[:o1].reshape(depth, 9 * d)
    dng_sum = gsum[o1:o1 + sizes[1]].reshape(depth, 6, nd, LANES)
    grads["norm_g"] = lax.dynamic_index_in_dim(dng_sum, me, axis=2, keepdims=False)
    o2 = o1 + sizes[1]
    dlb_sum = gsum[o2:o2 + sizes[2]].reshape(depth, A_QK)
    grads["hgrn_lb_logits"] = lb_vjp(dlb_sum)[0]
    o3 = o2 + sizes[2]
    grads["hgrn_norm_g"] = gsum[o3:o3 + dhn.size].reshape(depth, A_VDIM)
    dmod_mine = lax.dynamic_slice_in_dim(dmod_all, me * ada_cols, ada_cols, axis=2)
    grads["w_ada"] = jnp.stack([_matmul(c_pad, _pad_rows(dmod_mine[:, l], 16), ta=True, name="ada_dw")
                                for l in range(depth)])

    outs = {}
    for k in order:
        w = weights[k]
        w2 = w.reshape(-1, w.shape[-1])
        if k in BIG_WEIGHTS:
            gp = [received[l, k] for l in range(depth)]
        else:
            gp = [grads[k].reshape((1,) + w2.shape)]
        res = _adamw(w2, mom1[k].reshape(w2.shape), mom2[k].reshape(w2.shape), gp, "adamw")
        outs[k] = [r.reshape(w.shape) for r in res]
    return (loss, dx[None], *[outs[k][0] for k in order], *[outs[k][1] for k in order],
            *[outs[k][2] for k in order], *[outs[k][3] for k in order])
```

```python
import functools
import math

import jax
import jax.numpy as jnp
from jax import lax
from jax.experimental import pallas as pl
from jax.experimental.pallas import tpu as pltpu

F32 = jnp.float32
BF16 = jnp.bfloat16

A_HEADS, A_KDIM, A_VDIM, A_CHUNK = 6, 128, 64, 64
B_HEADS, HDIM = 6, 64
C_GROUPS = ((128, 1), (512, 4), (2048, 16))
C_HPG = 4
C_HEADS = C_HPG * len(C_GROUPS)
N_BRANCH = 3
EPS = 1e-6
NEG_BIG = -1e30
TINY = 1e-30
A_QK = A_HEADS * A_KDIM
A_V = A_HEADS * A_VDIM
B_W = B_HEADS * HDIM
C_W = C_HEADS * HDIM
C_OUT = C_HPG * HDIM
COL_AQ, COL_AF, COL_AI, COL_AG = 0, A_QK, 2 * A_QK, 2 * A_QK + A_V
COL_BQ = 2 * A_QK + 2 * A_V
COL_BK, COL_BV = COL_BQ + B_W, COL_BQ + 2 * B_W
COL_CQ = COL_BQ + 3 * B_W
COL_CK, COL_CV = COL_CQ + C_W, COL_CQ + 2 * C_W
COL_GATE = COL_CQ + 3 * C_W

ADAM_LR, ADAM_B1, ADAM_B2, ADAM_EPS, ADAM_WD, ADAM_STEP = 0.001, 0.9, 0.999, 1e-08, 0.01, 10

N_DEV = 8
LANES = 128
VMEM_LIMIT = 48 * 1024 * 1024
MATMUL_VMEM_BUDGET = 28 * 1024 * 1024
SUB = 16
EXP_CLAMP = 80.0
MESH = pl.DeviceIdType.MESH

BIG_WEIGHTS = ("ffn1_w_in", "ffn1_w_out", "w_in", "w_branch_a", "w_branch_b", "w_branch_c", "w_out",
               "ffn2_w_in", "ffn2_w_out")
ROW_SHARDED = ("ffn1_w_out", "w_out", "ffn2_w_out")


def _cparams(sem):
    return pltpu.CompilerParams(dimension_semantics=sem, vmem_limit_bytes=VMEM_LIMIT)


def _tile(n, cap):
    best, t = None, LANES
    while t <= min(n, cap):
        if n % t == 0:
            best = t
        t += LANES
    return best or n


def _rows(t, cap=256):
    r = cap
    while t % r:
        r //= 2
    return r


def _divisors(n):
    return [t for t in range(LANES, n + 1, LANES) if n % t == 0] or [n]


def _matmul_tiles(m, n, k, a_size, b_size, o_size):
    best, best_key = None, None
    for tm in _divisors(m):
        for tn in _divisors(n):
            for tk in _divisors(k):
                if tm > 1024 or tn > 3072 or tk > 4096:
                    continue
                cast = (tm * tk * 2 if a_size > 2 else 0) + (tk * tn * 2 if b_size > 2 else 0)
                need = 2 * (tm * tk * a_size + tk * tn * b_size + tm * tn * o_size) + 2 * tm * tn * 4 + cast
                if need > MATMUL_VMEM_BUDGET:
                    continue
                key = (tm * tn * tk, tk)
                if best_key is None or key > best_key:
                    best, best_key = (tm, tn, tk), key
    return best


def _dot(a, b):
    return jnp.dot(a, b, preferred_element_type=F32)


def _dot_nt(a, b):
    return lax.dot_general(a, b, (((1,), (1,)), ((), ())), preferred_element_type=F32)


def _dot_tn(a, b):
    return lax.dot_general(a, b, (((0,), (0,)), ((), ())), preferred_element_type=F32)


def _split3(x):
    h = x.astype(BF16)
    r = x - h.astype(F32)
    m = r.astype(BF16)
    lo = (r - m.astype(F32)).astype(BF16)
    return h, m, lo


def _ones_left(mat01, x):
    h, m, lo = _split3(x)
    return _dot(mat01, h) + _dot(mat01, m) + _dot(mat01, lo)


def _silu(x):
    return x * jax.nn.sigmoid(x)


def _dsilu(x):
    s = jax.nn.sigmoid(x)
    return s * (1.0 + x * (1.0 - s))


def _matmul(a, b, *, ta=False, tb=False, out_dtype=F32, name):
    if ta:
        kdim, m = a.shape
    else:
        m, kdim = a.shape
    n = b.shape[0] if tb else b.shape[1]
    tm, tn, tk = _matmul_tiles(m, n, kdim, a.dtype.itemsize, b.dtype.itemsize, jnp.dtype(out_dtype).itemsize)
    nk = kdim // tk
    ni, nj = m // tm, n // tn
    a_bytes, b_bytes = m * kdim * a.dtype.itemsize, kdim * n * b.dtype.itemsize
    j_outer = nk == 1 and (b_bytes + a_bytes * nj) < (a_bytes + b_bytes * ni)
    dims = (((0 if ta else 1,), (1 if tb else 0,)), ((), ()))

    def body(a_ref, b_ref, o_ref, *scratch):
        p = lax.dot_general(a_ref[...].astype(BF16), b_ref[...].astype(BF16), dims, preferred_element_type=F32)
        if nk == 1:
            o_ref[...] = p.astype(o_ref.dtype)
            return
        acc = scratch[0]
        k = pl.program_id(2)

        @pl.when(k == 0)
        def _():
            acc[...] = p

        @pl.when(k > 0)
        def _():
            acc[...] += p

        @pl.when(k == nk - 1)
        def _():
            o_ref[...] = acc[...].astype(o_ref.dtype)

    def spec(shape, pick):
        if j_outer:
            return pl.BlockSpec(shape, lambda j, i, k: pick(i, j, k))
        return pl.BlockSpec(shape, lambda i, j, k: pick(i, j, k))

    a_spec = spec((tk, tm), lambda i, j, k: (k, i)) if ta else spec((tm, tk), lambda i, j, k: (i, k))
    b_spec = spec((tn, tk), lambda i, j, k: (j, k)) if tb else spec((tk, tn), lambda i, j, k: (k, j))
    return pl.pallas_call(
        body, name=name, grid=(nj, ni, nk) if j_outer else (ni, nj, nk), in_specs=[a_spec, b_spec],
        out_specs=spec((tm, tn), lambda i, j, k: (i, j)),
        out_shape=jax.ShapeDtypeStruct((m, n), out_dtype),
        scratch_shapes=[pltpu.VMEM((tm, tn), F32)] if nk > 1 else [],
        compiler_params=_cparams(("parallel", "parallel", "arbitrary")),
    )(a, b)


def _rms_fwd(z, mcol, acol, res, out_dtype, name):
    t, d = z.shape
    tr = _rows(t)
    has_res = res is not None

    def body(*refs):
        if has_res:
            z_ref, m_ref, a_ref, r_ref, o_ref = refs
        else:
            z_ref, m_ref, a_ref, o_ref = refs
        zf = z_ref[...]
        r = lax.rsqrt(jnp.mean(zf * zf, axis=-1, keepdims=True) + EPS)
        y = zf * r * m_ref[...] + a_ref[...]
        if has_res:
            y = r_ref[...] + y
        o_ref[...] = y.astype(o_ref.dtype)

    row = pl.BlockSpec((tr, d), lambda i: (i, 0))
    col = pl.BlockSpec((1, d), lambda i: (0, 0))
    ins = [z, mcol, acol] + ([res] if has_res else [])
    return pl.pallas_call(
        body, name=name, grid=(t // tr,), in_specs=[row, col, col] + ([row] if has_res else []),
        out_specs=row, out_shape=jax.ShapeDtypeStruct((t, d), out_dtype),
        compiler_params=_cparams(("parallel",)),
    )(*ins)


def _rms_bwd(d_out, z, mcol, dres, out_dtype, name):
    t, d = z.shape
    tr = _rows(t)
    has_res = dres is not None

    def body(*refs):
        if has_res:
            d_ref, z_ref, m_ref, r_ref, o_ref, s1_ref, s2_ref = refs
        else:
            d_ref, z_ref, m_ref, o_ref, s1_ref, s2_ref = refs
        i = pl.program_id(0)
        zf = z_ref[...]
        r = lax.rsqrt(jnp.mean(zf * zf, axis=-1, keepdims=True) + EPS)
        zh = zf * r
        df = d_ref[...].astype(F32)
        dzh = df * m_ref[...]
        dz = r * (dzh - zh * jnp.mean(dzh * zh, axis=-1, keepdims=True))
        if has_res:
            dz = dz + r_ref[...]
        o_ref[...] = dz.astype(o_ref.dtype)
        s1 = jnp.sum(df * zh, axis=0, keepdims=True)
        s2 = jnp.sum(df, axis=0, keepdims=True)

        @pl.when(i == 0)
        def _():
            s1_ref[...] = s1
            s2_ref[...] = s2

        @pl.when(i > 0)
        def _():
            s1_ref[...] += s1
            s2_ref[...] += s2

    row = pl.BlockSpec((tr, d), lambda i: (i, 0))
    col = pl.BlockSpec((1, d), lambda i: (0, 0))
    ins = [d_out, z, mcol] + ([dres] if has_res else [])
    return pl.pallas_call(
        body, name=name, grid=(t // tr,), in_specs=[row, row, col] + ([row] if has_res else []),
        out_specs=[row, col, col],
        out_shape=[jax.ShapeDtypeStruct((t, d), out_dtype), jax.ShapeDtypeStruct((1, d), F32),
                   jax.ShapeDtypeStruct((1, d), F32)],
        compiler_params=_cparams(("arbitrary",)),
    )(*ins)


FFN_IN_TILE = (512, 1408)


def _ffn_in_swiglu(h, w_in, name):
    t, d = h.shape
    f = w_in.shape[1] // 2
    tm, tn = _tile(t, FFN_IN_TILE[0]), _tile(f, FFN_IN_TILE[1])
    nj = f // tn

    def body(h_ref, wa_ref, wb_ref, a_ref, b_ref, s_ref):
        hv = h_ref[...].astype(BF16)
        a = _dot(hv, wa_ref[...].astype(BF16))
        b = _dot(hv, wb_ref[...].astype(BF16))
        a_ref[...] = a.astype(BF16)
        b_ref[...] = b.astype(BF16)
        s_ref[...] = (_silu(a) * b).astype(BF16)

    out = pl.BlockSpec((tm, tn), lambda j, i: (i, j))
    return pl.pallas_call(
        body, name=name, grid=(nj, t // tm),
        in_specs=[pl.BlockSpec((tm, d), lambda j, i: (i, 0)), pl.BlockSpec((d, tn), lambda j, i: (0, j)),
                  pl.BlockSpec((d, tn), lambda j, i: (0, j + nj))],
        out_specs=[out, out, out], out_shape=[jax.ShapeDtypeStruct((t, f), BF16)] * 3,
        compiler_params=_cparams(("parallel", "parallel")),
    )(h, w_in, w_in)


def _swiglu_bwd(ua, ub, ds, name):
    t, f = ua.shape
    tr = _rows(t)

    def body(a_ref, b_ref, ds_ref, du_ref):
        a = a_ref[...].astype(F32)
        b = b_ref[...].astype(F32)
        g = ds_ref[...].astype(F32)
        du_ref[:, :f] = (g * b * _dsilu(a)).astype(du_ref.dtype)
        du_ref[:, f:] = (g * _silu(a)).astype(du_ref.dtype)

    half = pl.BlockSpec((tr, f), lambda i: (i, 0))
    return pl.pallas_call(
        body, name=name, grid=(t // tr,), in_specs=[half, half, half],
        out_specs=pl.BlockSpec((tr, 2 * f), lambda i: (i, 0)), out_shape=jax.ShapeDtypeStruct((t, 2 * f), BF16),
        compiler_params=_cparams(("parallel",)),
    )(ua, ub, ds)


def _loss_head(y, target, name):
    t, d = y.shape
    tr = _rows(t)

    def body(y_ref, t_ref, dy_ref, sq_ref):
        i = pl.program_id(0)
        e = y_ref[...] - t_ref[...]
        dy_ref[...] = e * (1.0 / d)
        s = jnp.sum(e * e, axis=0, keepdims=True)

        @pl.when(i == 0)
        def _():
            sq_ref[...] = s

        @pl.when(i > 0)
        def _():
            sq_ref[...] += s

    row = pl.BlockSpec((tr, d), lambda i: (i, 0))
    col = pl.BlockSpec((1, d), lambda i: (0, 0))
    return pl.pallas_call(
        body, name=name, grid=(t // tr,), in_specs=[row, row], out_specs=[row, col],
        out_shape=[jax.ShapeDtypeStruct((t, d), F32), jax.ShapeDtypeStruct((1, d), F32)],
        compiler_params=_cparams(("arbitrary",)),
    )(y, target)


def _hgrn_consts():
    c = A_CHUNK
    shift = SUB.bit_length() - 1
    r = lax.broadcasted_iota(jnp.int32, (c, c), 0)
    s = lax.broadcasted_iota(jnp.int32, (c, c), 1)
    sub_r = lax.shift_right_logical(r, shift)
    incl = s <= r
    masks = [jnp.logical_and(sub_r == i, incl) for i in range(c // SUB)]
    rev_incl = jnp.where(s >= r, 1.0, 0.0).astype(BF16)
    r2 = lax.broadcasted_iota(jnp.int32, (2 * c + 8, c), 0)
    s2 = lax.broadcasted_iota(jnp.int32, (2 * c + 8, c), 1)
    sub_start = lax.shift_left(lax.shift_right_logical(r2 - c, shift), shift)
    running = jnp.where(s2 <= r2, 1.0, 0.0)
    before = jnp.where(s2 < sub_start, 1.0, 0.0)
    stack = jnp.where(r2 < c, running, jnp.where(r2 < 2 * c, before, 1.0)).astype(BF16)
    return stack, masks, incl, rev_incl


def _hgrn_gates(q_raw, f_raw, lbv, stack):
    sg = jax.nn.sigmoid(f_raw)
    sgn = jax.nn.sigmoid(-f_raw)
    f = lbv + (1.0 - lbv) * sg
    logf = jnp.log(jnp.maximum(f, TINY))
    return dict(sg=sg, sgn=sgn, f=f, k=(1.0 - lbv) * sgn, q=_silu(q_raw), bb=_ones_left(stack, logf))


def _hgrn_chunk(q_raw, f_raw, lbv, stack):
    return _hgrn_decays(_hgrn_gates(q_raw, f_raw, lbv, stack))


def _hgrn_decays(gates):
    c = A_CHUNK
    sg, sgn, f, k, q, bb = (gates[n] for n in ("sg", "sgn", "f", "k", "q", "bb"))
    b = bb[:c]
    bsrow = bb[c:2 * c]
    b_end = bb[2 * c:2 * c + 1]
    e_sub = jnp.exp(b - bsrow)
    e_b = jnp.exp(b)
    e_end = jnp.exp(b_end - b)
    qs = q * e_sub
    q_in = q * e_b
    kend = k * e_end
    kfac = [jnp.exp(jnp.minimum(bsrow[i * SUB:i * SUB + 1] - b, EXP_CLAMP)) for i in range(c // SUB)]
    return dict(sg=sg, sgn=sgn, f=f, k=k, q=q, b=b, b_end=b_end, e_sub=e_sub, e_b=e_b, e_end=e_end,
                qs=qs, q_in=q_in, kend=kend, kfac=kfac)


def _hgrn_scores(ch, masks):
    qs_b = ch["qs"].astype(BF16)
    a = None
    for i, mk in enumerate(masks):
        ki = (ch["k"] * ch["kfac"][i]).astype(BF16)
        part = jnp.where(mk, _dot_nt(qs_b, ki), 0.0)
        a = part if a is None else a + part
    return a


def _hgrn_fwd(p, lb, hn2, name):
    t = p.shape[0]
    tb = _rows(t)
    nt = t // tb
    nc = tb // A_CHUNK
    c = A_CHUNK

    def body(q_ref, f_ref, i_ref, g_ref, lb_ref, hn_ref, y_ref, o_ref, st_ref, s_scr):
        j = pl.program_id(1)

        @pl.when(j == 0)
        def _():
            s_scr[...] = jnp.zeros_like(s_scr)

        stack, masks, _, _ = _hgrn_consts()
        units = [(ci, hh) for ci in range(nc) for hh in range(2)]
        lsl = [slice(A_KDIM * hh, A_KDIM * (hh + 1)) for hh in range(2)]
        hsl = [slice(A_VDIM * hh, A_VDIM * (hh + 1)) for hh in range(2)]
        rows = [pl.ds(ci * c, c) for ci in range(nc)]
        gates = {u: _hgrn_gates(q_ref[rows[u[0]], lsl[u[1]]], f_ref[rows[u[0]], lsl[u[1]]], lb_ref[:, lsl[u[1]]], stack)
                 for u in units}
        ch = {u: _hgrn_decays(gates[u]) for u in units}
        v = {u: i_ref[rows[u[0]], hsl[u[1]]].astype(BF16) for u in units}
        a = {u: _hgrn_scores(ch[u], masks).astype(BF16) for u in units}
        grow = {u: _dot_tn(v[u], ch[u]["kend"].astype(BF16)) for u in units}
        states = [s_scr[0], s_scr[1]]
        entering = {}
        for ci, hh in units:
            entering[ci, hh] = states[hh]
            st_ref[hh, ci] = states[hh]
            states[hh] = states[hh] * jnp.exp(ch[ci, hh]["b_end"]) + grow[ci, hh]
        s_scr[0] = states[0]
        s_scr[1] = states[1]
        for u in units:
            o_ref[rows[u[0]], hsl[u[1]]] = (_dot_nt(ch[u]["q_in"].astype(BF16), entering[u].astype(BF16))
                                            + _dot(a[u], v[u]))
        for hh in range(2):
            hsl = slice(A_VDIM * hh, A_VDIM * (hh + 1))
            o = o_ref[:, hsl]
            r = lax.rsqrt(jnp.mean(o * o, axis=-1, keepdims=True) + EPS)
            y_ref[:, hsl] = (o * r * hn_ref[:, hsl] * _silu(g_ref[:, hsl])).astype(y_ref.dtype)

    w2 = 2 * A_KDIM
    return pl.pallas_call(
        body, name=name, grid=(A_HEADS // 2, nt),
        in_specs=[pl.BlockSpec((tb, w2), lambda h, j: (j, COL_AQ // w2 + h)),
                  pl.BlockSpec((tb, w2), lambda h, j: (j, COL_AF // w2 + h)),
                  pl.BlockSpec((tb, LANES), lambda h, j: (j, COL_AI // LANES + h)),
                  pl.BlockSpec((tb, LANES), lambda h, j: (j, COL_AG // LANES + h)),
                  pl.BlockSpec((1, w2), lambda h, j: (0, h)),
                  pl.BlockSpec((1, LANES), lambda h, j: (0, 0))],
        out_specs=[pl.BlockSpec((tb, LANES), lambda h, j: (j, h)),
                   pl.BlockSpec((tb, LANES), lambda h, j: (j, h)),
                   pl.BlockSpec((2, nc, A_VDIM, A_KDIM), lambda h, j: (h, j, 0, 0))],
        out_shape=[jax.ShapeDtypeStruct((t, A_V), BF16), jax.ShapeDtypeStruct((t, A_V), F32),
                   jax.ShapeDtypeStruct((A_HEADS, t // c, A_VDIM, A_KDIM), F32)],
        scratch_shapes=[pltpu.VMEM((2, A_VDIM, A_KDIM), F32)],
        compiler_params=_cparams(("parallel", "arbitrary")),
    )(p, p, p, p, lb, hn2)


def _hgrn_bwd(p, lb, hn2, o_raw, states, dya, name):
    t = p.shape[0]
    tb = _rows(t)
    nt = t // tb
    nc = tb // A_CHUNK
    c = A_CHUNK

    def body(q_ref, f_ref, i_ref, g_ref, lb_ref, hn_ref, o_ref, st_ref, dy_ref,
             dq_ref, df_ref, di_ref, dg_ref, dlb_ref, dhn_ref, ds_scr, do_scr):
        j = pl.program_id(1)

        @pl.when(j == 0)
        def _():
            ds_scr[...] = jnp.zeros_like(ds_scr)
            dlb_ref[...] = jnp.zeros_like(dlb_ref)
            dhn_ref[...] = jnp.zeros_like(dhn_ref)

        stack, masks, incl, rev_incl = _hgrn_consts()
        for hh in range(2):
            hsl = slice(A_VDIM * hh, A_VDIM * (hh + 1))
            o = o_ref[:, hsl]
            g = g_ref[:, hsl]
            dy = dy_ref[:, hsl].astype(F32)
            hn = hn_ref[:, hsl]
            r = lax.rsqrt(jnp.mean(o * o, axis=-1, keepdims=True) + EPS)
            oh = o * r
            sgate = _silu(g)
            dg_ref[:, hsl] = (dy * oh * hn * _dsilu(g)).astype(dg_ref.dtype)
            dhn_ref[0, :, hsl] += jnp.sum(dy * oh * sgate, axis=0, keepdims=True)
            doh = dy * hn * sgate
            do_scr[:, hsl] = r * (doh - oh * jnp.mean(doh * oh, axis=-1, keepdims=True))

        units = [(ci, hh) for ci in reversed(range(nc)) for hh in range(2)]
        lsl = [slice(A_KDIM * hh, A_KDIM * (hh + 1)) for hh in range(2)]
        hsl = [slice(A_VDIM * hh, A_VDIM * (hh + 1)) for hh in range(2)]
        rows = [pl.ds(ci * c, c) for ci in range(nc)]
        q_raw = {u: q_ref[rows[u[0]], lsl[u[1]]] for u in units}
        gates = {u: _hgrn_gates(q_raw[u], f_ref[rows[u[0]], lsl[u[1]]], lb_ref[:, lsl[u[1]]], stack) for u in units}
        ch = {u: _hgrn_decays(gates[u]) for u in units}
        v = {u: i_ref[rows[u[0]], hsl[u[1]]].astype(BF16) for u in units}
        do_b = {u: do_scr[rows[u[0]], hsl[u[1]]].astype(BF16) for u in units}
        st = {u: st_ref[u[1], u[0]] for u in units}
        qs_b = {u: ch[u]["qs"].astype(BF16) for u in units}
        a_b = {u: _hgrn_scores(ch[u], masks).astype(BF16) for u in units}
        da = {u: jnp.where(incl, _dot_nt(do_b[u], v[u]), 0.0) for u in units}
        dq_x = {u: _dot(do_b[u], st[u].astype(BF16)) for u in units}
        grow = {u: _dot_tn(do_b[u], ch[u]["q_in"].astype(BF16)) for u in units}
        dstates = [ds_scr[0], ds_scr[1]]
        leaving = {}
        for ci, hh in units:
            leaving[ci, hh] = dstates[hh]
            dstates[hh] = dstates[hh] * jnp.exp(ch[ci, hh]["b_end"]) + grow[ci, hh]
        for hh in range(2):
            ds_scr[hh] = dstates[hh]
        dst_b = {u: leaving[u].astype(BF16) for u in units}
        dv = {u: _dot_tn(a_b[u], do_b[u]) + _dot_nt(ch[u]["kend"].astype(BF16), dst_b[u]) for u in units}
        dk_x = {u: _dot(v[u], dst_b[u]) for u in units}
        dlb_acc = [jnp.zeros((1, A_KDIM), F32), jnp.zeros((1, A_KDIM), F32)]
        for u in units:
            ci, hh = u
            cu = ch[u]
            lbv = lb_ref[:, lsl[hh]]
            dq_i = None
            dk_i = None
            kdk_i = None
            for i, mk in enumerate(masks):
                dam = jnp.where(mk, da[u], 0.0).astype(BF16)
                ki = (cu["k"] * cu["kfac"][i]).astype(BF16)
                pq = _dot(dam, ki)
                pk = _dot_tn(dam, qs_b[u])
                dq_i = pq if dq_i is None else dq_i + pq
                dk_i = cu["kfac"][i] * pk if dk_i is None else dk_i + cu["kfac"][i] * pk
                kdk_i = ki.astype(F32) * pk if kdk_i is None else kdk_i + ki.astype(F32) * pk
            dq = cu["e_sub"] * dq_i + cu["e_b"] * dq_x[u]
            dk = dk_i + cu["e_end"] * dk_x[u]
            kx = cu["kend"] * dk_x[u]
            db = (qs_b[u].astype(F32) * dq_i + cu["q_in"] * dq_x[u]) - (kdk_i + kx)
            later = (jnp.exp(cu["b_end"]) * jnp.sum(leaving[u] * st[u], axis=0, keepdims=True)
                     + jnp.sum(kx, axis=0, keepdims=True))
            dlogf = later + _ones_left(rev_incl, db)
            dfv = jnp.where(cu["f"] > TINY, dlogf / cu["f"], 0.0)
            dq_ref[rows[ci], lsl[hh]] = (dq * _dsilu(q_raw[u])).astype(dq_ref.dtype)
            df_ref[rows[ci], lsl[hh]] = ((1.0 - lbv) * cu["sg"] * cu["sgn"] * (dfv - dk)).astype(df_ref.dtype)
            dlb_acc[hh] = dlb_acc[hh] + jnp.sum(dfv * (1.0 - cu["sg"]) - dk * cu["sgn"], axis=0, keepdims=True)
            di_ref[rows[ci], hsl[hh]] = dv[u].astype(di_ref.dtype)
        for hh in range(2):
            dlb_ref[:, A_KDIM * hh:A_KDIM * (hh + 1)] += dlb_acc[hh]

    w2 = 2 * A_KDIM
    rev = lambda j: nt - 1 - j
    return pl.pallas_call(
        body, name=name, grid=(A_HEADS // 2, nt),
        in_specs=[pl.BlockSpec((tb, w2), lambda h, j: (rev(j), COL_AQ // w2 + h)),
                  pl.BlockSpec((tb, w2), lambda h, j: (rev(j), COL_AF // w2 + h)),
                  pl.BlockSpec((tb, LANES), lambda h, j: (rev(j), COL_AI // LANES + h)),
                  pl.BlockSpec((tb, LANES), lambda h, j: (rev(j), COL_AG // LANES + h)),
                  pl.BlockSpec((1, w2), lambda h, j: (0, h)),
                  pl.BlockSpec((1, LANES), lambda h, j: (0, 0)),
                  pl.BlockSpec((tb, LANES), lambda h, j: (rev(j), h)),
                  pl.BlockSpec((2, nc, A_VDIM, A_KDIM), lambda h, j: (h, rev(j), 0, 0)),
                  pl.BlockSpec((tb, LANES), lambda h, j: (rev(j), h))],
        out_specs=[pl.BlockSpec((tb, w2), lambda h, j: (rev(j), h)),
                   pl.BlockSpec((tb, w2), lambda h, j: (rev(j), h)),
                   pl.BlockSpec((tb, LANES), lambda h, j: (rev(j), h)),
                   pl.BlockSpec((tb, LANES), lambda h, j: (rev(j), h)),
                   pl.BlockSpec((1, w2), lambda h, j: (0, h)),
                   pl.BlockSpec((1, 1, LANES), lambda h, j: (h, 0, 0))],
        out_shape=[jax.ShapeDtypeStruct((t, A_QK), BF16), jax.ShapeDtypeStruct((t, A_QK), BF16),
                   jax.ShapeDtypeStruct((t, A_V), BF16), jax.ShapeDtypeStruct((t, A_V), BF16),
                   jax.ShapeDtypeStruct((1, A_QK), F32), jax.ShapeDtypeStruct((A_HEADS // 2, 1, LANES), F32)],
        scratch_shapes=[pltpu.VMEM((2, A_VDIM, A_KDIM), F32), pltpu.VMEM((tb, LANES), F32)],
        compiler_params=_cparams(("parallel", "arbitrary")),
    )(p, p, p, p, lb, hn2, o_raw, states, dya)


BLK = 128
SCALE = HDIM ** -0.5
SB_CHUNK = 4


def _softplus(z):
    return jnp.maximum(z, 0.0) + jnp.log(1.0 + jnp.exp(-jnp.abs(z)))


def _sb_sum_matrix(keep, with_total=False):
    width = 2 * BLK if with_total else BLK
    sp = lax.broadcasted_iota(jnp.int32, (BLK, width), 0)
    s = lax.broadcasted_iota(jnp.int32, (BLK, width), 1)
    return jnp.where(jnp.logical_or(s >= BLK, keep(sp, s)), 1.0, 0.0).astype(BF16)


def _lanes(col):
    return jnp.broadcast_to(col, (BLK, BLK))


def _sb_fwd(p, kv, name, gather=None):
    t = p.shape[0]
    nq = t // BLK
    nh = B_HEADS // 2
    cw = SB_CHUNK * BLK
    fused = gather is not None
    n = len(gather) if fused else 0

    def body(*refs):
        q_ref, kb, vb = refs[:3]
        o_ref, tot_ref = refs[3 + n:5 + n]
        zbuf, stage, sbuf, abuf = refs[5 + 2 * n:9 + 2 * n]
        hp = pl.program_id(0)
        qi = pl.program_id(1)
        if fused:
            g = _Many(_Gather, refs[3:3 + n], refs[5 + n:5 + 2 * n], *refs[9 + 2 * n:])
            pl.when(jnp.logical_and(hp == 0, qi == 0))(g.start)
            pl.when(jnp.logical_and(hp == nh - 1, qi == 0))(g.forward)

        @pl.when(qi == 0)
        def _():
            abuf[...] = jnp.zeros_like(abuf)

        row = lax.broadcasted_iota(jnp.int32, (BLK, BLK), 0)
        col = lax.broadcasted_iota(jnp.int32, (BLK, BLK), 1)
        sums = _sb_sum_matrix(lambda sp, s: sp >= s, True)
        hsl = [slice(HDIM * h, HDIM * (h + 1)) for h in range(2)]
        nchunk = qi // SB_CHUNK + 1
        for h in range(2):
            zbuf[h] = _dot_nt((q_ref[:, hsl[h]] * SCALE).astype(BF16), kb[:, hsl[h]])

        col_minus_row = col - row

        def causal(j):
            return col_minus_row < (qi - j) * BLK

        def l_pass(c, carry):
            for b in range(SB_CHUNK):
                j = c * SB_CHUNK + b
                off = pl.multiple_of(j * BLK, BLK)
                mask = causal(j)
                for h in range(2):
                    lm = jnp.where(mask, -_softplus(zbuf[h, :, pl.ds(off, BLK)]), 0.0)
                    stage[h, pl.ds(off, BLK), :] = lm.astype(BF16)
            return carry

        lax.fori_loop(0, nchunk, l_pass, 0)

        def sum_pass(c, carry):
            rows = pl.ds(pl.multiple_of(c * cw, cw), cw)
            for h in range(2):
                sbuf[h, rows, :] = _dot(stage[h, rows, :], sums)
            return carry

        lax.fori_loop(0, nchunk, sum_pass, 0)

        def a_pass(it, carry):
            c = nchunk - 1 - it
            runs = list(carry)
            for b in reversed(range(SB_CHUNK)):
                j = c * SB_CHUNK + b
                off = pl.multiple_of(j * BLK, BLK)
                mask = causal(j)
                for h in range(2):
                    s = sbuf[h, pl.ds(off, BLK), :BLK]
                    a = jnp.where(mask, jnp.exp(zbuf[h, :, pl.ds(off, BLK)] + s + runs[h]), 0.0)
                    abuf[h, :, pl.ds(off, BLK)] = a.astype(BF16)
                    runs[h] = runs[h] + sbuf[h, pl.ds(off, BLK), BLK:]
            return tuple(runs)

        zero = jnp.zeros((BLK, BLK), F32)
        runs = lax.fori_loop(0, nchunk, a_pass, (zero, zero))
        for h in range(2):
            tot_ref[:, hsl[h]] = runs[h][:, :HDIM]
            o_ref[:, hsl[h]] = _dot(abuf[h], vb[:, hsl[h]])
        if fused:
            pl.when(jnp.logical_and(hp == nh - 1, qi == nq - 1))(g.finish)

    out_blk = pl.BlockSpec((BLK, LANES), lambda h, i: (i, h))
    hbm = pl.BlockSpec(memory_space=pl.ANY)
    in_specs = [pl.BlockSpec((BLK, LANES), lambda h, i: (i, COL_BQ // LANES + h)),
                pl.BlockSpec((t, LANES), lambda h, i: (0, h)),
                pl.BlockSpec((t, LANES), lambda h, i: (0, B_W // LANES + h))]
    out_shape = [jax.ShapeDtypeStruct((t, B_W), F32)] * 2
    scratch = [pltpu.VMEM((2, BLK, t), F32), pltpu.VMEM((2, t, BLK), BF16),
               pltpu.VMEM((2, t, 2 * BLK), F32), pltpu.VMEM((2, BLK, t), BF16)]
    if fused:
        out_shape = out_shape + _gathered_shapes(gather)
    return pl.pallas_call(
        body, name=name, grid=(nh, nq),
        in_specs=in_specs + [hbm] * n,
        out_specs=[out_blk, out_blk] + [hbm] * n,
        out_shape=out_shape,
        scratch_shapes=scratch + (_comm_sems(n) if fused else []),
        compiler_params=_cparams(("arbitrary", "arbitrary")),
    )(p, kv, kv, *(gather if fused else []))


def _sb_bwd(p, kv, tot, do, name, exchange=None):
    t = p.shape[0]
    nq = t // BLK
    nh = B_HEADS // 2
    cw = SB_CHUNK * BLK
    fused = exchange is not None
    n = len(exchange) if fused else 0

    def body(*refs):
        q_ref, kb, vb, tot_ref, do_ref = refs[:5]
        dq_ref, dk_ref, dv_ref = refs[5 + n:8 + n]
        zbuf, dabuf, lbuf, stage, sbuf, abuf, dzbuf, dkt, dvt = refs[8 + 2 * n:17 + 2 * n]
        hp = pl.program_id(0)
        qi = pl.program_id(1)
        if fused:
            ex = _Many(_Exchange, refs[5:5 + n], refs[8 + n:8 + 2 * n], *refs[17 + 2 * n:])
            pl.when(jnp.logical_and(hp == 0, qi == 0))(ex.start)

        @pl.when(qi == 0)
        def _():
            dkt[...] = jnp.zeros_like(dkt)
            dvt[...] = jnp.zeros_like(dvt)
            dzbuf[...] = jnp.zeros_like(dzbuf)
            abuf[...] = jnp.zeros_like(abuf)

        row = lax.broadcasted_iota(jnp.int32, (BLK, BLK), 0)
        col = lax.broadcasted_iota(jnp.int32, (BLK, BLK), 1)
        sums = _sb_sum_matrix(lambda sp, s: sp <= s)
        hsl = [slice(HDIM * h, HDIM * (h + 1)) for h in range(2)]
        dob = [do_ref[:, hsl[h]].astype(BF16) for h in range(2)]
        total =[jnp.concatenate([tot_ref[:, hsl[h]], tot_ref[:, hsl[h]]], axis=1) for h in range(2)]
        nchunk = qi // SB_CHUNK + 1
        for h in range(2):
            zbuf[h] = _dot_nt((q_ref[:, hsl[h]] * SCALE).astype(BF16), kb[:, hsl[h]])
            dabuf[h] = _dot_nt(dob[h], vb[:, hsl[h]])

        col_minus_row = col - row

        def causal(j):
            return col_minus_row < (qi - j) * BLK

        def blocks(c):
            for b in range(SB_CHUNK):
                j = c * SB_CHUNK + b
                yield j, pl.ds(pl.multiple_of(j * BLK, BLK), BLK)

        def l_pass(c, carry):
            for j, blk_ in blocks(c):
                mask = causal(j)
                for h in range(2):
                    lm = jnp.where(mask, -_softplus(zbuf[h, :, blk_]), 0.0)
                    lbuf[h, :, blk_] = lm
                    stage[h, blk_, :] = lm.astype(BF16)
            return carry

        lax.fori_loop(0, nchunk, l_pass, 0)

        def sum_pass():
            def run_(c, carry):
                rows = pl.ds(pl.multiple_of(c * cw, cw), cw)
                for h in range(2):
                    sbuf[h, rows, :] = _dot(stage[h, rows, :], sums)
                return carry
            lax.fori_loop(0, nchunk, run_, 0)

        sum_pass()

        def g_pass(c, carry):
            runs = list(carry)
            for j, blk_ in blocks(c):
                mask = causal(j)
                for h in range(2):
                    upto = sbuf[h, blk_, :]
                    log_a = zbuf[h, :, blk_] + lbuf[h, :, blk_] + (total[h] - runs[h] - upto)
                    a = jnp.where(mask, jnp.exp(log_a), 0.0)
                    abuf[h, :, blk_] = a.astype(BF16)
                    g = a * dabuf[h, :, blk_]
                    dabuf[h, :, blk_] = g
                    stage[h, blk_, :] = g.astype(BF16)
                    runs[h] = runs[h] + _lanes(upto[:, BLK - 1:BLK])
            return tuple(runs)

        zero = jnp.zeros((BLK, BLK), F32)
        lax.fori_loop(0, nchunk, g_pass, (zero, zero))
        sum_pass()

        def dz_pass(c, carry):
            runs = list(carry)
            for j, blk_ in blocks(c):
                mask = causal(j)
                for h in range(2):
                    lm = lbuf[h, :, blk_]
                    g = dabuf[h, :, blk_]
                    upto = sbuf[h, blk_, :]
                    before = runs[h] + upto - g
                    dz = jnp.where(mask, g * jnp.exp(lm) - jnp.exp(zbuf[h, :, blk_] + lm) * before, 0.0)
                    dzbuf[h, :, blk_] = (dz * SCALE).astype(BF16)
                    runs[h] = runs[h] + _lanes(upto[:, BLK - 1:BLK])
            return tuple(runs)

        lax.fori_loop(0, nchunk, dz_pass, (zero, zero))
        for h in range(2):
            dq_ref[:, hsl[h]] = _dot(dzbuf[h], kb[:, hsl[h]]).astype(dq_ref.dtype)
        q_t = q_ref[...].T.astype(BF16)
        do_t = do_ref[...].T.astype(BF16)
        for h in range(2):
            dkt[hsl[h], :] += _dot(q_t[hsl[h], :], dzbuf[h])
            dvt[hsl[h], :] += _dot(do_t[hsl[h], :], abuf[h])

        @pl.when(qi == nq - 1)
        def _():
            dk_ref[...] = dkt[...].T.astype(dk_ref.dtype)
            dv_ref[...] = dvt[...].T.astype(dv_ref.dtype)

        if fused:
            pl.when(jnp.logical_and(hp == nh - 1, qi == nq - 1))(ex.finish)

    blk = lambda h, i: (i, h)
    whole = lambda h, i: (0, h)
    hbm = pl.BlockSpec(memory_space=pl.ANY)
    in_specs = [pl.BlockSpec((BLK, LANES), lambda h, i: (i, COL_BQ // LANES + h)),
                pl.BlockSpec((t, LANES), lambda h, i: (0, h)),
                pl.BlockSpec((t, LANES), lambda h, i: (0, B_W // LANES + h)),
                pl.BlockSpec((BLK, LANES), blk), pl.BlockSpec((BLK, LANES), blk)]
    out_specs = [pl.BlockSpec((BLK, LANES), blk), pl.BlockSpec((t, LANES), whole), pl.BlockSpec((t, LANES), whole)]
    out_shape = [jax.ShapeDtypeStruct((t, B_W), BF16)] * 3
    scratch = [pltpu.VMEM((2, BLK, t), F32), pltpu.VMEM((2, BLK, t), F32), pltpu.VMEM((2, BLK, t), F32),
               pltpu.VMEM((2, t, BLK), BF16), pltpu.VMEM((2, t, BLK), F32), pltpu.VMEM((2, BLK, t), BF16),
               pltpu.VMEM((2, BLK, t), BF16), pltpu.VMEM((LANES, t), F32), pltpu.VMEM((LANES, t), F32)]
    if fused:
        out_shape = out_shape + [jax.ShapeDtypeStruct(e.shape, e.dtype) for e in exchange]
    return pl.pallas_call(
        body, name=name, grid=(nh, nq),
        in_specs=in_specs + [hbm] * n,
        out_specs=out_specs + [hbm] * n,
        out_shape=out_shape,
        scratch_shapes=scratch + (_comm_sems(n) if fused else []),
        compiler_params=_cparams(("arbitrary", "arbitrary")),
    )(p, kv, kv, tot, do, *(exchange if fused else []))


def _alibi_slopes(n):
    def pow2(m):
        start = 2.0 ** (-8.0 / m)
        return [start ** (i + 1) for i in range(m)]
    if math.log2(n).is_integer():
        s = pow2(n)
    else:
        c = 2 ** int(math.floor(math.log2(n)))
        s = pow2(c) + pow2(2 * c)[0::2][: n - c]
    return sorted(s, reverse=True)


def _dil_scores(qh, kh, sl, prev, exists=None):
    row = lax.broadcasted_iota(jnp.int32, (BLK, BLK), 0)
    col = lax.broadcasted_iota(jnp.int32, (BLK, BLK), 1)
    dist = row - col + (BLK if prev else 0)
    if prev:
        valid = (col - row) >= jnp.where(exists, 0, 2 * BLK)
    else:
        valid = col <= row
    s = _dot_nt(qh, kh) - sl * dist.astype(F32)
    return s, valid


DIL_UNITS = 4


def _dil_plan(r):
    per_trip = min(r, DIL_UNITS)
    return per_trip, DIL_UNITS // per_trip


def _dil_rows(b, rho, r):
    return pl.ds(b * BLK * r + rho, BLK, stride=r) if r > 1 else pl.ds(b * BLK, BLK)


def _dil_fwd(p, gi, name):
    t = p.shape[0]
    _, r = C_GROUPS[gi]
    per_trip, nsub = _dil_plan(r)
    sbr = BLK * r * nsub
    nsb = t // sbr
    slope_cols = _slope_cols(gi)

    def body(q_ref, kc_ref, kp_ref, vc_ref, vp_ref, sl_ref, o_ref, lse_ref):
        i = pl.program_id(1)

        hsl = [slice(HDIM * h, HDIM * (h + 1)) for h in range(2)]
        sl = [sl_ref[:, HDIM * h:HDIM * h + 1] for h in range(2)]

        def residues(it, carry):
            pairs = [(b, dr) for b in range(nsub) for dr in range(per_trip)]
            units = [(pr, h) for pr in pairs for h in range(2)]
            rows = {(b, dr): _dil_rows(b, it * per_trip + dr, r) for b, dr in pairs}
            blocks, prev_exists = {}, {}
            for b, dr in pairs:
                rw = rows[b, dr]
                if b == 0:
                    before = _dil_rows(nsub - 1, it * per_trip + dr, r)
                    kp, vp, prev_exists[b, dr] = kp_ref[before, :], vp_ref[before, :], i > 0
                else:
                    before = rows[b - 1, dr]
                    kp, vp, prev_exists[b, dr] = kc_ref[before, :], vc_ref[before, :], True
                blocks[b, dr] = [q_ref[rw, :], kc_ref[rw, :], kp, vc_ref[rw, :], vp]
            qh = {u: (blocks[u[0]][0][:, hsl[u[1]]] * SCALE).astype(BF16) for u in units}
            sc = {u: _dil_scores(qh[u], blocks[u[0]][1][:, hsl[u[1]]].astype(BF16), sl[u[1]], False) for u in units}
            sp = {u: _dil_scores(qh[u], blocks[u[0]][2][:, hsl[u[1]]].astype(BF16), sl[u[1]], True, prev_exists[u[0]])
                  for u in units}
            pc, pp, den, lse = {}, {}, {}, {}
            for u in units:
                s_c = jnp.where(sc[u][1], sc[u][0], NEG_BIG)
                s_p = jnp.where(sp[u][1], sp[u][0], NEG_BIG)
                m = jnp.maximum(jnp.max(s_c, axis=1, keepdims=True), jnp.max(s_p, axis=1, keepdims=True))
                pc[u] = jnp.exp(s_c - m)
                pp[u] = jnp.exp(s_p - m)
                den[u] = jnp.sum(pc[u], axis=1, keepdims=True) + jnp.sum(pp[u], axis=1, keepdims=True)
                lse[u] = jnp.broadcast_to(m + jnp.log(den[u]), (BLK, HDIM))
            o = {u: (_dot(pc[u].astype(BF16), blocks[u[0]][3][:, hsl[u[1]]].astype(BF16))
                     + _dot(pp[u].astype(BF16), blocks[u[0]][4][:, hsl[u[1]]].astype(BF16))) / den[u] for u in units}
            for pr in pairs:
                o_ref[rows[pr], :] = jnp.concatenate([o[pr, 0], o[pr, 1]], axis=1)
                lse_ref[rows[pr], :] = jnp.concatenate([lse[pr, 0], lse[pr, 1]], axis=1)
            return carry

        lax.fori_loop(0, r // per_trip, residues, 0)

    def at(col0, pick):
        return pl.BlockSpec((sbr, LANES), lambda c, i: (pick(i), col0 // LANES + c))

    cur = lambda i: i
    prv = lambda i: jnp.maximum(i - 1, 0)
    cq, ck, cv = COL_CQ + gi * C_OUT, COL_CK + gi * C_OUT, COL_CV + gi * C_OUT
    out = pl.BlockSpec((sbr, LANES), lambda c, i: (i, c))
    return pl.pallas_call(
        body, name=name, grid=(C_OUT // LANES, nsb),
        in_specs=[at(cq, cur), at(ck, cur), at(ck, prv), at(cv, cur), at(cv, prv),
                  pl.BlockSpec((1, LANES), lambda c, i: (0, c))],
        out_specs=[out, out], out_shape=[jax.ShapeDtypeStruct((t, C_OUT), F32)] * 2,
        compiler_params=_cparams(("parallel", "parallel")),
    )(p, p, p, p, p, slope_cols)


def _dil_bwd(p, do, o, lse, gi, name):
    t = p.shape[0]
    _, r = C_GROUPS[gi]
    per_trip, nsub = _dil_plan(r)
    sbr = BLK * r * nsub
    nsb = t // sbr
    slope_cols = _slope_cols(gi)

    def body(q_ref, qn_ref, kc_ref, kp_ref, vc_ref, vp_ref, do_ref, don_ref, o_ref, on_ref, l_ref, ln_ref, sl_ref,
             dq_ref, dk_ref, dv_ref):
        i = pl.program_id(1)

        hsl = [slice(HDIM * h, HDIM * (h + 1)) for h in range(2)]
        sl = [sl_ref[:, HDIM * h:HDIM * h + 1] for h in range(2)]

        def residues(it, carry):
            pairs = [(b, dr) for b in range(nsub) for dr in range(per_trip)]
            units = [(pr, h) for pr in pairs for h in range(2)]
            rows = {(b, dr): _dil_rows(b, it * per_trip + dr, r) for b, dr in pairs}
            blocks, has_prev, has_next = {}, {}, {}
            for b, dr in pairs:
                rw = rows[b, dr]
                if b == 0:
                    before = _dil_rows(nsub - 1, it * per_trip + dr, r)
                    kp, vp, has_prev[b, dr] = kp_ref[before, :], vp_ref[before, :], i > 0
                else:
                    kp, vp, has_prev[b, dr] = kc_ref[rows[b - 1, dr], :], vc_ref[rows[b - 1, dr], :], True
                if b == nsub - 1:
                    after = _dil_rows(0, it * per_trip + dr, r)
                    nxt = [ref[after, :] for ref in (qn_ref, don_ref, on_ref, ln_ref)]
                    has_next[b, dr] = i < nsb - 1
                else:
                    nxt = [ref[rows[b + 1, dr], :] for ref in (q_ref, do_ref, o_ref, l_ref)]
                    has_next[b, dr] = True
                blocks[b, dr] = [q_ref[rw, :], nxt[0], kc_ref[rw, :], kp, vc_ref[rw, :], vp, do_ref[rw, :], nxt[1],
                                 o_ref[rw, :], nxt[2], l_ref[rw, :], nxt[3]]
            part = lambda u, k: blocks[u[0]][k][:, hsl[u[1]]]
            qb = {u: part(u, 0).astype(BF16) for u in units}
            qnb = {u: part(u, 1).astype(BF16) for u in units}
            qh = {u: (part(u, 0) * SCALE).astype(BF16) for u in units}
            qnh = {u: (part(u, 1) * SCALE).astype(BF16) for u in units}
            kc = {u: part(u, 2).astype(BF16) for u in units}
            kp = {u: part(u, 3).astype(BF16) for u in units}
            vc = {u: part(u, 4).astype(BF16) for u in units}
            vp = {u: part(u, 5).astype(BF16) for u in units}
            dob = {u: part(u, 6).astype(BF16) for u in units}
            donb = {u: part(u, 7).astype(BF16) for u in units}
            delta = {u: jnp.sum(part(u, 6) * part(u, 8), axis=1, keepdims=True) for u in units}
            deltan = {u: jnp.sum(part(u, 7) * part(u, 9), axis=1, keepdims=True) for u in units}
            lse_c = {u: part(u, 10)[:, :1] for u in units}
            lse_n = {u: part(u, 11)[:, :1] for u in units}
            s_cc = {u: _dil_scores(qh[u], kc[u], sl[u[1]], False) for u in units}
            s_cp = {u: _dil_scores(qh[u], kp[u], sl[u[1]], True, has_prev[u[0]]) for u in units}
            s_nc = {u: _dil_scores(qnh[u], kc[u], sl[u[1]], True, has_next[u[0]]) for u in units}
            da_cc = {u: _dot_nt(dob[u], vc[u]) for u in units}
            da_cp = {u: _dot_nt(dob[u], vp[u]) for u in units}
            da_nc = {u: _dot_nt(donb[u], vc[u]) for u in units}

            def prob(s_ok, lse_col):
                s, ok = s_ok
                return jnp.where(ok, jnp.exp(jnp.where(ok, s, NEG_BIG) - lse_col), 0.0)

            p_cc = {u: prob(s_cc[u], lse_c[u]) for u in units}
            p_cp = {u: prob(s_cp[u], lse_c[u]) for u in units}
            p_nc = {u: prob(s_nc[u], lse_n[u]) for u in units}
            ds_cc = {u: (p_cc[u] * (da_cc[u] - delta[u]) * SCALE).astype(BF16) for u in units}
            ds_cp = {u: (p_cp[u] * (da_cp[u] - delta[u]) * SCALE).astype(BF16) for u in units}
            ds_nc = {u: (p_nc[u] * (da_nc[u] - deltan[u]) * SCALE).astype(BF16) for u in units}
            dq = {u: _dot(ds_cc[u], kc[u]) + _dot(ds_cp[u], kp[u]) for u in units}
            dk = {u: _dot_tn(ds_cc[u], qb[u]) + _dot_tn(ds_nc[u], qnb[u]) for u in units}
            dv = {u: _dot_tn(p_cc[u].astype(BF16), dob[u]) + _dot_tn(p_nc[u].astype(BF16), donb[u]) for u in units}
            for pr in pairs:
                dq_ref[rows[pr], :] = jnp.concatenate([dq[pr, 0], dq[pr, 1]], axis=1)
                dk_ref[rows[pr], :] = jnp.concatenate([dk[pr, 0], dk[pr, 1]], axis=1)
                dv_ref[rows[pr], :] = jnp.concatenate([dv[pr, 0], dv[pr, 1]], axis=1)
            return carry

        lax.fori_loop(0, r // per_trip, residues, 0)

    def at(col0, pick):
        return pl.BlockSpec((sbr, LANES), lambda c, i: (pick(i), col0 // LANES + c))

    cur = lambda i: i
    prv = lambda i: jnp.maximum(i - 1, 0)
    nxt = lambda i: jnp.minimum(i + 1, nsb - 1)
    cq, ck, cv = COL_CQ + gi * C_OUT, COL_CK + gi * C_OUT, COL_CV + gi * C_OUT
    return pl.pallas_call(
        body, name=name, grid=(C_OUT // LANES, nsb),
        in_specs=[at(cq, cur), at(cq, nxt), at(ck, cur), at(ck, prv), at(cv, cur), at(cv, prv),
                  at(0, cur), at(0, nxt), at(0, cur), at(0, nxt), at(0, cur), at(0, nxt),
                  pl.BlockSpec((1, LANES), lambda c, i: (0, c))],
        out_specs=[at(0, cur)] * 3, out_shape=[jax.ShapeDtypeStruct((t, C_OUT), F32)] * 3,
        compiler_params=_cparams(("parallel", "parallel")),
    )(p, p, p, p, p, p, do, do, o, o, lse, lse, slope_cols)


def _dil_merge(os_, ls_, name):
    t, w = os_[0].shape
    tr = _rows(t)

    def body(o0, o1, o2, l0, l1, l2, y_ref, lse_ref):
        a, b, c = l0[...], l1[...], l2[...]
        m = jnp.maximum(jnp.maximum(a, b), c)
        ea, eb, ec = jnp.exp(a - m), jnp.exp(b - m), jnp.exp(c - m)
        den = ea + eb + ec
        y_ref[...] = (ea * o0[...] + eb * o1[...] + ec * o2[...]) / den
        lse_ref[...] = m + jnp.log(den)

    row = pl.BlockSpec((tr, w), lambda i: (i, 0))
    return pl.pallas_call(
        body, name=name, grid=(t // tr,), in_specs=[row] * 6, out_specs=[row, row],
        out_shape=[jax.ShapeDtypeStruct((t, w), F32)] * 2, compiler_params=_cparams(("parallel",)),
    )(*os_, *ls_)


def _gate_fwd(ys, gl, ws, name):
    t = gl.shape[0]
    d = gl.shape[1] // N_BRANCH
    tr = _rows(t)

    def body(ya, yb, yc, gl_ref, wa, wb, wc, m_ref):
        acc = None
        for i, (y, w) in enumerate(((ya, wa), (yb, wb), (yc, wc))):
            z = _dot(y[...].astype(BF16), w[...])
            term = jax.nn.sigmoid(gl_ref[:, i * d:(i + 1) * d]) * z
            acc = term if acc is None else acc + term
        m_ref[...] = acc.astype(m_ref.dtype)

    rows = [pl.BlockSpec((tr, y.shape[1]), lambda i: (i, 0)) for y in ys]
    wsp = [pl.BlockSpec(w.shape, lambda i: (0, 0)) for w in ws]
    return pl.pallas_call(
        body, name=name, grid=(t // tr,),
        in_specs=rows + [pl.BlockSpec((tr, N_BRANCH * d), lambda i: (i, 0))] + wsp,
        out_specs=pl.BlockSpec((tr, d), lambda i: (i, 0)), out_shape=jax.ShapeDtypeStruct((t, d), BF16),
        compiler_params=_cparams(("parallel",)),
    )(*ys, gl, *ws)


def _gate_bwd(dm, ys, gl, ws, name):
    t = gl.shape[0]
    d = gl.shape[1] // N_BRANCH
    tr = _rows(t)

    def body(dm_ref, ya, yb, yc, gl_ref, wa, wb, wc, dya, dyb, dyc, dgl_ref, dwa, dwb, dwc):
        step = pl.program_id(0)
        dmv = dm_ref[...].astype(F32)
        for i, (y, w, dy, dw) in enumerate(((ya, wa, dya, dwa), (yb, wb, dyb, dwb), (yc, wc, dyc, dwc))):
            yb16 = y[...].astype(BF16)
            z = _dot(yb16, w[...])
            sg = jax.nn.sigmoid(gl_ref[:, i * d:(i + 1) * d])
            dgl_ref[:, i * d:(i + 1) * d] = (dmv * z * sg * (1.0 - sg)).astype(dgl_ref.dtype)
            e = (dmv * sg).astype(BF16)
            dy[...] = _dot_nt(e, w[...])
            contrib = _dot_tn(yb16, e)

            @pl.when(step == 0)
            def _(dw=dw, contrib=contrib):
                dw[...] = contrib

            @pl.when(step > 0)
            def _(dw=dw, contrib=contrib):
                dw[...] += contrib

    rows = [pl.BlockSpec((tr, y.shape[1]), lambda i: (i, 0)) for y in ys]
    wsp = [pl.BlockSpec(w.shape, lambda i: (0, 0)) for w in ws]
    gsp = pl.BlockSpec((tr, N_BRANCH * d), lambda i: (i, 0))
    return pl.pallas_call(
        body, name=name, grid=(t // tr,),
        in_specs=[pl.BlockSpec((tr, d), lambda i: (i, 0))] + rows + [gsp] + wsp,
        out_specs=rows + [gsp] + wsp,
        out_shape=[jax.ShapeDtypeStruct(y.shape, F32) for y in ys] + [jax.ShapeDtypeStruct(gl.shape, BF16)]
        + [jax.ShapeDtypeStruct(w.shape, F32) for w in ws],
        compiler_params=_cparams(("arbitrary",)),
    )(dm, *ys, gl, *ws)


def _adamw(w, m, v, gparts, name):
    depth = len(gparts)
    n, r, c = gparts[0].shape
    br = max(b for b in range(8, min(r, LANES) + 1, 8) if r % b == 0) if r % 8 == 0 else r
    nb = r // br
    c1 = 1.0 - ADAM_B1 ** ADAM_STEP
    c2 = 1.0 - ADAM_B2 ** ADAM_STEP

    def body(w_ref, m_ref, v_ref, *rest):
        g_refs, (go_ref, d_ref, mo_ref, vo_ref) = rest[:depth], rest[depth:]
        li = pl.program_id(0)

        def update(g_ref):
            g = g_ref[0].astype(F32)
            for i in range(1, n):
                g = g + g_ref[i].astype(F32)
            mn = ADAM_B1 * m_ref[...] + (1.0 - ADAM_B1) * g
            vn = ADAM_B2 * v_ref[...] + (1.0 - ADAM_B2) * (g * g)
            go_ref[...] = g
            mo_ref[...] = mn
            vo_ref[...] = vn
            d_ref[...] = -ADAM_LR * ((mn / c1) / (jnp.sqrt(vn / c2) + ADAM_EPS) + ADAM_WD * w_ref[...])

        for l in range(depth):
            pl.when(li == l)(functools.partial(update, g_refs[l]))

    def g_spec(l):
        return pl.BlockSpec((n, br, c), lambda li, i: (0, jnp.where(li == l, i, jnp.where(li < l, 0, nb - 1)), 0))

    blk = pl.BlockSpec((br, c), lambda li, i: (li * nb + i, 0))
    return pl.pallas_call(
        body, name=name, grid=(depth, nb),
        in_specs=[blk, blk, blk] + [g_spec(l) for l in range(depth)],
        out_specs=[blk] * 4, out_shape=[jax.ShapeDtypeStruct((depth * r, c), F32)] * 4,
        compiler_params=_cparams(("arbitrary", "arbitrary")),
    )(w, m, v, *gparts)


def _my_coords():
    return lax.axis_index("x"), lax.axis_index("y"), lax.axis_index("c")


COMM_SEMS = [pltpu.SemaphoreType.DMA((7,)), pltpu.SemaphoreType.DMA((7,)), pltpu.SemaphoreType.DMA]


class _Gather:
    def __init__(self, x_ref, out_ref, send_sems, recv_sems, local_sem):
        self.x_ref, self.out_ref = x_ref, out_ref
        self.send_sems, self.recv_sems, self.local_sem = send_sems, recv_sems, local_sem
        self.m_per = x_ref.shape[0]
        x, y, c = _my_coords()
        self.c = c
        self.me, self.sibling = (x, y, c), (x, y, 1 - c)
        self.chips = [(1 - x, y), (x, 1 - y), (1 - x, 1 - y)]

    def rows(self, px, py, pc):
        return self.out_ref.at[pl.ds((4 * px + 2 * py + pc) * self.m_per, self.m_per), :]

    def copy(self, k, block, to, src=None):
        return pltpu.make_async_remote_copy(
            src_ref=self.rows(*block) if src is None else src, dst_ref=self.rows(*block),
            send_sem=self.send_sems.at[k], recv_sem=self.recv_sems.at[k], device_id=to, device_id_type=MESH)

    def mine(self):
        return pltpu.make_async_copy(self.x_ref, self.rows(*self.me), self.local_sem)

    def first(self):
        out = [self.copy(0, self.me, self.sibling, src=self.x_ref)]
        return out + [self.copy(1 + j, self.me, (*chip, self.c), src=self.x_ref) for j, chip in enumerate(self.chips)]

    def passed(self):
        return [self.copy(4 + j, (*chip, self.c), self.sibling) for j, chip in enumerate(self.chips)]

    def start(self):
        self.mine().start()
        for cp in self.first():
            cp.start()

    def forward(self):
        passed = self.passed()
        for j, chip in enumerate(self.chips):
            self.copy(1 + j, (*chip, self.c), self.me).wait_recv()
            passed[j].start()

    def finish(self):
        self.copy(0, self.sibling, self.me).wait_recv()
        for j, chip in enumerate(self.chips):
            self.copy(4 + j, (*chip, 1 - self.c), self.me).wait_recv()
        for cp in self.first() + self.passed():
            cp.wait_send()
        self.mine().wait()


class _Exchange:
    def __init__(self, send_ref, recv_ref, send_sems, recv_sems, local_sem):
        self.send_ref, self.recv_ref = send_ref, recv_ref
        self.send_sems, self.recv_sems, self.local_sem = send_sems, recv_sems, local_sem
        x, y, c = _my_coords()
        self.me = 4 * x + 2 * y + c
        self.peers = []
        for k in range(1, N_DEV):
            px = 1 - x if k & 4 else x
            py = 1 - y if k & 2 else y
            pc = 1 - c if k & 1 else c
            self.peers.append((4 * px + 2 * py + pc, (px, py, pc)))

    def mine(self):
        return pltpu.make_async_copy(self.send_ref.at[self.me], self.recv_ref.at[self.me], self.local_sem)

    def copy(self, k, src_slot, dst_slot):
        return pltpu.make_async_remote_copy(
            src_ref=self.send_ref.at[src_slot], dst_ref=self.recv_ref.at[dst_slot],
            send_sem=self.send_sems.at[k], recv_sem=self.recv_sems.at[k],
            device_id=self.peers[k][1], device_id_type=MESH)

    def start(self):
        self.mine().start()
        for k, (peer, _) in enumerate(self.peers):
            self.copy(k, peer, self.me).start()

    def finish(self):
        for k, (peer, _) in enumerate(self.peers):
            self.copy(k, peer, self.me).wait_send()
            self.copy(k, self.me, peer).wait_recv()
        self.mine().wait()


def _all_gather(x_shard, in_vmem, with_sum, name):
    m_per, n = x_shard.shape

    def body(x_ref, out_ref, *rest):
        if with_sum:
            sum_ref, send_sems, recv_sems, local_sem = rest
        else:
            send_sems, recv_sems, local_sem = rest
        g = _Gather(x_ref, out_ref, send_sems, recv_sems, local_sem)
        g.start()
        g.forward()
        g.finish()
        if with_sum:
            acc = out_ref[pl.ds(0, m_per), :]
            for d in range(1, N_DEV):
                acc = acc + out_ref[pl.ds(d * m_per, m_per), :]
            sum_ref[...] = acc

    space = pltpu.VMEM if in_vmem else pl.ANY
    out_shape = [jax.ShapeDtypeStruct((N_DEV * m_per, n), x_shard.dtype)]
    out_specs = [pl.BlockSpec(memory_space=space)]
    if with_sum:
        out_shape.append(jax.ShapeDtypeStruct((m_per, n), x_shard.dtype))
        out_specs.append(pl.BlockSpec(memory_space=pltpu.VMEM))
    res = pl.pallas_call(
        body, name=name, out_shape=out_shape, in_specs=[pl.BlockSpec(memory_space=space)], out_specs=out_specs,
        scratch_shapes=COMM_SEMS, compiler_params=pltpu.CompilerParams(vmem_limit_bytes=VMEM_LIMIT),
    )(x_shard)
    return res if with_sum else res[0]


def _comm_sems(n):
    return [pltpu.SemaphoreType.DMA((n, 7)), pltpu.SemaphoreType.DMA((n, 7)), pltpu.SemaphoreType.DMA((n,))]


class _Many:
    def __init__(self, kind, ins, outs, send_sems, recv_sems, local_sems):
        self.parts = [kind(i, o, send_sems.at[b], recv_sems.at[b], local_sems.at[b])
                      for b, (i, o) in enumerate(zip(ins, outs))]

    def start(self):
        for part in self.parts:
            part.start()

    def forward(self):
        for part in self.parts:
            part.forward()

    def finish(self):
        for part in self.parts:
            part.finish()


def _gathered_shapes(shards):
    return [jax.ShapeDtypeStruct((N_DEV * s.shape[0],) + s.shape[1:], s.dtype) for s in shards]


def _all_gather_many(shards, name):
    n = len(shards)

    def body(*refs):
        g = _Many(_Gather, refs[:n], refs[n:2 * n], *refs[2 * n:])
        g.start()
        g.forward()
        g.finish()

    hbm = pl.BlockSpec(memory_space=pl.ANY)
    return pl.pallas_call(body, name=name, out_shape=_gathered_shapes(shards), in_specs=[hbm] * n,
                          out_specs=[hbm] * n, scratch_shapes=_comm_sems(n))(*shards)


def _all_to_all_many(sends, name):
    n = len(sends)

    def body(*refs):
        ex = _Many(_Exchange, refs[:n], refs[n:2 * n], *refs[2 * n:])
        ex.start()
        ex.finish()

    hbm = pl.BlockSpec(memory_space=pl.ANY)
    return pl.pallas_call(body, name=name, out_shape=[jax.ShapeDtypeStruct(s.shape, s.dtype) for s in sends],
                          in_specs=[hbm] * n, out_specs=[hbm] * n, scratch_shapes=_comm_sems(n))(*sends)


def _row(v):
    return v.reshape(1, -1)


def _ffn_fwd(x, w_in, w_out, g_pre, g_post, m, res_w, tag):
    shift, scale, gate = m[0], m[1], m[2]
    mpre = _row(g_pre * (1.0 + scale))
    mpost = _row(res_w * gate * g_post)
    h = _rms_fwd(x, mpre, _row(shift), None, BF16, tag + "_pre")
    ua, ub, s = _ffn_in_swiglu(h, w_in, tag + "_in")
    y = _matmul(s, w_out, name=tag + "_out")
    x_new = _rms_fwd(y, mpost, jnp.zeros_like(mpost), x, F32, tag + "_post")
    return x_new, (x, h, ua, ub, s, y, mpre, mpost)


def _sub_bwd_post(dx_new, y, mpost, g_post, gate, res_w, tag):
    dy, c1, _ = _rms_bwd(dx_new, y, mpost, None, BF16, tag + "_post_bwd")
    c1 = c1[0]
    return dy, c1 * res_w * g_post, c1 * res_w * gate


def _sub_bwd_pre(dh, x, mpre, dx_new, g_pre, scale, tag):
    dx, c2, c3 = _rms_bwd(dh, x, mpre, dx_new, F32, tag + "_pre_bwd")
    c2, c3 = c2[0], c3[0]
    return dx, c3, c2 * g_pre, c2 * (1.0 + scale)


def _ffn_bwd(dx_new, saved, w_in, w_out, g_pre, g_post, m, res_w, tag):
    x, h, ua, ub, s, y, mpre, mpost = saved
    scale, gate = m[1], m[2]
    dy, dgate, dg_post = _sub_bwd_post(dx_new, y, mpost, g_post, gate, res_w, tag)
    ds = _matmul(dy, w_out, tb=True, out_dtype=BF16, name=tag + "_out_dx")
    dw_out = _matmul(s, dy, ta=True, out_dtype=BF16, name=tag + "_out_dw")
    du = _swiglu_bwd(ua, ub, ds, tag + "_act_bwd")
    dh = _matmul(du, w_in, tb=True, name=tag + "_in_dx")
    dw_in = _matmul(h, du, ta=True, out_dtype=BF16, name=tag + "_in_dw")
    dx, dshift, dscale, dg_pre = _sub_bwd_pre(dh, x, mpre, dx_new, g_pre, scale, tag)
    return dx, dw_in, dw_out, jnp.stack([dshift, dscale, dgate]), dg_pre, dg_post


def _slope_cols(gi):
    _, r = C_GROUPS[gi]
    sl = jnp.asarray(_alibi_slopes(C_HEADS)[gi * C_HPG:(gi + 1) * C_HPG], F32) * float(r)
    return jnp.repeat(sl, HDIM).reshape(1, C_OUT)


def _mix_fwd(x, w, g_pre, g_post, m, lb, hn, tag, gather=None):
    t, d = x.shape
    shift, scale, gate = m[0], m[1], m[2]
    mpre = _row(g_pre * (1.0 + scale))
    mpost = _row(gate * g_post)
    h = _rms_fwd(x, mpre, _row(shift), None, BF16, tag + "_pre")
    p = _matmul(h, w["w_in"], name=tag + "_in")
    hn2 = _row(jnp.tile(hn, 2))
    ya, oa, states = _hgrn_fwd(p, _row(lb), hn2, tag + "_hgrn")
    kv = p[:, COL_BK:COL_CQ].astype(BF16)
    if gather is None:
        (yb, sb_tot), gathered = _sb_fwd(p, kv, tag + "_sb"), None
    else:
        res = _sb_fwd(p, kv, tag + "_sb_gather", gather)
        yb, sb_tot, gathered = res[0], res[1], list(res[2:])
    og, lg = zip(*[_dil_fwd(p, gi, tag + "_dil%d" % gi) for gi in range(len(C_GROUPS))])
    yc, lse_c = _dil_merge(og, lg, tag + "_dil_merge")
    gl = p[:, COL_GATE:]
    ws = (w["w_branch_a"], w["w_branch_b"], w["w_branch_c"])
    merged = _gate_fwd((ya, yb, yc), gl, ws, tag + "_gate")
    y = _matmul(merged, w["w_out"], name=tag + "_out")
    x_new = _rms_fwd(y, mpost, jnp.zeros_like(mpost), x, F32, tag + "_post")
    return x_new, (x, h, p, hn2, ya, oa, states, yb, kv, sb_tot, yc, lse_c, gl, merged, y, mpre, mpost), gathered


def _mix_bwd(dx_new, saved, w, g_pre, g_post, m, lb, tag, exchange=None):
    x, h, p, hn2, ya, oa, states, yb, kv, sb_tot, yc, lse_c, gl, merged, y, mpre, mpost = saved
    t = x.shape[0]
    scale, gate = m[1], m[2]
    dy, dgate, dg_post = _sub_bwd_post(dx_new, y, mpost, g_post, gate, 1.0, tag)
    dmerged = _matmul(dy, w["w_out"], tb=True, out_dtype=BF16, name=tag + "_out_dx")
    dw_out = _matmul(merged, dy, ta=True, out_dtype=BF16, name=tag + "_out_dw")
    ws = (w["w_branch_a"], w["w_branch_b"], w["w_branch_c"])
    dya, dyb, dyc, dgl, dwa, dwb, dwc = _gate_bwd(dmerged, (ya, yb, yc), gl, ws, tag + "_gate_bwd")
    dqa, dfa, dia, dga, dlb, dhn = _hgrn_bwd(p, _row(lb), hn2, oa, states, dya, tag + "_hgrn_bwd")
    if exchange is None:
        (dbq, dbk, dbv), received = _sb_bwd(p, kv, sb_tot, dyb, tag + "_sb_bwd"), None
    else:
        res = _sb_bwd(p, kv, sb_tot, dyb, tag + "_sb_bwd_exchange", exchange)
        dbq, dbk, dbv, received = res[0], res[1], res[2], list(res[3:])
    dcq, dck, dcv = zip(*[_dil_bwd(p, dyc, yc, lse_c, gi, tag + "_dil%d_bwd" % gi) for gi in range(len(C_GROUPS))])
    dil = [g.astype(BF16) for g in (*dcq, *dck, *dcv)]
    dp = jnp.concatenate([dqa, dfa, dia, dga, dbq, dbk, dbv, *dil, dgl], axis=1)
    dh = _matmul(dp, w["w_in"], tb=True, name=tag + "_in_dx")
    dw_in = _matmul(h, dp, ta=True, out_dtype=BF16, name=tag + "_in_dw")
    dx, dshift, dscale, dg_pre = _sub_bwd_pre(dh, x, mpre, dx_new, g_pre, scale, tag)
    dhn_v = jnp.sum(dhn, axis=(0, 1))
    dhn_v = dhn_v[:A_VDIM] + dhn_v[A_VDIM:]
    dws = dict(w_in=dw_in, w_out=dw_out, w_branch_a=dwa.astype(BF16), w_branch_b=dwb.astype(BF16),
               w_branch_c=dwc.astype(BF16))
    return dx, dws, jnp.stack([dshift, dscale, dgate]), dg_pre, dg_post, dlb[0], dhn_v, received


class _LocalWeights:
    def __init__(self, wts):
        self.wts = wts

    def first(self):
        return None

    def shard(self, l):
        return None

    def layer(self, l, gathered):
        return {k: v[l] for k, v in self.wts.items()}

    fused = False

    def pack(self, names, dws):
        return [dws[k] for k in names]

    def last(self, packed):
        return packed


class _ShardedWeights:
    def __init__(self, shards):
        self.shards = shards

    def shard(self, l):
        return [self.shards[k][l].astype(BF16) for k in BIG_WEIGHTS]

    def first(self):
        return _all_gather_many(self.shard(0), "weights_all_gather")

    def layer(self, l, gathered):
        out = {}
        for k, got in zip(BIG_WEIGHTS, gathered):
            _, r, c = self.shards[k].shape
            out[k] = got if k in ROW_SHARDED else got.reshape(N_DEV, r, c).transpose(1, 0, 2).reshape(r, N_DEV * c)
        return out

    fused = True

    def pack(self, names, dws):
        out = []
        for k in names:
            _, r, c = self.shards[k].shape
            g = dws[k]
            out.append(g.reshape(N_DEV, r, c) if k in ROW_SHARDED else g.reshape(r, N_DEV, c).transpose(1, 0, 2))
        return out

    def last(self, packed):
        return _all_to_all_many(packed, "grads_all_to_all")


def _local_step(x, target, mod, norm_g, lb_all, hnorm, supply):
    depth = mod.shape[0]
    d = x.shape[1]
    saved, wls = [], []
    gathered = supply.first()
    for l in range(depth):
        wl = supply.layer(l, gathered)
        wls.append(wl)
        x, s0 = _ffn_fwd(x, wl["ffn1_w_in"], wl["ffn1_w_out"], norm_g[l, 0], norm_g[l, 1], mod[l, 0], 0.5, "ffn1")
        nxt = supply.shard(l + 1) if l + 1 < depth else None
        x, s1, gathered = _mix_fwd(x, wl, norm_g[l, 2], norm_g[l, 3], mod[l, 1], lb_all[l], hnorm[l], "mix", nxt)
        x, s2 = _ffn_fwd(x, wl["ffn2_w_in"], wl["ffn2_w_out"], norm_g[l, 4], norm_g[l, 5], mod[l, 2], 0.5, "ffn2")
        saved.append((s0, s1, s2))
    dx, sq = _loss_head(x, target, "loss_head")
    loss = 0.5 * jnp.sum(sq) / d
    dmod, dng, dlb, dhn = [], [], [], []
    early = ("ffn2_w_in", "ffn2_w_out")
    late = tuple(k for k in BIG_WEIGHTS if k not in early)
    returned = {}
    waiting = []
    for l in reversed(range(depth)):
        wl = wls[l]
        s0, s1, s2 = saved[l]
        dx, dwi2, dwo2, dm2, dgp2, dgq2 = _ffn_bwd(dx, s2, wl["ffn2_w_in"], wl["ffn2_w_out"], norm_g[l, 4],
                                                   norm_g[l, 5], mod[l, 2], 0.5, "ffn2")
        waiting += zip([(l, k) for k in early], supply.pack(early, dict(ffn2_w_in=dwi2, ffn2_w_out=dwo2)))
        keys, bufs = [k for k, _ in waiting], [b for _, b in waiting]
        dx, dwm, dm1, dgp1, dgq1, dlb_l, dhn_l, received = _mix_bwd(
            dx, s1, wl, norm_g[l, 2], norm_g[l, 3], mod[l, 1], lb_all[l], "mix", bufs if supply.fused else None)
        returned.update(zip(keys, received if supply.fused else bufs))
        dx, dwi1, dwo1, dm0, dgp0, dgq0 = _ffn_bwd(dx, s0, wl["ffn1_w_in"], wl["ffn1_w_out"], norm_g[l, 0],
                                                   norm_g[l, 1], mod[l, 0], 0.5, "ffn1")
        dmod.append(jnp.stack([dm0, dm1, dm2]))
        dng.append(jnp.stack([dgp0, dgq0, dgp1, dgq1, dgp2, dgq2]))
        dlb.append(dlb_l)
        dhn.append(dhn_l)
        waiting = list(zip([(l, k) for k in late], supply.pack(late, dict(dwm, ffn1_w_in=dwi1, ffn1_w_out=dwo1))))
    returned.update(zip([k for k, _ in waiting], supply.last([b for _, b in waiting])))
    rev = lambda lst: jnp.stack(lst[::-1])
    return loss, dx, rev(dmod), rev(dng), rev(dlb), rev(dhn), returned


def _lb_all(logits):
    lb_p = jax.nn.softmax(logits.astype(F32), axis=0)
    return jnp.cumsum(lb_p, axis=0) - lb_p[0:1]


def _pad_rows(a, rows):
    return jnp.pad(a, ((0, rows - a.shape[0]), (0, 0)))


def kernel(x, c, w_ada, b_ada, norm_g, ffn1_w_in, ffn1_w_out, w_in, hgrn_lb_logits, hgrn_norm_g, w_branch_a, w_branch_b, w_branch_c, w_out, ffn2_w_in, ffn2_w_out, loss_target, m_w_ada, m_b_ada, m_norm_g, m_ffn1_w_in, m_ffn1_w_out, m_w_in, m_hgrn_lb_logits, m_hgrn_norm_g, m_w_branch_a, m_w_branch_b, m_w_branch_c, m_w_out, m_ffn2_w_in, m_ffn2_w_out, v_w_ada, v_b_ada, v_norm_g, v_ffn1_w_in, v_ffn1_w_out, v_w_in, v_hgrn_lb_logits, v_hgrn_norm_g, v_w_branch_a, v_w_branch_b, v_w_branch_c, v_w_out, v_ffn2_w_in, v_ffn2_w_out):
    weights = dict(w_ada=w_ada, b_ada=b_ada, norm_g=norm_g, ffn1_w_in=ffn1_w_in, ffn1_w_out=ffn1_w_out, w_in=w_in,
                   hgrn_lb_logits=hgrn_lb_logits, hgrn_norm_g=hgrn_norm_g, w_branch_a=w_branch_a,
                   w_branch_b=w_branch_b, w_branch_c=w_branch_c, w_out=w_out, ffn2_w_in=ffn2_w_in,
                   ffn2_w_out=ffn2_w_out)
    mom1 = dict(w_ada=m_w_ada, b_ada=m_b_ada, norm_g=m_norm_g, ffn1_w_in=m_ffn1_w_in, ffn1_w_out=m_ffn1_w_out,
                w_in=m_w_in, hgrn_lb_logits=m_hgrn_lb_logits, hgrn_norm_g=m_hgrn_norm_g, w_branch_a=m_w_branch_a,
                w_branch_b=m_w_branch_b, w_branch_c=m_w_branch_c, w_out=m_w_out, ffn2_w_in=m_ffn2_w_in,
                ffn2_w_out=m_ffn2_w_out)
    mom2 = dict(w_ada=v_w_ada, b_ada=v_b_ada, norm_g=v_norm_g, ffn1_w_in=v_ffn1_w_in, ffn1_w_out=v_ffn1_w_out,
                w_in=v_w_in, hgrn_lb_logits=v_hgrn_lb_logits, hgrn_norm_g=v_hgrn_norm_g, w_branch_a=v_w_branch_a,
                w_branch_b=v_w_branch_b, w_branch_c=v_w_branch_c, w_out=v_w_out, ffn2_w_in=v_ffn2_w_in,
                ffn2_w_out=v_ffn2_w_out)
    order = list(weights)
    depth, d, ada_cols = w_ada.shape
    nd = d // LANES
    xi, yi, ci = _my_coords()
    me = 4 * xi + 2 * yi + ci

    small = jnp.concatenate([c.reshape(nd, LANES), norm_g.reshape(depth * 6, LANES)], axis=0)
    g1 = _all_gather(small, True, False, "small_all_gather").reshape(N_DEV, small.shape[0], LANES)
    c_act = _silu(g1[:, :nd].reshape(N_DEV, d))
    norm_full = g1[:, nd:].reshape(N_DEV, depth, 6, LANES).transpose(1, 2, 0, 3).reshape(depth, 6, d)

    c_pad = _pad_rows(c_act, 16)
    mod_sh = jnp.stack([_matmul(c_pad, w_ada[l], name="ada_mod")[:N_DEV]
                        + lax.dynamic_slice_in_dim(b_ada[l], me * ada_cols, ada_cols)[None]
                        for l in range(depth)])
    g2 = _all_gather(mod_sh.reshape(-1, LANES), True, False, "mod_all_gather")
    g2 = g2.reshape(N_DEV, depth, N_DEV, ada_cols)
    mod = lax.dynamic_index_in_dim(g2, me, axis=2, keepdims=False)
    mod = mod.transpose(1, 0, 2).reshape(depth, 3, 3, d)

    supply = _ShardedWeights({k: weights[k] for k in BIG_WEIGHTS})
    lb_all, lb_vjp = jax.vjp(_lb_all, hgrn_lb_logits)

    loss, dx, dmod, dng, dlb, dhn, received = _local_step(x[0], loss_target[0], mod, norm_full, lb_all,
                                                          hgrn_norm_g, supply)
    loss = lax.psum(loss, ("x", "y", "c"))

    dhn_pad = jnp.pad(dhn.reshape(-1), (0, 8 * LANES - dhn.size))
    pieces = [dmod.reshape(-1), dng.reshape(-1), dlb.reshape(-1), dhn_pad]
    sizes = [p_.size for p_ in pieces]
    smallg = jnp.concatenate(pieces).reshape(-1, LANES)
    g3, gsum = _all_gather(smallg, True, True, "small_grads_all_gather")
    g3 = g3.reshape(N_DEV, -1)
    gsum = gsum.reshape(-1)
    dmod_all = g3[:, :sizes[0]].reshape(N_DEV, depth, 9 * d)
    o1 = sizes[0]
    grads = {}
    grads["b_ada"] = gsum[:o1].reshape(depth, 9 * d)
    dng_sum = gsum[o1:o1 + sizes[1]].reshape(depth, 6, nd, LANES)
    grads["norm_g"] = lax.dynamic_index_in_dim(dng_sum, me, axis=2, keepdims=False)
    o2 = o1 + sizes[1]
    dlb_sum = gsum[o2:o2 + sizes[2]].reshape(depth, A_QK)
    grads["hgrn_lb_logits"] = lb_vjp(dlb_sum)[0]
    o3 = o2 + sizes[2]
    grads["hgrn_norm_g"] = gsum[o3:o3 + dhn.size].reshape(depth, A_VDIM)
    dmod_mine = lax.dynamic_slice_in_dim(dmod_all, me * ada_cols, ada_cols, axis=2)
    grads["w_ada"] = jnp.stack([_matmul(c_pad, _pad_rows(dmod_mine[:, l], 16), ta=True, name="ada_dw")
                                for l in range(depth)])

    outs = {}
    for k in order:
        w = weights[k]
        w2 = w.reshape(-1, w.shape[-1])
        if k in BIG_WEIGHTS:
            gp = [received[l, k] for l in range(depth)]
        else:
            gp = [grads[k].reshape((1,) + w2.shape)]
        res = _adamw(w2, mom1[k].reshape(w2.shape), mom2[k].reshape(w2.shape), gp, "adamw")
        outs[k] = [r.reshape(w.shape) for r in res]
    return (loss, dx[None], *[outs[k][0] for k in order], *[outs[k][1] for k in order],
            *[outs[k][2] for k in order], *[outs[k][3] for k in order])
```

```python
import functools
import math

import jax
import jax.numpy as jnp
from jax import lax
from jax.experimental import pallas as pl
from jax.experimental.pallas import tpu as pltpu

F32 = jnp.float32
BF16 = jnp.bfloat16

A_HEADS, A_KDIM, A_VDIM, A_CHUNK = 6, 128, 64, 64
B_HEADS, HDIM = 6, 64
C_GROUPS = ((128, 1), (512, 4), (2048, 16))
C_HPG = 4
C_HEADS = C_HPG * len(C_GROUPS)
N_BRANCH = 3
EPS = 1e-6
NEG_BIG = -1e30
TINY = 1e-30
A_QK = A_HEADS * A_KDIM
A_V = A_HEADS * A_VDIM
B_W = B_HEADS * HDIM
C_W = C_HEADS * HDIM
C_OUT = C_HPG * HDIM
COL_AQ, COL_AF, COL_AI, COL_AG = 0, A_QK, 2 * A_QK, 2 * A_QK + A_V
COL_BQ = 2 * A_QK + 2 * A_V
COL_BK, COL_BV = COL_BQ + B_W, COL_BQ + 2 * B_W
COL_CQ = COL_BQ + 3 * B_W
COL_CK, COL_CV = COL_CQ + C_W, COL_CQ + 2 * C_W
COL_GATE = COL_CQ + 3 * C_W

ADAM_LR, ADAM_B1, ADAM_B2, ADAM_EPS, ADAM_WD, ADAM_STEP = 0.001, 0.9, 0.999, 1e-08, 0.01, 10

N_DEV = 8
LANES = 128
VMEM_LIMIT = 48 * 1024 * 1024
MATMUL_VMEM_BUDGET = 28 * 1024 * 1024
ELEMENTWISE_ROWS = 512
SUB = 16
EXP_CLAMP = 80.0
MESH = pl.DeviceIdType.MESH

BIG_WEIGHTS = ("ffn1_w_in", "ffn1_w_out", "w_in", "w_branch_a", "w_branch_b", "w_branch_c", "w_out",
               "ffn2_w_in", "ffn2_w_out")
ROW_SHARDED = ("ffn1_w_out", "w_out", "ffn2_w_out")


def _cparams(sem):
    return pltpu.CompilerParams(dimension_semantics=sem, vmem_limit_bytes=VMEM_LIMIT)


def _tile(n, cap):
    best, t = None, LANES
    while t <= min(n, cap):
        if n % t == 0:
            best = t
        t += LANES
    return best or n


def _rows(t, cap=256):
    r = cap
    while t % r:
        r //= 2
    return r


def _divisors(n):
    return [t for t in range(LANES, n + 1, LANES) if n % t == 0] or [n]


def _matmul_tiles(m, n, k, a_size, b_size, o_size):
    best, best_key = None, None
    for tm in _divisors(m):
        for tn in _divisors(n):
            for tk in _divisors(k):
                if tm > 1024 or tn > 3072 or tk > 4096:
                    continue
                cast = (tm * tk * 2 if a_size > 2 else 0) + (tk * tn * 2 if b_size > 2 else 0)
                need = 2 * (tm * tk * a_size + tk * tn * b_size + tm * tn * o_size) + 2 * tm * tn * 4 + cast
                if need > MATMUL_VMEM_BUDGET:
                    continue
                key = (tm * tn * tk, tk)
                if best_key is None or key > best_key:
                    best, best_key = (tm, tn, tk), key
    return best


def _dot(a, b):
    return jnp.dot(a, b, preferred_element_type=F32)


def _dot_nt(a, b):
    return lax.dot_general(a, b, (((1,), (1,)), ((), ())), preferred_element_type=F32)


def _dot_tn(a, b):
    return lax.dot_general(a, b, (((0,), (0,)), ((), ())), preferred_element_type=F32)


def _split3(x):
    h = x.astype(BF16)
    r = x - h.astype(F32)
    m = r.astype(BF16)
    lo = (r - m.astype(F32)).astype(BF16)
    return h, m, lo


def _ones_left(mat01, x):
    h, m, lo = _split3(x)
    return _dot(mat01, h) + _dot(mat01, m) + _dot(mat01, lo)


def _silu(x):
    return x * jax.nn.sigmoid(x)


def _dsilu(x):
    s = jax.nn.sigmoid(x)
    return s * (1.0 + x * (1.0 - s))


def _matmul(a, b, *, ta=False, tb=False, out_dtype=F32, name):
    if ta:
        kdim, m = a.shape
    else:
        m, kdim = a.shape
    n = b.shape[0] if tb else b.shape[1]
    tm, tn, tk = _matmul_tiles(m, n, kdim, a.dtype.itemsize, b.dtype.itemsize, jnp.dtype(out_dtype).itemsize)
    nk = kdim // tk
    ni, nj = m // tm, n // tn
    a_bytes, b_bytes = m * kdim * a.dtype.itemsize, kdim * n * b.dtype.itemsize
    j_outer = nk == 1 and (b_bytes + a_bytes * nj) < (a_bytes + b_bytes * ni)
    dims = (((0 if ta else 1,), (1 if tb else 0,)), ((), ()))

    def body(a_ref, b_ref, o_ref, *scratch):
        p = lax.dot_general(a_ref[...].astype(BF16), b_ref[...].astype(BF16), dims, preferred_element_type=F32)
        if nk == 1:
            o_ref[...] = p.astype(o_ref.dtype)
            return
        acc = scratch[0]
        k = pl.program_id(2)

        @pl.when(k == 0)
        def _():
            acc[...] = p

        @pl.when(k > 0)
        def _():
            acc[...] += p

        @pl.when(k == nk - 1)
        def _():
            o_ref[...] = acc[...].astype(o_ref.dtype)

    def spec(shape, pick):
        if j_outer:
            return pl.BlockSpec(shape, lambda j, i, k: pick(i, j, k))
        return pl.BlockSpec(shape, lambda i, j, k: pick(i, j, k))

    a_spec = spec((tk, tm), lambda i, j, k: (k, i)) if ta else spec((tm, tk), lambda i, j, k: (i, k))
    b_spec = spec((tn, tk), lambda i, j, k: (j, k)) if tb else spec((tk, tn), lambda i, j, k: (k, j))
    return pl.pallas_call(
        body, name=name, grid=(nj, ni, nk) if j_outer else (ni, nj, nk), in_specs=[a_spec, b_spec],
        out_specs=spec((tm, tn), lambda i, j, k: (i, j)),
        out_shape=jax.ShapeDtypeStruct((m, n), out_dtype),
        scratch_shapes=[pltpu.VMEM((tm, tn), F32)] if nk > 1 else [],
        compiler_params=_cparams(("parallel", "parallel", "arbitrary")),
    )(a, b)


def _rms_fwd(z, mcol, acol, res, out_dtype, name):
    t, d = z.shape
    tr = _rows(t, ELEMENTWISE_ROWS)
    has_res = res is not None

    def body(*refs):
        if has_res:
            z_ref, m_ref, a_ref, r_ref, o_ref = refs
        else:
            z_ref, m_ref, a_ref, o_ref = refs
        zf = z_ref[...]
        r = lax.rsqrt(jnp.mean(zf * zf, axis=-1, keepdims=True) + EPS)
        y = zf * r * m_ref[...] + a_ref[...]
        if has_res:
            y = r_ref[...] + y
        o_ref[...] = y.astype(o_ref.dtype)

    row = pl.BlockSpec((tr, d), lambda i: (i, 0))
    col = pl.BlockSpec((1, d), lambda i: (0, 0))
    ins = [z, mcol, acol] + ([res] if has_res else [])
    return pl.pallas_call(
        body, name=name, grid=(t // tr,), in_specs=[row, col, col] + ([row] if has_res else []),
        out_specs=row, out_shape=jax.ShapeDtypeStruct((t, d), out_dtype),
        compiler_params=_cparams(("parallel",)),
    )(*ins)


def _rms_bwd(d_out, z, mcol, dres, out_dtype, name):
    t, d = z.shape
    tr = _rows(t, ELEMENTWISE_ROWS)
    has_res = dres is not None

    def body(*refs):
        if has_res:
            d_ref, z_ref, m_ref, r_ref, o_ref, s1_ref, s2_ref = refs
        else:
            d_ref, z_ref, m_ref, o_ref, s1_ref, s2_ref = refs
        i = pl.program_id(0)
        zf = z_ref[...]
        r = lax.rsqrt(jnp.mean(zf * zf, axis=-1, keepdims=True) + EPS)
        zh = zf * r
        df = d_ref[...].astype(F32)
        dzh = df * m_ref[...]
        dz = r * (dzh - zh * jnp.mean(dzh * zh, axis=-1, keepdims=True))
        if has_res:
            dz = dz + r_ref[...]
        o_ref[...] = dz.astype(o_ref.dtype)
        s1 = jnp.sum(df * zh, axis=0, keepdims=True)
        s2 = jnp.sum(df, axis=0, keepdims=True)

        @pl.when(i == 0)
        def _():
            s1_ref[...] = s1
            s2_ref[...] = s2

        @pl.when(i > 0)
        def _():
            s1_ref[...] += s1
            s2_ref[...] += s2

    row = pl.BlockSpec((tr, d), lambda i: (i, 0))
    col = pl.BlockSpec((1, d), lambda i: (0, 0))
    ins = [d_out, z, mcol] + ([dres] if has_res else [])
    return pl.pallas_call(
        body, name=name, grid=(t // tr,), in_specs=[row, row, col] + ([row] if has_res else []),
        out_specs=[row, col, col],
        out_shape=[jax.ShapeDtypeStruct((t, d), out_dtype), jax.ShapeDtypeStruct((1, d), F32),
                   jax.ShapeDtypeStruct((1, d), F32)],
        compiler_params=_cparams(("arbitrary",)),
    )(*ins)


FFN_IN_TILE = (512, 1408)


def _ffn_in_swiglu(h, w_in, name):
    t, d = h.shape
    f = w_in.shape[1] // 2
    tm, tn = _tile(t, FFN_IN_TILE[0]), _tile(f, FFN_IN_TILE[1])
    nj = f // tn

    def body(h_ref, wa_ref, wb_ref, a_ref, b_ref, s_ref):
        hv = h_ref[...].astype(BF16)
        a = _dot(hv, wa_ref[...].astype(BF16))
        b = _dot(hv, wb_ref[...].astype(BF16))
        a_ref[...] = a.astype(BF16)
        b_ref[...] = b.astype(BF16)
        s_ref[...] = (_silu(a) * b).astype(BF16)

    out = pl.BlockSpec((tm, tn), lambda j, i: (i, j))
    return pl.pallas_call(
        body, name=name, grid=(nj, t // tm),
        in_specs=[pl.BlockSpec((tm, d), lambda j, i: (i, 0)), pl.BlockSpec((d, tn), lambda j, i: (0, j)),
                  pl.BlockSpec((d, tn), lambda j, i: (0, j + nj))],
        out_specs=[out, out, out], out_shape=[jax.ShapeDtypeStruct((t, f), BF16)] * 3,
        compiler_params=_cparams(("parallel", "parallel")),
    )(h, w_in, w_in)


def _swiglu_bwd(ua, ub, ds, name):
    t, f = ua.shape
    tr = _rows(t, ELEMENTWISE_ROWS)

    def body(a_ref, b_ref, ds_ref, du_ref):
        a = a_ref[...].astype(F32)
        b = b_ref[...].astype(F32)
        g = ds_ref[...].astype(F32)
        du_ref[:, :f] = (g * b * _dsilu(a)).astype(du_ref.dtype)
        du_ref[:, f:] = (g * _silu(a)).astype(du_ref.dtype)

    half = pl.BlockSpec((tr, f), lambda i: (i, 0))
    return pl.pallas_call(
        body, name=name, grid=(t // tr,), in_specs=[half, half, half],
        out_specs=pl.BlockSpec((tr, 2 * f), lambda i: (i, 0)), out_shape=jax.ShapeDtypeStruct((t, 2 * f), BF16),
        compiler_params=_cparams(("parallel",)),
    )(ua, ub, ds)


def _loss_head(y, target, name):
    t, d = y.shape
    tr = _rows(t, ELEMENTWISE_ROWS)

    def body(y_ref, t_ref, dy_ref, sq_ref):
        i = pl.program_id(0)
        e = y_ref[...] - t_ref[...]
        dy_ref[...] = e * (1.0 / d)
        s = jnp.sum(e * e, axis=0, keepdims=True)

        @pl.when(i == 0)
        def _():
            sq_ref[...] = s

        @pl.when(i > 0)
        def _():
            sq_ref[...] += s

    row = pl.BlockSpec((tr, d), lambda i: (i, 0))
    col = pl.BlockSpec((1, d), lambda i: (0, 0))
    return pl.pallas_call(
        body, name=name, grid=(t // tr,), in_specs=[row, row], out_specs=[row, col],
        out_shape=[jax.ShapeDtypeStruct((t, d), F32), jax.ShapeDtypeStruct((1, d), F32)],
        compiler_params=_cparams(("arbitrary",)),
    )(y, target)


def _hgrn_consts():
    c = A_CHUNK
    shift = SUB.bit_length() - 1
    r = lax.broadcasted_iota(jnp.int32, (c, c), 0)
    s = lax.broadcasted_iota(jnp.int32, (c, c), 1)
    sub_r = lax.shift_right_logical(r, shift)
    incl = s <= r
    masks = [jnp.logical_and(sub_r == i, incl) for i in range(c // SUB)]
    rev_incl = jnp.where(s >= r, 1.0, 0.0).astype(BF16)
    r2 = lax.broadcasted_iota(jnp.int32, (2 * c + 8, c), 0)
    s2 = lax.broadcasted_iota(jnp.int32, (2 * c + 8, c), 1)
    sub_start = lax.shift_left(lax.shift_right_logical(r2 - c, shift), shift)
    running = jnp.where(s2 <= r2, 1.0, 0.0)
    before = jnp.where(s2 < sub_start, 1.0, 0.0)
    stack = jnp.where(r2 < c, running, jnp.where(r2 < 2 * c, before, 1.0)).astype(BF16)
    return stack, masks, incl, rev_incl


def _hgrn_gates(q_raw, f_raw, lbv, stack):
    sg = jax.nn.sigmoid(f_raw)
    sgn = jax.nn.sigmoid(-f_raw)
    f = lbv + (1.0 - lbv) * sg
    logf = jnp.log(jnp.maximum(f, TINY))
    return dict(sg=sg, sgn=sgn, f=f, k=(1.0 - lbv) * sgn, q=_silu(q_raw), bb=_ones_left(stack, logf))


def _hgrn_chunk(q_raw, f_raw, lbv, stack):
    return _hgrn_decays(_hgrn_gates(q_raw, f_raw, lbv, stack))


def _hgrn_decays(gates):
    c = A_CHUNK
    sg, sgn, f, k, q, bb = (gates[n] for n in ("sg", "sgn", "f", "k", "q", "bb"))
    b = bb[:c]
    bsrow = bb[c:2 * c]
    b_end = bb[2 * c:2 * c + 1]
    e_sub = jnp.exp(b - bsrow)
    e_b = jnp.exp(b)
    e_end = jnp.exp(b_end - b)
    qs = q * e_sub
    q_in = q * e_b
    kend = k * e_end
    kfac = [jnp.exp(jnp.minimum(bsrow[i * SUB:i * SUB + 1] - b, EXP_CLAMP)) for i in range(c // SUB)]
    return dict(sg=sg, sgn=sgn, f=f, k=k, q=q, b=b, b_end=b_end, e_sub=e_sub, e_b=e_b, e_end=e_end,
                qs=qs, q_in=q_in, kend=kend, kfac=kfac)


def _hgrn_scores(ch, masks):
    qs_b = ch["qs"].astype(BF16)
    a = None
    for i, mk in enumerate(masks):
        ki = (ch["k"] * ch["kfac"][i]).astype(BF16)
        part = jnp.where(mk, _dot_nt(qs_b, ki), 0.0)
        a = part if a is None else a + part
    return a


def _hgrn_fwd(p, lb, hn2, name):
    t = p.shape[0]
    tb = _rows(t)
    nt = t // tb
    nc = tb // A_CHUNK
    c = A_CHUNK

    def body(q_ref, f_ref, i_ref, g_ref, lb_ref, hn_ref, y_ref, o_ref, st_ref, s_scr):
        j = pl.program_id(1)

        @pl.when(j == 0)
        def _():
            s_scr[...] = jnp.zeros_like(s_scr)

        stack, masks, _, _ = _hgrn_consts()
        units = [(ci, hh) for ci in range(nc) for hh in range(2)]
        lsl = [slice(A_KDIM * hh, A_KDIM * (hh + 1)) for hh in range(2)]
        hsl = [slice(A_VDIM * hh, A_VDIM * (hh + 1)) for hh in range(2)]
        rows = [pl.ds(ci * c, c) for ci in range(nc)]
        gates = {u: _hgrn_gates(q_ref[rows[u[0]], lsl[u[1]]], f_ref[rows[u[0]], lsl[u[1]]], lb_ref[:, lsl[u[1]]], stack)
                 for u in units}
        ch = {u: _hgrn_decays(gates[u]) for u in units}
        v = {u: i_ref[rows[u[0]], hsl[u[1]]].astype(BF16) for u in units}
        a = {u: _hgrn_scores(ch[u], masks).astype(BF16) for u in units}
        grow = {u: _dot_tn(v[u], ch[u]["kend"].astype(BF16)) for u in units}
        states = [s_scr[0], s_scr[1]]
        entering = {}
        for ci, hh in units:
            entering[ci, hh] = states[hh]
            st_ref[hh, ci] = states[hh]
            states[hh] = states[hh] * jnp.exp(ch[ci, hh]["b_end"]) + grow[ci, hh]
        s_scr[0] = states[0]
        s_scr[1] = states[1]
        for u in units:
            o_ref[rows[u[0]], hsl[u[1]]] = (_dot_nt(ch[u]["q_in"].astype(BF16), entering[u].astype(BF16))
                                            + _dot(a[u], v[u]))
        for hh in range(2):
            hsl = slice(A_VDIM * hh, A_VDIM * (hh + 1))
            o = o_ref[:, hsl]
            r = lax.rsqrt(jnp.mean(o * o, axis=-1, keepdims=True) + EPS)
            y_ref[:, hsl] = (o * r * hn_ref[:, hsl] * _silu(g_ref[:, hsl])).astype(y_ref.dtype)

    w2 = 2 * A_KDIM
    return pl.pallas_call(
        body, name=name, grid=(A_HEADS // 2, nt),
        in_specs=[pl.BlockSpec((tb, w2), lambda h, j: (j, COL_AQ // w2 + h)),
                  pl.BlockSpec((tb, w2), lambda h, j: (j, COL_AF // w2 + h)),
                  pl.BlockSpec((tb, LANES), lambda h, j: (j, COL_AI // LANES + h)),
                  pl.BlockSpec((tb, LANES), lambda h, j: (j, COL_AG // LANES + h)),
                  pl.BlockSpec((1, w2), lambda h, j: (0, h)),
                  pl.BlockSpec((1, LANES), lambda h, j: (0, 0))],
        out_specs=[pl.BlockSpec((tb, LANES), lambda h, j: (j, h)),
                   pl.BlockSpec((tb, LANES), lambda h, j: (j, h)),
                   pl.BlockSpec((2, nc, A_VDIM, A_KDIM), lambda h, j: (h, j, 0, 0))],
        out_shape=[jax.ShapeDtypeStruct((t, A_V), BF16), jax.ShapeDtypeStruct((t, A_V), F32),
                   jax.ShapeDtypeStruct((A_HEADS, t // c, A_VDIM, A_KDIM), F32)],
        scratch_shapes=[pltpu.VMEM((2, A_VDIM, A_KDIM), F32)],
        compiler_params=_cparams(("parallel", "arbitrary")),
    )(p, p, p, p, lb, hn2)


def _hgrn_bwd(p, lb, hn2, o_raw, states, dya, name):
    t = p.shape[0]
    tb = _rows(t)
    nt = t // tb
    nc = tb // A_CHUNK
    c = A_CHUNK

    def body(q_ref, f_ref, i_ref, g_ref, lb_ref, hn_ref, o_ref, st_ref, dy_ref,
             dq_ref, df_ref, di_ref, dg_ref, dlb_ref, dhn_ref, ds_scr, do_scr):
        j = pl.program_id(1)

        @pl.when(j == 0)
        def _():
            ds_scr[...] = jnp.zeros_like(ds_scr)
            dlb_ref[...] = jnp.zeros_like(dlb_ref)
            dhn_ref[...] = jnp.zeros_like(dhn_ref)

        stack, masks, incl, rev_incl = _hgrn_consts()
        for hh in range(2):
            hsl = slice(A_VDIM * hh, A_VDIM * (hh + 1))
            o = o_ref[:, hsl]
            g = g_ref[:, hsl]
            dy = dy_ref[:, hsl].astype(F32)
            hn = hn_ref[:, hsl]
            r = lax.rsqrt(jnp.mean(o * o, axis=-1, keepdims=True) + EPS)
            oh = o * r
            sgate = _silu(g)
            dg_ref[:, hsl] = (dy * oh * hn * _dsilu(g)).astype(dg_ref.dtype)
            dhn_ref[0, :, hsl] += jnp.sum(dy * oh * sgate, axis=0, keepdims=True)
            doh = dy * hn * sgate
            do_scr[:, hsl] = r * (doh - oh * jnp.mean(doh * oh, axis=-1, keepdims=True))

        units = [(ci, hh) for ci in reversed(range(nc)) for hh in range(2)]
        lsl = [slice(A_KDIM * hh, A_KDIM * (hh + 1)) for hh in range(2)]
        hsl = [slice(A_VDIM * hh, A_VDIM * (hh + 1)) for hh in range(2)]
        rows = [pl.ds(ci * c, c) for ci in range(nc)]
        q_raw = {u: q_ref[rows[u[0]], lsl[u[1]]] for u in units}
        gates = {u: _hgrn_gates(q_raw[u], f_ref[rows[u[0]], lsl[u[1]]], lb_ref[:, lsl[u[1]]], stack) for u in units}
        ch = {u: _hgrn_decays(gates[u]) for u in units}
        v = {u: i_ref[rows[u[0]], hsl[u[1]]].astype(BF16) for u in units}
        do_b = {u: do_scr[rows[u[0]], hsl[u[1]]].astype(BF16) for u in units}
        st = {u: st_ref[u[1], u[0]] for u in units}
        qs_b = {u: ch[u]["qs"].astype(BF16) for u in units}
        a_b = {u: _hgrn_scores(ch[u], masks).astype(BF16) for u in units}
        da = {u: jnp.where(incl, _dot_nt(do_b[u], v[u]), 0.0) for u in units}
        dq_x = {u: _dot(do_b[u], st[u].astype(BF16)) for u in units}
        grow = {u: _dot_tn(do_b[u], ch[u]["q_in"].astype(BF16)) for u in units}
        dstates = [ds_scr[0], ds_scr[1]]
        leaving = {}
        for ci, hh in units:
            leaving[ci, hh] = dstates[hh]
            dstates[hh] = dstates[hh] * jnp.exp(ch[ci, hh]["b_end"]) + grow[ci, hh]
        for hh in range(2):
            ds_scr[hh] = dstates[hh]
        dst_b = {u: leaving[u].astype(BF16) for u in units}
        dv = {u: _dot_tn(a_b[u], do_b[u]) + _dot_nt(ch[u]["kend"].astype(BF16), dst_b[u]) for u in units}
        dk_x = {u: _dot(v[u], dst_b[u]) for u in units}
        dlb_acc = [jnp.zeros((1, A_KDIM), F32), jnp.zeros((1, A_KDIM), F32)]
        for u in units:
            ci, hh = u
            cu = ch[u]
            lbv = lb_ref[:, lsl[hh]]
            dq_i = None
            dk_i = None
            kdk_i = None
            for i, mk in enumerate(masks):
                dam = jnp.where(mk, da[u], 0.0).astype(BF16)
                ki = (cu["k"] * cu["kfac"][i]).astype(BF16)
                pq = _dot(dam, ki)
                pk = _dot_tn(dam, qs_b[u])
                dq_i = pq if dq_i is None else dq_i + pq
                dk_i = cu["kfac"][i] * pk if dk_i is None else dk_i + cu["kfac"][i] * pk
                kdk_i = ki.astype(F32) * pk if kdk_i is None else kdk_i + ki.astype(F32) * pk
            dq = cu["e_sub"] * dq_i + cu["e_b"] * dq_x[u]
            dk = dk_i + cu["e_end"] * dk_x[u]
            kx = cu["kend"] * dk_x[u]
            db = (qs_b[u].astype(F32) * dq_i + cu["q_in"] * dq_x[u]) - (kdk_i + kx)
            later = (jnp.exp(cu["b_end"]) * jnp.sum(leaving[u] * st[u], axis=0, keepdims=True)
                     + jnp.sum(kx, axis=0, keepdims=True))
            dlogf = later + _ones_left(rev_incl, db)
            dfv = jnp.where(cu["f"] > TINY, dlogf / cu["f"], 0.0)
            dq_ref[rows[ci], lsl[hh]] = (dq * _dsilu(q_raw[u])).astype(dq_ref.dtype)
            df_ref[rows[ci], lsl[hh]] = ((1.0 - lbv) * cu["sg"] * cu["sgn"] * (dfv - dk)).astype(df_ref.dtype)
            dlb_acc[hh] = dlb_acc[hh] + jnp.sum(dfv * (1.0 - cu["sg"]) - dk * cu["sgn"], axis=0, keepdims=True)
            di_ref[rows[ci], hsl[hh]] = dv[u].astype(di_ref.dtype)
        for hh in range(2):
            dlb_ref[:, A_KDIM * hh:A_KDIM * (hh + 1)] += dlb_acc[hh]

    w2 = 2 * A_KDIM
    rev = lambda j: nt - 1 - j
    return pl.pallas_call(
        body, name=name, grid=(A_HEADS // 2, nt),
        in_specs=[pl.BlockSpec((tb, w2), lambda h, j: (rev(j), COL_AQ // w2 + h)),
                  pl.BlockSpec((tb, w2), lambda h, j: (rev(j), COL_AF // w2 + h)),
                  pl.BlockSpec((tb, LANES), lambda h, j: (rev(j), COL_AI // LANES + h)),
                  pl.BlockSpec((tb, LANES), lambda h, j: (rev(j), COL_AG // LANES + h)),
                  pl.BlockSpec((1, w2), lambda h, j: (0, h)),
                  pl.BlockSpec((1, LANES), lambda h, j: (0, 0)),
                  pl.BlockSpec((tb, LANES), lambda h, j: (rev(j), h)),
                  pl.BlockSpec((2, nc, A_VDIM, A_KDIM), lambda h, j: (h, rev(j), 0, 0)),
                  pl.BlockSpec((tb, LANES), lambda h, j: (rev(j), h))],
        out_specs=[pl.BlockSpec((tb, w2), lambda h, j: (rev(j), h)),
                   pl.BlockSpec((tb, w2), lambda h, j: (rev(j), h)),
                   pl.BlockSpec((tb, LANES), lambda h, j: (rev(j), h)),
                   pl.BlockSpec((tb, LANES), lambda h, j: (rev(j), h)),
                   pl.BlockSpec((1, w2), lambda h, j: (0, h)),
                   pl.BlockSpec((1, 1, LANES), lambda h, j: (h, 0, 0))],
        out_shape=[jax.ShapeDtypeStruct((t, A_QK), BF16), jax.ShapeDtypeStruct((t, A_QK), BF16),
                   jax.ShapeDtypeStruct((t, A_V), BF16), jax.ShapeDtypeStruct((t, A_V), BF16),
                   jax.ShapeDtypeStruct((1, A_QK), F32), jax.ShapeDtypeStruct((A_HEADS // 2, 1, LANES), F32)],
        scratch_shapes=[pltpu.VMEM((2, A_VDIM, A_KDIM), F32), pltpu.VMEM((tb, LANES), F32)],
        compiler_params=_cparams(("parallel", "arbitrary")),
    )(p, p, p, p, lb, hn2, o_raw, states, dya)


BLK = 128
SCALE = HDIM ** -0.5
SB_CHUNK = 4


def _softplus(z):
    return jnp.maximum(z, 0.0) + jnp.log(1.0 + jnp.exp(-jnp.abs(z)))


def _sb_sum_matrix(keep, with_total=False):
    width = 2 * BLK if with_total else BLK
    sp = lax.broadcasted_iota(jnp.int32, (BLK, width), 0)
    s = lax.broadcasted_iota(jnp.int32, (BLK, width), 1)
    return jnp.where(jnp.logical_or(s >= BLK, keep(sp, s)), 1.0, 0.0).astype(BF16)


def _lanes(col):
    return jnp.broadcast_to(col, (BLK, BLK))


def _sb_fwd(p, kv, name, gather=None):
    t = p.shape[0]
    nq = t // BLK
    nh = B_HEADS // 2
    cw = SB_CHUNK * BLK
    fused = gather is not None
    n = len(gather) if fused else 0

    def body(*refs):
        q_ref, kb, vb = refs[:3]
        o_ref, tot_ref = refs[3 + n:5 + n]
        zbuf, stage, sbuf, abuf = refs[5 + 2 * n:9 + 2 * n]
        hp = pl.program_id(0)
        qi = pl.program_id(1)
        if fused:
            g = _Many(_Gather, refs[3:3 + n], refs[5 + n:5 + 2 * n], *refs[9 + 2 * n:])
            pl.when(jnp.logical_and(hp == 0, qi == 0))(g.start)
            pl.when(jnp.logical_and(hp == nh - 1, qi == 0))(g.forward)

        @pl.when(qi == 0)
        def _():
            abuf[...] = jnp.zeros_like(abuf)

        row = lax.broadcasted_iota(jnp.int32, (BLK, BLK), 0)
        col = lax.broadcasted_iota(jnp.int32, (BLK, BLK), 1)
        sums = _sb_sum_matrix(lambda sp, s: sp >= s, True)
        hsl = [slice(HDIM * h, HDIM * (h + 1)) for h in range(2)]
        nchunk = qi // SB_CHUNK + 1
        for h in range(2):
            zbuf[h] = _dot_nt((q_ref[:, hsl[h]] * SCALE).astype(BF16), kb[:, hsl[h]])

        col_minus_row = col - row

        def causal(j):
            return col_minus_row < (qi - j) * BLK

        def l_pass(c, carry):
            for b in range(SB_CHUNK):
                j = c * SB_CHUNK + b
                off = pl.multiple_of(j * BLK, BLK)
                mask = causal(j)
                for h in range(2):
                    lm = jnp.where(mask, -_softplus(zbuf[h, :, pl.ds(off, BLK)]), 0.0)
                    stage[h, pl.ds(off, BLK), :] = lm.astype(BF16)
            return carry

        lax.fori_loop(0, nchunk, l_pass, 0)

        def sum_pass(c, carry):
            rows = pl.ds(pl.multiple_of(c * cw, cw), cw)
            for h in range(2):
                sbuf[h, rows, :] = _dot(stage[h, rows, :], sums)
            return carry

        lax.fori_loop(0, nchunk, sum_pass, 0)

        def a_pass(it, carry):
            c = nchunk - 1 - it
            runs = list(carry)
            for b in reversed(range(SB_CHUNK)):
                j = c * SB_CHUNK + b
                off = pl.multiple_of(j * BLK, BLK)
                mask = causal(j)
                for h in range(2):
                    s = sbuf[h, pl.ds(off, BLK), :BLK]
                    a = jnp.where(mask, jnp.exp(zbuf[h, :, pl.ds(off, BLK)] + s + runs[h]), 0.0)
                    abuf[h, :, pl.ds(off, BLK)] = a.astype(BF16)
                    runs[h] = runs[h] + sbuf[h, pl.ds(off, BLK), BLK:]
            return tuple(runs)

        zero = jnp.zeros((BLK, BLK), F32)
        runs = lax.fori_loop(0, nchunk, a_pass, (zero, zero))
        for h in range(2):
            tot_ref[:, hsl[h]] = runs[h][:, :HDIM]
            o_ref[:, hsl[h]] = _dot(abuf[h], vb[:, hsl[h]])
        if fused:
            pl.when(jnp.logical_and(hp == nh - 1, qi == nq - 1))(g.finish)

    out_blk = pl.BlockSpec((BLK, LANES), lambda h, i: (i, h))
    hbm = pl.BlockSpec(memory_space=pl.ANY)
    in_specs = [pl.BlockSpec((BLK, LANES), lambda h, i: (i, COL_BQ // LANES + h)),
                pl.BlockSpec((t, LANES), lambda h, i: (0, h)),
                pl.BlockSpec((t, LANES), lambda h, i: (0, B_W // LANES + h))]
    out_shape = [jax.ShapeDtypeStruct((t, B_W), F32)] * 2
    scratch = [pltpu.VMEM((2, BLK, t), F32), pltpu.VMEM((2, t, BLK), BF16),
               pltpu.VMEM((2, t, 2 * BLK), F32), pltpu.VMEM((2, BLK, t), BF16)]
    if fused:
        out_shape = out_shape + _gathered_shapes(gather)
    return pl.pallas_call(
        body, name=name, grid=(nh, nq),
        in_specs=in_specs + [hbm] * n,
        out_specs=[out_blk, out_blk] + [hbm] * n,
        out_shape=out_shape,
        scratch_shapes=scratch + (_comm_sems(n) if fused else []),
        compiler_params=_cparams(("arbitrary", "arbitrary")),
    )(p, kv, kv, *(gather if fused else []))


def _sb_bwd(p, kv, tot, do, name, exchange=None):
    t = p.shape[0]
    nq = t // BLK
    nh = B_HEADS // 2
    cw = SB_CHUNK * BLK
    fused = exchange is not None
    n = len(exchange) if fused else 0

    def body(*refs):
        q_ref, kb, vb, tot_ref, do_ref = refs[:5]
        dq_ref, dk_ref, dv_ref = refs[5 + n:8 + n]
        zbuf, dabuf, lbuf, stage, sbuf, abuf, dzbuf, dkt, dvt = refs[8 + 2 * n:17 + 2 * n]
        hp = pl.program_id(0)
        qi = pl.program_id(1)
        if fused:
            ex = _Many(_Exchange, refs[5:5 + n], refs[8 + n:8 + 2 * n], *refs[17 + 2 * n:])
            pl.when(jnp.logical_and(hp == 0, qi == 0))(ex.start)

        @pl.when(qi == 0)
        def _():
            dkt[...] = jnp.zeros_like(dkt)
            dvt[...] = jnp.zeros_like(dvt)
            dzbuf[...] = jnp.zeros_like(dzbuf)
            abuf[...] = jnp.zeros_like(abuf)

        row = lax.broadcasted_iota(jnp.int32, (BLK, BLK), 0)
        col = lax.broadcasted_iota(jnp.int32, (BLK, BLK), 1)
        sums = _sb_sum_matrix(lambda sp, s: sp <= s)
        hsl = [slice(HDIM * h, HDIM * (h + 1)) for h in range(2)]
        dob = [do_ref[:, hsl[h]].astype(BF16) for h in range(2)]
        total =[jnp.concatenate([tot_ref[:, hsl[h]], tot_ref[:, hsl[h]]], axis=1) for h in range(2)]
        nchunk = qi // SB_CHUNK + 1
        for h in range(2):
            zbuf[h] = _dot_nt((q_ref[:, hsl[h]] * SCALE).astype(BF16), kb[:, hsl[h]])
            dabuf[h] = _dot_nt(dob[h], vb[:, hsl[h]])

        col_minus_row = col - row

        def causal(j):
            return col_minus_row < (qi - j) * BLK

        def blocks(c):
            for b in range(SB_CHUNK):
                j = c * SB_CHUNK + b
                yield j, pl.ds(pl.multiple_of(j * BLK, BLK), BLK)

        def l_pass(c, carry):
            for j, blk_ in blocks(c):
                mask = causal(j)
                for h in range(2):
                    lm = jnp.where(mask, -_softplus(zbuf[h, :, blk_]), 0.0)
                    lbuf[h, :, blk_] = lm
                    stage[h, blk_, :] = lm.astype(BF16)
            return carry

        lax.fori_loop(0, nchunk, l_pass, 0)

        def sum_pass():
            def run_(c, carry):
                rows = pl.ds(pl.multiple_of(c * cw, cw), cw)
                for h in range(2):
                    sbuf[h, rows, :] = _dot(stage[h, rows, :], sums)
                return carry
            lax.fori_loop(0, nchunk, run_, 0)

        sum_pass()

        def g_pass(c, carry):
            runs = list(carry)
            for j, blk_ in blocks(c):
                mask = causal(j)
                for h in range(2):
                    upto = sbuf[h, blk_, :]
                    log_a = zbuf[h, :, blk_] + lbuf[h, :, blk_] + (total[h] - runs[h] - upto)
                    a = jnp.where(mask, jnp.exp(log_a), 0.0)
                    abuf[h, :, blk_] = a.astype(BF16)
                    g = a * dabuf[h, :, blk_]
                    dabuf[h, :, blk_] = g
                    stage[h, blk_, :] = g.astype(BF16)
                    runs[h] = runs[h] + _lanes(upto[:, BLK - 1:BLK])
            return tuple(runs)

        zero = jnp.zeros((BLK, BLK), F32)
        lax.fori_loop(0, nchunk, g_pass, (zero, zero))
        sum_pass()

        def dz_pass(c, carry):
            runs = list(carry)
            for j, blk_ in blocks(c):
                mask = causal(j)
                for h in range(2):
                    lm = lbuf[h, :, blk_]
                    g = dabuf[h, :, blk_]
                    upto = sbuf[h, blk_, :]
                    before = runs[h] + upto - g
                    dz = jnp.where(mask, g * jnp.exp(lm) - jnp.exp(zbuf[h, :, blk_] + lm) * before, 0.0)
                    dzbuf[h, :, blk_] = (dz * SCALE).astype(BF16)
                    runs[h] = runs[h] + _lanes(upto[:, BLK - 1:BLK])
            return tuple(runs)

        lax.fori_loop(0, nchunk, dz_pass, (zero, zero))
        for h in range(2):
            dq_ref[:, hsl[h]] = _dot(dzbuf[h], kb[:, hsl[h]]).astype(dq_ref.dtype)
        q_t = q_ref[...].T.astype(BF16)
        do_t = do_ref[...].T.astype(BF16)
        for h in range(2):
            dkt[hsl[h], :] += _dot(q_t[hsl[h], :], dzbuf[h])
            dvt[hsl[h], :] += _dot(do_t[hsl[h], :], abuf[h])

        @pl.when(qi == nq - 1)
        def _():
            dk_ref[...] = dkt[...].T.astype(dk_ref.dtype)
            dv_ref[...] = dvt[...].T.astype(dv_ref.dtype)

        if fused:
            pl.when(jnp.logical_and(hp == nh - 1, qi == nq - 1))(ex.finish)

    blk = lambda h, i: (i, h)
    whole = lambda h, i: (0, h)
    hbm = pl.BlockSpec(memory_space=pl.ANY)
    in_specs = [pl.BlockSpec((BLK, LANES), lambda h, i: (i, COL_BQ // LANES + h)),
                pl.BlockSpec((t, LANES), lambda h, i: (0, h)),
                pl.BlockSpec((t, LANES), lambda h, i: (0, B_W // LANES + h)),
                pl.BlockSpec((BLK, LANES), blk), pl.BlockSpec((BLK, LANES), blk)]
    out_specs = [pl.BlockSpec((BLK, LANES), blk), pl.BlockSpec((t, LANES), whole), pl.BlockSpec((t, LANES), whole)]
    out_shape = [jax.ShapeDtypeStruct((t, B_W), BF16)] * 3
    scratch = [pltpu.VMEM((2, BLK, t), F32), pltpu.VMEM((2, BLK, t), F32), pltpu.VMEM((2, BLK, t), F32),
               pltpu.VMEM((2, t, BLK), BF16), pltpu.VMEM((2, t, BLK), F32), pltpu.VMEM((2, BLK, t), BF16),
               pltpu.VMEM((2, BLK, t), BF16), pltpu.VMEM((LANES, t), F32), pltpu.VMEM((LANES, t), F32)]
    if fused:
        out_shape = out_shape + [jax.ShapeDtypeStruct(e.shape, e.dtype) for e in exchange]
    return pl.pallas_call(
        body, name=name, grid=(nh, nq),
        in_specs=in_specs + [hbm] * n,
        out_specs=out_specs + [hbm] * n,
        out_shape=out_shape,
        scratch_shapes=scratch + (_comm_sems(n) if fused else []),
        compiler_params=_cparams(("arbitrary", "arbitrary")),
    )(p, kv, kv, tot, do, *(exchange if fused else []))


def _alibi_slopes(n):
    def pow2(m):
        start = 2.0 ** (-8.0 / m)
        return [start ** (i + 1) for i in range(m)]
    if math.log2(n).is_integer():
        s = pow2(n)
    else:
        c = 2 ** int(math.floor(math.log2(n)))
        s = pow2(c) + pow2(2 * c)[0::2][: n - c]
    return sorted(s, reverse=True)


def _dil_scores(qh, kh, sl, prev, exists=None):
    row = lax.broadcasted_iota(jnp.int32, (BLK, BLK), 0)
    col = lax.broadcasted_iota(jnp.int32, (BLK, BLK), 1)
    dist = row - col + (BLK if prev else 0)
    if prev:
        valid = (col - row) >= jnp.where(exists, 0, 2 * BLK)
    else:
        valid = col <= row
    s = _dot_nt(qh, kh) - sl * dist.astype(F32)
    return s, valid


DIL_UNITS = 4


def _dil_plan(r):
    per_trip = min(r, DIL_UNITS)
    return per_trip, DIL_UNITS // per_trip


def _dil_rows(b, rho, r):
    return pl.ds(b * BLK * r + rho, BLK, stride=r) if r > 1 else pl.ds(b * BLK, BLK)


def _dil_fwd(p, gi, name):
    t = p.shape[0]
    _, r = C_GROUPS[gi]
    per_trip, nsub = _dil_plan(r)
    sbr = BLK * r * nsub
    nsb = t // sbr
    slope_cols = _slope_cols(gi)

    def body(q_ref, kc_ref, kp_ref, vc_ref, vp_ref, sl_ref, o_ref, lse_ref):
        i = pl.program_id(1)

        hsl = [slice(HDIM * h, HDIM * (h + 1)) for h in range(2)]
        sl = [sl_ref[:, HDIM * h:HDIM * h + 1] for h in range(2)]

        def residues(it, carry):
            pairs = [(b, dr) for b in range(nsub) for dr in range(per_trip)]
            units = [(pr, h) for pr in pairs for h in range(2)]
            rows = {(b, dr): _dil_rows(b, it * per_trip + dr, r) for b, dr in pairs}
            blocks, prev_exists = {}, {}
            for b, dr in pairs:
                rw = rows[b, dr]
                if b == 0:
                    before = _dil_rows(nsub - 1, it * per_trip + dr, r)
                    kp, vp, prev_exists[b, dr] = kp_ref[before, :], vp_ref[before, :], i > 0
                else:
                    before = rows[b - 1, dr]
                    kp, vp, prev_exists[b, dr] = kc_ref[before, :], vc_ref[before, :], True
                blocks[b, dr] = [q_ref[rw, :], kc_ref[rw, :], kp, vc_ref[rw, :], vp]
            qh = {u: (blocks[u[0]][0][:, hsl[u[1]]] * SCALE).astype(BF16) for u in units}
            sc = {u: _dil_scores(qh[u], blocks[u[0]][1][:, hsl[u[1]]].astype(BF16), sl[u[1]], False) for u in units}
            sp = {u: _dil_scores(qh[u], blocks[u[0]][2][:, hsl[u[1]]].astype(BF16), sl[u[1]], True, prev_exists[u[0]])
                  for u in units}
            pc, pp, den, lse = {}, {}, {}, {}
            for u in units:
                s_c = jnp.where(sc[u][1], sc[u][0], NEG_BIG)
                s_p = jnp.where(sp[u][1], sp[u][0], NEG_BIG)
                m = jnp.maximum(jnp.max(s_c, axis=1, keepdims=True), jnp.max(s_p, axis=1, keepdims=True))
                pc[u] = jnp.exp(s_c - m)
                pp[u] = jnp.exp(s_p - m)
                den[u] = jnp.sum(pc[u], axis=1, keepdims=True) + jnp.sum(pp[u], axis=1, keepdims=True)
                lse[u] = jnp.broadcast_to(m + jnp.log(den[u]), (BLK, HDIM))
            o = {u: (_dot(pc[u].astype(BF16), blocks[u[0]][3][:, hsl[u[1]]].astype(BF16))
                     + _dot(pp[u].astype(BF16), blocks[u[0]][4][:, hsl[u[1]]].astype(BF16))) / den[u] for u in units}
            for pr in pairs:
                o_ref[rows[pr], :] = jnp.concatenate([o[pr, 0], o[pr, 1]], axis=1)
                lse_ref[rows[pr], :] = jnp.concatenate([lse[pr, 0], lse[pr, 1]], axis=1)
            return carry

        lax.fori_loop(0, r // per_trip, residues, 0)

    def at(col0, pick):
        return pl.BlockSpec((sbr, LANES), lambda c, i: (pick(i), col0 // LANES + c))

    cur = lambda i: i
    prv = lambda i: jnp.maximum(i - 1, 0)
    cq, ck, cv = COL_CQ + gi * C_OUT, COL_CK + gi * C_OUT, COL_CV + gi * C_OUT
    out = pl.BlockSpec((sbr, LANES), lambda c, i: (i, c))
    return pl.pallas_call(
        body, name=name, grid=(C_OUT // LANES, nsb),
        in_specs=[at(cq, cur), at(ck, cur), at(ck, prv), at(cv, cur), at(cv, prv),
                  pl.BlockSpec((1, LANES), lambda c, i: (0, c))],
        out_specs=[out, out], out_shape=[jax.ShapeDtypeStruct((t, C_OUT), F32)] * 2,
        compiler_params=_cparams(("parallel", "parallel")),
    )(p, p, p, p, p, slope_cols)


def _dil_bwd(p, do, o, lse, gi, name):
    t = p.shape[0]
    _, r = C_GROUPS[gi]
    per_trip, nsub = _dil_plan(r)
    sbr = BLK * r * nsub
    nsb = t // sbr
    slope_cols = _slope_cols(gi)

    def body(q_ref, qn_ref, kc_ref, kp_ref, vc_ref, vp_ref, do_ref, don_ref, o_ref, on_ref, l_ref, ln_ref, sl_ref,
             dq_ref, dk_ref, dv_ref):
        i = pl.program_id(1)

        hsl = [slice(HDIM * h, HDIM * (h + 1)) for h in range(2)]
        sl = [sl_ref[:, HDIM * h:HDIM * h + 1] for h in range(2)]

        def residues(it, carry):
            pairs = [(b, dr) for b in range(nsub) for dr in range(per_trip)]
            units = [(pr, h) for pr in pairs for h in range(2)]
            rows = {(b, dr): _dil_rows(b, it * per_trip + dr, r) for b, dr in pairs}
            blocks, has_prev, has_next = {}, {}, {}
            for b, dr in pairs:
                rw = rows[b, dr]
                if b == 0:
                    before = _dil_rows(nsub - 1, it * per_trip + dr, r)
                    kp, vp, has_prev[b, dr] = kp_ref[before, :], vp_ref[before, :], i > 0
                else:
                    kp, vp, has_prev[b, dr] = kc_ref[rows[b - 1, dr], :], vc_ref[rows[b - 1, dr], :], True
                if b == nsub - 1:
                    after = _dil_rows(0, it * per_trip + dr, r)
                    nxt = [ref[after, :] for ref in (qn_ref, don_ref, on_ref, ln_ref)]
                    has_next[b, dr] = i < nsb - 1
                else:
                    nxt = [ref[rows[b + 1, dr], :] for ref in (q_ref, do_ref, o_ref, l_ref)]
                    has_next[b, dr] = True
                blocks[b, dr] = [q_ref[rw, :], nxt[0], kc_ref[rw, :], kp, vc_ref[rw, :], vp, do_ref[rw, :], nxt[1],
                                 o_ref[rw, :], nxt[2], l_ref[rw, :], nxt[3]]
            part = lambda u, k: blocks[u[0]][k][:, hsl[u[1]]]
            qb = {u: part(u, 0).astype(BF16) for u in units}
            qnb = {u: part(u, 1).astype(BF16) for u in units}
            qh = {u: (part(u, 0) * SCALE).astype(BF16) for u in units}
            qnh = {u: (part(u, 1) * SCALE).astype(BF16) for u in units}
            kc = {u: part(u, 2).astype(BF16) for u in units}
            kp = {u: part(u, 3).astype(BF16) for u in units}
            vc = {u: part(u, 4).astype(BF16) for u in units}
            vp = {u: part(u, 5).astype(BF16) for u in units}
            dob = {u: part(u, 6).astype(BF16) for u in units}
            donb = {u: part(u, 7).astype(BF16) for u in units}
            delta = {u: jnp.sum(part(u, 6) * part(u, 8), axis=1, keepdims=True) for u in units}
            deltan = {u: jnp.sum(part(u, 7) * part(u, 9), axis=1, keepdims=True) for u in units}
            lse_c = {u: part(u, 10)[:, :1] for u in units}
            lse_n = {u: part(u, 11)[:, :1] for u in units}
            s_cc = {u: _dil_scores(qh[u], kc[u], sl[u[1]], False) for u in units}
            s_cp = {u: _dil_scores(qh[u], kp[u], sl[u[1]], True, has_prev[u[0]]) for u in units}
            s_nc = {u: _dil_scores(qnh[u], kc[u], sl[u[1]], True, has_next[u[0]]) for u in units}
            da_cc = {u: _dot_nt(dob[u], vc[u]) for u in units}
            da_cp = {u: _dot_nt(dob[u], vp[u]) for u in units}
            da_nc = {u: _dot_nt(donb[u], vc[u]) for u in units}

            def prob(s_ok, lse_col):
                s, ok = s_ok
                return jnp.where(ok, jnp.exp(jnp.where(ok, s, NEG_BIG) - lse_col), 0.0)

            p_cc = {u: prob(s_cc[u], lse_c[u]) for u in units}
            p_cp = {u: prob(s_cp[u], lse_c[u]) for u in units}
            p_nc = {u: prob(s_nc[u], lse_n[u]) for u in units}
            ds_cc = {u: (p_cc[u] * (da_cc[u] - delta[u]) * SCALE).astype(BF16) for u in units}
            ds_cp = {u: (p_cp[u] * (da_cp[u] - delta[u]) * SCALE).astype(BF16) for u in units}
            ds_nc = {u: (p_nc[u] * (da_nc[u] - deltan[u]) * SCALE).astype(BF16) for u in units}
            dq = {u: _dot(ds_cc[u], kc[u]) + _dot(ds_cp[u], kp[u]) for u in units}
            dk = {u: _dot_tn(ds_cc[u], qb[u]) + _dot_tn(ds_nc[u], qnb[u]) for u in units}
            dv = {u: _dot_tn(p_cc[u].astype(BF16), dob[u]) + _dot_tn(p_nc[u].astype(BF16), donb[u]) for u in units}
            for pr in pairs:
                dq_ref[rows[pr], :] = jnp.concatenate([dq[pr, 0], dq[pr, 1]], axis=1)
                dk_ref[rows[pr], :] = jnp.concatenate([dk[pr, 0], dk[pr, 1]], axis=1)
                dv_ref[rows[pr], :] = jnp.concatenate([dv[pr, 0], dv[pr, 1]], axis=1)
            return carry

        lax.fori_loop(0, r // per_trip, residues, 0)

    def at(col0, pick):
        return pl.BlockSpec((sbr, LANES), lambda c, i: (pick(i), col0 // LANES + c))

    cur = lambda i: i
    prv = lambda i: jnp.maximum(i - 1, 0)
    nxt = lambda i: jnp.minimum(i + 1, nsb - 1)
    cq, ck, cv = COL_CQ + gi * C_OUT, COL_CK + gi * C_OUT, COL_CV + gi * C_OUT
    return pl.pallas_call(
        body, name=name, grid=(C_OUT // LANES, nsb),
        in_specs=[at(cq, cur), at(cq, nxt), at(ck, cur), at(ck, prv), at(cv, cur), at(cv, prv),
                  at(0, cur), at(0, nxt), at(0, cur), at(0, nxt), at(0, cur), at(0, nxt),
                  pl.BlockSpec((1, LANES), lambda c, i: (0, c))],
        out_specs=[at(0, cur)] * 3, out_shape=[jax.ShapeDtypeStruct((t, C_OUT), F32)] * 3,
        compiler_params=_cparams(("parallel", "parallel")),
    )(p, p, p, p, p, p, do, do, o, o, lse, lse, slope_cols)


def _dil_merge(os_, ls_, name):
    t, w = os_[0].shape
    tr = _rows(t, ELEMENTWISE_ROWS)

    def body(o0, o1, o2, l0, l1, l2, y_ref, lse_ref):
        a, b, c = l0[...], l1[...], l2[...]
        m = jnp.maximum(jnp.maximum(a, b), c)
        ea, eb, ec = jnp.exp(a - m), jnp.exp(b - m), jnp.exp(c - m)
        den = ea + eb + ec
        y_ref[...] = (ea * o0[...] + eb * o1[...] + ec * o2[...]) / den
        lse_ref[...] = m + jnp.log(den)

    row = pl.BlockSpec((tr, w), lambda i: (i, 0))
    return pl.pallas_call(
        body, name=name, grid=(t // tr,), in_specs=[row] * 6, out_specs=[row, row],
        out_shape=[jax.ShapeDtypeStruct((t, w), F32)] * 2, compiler_params=_cparams(("parallel",)),
    )(*os_, *ls_)


def _gate_fwd(ys, gl, ws, name):
    t = gl.shape[0]
    d = gl.shape[1] // N_BRANCH
    tr = _rows(t)

    def body(ya, yb, yc, gl_ref, wa, wb, wc, m_ref):
        acc = None
        for i, (y, w) in enumerate(((ya, wa), (yb, wb), (yc, wc))):
            z = _dot(y[...].astype(BF16), w[...])
            term = jax.nn.sigmoid(gl_ref[:, i * d:(i + 1) * d]) * z
            acc = term if acc is None else acc + term
        m_ref[...] = acc.astype(m_ref.dtype)

    rows = [pl.BlockSpec((tr, y.shape[1]), lambda i: (i, 0)) for y in ys]
    wsp = [pl.BlockSpec(w.shape, lambda i: (0, 0)) for w in ws]
    return pl.pallas_call(
        body, name=name, grid=(t // tr,),
        in_specs=rows + [pl.BlockSpec((tr, N_BRANCH * d), lambda i: (i, 0))] + wsp,
        out_specs=pl.BlockSpec((tr, d), lambda i: (i, 0)), out_shape=jax.ShapeDtypeStruct((t, d), BF16),
        compiler_params=_cparams(("parallel",)),
    )(*ys, gl, *ws)


def _gate_bwd(dm, ys, gl, ws, name):
    t = gl.shape[0]
    d = gl.shape[1] // N_BRANCH
    tr = _rows(t)

    def body(dm_ref, ya, yb, yc, gl_ref, wa, wb, wc, dya, dyb, dyc, dgl_ref, dwa, dwb, dwc):
        step = pl.program_id(0)
        dmv = dm_ref[...].astype(F32)
        for i, (y, w, dy, dw) in enumerate(((ya, wa, dya, dwa), (yb, wb, dyb, dwb), (yc, wc, dyc, dwc))):
            yb16 = y[...].astype(BF16)
            z = _dot(yb16, w[...])
            sg = jax.nn.sigmoid(gl_ref[:, i * d:(i + 1) * d])
            dgl_ref[:, i * d:(i + 1) * d] = (dmv * z * sg * (1.0 - sg)).astype(dgl_ref.dtype)
            e = (dmv * sg).astype(BF16)
            dy[...] = _dot_nt(e, w[...])
            contrib = _dot_tn(yb16, e)

            @pl.when(step == 0)
            def _(dw=dw, contrib=contrib):
                dw[...] = contrib

            @pl.when(step > 0)
            def _(dw=dw, contrib=contrib):
                dw[...] += contrib

    rows = [pl.BlockSpec((tr, y.shape[1]), lambda i: (i, 0)) for y in ys]
    wsp = [pl.BlockSpec(w.shape, lambda i: (0, 0)) for w in ws]
    gsp = pl.BlockSpec((tr, N_BRANCH * d), lambda i: (i, 0))
    return pl.pallas_call(
        body, name=name, grid=(t // tr,),
        in_specs=[pl.BlockSpec((tr, d), lambda i: (i, 0))] + rows + [gsp] + wsp,
        out_specs=rows + [gsp] + wsp,
        out_shape=[jax.ShapeDtypeStruct(y.shape, F32) for y in ys] + [jax.ShapeDtypeStruct(gl.shape, BF16)]
        + [jax.ShapeDtypeStruct(w.shape, F32) for w in ws],
        compiler_params=_cparams(("arbitrary",)),
    )(dm, *ys, gl, *ws)


def _adamw(w, m, v, gparts, name):
    depth = len(gparts)
    n, r, c = gparts[0].shape
    br = max(b for b in range(8, min(r, LANES) + 1, 8) if r % b == 0) if r % 8 == 0 else r
    nb = r // br
    c1 = 1.0 - ADAM_B1 ** ADAM_STEP
    c2 = 1.0 - ADAM_B2 ** ADAM_STEP

    def body(w_ref, m_ref, v_ref, *rest):
        g_refs, (go_ref, d_ref, mo_ref, vo_ref) = rest[:depth], rest[depth:]
        li = pl.program_id(0)

        def update(g_ref):
            g = g_ref[0].astype(F32)
            for i in range(1, n):
                g = g + g_ref[i].astype(F32)
            mn = ADAM_B1 * m_ref[...] + (1.0 - ADAM_B1) * g
            vn = ADAM_B2 * v_ref[...] + (1.0 - ADAM_B2) * (g * g)
            go_ref[...] = g
            mo_ref[...] = mn
            vo_ref[...] = vn
            d_ref[...] = -ADAM_LR * ((mn / c1) / (jnp.sqrt(vn / c2) + ADAM_EPS) + ADAM_WD * w_ref[...])

        for l in range(depth):
            pl.when(li == l)(functools.partial(update, g_refs[l]))

    def g_spec(l):
        return pl.BlockSpec((n, br, c), lambda li, i: (0, jnp.where(li == l, i, jnp.where(li < l, 0, nb - 1)), 0))

    blk = pl.BlockSpec((br, c), lambda li, i: (li * nb + i, 0))
    return pl.pallas_call(
        body, name=name, grid=(depth, nb),
        in_specs=[blk, blk, blk] + [g_spec(l) for l in range(depth)],
        out_specs=[blk] * 4, out_shape=[jax.ShapeDtypeStruct((depth * r, c), F32)] * 4,
        compiler_params=_cparams(("arbitrary", "arbitrary")),
    )(w, m, v, *gparts)


def _my_coords():
    return lax.axis_index("x"), lax.axis_index("y"), lax.axis_index("c")


COMM_SEMS = [pltpu.SemaphoreType.DMA((7,)), pltpu.SemaphoreType.DMA((7,)), pltpu.SemaphoreType.DMA]


class _Gather:
    def __init__(self, x_ref, out_ref, send_sems, recv_sems, local_sem):
        self.x_ref, self.out_ref = x_ref, out_ref
        self.send_sems, self.recv_sems, self.local_sem = send_sems, recv_sems, local_sem
        self.m_per = x_ref.shape[0]
        x, y, c = _my_coords()
        self.c = c
        self.me, self.sibling = (x, y, c), (x, y, 1 - c)
        self.chips = [(1 - x, y), (x, 1 - y), (1 - x, 1 - y)]

    def rows(self, px, py, pc):
        return self.out_ref.at[pl.ds((4 * px + 2 * py + pc) * self.m_per, self.m_per), :]

    def copy(self, k, block, to, src=None):
        return pltpu.make_async_remote_copy(
            src_ref=self.rows(*block) if src is None else src, dst_ref=self.rows(*block),
            send_sem=self.send_sems.at[k], recv_sem=self.recv_sems.at[k], device_id=to, device_id_type=MESH)

    def mine(self):
        return pltpu.make_async_copy(self.x_ref, self.rows(*self.me), self.local_sem)

    def first(self):
        out = [self.copy(0, self.me, self.sibling, src=self.x_ref)]
        return out + [self.copy(1 + j, self.me, (*chip, self.c), src=self.x_ref) for j, chip in enumerate(self.chips)]

    def passed(self):
        return [self.copy(4 + j, (*chip, self.c), self.sibling) for j, chip in enumerate(self.chips)]

    def start(self):
        self.mine().start()
        for cp in self.first():
            cp.start()

    def forward(self):
        passed = self.passed()
        for j, chip in enumerate(self.chips):
            self.copy(1 + j, (*chip, self.c), self.me).wait_recv()
            passed[j].start()

    def finish(self):
        self.copy(0, self.sibling, self.me).wait_recv()
        for j, chip in enumerate(self.chips):
            self.copy(4 + j, (*chip, 1 - self.c), self.me).wait_recv()
        for cp in self.first() + self.passed():
            cp.wait_send()
        self.mine().wait()


class _Exchange:
    def __init__(self, send_ref, recv_ref, send_sems, recv_sems, local_sem):
        self.send_ref, self.recv_ref = send_ref, recv_ref
        self.send_sems, self.recv_sems, self.local_sem = send_sems, recv_sems, local_sem
        x, y, c = _my_coords()
        self.me = 4 * x + 2 * y + c
        self.peers = []
        for k in range(1, N_DEV):
            px = 1 - x if k & 4 else x
            py = 1 - y if k & 2 else y
            pc = 1 - c if k & 1 else c
            self.peers.append((4 * px + 2 * py + pc, (px, py, pc)))

    def mine(self):
        return pltpu.make_async_copy(self.send_ref.at[self.me], self.recv_ref.at[self.me], self.local_sem)

    def copy(self, k, src_slot, dst_slot):
        return pltpu.make_async_remote_copy(
            src_ref=self.send_ref.at[src_slot], dst_ref=self.recv_ref.at[dst_slot],
            send_sem=self.send_sems.at[k], recv_sem=self.recv_sems.at[k],
            device_id=self.peers[k][1], device_id_type=MESH)

    def start(self):
        self.mine().start()
        for k, (peer, _) in enumerate(self.peers):
            self.copy(k, peer, self.me).start()

    def finish(self):
        for k, (peer, _) in enumerate(self.peers):
            self.copy(k, peer, self.me).wait_send()
            self.copy(k, self.me, peer).wait_recv()
        self.mine().wait()


def _all_gather(x_shard, in_vmem, with_sum, name):
    m_per, n = x_shard.shape

    def body(x_ref, out_ref, *rest):
        if with_sum:
            sum_ref, send_sems, recv_sems, local_sem = rest
        else:
            send_sems, recv_sems, local_sem = rest
        g = _Gather(x_ref, out_ref, send_sems, recv_sems, local_sem)
        g.start()
        g.forward()
        g.finish()
        if with_sum:
            acc = out_ref[pl.ds(0, m_per), :]
            for d in range(1, N_DEV):
                acc = acc + out_ref[pl.ds(d * m_per, m_per), :]
            sum_ref[...] = acc

    space = pltpu.VMEM if in_vmem else pl.ANY
    out_shape = [jax.ShapeDtypeStruct((N_DEV * m_per, n), x_shard.dtype)]
    out_specs = [pl.BlockSpec(memory_space=space)]
    if with_sum:
        out_shape.append(jax.ShapeDtypeStruct((m_per, n), x_shard.dtype))
        out_specs.append(pl.BlockSpec(memory_space=pltpu.VMEM))
    res = pl.pallas_call(
        body, name=name, out_shape=out_shape, in_specs=[pl.BlockSpec(memory_space=space)], out_specs=out_specs,
        scratch_shapes=COMM_SEMS, compiler_params=pltpu.CompilerParams(vmem_limit_bytes=VMEM_LIMIT),
    )(x_shard)
    return res if with_sum else res[0]


def _comm_sems(n):
    return [pltpu.SemaphoreType.DMA((n, 7)), pltpu.SemaphoreType.DMA((n, 7)), pltpu.SemaphoreType.DMA((n,))]


class _Many:
    def __init__(self, kind, ins, outs, send_sems, recv_sems, local_sems):
        self.parts = [kind(i, o, send_sems.at[b], recv_sems.at[b], local_sems.at[b])
                      for b, (i, o) in enumerate(zip(ins, outs))]

    def start(self):
        for part in self.parts:
            part.start()

    def forward(self):
        for part in self.parts:
            part.forward()

    def finish(self):
        for part in self.parts:
            part.finish()


def _gathered_shapes(shards):
    return [jax.ShapeDtypeStruct((N_DEV * s.shape[0],) + s.shape[1:], s.dtype) for s in shards]


def _all_gather_many(shards, name):
    n = len(shards)

    def body(*refs):
        g = _Many(_Gather, refs[:n], refs[n:2 * n], *refs[2 * n:])
        g.start()
        g.forward()
        g.finish()

    hbm = pl.BlockSpec(memory_space=pl.ANY)
    return pl.pallas_call(body, name=name, out_shape=_gathered_shapes(shards), in_specs=[hbm] * n,
                          out_specs=[hbm] * n, scratch_shapes=_comm_sems(n))(*shards)


def _all_to_all_many(sends, name):
    n = len(sends)

    def body(*refs):
        ex = _Many(_Exchange, refs[:n], refs[n:2 * n], *refs[2 * n:])
        ex.start()
        ex.finish()

    hbm = pl.BlockSpec(memory_space=pl.ANY)
    return pl.pallas_call(body, name=name, out_shape=[jax.ShapeDtypeStruct(s.shape, s.dtype) for s in sends],
                          in_specs=[hbm] * n, out_specs=[hbm] * n, scratch_shapes=_comm_sems(n))(*sends)


def _row(v):
    return v.reshape(1, -1)


def _ffn_fwd(x, w_in, w_out, g_pre, g_post, m, res_w, tag):
    shift, scale, gate = m[0], m[1], m[2]
    mpre = _row(g_pre * (1.0 + scale))
    mpost = _row(res_w * gate * g_post)
    h = _rms_fwd(x, mpre, _row(shift), None, BF16, tag + "_pre")
    ua, ub, s = _ffn_in_swiglu(h, w_in, tag + "_in")
    y = _matmul(s, w_out, name=tag + "_out")
    x_new = _rms_fwd(y, mpost, jnp.zeros_like(mpost), x, F32, tag + "_post")
    return x_new, (x, h, ua, ub, s, y, mpre, mpost)


def _sub_bwd_post(dx_new, y, mpost, g_post, gate, res_w, tag):
    dy, c1, _ = _rms_bwd(dx_new, y, mpost, None, BF16, tag + "_post_bwd")
    c1 = c1[0]
    return dy, c1 * res_w * g_post, c1 * res_w * gate


def _sub_bwd_pre(dh, x, mpre, dx_new, g_pre, scale, tag):
    dx, c2, c3 = _rms_bwd(dh, x, mpre, dx_new, F32, tag + "_pre_bwd")
    c2, c3 = c2[0], c3[0]
    return dx, c3, c2 * g_pre, c2 * (1.0 + scale)


def _ffn_bwd(dx_new, saved, w_in, w_out, g_pre, g_post, m, res_w, tag):
    x, h, ua, ub, s, y, mpre, mpost = saved
    scale, gate = m[1], m[2]
    dy, dgate, dg_post = _sub_bwd_post(dx_new, y, mpost, g_post, gate, res_w, tag)
    ds = _matmul(dy, w_out, tb=True, out_dtype=BF16, name=tag + "_out_dx")
    dw_out = _matmul(s, dy, ta=True, out_dtype=BF16, name=tag + "_out_dw")
    du = _swiglu_bwd(ua, ub, ds, tag + "_act_bwd")
    dh = _matmul(du, w_in, tb=True, name=tag + "_in_dx")
    dw_in = _matmul(h, du, ta=True, out_dtype=BF16, name=tag + "_in_dw")
    dx, dshift, dscale, dg_pre = _sub_bwd_pre(dh, x, mpre, dx_new, g_pre, scale, tag)
    return dx, dw_in, dw_out, jnp.stack([dshift, dscale, dgate]), dg_pre, dg_post


def _slope_cols(gi):
    _, r = C_GROUPS[gi]
    sl = jnp.asarray(_alibi_slopes(C_HEADS)[gi * C_HPG:(gi + 1) * C_HPG], F32) * float(r)
    return jnp.repeat(sl, HDIM).reshape(1, C_OUT)


def _mix_fwd(x, w, g_pre, g_post, m, lb, hn, tag, gather=None):
    t, d = x.shape
    shift, scale, gate = m[0], m[1], m[2]
    mpre = _row(g_pre * (1.0 + scale))
    mpost = _row(gate * g_post)
    h = _rms_fwd(x, mpre, _row(shift), None, BF16, tag + "_pre")
    p = _matmul(h, w["w_in"], name=tag + "_in")
    hn2 = _row(jnp.tile(hn, 2))
    ya, oa, states = _hgrn_fwd(p, _row(lb), hn2, tag + "_hgrn")
    kv = p[:, COL_BK:COL_CQ].astype(BF16)
    if gather is None:
        (yb, sb_tot), gathered = _sb_fwd(p, kv, tag + "_sb"), None
    else:
        res = _sb_fwd(p, kv, tag + "_sb_gather", gather)
        yb, sb_tot, gathered = res[0], res[1], list(res[2:])
    og, lg = zip(*[_dil_fwd(p, gi, tag + "_dil%d" % gi) for gi in range(len(C_GROUPS))])
    yc, lse_c = _dil_merge(og, lg, tag + "_dil_merge")
    gl = p[:, COL_GATE:]
    ws = (w["w_branch_a"], w["w_branch_b"], w["w_branch_c"])
    merged = _gate_fwd((ya, yb, yc), gl, ws, tag + "_gate")
    y = _matmul(merged, w["w_out"], name=tag + "_out")
    x_new = _rms_fwd(y, mpost, jnp.zeros_like(mpost), x, F32, tag + "_post")
    return x_new, (x, h, p, hn2, ya, oa, states, yb, kv, sb_tot, yc, lse_c, gl, merged, y, mpre, mpost), gathered


def _mix_bwd(dx_new, saved, w, g_pre, g_post, m, lb, tag, exchange=None):
    x, h, p, hn2, ya, oa, states, yb, kv, sb_tot, yc, lse_c, gl, merged, y, mpre, mpost = saved
    t = x.shape[0]
    scale, gate = m[1], m[2]
    dy, dgate, dg_post = _sub_bwd_post(dx_new, y, mpost, g_post, gate, 1.0, tag)
    dmerged = _matmul(dy, w["w_out"], tb=True, out_dtype=BF16, name=tag + "_out_dx")
    dw_out = _matmul(merged, dy, ta=True, out_dtype=BF16, name=tag + "_out_dw")
    ws = (w["w_branch_a"], w["w_branch_b"], w["w_branch_c"])
    dya, dyb, dyc, dgl, dwa, dwb, dwc = _gate_bwd(dmerged, (ya, yb, yc), gl, ws, tag + "_gate_bwd")
    dqa, dfa, dia, dga, dlb, dhn = _hgrn_bwd(p, _row(lb), hn2, oa, states, dya, tag + "_hgrn_bwd")
    if exchange is None:
        (dbq, dbk, dbv), received = _sb_bwd(p, kv, sb_tot, dyb, tag + "_sb_bwd"), None
    else:
        res = _sb_bwd(p, kv, sb_tot, dyb, tag + "_sb_bwd_exchange", exchange)
        dbq, dbk, dbv, received = res[0], res[1], res[2], list(res[3:])
    dcq, dck, dcv = zip(*[_dil_bwd(p, dyc, yc, lse_c, gi, tag + "_dil%d_bwd" % gi) for gi in range(len(C_GROUPS))])
    dil = [g.astype(BF16) for g in (*dcq, *dck, *dcv)]
    dp = jnp.concatenate([dqa, dfa, dia, dga, dbq, dbk, dbv, *dil, dgl], axis=1)
    dh = _matmul(dp, w["w_in"], tb=True, name=tag + "_in_dx")
    dw_in = _matmul(h, dp, ta=True, out_dtype=BF16, name=tag + "_in_dw")
    dx, dshift, dscale, dg_pre = _sub_bwd_pre(dh, x, mpre, dx_new, g_pre, scale, tag)
    dhn_v = jnp.sum(dhn, axis=(0, 1))
    dhn_v = dhn_v[:A_VDIM] + dhn_v[A_VDIM:]
    dws = dict(w_in=dw_in, w_out=dw_out, w_branch_a=dwa.astype(BF16), w_branch_b=dwb.astype(BF16),
               w_branch_c=dwc.astype(BF16))
    return dx, dws, jnp.stack([dshift, dscale, dgate]), dg_pre, dg_post, dlb[0], dhn_v, received


class _LocalWeights:
    def __init__(self, wts):
        self.wts = wts

    def first(self):
        return None

    def shard(self, l):
        return None

    def layer(self, l, gathered):
        return {k: v[l] for k, v in self.wts.items()}

    fused = False

    def pack(self, names, dws):
        return [dws[k] for k in names]

    def last(self, packed):
        return packed


class _ShardedWeights:
    def __init__(self, shards):
        self.shards = shards

    def shard(self, l):
        return [self.shards[k][l].astype(BF16) for k in BIG_WEIGHTS]

    def first(self):
        return _all_gather_many(self.shard(0), "weights_all_gather")

    def layer(self, l, gathered):
        out = {}
        for k, got in zip(BIG_WEIGHTS, gathered):
            _, r, c = self.shards[k].shape
            out[k] = got if k in ROW_SHARDED else got.reshape(N_DEV, r, c).transpose(1, 0, 2).reshape(r, N_DEV * c)
        return out

    fused = True

    def pack(self, names, dws):
        out = []
        for k in names:
            _, r, c = self.shards[k].shape
            g = dws[k]
            out.append(g.reshape(N_DEV, r, c) if k in ROW_SHARDED else g.reshape(r, N_DEV, c).transpose(1, 0, 2))
        return out

    def last(self, packed):
        return _all_to_all_many(packed, "grads_all_to_all")


def _local_step(x, target, mod, norm_g, lb_all, hnorm, supply):
    depth = mod.shape[0]
    d = x.shape[1]
    saved, wls = [], []
    gathered = supply.first()
    for l in range(depth):
        wl = supply.layer(l, gathered)
        wls.append(wl)
        x, s0 = _ffn_fwd(x, wl["ffn1_w_in"], wl["ffn1_w_out"], norm_g[l, 0], norm_g[l, 1], mod[l, 0], 0.5, "ffn1")
        nxt = supply.shard(l + 1) if l + 1 < depth else None
        x, s1, gathered = _mix_fwd(x, wl, norm_g[l, 2], norm_g[l, 3], mod[l, 1], lb_all[l], hnorm[l], "mix", nxt)
        x, s2 = _ffn_fwd(x, wl["ffn2_w_in"], wl["ffn2_w_out"], norm_g[l, 4], norm_g[l, 5], mod[l, 2], 0.5, "ffn2")
        saved.append((s0, s1, s2))
    dx, sq = _loss_head(x, target, "loss_head")
    loss = 0.5 * jnp.sum(sq) / d
    dmod, dng, dlb, dhn = [], [], [], []
    early = ("ffn2_w_in", "ffn2_w_out")
    late = tuple(k for k in BIG_WEIGHTS if k not in early)
    returned = {}
    waiting = []
    for l in reversed(range(depth)):
        wl = wls[l]
        s0, s1, s2 = saved[l]
        dx, dwi2, dwo2, dm2, dgp2, dgq2 = _ffn_bwd(dx, s2, wl["ffn2_w_in"], wl["ffn2_w_out"], norm_g[l, 4],
                                                   norm_g[l, 5], mod[l, 2], 0.5, "ffn2")
        waiting += zip([(l, k) for k in early], supply.pack(early, dict(ffn2_w_in=dwi2, ffn2_w_out=dwo2)))
        keys, bufs = [k for k, _ in waiting], [b for _, b in waiting]
        dx, dwm, dm1, dgp1, dgq1, dlb_l, dhn_l, received = _mix_bwd(
            dx, s1, wl, norm_g[l, 2], norm_g[l, 3], mod[l, 1], lb_all[l], "mix", bufs if supply.fused else None)
        returned.update(zip(keys, received if supply.fused else bufs))
        dx, dwi1, dwo1, dm0, dgp0, dgq0 = _ffn_bwd(dx, s0, wl["ffn1_w_in"], wl["ffn1_w_out"], norm_g[l, 0],
                                                   norm_g[l, 1], mod[l, 0], 0.5, "ffn1")
        dmod.append(jnp.stack([dm0, dm1, dm2]))
        dng.append(jnp.stack([dgp0, dgq0, dgp1, dgq1, dgp2, dgq2]))
        dlb.append(dlb_l)
        dhn.append(dhn_l)
        waiting = list(zip([(l, k) for k in late], supply.pack(late, dict(dwm, ffn1_w_in=dwi1, ffn1_w_out=dwo1))))
    returned.update(zip([k for k, _ in waiting], supply.last([b for _, b in waiting])))
    rev = lambda lst: jnp.stack(lst[::-1])
    return loss, dx, rev(dmod), rev(dng), rev(dlb), rev(dhn), returned


def _lb_all(logits):
    lb_p = jax.nn.softmax(logits.astype(F32), axis=0)
    return jnp.cumsum(lb_p, axis=0) - lb_p[0:1]


def _pad_rows(a, rows):
    return jnp.pad(a, ((0, rows - a.shape[0]), (0, 0)))


def kernel(x, c, w_ada, b_ada, norm_g, ffn1_w_in, ffn1_w_out, w_in, hgrn_lb_logits, hgrn_norm_g, w_branch_a, w_branch_b, w_branch_c, w_out, ffn2_w_in, ffn2_w_out, loss_target, m_w_ada, m_b_ada, m_norm_g, m_ffn1_w_in, m_ffn1_w_out, m_w_in, m_hgrn_lb_logits, m_hgrn_norm_g, m_w_branch_a, m_w_branch_b, m_w_branch_c, m_w_out, m_ffn2_w_in, m_ffn2_w_out, v_w_ada, v_b_ada, v_norm_g, v_ffn1_w_in, v_ffn1_w_out, v_w_in, v_hgrn_lb_logits, v_hgrn_norm_g, v_w_branch_a, v_w_branch_b, v_w_branch_c, v_w_out, v_ffn2_w_in, v_ffn2_w_out):
    weights = dict(w_ada=w_ada, b_ada=b_ada, norm_g=norm_g, ffn1_w_in=ffn1_w_in, ffn1_w_out=ffn1_w_out, w_in=w_in,
                   hgrn_lb_logits=hgrn_lb_logits, hgrn_norm_g=hgrn_norm_g, w_branch_a=w_branch_a,
                   w_branch_b=w_branch_b, w_branch_c=w_branch_c, w_out=w_out, ffn2_w_in=ffn2_w_in,
                   ffn2_w_out=ffn2_w_out)
    mom1 = dict(w_ada=m_w_ada, b_ada=m_b_ada, norm_g=m_norm_g, ffn1_w_in=m_ffn1_w_in, ffn1_w_out=m_ffn1_w_out,
                w_in=m_w_in, hgrn_lb_logits=m_hgrn_lb_logits, hgrn_norm_g=m_hgrn_norm_g, w_branch_a=m_w_branch_a,
                w_branch_b=m_w_branch_b, w_branch_c=m_w_branch_c, w_out=m_w_out, ffn2_w_in=m_ffn2_w_in,
                ffn2_w_out=m_ffn2_w_out)
    mom2 = dict(w_ada=v_w_ada, b_ada=v_b_ada, norm_g=v_norm_g, ffn1_w_in=v_ffn1_w_in, ffn1_w_out=v_ffn1_w_out,
                w_in=v_w_in, hgrn_lb_logits=v_hgrn_lb_logits, hgrn_norm_g=v_hgrn_norm_g, w_branch_a=v_w_branch_a,
                w_branch_b=v_w_branch_b, w_branch_c=v_w_branch_c, w_out=v_w_out, ffn2_w_in=v_ffn2_w_in,
                ffn2_w_out=v_ffn2_w_out)
    order = list(weights)
    depth, d, ada_cols = w_ada.shape
    nd = d // LANES
    xi, yi, ci = _my_coords()
    me = 4 * xi + 2 * yi + ci

    small = jnp.concatenate([c.reshape(nd, LANES), norm_g.reshape(depth * 6, LANES)], axis=0)
    g1 = _all_gather(small, True, False, "small_all_gather").reshape(N_DEV, small.shape[0], LANES)
    c_act = _silu(g1[:, :nd].reshape(N_DEV, d))
    norm_full = g1[:, nd:].reshape(N_DEV, depth, 6, LANES).transpose(1, 2, 0, 3).reshape(depth, 6, d)

    c_pad = _pad_rows(c_act, 16)
    mod_sh = jnp.stack([_matmul(c_pad, w_ada[l], name="ada_mod")[:N_DEV]
                        + lax.dynamic_slice_in_dim(b_ada[l], me * ada_cols, ada_cols)[None]
                        for l in range(depth)])
    g2 = _all_gather(mod_sh.reshape(-1, LANES), True, False, "mod_all_gather")
    g2 = g2.reshape(N_DEV, depth, N_DEV, ada_cols)
    mod = lax.dynamic_index_in_dim(g2, me, axis=2, keepdims=False)
    mod = mod.transpose(1, 0, 2).reshape(depth, 3, 3, d)

    supply = _ShardedWeights({k: weights[k] for k in BIG_WEIGHTS})
    lb_all, lb_vjp = jax.vjp(_lb_all, hgrn_lb_logits)

    loss, dx, dmod, dng, dlb, dhn, received = _local_step(x[0], loss_target[0], mod, norm_full, lb_all,
                                                          hgrn_norm_g, supply)
    loss = lax.psum(loss, ("x", "y", "c"))

    dhn_pad = jnp.pad(dhn.reshape(-1), (0, 8 * LANES - dhn.size))
    pieces = [dmod.reshape(-1), dng.reshape(-1), dlb.reshape(-1), dhn_pad]
    sizes = [p_.size for p_ in pieces]
    smallg = jnp.concatenate(pieces).reshape(-1, LANES)
    g3, gsum = _all_gather(smallg, True, True, "small_grads_all_gather")
    g3 = g3.reshape(N_DEV, -1)
    gsum = gsum.reshape(-1)
    dmod_all = g3[:, :sizes[0]].reshape(N_DEV, depth, 9 * d)
    o1 = sizes[0]
    grads = {}
    grads["b_ada"] = gsum[:o1].reshape(depth, 9 * d)
    dng_sum = gsum[o1:o1 + sizes[1]].reshape(depth, 6, nd, LANES)
    grads["norm_g"] = lax.dynamic_index_in_dim(dng_sum, me, axis=2, keepdims=False)
    o2 = o1 + sizes[1]
    dlb_sum = gsum[o2:o2 + sizes[2]].reshape(depth, A_QK)
    grads["hgrn_lb_logits"] = lb_vjp(dlb_sum)[0]
    o3 = o2 + sizes[2]
    grads["hgrn_norm_g"] = gsum[o3:o3 + dhn.size].reshape(depth, A_VDIM)
    dmod_mine = lax.dynamic_slice_in_dim(dmod_all, me * ada_cols, ada_cols, axis=2)
    grads["w_ada"] = jnp.stack([_matmul(c_pad, _pad_rows(dmod_mine[:, l], 16), ta=True, name="ada_dw")
                                for l in range(depth)])

    outs = {}
    for k in order:
        w = weights[k]
        w2 = w.reshape(-1, w.shape[-1])
        if k in BIG_WEIGHTS:
            gp = [received[l, k] for l in range(depth)]
        else:
            gp = [grads[k].reshape((1,) + w2.shape)]
        res = _adamw(w2, mom1[k].reshape(w2.shape), mom2[k].reshape(w2.shape), gp, "adamw")
        outs[k] = [r.reshape(w.shape) for r in res]
    return (loss, dx[None], *[outs[k][0] for k in order], *[outs[k][1] for k in order],
            *[outs[k][2] for k in order], *[outs[k][3] for k in order])
```

```python
import functools
import math

import jax
import jax.numpy as jnp
from jax import lax
from jax.experimental import pallas as pl
from jax.experimental.pallas import tpu as pltpu

F32 = jnp.float32
BF16 = jnp.bfloat16

A_HEADS, A_KDIM, A_VDIM, A_CHUNK = 6, 128, 64, 64
B_HEADS, HDIM = 6, 64
C_GROUPS = ((128, 1), (512, 4), (2048, 16))
C_HPG = 4
C_HEADS = C_HPG * len(C_GROUPS)
N_BRANCH = 3
EPS = 1e-6
NEG_BIG = -1e30
TINY = 1e-30
A_QK = A_HEADS * A_KDIM
A_V = A_HEADS * A_VDIM
B_W = B_HEADS * HDIM
C_W = C_HEADS * HDIM
C_OUT = C_HPG * HDIM
COL_AQ, COL_AF, COL_AI, COL_AG = 0, A_QK, 2 * A_QK, 2 * A_QK + A_V
COL_BQ = 2 * A_QK + 2 * A_V
COL_BK, COL_BV = COL_BQ + B_W, COL_BQ + 2 * B_W
COL_CQ = COL_BQ + 3 * B_W
COL_CK, COL_CV = COL_CQ + C_W, COL_CQ + 2 * C_W
COL_GATE = COL_CQ + 3 * C_W

ADAM_LR, ADAM_B1, ADAM_B2, ADAM_EPS, ADAM_WD, ADAM_STEP = 0.001, 0.9, 0.999, 1e-08, 0.01, 10

N_DEV = 8
LANES = 128
VMEM_LIMIT = 48 * 1024 * 1024
MATMUL_VMEM_BUDGET = 28 * 1024 * 1024
ELEMENTWISE_ROWS = 512
SUB = 16
EXP_CLAMP = 80.0
MESH = pl.DeviceIdType.MESH

BIG_WEIGHTS = ("ffn1_w_in", "ffn1_w_out", "w_in", "w_branch_a", "w_branch_b", "w_branch_c", "w_out",
               "ffn2_w_in", "ffn2_w_out")
ROW_SHARDED = ("ffn1_w_out", "w_out", "ffn2_w_out")


def _cparams(sem):
    return pltpu.CompilerParams(dimension_semantics=sem, vmem_limit_bytes=VMEM_LIMIT)


def _tile(n, cap):
    best, t = None, LANES
    while t <= min(n, cap):
        if n % t == 0:
            best = t
        t += LANES
    return best or n


def _rows(t, cap=256):
    r = cap
    while t % r:
        r //= 2
    return r


def _divisors(n):
    return [t for t in range(LANES, n + 1, LANES) if n % t == 0] or [n]


def _matmul_tiles(m, n, k, a_size, b_size, o_size):
    best, best_key = None, None
    for tm in _divisors(m):
        for tn in _divisors(n):
            for tk in _divisors(k):
                if tm > 1024 or tn > 3072 or tk > 4096:
                    continue
                cast = (tm * tk * 2 if a_size > 2 else 0) + (tk * tn * 2 if b_size > 2 else 0)
                need = 2 * (tm * tk * a_size + tk * tn * b_size + tm * tn * o_size) + 2 * tm * tn * 4 + cast
                if need > MATMUL_VMEM_BUDGET:
                    continue
                key = (tm * tn * tk, tk)
                if best_key is None or key > best_key:
                    best, best_key = (tm, tn, tk), key
    return best


def _dot(a, b):
    return jnp.dot(a, b, preferred_element_type=F32)


def _dot_nt(a, b):
    return lax.dot_general(a, b, (((1,), (1,)), ((), ())), preferred_element_type=F32)


def _dot_tn(a, b):
    return lax.dot_general(a, b, (((0,), (0,)), ((), ())), preferred_element_type=F32)


def _split3(x):
    h = x.astype(BF16)
    r = x - h.astype(F32)
    m = r.astype(BF16)
    lo = (r - m.astype(F32)).astype(BF16)
    return h, m, lo


def _ones_left(mat01, x):
    h, m, lo = _split3(x)
    return _dot(mat01, h) + _dot(mat01, m) + _dot(mat01, lo)


def _silu(x):
    return x * jax.nn.sigmoid(x)


def _dsilu(x):
    s = jax.nn.sigmoid(x)
    return s * (1.0 + x * (1.0 - s))


def _matmul(a, b, *, ta=False, tb=False, out_dtype=F32, name):
    if ta:
        kdim, m = a.shape
    else:
        m, kdim = a.shape
    n = b.shape[0] if tb else b.shape[1]
    tm, tn, tk = _matmul_tiles(m, n, kdim, a.dtype.itemsize, b.dtype.itemsize, jnp.dtype(out_dtype).itemsize)
    nk = kdim // tk
    ni, nj = m // tm, n // tn
    a_bytes, b_bytes = m * kdim * a.dtype.itemsize, kdim * n * b.dtype.itemsize
    j_outer = nk == 1 and (b_bytes + a_bytes * nj) < (a_bytes + b_bytes * ni)
    dims = (((0 if ta else 1,), (1 if tb else 0,)), ((), ()))

    def body(a_ref, b_ref, o_ref, *scratch):
        p = lax.dot_general(a_ref[...].astype(BF16), b_ref[...].astype(BF16), dims, preferred_element_type=F32)
        if nk == 1:
            o_ref[...] = p.astype(o_ref.dtype)
            return
        acc = scratch[0]
        k = pl.program_id(2)

        @pl.when(k == 0)
        def _():
            acc[...] = p

        @pl.when(k > 0)
        def _():
            acc[...] += p

        @pl.when(k == nk - 1)
        def _():
            o_ref[...] = acc[...].astype(o_ref.dtype)

    def spec(shape, pick):
        if j_outer:
            return pl.BlockSpec(shape, lambda j, i, k: pick(i, j, k))
        return pl.BlockSpec(shape, lambda i, j, k: pick(i, j, k))

    a_spec = spec((tk, tm), lambda i, j, k: (k, i)) if ta else spec((tm, tk), lambda i, j, k: (i, k))
    b_spec = spec((tn, tk), lambda i, j, k: (j, k)) if tb else spec((tk, tn), lambda i, j, k: (k, j))
    return pl.pallas_call(
        body, name=name, grid=(nj, ni, nk) if j_outer else (ni, nj, nk), in_specs=[a_spec, b_spec],
        out_specs=spec((tm, tn), lambda i, j, k: (i, j)),
        out_shape=jax.ShapeDtypeStruct((m, n), out_dtype),
        scratch_shapes=[pltpu.VMEM((tm, tn), F32)] if nk > 1 else [],
        compiler_params=_cparams(("parallel", "parallel", "arbitrary")),
    )(a, b)


def _rms_fwd(z, mcol, acol, res, out_dtype, name):
    t, d = z.shape
    tr = _rows(t, ELEMENTWISE_ROWS)
    has_res = res is not None

    def body(*refs):
        if has_res:
            z_ref, m_ref, a_ref, r_ref, o_ref = refs
        else:
            z_ref, m_ref, a_ref, o_ref = refs
        zf = z_ref[...]
        r = lax.rsqrt(jnp.mean(zf * zf, axis=-1, keepdims=True) + EPS)
        y = zf * r * m_ref[...] + a_ref[...]
        if has_res:
            y = r_ref[...] + y
        o_ref[...] = y.astype(o_ref.dtype)

    row = pl.BlockSpec((tr, d), lambda i: (i, 0))
    col = pl.BlockSpec((1, d), lambda i: (0, 0))
    ins = [z, mcol, acol] + ([res] if has_res else [])
    return pl.pallas_call(
        body, name=name, grid=(t // tr,), in_specs=[row, col, col] + ([row] if has_res else []),
        out_specs=row, out_shape=jax.ShapeDtypeStruct((t, d), out_dtype),
        compiler_params=_cparams(("parallel",)),
    )(*ins)


def _rms_bwd(d_out, z, mcol, dres, out_dtype, name):
    t, d = z.shape
    tr = _rows(t, ELEMENTWISE_ROWS)
    has_res = dres is not None

    def body(*refs):
        if has_res:
            d_ref, z_ref, m_ref, r_ref, o_ref, s1_ref, s2_ref = refs
        else:
            d_ref, z_ref, m_ref, o_ref, s1_ref, s2_ref = refs
        i = pl.program_id(0)
        zf = z_ref[...]
        r = lax.rsqrt(jnp.mean(zf * zf, axis=-1, keepdims=True) + EPS)
        zh = zf * r
        df = d_ref[...].astype(F32)
        dzh = df * m_ref[...]
        dz = r * (dzh - zh * jnp.mean(dzh * zh, axis=-1, keepdims=True))
        if has_res:
            dz = dz + r_ref[...]
        o_ref[...] = dz.astype(o_ref.dtype)
        s1 = jnp.sum(df * zh, axis=0, keepdims=True)
        s2 = jnp.sum(df, axis=0, keepdims=True)

        @pl.when(i == 0)
        def _():
            s1_ref[...] = s1
            s2_ref[...] = s2

        @pl.when(i > 0)
        def _():
            s1_ref[...] += s1
            s2_ref[...] += s2

    row = pl.BlockSpec((tr, d), lambda i: (i, 0))
    col = pl.BlockSpec((1, d), lambda i: (0, 0))
    ins = [d_out, z, mcol] + ([dres] if has_res else [])
    return pl.pallas_call(
        body, name=name, grid=(t // tr,), in_specs=[row, row, col] + ([row] if has_res else []),
        out_specs=[row, col, col],
        out_shape=[jax.ShapeDtypeStruct((t, d), out_dtype), jax.ShapeDtypeStruct((1, d), F32),
                   jax.ShapeDtypeStruct((1, d), F32)],
        compiler_params=_cparams(("arbitrary",)),
    )(*ins)


FFN_IN_TILE = (512, 1408)


def _ffn_in_swiglu(h, w_in, name):
    t, d = h.shape
    f = w_in.shape[1] // 2
    tm, tn = _tile(t, FFN_IN_TILE[0]), _tile(f, FFN_IN_TILE[1])
    nj = f // tn

    def body(h_ref, wa_ref, wb_ref, a_ref, b_ref, s_ref):
        hv = h_ref[...].astype(BF16)
        a = _dot(hv, wa_ref[...].astype(BF16))
        b = _dot(hv, wb_ref[...].astype(BF16))
        a_ref[...] = a.astype(BF16)
        b_ref[...] = b.astype(BF16)
        s_ref[...] = (_silu(a) * b).astype(BF16)

    out = pl.BlockSpec((tm, tn), lambda j, i: (i, j))
    return pl.pallas_call(
        body, name=name, grid=(nj, t // tm),
        in_specs=[pl.BlockSpec((tm, d), lambda j, i: (i, 0)), pl.BlockSpec((d, tn), lambda j, i: (0, j)),
                  pl.BlockSpec((d, tn), lambda j, i: (0, j + nj))],
        out_specs=[out, out, out], out_shape=[jax.ShapeDtypeStruct((t, f), BF16)] * 3,
        compiler_params=_cparams(("parallel", "parallel")),
    )(h, w_in, w_in)


def _swiglu_bwd(ua, ub, ds, name):
    t, f = ua.shape
    tr = _rows(t, ELEMENTWISE_ROWS)

    def body(a_ref, b_ref, ds_ref, du_ref):
        a = a_ref[...].astype(F32)
        b = b_ref[...].astype(F32)
        g = ds_ref[...].astype(F32)
        du_ref[:, :f] = (g * b * _dsilu(a)).astype(du_ref.dtype)
        du_ref[:, f:] = (g * _silu(a)).astype(du_ref.dtype)

    half = pl.BlockSpec((tr, f), lambda i: (i, 0))
    return pl.pallas_call(
        body, name=name, grid=(t // tr,), in_specs=[half, half, half],
        out_specs=pl.BlockSpec((tr, 2 * f), lambda i: (i, 0)), out_shape=jax.ShapeDtypeStruct((t, 2 * f), BF16),
        compiler_params=_cparams(("parallel",)),
    )(ua, ub, ds)


def _loss_head(y, target, name):
    t, d = y.shape
    tr = _rows(t, ELEMENTWISE_ROWS)

    def body(y_ref, t_ref, dy_ref, sq_ref):
        i = pl.program_id(0)
        e = y_ref[...] - t_ref[...]
        dy_ref[...] = e * (1.0 / d)
        s = jnp.sum(e * e, axis=0, keepdims=True)

        @pl.when(i == 0)
        def _():
            sq_ref[...] = s

        @pl.when(i > 0)
        def _():
            sq_ref[...] += s

    row = pl.BlockSpec((tr, d), lambda i: (i, 0))
    col = pl.BlockSpec((1, d), lambda i: (0, 0))
    return pl.pallas_call(
        body, name=name, grid=(t // tr,), in_specs=[row, row], out_specs=[row, col],
        out_shape=[jax.ShapeDtypeStruct((t, d), F32), jax.ShapeDtypeStruct((1, d), F32)],
        compiler_params=_cparams(("arbitrary",)),
    )(y, target)


def _hgrn_consts():
    c = A_CHUNK
    shift = SUB.bit_length() - 1
    r = lax.broadcasted_iota(jnp.int32, (c, c), 0)
    s = lax.broadcasted_iota(jnp.int32, (c, c), 1)
    sub_r = lax.shift_right_logical(r, shift)
    incl = s <= r
    masks = [jnp.logical_and(sub_r == i, incl) for i in range(c // SUB)]
    rev_incl = jnp.where(s >= r, 1.0, 0.0).astype(BF16)
    r2 = lax.broadcasted_iota(jnp.int32, (2 * c + 8, c), 0)
    s2 = lax.broadcasted_iota(jnp.int32, (2 * c + 8, c), 1)
    sub_start = lax.shift_left(lax.shift_right_logical(r2 - c, shift), shift)
    running = jnp.where(s2 <= r2, 1.0, 0.0)
    before = jnp.where(s2 < sub_start, 1.0, 0.0)
    stack = jnp.where(r2 < c, running, jnp.where(r2 < 2 * c, before, 1.0)).astype(BF16)
    return stack, masks, incl, rev_incl


def _hgrn_gates(q_raw, f_raw, lbv, stack):
    sg = jax.nn.sigmoid(f_raw)
    sgn = jax.nn.sigmoid(-f_raw)
    f = lbv + (1.0 - lbv) * sg
    logf = jnp.log(jnp.maximum(f, TINY))
    return dict(sg=sg, sgn=sgn, f=f, k=(1.0 - lbv) * sgn, q=_silu(q_raw), bb=_ones_left(stack, logf))


def _hgrn_chunk(q_raw, f_raw, lbv, stack):
    return _hgrn_decays(_hgrn_gates(q_raw, f_raw, lbv, stack))


def _hgrn_decays(gates):
    c = A_CHUNK
    sg, sgn, f, k, q, bb = (gates[n] for n in ("sg", "sgn", "f", "k", "q", "bb"))
    b = bb[:c]
    bsrow = bb[c:2 * c]
    b_end = bb[2 * c:2 * c + 1]
    e_sub = jnp.exp(b - bsrow)
    e_b = jnp.exp(b)
    e_end = jnp.exp(b_end - b)
    qs = q * e_sub
    q_in = q * e_b
    kend = k * e_end
    kfac = [jnp.exp(jnp.minimum(bsrow[i * SUB:i * SUB + 1] - b, EXP_CLAMP)) for i in range(c // SUB)]
    return dict(sg=sg, sgn=sgn, f=f, k=k, q=q, b=b, b_end=b_end, e_sub=e_sub, e_b=e_b, e_end=e_end,
                qs=qs, q_in=q_in, kend=kend, kfac=kfac)


def _hgrn_scores(ch, masks):
    qs_b = ch["qs"].astype(BF16)
    a = None
    for i, mk in enumerate(masks):
        ki = (ch["k"] * ch["kfac"][i]).astype(BF16)
        part = jnp.where(mk, _dot_nt(qs_b, ki), 0.0)
        a = part if a is None else a + part
    return a


def _hgrn_fwd(p, lb, hn2, name):
    t = p.shape[0]
    tb = _rows(t)
    nt = t // tb
    nc = tb // A_CHUNK
    c = A_CHUNK

    def body(q_ref, f_ref, i_ref, g_ref, lb_ref, hn_ref, y_ref, o_ref, st_ref, s_scr):
        j = pl.program_id(1)

        @pl.when(j == 0)
        def _():
            s_scr[...] = jnp.zeros_like(s_scr)

        stack, masks, _, _ = _hgrn_consts()
        units = [(ci, hh) for ci in range(nc) for hh in range(2)]
        lsl = [slice(A_KDIM * hh, A_KDIM * (hh + 1)) for hh in range(2)]
        hsl = [slice(A_VDIM * hh, A_VDIM * (hh + 1)) for hh in range(2)]
        rows = [pl.ds(ci * c, c) for ci in range(nc)]
        gates = {u: _hgrn_gates(q_ref[rows[u[0]], lsl[u[1]]], f_ref[rows[u[0]], lsl[u[1]]], lb_ref[:, lsl[u[1]]], stack)
                 for u in units}
        ch = {u: _hgrn_decays(gates[u]) for u in units}
        v = {u: i_ref[rows[u[0]], hsl[u[1]]].astype(BF16) for u in units}
        a = {u: _hgrn_scores(ch[u], masks).astype(BF16) for u in units}
        grow = {u: _dot_tn(v[u], ch[u]["kend"].astype(BF16)) for u in units}
        states = [s_scr[0], s_scr[1]]
        entering = {}
        for ci, hh in units:
            entering[ci, hh] = states[hh]
            st_ref[hh, ci] = states[hh]
            states[hh] = states[hh] * jnp.exp(ch[ci, hh]["b_end"]) + grow[ci, hh]
        s_scr[0] = states[0]
        s_scr[1] = states[1]
        for u in units:
            o_ref[rows[u[0]], hsl[u[1]]] = (_dot_nt(ch[u]["q_in"].astype(BF16), entering[u].astype(BF16))
                                            + _dot(a[u], v[u]))
        for hh in range(2):
            hsl = slice(A_VDIM * hh, A_VDIM * (hh + 1))
            o = o_ref[:, hsl]
            r = lax.rsqrt(jnp.mean(o * o, axis=-1, keepdims=True) + EPS)
            y_ref[:, hsl] = (o * r * hn_ref[:, hsl] * _silu(g_ref[:, hsl])).astype(y_ref.dtype)

    w2 = 2 * A_KDIM
    return pl.pallas_call(
        body, name=name, grid=(A_HEADS // 2, nt),
        in_specs=[pl.BlockSpec((tb, w2), lambda h, j: (j, COL_AQ // w2 + h)),
                  pl.BlockSpec((tb, w2), lambda h, j: (j, COL_AF // w2 + h)),
                  pl.BlockSpec((tb, LANES), lambda h, j: (j, COL_AI // LANES + h)),
                  pl.BlockSpec((tb, LANES), lambda h, j: (j, COL_AG // LANES + h)),
                  pl.BlockSpec((1, w2), lambda h, j: (0, h)),
                  pl.BlockSpec((1, LANES), lambda h, j: (0, 0))],
        out_specs=[pl.BlockSpec((tb, LANES), lambda h, j: (j, h)),
                   pl.BlockSpec((tb, LANES), lambda h, j: (j, h)),
                   pl.BlockSpec((2, nc, A_VDIM, A_KDIM), lambda h, j: (h, j, 0, 0))],
        out_shape=[jax.ShapeDtypeStruct((t, A_V), BF16), jax.ShapeDtypeStruct((t, A_V), F32),
                   jax.ShapeDtypeStruct((A_HEADS, t // c, A_VDIM, A_KDIM), F32)],
        scratch_shapes=[pltpu.VMEM((2, A_VDIM, A_KDIM), F32)],
        compiler_params=_cparams(("parallel", "arbitrary")),
    )(p, p, p, p, lb, hn2)


def _hgrn_bwd(p, lb, hn2, o_raw, states, dya, name):
    t = p.shape[0]
    tb = _rows(t)
    nt = t // tb
    nc = tb // A_CHUNK
    c = A_CHUNK

    def body(q_ref, f_ref, i_ref, g_ref, lb_ref, hn_ref, o_ref, st_ref, dy_ref,
             dq_ref, df_ref, di_ref, dg_ref, dlb_ref, dhn_ref, ds_scr, do_scr):
        j = pl.program_id(1)

        @pl.when(j == 0)
        def _():
            ds_scr[...] = jnp.zeros_like(ds_scr)
            dlb_ref[...] = jnp.zeros_like(dlb_ref)
            dhn_ref[...] = jnp.zeros_like(dhn_ref)

        stack, masks, incl, rev_incl = _hgrn_consts()
        for hh in range(2):
            hsl = slice(A_VDIM * hh, A_VDIM * (hh + 1))
            o = o_ref[:, hsl]
            g = g_ref[:, hsl]
            dy = dy_ref[:, hsl].astype(F32)
            hn = hn_ref[:, hsl]
            r = lax.rsqrt(jnp.mean(o * o, axis=-1, keepdims=True) + EPS)
            oh = o * r
            sgate = _silu(g)
            dg_ref[:, hsl] = (dy * oh * hn * _dsilu(g)).astype(dg_ref.dtype)
            dhn_ref[0, :, hsl] += jnp.sum(dy * oh * sgate, axis=0, keepdims=True)
            doh = dy * hn * sgate
            do_scr[:, hsl] = r * (doh - oh * jnp.mean(doh * oh, axis=-1, keepdims=True))

        units = [(ci, hh) for ci in reversed(range(nc)) for hh in range(2)]
        lsl = [slice(A_KDIM * hh, A_KDIM * (hh + 1)) for hh in range(2)]
        hsl = [slice(A_VDIM * hh, A_VDIM * (hh + 1)) for hh in range(2)]
        rows = [pl.ds(ci * c, c) for ci in range(nc)]
        q_raw = {u: q_ref[rows[u[0]], lsl[u[1]]] for u in units}
        gates = {u: _hgrn_gates(q_raw[u], f_ref[rows[u[0]], lsl[u[1]]], lb_ref[:, lsl[u[1]]], stack) for u in units}
        ch = {u: _hgrn_decays(gates[u]) for u in units}
        v = {u: i_ref[rows[u[0]], hsl[u[1]]].astype(BF16) for u in units}
        do_b = {u: do_scr[rows[u[0]], hsl[u[1]]].astype(BF16) for u in units}
        st = {u: st_ref[u[1], u[0]] for u in units}
        qs_b = {u: ch[u]["qs"].astype(BF16) for u in units}
        a_b = {u: _hgrn_scores(ch[u], masks).astype(BF16) for u in units}
        da = {u: jnp.where(incl, _dot_nt(do_b[u], v[u]), 0.0) for u in units}
        dq_x = {u: _dot(do_b[u], st[u].astype(BF16)) for u in units}
        grow = {u: _dot_tn(do_b[u], ch[u]["q_in"].astype(BF16)) for u in units}
        dstates = [ds_scr[0], ds_scr[1]]
        leaving = {}
        for ci, hh in units:
            leaving[ci, hh] = dstates[hh]
            dstates[hh] = dstates[hh] * jnp.exp(ch[ci, hh]["b_end"]) + grow[ci, hh]
        for hh in range(2):
            ds_scr[hh] = dstates[hh]
        dst_b = {u: leaving[u].astype(BF16) for u in units}
        dv = {u: _dot_tn(a_b[u], do_b[u]) + _dot_nt(ch[u]["kend"].astype(BF16), dst_b[u]) for u in units}
        dk_x = {u: _dot(v[u], dst_b[u]) for u in units}
        dlb_acc = [jnp.zeros((1, A_KDIM), F32), jnp.zeros((1, A_KDIM), F32)]
        for u in units:
            ci, hh = u
            cu = ch[u]
            lbv = lb_ref[:, lsl[hh]]
            dq_i = None
            dk_i = None
            kdk_i = None
            for i, mk in enumerate(masks):
                dam = jnp.where(mk, da[u], 0.0).astype(BF16)
                ki = (cu["k"] * cu["kfac"][i]).astype(BF16)
                pq = _dot(dam, ki)
                pk = _dot_tn(dam, qs_b[u])
                dq_i = pq if dq_i is None else dq_i + pq
                dk_i = cu["kfac"][i] * pk if dk_i is None else dk_i + cu["kfac"][i] * pk
                kdk_i = ki.astype(F32) * pk if kdk_i is None else kdk_i + ki.astype(F32) * pk
            dq = cu["e_sub"] * dq_i + cu["e_b"] * dq_x[u]
            dk = dk_i + cu["e_end"] * dk_x[u]
            kx = cu["kend"] * dk_x[u]
            db = (qs_b[u].astype(F32) * dq_i + cu["q_in"] * dq_x[u]) - (kdk_i + kx)
            later = (jnp.exp(cu["b_end"]) * jnp.sum(leaving[u] * st[u], axis=0, keepdims=True)
                     + jnp.sum(kx, axis=0, keepdims=True))
            dlogf = later + _ones_left(rev_incl, db)
            dfv = jnp.where(cu["f"] > TINY, dlogf / cu["f"], 0.0)
            dq_ref[rows[ci], lsl[hh]] = (dq * _dsilu(q_raw[u])).astype(dq_ref.dtype)
            df_ref[rows[ci], lsl[hh]] = ((1.0 - lbv) * cu["sg"] * cu["sgn"] * (dfv - dk)).astype(df_ref.dtype)
            dlb_acc[hh] = dlb_acc[hh] + jnp.sum(dfv * (1.0 - cu["sg"]) - dk * cu["sgn"], axis=0, keepdims=True)
            di_ref[rows[ci], hsl[hh]] = dv[u].astype(di_ref.dtype)
        for hh in range(2):
            dlb_ref[:, A_KDIM * hh:A_KDIM * (hh + 1)] += dlb_acc[hh]

    w2 = 2 * A_KDIM
    rev = lambda j: nt - 1 - j
    return pl.pallas_call(
        body, name=name, grid=(A_HEADS // 2, nt),
        in_specs=[pl.BlockSpec((tb, w2), lambda h, j: (rev(j), COL_AQ // w2 + h)),
                  pl.BlockSpec((tb, w2), lambda h, j: (rev(j), COL_AF // w2 + h)),
                  pl.BlockSpec((tb, LANES), lambda h, j: (rev(j), COL_AI // LANES + h)),
                  pl.BlockSpec((tb, LANES), lambda h, j: (rev(j), COL_AG // LANES + h)),
                  pl.BlockSpec((1, w2), lambda h, j: (0, h)),
                  pl.BlockSpec((1, LANES), lambda h, j: (0, 0)),
                  pl.BlockSpec((tb, LANES), lambda h, j: (rev(j), h)),
                  pl.BlockSpec((2, nc, A_VDIM, A_KDIM), lambda h, j: (h, rev(j), 0, 0)),
                  pl.BlockSpec((tb, LANES), lambda h, j: (rev(j), h))],
        out_specs=[pl.BlockSpec((tb, w2), lambda h, j: (rev(j), h)),
                   pl.BlockSpec((tb, w2), lambda h, j: (rev(j), h)),
                   pl.BlockSpec((tb, LANES), lambda h, j: (rev(j), h)),
                   pl.BlockSpec((tb, LANES), lambda h, j: (rev(j), h)),
                   pl.BlockSpec((1, w2), lambda h, j: (0, h)),
                   pl.BlockSpec((1, 1, LANES), lambda h, j: (h, 0, 0))],
        out_shape=[jax.ShapeDtypeStruct((t, A_QK), BF16), jax.ShapeDtypeStruct((t, A_QK), BF16),
                   jax.ShapeDtypeStruct((t, A_V), BF16), jax.ShapeDtypeStruct((t, A_V), BF16),
                   jax.ShapeDtypeStruct((1, A_QK), F32), jax.ShapeDtypeStruct((A_HEADS // 2, 1, LANES), F32)],
        scratch_shapes=[pltpu.VMEM((2, A_VDIM, A_KDIM), F32), pltpu.VMEM((tb, LANES), F32)],
        compiler_params=_cparams(("parallel", "arbitrary")),
    )(p, p, p, p, lb, hn2, o_raw, states, dya)


BLK = 128
SCALE = HDIM ** -0.5
SB_CHUNK = 4


def _softplus(z):
    return jnp.maximum(z, 0.0) + jnp.log(1.0 + jnp.exp(-jnp.abs(z)))


def _sb_sum_matrix(keep, with_total=False):
    width = 2 * BLK if with_total else BLK
    sp = lax.broadcasted_iota(jnp.int32, (BLK, width), 0)
    s = lax.broadcasted_iota(jnp.int32, (BLK, width), 1)
    return jnp.where(jnp.logical_or(s >= BLK, keep(sp, s)), 1.0, 0.0).astype(BF16)


def _lanes(col):
    return jnp.broadcast_to(col, (BLK, BLK))


def _sb_fwd(p, kv, name, gather=None):
    t = p.shape[0]
    nq = t // BLK
    nh = B_HEADS // 2
    cw = SB_CHUNK * BLK
    fused = gather is not None
    n = len(gather) if fused else 0

    def body(*refs):
        q_ref, kb, vb = refs[:3]
        o_ref, tot_ref = refs[3 + n:5 + n]
        zbuf, stage, sbuf, abuf = refs[5 + 2 * n:9 + 2 * n]
        hp = pl.program_id(0)
        qi = pl.program_id(1)
        if fused:
            g = _Many(_Gather, refs[3:3 + n], refs[5 + n:5 + 2 * n], *refs[9 + 2 * n:])
            pl.when(jnp.logical_and(hp == 0, qi == 0))(g.start)
            pl.when(jnp.logical_and(hp == nh - 1, qi == 0))(g.forward)

        @pl.when(qi == 0)
        def _():
            abuf[...] = jnp.zeros_like(abuf)

        row = lax.broadcasted_iota(jnp.int32, (BLK, BLK), 0)
        col = lax.broadcasted_iota(jnp.int32, (BLK, BLK), 1)
        sums = _sb_sum_matrix(lambda sp, s: sp >= s, True)
        hsl = [slice(HDIM * h, HDIM * (h + 1)) for h in range(2)]
        nchunk = qi // SB_CHUNK + 1
        for h in range(2):
            zbuf[h] = _dot_nt((q_ref[:, hsl[h]] * SCALE).astype(BF16), kb[:, hsl[h]])

        col_minus_row = col - row

        def causal(j):
            return col_minus_row < (qi - j) * BLK

        def l_pass(masked):
            def run(c, carry):
                for b in range(SB_CHUNK):
                    j = c * SB_CHUNK + b
                    off = pl.multiple_of(j * BLK, BLK)
                    for h in range(2):
                        lm = -_softplus(zbuf[h, :, pl.ds(off, BLK)])
                        if masked:
                            lm = jnp.where(causal(j), lm, 0.0)
                        stage[h, pl.ds(off, BLK), :] = lm.astype(BF16)
                return carry
            return run

        lax.fori_loop(0, nchunk - 1, l_pass(False), 0)
        l_pass(True)(nchunk - 1, 0)

        def sum_pass(c, carry):
            rows = pl.ds(pl.multiple_of(c * cw, cw), cw)
            for h in range(2):
                sbuf[h, rows, :] = _dot(stage[h, rows, :], sums)
            return carry

        lax.fori_loop(0, nchunk, sum_pass, 0)

        def a_pass(masked):
            def run(it, carry):
                c = nchunk - 1 - it
                runs = list(carry)
                for b in reversed(range(SB_CHUNK)):
                    j = c * SB_CHUNK + b
                    off = pl.multiple_of(j * BLK, BLK)
                    for h in range(2):
                        s = sbuf[h, pl.ds(off, BLK), :BLK]
                        a = jnp.exp(zbuf[h, :, pl.ds(off, BLK)] + s + runs[h])
                        if masked:
                            a = jnp.where(causal(j), a, 0.0)
                        abuf[h, :, pl.ds(off, BLK)] = a.astype(BF16)
                        runs[h] = runs[h] + sbuf[h, pl.ds(off, BLK), BLK:]
                return tuple(runs)
            return run

        zero = jnp.zeros((BLK, BLK), F32)
        runs = lax.fori_loop(1, nchunk, a_pass(False), a_pass(True)(0, (zero, zero)))
        for h in range(2):
            tot_ref[:, hsl[h]] = runs[h][:, :HDIM]
            o_ref[:, hsl[h]] = _dot(abuf[h], vb[:, hsl[h]])
        if fused:
            pl.when(jnp.logical_and(hp == nh - 1, qi == nq - 1))(g.finish)

    out_blk = pl.BlockSpec((BLK, LANES), lambda h, i: (i, h))
    hbm = pl.BlockSpec(memory_space=pl.ANY)
    in_specs = [pl.BlockSpec((BLK, LANES), lambda h, i: (i, COL_BQ // LANES + h)),
                pl.BlockSpec((t, LANES), lambda h, i: (0, h)),
                pl.BlockSpec((t, LANES), lambda h, i: (0, B_W // LANES + h))]
    out_shape = [jax.ShapeDtypeStruct((t, B_W), F32)] * 2
    scratch = [pltpu.VMEM((2, BLK, t), F32), pltpu.VMEM((2, t, BLK), BF16),
               pltpu.VMEM((2, t, 2 * BLK), F32), pltpu.VMEM((2, BLK, t), BF16)]
    if fused:
        out_shape = out_shape + _gathered_shapes(gather)
    return pl.pallas_call(
        body, name=name, grid=(nh, nq),
        in_specs=in_specs + [hbm] * n,
        out_specs=[out_blk, out_blk] + [hbm] * n,
        out_shape=out_shape,
        scratch_shapes=scratch + (_comm_sems(n) if fused else []),
        compiler_params=_cparams(("arbitrary", "arbitrary")),
    )(p, kv, kv, *(gather if fused else []))


def _sb_bwd(p, kv, tot, do, name, exchange=None):
    t = p.shape[0]
    nq = t // BLK
    nh = B_HEADS // 2
    cw = SB_CHUNK * BLK
    fused = exchange is not None
    n = len(exchange) if fused else 0

    def body(*refs):
        q_ref, kb, vb, tot_ref, do_ref = refs[:5]
        dq_ref, dk_ref, dv_ref = refs[5 + n:8 + n]
        zbuf, dabuf, lbuf, stage, sbuf, abuf, dzbuf, dkt, dvt = refs[8 + 2 * n:17 + 2 * n]
        hp = pl.program_id(0)
        qi = pl.program_id(1)
        if fused:
            ex = _Many(_Exchange, refs[5:5 + n], refs[8 + n:8 + 2 * n], *refs[17 + 2 * n:])
            pl.when(jnp.logical_and(hp == 0, qi == 0))(ex.start)

        @pl.when(qi == 0)
        def _():
            dkt[...] = jnp.zeros_like(dkt)
            dvt[...] = jnp.zeros_like(dvt)
            dzbuf[...] = jnp.zeros_like(dzbuf)
            abuf[...] = jnp.zeros_like(abuf)

        row = lax.broadcasted_iota(jnp.int32, (BLK, BLK), 0)
        col = lax.broadcasted_iota(jnp.int32, (BLK, BLK), 1)
        sums = _sb_sum_matrix(lambda sp, s: sp <= s)
        hsl = [slice(HDIM * h, HDIM * (h + 1)) for h in range(2)]
        dob = [do_ref[:, hsl[h]].astype(BF16) for h in range(2)]
        total =[jnp.concatenate([tot_ref[:, hsl[h]], tot_ref[:, hsl[h]]], axis=1) for h in range(2)]
        nchunk = qi // SB_CHUNK + 1
        for h in range(2):
            zbuf[h] = _dot_nt((q_ref[:, hsl[h]] * SCALE).astype(BF16), kb[:, hsl[h]])
            dabuf[h] = _dot_nt(dob[h], vb[:, hsl[h]])

        col_minus_row = col - row

        def causal(j):
            return col_minus_row < (qi - j) * BLK

        def blocks(c):
            for b in range(SB_CHUNK):
                j = c * SB_CHUNK + b
                yield j, pl.ds(pl.multiple_of(j * BLK, BLK), BLK)

        def l_pass(masked, c, carry):
            for j, blk_ in blocks(c):
                for h in range(2):
                    lm = -_softplus(zbuf[h, :, blk_])
                    if masked:
                        lm = jnp.where(causal(j), lm, 0.0)
                    lbuf[h, :, blk_] = lm
                    stage[h, blk_, :] = lm.astype(BF16)
            return carry

        def split(pass_, init):
            carry = lax.fori_loop(0, nchunk - 1, functools.partial(pass_, False), init)
            return pass_(True, nchunk - 1, carry)

        split(l_pass, 0)

        def sum_pass():
            def run_(c, carry):
                rows = pl.ds(pl.multiple_of(c * cw, cw), cw)
                for h in range(2):
                    sbuf[h, rows, :] = _dot(stage[h, rows, :], sums)
                return carry
            lax.fori_loop(0, nchunk, run_, 0)

        sum_pass()

        def g_pass(masked, c, carry):
            runs = list(carry)
            for j, blk_ in blocks(c):
                for h in range(2):
                    upto = sbuf[h, blk_, :]
                    log_a = zbuf[h, :, blk_] + lbuf[h, :, blk_] + (total[h] - runs[h] - upto)
                    a = jnp.exp(log_a)
                    if masked:
                        a = jnp.where(causal(j), a, 0.0)
                    abuf[h, :, blk_] = a.astype(BF16)
                    g = a * dabuf[h, :, blk_]
                    dabuf[h, :, blk_] = g
                    stage[h, blk_, :] = g.astype(BF16)
                    runs[h] = runs[h] + _lanes(upto[:, BLK - 1:BLK])
            return tuple(runs)

        zero = jnp.zeros((BLK, BLK), F32)
        split(g_pass, (zero, zero))
        sum_pass()

        def dz_pass(masked, c, carry):
            runs = list(carry)
            for j, blk_ in blocks(c):
                for h in range(2):
                    lm = lbuf[h, :, blk_]
                    g = dabuf[h, :, blk_]
                    upto = sbuf[h, blk_, :]
                    before = runs[h] + upto - g
                    dz = g * jnp.exp(lm) - jnp.exp(zbuf[h, :, blk_] + lm) * before
                    if masked:
                        dz = jnp.where(causal(j), dz, 0.0)
                    dzbuf[h, :, blk_] = (dz * SCALE).astype(BF16)
                    runs[h] = runs[h] + _lanes(upto[:, BLK - 1:BLK])
            return tuple(runs)

        split(dz_pass, (zero, zero))
        for h in range(2):
            dq_ref[:, hsl[h]] = _dot(dzbuf[h], kb[:, hsl[h]]).astype(dq_ref.dtype)
        q_t = q_ref[...].T.astype(BF16)
        do_t = do_ref[...].T.astype(BF16)
        for h in range(2):
            dkt[hsl[h], :] += _dot(q_t[hsl[h], :], dzbuf[h])
            dvt[hsl[h], :] += _dot(do_t[hsl[h], :], abuf[h])

        @pl.when(qi == nq - 1)
        def _():
            dk_ref[...] = dkt[...].T.astype(dk_ref.dtype)
            dv_ref[...] = dvt[...].T.astype(dv_ref.dtype)

        if fused:
            pl.when(jnp.logical_and(hp == nh - 1, qi == nq - 1))(ex.finish)

    blk = lambda h, i: (i, h)
    whole = lambda h, i: (0, h)
    hbm = pl.BlockSpec(memory_space=pl.ANY)
    in_specs = [pl.BlockSpec((BLK, LANES), lambda h, i: (i, COL_BQ // LANES + h)),
                pl.BlockSpec((t, LANES), lambda h, i: (0, h)),
                pl.BlockSpec((t, LANES), lambda h, i: (0, B_W // LANES + h)),
                pl.BlockSpec((BLK, LANES), blk), pl.BlockSpec((BLK, LANES), blk)]
    out_specs = [pl.BlockSpec((BLK, LANES), blk), pl.BlockSpec((t, LANES), whole), pl.BlockSpec((t, LANES), whole)]
    out_shape = [jax.ShapeDtypeStruct((t, B_W), BF16)] * 3
    scratch = [pltpu.VMEM((2, BLK, t), F32), pltpu.VMEM((2, BLK, t), F32), pltpu.VMEM((2, BLK, t), F32),
               pltpu.VMEM((2, t, BLK), BF16), pltpu.VMEM((2, t, BLK), F32), pltpu.VMEM((2, BLK, t), BF16),
               pltpu.VMEM((2, BLK, t), BF16), pltpu.VMEM((LANES, t), F32), pltpu.VMEM((LANES, t), F32)]
    if fused:
        out_shape = out_shape + [jax.ShapeDtypeStruct(e.shape, e.dtype) for e in exchange]
    return pl.pallas_call(
        body, name=name, grid=(nh, nq),
        in_specs=in_specs + [hbm] * n,
        out_specs=out_specs + [hbm] * n,
        out_shape=out_shape,
        scratch_shapes=scratch + (_comm_sems(n) if fused else []),
        compiler_params=_cparams(("arbitrary", "arbitrary")),
    )(p, kv, kv, tot, do, *(exchange if fused else []))


def _alibi_slopes(n):
    def pow2(m):
        start = 2.0 ** (-8.0 / m)
        return [start ** (i + 1) for i in range(m)]
    if math.log2(n).is_integer():
        s = pow2(n)
    else:
        c = 2 ** int(math.floor(math.log2(n)))
        s = pow2(c) + pow2(2 * c)[0::2][: n - c]
    return sorted(s, reverse=True)


def _dil_scores(qh, kh, sl, prev, exists=None):
    row = lax.broadcasted_iota(jnp.int32, (BLK, BLK), 0)
    col = lax.broadcasted_iota(jnp.int32, (BLK, BLK), 1)
    dist = row - col + (BLK if prev else 0)
    if prev:
        valid = (col - row) >= jnp.where(exists, 0, 2 * BLK)
    else:
        valid = col <= row
    s = _dot_nt(qh, kh) - sl * dist.astype(F32)
    return s, valid


DIL_UNITS = 4


def _dil_plan(r):
    per_trip = min(r, DIL_UNITS)
    return per_trip, DIL_UNITS // per_trip


def _dil_rows(b, rho, r):
    return pl.ds(b * BLK * r + rho, BLK, stride=r) if r > 1 else pl.ds(b * BLK, BLK)


def _dil_fwd(p, gi, name):
    t = p.shape[0]
    _, r = C_GROUPS[gi]
    per_trip, nsub = _dil_plan(r)
    sbr = BLK * r * nsub
    nsb = t // sbr
    slope_cols = _slope_cols(gi)

    def body(q_ref, kc_ref, kp_ref, vc_ref, vp_ref, sl_ref, o_ref, lse_ref):
        i = pl.program_id(1)

        hsl = [slice(HDIM * h, HDIM * (h + 1)) for h in range(2)]
        sl = [sl_ref[:, HDIM * h:HDIM * h + 1] for h in range(2)]

        def residues(it, carry):
            pairs = [(b, dr) for b in range(nsub) for dr in range(per_trip)]
            units = [(pr, h) for pr in pairs for h in range(2)]
            rows = {(b, dr): _dil_rows(b, it * per_trip + dr, r) for b, dr in pairs}
            blocks, prev_exists = {}, {}
            for b, dr in pairs:
                rw = rows[b, dr]
                if b == 0:
                    before = _dil_rows(nsub - 1, it * per_trip + dr, r)
                    kp, vp, prev_exists[b, dr] = kp_ref[before, :], vp_ref[before, :], i > 0
                else:
                    before = rows[b - 1, dr]
                    kp, vp, prev_exists[b, dr] = kc_ref[before, :], vc_ref[before, :], True
                blocks[b, dr] = [q_ref[rw, :], kc_ref[rw, :], kp, vc_ref[rw, :], vp]
            qh = {u: (blocks[u[0]][0][:, hsl[u[1]]] * SCALE).astype(BF16) for u in units}
            sc = {u: _dil_scores(qh[u], blocks[u[0]][1][:, hsl[u[1]]].astype(BF16), sl[u[1]], False) for u in units}
            sp = {u: _dil_scores(qh[u], blocks[u[0]][2][:, hsl[u[1]]].astype(BF16), sl[u[1]], True, prev_exists[u[0]])
                  for u in units}
            pc, pp, den, lse = {}, {}, {}, {}
            for u in units:
                s_c = jnp.where(sc[u][1], sc[u][0], NEG_BIG)
                s_p = jnp.where(sp[u][1], sp[u][0], NEG_BIG)
                m = jnp.maximum(jnp.max(s_c, axis=1, keepdims=True), jnp.max(s_p, axis=1, keepdims=True))
                pc[u] = jnp.exp(s_c - m)
                pp[u] = jnp.exp(s_p - m)
                den[u] = jnp.sum(pc[u], axis=1, keepdims=True) + jnp.sum(pp[u], axis=1, keepdims=True)
                lse[u] = jnp.broadcast_to(m + jnp.log(den[u]), (BLK, HDIM))
            o = {u: (_dot(pc[u].astype(BF16), blocks[u[0]][3][:, hsl[u[1]]].astype(BF16))
                     + _dot(pp[u].astype(BF16), blocks[u[0]][4][:, hsl[u[1]]].astype(BF16))) / den[u] for u in units}
            for pr in pairs:
                o_ref[rows[pr], :] = jnp.concatenate([o[pr, 0], o[pr, 1]], axis=1)
                lse_ref[rows[pr], :] = jnp.concatenate([lse[pr, 0], lse[pr, 1]], axis=1)
            return carry

        lax.fori_loop(0, r // per_trip, residues, 0)

    def at(col0, pick):
        return pl.BlockSpec((sbr, LANES), lambda c, i: (pick(i), col0 // LANES + c))

    cur = lambda i: i
    prv = lambda i: jnp.maximum(i - 1, 0)
    cq, ck, cv = COL_CQ + gi * C_OUT, COL_CK + gi * C_OUT, COL_CV + gi * C_OUT
    out = pl.BlockSpec((sbr, LANES), lambda c, i: (i, c))
    return pl.pallas_call(
        body, name=name, grid=(C_OUT // LANES, nsb),
        in_specs=[at(cq, cur), at(ck, cur), at(ck, prv), at(cv, cur), at(cv, prv),
                  pl.BlockSpec((1, LANES), lambda c, i: (0, c))],
        out_specs=[out, out], out_shape=[jax.ShapeDtypeStruct((t, C_OUT), F32)] * 2,
        compiler_params=_cparams(("parallel", "parallel")),
    )(p, p, p, p, p, slope_cols)


def _dil_bwd(p, do, o, lse, gi, name):
    t = p.shape[0]
    _, r = C_GROUPS[gi]
    per_trip, nsub = _dil_plan(r)
    sbr = BLK * r * nsub
    nsb = t // sbr
    slope_cols = _slope_cols(gi)

    def body(q_ref, qn_ref, kc_ref, kp_ref, vc_ref, vp_ref, do_ref, don_ref, o_ref, on_ref, l_ref, ln_ref, sl_ref,
             dq_ref, dk_ref, dv_ref):
        i = pl.program_id(1)

        hsl = [slice(HDIM * h, HDIM * (h + 1)) for h in range(2)]
        sl = [sl_ref[:, HDIM * h:HDIM * h + 1] for h in range(2)]

        def residues(it, carry):
            pairs = [(b, dr) for b in range(nsub) for dr in range(per_trip)]
            units = [(pr, h) for pr in pairs for h in range(2)]
            rows = {(b, dr): _dil_rows(b, it * per_trip + dr, r) for b, dr in pairs}
            blocks, has_prev, has_next = {}, {}, {}
            for b, dr in pairs:
                rw = rows[b, dr]
                if b == 0:
                    before = _dil_rows(nsub - 1, it * per_trip + dr, r)
                    kp, vp, has_prev[b, dr] = kp_ref[before, :], vp_ref[before, :], i > 0
                else:
                    kp, vp, has_prev[b, dr] = kc_ref[rows[b - 1, dr], :], vc_ref[rows[b - 1, dr], :], True
                if b == nsub - 1:
                    after = _dil_rows(0, it * per_trip + dr, r)
                    nxt = [ref[after, :] for ref in (qn_ref, don_ref, on_ref, ln_ref)]
                    has_next[b, dr] = i < nsb - 1
                else:
                    nxt = [ref[rows[b + 1, dr], :] for ref in (q_ref, do_ref, o_ref, l_ref)]
                    has_next[b, dr] = True
                blocks[b, dr] = [q_ref[rw, :], nxt[0], kc_ref[rw, :], kp, vc_ref[rw, :], vp, do_ref[rw, :], nxt[1],
                                 o_ref[rw, :], nxt[2], l_ref[rw, :], nxt[3]]
            part = lambda u, k: blocks[u[0]][k][:, hsl[u[1]]]
            qb = {u: part(u, 0).astype(BF16) for u in units}
            qnb = {u: part(u, 1).astype(BF16) for u in units}
            qh = {u: (part(u, 0) * SCALE).astype(BF16) for u in units}
            qnh = {u: (part(u, 1) * SCALE).astype(BF16) for u in units}
            kc = {u: part(u, 2).astype(BF16) for u in units}
            kp = {u: part(u, 3).astype(BF16) for u in units}
            vc = {u: part(u, 4).astype(BF16) for u in units}
            vp = {u: part(u, 5).astype(BF16) for u in units}
            dob = {u: part(u, 6).astype(BF16) for u in units}
            donb = {u: part(u, 7).astype(BF16) for u in units}
            delta = {u: jnp.sum(part(u, 6) * part(u, 8), axis=1, keepdims=True) for u in units}
            deltan = {u: jnp.sum(part(u, 7) * part(u, 9), axis=1, keepdims=True) for u in units}
            lse_c = {u: part(u, 10)[:, :1] for u in units}
            lse_n = {u: part(u, 11)[:, :1] for u in units}
            s_cc = {u: _dil_scores(qh[u], kc[u], sl[u[1]], False) for u in units}
            s_cp = {u: _dil_scores(qh[u], kp[u], sl[u[1]], True, has_prev[u[0]]) for u in units}
            s_nc = {u: _dil_scores(qnh[u], kc[u], sl[u[1]], True, has_next[u[0]]) for u in units}
            da_cc = {u: _dot_nt(dob[u], vc[u]) for u in units}
            da_cp = {u: _dot_nt(dob[u], vp[u]) for u in units}
            da_nc = {u: _dot_nt(donb[u], vc[u]) for u in units}

            def prob(s_ok, lse_col):
                s, ok = s_ok
                return jnp.where(ok, jnp.exp(jnp.where(ok, s, NEG_BIG) - lse_col), 0.0)

            p_cc = {u: prob(s_cc[u], lse_c[u]) for u in units}
            p_cp = {u: prob(s_cp[u], lse_c[u]) for u in units}
            p_nc = {u: prob(s_nc[u], lse_n[u]) for u in units}
            ds_cc = {u: (p_cc[u] * (da_cc[u] - delta[u]) * SCALE).astype(BF16) for u in units}
            ds_cp = {u: (p_cp[u] * (da_cp[u] - delta[u]) * SCALE).astype(BF16) for u in units}
            ds_nc = {u: (p_nc[u] * (da_nc[u] - deltan[u]) * SCALE).astype(BF16) for u in units}
            dq = {u: _dot(ds_cc[u], kc[u]) + _dot(ds_cp[u], kp[u]) for u in units}
            dk = {u: _dot_tn(ds_cc[u], qb[u]) + _dot_tn(ds_nc[u], qnb[u]) for u in units}
            dv = {u: _dot_tn(p_cc[u].astype(BF16), dob[u]) + _dot_tn(p_nc[u].astype(BF16), donb[u]) for u in units}
            for pr in pairs:
                dq_ref[rows[pr], :] = jnp.concatenate([dq[pr, 0], dq[pr, 1]], axis=1)
                dk_ref[rows[pr], :] = jnp.concatenate([dk[pr, 0], dk[pr, 1]], axis=1)
                dv_ref[rows[pr], :] = jnp.concatenate([dv[pr, 0], dv[pr, 1]], axis=1)
            return carry

        lax.fori_loop(0, r // per_trip, residues, 0)

    def at(col0, pick):
        return pl.BlockSpec((sbr, LANES), lambda c, i: (pick(i), col0 // LANES + c))

    cur = lambda i: i
    prv = lambda i: jnp.maximum(i - 1, 0)
    nxt = lambda i: jnp.minimum(i + 1, nsb - 1)
    cq, ck, cv = COL_CQ + gi * C_OUT, COL_CK + gi * C_OUT, COL_CV + gi * C_OUT
    return pl.pallas_call(
        body, name=name, grid=(C_OUT // LANES, nsb),
        in_specs=[at(cq, cur), at(cq, nxt), at(ck, cur), at(ck, prv), at(cv, cur), at(cv, prv),
                  at(0, cur), at(0, nxt), at(0, cur), at(0, nxt), at(0, cur), at(0, nxt),
                  pl.BlockSpec((1, LANES), lambda c, i: (0, c))],
        out_specs=[at(0, cur)] * 3, out_shape=[jax.ShapeDtypeStruct((t, C_OUT), F32)] * 3,
        compiler_params=_cparams(("parallel", "parallel")),
    )(p, p, p, p, p, p, do, do, o, o, lse, lse, slope_cols)


def _dil_merge(os_, ls_, name):
    t, w = os_[0].shape
    tr = _rows(t, ELEMENTWISE_ROWS)

    def body(o0, o1, o2, l0, l1, l2, y_ref, lse_ref):
        a, b, c = l0[...], l1[...], l2[...]
        m = jnp.maximum(jnp.maximum(a, b), c)
        ea, eb, ec = jnp.exp(a - m), jnp.exp(b - m), jnp.exp(c - m)
        den = ea + eb + ec
        y_ref[...] = (ea * o0[...] + eb * o1[...] + ec * o2[...]) / den
        lse_ref[...] = m + jnp.log(den)

    row = pl.BlockSpec((tr, w), lambda i: (i, 0))
    return pl.pallas_call(
        body, name=name, grid=(t // tr,), in_specs=[row] * 6, out_specs=[row, row],
        out_shape=[jax.ShapeDtypeStruct((t, w), F32)] * 2, compiler_params=_cparams(("parallel",)),
    )(*os_, *ls_)


def _gate_fwd(ys, gl, ws, name):
    t = gl.shape[0]
    d = gl.shape[1] // N_BRANCH
    tr = _rows(t)

    def body(ya, yb, yc, gl_ref, wa, wb, wc, m_ref):
        acc = None
        for i, (y, w) in enumerate(((ya, wa), (yb, wb), (yc, wc))):
            z = _dot(y[...].astype(BF16), w[...])
            term = jax.nn.sigmoid(gl_ref[:, i * d:(i + 1) * d]) * z
            acc = term if acc is None else acc + term
        m_ref[...] = acc.astype(m_ref.dtype)

    rows = [pl.BlockSpec((tr, y.shape[1]), lambda i: (i, 0)) for y in ys]
    wsp = [pl.BlockSpec(w.shape, lambda i: (0, 0)) for w in ws]
    return pl.pallas_call(
        body, name=name, grid=(t // tr,),
        in_specs=rows + [pl.BlockSpec((tr, N_BRANCH * d), lambda i: (i, 0))] + wsp,
        out_specs=pl.BlockSpec((tr, d), lambda i: (i, 0)), out_shape=jax.ShapeDtypeStruct((t, d), BF16),
        compiler_params=_cparams(("parallel",)),
    )(*ys, gl, *ws)


def _gate_bwd(dm, ys, gl, ws, name):
    t = gl.shape[0]
    d = gl.shape[1] // N_BRANCH
    tr = _rows(t)

    def body(dm_ref, ya, yb, yc, gl_ref, wa, wb, wc, dya, dyb, dyc, dgl_ref, dwa, dwb, dwc):
        step = pl.program_id(0)
        dmv = dm_ref[...].astype(F32)
        for i, (y, w, dy, dw) in enumerate(((ya, wa, dya, dwa), (yb, wb, dyb, dwb), (yc, wc, dyc, dwc))):
            yb16 = y[...].astype(BF16)
            z = _dot(yb16, w[...])
            sg = jax.nn.sigmoid(gl_ref[:, i * d:(i + 1) * d])
            dgl_ref[:, i * d:(i + 1) * d] = (dmv * z * sg * (1.0 - sg)).astype(dgl_ref.dtype)
            e = (dmv * sg).astype(BF16)
            dy[...] = _dot_nt(e, w[...])
            contrib = _dot_tn(yb16, e)

            @pl.when(step == 0)
            def _(dw=dw, contrib=contrib):
                dw[...] = contrib

            @pl.when(step > 0)
            def _(dw=dw, contrib=contrib):
                dw[...] += contrib

    rows = [pl.BlockSpec((tr, y.shape[1]), lambda i: (i, 0)) for y in ys]
    wsp = [pl.BlockSpec(w.shape, lambda i: (0, 0)) for w in ws]
    gsp = pl.BlockSpec((tr, N_BRANCH * d), lambda i: (i, 0))
    return pl.pallas_call(
        body, name=name, grid=(t // tr,),
        in_specs=[pl.BlockSpec((tr, d), lambda i: (i, 0))] + rows + [gsp] + wsp,
        out_specs=rows + [gsp] + wsp,
        out_shape=[jax.ShapeDtypeStruct(y.shape, F32) for y in ys] + [jax.ShapeDtypeStruct(gl.shape, BF16)]
        + [jax.ShapeDtypeStruct(w.shape, F32) for w in ws],
        compiler_params=_cparams(("arbitrary",)),
    )(dm, *ys, gl, *ws)


def _adamw(w, m, v, gparts, name):
    depth = len(gparts)
    n, r, c = gparts[0].shape
    br = max(b for b in range(8, min(r, LANES) + 1, 8) if r % b == 0) if r % 8 == 0 else r
    nb = r // br
    c1 = 1.0 - ADAM_B1 ** ADAM_STEP
    c2 = 1.0 - ADAM_B2 ** ADAM_STEP

    def body(w_ref, m_ref, v_ref, *rest):
        g_refs, (go_ref, d_ref, mo_ref, vo_ref) = rest[:depth], rest[depth:]
        li = pl.program_id(0)

        def update(g_ref):
            g = g_ref[0].astype(F32)
            for i in range(1, n):
                g = g + g_ref[i].astype(F32)
            mn = ADAM_B1 * m_ref[...] + (1.0 - ADAM_B1) * g
            vn = ADAM_B2 * v_ref[...] + (1.0 - ADAM_B2) * (g * g)
            go_ref[...] = g
            mo_ref[...] = mn
            vo_ref[...] = vn
            d_ref[...] = -ADAM_LR * ((mn / c1) / (jnp.sqrt(vn / c2) + ADAM_EPS) + ADAM_WD * w_ref[...])

        for l in range(depth):
            pl.when(li == l)(functools.partial(update, g_refs[l]))

    def g_spec(l):
        return pl.BlockSpec((n, br, c), lambda li, i: (0, jnp.where(li == l, i, jnp.where(li < l, 0, nb - 1)), 0))

    blk = pl.BlockSpec((br, c), lambda li, i: (li * nb + i, 0))
    return pl.pallas_call(
        body, name=name, grid=(depth, nb),
        in_specs=[blk, blk, blk] + [g_spec(l) for l in range(depth)],
        out_specs=[blk] * 4, out_shape=[jax.ShapeDtypeStruct((depth * r, c), F32)] * 4,
        compiler_params=_cparams(("arbitrary", "arbitrary")),
    )(w, m, v, *gparts)


def _my_coords():
    return lax.axis_index("x"), lax.axis_index("y"), lax.axis_index("c")


COMM_SEMS = [pltpu.SemaphoreType.DMA((7,)), pltpu.SemaphoreType.DMA((7,)), pltpu.SemaphoreType.DMA]


class _Gather:
    def __init__(self, x_ref, out_ref, send_sems, recv_sems, local_sem):
        self.x_ref, self.out_ref = x_ref, out_ref
        self.send_sems, self.recv_sems, self.local_sem = send_sems, recv_sems, local_sem
        self.m_per = x_ref.shape[0]
        x, y, c = _my_coords()
        self.c = c
        self.me, self.sibling = (x, y, c), (x, y, 1 - c)
        self.chips = [(1 - x, y), (x, 1 - y), (1 - x, 1 - y)]

    def rows(self, px, py, pc):
        return self.out_ref.at[pl.ds((4 * px + 2 * py + pc) * self.m_per, self.m_per), :]

    def copy(self, k, block, to, src=None):
        return pltpu.make_async_remote_copy(
            src_ref=self.rows(*block) if src is None else src, dst_ref=self.rows(*block),
            send_sem=self.send_sems.at[k], recv_sem=self.recv_sems.at[k], device_id=to, device_id_type=MESH)

    def mine(self):
        return pltpu.make_async_copy(self.x_ref, self.rows(*self.me), self.local_sem)

    def first(self):
        out = [self.copy(0, self.me, self.sibling, src=self.x_ref)]
        return out + [self.copy(1 + j, self.me, (*chip, self.c), src=self.x_ref) for j, chip in enumerate(self.chips)]

    def passed(self):
        return [self.copy(4 + j, (*chip, self.c), self.sibling) for j, chip in enumerate(self.chips)]

    def start(self):
        self.mine().start()
        for cp in self.first():
            cp.start()

    def forward(self):
        passed = self.passed()
        for j, chip in enumerate(self.chips):
            self.copy(1 + j, (*chip, self.c), self.me).wait_recv()
            passed[j].start()

    def finish(self):
        self.copy(0, self.sibling, self.me).wait_recv()
        for j, chip in enumerate(self.chips):
            self.copy(4 + j, (*chip, 1 - self.c), self.me).wait_recv()
        for cp in self.first() + self.passed():
            cp.wait_send()
        self.mine().wait()


class _Exchange:
    def __init__(self, send_ref, recv_ref, send_sems, recv_sems, local_sem):
        self.send_ref, self.recv_ref = send_ref, recv_ref
        self.send_sems, self.recv_sems, self.local_sem = send_sems, recv_sems, local_sem
        x, y, c = _my_coords()
        self.me = 4 * x + 2 * y + c
        self.peers = []
        for k in range(1, N_DEV):
            px = 1 - x if k & 4 else x
            py = 1 - y if k & 2 else y
            pc = 1 - c if k & 1 else c
            self.peers.append((4 * px + 2 * py + pc, (px, py, pc)))

    def mine(self):
        return pltpu.make_async_copy(self.send_ref.at[self.me], self.recv_ref.at[self.me], self.local_sem)

    def copy(self, k, src_slot, dst_slot):
        return pltpu.make_async_remote_copy(
            src_ref=self.send_ref.at[src_slot], dst_ref=self.recv_ref.at[dst_slot],
            send_sem=self.send_sems.at[k], recv_sem=self.recv_sems.at[k],
            device_id=self.peers[k][1], device_id_type=MESH)

    def start(self):
        self.mine().start()
        for k, (peer, _) in enumerate(self.peers):
            self.copy(k, peer, self.me).start()

    def finish(self):
        for k, (peer, _) in enumerate(self.peers):
            self.copy(k, peer, self.me).wait_send()
            self.copy(k, self.me, peer).wait_recv()
        self.mine().wait()


def _all_gather(x_shard, in_vmem, with_sum, name):
    m_per, n = x_shard.shape

    def body(x_ref, out_ref, *rest):
        if with_sum:
            sum_ref, send_sems, recv_sems, local_sem = rest
        else:
            send_sems, recv_sems, local_sem = rest
        g = _Gather(x_ref, out_ref, send_sems, recv_sems, local_sem)
        g.start()
        g.forward()
        g.finish()
        if with_sum:
            acc = out_ref[pl.ds(0, m_per), :]
            for d in range(1, N_DEV):
                acc = acc + out_ref[pl.ds(d * m_per, m_per), :]
            sum_ref[...] = acc

    space = pltpu.VMEM if in_vmem else pl.ANY
    out_shape = [jax.ShapeDtypeStruct((N_DEV * m_per, n), x_shard.dtype)]
    out_specs = [pl.BlockSpec(memory_space=space)]
    if with_sum:
        out_shape.append(jax.ShapeDtypeStruct((m_per, n), x_shard.dtype))
        out_specs.append(pl.BlockSpec(memory_space=pltpu.VMEM))
    res = pl.pallas_call(
        body, name=name, out_shape=out_shape, in_specs=[pl.BlockSpec(memory_space=space)], out_specs=out_specs,
        scratch_shapes=COMM_SEMS, compiler_params=pltpu.CompilerParams(vmem_limit_bytes=VMEM_LIMIT),
    )(x_shard)
    return res if with_sum else res[0]


def _comm_sems(n):
    return [pltpu.SemaphoreType.DMA((n, 7)), pltpu.SemaphoreType.DMA((n, 7)), pltpu.SemaphoreType.DMA((n,))]


class _Many:
    def __init__(self, kind, ins, outs, send_sems, recv_sems, local_sems):
        self.parts = [kind(i, o, send_sems.at[b], recv_sems.at[b], local_sems.at[b])
                      for b, (i, o) in enumerate(zip(ins, outs))]

    def start(self):
        for part in self.parts:
            part.start()

    def forward(self):
        for part in self.parts:
            part.forward()

    def finish(self):
        for part in self.parts:
            part.finish()


def _gathered_shapes(shards):
    return [jax.ShapeDtypeStruct((N_DEV * s.shape[0],) + s.shape[1:], s.dtype) for s in shards]


def _all_gather_many(shards, name):
    n = len(shards)

    def body(*refs):
        g = _Many(_Gather, refs[:n], refs[n:2 * n], *refs[2 * n:])
        g.start()
        g.forward()
        g.finish()

    hbm = pl.BlockSpec(memory_space=pl.ANY)
    return pl.pallas_call(body, name=name, out_shape=_gathered_shapes(shards), in_specs=[hbm] * n,
                          out_specs=[hbm] * n, scratch_shapes=_comm_sems(n))(*shards)


def _all_to_all_many(sends, name):
    n = len(sends)

    def body(*refs):
        ex = _Many(_Exchange, refs[:n], refs[n:2 * n], *refs[2 * n:])
        ex.start()
        ex.finish()

    hbm = pl.BlockSpec(memory_space=pl.ANY)
    return pl.pallas_call(body, name=name, out_shape=[jax.ShapeDtypeStruct(s.shape, s.dtype) for s in sends],
                          in_specs=[hbm] * n, out_specs=[hbm] * n, scratch_shapes=_comm_sems(n))(*sends)


def _row(v):
    return v.reshape(1, -1)


def _ffn_fwd(x, w_in, w_out, g_pre, g_post, m, res_w, tag):
    shift, scale, gate = m[0], m[1], m[2]
    mpre = _row(g_pre * (1.0 + scale))
    mpost = _row(res_w * gate * g_post)
    h = _rms_fwd(x, mpre, _row(shift), None, BF16, tag + "_pre")
    ua, ub, s = _ffn_in_swiglu(h, w_in, tag + "_in")
    y = _matmul(s, w_out, name=tag + "_out")
    x_new = _rms_fwd(y, mpost, jnp.zeros_like(mpost), x, F32, tag + "_post")
    return x_new, (x, h, ua, ub, s, y, mpre, mpost)


def _sub_bwd_post(dx_new, y, mpost, g_post, gate, res_w, tag):
    dy, c1, _ = _rms_bwd(dx_new, y, mpost, None, BF16, tag + "_post_bwd")
    c1 = c1[0]
    return dy, c1 * res_w * g_post, c1 * res_w * gate


def _sub_bwd_pre(dh, x, mpre, dx_new, g_pre, scale, tag):
    dx, c2, c3 = _rms_bwd(dh, x, mpre, dx_new, F32, tag + "_pre_bwd")
    c2, c3 = c2[0], c3[0]
    return dx, c3, c2 * g_pre, c2 * (1.0 + scale)


def _ffn_bwd(dx_new, saved, w_in, w_out, g_pre, g_post, m, res_w, tag):
    x, h, ua, ub, s, y, mpre, mpost = saved
    scale, gate = m[1], m[2]
    dy, dgate, dg_post = _sub_bwd_post(dx_new, y, mpost, g_post, gate, res_w, tag)
    ds = _matmul(dy, w_out, tb=True, out_dtype=BF16, name=tag + "_out_dx")
    dw_out = _matmul(s, dy, ta=True, out_dtype=BF16, name=tag + "_out_dw")
    du = _swiglu_bwd(ua, ub, ds, tag + "_act_bwd")
    dh = _matmul(du, w_in, tb=True, name=tag + "_in_dx")
    dw_in = _matmul(h, du, ta=True, out_dtype=BF16, name=tag + "_in_dw")
    dx, dshift, dscale, dg_pre = _sub_bwd_pre(dh, x, mpre, dx_new, g_pre, scale, tag)
    return dx, dw_in, dw_out, jnp.stack([dshift, dscale, dgate]), dg_pre, dg_post


def _slope_cols(gi):
    _, r = C_GROUPS[gi]
    sl = jnp.asarray(_alibi_slopes(C_HEADS)[gi * C_HPG:(gi + 1) * C_HPG], F32) * float(r)
    return jnp.repeat(sl, HDIM).reshape(1, C_OUT)


def _mix_fwd(x, w, g_pre, g_post, m, lb, hn, tag, gather=None):
    t, d = x.shape
    shift, scale, gate = m[0], m[1], m[2]
    mpre = _row(g_pre * (1.0 + scale))
    mpost = _row(gate * g_post)
    h = _rms_fwd(x, mpre, _row(shift), None, BF16, tag + "_pre")
    p = _matmul(h, w["w_in"], name=tag + "_in")
    hn2 = _row(jnp.tile(hn, 2))
    ya, oa, states = _hgrn_fwd(p, _row(lb), hn2, tag + "_hgrn")
    kv = p[:, COL_BK:COL_CQ].astype(BF16)
    if gather is None:
        (yb, sb_tot), gathered = _sb_fwd(p, kv, tag + "_sb"), None
    else:
        res = _sb_fwd(p, kv, tag + "_sb_gather", gather)
        yb, sb_tot, gathered = res[0], res[1], list(res[2:])
    og, lg = zip(*[_dil_fwd(p, gi, tag + "_dil%d" % gi) for gi in range(len(C_GROUPS))])
    yc, lse_c = _dil_merge(og, lg, tag + "_dil_merge")
    gl = p[:, COL_GATE:]
    ws = (w["w_branch_a"], w["w_branch_b"], w["w_branch_c"])
    merged = _gate_fwd((ya, yb, yc), gl, ws, tag + "_gate")
    y = _matmul(merged, w["w_out"], name=tag + "_out")
    x_new = _rms_fwd(y, mpost, jnp.zeros_like(mpost), x, F32, tag + "_post")
    return x_new, (x, h, p, hn2, ya, oa, states, yb, kv, sb_tot, yc, lse_c, gl, merged, y, mpre, mpost), gathered


def _mix_bwd(dx_new, saved, w, g_pre, g_post, m, lb, tag, exchange=None):
    x, h, p, hn2, ya, oa, states, yb, kv, sb_tot, yc, lse_c, gl, merged, y, mpre, mpost = saved
    t = x.shape[0]
    scale, gate = m[1], m[2]
    dy, dgate, dg_post = _sub_bwd_post(dx_new, y, mpost, g_post, gate, 1.0, tag)
    dmerged = _matmul(dy, w["w_out"], tb=True, out_dtype=BF16, name=tag + "_out_dx")
    dw_out = _matmul(merged, dy, ta=True, out_dtype=BF16, name=tag + "_out_dw")
    ws = (w["w_branch_a"], w["w_branch_b"], w["w_branch_c"])
    dya, dyb, dyc, dgl, dwa, dwb, dwc = _gate_bwd(dmerged, (ya, yb, yc), gl, ws, tag + "_gate_bwd")
    dqa, dfa, dia, dga, dlb, dhn = _hgrn_bwd(p, _row(lb), hn2, oa, states, dya, tag + "_hgrn_bwd")
    if exchange is None:
        (dbq, dbk, dbv), received = _sb_bwd(p, kv, sb_tot, dyb, tag + "_sb_bwd"), None
    else:
        res = _sb_bwd(p, kv, sb_tot, dyb, tag + "_sb_bwd_exchange", exchange)
        dbq, dbk, dbv, received = res[0], res[1], res[2], list(res[3:])
    dcq, dck, dcv = zip(*[_dil_bwd(p, dyc, yc, lse_c, gi, tag + "_dil%d_bwd" % gi) for gi in range(len(C_GROUPS))])
    dil = [g.astype(BF16) for g in (*dcq, *dck, *dcv)]
    dp = jnp.concatenate([dqa, dfa, dia, dga, dbq, dbk, dbv, *dil, dgl], axis=1)
    dh = _matmul(dp, w["w_in"], tb=True, name=tag + "_in_dx")
    dw_in = _matmul(h, dp, ta=True, out_dtype=BF16, name=tag + "_in_dw")
    dx, dshift, dscale, dg_pre = _sub_bwd_pre(dh, x, mpre, dx_new, g_pre, scale, tag)
    dhn_v = jnp.sum(dhn, axis=(0, 1))
    dhn_v = dhn_v[:A_VDIM] + dhn_v[A_VDIM:]
    dws = dict(w_in=dw_in, w_out=dw_out, w_branch_a=dwa.astype(BF16), w_branch_b=dwb.astype(BF16),
               w_branch_c=dwc.astype(BF16))
    return dx, dws, jnp.stack([dshift, dscale, dgate]), dg_pre, dg_post, dlb[0], dhn_v, received


class _LocalWeights:
    def __init__(self, wts):
        self.wts = wts

    def first(self):
        return None

    def shard(self, l):
        return None

    def layer(self, l, gathered):
        return {k: v[l] for k, v in self.wts.items()}

    fused = False

    def pack(self, names, dws):
        return [dws[k] for k in names]

    def last(self, packed):
        return packed


class _ShardedWeights:
    def __init__(self, shards):
        self.shards = shards

    def shard(self, l):
        return [self.shards[k][l].astype(BF16) for k in BIG_WEIGHTS]

    def first(self):
        return _all_gather_many(self.shard(0), "weights_all_gather")

    def layer(self, l, gathered):
        out = {}
        for k, got in zip(BIG_WEIGHTS, gathered):
            _, r, c = self.shards[k].shape
            out[k] = got if k in ROW_SHARDED else got.reshape(N_DEV, r, c).transpose(1, 0, 2).reshape(r, N_DEV * c)
        return out

    fused = True

    def pack(self, names, dws):
        out = []
        for k in names:
            _, r, c = self.shards[k].shape
            g = dws[k]
            out.append(g.reshape(N_DEV, r, c) if k in ROW_SHARDED else g.reshape(r, N_DEV, c).transpose(1, 0, 2))
        return out

    def last(self, packed):
        return _all_to_all_many(packed, "grads_all_to_all")


def _local_step(x, target, mod, norm_g, lb_all, hnorm, supply):
    depth = mod.shape[0]
    d = x.shape[1]
    saved, wls = [], []
    gathered = supply.first()
    for l in range(depth):
        wl = supply.layer(l, gathered)
        wls.append(wl)
        x, s0 = _ffn_fwd(x, wl["ffn1_w_in"], wl["ffn1_w_out"], norm_g[l, 0], norm_g[l, 1], mod[l, 0], 0.5, "ffn1")
        nxt = supply.shard(l + 1) if l + 1 < depth else None
        x, s1, gathered = _mix_fwd(x, wl, norm_g[l, 2], norm_g[l, 3], mod[l, 1], lb_all[l], hnorm[l], "mix", nxt)
        x, s2 = _ffn_fwd(x, wl["ffn2_w_in"], wl["ffn2_w_out"], norm_g[l, 4], norm_g[l, 5], mod[l, 2], 0.5, "ffn2")
        saved.append((s0, s1, s2))
    dx, sq = _loss_head(x, target, "loss_head")
    loss = 0.5 * jnp.sum(sq) / d
    dmod, dng, dlb, dhn = [], [], [], []
    early = ("ffn2_w_in", "ffn2_w_out")
    late = tuple(k for k in BIG_WEIGHTS if k not in early)
    returned = {}
    waiting = []
    for l in reversed(range(depth)):
        wl = wls[l]
        s0, s1, s2 = saved[l]
        dx, dwi2, dwo2, dm2, dgp2, dgq2 = _ffn_bwd(dx, s2, wl["ffn2_w_in"], wl["ffn2_w_out"], norm_g[l, 4],
                                                   norm_g[l, 5], mod[l, 2], 0.5, "ffn2")
        waiting += zip([(l, k) for k in early], supply.pack(early, dict(ffn2_w_in=dwi2, ffn2_w_out=dwo2)))
        keys, bufs = [k for k, _ in waiting], [b for _, b in waiting]
        dx, dwm, dm1, dgp1, dgq1, dlb_l, dhn_l, received = _mix_bwd(
            dx, s1, wl, norm_g[l, 2], norm_g[l, 3], mod[l, 1], lb_all[l], "mix", bufs if supply.fused else None)
        returned.update(zip(keys, received if supply.fused else bufs))
        dx, dwi1, dwo1, dm0, dgp0, dgq0 = _ffn_bwd(dx, s0, wl["ffn1_w_in"], wl["ffn1_w_out"], norm_g[l, 0],
                                                   norm_g[l, 1], mod[l, 0], 0.5, "ffn1")
        dmod.append(jnp.stack([dm0, dm1, dm2]))
        dng.append(jnp.stack([dgp0, dgq0, dgp1, dgq1, dgp2, dgq2]))
        dlb.append(dlb_l)
        dhn.append(dhn_l)
        waiting = list(zip([(l, k) for k in late], supply.pack(late, dict(dwm, ffn1_w_in=dwi1, ffn1_w_out=dwo1))))
    returned.update(zip([k for k, _ in waiting], supply.last([b for _, b in waiting])))
    rev = lambda lst: jnp.stack(lst[::-1])
    return loss, dx, rev(dmod), rev(dng), rev(dlb), rev(dhn), returned


def _lb_all(logits):
    lb_p = jax.nn.softmax(logits.astype(F32), axis=0)
    return jnp.cumsum(lb_p, axis=0) - lb_p[0:1]


def _pad_rows(a, rows):
    return jnp.pad(a, ((0, rows - a.shape[0]), (0, 0)))


def kernel(x, c, w_ada, b_ada, norm_g, ffn1_w_in, ffn1_w_out, w_in, hgrn_lb_logits, hgrn_norm_g, w_branch_a, w_branch_b, w_branch_c, w_out, ffn2_w_in, ffn2_w_out, loss_target, m_w_ada, m_b_ada, m_norm_g, m_ffn1_w_in, m_ffn1_w_out, m_w_in, m_hgrn_lb_logits, m_hgrn_norm_g, m_w_branch_a, m_w_branch_b, m_w_branch_c, m_w_out, m_ffn2_w_in, m_ffn2_w_out, v_w_ada, v_b_ada, v_norm_g, v_ffn1_w_in, v_ffn1_w_out, v_w_in, v_hgrn_lb_logits, v_hgrn_norm_g, v_w_branch_a, v_w_branch_b, v_w_branch_c, v_w_out, v_ffn2_w_in, v_ffn2_w_out):
    weights = dict(w_ada=w_ada, b_ada=b_ada, norm_g=norm_g, ffn1_w_in=ffn1_w_in, ffn1_w_out=ffn1_w_out, w_in=w_in,
                   hgrn_lb_logits=hgrn_lb_logits, hgrn_norm_g=hgrn_norm_g, w_branch_a=w_branch_a,
                   w_branch_b=w_branch_b, w_branch_c=w_branch_c, w_out=w_out, ffn2_w_in=ffn2_w_in,
                   ffn2_w_out=ffn2_w_out)
    mom1 = dict(w_ada=m_w_ada, b_ada=m_b_ada, norm_g=m_norm_g, ffn1_w_in=m_ffn1_w_in, ffn1_w_out=m_ffn1_w_out,
                w_in=m_w_in, hgrn_lb_logits=m_hgrn_lb_logits, hgrn_norm_g=m_hgrn_norm_g, w_branch_a=m_w_branch_a,
                w_branch_b=m_w_branch_b, w_branch_c=m_w_branch_c, w_out=m_w_out, ffn2_w_in=m_ffn2_w_in,
                ffn2_w_out=m_ffn2_w_out)
    mom2 = dict(w_ada=v_w_ada, b_ada=v_b_ada, norm_g=v_norm_g, ffn1_w_in=v_ffn1_w_in, ffn1_w_out=v_ffn1_w_out,
                w_in=v_w_in, hgrn_lb_logits=v_hgrn_lb_logits, hgrn_norm_g=v_hgrn_norm_g, w_branch_a=v_w_branch_a,
                w_branch_b=v_w_branch_b, w_branch_c=v_w_branch_c, w_out=v_w_out, ffn2_w_in=v_ffn2_w_in,
                ffn2_w_out=v_ffn2_w_out)
    order = list(weights)
    depth, d, ada_cols = w_ada.shape
    nd = d // LANES
    xi, yi, ci = _my_coords()
    me = 4 * xi + 2 * yi + ci

    small = jnp.concatenate([c.reshape(nd, LANES), norm_g.reshape(depth * 6, LANES)], axis=0)
    g1 = _all_gather(small, True, False, "small_all_gather").reshape(N_DEV, small.shape[0], LANES)
    c_act = _silu(g1[:, :nd].reshape(N_DEV, d))
    norm_full = g1[:, nd:].reshape(N_DEV, depth, 6, LANES).transpose(1, 2, 0, 3).reshape(depth, 6, d)

    c_pad = _pad_rows(c_act, 16)
    mod_sh = jnp.stack([_matmul(c_pad, w_ada[l], name="ada_mod")[:N_DEV]
                        + lax.dynamic_slice_in_dim(b_ada[l], me * ada_cols, ada_cols)[None]
                        for l in range(depth)])
    g2 = _all_gather(mod_sh.reshape(-1, LANES), True, False, "mod_all_gather")
    g2 = g2.reshape(N_DEV, depth, N_DEV, ada_cols)
    mod = lax.dynamic_index_in_dim(g2, me, axis=2, keepdims=False)
    mod = mod.transpose(1, 0, 2).reshape(depth, 3, 3, d)

    supply = _ShardedWeights({k: weights[k] for k in BIG_WEIGHTS})
    lb_all, lb_vjp = jax.vjp(_lb_all, hgrn_lb_logits)

    loss, dx, dmod, dng, dlb, dhn, received = _local_step(x[0], loss_target[0], mod, norm_full, lb_all,
                                                          hgrn_norm_g, supply)
    loss = lax.psum(loss, ("x", "y", "c"))

    dhn_pad = jnp.pad(dhn.reshape(-1), (0, 8 * LANES - dhn.size))
    pieces = [dmod.reshape(-1), dng.reshape(-1), dlb.reshape(-1), dhn_pad]
    sizes = [p_.size for p_ in pieces]
    smallg = jnp.concatenate(pieces).reshape(-1, LANES)
    g3, gsum = _all_gather(smallg, True, True, "small_grads_all_gather")
    g3 = g3.reshape(N_DEV, -1)
    gsum = gsum.reshape(-1)
    dmod_all = g3[:, :sizes[0]].reshape(N_DEV, depth, 9 * d)
    o1 = sizes[0]
    grads = {}
    grads["b_ada"] = gsum[:o1].reshape(depth, 9 * d)
    dng_sum = gsum[o1:o1 + sizes[1]].reshape(depth, 6, nd, LANES)
    grads["norm_g"] = lax.dynamic_index_in_dim(dng_sum, me, axis=2, keepdims=False)
    o2 = o1 + sizes[1]
    dlb_sum = gsum[o2:o2 + sizes[2]].reshape(depth, A_QK)
    grads["hgrn_lb_logits"] = lb_vjp(dlb_sum)[0]
    o3 = o2 + sizes[2]
    grads["hgrn_norm_g"] = gsum[o3:o3 + dhn.size].reshape(depth, A_VDIM)
    dmod_mine = lax.dynamic_slice_in_dim(dmod_all, me * ada_cols, ada_cols, axis=2)
    grads["w_ada"] = jnp.stack([_matmul(c_pad, _pad_rows(dmod_mine[:, l], 16), ta=True, name="ada_dw")
                                for l in range(depth)])

    outs = {}
    for k in order:
        w = weights[k]
        w2 = w.reshape(-1, w.shape[-1])
        if k in BIG_WEIGHTS:
            gp = [received[l, k] for l in range(depth)]
        else:
            gp = [grads[k].reshape((1,) + w2.shape)]
        res = _adamw(w2, mom1[k].reshape(w2.shape), mom2[k].reshape(w2.shape), gp, "adamw")
        outs[k] = [r.reshape(w.shape) for r in res]
    return (loss, dx[None], *[outs[k][0] for k in order], *[outs[k][1] for k in order],
            *[outs[k][2] for k in order], *[outs[k][3] for k in order])
```
